```python
import jax, jax.numpy as jnp
from jax import lax
import numpy as np

D_MODEL = 1024
BATCH = 8
SEQ = 8192
DEPTH = 2

HEAD_DIM = 64
D_MIX = D_MODEL
N_HEADS_FOX = D_MIX // (2 * HEAD_DIM)
N_HEADS_DIL = D_MIX // (2 * HEAD_DIM)
D_FOX = N_HEADS_FOX * HEAD_DIM
D_DIL = N_HEADS_DIL * HEAD_DIM
N_IN = 4 * D_FOX + N_HEADS_FOX + 4 * D_DIL
PLE_DIM = 256
ROPE_THETA = 500000.0
ROPE_DIM = HEAD_DIM // 4
DILATED_PATTERNS = ((128, 1), (512, 4), (2048, 16))
BLOCK = 128
EPS = 1e-6
NEG = -1e30
FORGET_BIAS_INIT = 3.0

kernel_name = "fox_dilated_hybrid_heads"


def rms_norm(x, g):
    xf = x.astype(jnp.float32)
    y = xf * lax.rsqrt(jnp.mean(xf * xf, axis=-1, keepdims=True) + EPS)
    return (y * g.astype(jnp.float32)).astype(x.dtype)


def partial_rope(x, positions):
    half = ROPE_DIM // 2
    inv_freq = ROPE_THETA ** (-jnp.arange(half, dtype=jnp.float32) / half)
    ang = positions.astype(jnp.float32)[..., None] * inv_freq
    cos = jnp.cos(ang)[:, :, None, :]
    sin = jnp.sin(ang)[:, :, None, :]
    xr = x[..., :ROPE_DIM].astype(jnp.float32)
    x1, x2 = xr[..., :half], xr[..., half:]
    rot = jnp.concatenate([x1 * cos - x2 * sin, x2 * cos + x1 * sin], axis=-1)
    return jnp.concatenate([rot.astype(x.dtype), x[..., ROPE_DIM:]], axis=-1)


def forgetting_attention(q, k, v, log_f):
    B, S, H, Dh = q.shape
    nb = S // BLOCK
    scale = Dh ** -0.5
    c = jnp.cumsum(log_f.astype(jnp.float32), axis=1).transpose(0, 2, 1)
    qb = q.reshape(B, nb, BLOCK, H, Dh).transpose(1, 0, 2, 3, 4)
    cb = c.reshape(B, H, nb, BLOCK).transpose(2, 0, 1, 3)
    key_pos = jnp.arange(S)

    def one_block(args):
        i, q_i, c_i = args
        s = jnp.einsum('bqhd,bkhd->bhqk', q_i, k, preferred_element_type=jnp.float32) * scale
        s = s + c_i[..., :, None] - c[:, :, None, :]
        q_pos = i * BLOCK + jnp.arange(BLOCK)
        causal = key_pos[None, :] <= q_pos[:, None]
        s = jnp.where(causal, s, NEG)
        prob = jax.nn.softmax(s, axis=-1)
        return jnp.einsum('bhqk,bkhd->bqhd', prob.astype(v.dtype), v)

    out = lax.map(one_block, (jnp.arange(nb), qb, cb))
    return out.transpose(1, 0, 2, 3, 4).reshape(B, S, H, Dh)


def dilated_branch(q, k, v, window, dilation):
    B, S, H, Dh = q.shape
    L = S // dilation
    n_back = window // dilation
    n_prev = -(-n_back // BLOCK)
    nb = -(-L // BLOCK)
    Lp = nb * BLOCK
    N = B * dilation
    KW = (n_prev + 1) * BLOCK

    def to_streams(x):
        x = x.reshape(B, L, dilation, H, Dh).transpose(0, 2, 1, 3, 4).reshape(N, L, H, Dh)
        return jnp.pad(x, ((0, 0), (0, Lp - L), (0, 0), (0, 0)))

    def key_band(x):
        xs = jnp.pad(to_streams(x), ((0, 0), (n_prev * BLOCK, 0), (0, 0), (0, 0)))
        xs = xs.reshape(N, nb + n_prev, BLOCK, H, Dh)
        return jnp.concatenate([xs[:, j:j + nb] for j in range(n_prev + 1)], axis=2)

    qs = to_streams(q).reshape(N, nb, BLOCK, H, Dh)
    kb, vb = key_band(k), key_band(v)
    s = jnp.einsum('nbqhd,nbkhd->nbhqk', qs, kb, preferred_element_type=jnp.float32) * (Dh ** -0.5)
    qq = jnp.arange(BLOCK)[:, None]
    kk = jnp.arange(KW)[None, :]
    dist = qq + n_prev * BLOCK - kk
    key_idx = jnp.arange(nb)[:, None, None] * BLOCK - n_prev * BLOCK + kk[None]
    valid = (dist >= 0) & (dist <= n_back) & (key_idx >= 0)
    s = jnp.where(valid[None, :, None], s, NEG)
    lse = jax.nn.logsumexp(s, axis=-1)
    prob = jnp.exp(s - lse[..., None])
    o = jnp.einsum('nbhqk,nbkhd->nbqhd', prob.astype(v.dtype), vb)

    def from_streams(x):
        tail = x.shape[3:]
        x = x.reshape((N, Lp) + tail)[:, :L]
        x = x.reshape((B, dilation, L) + tail).swapaxes(1, 2)
        return x.reshape((B, S) + tail)

    return from_streams(o), from_streams(lse.transpose(0, 1, 3, 2))


def dilated_attention(q, k, v):
    outs, lses = [], []
    for window, dilation in DILATED_PATTERNS:
        o, l = dilated_branch(q, k, v, window, dilation)
        outs.append(o)
        lses.append(l)
    w = jax.nn.softmax(jnp.stack(lses, axis=0), axis=0)
    o = jnp.stack(outs, axis=0).astype(jnp.float32)
    return jnp.sum(w[..., None] * o, axis=0).astype(q.dtype)


def hybrid_layer(h, p_i, positions, norm_g, w_in, b_f, qk_g, w_out, w_ple, ple_norm_g, w_ple_gate):
    B, S, _ = h.shape
    u = rms_norm(h, norm_g)
    z = u @ w_in
    cuts = np.cumsum([D_FOX, D_FOX, D_FOX, D_FOX, N_HEADS_FOX, D_DIL, D_DIL, D_DIL])
    qa, ka, va, ga, fa, qb, kb, vb, gb = jnp.split(z, cuts.tolist(), axis=-1)
    heads = lambda t, n: t.reshape(B, S, n, HEAD_DIM)

    qa = rms_norm(heads(qa, N_HEADS_FOX), qk_g[0])
    ka = rms_norm(heads(ka, N_HEADS_FOX), qk_g[1])
    log_f = jax.nn.log_sigmoid(fa.astype(jnp.float32) + b_f.astype(jnp.float32))
    oa = forgetting_attention(qa, ka, heads(va, N_HEADS_FOX), log_f).reshape(B, S, D_FOX)
    oa = oa * jax.nn.silu(ga)

    qb = partial_rope(rms_norm(heads(qb, N_HEADS_DIL), qk_g[2]), positions)
    kb = partial_rope(rms_norm(heads(kb, N_HEADS_DIL), qk_g[3]), positions)
    ob = dilated_attention(qb, kb, heads(vb, N_HEADS_DIL)).reshape(B, S, D_DIL)
    ob = ob * jax.nn.silu(gb)

    h = h + jnp.concatenate([oa, ob], axis=-1) @ w_out

    gate = jax.nn.sigmoid(rms_norm(h, ple_norm_g) @ w_ple_gate)
    return h + (p_i @ w_ple) * gate


def _fwd_setup_inputs(seed: int = 0) -> dict:
    key = jax.random.key(seed)
    ks = jax.random.split(key, 12)
    f32 = jnp.float32
    x = jax.random.normal(ks[0], (BATCH, SEQ, D_MODEL), f32)
    p = jax.random.normal(ks[1], (DEPTH, BATCH, SEQ, PLE_DIM), f32)
    positions = jnp.broadcast_to(jnp.arange(SEQ, dtype=jnp.int32), (BATCH, SEQ))
    norm_g = 1.0 + 0.05 * jax.random.normal(ks[2], (DEPTH, D_MODEL), f32)
    w_in = jax.random.normal(ks[3], (DEPTH, D_MODEL, N_IN), f32) * D_MODEL ** -0.5
    b_f = FORGET_BIAS_INIT + 0.1 * jax.random.normal(ks[4], (DEPTH, N_HEADS_FOX), f32)
    qk_norm_g = 1.0 + 0.05 * jax.random.normal(ks[5], (DEPTH, 4, HEAD_DIM), f32)
    w_out = jax.random.normal(ks[6], (DEPTH, D_MIX, D_MODEL), f32) * D_MIX ** -0.5
    w_ple = jax.random.normal(ks[7], (DEPTH, PLE_DIM, D_MODEL), f32) * PLE_DIM ** -0.5
    ple_norm_g = 1.0 + 0.05 * jax.random.normal(ks[8], (DEPTH, D_MODEL), f32)
    w_ple_gate = jax.random.normal(ks[9], (DEPTH, D_MODEL, D_MODEL), f32) * D_MODEL ** -0.5
    return {"x": x, "p": p, "positions": positions, "norm_g": norm_g, "w_in": w_in,
            "b_f": b_f, "qk_norm_g": qk_norm_g, "w_out": w_out, "w_ple": w_ple,
            "ple_norm_g": ple_norm_g, "w_ple_gate": w_ple_gate}


def _fwd_reference(x, p, positions, norm_g, w_in, b_f, qk_norm_g, w_out, w_ple, ple_norm_g, w_ple_gate):
    h = x
    for i in range(DEPTH):
        h = hybrid_layer(h, p[i], positions, norm_g[i], w_in[i], b_f[i], qk_norm_g[i],
                         w_out[i], w_ple[i], ple_norm_g[i], w_ple_gate[i])
    return h


import jax as _jax
import jax.numpy as _jnp

TWIN_FORMAT = 'train_step'
FWD_PARAMS = ['x', 'p', 'positions', 'norm_g', 'w_in', 'b_f', 'qk_norm_g', 'w_out', 'w_ple', 'ple_norm_g', 'w_ple_gate']
TWIN_WEIGHTS = ['norm_g', 'w_in', 'b_f', 'qk_norm_g', 'w_out', 'w_ple', 'ple_norm_g', 'w_ple_gate']
TWIN_DIFF_INPUT = 'x'
TWIN_INPUTS = ['x', 'p', 'positions', 'norm_g', 'w_in', 'b_f', 'qk_norm_g', 'w_out', 'w_ple', 'ple_norm_g', 'w_ple_gate', 'loss_target', 'm_norm_g', 'm_w_in', 'm_b_f', 'm_qk_norm_g', 'm_w_out', 'm_w_ple', 'm_ple_norm_g', 'm_w_ple_gate', 'v_norm_g', 'v_w_in', 'v_b_f', 'v_qk_norm_g', 'v_w_out', 'v_w_ple', 'v_ple_norm_g', 'v_w_ple_gate']
TWIN_OUTPUTS = ['loss', 'grad_x', 'grad_norm_g', 'grad_w_in', 'grad_b_f', 'grad_qk_norm_g', 'grad_w_out', 'grad_w_ple', 'grad_ple_norm_g', 'grad_w_ple_gate', 'delta_norm_g', 'delta_w_in', 'delta_b_f', 'delta_qk_norm_g', 'delta_w_out', 'delta_w_ple', 'delta_ple_norm_g', 'delta_w_ple_gate', 'new_m_norm_g', 'new_m_w_in', 'new_m_b_f', 'new_m_qk_norm_g', 'new_m_w_out', 'new_m_w_ple', 'new_m_ple_norm_g', 'new_m_w_ple_gate', 'new_v_norm_g', 'new_v_w_in', 'new_v_b_f', 'new_v_qk_norm_g', 'new_v_w_out', 'new_v_w_ple', 'new_v_ple_norm_g', 'new_v_w_ple_gate']
TWIN_LEAF_KINDS = {'loss': 'loss', 'grad_x': 'grad_x', 'grad_norm_g': 'grad_w', 'grad_w_in': 'grad_w', 'grad_b_f': 'grad_w', 'grad_qk_norm_g': 'grad_w', 'grad_w_out': 'grad_w', 'grad_w_ple': 'grad_w', 'grad_ple_norm_g': 'grad_w', 'grad_w_ple_gate': 'grad_w', 'delta_norm_g': 'delta_w', 'delta_w_in': 'delta_w', 'delta_b_f': 'delta_w', 'delta_qk_norm_g': 'delta_w', 'delta_w_out': 'delta_w', 'delta_w_ple': 'delta_w', 'delta_ple_norm_g': 'delta_w', 'delta_w_ple_gate': 'delta_w', 'new_m_norm_g': 'new_m', 'new_m_w_in': 'new_m', 'new_m_b_f': 'new_m', 'new_m_qk_norm_g': 'new_m', 'new_m_w_out': 'new_m', 'new_m_w_ple': 'new_m', 'new_m_ple_norm_g': 'new_m', 'new_m_w_ple_gate': 'new_m', 'new_v_norm_g': 'new_v', 'new_v_w_in': 'new_v', 'new_v_b_f': 'new_v', 'new_v_qk_norm_g': 'new_v', 'new_v_w_out': 'new_v', 'new_v_w_ple': 'new_v', 'new_v_ple_norm_g': 'new_v', 'new_v_w_ple_gate': 'new_v'}


def _forward(args):
    return _fwd_reference(*[args[k] for k in FWD_PARAMS])


def _output_shape():
    def fwd():
        inp = _fwd_setup_inputs(0)
        return _fwd_reference(*[inp[k] for k in FWD_PARAMS])
    out = _jax.eval_shape(fwd)
    return out.shape, out.dtype

N_MICROBATCH = 1
ADAM_LR = 0.001
ADAM_B1 = 0.9
ADAM_B2 = 0.999
ADAM_EPS = 1e-08
ADAM_WD = 0.01
ADAM_STEP = 10
PER_EXAMPLE_BATCH_AXIS = {'x': 0, 'p': 1, 'positions': 0, 'loss_target': 0}
SHARED_INPUTS = []
_WEIGHT_DTYPES = {'norm_g': _jnp.float32, 'w_in': _jnp.float32, 'b_f': _jnp.float32, 'qk_norm_g': _jnp.float32, 'w_out': _jnp.float32, 'w_ple': _jnp.float32, 'ple_norm_g': _jnp.float32, 'w_ple_gate': _jnp.float32}
MOMENT_SCALE = {'norm_g': 1.959174e+00, 'w_in': 7.664919e-02, 'b_f': 4.018725e+01, 'qk_norm_g': 5.299322e+00, 'w_out': 7.611332e-02, 'w_ple': 8.909194e-01, 'ple_norm_g': 1.931175e+00, 'w_ple_gate': 1.007561e-01}


def _to_microbatches(a, axis):
    t = _jnp.moveaxis(a, axis, 0)
    t = t.reshape((N_MICROBATCH, t.shape[0] // N_MICROBATCH) + t.shape[1:])
    return _jnp.moveaxis(t, 1, axis + 1)


def setup_inputs(seed: int = 0) -> dict:
    inp = _fwd_setup_inputs(seed)
    key = _jax.random.fold_in(_jax.random.key(seed), 7919)
    shape, _ = _output_shape()
    out = dict(inp)
    out["loss_target"] = _jax.random.normal(_jax.random.fold_in(key, 0), shape, _jnp.float32)
    for i, name in enumerate(TWIN_WEIGHTS):
        w = inp[name].astype(_jnp.float32)
        if MOMENT_SCALE is None:
            s = _jnp.sqrt(_jnp.mean(_jnp.square(w)) + 1e-30)
        else:
            s = MOMENT_SCALE[name]
        km, kv = _jax.random.split(_jax.random.fold_in(key, i + 1))
        out[name] = w
        out["m_" + name] = s * _jax.random.normal(km, w.shape, _jnp.float32)
        out["v_" + name] = (s * s) * _jax.random.uniform(kv, w.shape, _jnp.float32, 0.5, 1.5)
    if N_MICROBATCH > 1:
        for name, axis in PER_EXAMPLE_BATCH_AXIS.items():
            out[name] = _to_microbatches(out[name], axis)
    return {'x': out['x'], 'p': out['p'], 'positions': out['positions'], 'norm_g': out['norm_g'], 'w_in': out['w_in'], 'b_f': out['b_f'], 'qk_norm_g': out['qk_norm_g'], 'w_out': out['w_out'], 'w_ple': out['w_ple'], 'ple_norm_g': out['ple_norm_g'], 'w_ple_gate': out['w_ple_gate'], 'loss_target': out['loss_target'], 'm_norm_g': out['m_norm_g'], 'm_w_in': out['m_w_in'], 'm_b_f': out['m_b_f'], 'm_qk_norm_g': out['m_qk_norm_g'], 'm_w_out': out['m_w_out'], 'm_w_ple': out['m_w_ple'], 'm_ple_norm_g': out['m_ple_norm_g'], 'm_w_ple_gate': out['m_w_ple_gate'], 'v_norm_g': out['v_norm_g'], 'v_w_in': out['v_w_in'], 'v_b_f': out['v_b_f'], 'v_qk_norm_g': out['v_qk_norm_g'], 'v_w_out': out['v_w_out'], 'v_w_ple': out['v_w_ple'], 'v_ple_norm_g': out['v_ple_norm_g'], 'v_w_ple_gate': out['v_w_ple_gate']}


def _loss(weights, diff, rest, loss_target):
    with _jax.named_scope("forward"):
        args = {**rest, TWIN_DIFF_INPUT: diff, **{k: w.astype(_WEIGHT_DTYPES[k]) for k, w in weights.items()}}
        y = _forward(args)
    with _jax.named_scope("loss_head"):
        err = _jnp.square(y.astype(_jnp.float32) - loss_target)
        return 0.5 * _jnp.sum(_jnp.mean(err, axis=-1)) if err.ndim else 0.5 * err


def _adamw(w, g, m, v):
    m = ADAM_B1 * m + (1.0 - ADAM_B1) * g
    v = ADAM_B2 * v + (1.0 - ADAM_B2) * _jnp.square(g)
    m_hat = m / (1.0 - ADAM_B1 ** ADAM_STEP)
    v_hat = v / (1.0 - ADAM_B2 ** ADAM_STEP)
    delta = -ADAM_LR * (m_hat / (_jnp.sqrt(v_hat) + ADAM_EPS) + ADAM_WD * w)
    return delta, m, v


def reference(x, p, positions, norm_g, w_in, b_f, qk_norm_g, w_out, w_ple, ple_norm_g, w_ple_gate, loss_target, m_norm_g, m_w_in, m_b_f, m_qk_norm_g, m_w_out, m_w_ple, m_ple_norm_g, m_w_ple_gate, v_norm_g, v_w_in, v_b_f, v_qk_norm_g, v_w_out, v_w_ple, v_ple_norm_g, v_w_ple_gate):
    given = dict(x=x, p=p, positions=positions, norm_g=norm_g, w_in=w_in, b_f=b_f, qk_norm_g=qk_norm_g, w_out=w_out, w_ple=w_ple, ple_norm_g=ple_norm_g, w_ple_gate=w_ple_gate, loss_target=loss_target, m_norm_g=m_norm_g, m_w_in=m_w_in, m_b_f=m_b_f, m_qk_norm_g=m_qk_norm_g, m_w_out=m_w_out, m_w_ple=m_w_ple, m_ple_norm_g=m_ple_norm_g, m_w_ple_gate=m_w_ple_gate, v_norm_g=v_norm_g, v_w_in=v_w_in, v_b_f=v_b_f, v_qk_norm_g=v_qk_norm_g, v_w_out=v_w_out, v_w_ple=v_w_ple, v_ple_norm_g=v_ple_norm_g, v_w_ple_gate=v_w_ple_gate)
    weights = {n: given[n] for n in TWIN_WEIGHTS}
    shared = {n: given[n] for n in SHARED_INPUTS}
    per_example = {n: given[n] for n in ['x', 'p', 'positions']}
    grad_fn = _jax.value_and_grad(_loss, argnums=(0, 1))

    def one_microbatch(ex, loss_target):
        ex = dict(ex)
        diff = ex.pop(TWIN_DIFF_INPUT)
        return grad_fn(weights, diff, {**shared, **ex}, loss_target)

    if N_MICROBATCH == 1:
        loss, (grad_w, grad_x) = one_microbatch(per_example, given["loss_target"])
    else:
        def body(carry, xs):
            loss_sum, grad_sum = carry
            l_k, (gw_k, gx_k) = one_microbatch(xs[0], xs[1])
            with _jax.named_scope("update"):
                return (loss_sum + l_k, _jax.tree.map(_jnp.add, grad_sum, gw_k)), gx_k

        init = (_jnp.zeros((), _jnp.float32), _jax.tree.map(_jnp.zeros_like, weights))
        (loss, grad_w), grad_x = _jax.lax.scan(body, init, (per_example, given["loss_target"]))
    with _jax.named_scope("update"):
        delta_w, new_m, new_v = {}, {}, {}
        for n in TWIN_WEIGHTS:
            delta_w[n], new_m[n], new_v[n] = _adamw(weights[n], grad_w[n], given["m_" + n], given["v_" + n])
    return (loss, grad_x, *[grad_w[n] for n in TWIN_WEIGHTS], *[delta_w[n] for n in TWIN_WEIGHTS],
            *[new_m[n] for n in TWIN_WEIGHTS], *[new_v[n] for n in TWIN_WEIGHTS])
```

```python
import functools

import jax
import jax.numpy as jnp
from jax import lax
from jax.experimental import pallas as pl
from jax.experimental.pallas import tpu as pltpu

F32 = jnp.float32
BF16 = jnp.bfloat16

D_MODEL = 1024
HEAD_DIM = 64
N_HEADS = 8
HEAD_PAD = 128
D_BRANCH = N_HEADS * HEAD_DIM
N_MAIN = 8 * D_BRANCH
N_IN = N_MAIN + N_HEADS
PLE_DIM = 256
ROPE_THETA = 500000.0
ROPE_HALF = 8
EPS = 1e-6
NEG = -1e30
SCALE = HEAD_DIM ** -0.5
DILATED_PATTERNS = ((128, 1), (512, 4), (2048, 16))
N_DEV = 8

ADAM_LR = 0.001
ADAM_B1 = 0.9
ADAM_B2 = 0.999
ADAM_EPS = 1e-08
ADAM_WD = 0.01
ADAM_STEP = 10

ATT_T = 512
TOK_T = 256
CUM_T = 512
VMEM_LIMIT = 56 * 1024 * 1024

R_WIN = 2 * D_MODEL * 513 // 128
R_WOUT = 2 * 128 * D_MODEL // 128
R_WPLE = 2 * PLE_DIM * 128 // 128
R_BIG = R_WIN + 2 * R_WOUT + R_WPLE
O_WOUT = R_WIN
O_WPLE = O_WOUT + R_WOUT
O_WPG = O_WPLE + R_WPLE
O_NG = R_BIG
O_PNG = O_NG + 16
O_QK = O_PNG + 16
O_BF = O_QK + 4
O_LOSS = O_BF + 1
ADAM_T = 512
R_TOT = -(-(O_LOSS + 1) // ADAM_T) * ADAM_T


def _cp(*sem):
    return pltpu.CompilerParams(dimension_semantics=sem, vmem_limit_bytes=VMEM_LIMIT)


def _sigmoid(x):
    return 1.0 / (1.0 + jnp.exp(-x))


def _split3(x):
    hi = x.astype(BF16)
    r1 = x - hi.astype(F32)
    mid = r1.astype(BF16)
    lo = (r1 - mid.astype(F32)).astype(BF16)
    return hi, mid, lo


def _dot(a, b):
    return jnp.dot(a, b, preferred_element_type=F32)


def _dot_nt(a, b):
    return lax.dot_general(a, b, (((1,), (1,)), ((), ())), preferred_element_type=F32)


def _dot_tn(a, b):
    return lax.dot_general(a, b, (((0,), (0,)), ((), ())), preferred_element_type=F32)


def _inproj_fwd(h, g, wm, wf):
    S = h.shape[0]

    def body(h_ref, g_ref, wm_ref, wf_ref, zm_ref, zf_ref, u_ref):
        x = h_ref[...]
        r = lax.rsqrt(jnp.mean(x * x, axis=-1, keepdims=True) + EPS)
        u = (x * r * g_ref[...]).astype(BF16)
        u_ref[...] = u
        zm_ref[...] = _dot(u, wm_ref[...])
        zf_ref[...] = _dot(u, wf_ref[...])

    return pl.pallas_call(
        body, name="inproj_fwd", grid=(S // TOK_T,),
        in_specs=[pl.BlockSpec((TOK_T, D_MODEL), lambda i: (i, 0)),
                  pl.BlockSpec((1, D_MODEL), lambda i: (0, 0)),
                  pl.BlockSpec((D_MODEL, N_MAIN), lambda i: (0, 0)),
                  pl.BlockSpec((D_MODEL, 128), lambda i: (0, 0))],
        out_specs=[pl.BlockSpec((TOK_T, N_MAIN), lambda i: (i, 0)),
                   pl.BlockSpec((TOK_T, 128), lambda i: (i, 0)),
                   pl.BlockSpec((TOK_T, D_MODEL), lambda i: (i, 0))],
        out_shape=[jax.ShapeDtypeStruct((S, N_MAIN), F32), jax.ShapeDtypeStruct((S, 128), F32),
                   jax.ShapeDtypeStruct((S, D_MODEL), BF16)],
        compiler_params=_cp("parallel"),
    )(h, g, wm, wf)


def _log_sigmoid(x):
    return jnp.minimum(x, 0.0) - jnp.log(1.0 + jnp.exp(-jnp.abs(x)))


def _forget_cumsum(zf, bf, tri):
    S = zf.shape[0]

    def body(zf_ref, b_ref, tri_ref, c_ref, carry):
        @pl.when(pl.program_id(0) == 0)
        def _():
            carry[...] = jnp.zeros_like(carry)

        lf = _log_sigmoid(zf_ref[...] + b_ref[...])
        hi, mid, lo = _split3(lf)
        t = tri_ref[...]
        cs = _dot(t, hi) + _dot(t, mid) + _dot(t, lo) + carry[...]
        c_ref[...] = cs
        carry[...] = cs[CUM_T - 1:CUM_T, :]

    return pl.pallas_call(
        body, name="forget_cumsum", grid=(S // CUM_T,),
        in_specs=[pl.BlockSpec((CUM_T, 128), lambda i: (i, 0)),
                  pl.BlockSpec((1, 128), lambda i: (0, 0)),
                  pl.BlockSpec((CUM_T, CUM_T), lambda i: (0, 0))],
        out_specs=pl.BlockSpec((CUM_T, 128), lambda i: (i, 0)),
        out_shape=jax.ShapeDtypeStruct((S, 128), F32),
        scratch_shapes=[pltpu.VMEM((1, 128), F32)],
        compiler_params=_cp("arbitrary"),
    )(zf, bf, tri)


def _forget_bwd(dc, zf, bf, triu):
    S = zf.shape[0]
    n = S // CUM_T

    def body(dc_ref, zf_ref, b_ref, tri_ref, dzf_ref, db_ref, carry):
        @pl.when(pl.program_id(0) == 0)
        def _():
            carry[...] = jnp.zeros_like(carry)
            db_ref[...] = jnp.zeros_like(db_ref)

        hi, mid, lo = _split3(dc_ref[...])
        t = tri_ref[...]
        dlf = _dot(t, hi) + _dot(t, mid) + _dot(t, lo) + carry[...]
        carry[...] = dlf[0:1, :]
        dfa = dlf * (1.0 - _sigmoid(zf_ref[...] + b_ref[...]))
        dzf_ref[...] = dfa.astype(BF16)
        db_ref[...] += jnp.sum(dfa, axis=0, keepdims=True)

    return pl.pallas_call(
        body, name="forget_bwd", grid=(n,),
        in_specs=[pl.BlockSpec((CUM_T, 128), lambda i: (n - 1 - i, 0)),
                  pl.BlockSpec((CUM_T, 128), lambda i: (n - 1 - i, 0)),
                  pl.BlockSpec((1, 128), lambda i: (0, 0)),
                  pl.BlockSpec((CUM_T, CUM_T), lambda i: (0, 0))],
        out_specs=[pl.BlockSpec((CUM_T, 128), lambda i: (n - 1 - i, 0)),
                   pl.BlockSpec((1, 128), lambda i: (0, 0))],
        out_shape=[jax.ShapeDtypeStruct((S, 128), BF16), jax.ShapeDtypeStruct((1, 128), F32)],
        scratch_shapes=[pltpu.VMEM((1, 128), F32)],
        compiler_params=_cp("arbitrary"),
    )(dc, zf, bf, triu)


def _pair_rsqrt(x, lo_half):
    x2 = x * x
    s0 = jnp.sum(jnp.where(lo_half, x2, 0.0), axis=1, keepdims=True)
    s1 = jnp.sum(jnp.where(lo_half, 0.0, x2), axis=1, keepdims=True)
    return lax.rsqrt(jnp.where(lo_half, s0, s1) * (1.0 / HEAD_DIM) + EPS)


def _pair_mean(x, lo_half):
    s0 = jnp.sum(jnp.where(lo_half, x, 0.0), axis=1, keepdims=True)
    s1 = jnp.sum(jnp.where(lo_half, 0.0, x), axis=1, keepdims=True)
    return jnp.where(lo_half, s0, s1) * (1.0 / HEAD_DIM)


def _prep_fwd(zm, c, qkg, rope_c, rope_a, rope_b):
    S = zm.shape[0]
    shp = jax.ShapeDtypeStruct((N_HEADS, S, HEAD_PAD), BF16)

    def body(z_ref, c_ref, g_ref, rc_ref, ra_ref, rb_ref, qa_ref, ka_ref, va_ref, qb_ref, kb_ref, vb_ref):
        lane = lax.broadcasted_iota(jnp.int32, (TOK_T, HEAD_PAD), 1)
        lo_half = lane < HEAD_DIM
        aug = (lane >= HEAD_DIM) & (lane < HEAD_DIM + 3)
        q_pad = jnp.where(aug, -1.0, 0.0)
        cs = c_ref[...]
        rc, ra, rb = rc_ref[...], ra_ref[...], rb_ref[...]

        def norm(col, gi):
            x = z_ref[:, col:col + HEAD_PAD]
            return x * _pair_rsqrt(x, lo_half) * g_ref[gi:gi + 1, :]

        def rope(y):
            return y * rc + pltpu.roll(y, HEAD_PAD - ROPE_HALF, 1) * ra + pltpu.roll(y, ROPE_HALF, 1) * rb

        def put(ref, pi, y, pad_even, pad_odd):
            ref[2 * pi] = jnp.where(lo_half, y, pad_even).astype(BF16)
            ref[2 * pi + 1] = jnp.where(lo_half, pltpu.roll(y, HEAD_DIM, 1), pad_odd).astype(BF16)

        def k_pad(h):
            ch = cs[:, h:h + 1]
            hi = ch.astype(BF16).astype(F32)
            mid = (ch - hi).astype(BF16).astype(F32)
            lo = ch - hi - mid
            ones = jnp.where(lane == HEAD_DIM + 3, 1.0, 0.0)
            return jnp.where(lane == HEAD_DIM, hi, jnp.where(lane == HEAD_DIM + 1, mid,
                                                              jnp.where(lane == HEAD_DIM + 2, lo, ones)))

        for pi in range(N_HEADS // 2):
            col = HEAD_PAD * pi
            put(qa_ref, pi, norm(col, 0) * SCALE, q_pad, q_pad)
            put(ka_ref, pi, norm(D_BRANCH + col, 1), k_pad(2 * pi), k_pad(2 * pi + 1))
            put(va_ref, pi, z_ref[:, 2 * D_BRANCH + col:2 * D_BRANCH + col + HEAD_PAD], 0.0, 0.0)
            put(qb_ref, pi, rope(norm(4 * D_BRANCH + col, 2)) * SCALE, 0.0, 0.0)
            put(kb_ref, pi, rope(norm(5 * D_BRANCH + col, 3)), 0.0, 0.0)
            put(vb_ref, pi, z_ref[:, 6 * D_BRANCH + col:6 * D_BRANCH + col + HEAD_PAD], 0.0, 0.0)

    tok = lambda w: pl.BlockSpec((TOK_T, w), lambda i: (i, 0))
    head = pl.BlockSpec((N_HEADS, TOK_T, HEAD_PAD), lambda i: (0, i, 0))
    return pl.pallas_call(
        body, name="prep_fwd", grid=(S // TOK_T,),
        in_specs=[tok(N_MAIN), tok(128), pl.BlockSpec((4, 128), lambda i: (0, 0)), tok(128), tok(128), tok(128)],
        out_specs=[head] * 6, out_shape=[shp] * 6,
        compiler_params=_cp("parallel"),
    )(zm, c, qkg, rope_c, rope_a, rope_b)


def _pair(ref, pi, lo_half):
    return jnp.where(lo_half, ref[2 * pi], pltpu.roll(ref[2 * pi + 1], HEAD_DIM, 1))


def _mid_fwd(oa, ob, zm, h0, p, w_out, w_pg, w_ple, g2):
    S = h0.shape[0]

    def body(oa_ref, ob_ref, ga_ref, gb_ref, h0_ref, p_ref, wo_ref, wg_ref, wp_ref, g2_ref,
             y_ref, h1_ref, h2_ref, u2_ref, e_ref, gate_ref):
        lane = lax.broadcasted_iota(jnp.int32, (TOK_T, HEAD_PAD), 1)
        lo_half = lane < HEAD_DIM
        parts = []
        for o_ref, g_ref in ((oa_ref, ga_ref), (ob_ref, gb_ref)):
            for pi in range(N_HEADS // 2):
                g = g_ref[:, HEAD_PAD * pi:HEAD_PAD * (pi + 1)]
                parts.append((_pair(o_ref, pi, lo_half) * (g * _sigmoid(g))).astype(BF16))
        y = jnp.concatenate(parts, axis=1)
        y_ref[...] = y
        h1 = h0_ref[...] + _dot(y, wo_ref[...])
        h1_ref[...] = h1
        r = lax.rsqrt(jnp.mean(h1 * h1, axis=-1, keepdims=True) + EPS)
        u2 = (h1 * r * g2_ref[...]).astype(BF16)
        u2_ref[...] = u2
        gate = _sigmoid(_dot(u2, wg_ref[...]))
        e = _dot(p_ref[...].astype(BF16), wp_ref[...])
        e_ref[...] = e
        gate_ref[...] = gate
        h2_ref[...] = h1 + e * gate

    tok = lambda w: pl.BlockSpec((TOK_T, w), lambda i: (i, 0))
    head = pl.BlockSpec((N_HEADS, TOK_T, HEAD_PAD), lambda i: (0, i, 0))
    full = lambda a, b: pl.BlockSpec((a, b), lambda i: (0, 0))
    act = lambda dt: jax.ShapeDtypeStruct((S, D_MODEL), dt)
    return pl.pallas_call(
        body, name="mid_fwd", grid=(S // TOK_T,),
        in_specs=[head, head,
                  pl.BlockSpec((TOK_T, D_BRANCH), lambda i: (i, 3)), pl.BlockSpec((TOK_T, D_BRANCH), lambda i: (i, 7)),
                  tok(D_MODEL), tok(PLE_DIM), full(D_MODEL, D_MODEL), full(D_MODEL, D_MODEL),
                  full(PLE_DIM, D_MODEL), full(1, D_MODEL)],
        out_specs=[tok(D_MODEL)] * 6,
        out_shape=[act(BF16), act(F32), act(F32), act(BF16), act(F32), act(F32)],
        compiler_params=_cp("parallel"),
    )(oa, ob, zm, zm, h0, p, w_out, w_pg, w_ple, g2)


def _loss_fwd_bwd(y, t):
    S = y.shape[0]

    def body(y_ref, t_ref, dy_ref, loss_ref):
        @pl.when(pl.program_id(0) == 0)
        def _():
            loss_ref[...] = jnp.zeros_like(loss_ref)

        err = y_ref[...] - t_ref[...]
        dy_ref[...] = err * (1.0 / D_MODEL)
        part = jnp.sum(jnp.sum(err * err, axis=1, keepdims=True), axis=0, keepdims=True)
        loss_ref[...] += part * (0.5 / D_MODEL)

    tok = pl.BlockSpec((TOK_T, D_MODEL), lambda i: (i, 0))
    return pl.pallas_call(
        body, name="loss", grid=(S // TOK_T,),
        in_specs=[tok, tok], out_specs=[tok, pl.BlockSpec((8, 128), lambda i: (0, 0))],
        out_shape=[jax.ShapeDtypeStruct((S, D_MODEL), F32), jax.ShapeDtypeStruct((8, 128), F32)],
        compiler_params=_cp("arbitrary"),
    )(y, t)


def _bias_tables(full_range):
    T = ATT_T
    nb = 1 if full_range else DILATED_PATTERNS[-1][0] // T + 1
    r = lax.broadcasted_iota(jnp.int32, (nb, T, T), 1)
    c = lax.broadcasted_iota(jnp.int32, (nb, T, T), 2)
    b = lax.broadcasted_iota(jnp.int32, (nb, T, T), 0)
    delta = T * b + r - c
    if full_range:
        bias = jnp.where(delta >= 0, 0.0, NEG).astype(F32)
    else:
        mult = jnp.zeros((nb, T, T), F32)
        for window, dil in DILATED_PATTERNS:
            ok = (delta >= 0) & (delta <= window) & (delta % dil == 0)
            mult = mult + ok.astype(F32)
        bias = jnp.where(mult > 0, jnp.log(jnp.maximum(mult, 1.0)), NEG).astype(F32)
    return bias, jnp.swapaxes(bias, 1, 2)


def _attn_fwd(q, k, v, table, full_range, name):
    H, S, _ = q.shape
    T = ATT_T
    nb = table.shape[0]

    def body(q_ref, k_ref, v_ref, tab_ref, o_ref, lse_ref):
        i = pl.program_id(1)
        qb = q_ref[0]

        def step(j, bias, carry):
            m, l, acc = carry
            rows = pl.ds(pl.multiple_of(j * T, T), T)
            s = _dot_nt(qb, k_ref[0, rows, :])
            if bias is not None:
                s = s + bias
            m_new = jnp.maximum(m, jnp.max(s, axis=1, keepdims=True))
            alpha = jnp.exp(m - m_new)
            pr = jnp.exp(s - m_new)
            l = alpha * l + jnp.sum(pr, axis=1, keepdims=True)
            acc = alpha * acc + _dot(pr.astype(BF16), v_ref[0, rows, :])
            return m_new, l, acc

        init = (jnp.full((T, 1), NEG, F32), jnp.zeros((T, 1), F32), jnp.zeros((T, HEAD_PAD), F32))
        carry = step(i, tab_ref[0], init)
        if full_range:
            carry = lax.fori_loop(0, i, lambda j, cr: step(j, None, cr), carry)
        else:
            carry = lax.fori_loop(jnp.maximum(i - (nb - 1), 0), i, lambda j, cr: step(j, tab_ref[i - j], cr), carry)
        m, l, acc = carry
        o_ref[0] = acc / l
        lse_ref[0] = m + jnp.log(l)

    return pl.pallas_call(
        body, name=name, grid=(H, S // T),
        in_specs=[pl.BlockSpec((1, T, HEAD_PAD), lambda h, i: (h, i, 0)),
                  pl.BlockSpec((1, S, HEAD_PAD), lambda h, i: (h, 0, 0)),
                  pl.BlockSpec((1, S, HEAD_PAD), lambda h, i: (h, 0, 0)),
                  pl.BlockSpec((nb, T, T), lambda h, i: (0, 0, 0))],
        out_specs=[pl.BlockSpec((1, T, HEAD_PAD), lambda h, i: (h, i, 0)),
                   pl.BlockSpec((1, T, 1), lambda h, i: (h, i, 0))],
        out_shape=[jax.ShapeDtypeStruct((H, S, HEAD_PAD), F32), jax.ShapeDtypeStruct((H, S, 1), F32)],
        compiler_params=_cp("parallel", "arbitrary"),
    )(q, k, v, table)


def _attn_bwd(q, k, v, do, lse, dd, table_t, full_range, name):
    H, S, _ = q.shape
    T = ATT_T
    nq = S // T
    nb = table_t.shape[0]

    def body(q_ref, do_ref, lse_ref, dd_ref, k_ref, v_ref, tab_ref, dq_ref, dk_ref, dv_ref):
        j = pl.program_id(1)

        @pl.when(j == 0)
        def _():
            dq_ref[...] = jnp.zeros_like(dq_ref)

        kb = k_ref[0]
        vb = v_ref[0]

        def step(i, bias, carry):
            dk, dv = carry
            rows = pl.ds(pl.multiple_of(i * T, T), T)
            qb = q_ref[0, rows, :]
            dob = do_ref[0, rows, :]
            st = _dot_nt(kb, qb)
            if bias is not None:
                st = st + bias
            pt = jnp.exp(st - lse_ref[0, i])
            dpt = _dot_nt(vb, dob)
            dst = (pt * (dpt - dd_ref[0, i])).astype(BF16)
            dv = dv + _dot(pt.astype(BF16), dob)
            dk = dk + _dot(dst, qb)
            dq_ref[0, rows, :] += _dot_tn(dst, kb)
            return dk, dv

        zero = jnp.zeros((T, HEAD_PAD), F32)
        carry = step(j, tab_ref[0], (zero, zero))
        if full_range:
            carry = lax.fori_loop(j + 1, nq, lambda i, cr: step(i, None, cr), carry)
        else:
            carry = lax.fori_loop(j + 1, jnp.minimum(j + nb, nq), lambda i, cr: step(i, tab_ref[i - j], cr), carry)
        dk_ref[0] = carry[0]
        dv_ref[0] = carry[1]

    per_head = pl.BlockSpec((1, S, HEAD_PAD), lambda h, j: (h, 0, 0))
    rows = pl.BlockSpec((1, nq, 1, T), lambda h, j: (h, 0, 0, 0))
    blk = pl.BlockSpec((1, T, HEAD_PAD), lambda h, j: (h, j, 0))
    shp = jax.ShapeDtypeStruct((H, S, HEAD_PAD), F32)
    return pl.pallas_call(
        body, name=name, grid=(H, nq),
        in_specs=[per_head, per_head, rows, rows, blk, blk, pl.BlockSpec((nb, T, T), lambda h, j: (0, 0, 0))],
        out_specs=[per_head, blk, blk], out_shape=[shp, shp, shp],
        compiler_params=_cp("parallel", "arbitrary"),
    )(q, do, lse, dd, k, v, table_t)


def _mid_bwd(dh2, h1, e, gate, g2, w_pg_t, w_out_t, oa, ob, zm):
    S = dh2.shape[0]

    def body(dh2_ref, h1_ref, e_ref, gate_ref, g2_ref, wg_ref, wo_ref, oa_ref, ob_ref, ga_ref, gb_ref,
             dh1_ref, dh1b_ref, de_ref, dpre_ref, doa_ref, dob_ref, dga_ref, dgb_ref, dda_ref, ddb_ref, dg2_ref):
        @pl.when(pl.program_id(0) == 0)
        def _():
            dg2_ref[...] = jnp.zeros_like(dg2_ref)

        lane = lax.broadcasted_iota(jnp.int32, (TOK_T, HEAD_PAD), 1)
        lo_half = lane < HEAD_DIM
        dh2 = dh2_ref[...]
        gate = gate_ref[...]
        de_ref[...] = (dh2 * gate).astype(BF16)
        dpre = (dh2 * e_ref[...] * gate * (1.0 - gate)).astype(BF16)
        dpre_ref[...] = dpre
        du2 = _dot(dpre, wg_ref[...])
        h1 = h1_ref[...]
        r = lax.rsqrt(jnp.mean(h1 * h1, axis=-1, keepdims=True) + EPS)
        xh = h1 * r
        a = du2 * g2_ref[...]
        dh1 = dh2 + r * (a - xh * jnp.mean(a * xh, axis=-1, keepdims=True))
        dg2_ref[...] += jnp.sum(du2 * xh, axis=0, keepdims=True)
        dh1_ref[...] = dh1
        dh1b = dh1.astype(BF16)
        dh1b_ref[...] = dh1b
        dy = _dot(dh1b, wo_ref[...])
        for bi, (o_ref, g_ref, do_ref, dg_ref, dd_ref) in enumerate(
                ((oa_ref, ga_ref, doa_ref, dga_ref, dda_ref), (ob_ref, gb_ref, dob_ref, dgb_ref, ddb_ref))):
            for pi in range(N_HEADS // 2):
                col = bi * D_BRANCH + HEAD_PAD * pi
                dyp = dy[:, col:col + HEAD_PAD]
                g = g_ref[:, HEAD_PAD * pi:HEAD_PAD * (pi + 1)]
                sg = _sigmoid(g)
                dg_ref[:, HEAD_PAD * pi:HEAD_PAD * (pi + 1)] = (
                    dyp * _pair(o_ref, pi, lo_half) * (sg * (1.0 + g * (1.0 - sg)))).astype(BF16)
                dop = dyp * (g * sg)
                for hh, d_head in ((2 * pi, dop), (2 * pi + 1, pltpu.roll(dop, HEAD_DIM, 1))):
                    d_head = jnp.where(lo_half, d_head, 0.0)
                    do_ref[hh] = d_head.astype(BF16)
                    dd_ref[hh] = jnp.sum(d_head * o_ref[hh], axis=1, keepdims=True)

    tok = lambda w: pl.BlockSpec((TOK_T, w), lambda i: (i, 0))
    head = pl.BlockSpec((N_HEADS, TOK_T, HEAD_PAD), lambda i: (0, i, 0))
    head1 = pl.BlockSpec((N_HEADS, TOK_T, 1), lambda i: (0, i, 0))
    full = lambda a, b: pl.BlockSpec((a, b), lambda i: (0, 0))
    act = lambda w, dt: jax.ShapeDtypeStruct((S, w), dt)
    hshape = lambda w, dt: jax.ShapeDtypeStruct((N_HEADS, S, w), dt)
    return pl.pallas_call(
        body, name="mid_bwd", grid=(S // TOK_T,),
        in_specs=[tok(D_MODEL)] * 4 + [full(1, D_MODEL), full(D_MODEL, D_MODEL), full(D_MODEL, D_MODEL), head, head,
                                      pl.BlockSpec((TOK_T, D_BRANCH), lambda i: (i, 3)),
                                      pl.BlockSpec((TOK_T, D_BRANCH), lambda i: (i, 7))],
        out_specs=[tok(D_MODEL)] * 4 + [head, head, tok(D_BRANCH), tok(D_BRANCH), head1, head1, full(1, D_MODEL)],
        out_shape=[act(D_MODEL, F32), act(D_MODEL, BF16), act(D_MODEL, BF16), act(D_MODEL, BF16),
                   hshape(HEAD_PAD, BF16), hshape(HEAD_PAD, BF16), act(D_BRANCH, BF16), act(D_BRANCH, BF16),
                   hshape(1, F32), hshape(1, F32), jax.ShapeDtypeStruct((1, D_MODEL), F32)],
        compiler_params=_cp("arbitrary"),
    )(dh2, h1, e, gate, g2, w_pg_t, w_out_t, oa, ob, zm, zm)


def _prep_bwd(dqa, dka, dva, dqb, dkb, dvb, zm, qkg, rope_c, rope_a, rope_b, dga, dgb):
    S = zm.shape[0]

    def body(dqa_ref, dka_ref, dva_ref, dqb_ref, dkb_ref, dvb_ref, z_ref, g_ref, rc_ref, ra_ref, rb_ref,
             dga_ref, dgb_ref, dz_ref, dc_ref, dqkg_ref):
        @pl.when(pl.program_id(0) == 0)
        def _():
            dqkg_ref[...] = jnp.zeros_like(dqkg_ref)

        lane = lax.broadcasted_iota(jnp.int32, (TOK_T, HEAD_PAD), 1)
        lo_half = lane < HEAD_DIM
        rc, ra, rb = rc_ref[...], ra_ref[...], rb_ref[...]

        def unrope(dy):
            return dy * rc + pltpu.roll(dy * ra, ROPE_HALF, 1) + pltpu.roll(dy * rb, HEAD_PAD - ROPE_HALF, 1)

        def norm_bwd(col, gi, dy):
            x = z_ref[:, col:col + HEAD_PAD]
            r = _pair_rsqrt(x, lo_half)
            xh = x * r
            dqkg_ref[gi:gi + 1, :] += jnp.sum(dy * xh, axis=0, keepdims=True)
            a = dy * g_ref[gi:gi + 1, :]
            dz_ref[:, col:col + HEAD_PAD] = (r * (a - xh * _pair_mean(a * xh, lo_half))).astype(BF16)

        dc = jnp.zeros((TOK_T, HEAD_PAD), F32)
        for pi in range(N_HEADS // 2):
            col = HEAD_PAD * pi
            norm_bwd(col, 0, _pair(dqa_ref, pi, lo_half) * SCALE)
            norm_bwd(D_BRANCH + col, 1, _pair(dka_ref, pi, lo_half))
            dz_ref[:, 2 * D_BRANCH + col:2 * D_BRANCH + col + HEAD_PAD] = _pair(dva_ref, pi, lo_half).astype(BF16)
            norm_bwd(4 * D_BRANCH + col, 2, unrope(_pair(dqb_ref, pi, lo_half) * SCALE))
            norm_bwd(5 * D_BRANCH + col, 3, unrope(_pair(dkb_ref, pi, lo_half)))
            dz_ref[:, 6 * D_BRANCH + col:6 * D_BRANCH + col + HEAD_PAD] = _pair(dvb_ref, pi, lo_half).astype(BF16)
            for hh in (2 * pi, 2 * pi + 1):
                dch = dka_ref[hh][:, HEAD_DIM:HEAD_DIM + 1] + dqa_ref[hh][:, HEAD_DIM + 3:HEAD_DIM + 4]
                dc = dc + jnp.where(lane == hh, dch, 0.0)
        dz_ref[:, 3 * D_BRANCH:4 * D_BRANCH] = dga_ref[...]
        dz_ref[:, 7 * D_BRANCH:8 * D_BRANCH] = dgb_ref[...]
        dc_ref[...] = dc

    tok = lambda w: pl.BlockSpec((TOK_T, w), lambda i: (i, 0))
    head = pl.BlockSpec((N_HEADS, TOK_T, HEAD_PAD), lambda i: (0, i, 0))
    return pl.pallas_call(
        body, name="prep_bwd", grid=(S // TOK_T,),
        in_specs=[head] * 6 + [tok(N_MAIN), pl.BlockSpec((4, 128), lambda i: (0, 0)), tok(128), tok(128), tok(128),
                               tok(D_BRANCH), tok(D_BRANCH)],
        out_specs=[tok(N_MAIN), tok(128), pl.BlockSpec((4, 128), lambda i: (0, 0))],
        out_shape=[jax.ShapeDtypeStruct((S, N_MAIN), BF16), jax.ShapeDtypeStruct((S, 128), F32),
                   jax.ShapeDtypeStruct((4, 128), F32)],
        compiler_params=_cp("arbitrary"),
    )(dqa, dka, dva, dqb, dkb, dvb, zm, qkg, rope_c, rope_a, rope_b, dga, dgb)


def _inproj_bwd(dzm, dzf, wm_t, wf_t, h0, dh1, g):
    S = h0.shape[0]

    def body(dzm_ref, dzf_ref, wm_ref, wf_ref, h_ref, dh1_ref, g_ref, dh0_ref, dg_ref):
        @pl.when(pl.program_id(0) == 0)
        def _():
            dg_ref[...] = jnp.zeros_like(dg_ref)

        du = _dot(dzm_ref[...], wm_ref[...]) + _dot(dzf_ref[...], wf_ref[...])
        x = h_ref[...]
        r = lax.rsqrt(jnp.mean(x * x, axis=-1, keepdims=True) + EPS)
        xh = x * r
        a = du * g_ref[...]
        dh0_ref[...] = dh1_ref[...] + r * (a - xh * jnp.mean(a * xh, axis=-1, keepdims=True))
        dg_ref[...] += jnp.sum(du * xh, axis=0, keepdims=True)

    tok = lambda w: pl.BlockSpec((TOK_T, w), lambda i: (i, 0))
    full = lambda a, b: pl.BlockSpec((a, b), lambda i: (0, 0))
    return pl.pallas_call(
        body, name="inproj_bwd", grid=(S // TOK_T,),
        in_specs=[tok(N_MAIN), tok(128), full(N_MAIN, D_MODEL), full(128, D_MODEL), tok(D_MODEL), tok(D_MODEL),
                  full(1, D_MODEL)],
        out_specs=[tok(D_MODEL), full(1, D_MODEL)],
        out_shape=[jax.ShapeDtypeStruct((S, D_MODEL), F32), jax.ShapeDtypeStruct((1, D_MODEL), F32)],
        compiler_params=_cp("arbitrary"),
    )(dzm, dzf, wm_t, wf_t, h0, dh1, g)


def _wgrad(a_t, b, name):
    M, S = a_t.shape
    N = b.shape[1]
    tn = min(N, 1024)
    ts = 512

    def body(a_ref, b_ref, o_ref):
        @pl.when(pl.program_id(1) == 0)
        def _():
            o_ref[...] = jnp.zeros_like(o_ref)

        o_ref[...] += _dot(a_ref[...], b_ref[...])

    return pl.pallas_call(
        body, name=name, grid=(N // tn, S // ts),
        in_specs=[pl.BlockSpec((M, ts), lambda n, s: (0, s)), pl.BlockSpec((ts, tn), lambda n, s: (s, n))],
        out_specs=pl.BlockSpec((M, tn), lambda n, s: (0, n)),
        out_shape=jax.ShapeDtypeStruct((M, N), F32),
        compiler_params=_cp("parallel", "arbitrary"),
    )(a_t, b)


def _rope_tables(positions):
    inv_freq = ROPE_THETA ** (-jnp.arange(ROPE_HALF, dtype=F32) / ROPE_HALF)
    ang = positions.astype(F32)[:, None] * inv_freq
    cos, sin = jnp.cos(ang), jnp.sin(ang)
    S = positions.shape[0]
    one, zero = jnp.ones((S, HEAD_DIM - 2 * ROPE_HALF), F32), jnp.zeros((S, HEAD_DIM - 2 * ROPE_HALF), F32)
    z8 = jnp.zeros((S, ROPE_HALF), F32)
    rc = jnp.concatenate([cos, cos, one], axis=1)
    ra = jnp.concatenate([-sin, z8, zero], axis=1)
    rb = jnp.concatenate([z8, sin, zero], axis=1)
    return tuple(jnp.tile(t, (1, 2)) for t in (rc, ra, rb))


def _layer_weights(w_in, w_out, w_ple, w_pg):
    w_in = w_in.astype(BF16)
    wm = jnp.concatenate([w_in[:, :4 * D_BRANCH], w_in[:, 4 * D_BRANCH + N_HEADS:]], axis=1)
    wf = jnp.pad(w_in[:, 4 * D_BRANCH:4 * D_BRANCH + N_HEADS], ((0, 0), (0, 128 - N_HEADS)))
    w_out, w_ple, w_pg = w_out.astype(BF16), w_ple.astype(BF16), w_pg.astype(BF16)
    return dict(wm=wm, wf=wf, wm_t=wm.T, wf_t=wf.T, w_out=w_out, w_out_t=w_out.T, w_ple=w_ple, w_pg=w_pg,
                w_pg_t=w_pg.T)


def _row(v, width=128):
    v = v.reshape(1, -1).astype(F32)
    return jnp.pad(v, ((0, 0), (0, width - v.shape[1])))


def _layer_fwd(h0, p, rope, tabs, w, norm_g, b_f, qk_g, ple_g):
    S = h0.shape[0]
    g1 = norm_g.reshape(1, D_MODEL)
    g2 = ple_g.reshape(1, D_MODEL)
    qkg = jnp.tile(qk_g, (1, 2))
    bf = _row(b_f)
    zm, zf, u = _inproj_fwd(h0, g1, w["wm"], w["wf"])
    c = _forget_cumsum(zf, bf, tabs["tril"])
    qa, ka, va, qb, kb, vb = _prep_fwd(zm, c, qkg, *rope)
    oa, lse_a = _attn_fwd(qa, ka, va, tabs["fox"][0], True, "fox_fwd")
    ob, lse_b = _attn_fwd(qb, kb, vb, tabs["dil"][0], False, "dil_fwd")
    y, h1, h2, u2, e, gate = _mid_fwd(oa, ob, zm, h0, p, w["w_out"], w["w_pg"], w["w_ple"], g2)
    saved = dict(h0=h0, p=p, zm=zm, zf=zf, u=u, qa=qa, ka=ka, va=va, qb=qb, kb=kb, vb=vb, oa=oa, ob=ob,
                 lse_a=lse_a, lse_b=lse_b, y=y, h1=h1, u2=u2, e=e, gate=gate, g1=g1, g2=g2, qkg=qkg, bf=bf)
    return h2, saved


def _layer_bwd(dh2, sv, rope, tabs, w):
    S = dh2.shape[0]
    nq = S // ATT_T
    rows = lambda a: a.reshape(N_HEADS, nq, 1, ATT_T)
    (dh1, dh1b, de, dpre, doa, dob, dga, dgb, dda, ddb, dg2) = _mid_bwd(
        dh2, sv["h1"], sv["e"], sv["gate"], sv["g2"], w["w_pg_t"], w["w_out_t"], sv["oa"], sv["ob"], sv["zm"])
    dqa, dka, dva = _attn_bwd(sv["qa"], sv["ka"], sv["va"], doa, rows(sv["lse_a"]), rows(dda), tabs["fox"][1], True,
                              "fox_bwd")
    dqb, dkb, dvb = _attn_bwd(sv["qb"], sv["kb"], sv["vb"], dob, rows(sv["lse_b"]), rows(ddb), tabs["dil"][1], False,
                              "dil_bwd")
    dzm, dc, dqkg = _prep_bwd(dqa, dka, dva, dqb, dkb, dvb, sv["zm"], sv["qkg"], *rope, dga, dgb)
    dzf, dbf = _forget_bwd(dc, sv["zf"], sv["bf"], tabs["triu"])
    dh0, dg1 = _inproj_bwd(dzm, dzf, w["wm_t"], w["wf_t"], sv["h0"], dh1, sv["g1"])
    u_t = sv["u"].T
    dwm = _wgrad(u_t, dzm, "wgrad_in")
    dwf = _wgrad(u_t, dzf, "wgrad_f")
    dw_in = jnp.concatenate([dwm[:, :4 * D_BRANCH], dwf[:, :N_HEADS], dwm[:, 4 * D_BRANCH:]], axis=1)
    grads = dict(
        norm_g=dg1.reshape(D_MODEL), w_in=dw_in, b_f=dbf[0, :N_HEADS],
        qk_norm_g=dqkg[:, :HEAD_DIM] + dqkg[:, HEAD_DIM:],
        w_out=_wgrad(sv["y"].T, dh1b, "wgrad_out"), w_ple=_wgrad(sv["p"].astype(BF16).T, de, "wgrad_ple"),
        ple_norm_g=dg2.reshape(D_MODEL), w_ple_gate=_wgrad(sv["u2"].T, dpre, "wgrad_gate"))
    return dh0, grads


def _tables():
    T = CUM_T
    r = lax.broadcasted_iota(jnp.int32, (T, T), 0)
    c = lax.broadcasted_iota(jnp.int32, (T, T), 1)
    return dict(fox=_bias_tables(True), dil=_bias_tables(False),
                tril=(c <= r).astype(BF16), triu=(c >= r).astype(BF16))


def _local_step(x, p, positions, target, layers, small):
    rope = _rope_tables(positions)
    tabs = _tables()
    ws = [_layer_weights(*lw) for lw in layers]
    h = x
    saved = []
    for w, lp, sm in zip(ws, p, small):
        h, sv = _layer_fwd(h, lp, rope, tabs, w, *sm)
        saved.append(sv)
    dh, loss = _loss_fwd_bwd(h, target)
    grads = [None] * len(ws)
    for li in reversed(range(len(ws))):
        dh, grads[li] = _layer_bwd(dh, saved[li], rope, tabs, ws[li])
    return loss[0, 0], dh, grads


def _peers():
    x, y, c = lax.axis_index("x"), lax.axis_index("y"), lax.axis_index("c")
    me = 4 * x + 2 * y + c
    flip = lambda v, bit: 1 - v if bit else v
    return me, [(flip(x, k & 4), flip(y, k & 2), flip(c, k & 1)) for k in range(1, N_DEV)]


def _all_gather(block):
    def body(x_ref, out_ref, send_sems, recv_sems, local_sem):
        me, peers = _peers()
        mine = pltpu.make_async_copy(x_ref, out_ref.at[me], local_sem)
        mine.start()
        copies = [pltpu.make_async_remote_copy(src_ref=x_ref, dst_ref=out_ref.at[me], send_sem=send_sems.at[k],
                                               recv_sem=recv_sems.at[k], device_id=peer,
                                               device_id_type=pl.DeviceIdType.MESH)
                  for k, peer in enumerate(peers)]
        for cp in copies:
            cp.start()
        for cp in copies:
            cp.wait()
        mine.wait()

    return pl.pallas_call(
        body, name="gather_weights",
        in_specs=[pl.BlockSpec(memory_space=pltpu.HBM)], out_specs=pl.BlockSpec(memory_space=pltpu.HBM),
        out_shape=jax.ShapeDtypeStruct((N_DEV,) + block.shape, block.dtype),
        scratch_shapes=[pltpu.SemaphoreType.DMA((N_DEV - 1,)), pltpu.SemaphoreType.DMA((N_DEV - 1,)),
                        pltpu.SemaphoreType.DMA],
    )(block)


def _all_to_all(parts):
    def body(x_ref, out_ref, send_sems, recv_sems, local_sem):
        me, peers = _peers()
        mine = pltpu.make_async_copy(x_ref.at[me], out_ref.at[me], local_sem)
        mine.start()
        copies = [pltpu.make_async_remote_copy(src_ref=x_ref.at[4 * px + 2 * py + pc], dst_ref=out_ref.at[me],
                                               send_sem=send_sems.at[k], recv_sem=recv_sems.at[k],
                                               device_id=(px, py, pc), device_id_type=pl.DeviceIdType.MESH)
                  for k, (px, py, pc) in enumerate(peers)]
        for cp in copies:
            cp.start()
        for cp in copies:
            cp.wait()
        mine.wait()

    return pl.pallas_call(
        body, name="exchange_grads",
        in_specs=[pl.BlockSpec(memory_space=pltpu.HBM)], out_specs=pl.BlockSpec(memory_space=pltpu.HBM),
        out_shape=jax.ShapeDtypeStruct(parts.shape, parts.dtype),
        scratch_shapes=[pltpu.SemaphoreType.DMA((N_DEV - 1,)), pltpu.SemaphoreType.DMA((N_DEV - 1,)),
                        pltpu.SemaphoreType.DMA],
    )(parts)


def _adamw(parts, w, m, v):
    def body(p_ref, w_ref, m_ref, v_ref, g_ref, d_ref, nm_ref, nv_ref):
        g = p_ref[0]
        for s in range(1, N_DEV):
            g = g + p_ref[s]
        g_ref[...] = g
        nm = ADAM_B1 * m_ref[...] + (1.0 - ADAM_B1) * g
        nv = ADAM_B2 * v_ref[...] + (1.0 - ADAM_B2) * (g * g)
        nm_ref[...] = nm
        nv_ref[...] = nv
        m_hat = nm / (1.0 - ADAM_B1 ** ADAM_STEP)
        v_hat = nv / (1.0 - ADAM_B2 ** ADAM_STEP)
        d_ref[...] = -ADAM_LR * (m_hat / (jnp.sqrt(v_hat) + ADAM_EPS) + ADAM_WD * w_ref[...])

    blk = pl.BlockSpec((ADAM_T, 128), lambda i: (i, 0))
    shp = jax.ShapeDtypeStruct((R_TOT, 128), F32)
    return pl.pallas_call(
        body, name="adamw", grid=(R_TOT // ADAM_T,),
        in_specs=[pl.BlockSpec((N_DEV, ADAM_T, 128), lambda i: (0, i, 0)), blk, blk, blk],
        out_specs=[blk] * 4, out_shape=[shp] * 4,
        compiler_params=_cp("parallel"),
    )(parts, w, m, v)


def _rows(a):
    return a.astype(F32).reshape(-1, 128)


def _pack(w_in, w_out, w_ple, w_pg, norm_g, ple_g, qk_g, b_f, loss_row):
    parts = [_rows(w_in), _rows(w_out), _rows(w_ple), _rows(w_pg), _rows(norm_g), _rows(ple_g), _rows(qk_g),
             _row(b_f.reshape(-1)), loss_row]
    flat = jnp.concatenate(parts, axis=0)
    return jnp.pad(flat, ((0, R_TOT - flat.shape[0]), (0, 0)))


def _unpack(flat):
    return (flat[O_NG:O_NG + 16].reshape(2, D_MODEL), flat[:R_WIN].reshape(2, D_MODEL, 513),
            flat[O_BF, :2 * N_HEADS].reshape(2, N_HEADS), flat[O_QK:O_QK + 4].reshape(2, 4, HEAD_DIM),
            flat[O_WOUT:O_WOUT + R_WOUT].reshape(2, 128, D_MODEL), flat[O_WPLE:O_WPLE + R_WPLE].reshape(2, PLE_DIM, 128),
            flat[O_PNG:O_PNG + 16].reshape(2, D_MODEL), flat[O_WPG:O_WPG + R_WOUT].reshape(2, 128, D_MODEL))


def kernel(x, p, positions, norm_g, w_in, b_f, qk_norm_g, w_out, w_ple, ple_norm_g, w_ple_gate, loss_target, m_norm_g, m_w_in, m_b_f, m_qk_norm_g, m_w_out, m_w_ple, m_ple_norm_g, m_w_ple_gate, v_norm_g, v_w_in, v_b_f, v_qk_norm_g, v_w_out, v_w_ple, v_ple_norm_g, v_w_ple_gate):
    bf16_rows = lambda a: a.astype(BF16).reshape(-1, 128)
    mine = jnp.concatenate([bf16_rows(w_in), bf16_rows(w_out), bf16_rows(w_ple), bf16_rows(w_ple_gate)], axis=0)
    g = _all_gather(mine)
    full_in = g[:, :R_WIN].reshape(N_DEV, 2, D_MODEL, 513).transpose(1, 2, 0, 3).reshape(2, D_MODEL, N_IN)
    full_out = g[:, O_WOUT:O_WOUT + R_WOUT].reshape(N_DEV, 2, 128, D_MODEL).transpose(1, 0, 2, 3).reshape(2, D_MODEL, D_MODEL)
    full_ple = g[:, O_WPLE:O_WPLE + R_WPLE].reshape(N_DEV, 2, PLE_DIM, 128).transpose(1, 2, 0, 3).reshape(2, PLE_DIM, D_MODEL)
    full_pg = g[:, O_WPG:O_WPG + R_WOUT].reshape(N_DEV, 2, 128, D_MODEL).transpose(1, 0, 2, 3).reshape(2, D_MODEL, D_MODEL)

    layers = [(full_in[l], full_out[l], full_ple[l], full_pg[l]) for l in range(2)]
    small = [(norm_g[l], b_f[l], qk_norm_g[l], ple_norm_g[l]) for l in range(2)]
    loss, dx, grads = _local_step(x[0], p[:, 0], positions[0], loss_target[0], layers, small)

    stack = lambda name: jnp.stack([gl[name] for gl in grads], axis=0)
    d_in, d_out, d_ple, d_pg = stack("w_in"), stack("w_out"), stack("w_ple"), stack("w_ple_gate")
    loss_row = _row(loss.reshape(1))
    parts = jnp.stack([
        _pack(d_in[:, :, 513 * d:513 * (d + 1)], d_out[:, 128 * d:128 * (d + 1)], d_ple[:, :, 128 * d:128 * (d + 1)],
              d_pg[:, 128 * d:128 * (d + 1)], stack("norm_g"), stack("ple_norm_g"), stack("qk_norm_g"), stack("b_f"),
              loss_row)
        for d in range(N_DEV)], axis=0)
    got = _all_to_all(parts)

    zero_row = jnp.zeros((1, 128), F32)
    pack_mine = lambda wi, wo, wp, wg, ng, pg, qk, bf: _pack(wi, wo, wp, wg, ng, pg, qk, bf, zero_row)
    w_flat = pack_mine(w_in, w_out, w_ple, w_ple_gate, norm_g, ple_norm_g, qk_norm_g, b_f)
    m_flat = pack_mine(m_w_in, m_w_out, m_w_ple, m_w_ple_gate, m_norm_g, m_ple_norm_g, m_qk_norm_g, m_b_f)
    v_flat = pack_mine(v_w_in, v_w_out, v_w_ple, v_w_ple_gate, v_norm_g, v_ple_norm_g, v_qk_norm_g, v_b_f)
    g_flat, d_flat, nm_flat, nv_flat = _adamw(got, w_flat, m_flat, v_flat)
    return (g_flat[O_LOSS, 0], dx[None], *_unpack(g_flat), *_unpack(d_flat), *_unpack(nm_flat), *_unpack(nv_flat))
```

```python
import functools

import jax
import jax.numpy as jnp
from jax import lax
from jax.experimental import pallas as pl
from jax.experimental.pallas import tpu as pltpu

F32 = jnp.float32
BF16 = jnp.bfloat16

D_MODEL = 1024
HEAD_DIM = 64
N_HEADS = 8
HEAD_PAD = 128
D_BRANCH = N_HEADS * HEAD_DIM
N_MAIN = 8 * D_BRANCH
N_IN = N_MAIN + N_HEADS
PLE_DIM = 256
ROPE_THETA = 500000.0
ROPE_HALF = 8
EPS = 1e-6
NEG = -1e30
SCALE = HEAD_DIM ** -0.5
DILATED_PATTERNS = ((128, 1), (512, 4), (2048, 16))
N_DEV = 8

ADAM_LR = 0.001
ADAM_B1 = 0.9
ADAM_B2 = 0.999
ADAM_EPS = 1e-08
ADAM_WD = 0.01
ADAM_STEP = 10

ATT_T = 512
ATT_FWD_HEADS = 2
ATT_CHUNK = 32
TOK_T = 256
CUM_T = 512
VMEM_LIMIT = 56 * 1024 * 1024


def _cp(*sem):
    return pltpu.CompilerParams(dimension_semantics=sem, vmem_limit_bytes=VMEM_LIMIT)


def _sigmoid(x):
    return 1.0 / (1.0 + jnp.exp(-x))


def _split3(x):
    hi = x.astype(BF16)
    r1 = x - hi.astype(F32)
    mid = r1.astype(BF16)
    lo = (r1 - mid.astype(F32)).astype(BF16)
    return hi, mid, lo


def _dot(a, b):
    return jnp.dot(a, b, preferred_element_type=F32)


def _dot_nt(a, b):
    return lax.dot_general(a, b, (((1,), (1,)), ((), ())), preferred_element_type=F32)


def _dot_tn(a, b):
    return lax.dot_general(a, b, (((0,), (0,)), ((), ())), preferred_element_type=F32)


def _inproj_fwd(h, g, wm, wf):
    S = h.shape[0]

    def body(h_ref, g_ref, wm_ref, wf_ref, zm_ref, zf_ref, u_ref):
        x = h_ref[...]
        r = lax.rsqrt(jnp.mean(x * x, axis=-1, keepdims=True) + EPS)
        u = (x * r * g_ref[...]).astype(BF16)
        u_ref[...] = u
        zm_ref[...] = _dot(u, wm_ref[...])
        zf_ref[...] = _dot(u, wf_ref[...])

    return pl.pallas_call(
        body, name="inproj_fwd", grid=(S // TOK_T,),
        in_specs=[pl.BlockSpec((TOK_T, D_MODEL), lambda i: (i, 0)),
                  pl.BlockSpec((1, D_MODEL), lambda i: (0, 0)),
                  pl.BlockSpec((D_MODEL, N_MAIN), lambda i: (0, 0)),
                  pl.BlockSpec((D_MODEL, 128), lambda i: (0, 0))],
        out_specs=[pl.BlockSpec((TOK_T, N_MAIN), lambda i: (i, 0)),
                   pl.BlockSpec((TOK_T, 128), lambda i: (i, 0)),
                   pl.BlockSpec((TOK_T, D_MODEL), lambda i: (i, 0))],
        out_shape=[jax.ShapeDtypeStruct((S, N_MAIN), F32), jax.ShapeDtypeStruct((S, 128), F32),
                   jax.ShapeDtypeStruct((S, D_MODEL), BF16)],
        compiler_params=_cp("parallel"),
    )(h, g, wm, wf)


def _log_sigmoid(x):
    return jnp.minimum(x, 0.0) - jnp.log(1.0 + jnp.exp(-jnp.abs(x)))


def _forget_cumsum(zf, bf, tri):
    S = zf.shape[0]

    def body(zf_ref, b_ref, tri_ref, c_ref, carry):
        @pl.when(pl.program_id(0) == 0)
        def _():
            carry[...] = jnp.zeros_like(carry)

        lf = _log_sigmoid(zf_ref[...] + b_ref[...])
        hi, mid, lo = _split3(lf)
        t = tri_ref[...]
        cs = _dot(t, hi) + _dot(t, mid) + _dot(t, lo) + carry[...]
        c_ref[...] = cs
        carry[...] = cs[CUM_T - 1:CUM_T, :]

    return pl.pallas_call(
        body, name="forget_cumsum", grid=(S // CUM_T,),
        in_specs=[pl.BlockSpec((CUM_T, 128), lambda i: (i, 0)),
                  pl.BlockSpec((1, 128), lambda i: (0, 0)),
                  pl.BlockSpec((CUM_T, CUM_T), lambda i: (0, 0))],
        out_specs=pl.BlockSpec((CUM_T, 128), lambda i: (i, 0)),
        out_shape=jax.ShapeDtypeStruct((S, 128), F32),
        scratch_shapes=[pltpu.VMEM((1, 128), F32)],
        compiler_params=_cp("arbitrary"),
    )(zf, bf, tri)


def _forget_bwd(dc, zf, bf, triu):
    S = zf.shape[0]
    n = S // CUM_T

    def body(dc_ref, zf_ref, b_ref, tri_ref, dzf_ref, db_ref, carry):
        @pl.when(pl.program_id(0) == 0)
        def _():
            carry[...] = jnp.zeros_like(carry)
            db_ref[...] = jnp.zeros_like(db_ref)

        hi, mid, lo = _split3(dc_ref[...])
        t = tri_ref[...]
        dlf = _dot(t, hi) + _dot(t, mid) + _dot(t, lo) + carry[...]
        carry[...] = dlf[0:1, :]
        dfa = dlf * (1.0 - _sigmoid(zf_ref[...] + b_ref[...]))
        dzf_ref[...] = dfa.astype(BF16)
        db_ref[...] += jnp.sum(dfa, axis=0, keepdims=True)

    return pl.pallas_call(
        body, name="forget_bwd", grid=(n,),
        in_specs=[pl.BlockSpec((CUM_T, 128), lambda i: (n - 1 - i, 0)),
                  pl.BlockSpec((CUM_T, 128), lambda i: (n - 1 - i, 0)),
                  pl.BlockSpec((1, 128), lambda i: (0, 0)),
                  pl.BlockSpec((CUM_T, CUM_T), lambda i: (0, 0))],
        out_specs=[pl.BlockSpec((CUM_T, 128), lambda i: (n - 1 - i, 0)),
                   pl.BlockSpec((1, 128), lambda i: (0, 0))],
        out_shape=[jax.ShapeDtypeStruct((S, 128), BF16), jax.ShapeDtypeStruct((1, 128), F32)],
        scratch_shapes=[pltpu.VMEM((1, 128), F32)],
        compiler_params=_cp("arbitrary"),
    )(dc, zf, bf, triu)


def _pair_rsqrt(x, lo_half):
    x2 = x * x
    s0 = jnp.sum(jnp.where(lo_half, x2, 0.0), axis=1, keepdims=True)
    s1 = jnp.sum(jnp.where(lo_half, 0.0, x2), axis=1, keepdims=True)
    return lax.rsqrt(jnp.where(lo_half, s0, s1) * (1.0 / HEAD_DIM) + EPS)


def _pair_mean(x, lo_half):
    s0 = jnp.sum(jnp.where(lo_half, x, 0.0), axis=1, keepdims=True)
    s1 = jnp.sum(jnp.where(lo_half, 0.0, x), axis=1, keepdims=True)
    return jnp.where(lo_half, s0, s1) * (1.0 / HEAD_DIM)


def _prep_fwd(zm, c, qkg, rope_c, rope_a, rope_b):
    S = zm.shape[0]
    shp = jax.ShapeDtypeStruct((N_HEADS, S, HEAD_PAD), BF16)

    def body(z_ref, c_ref, g_ref, rc_ref, ra_ref, rb_ref, qa_ref, ka_ref, va_ref, qb_ref, kb_ref, vb_ref):
        lane = lax.broadcasted_iota(jnp.int32, (TOK_T, HEAD_PAD), 1)
        lo_half = lane < HEAD_DIM
        aug = (lane >= HEAD_DIM) & (lane < HEAD_DIM + 3)
        q_pad = jnp.where(aug, -1.0, 0.0)
        cs = c_ref[...]
        rc, ra, rb = rc_ref[...], ra_ref[...], rb_ref[...]

        def norm(col, gi):
            x = z_ref[:, col:col + HEAD_PAD]
            return x * _pair_rsqrt(x, lo_half) * g_ref[gi:gi + 1, :]

        def rope(y):
            return y * rc + pltpu.roll(y, HEAD_PAD - ROPE_HALF, 1) * ra + pltpu.roll(y, ROPE_HALF, 1) * rb

        def put(ref, pi, y, pad_even, pad_odd):
            ref[2 * pi] = jnp.where(lo_half, y, pad_even).astype(BF16)
            ref[2 * pi + 1] = jnp.where(lo_half, pltpu.roll(y, HEAD_DIM, 1), pad_odd).astype(BF16)

        def k_pad(h):
            ch = cs[:, h:h + 1]
            hi = ch.astype(BF16).astype(F32)
            mid = (ch - hi).astype(BF16).astype(F32)
            lo = ch - hi - mid
            ones = jnp.where(lane == HEAD_DIM + 3, 1.0, 0.0)
            return jnp.where(lane == HEAD_DIM, hi, jnp.where(lane == HEAD_DIM + 1, mid,
                                                              jnp.where(lane == HEAD_DIM + 2, lo, ones)))

        for pi in range(N_HEADS // 2):
            col = HEAD_PAD * pi
            put(qa_ref, pi, norm(col, 0) * SCALE, q_pad, q_pad)
            put(ka_ref, pi, norm(D_BRANCH + col, 1), k_pad(2 * pi), k_pad(2 * pi + 1))
            put(va_ref, pi, z_ref[:, 2 * D_BRANCH + col:2 * D_BRANCH + col + HEAD_PAD], 0.0, 0.0)
            put(qb_ref, pi, rope(norm(4 * D_BRANCH + col, 2)) * SCALE, 0.0, 0.0)
            put(kb_ref, pi, rope(norm(5 * D_BRANCH + col, 3)), 0.0, 0.0)
            put(vb_ref, pi, z_ref[:, 6 * D_BRANCH + col:6 * D_BRANCH + col + HEAD_PAD], 0.0, 0.0)

    tok = lambda w: pl.BlockSpec((TOK_T, w), lambda i: (i, 0))
    head = pl.BlockSpec((N_HEADS, TOK_T, HEAD_PAD), lambda i: (0, i, 0))
    return pl.pallas_call(
        body, name="prep_fwd", grid=(S // TOK_T,),
        in_specs=[tok(N_MAIN), tok(128), pl.BlockSpec((4, 128), lambda i: (0, 0)), tok(128), tok(128), tok(128)],
        out_specs=[head] * 6, out_shape=[shp] * 6,
        compiler_params=_cp("parallel"),
    )(zm, c, qkg, rope_c, rope_a, rope_b)


def _pair(ref, pi, lo_half):
    return jnp.where(lo_half, ref[2 * pi], pltpu.roll(ref[2 * pi + 1], HEAD_DIM, 1))


def _mid_fwd(oa, ob, zm, h0, p, w_out, w_pg, w_ple, g2):
    S = h0.shape[0]

    def body(oa_ref, ob_ref, ga_ref, gb_ref, h0_ref, p_ref, wo_ref, wg_ref, wp_ref, g2_ref,
             y_ref, h1_ref, h2_ref, u2_ref, e_ref, gate_ref):
        lane = lax.broadcasted_iota(jnp.int32, (TOK_T, HEAD_PAD), 1)
        lo_half = lane < HEAD_DIM
        parts = []
        for o_ref, g_ref in ((oa_ref, ga_ref), (ob_ref, gb_ref)):
            for pi in range(N_HEADS // 2):
                g = g_ref[:, HEAD_PAD * pi:HEAD_PAD * (pi + 1)]
                parts.append((_pair(o_ref, pi, lo_half) * (g * _sigmoid(g))).astype(BF16))
        y = jnp.concatenate(parts, axis=1)
        y_ref[...] = y
        h1 = h0_ref[...] + _dot(y, wo_ref[...])
        h1_ref[...] = h1
        r = lax.rsqrt(jnp.mean(h1 * h1, axis=-1, keepdims=True) + EPS)
        u2 = (h1 * r * g2_ref[...]).astype(BF16)
        u2_ref[...] = u2
        gate = _sigmoid(_dot(u2, wg_ref[...]))
        e = _dot(p_ref[...].astype(BF16), wp_ref[...])
        e_ref[...] = e
        gate_ref[...] = gate
        h2_ref[...] = h1 + e * gate

    tok = lambda w: pl.BlockSpec((TOK_T, w), lambda i: (i, 0))
    head = pl.BlockSpec((N_HEADS, TOK_T, HEAD_PAD), lambda i: (0, i, 0))
    full = lambda a, b: pl.BlockSpec((a, b), lambda i: (0, 0))
    act = lambda dt: jax.ShapeDtypeStruct((S, D_MODEL), dt)
    return pl.pallas_call(
        body, name="mid_fwd", grid=(S // TOK_T,),
        in_specs=[head, head,
                  pl.BlockSpec((TOK_T, D_BRANCH), lambda i: (i, 3)), pl.BlockSpec((TOK_T, D_BRANCH), lambda i: (i, 7)),
                  tok(D_MODEL), tok(PLE_DIM), full(D_MODEL, D_MODEL), full(D_MODEL, D_MODEL),
                  full(PLE_DIM, D_MODEL), full(1, D_MODEL)],
        out_specs=[tok(D_MODEL)] * 6,
        out_shape=[act(BF16), act(F32), act(F32), act(BF16), act(F32), act(F32)],
        compiler_params=_cp("parallel"),
    )(oa, ob, zm, zm, h0, p, w_out, w_pg, w_ple, g2)


def _loss_fwd_bwd(y, t):
    S = y.shape[0]

    def body(y_ref, t_ref, dy_ref, loss_ref):
        @pl.when(pl.program_id(0) == 0)
        def _():
            loss_ref[...] = jnp.zeros_like(loss_ref)

        err = y_ref[...] - t_ref[...]
        dy_ref[...] = err * (1.0 / D_MODEL)
        part = jnp.sum(jnp.sum(err * err, axis=1, keepdims=True), axis=0, keepdims=True)
        loss_ref[...] += part * (0.5 / D_MODEL)

    tok = pl.BlockSpec((TOK_T, D_MODEL), lambda i: (i, 0))
    return pl.pallas_call(
        body, name="loss", grid=(S // TOK_T,),
        in_specs=[tok, tok], out_specs=[tok, pl.BlockSpec((8, 128), lambda i: (0, 0))],
        out_shape=[jax.ShapeDtypeStruct((S, D_MODEL), F32), jax.ShapeDtypeStruct((8, 128), F32)],
        compiler_params=_cp("arbitrary"),
    )(y, t)


def _bias_tables(full_range):
    T = ATT_T
    nb = 1 if full_range else DILATED_PATTERNS[-1][0] // T + 1
    r = lax.broadcasted_iota(jnp.int32, (nb, T, T), 2)
    c = lax.broadcasted_iota(jnp.int32, (nb, T, T), 1)
    b = lax.broadcasted_iota(jnp.int32, (nb, T, T), 0)
    delta = T * b + r - c
    if full_range:
        bias = jnp.where(delta >= 0, 0.0, NEG).astype(F32)
    else:
        mult = jnp.zeros((nb, T, T), F32)
        for window, dil in DILATED_PATTERNS:
            ok = (delta >= 0) & (delta <= window) & (delta % dil == 0)
            mult = mult + ok.astype(F32)
        bias = jnp.where(mult > 0, jnp.log(jnp.maximum(mult, 1.0)), NEG).astype(F32)
    return bias


def _attn_fwd(q, k, v, table_t, full_range, name):
    H, S, _ = q.shape
    T = ATT_T
    nb = table_t.shape[0]
    HB = ATT_FWD_HEADS
    KC = ATT_CHUNK
    chunks = [slice(c, c + KC) for c in range(0, T, KC)]
    fold = lambda x, op: functools.reduce(op, [x[r:r + 8] for r in range(0, KC, 8)])

    def body(q_ref, k_ref, v_ref, tab_ref, o_ref, lse_ref, *scratch):
        st_refs, pt_refs, acc_refs = scratch[:HB], scratch[HB:2 * HB], scratch[2 * HB:]
        i = pl.program_id(1)
        rows = lambda j: pl.ds(pl.multiple_of(j * T, T), T)

        def scores(hh, j):
            st_refs[hh][...] = _dot_nt(k_ref[hh, rows(j), :], q_ref[hh])

        def block(j, b, nxt, stats):
            out = []
            for hh, (m, l) in enumerate(stats):
                st_ref, pt_ref, acc_ref = st_refs[hh], pt_refs[hh], acc_refs[hh]
                mx = None
                for ch in chunks:
                    x = st_ref[ch, :]
                    if b is not None:
                        x = x + tab_ref[b, ch, :]
                        st_ref[ch, :] = x
                    x = fold(x, jnp.maximum)
                    mx = x if mx is None else jnp.maximum(mx, x)
                m_new = jnp.maximum(m, jnp.max(mx, axis=0, keepdims=True))
                alpha = jnp.exp(m - m_new)
                ls = None
                for ch in chunks:
                    pc = jnp.exp(st_ref[ch, :] - m_new)
                    pt_ref[ch, :] = pc.astype(BF16)
                    pc = fold(pc, jnp.add)
                    ls = pc if ls is None else ls + pc
                if nxt is not None:
                    scores(hh, nxt)
                acc_ref[...] = alpha * acc_ref[...] + _dot_tn(v_ref[hh, rows(j), :], pt_ref[...])
                out.append((m_new, alpha * l + jnp.sum(ls, axis=0, keepdims=True)))
            return tuple(out)

        lo = 0 if full_range else jnp.maximum(i - (nb - 1), 0)
        for hh in range(HB):
            acc_refs[hh][...] = jnp.zeros_like(acc_refs[hh])
            scores(hh, lo)
        stats = lax.fori_loop(lo, i, lambda j, st: block(j, None if full_range else i - j, j + 1, st),
                              ((jnp.full((1, T), NEG, F32), jnp.zeros((1, T), F32)),) * HB)
        stats = block(i, 0, None, stats)
        for hh, (m, l) in enumerate(stats):
            o_ref[hh] = (acc_refs[hh][...] * (1.0 / l)).T
            lse_ref[hh, 0] = m + jnp.log(l)

    return pl.pallas_call(
        body, name=name, grid=(H // HB, S // T),
        in_specs=[pl.BlockSpec((HB, T, HEAD_PAD), lambda h, i: (h, i, 0)),
                  pl.BlockSpec((HB, S, HEAD_PAD), lambda h, i: (h, 0, 0)),
                  pl.BlockSpec((HB, S, HEAD_PAD), lambda h, i: (h, 0, 0)),
                  pl.BlockSpec((nb, T, T), lambda h, i: (0, 0, 0))],
        out_specs=[pl.BlockSpec((HB, T, HEAD_PAD), lambda h, i: (h, i, 0)),
                   pl.BlockSpec((HB, 1, 1, T), lambda h, i: (h, i, 0, 0))],
        out_shape=[jax.ShapeDtypeStruct((H, S, HEAD_PAD), F32), jax.ShapeDtypeStruct((H, S // T, 1, T), F32)],
        scratch_shapes=([pltpu.VMEM((T, T), F32)] * HB + [pltpu.VMEM((T, T), BF16)] * HB
                        + [pltpu.VMEM((HEAD_PAD, T), F32)] * HB),
        compiler_params=_cp("parallel", "arbitrary"),
    )(q, k, v, table_t)


def _attn_bwd(q, k, v, do, lse, dd, table_t, full_range, name):
    H, S, _ = q.shape
    T = ATT_T
    nq = S // T
    nb = table_t.shape[0]

    def body(q_ref, do_ref, lse_ref, dd_ref, k_ref, v_ref, tab_ref, dq_ref, dk_ref, dv_ref):
        j = pl.program_id(1)

        @pl.when(j == 0)
        def _():
            dq_ref[...] = jnp.zeros_like(dq_ref)

        kb = k_ref[0]
        vb = v_ref[0]

        def step(i, bias, carry):
            dk, dv = carry
            rows = pl.ds(pl.multiple_of(i * T, T), T)
            qb = q_ref[0, rows, :]
            dob = do_ref[0, rows, :]
            st = _dot_nt(kb, qb)
            if bias is not None:
                st = st + bias
            pt = jnp.exp(st - lse_ref[0, i])
            dpt = _dot_nt(vb, dob)
            dst = (pt * (dpt - dd_ref[0, i])).astype(BF16)
            dv = dv + _dot(pt.astype(BF16), dob)
            dk = dk + _dot(dst, qb)
            dq_ref[0, rows, :] += _dot_tn(dst, kb)
            return dk, dv

        zero = jnp.zeros((T, HEAD_PAD), F32)
        carry = step(j, tab_ref[0], (zero, zero))
        if full_range:
            carry = lax.fori_loop(j + 1, nq, lambda i, cr: step(i, None, cr), carry)
        else:
            carry = lax.fori_loop(j + 1, jnp.minimum(j + nb, nq), lambda i, cr: step(i, tab_ref[i - j], cr), carry)
        dk_ref[0] = carry[0]
        dv_ref[0] = carry[1]

    per_head = pl.BlockSpec((1, S, HEAD_PAD), lambda h, j: (h, 0, 0))
    rows = pl.BlockSpec((1, nq, 1, T), lambda h, j: (h, 0, 0, 0))
    blk = pl.BlockSpec((1, T, HEAD_PAD), lambda h, j: (h, j, 0))
    shp = jax.ShapeDtypeStruct((H, S, HEAD_PAD), F32)
    return pl.pallas_call(
        body, name=name, grid=(H, nq),
        in_specs=[per_head, per_head, rows, rows, blk, blk, pl.BlockSpec((nb, T, T), lambda h, j: (0, 0, 0))],
        out_specs=[per_head, blk, blk], out_shape=[shp, shp, shp],
        compiler_params=_cp("parallel", "arbitrary"),
    )(q, do, lse, dd, k, v, table_t)


def _mid_bwd(dh2, h1, e, gate, g2, w_pg_t, w_out_t, oa, ob, zm):
    S = dh2.shape[0]

    def body(dh2_ref, h1_ref, e_ref, gate_ref, g2_ref, wg_ref, wo_ref, oa_ref, ob_ref, ga_ref, gb_ref,
             dh1_ref, dh1b_ref, de_ref, dpre_ref, doa_ref, dob_ref, dga_ref, dgb_ref, dda_ref, ddb_ref, dg2_ref):
        @pl.when(pl.program_id(0) == 0)
        def _():
            dg2_ref[...] = jnp.zeros_like(dg2_ref)

        lane = lax.broadcasted_iota(jnp.int32, (TOK_T, HEAD_PAD), 1)
        lo_half = lane < HEAD_DIM
        dh2 = dh2_ref[...]
        gate = gate_ref[...]
        de_ref[...] = (dh2 * gate).astype(BF16)
        dpre = (dh2 * e_ref[...] * gate * (1.0 - gate)).astype(BF16)
        dpre_ref[...] = dpre
        du2 = _dot(dpre, wg_ref[...])
        h1 = h1_ref[...]
        r = lax.rsqrt(jnp.mean(h1 * h1, axis=-1, keepdims=True) + EPS)
        xh = h1 * r
        a = du2 * g2_ref[...]
        dh1 = dh2 + r * (a - xh * jnp.mean(a * xh, axis=-1, keepdims=True))
        dg2_ref[...] += jnp.sum(du2 * xh, axis=0, keepdims=True)
        dh1_ref[...] = dh1
        dh1b = dh1.astype(BF16)
        dh1b_ref[...] = dh1b
        dy = _dot(dh1b, wo_ref[...])
        for bi, (o_ref, g_ref, do_ref, dg_ref, dd_ref) in enumerate(
                ((oa_ref, ga_ref, doa_ref, dga_ref, dda_ref), (ob_ref, gb_ref, dob_ref, dgb_ref, ddb_ref))):
            for pi in range(N_HEADS // 2):
                col = bi * D_BRANCH + HEAD_PAD * pi
                dyp = dy[:, col:col + HEAD_PAD]
                g = g_ref[:, HEAD_PAD * pi:HEAD_PAD * (pi + 1)]
                sg = _sigmoid(g)
                dg_ref[:, HEAD_PAD * pi:HEAD_PAD * (pi + 1)] = (
                    dyp * _pair(o_ref, pi, lo_half) * (sg * (1.0 + g * (1.0 - sg)))).astype(BF16)
                dop = dyp * (g * sg)
                for hh, d_head in ((2 * pi, dop), (2 * pi + 1, pltpu.roll(dop, HEAD_DIM, 1))):
                    d_head = jnp.where(lo_half, d_head, 0.0)
                    do_ref[hh] = d_head.astype(BF16)
                    dd_ref[hh] = jnp.sum(d_head * o_ref[hh], axis=1, keepdims=True)

    tok = lambda w: pl.BlockSpec((TOK_T, w), lambda i: (i, 0))
    head = pl.BlockSpec((N_HEADS, TOK_T, HEAD_PAD), lambda i: (0, i, 0))
    head1 = pl.BlockSpec((N_HEADS, TOK_T, 1), lambda i: (0, i, 0))
    full = lambda a, b: pl.BlockSpec((a, b), lambda i: (0, 0))
    act = lambda w, dt: jax.ShapeDtypeStruct((S, w), dt)
    hshape = lambda w, dt: jax.ShapeDtypeStruct((N_HEADS, S, w), dt)
    return pl.pallas_call(
        body, name="mid_bwd", grid=(S // TOK_T,),
        in_specs=[tok(D_MODEL)] * 4 + [full(1, D_MODEL), full(D_MODEL, D_MODEL), full(D_MODEL, D_MODEL), head, head,
                                      pl.BlockSpec((TOK_T, D_BRANCH), lambda i: (i, 3)),
                                      pl.BlockSpec((TOK_T, D_BRANCH), lambda i: (i, 7))],
        out_specs=[tok(D_MODEL)] * 4 + [head, head, tok(D_BRANCH), tok(D_BRANCH), head1, head1, full(1, D_MODEL)],
        out_shape=[act(D_MODEL, F32), act(D_MODEL, BF16), act(D_MODEL, BF16), act(D_MODEL, BF16),
                   hshape(HEAD_PAD, BF16), hshape(HEAD_PAD, BF16), act(D_BRANCH, BF16), act(D_BRANCH, BF16),
                   hshape(1, F32), hshape(1, F32), jax.ShapeDtypeStruct((1, D_MODEL), F32)],
        compiler_params=_cp("arbitrary"),
    )(dh2, h1, e, gate, g2, w_pg_t, w_out_t, oa, ob, zm, zm)


def _prep_bwd(dqa, dka, dva, dqb, dkb, dvb, zm, qkg, rope_c, rope_a, rope_b, dga, dgb):
    S = zm.shape[0]

    def body(dqa_ref, dka_ref, dva_ref, dqb_ref, dkb_ref, dvb_ref, z_ref, g_ref, rc_ref, ra_ref, rb_ref,
             dga_ref, dgb_ref, dz_ref, dc_ref, dqkg_ref):
        @pl.when(pl.program_id(0) == 0)
        def _():
            dqkg_ref[...] = jnp.zeros_like(dqkg_ref)

        lane = lax.broadcasted_iota(jnp.int32, (TOK_T, HEAD_PAD), 1)
        lo_half = lane < HEAD_DIM
        rc, ra, rb = rc_ref[...], ra_ref[...], rb_ref[...]

        def unrope(dy):
            return dy * rc + pltpu.roll(dy * ra, ROPE_HALF, 1) + pltpu.roll(dy * rb, HEAD_PAD - ROPE_HALF, 1)

        def norm_bwd(col, gi, dy):
            x = z_ref[:, col:col + HEAD_PAD]
            r = _pair_rsqrt(x, lo_half)
            xh = x * r
            dqkg_ref[gi:gi + 1, :] += jnp.sum(dy * xh, axis=0, keepdims=True)
            a = dy * g_ref[gi:gi + 1, :]
            dz_ref[:, col:col + HEAD_PAD] = (r * (a - xh * _pair_mean(a * xh, lo_half))).astype(BF16)

        dc = jnp.zeros((TOK_T, HEAD_PAD), F32)
        for pi in range(N_HEADS // 2):
            col = HEAD_PAD * pi
            norm_bwd(col, 0, _pair(dqa_ref, pi, lo_half) * SCALE)
            norm_bwd(D_BRANCH + col, 1, _pair(dka_ref, pi, lo_half))
            dz_ref[:, 2 * D_BRANCH + col:2 * D_BRANCH + col + HEAD_PAD] = _pair(dva_ref, pi, lo_half).astype(BF16)
            norm_bwd(4 * D_BRANCH + col, 2, unrope(_pair(dqb_ref, pi, lo_half) * SCALE))
            norm_bwd(5 * D_BRANCH + col, 3, unrope(_pair(dkb_ref, pi, lo_half)))
            dz_ref[:, 6 * D_BRANCH + col:6 * D_BRANCH + col + HEAD_PAD] = _pair(dvb_ref, pi, lo_half).astype(BF16)
            for hh in (2 * pi, 2 * pi + 1):
                dch = dka_ref[hh][:, HEAD_DIM:HEAD_DIM + 1] + dqa_ref[hh][:, HEAD_DIM + 3:HEAD_DIM + 4]
                dc = dc + jnp.where(lane == hh, dch, 0.0)
        dz_ref[:, 3 * D_BRANCH:4 * D_BRANCH] = dga_ref[...]
        dz_ref[:, 7 * D_BRANCH:8 * D_BRANCH] = dgb_ref[...]
        dc_ref[...] = dc

    tok = lambda w: pl.BlockSpec((TOK_T, w), lambda i: (i, 0))
    head = pl.BlockSpec((N_HEADS, TOK_T, HEAD_PAD), lambda i: (0, i, 0))
    return pl.pallas_call(
        body, name="prep_bwd", grid=(S // TOK_T,),
        in_specs=[head] * 6 + [tok(N_MAIN), pl.BlockSpec((4, 128), lambda i: (0, 0)), tok(128), tok(128), tok(128),
                               tok(D_BRANCH), tok(D_BRANCH)],
        out_specs=[tok(N_MAIN), tok(128), pl.BlockSpec((4, 128), lambda i: (0, 0))],
        out_shape=[jax.ShapeDtypeStruct((S, N_MAIN), BF16), jax.ShapeDtypeStruct((S, 128), F32),
                   jax.ShapeDtypeStruct((4, 128), F32)],
        compiler_params=_cp("arbitrary"),
    )(dqa, dka, dva, dqb, dkb, dvb, zm, qkg, rope_c, rope_a, rope_b, dga, dgb)


def _inproj_bwd(dzm, dzf, wm_t, wf_t, h0, dh1, g):
    S = h0.shape[0]

    def body(dzm_ref, dzf_ref, wm_ref, wf_ref, h_ref, dh1_ref, g_ref, dh0_ref, dg_ref):
        @pl.when(pl.program_id(0) == 0)
        def _():
            dg_ref[...] = jnp.zeros_like(dg_ref)

        du = _dot(dzm_ref[...], wm_ref[...]) + _dot(dzf_ref[...], wf_ref[...])
        x = h_ref[...]
        r = lax.rsqrt(jnp.mean(x * x, axis=-1, keepdims=True) + EPS)
        xh = x * r
        a = du * g_ref[...]
        dh0_ref[...] = dh1_ref[...] + r * (a - xh * jnp.mean(a * xh, axis=-1, keepdims=True))
        dg_ref[...] += jnp.sum(du * xh, axis=0, keepdims=True)

    tok = lambda w: pl.BlockSpec((TOK_T, w), lambda i: (i, 0))
    full = lambda a, b: pl.BlockSpec((a, b), lambda i: (0, 0))
    return pl.pallas_call(
        body, name="inproj_bwd", grid=(S // TOK_T,),
        in_specs=[tok(N_MAIN), tok(128), full(N_MAIN, D_MODEL), full(128, D_MODEL), tok(D_MODEL), tok(D_MODEL),
                  full(1, D_MODEL)],
        out_specs=[tok(D_MODEL), full(1, D_MODEL)],
        out_shape=[jax.ShapeDtypeStruct((S, D_MODEL), F32), jax.ShapeDtypeStruct((1, D_MODEL), F32)],
        compiler_params=_cp("arbitrary"),
    )(dzm, dzf, wm_t, wf_t, h0, dh1, g)


def _wgrad(a, b, name):
    S, M = a.shape
    N = b.shape[1]
    tn = min(N, 2048)
    ts = 512

    def body(a_ref, b_ref, o_ref):
        @pl.when(pl.program_id(1) == 0)
        def _():
            o_ref[...] = jnp.zeros_like(o_ref)

        o_ref[...] += _dot_tn(a_ref[...].astype(BF16), b_ref[...])

    return pl.pallas_call(
        body, name=name, grid=(N // tn, S // ts),
        in_specs=[pl.BlockSpec((ts, M), lambda n, s: (s, 0)), pl.BlockSpec((ts, tn), lambda n, s: (s, n))],
        out_specs=pl.BlockSpec((M, tn), lambda n, s: (0, n)),
        out_shape=jax.ShapeDtypeStruct((M, N), F32),
        compiler_params=_cp("parallel", "arbitrary"),
    )(a, b)


def _rope_tables(positions):
    inv_freq = ROPE_THETA ** (-jnp.arange(ROPE_HALF, dtype=F32) / ROPE_HALF)
    ang = positions.astype(F32)[:, None] * inv_freq
    cos, sin = jnp.cos(ang), jnp.sin(ang)
    S = positions.shape[0]
    one, zero = jnp.ones((S, HEAD_DIM - 2 * ROPE_HALF), F32), jnp.zeros((S, HEAD_DIM - 2 * ROPE_HALF), F32)
    z8 = jnp.zeros((S, ROPE_HALF), F32)
    rc = jnp.concatenate([cos, cos, one], axis=1)
    ra = jnp.concatenate([-sin, z8, zero], axis=1)
    rb = jnp.concatenate([z8, sin, zero], axis=1)
    return tuple(jnp.tile(t, (1, 2)) for t in (rc, ra, rb))


def _layer_weights(w_in, w_out, w_ple, w_pg):
    w_in = w_in.astype(BF16)
    wm = jnp.concatenate([w_in[:, :4 * D_BRANCH], w_in[:, 4 * D_BRANCH + N_HEADS:]], axis=1)
    wf = jnp.pad(w_in[:, 4 * D_BRANCH:4 * D_BRANCH + N_HEADS], ((0, 0), (0, 128 - N_HEADS)))
    w_out, w_ple, w_pg = w_out.astype(BF16), w_ple.astype(BF16), w_pg.astype(BF16)
    return dict(wm=wm, wf=wf, wm_t=wm.T, wf_t=wf.T, w_out=w_out, w_out_t=w_out.T, w_ple=w_ple, w_pg=w_pg,
                w_pg_t=w_pg.T)


def _row(v, width=128):
    v = v.reshape(1, -1).astype(F32)
    return jnp.pad(v, ((0, 0), (0, width - v.shape[1])))


def _layer_fwd(h0, p, rope, tabs, w, norm_g, b_f, qk_g, ple_g):
    S = h0.shape[0]
    g1 = norm_g.reshape(1, D_MODEL)
    g2 = ple_g.reshape(1, D_MODEL)
    qkg = jnp.tile(qk_g, (1, 2))
    bf = _row(b_f)
    zm, zf, u = _inproj_fwd(h0, g1, w["wm"], w["wf"])
    c = _forget_cumsum(zf, bf, tabs["tril"])
    qa, ka, va, qb, kb, vb = _prep_fwd(zm, c, qkg, *rope)
    oa, lse_a = _attn_fwd(qa, ka, va, tabs["fox"], True, "fox_fwd")
    ob, lse_b = _attn_fwd(qb, kb, vb, tabs["dil"], False, "dil_fwd")
    y, h1, h2, u2, e, gate = _mid_fwd(oa, ob, zm, h0, p, w["w_out"], w["w_pg"], w["w_ple"], g2)
    saved = dict(h0=h0, p=p, zm=zm, zf=zf, u=u, qa=qa, ka=ka, va=va, qb=qb, kb=kb, vb=vb, oa=oa, ob=ob,
                 lse_a=lse_a, lse_b=lse_b, y=y, h1=h1, u2=u2, e=e, gate=gate, g1=g1, g2=g2, qkg=qkg, bf=bf)
    return h2, saved


def _layer_bwd(dh2, sv, rope, tabs, w):
    S = dh2.shape[0]
    nq = S // ATT_T
    rows = lambda a: a.reshape(N_HEADS, nq, 1, ATT_T)
    (dh1, dh1b, de, dpre, doa, dob, dga, dgb, dda, ddb, dg2) = _mid_bwd(
        dh2, sv["h1"], sv["e"], sv["gate"], sv["g2"], w["w_pg_t"], w["w_out_t"], sv["oa"], sv["ob"], sv["zm"])
    dqa, dka, dva = _attn_bwd(sv["qa"], sv["ka"], sv["va"], doa, sv["lse_a"], rows(dda), tabs["fox"], True,
                              "fox_bwd")
    dqb, dkb, dvb = _attn_bwd(sv["qb"], sv["kb"], sv["vb"], dob, sv["lse_b"], rows(ddb), tabs["dil"], False,
                              "dil_bwd")
    dzm, dc, dqkg = _prep_bwd(dqa, dka, dva, dqb, dkb, dvb, sv["zm"], sv["qkg"], *rope, dga, dgb)
    dzf, dbf = _forget_bwd(dc, sv["zf"], sv["bf"], tabs["triu"])
    dh0, dg1 = _inproj_bwd(dzm, dzf, w["wm_t"], w["wf_t"], sv["h0"], dh1, sv["g1"])
    dwm = _wgrad(sv["u"], dzm, "wgrad_in")
    dwf = _wgrad(sv["u"], dzf, "wgrad_f")
    dw_in = jnp.concatenate([dwm[:, :4 * D_BRANCH], dwf[:, :N_HEADS], dwm[:, 4 * D_BRANCH:]], axis=1)
    grads = dict(
        norm_g=dg1.reshape(D_MODEL), w_in=dw_in, b_f=dbf[0, :N_HEADS],
        qk_norm_g=dqkg[:, :HEAD_DIM] + dqkg[:, HEAD_DIM:],
        w_out=_wgrad(sv["y"], dh1b, "wgrad_out"), w_ple=_wgrad(sv["p"], de, "wgrad_ple"),
        ple_norm_g=dg2.reshape(D_MODEL), w_ple_gate=_wgrad(sv["u2"], dpre, "wgrad_gate"))
    return dh0, grads


def _tables():
    T = CUM_T
    r = lax.broadcasted_iota(jnp.int32, (T, T), 0)
    c = lax.broadcasted_iota(jnp.int32, (T, T), 1)
    return dict(fox=_bias_tables(True), dil=_bias_tables(False),
                tril=(c <= r).astype(BF16), triu=(c >= r).astype(BF16))


def _local_step(x, p, positions, target, layers, small):
    rope = _rope_tables(positions)
    tabs = _tables()
    ws = [_layer_weights(*lw) for lw in layers]
    h = x
    saved = []
    for w, lp, sm in zip(ws, p, small):
        h, sv = _layer_fwd(h, lp, rope, tabs, w, *sm)
        saved.append(sv)
    dh, loss = _loss_fwd_bwd(h, target)
    grads = [None] * len(ws)
    for li in reversed(range(len(ws))):
        dh, grads[li] = _layer_bwd(dh, saved[li], rope, tabs, ws[li])
    return loss[0, 0], dh, grads


def _peers():
    x, y, c = lax.axis_index("x"), lax.axis_index("y"), lax.axis_index("c")
    me = 4 * x + 2 * y + c
    flip = lambda v, bit: 1 - v if bit else v
    return me, [(flip(x, k & 4), flip(y, k & 2), flip(c, k & 1)) for k in range(1, N_DEV)]


def _shard(ref, axis, d, size):
    return ref.at[(slice(None),) * axis + (pl.ds(pl.multiple_of(d * size, size), size),)]


def _exchange(name, arrays, out_shapes, src_of, dst_of):
    n = len(arrays)

    def body(*refs):
        ins, outs = refs[:n], refs[n:2 * n]
        send_sems, recv_sems, local_sems = refs[2 * n:]
        me, peers = _peers()
        local = [pltpu.make_async_copy(src_of(a, ins[a], me), dst_of(a, outs[a], me), local_sems.at[a])
                 for a in range(n)]
        for cp in local:
            cp.start()
        copies = [pltpu.make_async_remote_copy(
            src_ref=src_of(a, ins[a], 4 * px + 2 * py + pc), dst_ref=dst_of(a, outs[a], me),
            send_sem=send_sems.at[k, a], recv_sem=recv_sems.at[k, a],
            device_id=(px, py, pc), device_id_type=pl.DeviceIdType.MESH)
            for k, (px, py, pc) in enumerate(peers) for a in range(n)]
        for cp in copies:
            cp.start()
        for cp in copies:
            cp.wait()
        for cp in local:
            cp.wait()

    hbm = pl.BlockSpec(memory_space=pltpu.HBM)
    return pl.pallas_call(
        body, name=name, in_specs=[hbm] * n, out_specs=[hbm] * n, out_shape=out_shapes,
        scratch_shapes=[pltpu.SemaphoreType.DMA((N_DEV - 1, n)), pltpu.SemaphoreType.DMA((N_DEV - 1, n)),
                        pltpu.SemaphoreType.DMA((n,))],
    )(*arrays)


_SHARD_AXES = (None, (1, 128), (2, 128), (1, 128))


def _gather_weights(shards):
    full = [(N_DEV,) + shards[0].shape, (2, D_MODEL, D_MODEL), (2, PLE_DIM, D_MODEL), (2, D_MODEL, D_MODEL)]
    return _exchange(
        "gather_weights", shards, [jax.ShapeDtypeStruct(f, BF16) for f in full],
        src_of=lambda a, ref, peer: ref,
        dst_of=lambda a, ref, me: ref.at[me] if a == 0 else _shard(ref, _SHARD_AXES[a][0], me, _SHARD_AXES[a][1]))


def _exchange_grads(partials, small):
    shard_shapes = [partials[0].shape[1:], (2, 128, D_MODEL), (2, PLE_DIM, 128), (2, 128, D_MODEL), small.shape]

    def src_of(a, ref, peer):
        if a == 0:
            return ref.at[peer]
        return ref if a == len(partials) else _shard(ref, _SHARD_AXES[a][0], peer, _SHARD_AXES[a][1])

    return _exchange(
        "exchange_grads", list(partials) + [small], [jax.ShapeDtypeStruct((N_DEV,) + s, F32) for s in shard_shapes],
        src_of=src_of, dst_of=lambda a, ref, me: ref.at[me])


def _adamw(name, parts, w, m, v, rows):
    L, R, C = w.shape

    def body(p_ref, w_ref, m_ref, v_ref, g_ref, d_ref, nm_ref, nv_ref):
        g = p_ref[0, 0]
        for s in range(1, N_DEV):
            g = g + p_ref[s, 0]
        g_ref[0] = g
        nm = ADAM_B1 * m_ref[0] + (1.0 - ADAM_B1) * g
        nv = ADAM_B2 * v_ref[0] + (1.0 - ADAM_B2) * (g * g)
        nm_ref[0] = nm
        nv_ref[0] = nv
        m_hat = nm / (1.0 - ADAM_B1 ** ADAM_STEP)
        v_hat = nv / (1.0 - ADAM_B2 ** ADAM_STEP)
        d_ref[0] = -ADAM_LR * (m_hat / (jnp.sqrt(v_hat) + ADAM_EPS) + ADAM_WD * w_ref[0])

    blk = pl.BlockSpec((1, rows, C), lambda l, i: (l, i, 0))
    shp = jax.ShapeDtypeStruct((L, R, C), F32)
    return pl.pallas_call(
        body, name=name, grid=(L, R // rows),
        in_specs=[pl.BlockSpec((N_DEV, 1, rows, C), lambda l, i: (0, l, i, 0)), blk, blk, blk],
        out_specs=[blk] * 4, out_shape=[shp] * 4,
        compiler_params=_cp("parallel", "parallel"),
    )(parts, w, m, v)


SMALL_ROWS = 40
LOSS_ROW = 37


def _pack_small(norm_g, ple_g, qk_g, b_f, last_row):
    rows = lambda a: a.astype(F32).reshape(-1, 128)
    flat = jnp.concatenate([rows(norm_g), rows(ple_g), rows(qk_g), _row(b_f.reshape(-1)), last_row], axis=0)
    return jnp.pad(flat, ((0, SMALL_ROWS - flat.shape[0]), (0, 0)))


def _unpack_small(flat):
    return (flat[0:16].reshape(2, D_MODEL), flat[16:32].reshape(2, D_MODEL), flat[32:36].reshape(2, 4, HEAD_DIM),
            flat[36, :2 * N_HEADS].reshape(2, N_HEADS))


def kernel(x, p, positions, norm_g, w_in, b_f, qk_norm_g, w_out, w_ple, ple_norm_g, w_ple_gate, loss_target, m_norm_g, m_w_in, m_b_f, m_qk_norm_g, m_w_out, m_w_ple, m_ple_norm_g, m_w_ple_gate, v_norm_g, v_w_in, v_b_f, v_qk_norm_g, v_w_out, v_w_ple, v_ple_norm_g, v_w_ple_gate):
    g_in, full_out, full_ple, full_pg = _gather_weights([a.astype(BF16) for a in (w_in, w_out, w_ple, w_ple_gate)])
    full_in = g_in.transpose(1, 2, 0, 3).reshape(2, D_MODEL, N_IN)
    layers = [(full_in[l], full_out[l], full_ple[l], full_pg[l]) for l in range(2)]
    small = [(norm_g[l], b_f[l], qk_norm_g[l], ple_norm_g[l]) for l in range(2)]
    loss, dx, grads = _local_step(x[0], p[:, 0], positions[0], loss_target[0], layers, small)

    stack = lambda name: jnp.stack([gl[name] for gl in grads], axis=0)
    d_in = stack("w_in").reshape(2, D_MODEL, N_DEV, 513).transpose(2, 0, 1, 3)
    small_part = _pack_small(stack("norm_g"), stack("ple_norm_g"), stack("qk_norm_g"), stack("b_f"),
                             _row(loss.reshape(1)))
    r_in, r_out, r_ple, r_pg, r_small = _exchange_grads([d_in, stack("w_out"), stack("w_ple"), stack("w_ple_gate")],
                                                        small_part)

    zero_row = jnp.zeros((1, 128), F32)
    small_of = lambda ng, pg, qk, bf: _pack_small(ng, pg, qk, bf, zero_row)[None]
    outs = dict(
        w_in=_adamw("adamw_in", r_in, w_in, m_w_in, v_w_in, 256),
        w_out=_adamw("adamw_out", r_out, w_out, m_w_out, v_w_out, 128),
        w_ple=_adamw("adamw_ple", r_ple, w_ple, m_w_ple, v_w_ple, 256),
        w_pg=_adamw("adamw_gate", r_pg, w_ple_gate, m_w_ple_gate, v_w_ple_gate, 128),
        small=_adamw("adamw_small", r_small[:, None], small_of(norm_g, ple_norm_g, qk_norm_g, b_f),
                     small_of(m_norm_g, m_ple_norm_g, m_qk_norm_g, m_b_f),
                     small_of(v_norm_g, v_ple_norm_g, v_qk_norm_g, v_b_f), SMALL_ROWS))
    leaves = []
    for kind in range(4):
        ng, pg, qk, bf = _unpack_small(outs["small"][kind][0])
        leaves += [ng, outs["w_in"][kind], bf, qk, outs["w_out"][kind], outs["w_ple"][kind], pg, outs["w_pg"][kind]]
    return (outs["small"][0][0, LOSS_ROW, 0], dx[None], *leaves)
```

```python
import functools

import jax
import jax.numpy as jnp
from jax import lax
from jax.experimental import pallas as pl
from jax.experimental.pallas import tpu as pltpu

F32 = jnp.float32
BF16 = jnp.bfloat16

D_MODEL = 1024
HEAD_DIM = 64
N_HEADS = 8
HEAD_PAD = 128
D_BRANCH = N_HEADS * HEAD_DIM
N_MAIN = 8 * D_BRANCH
N_IN = N_MAIN + N_HEADS
PLE_DIM = 256
ROPE_THETA = 500000.0
ROPE_HALF = 8
EPS = 1e-6
NEG = -1e30
SCALE = HEAD_DIM ** -0.5
DILATED_PATTERNS = ((128, 1), (512, 4), (2048, 16))
N_DEV = 8
W_IN_SHARD = N_IN // N_DEV
W_IN_ROWS = 2 * D_MODEL * W_IN_SHARD // 128
W_IN_TILE = W_IN_ROWS // 19

ADAM_LR = 0.001
ADAM_B1 = 0.9
ADAM_B2 = 0.999
ADAM_EPS = 1e-08
ADAM_WD = 0.01
ADAM_STEP = 10

ATT_T = 512
ATT_FWD_HEADS = 2
ATT_CHUNK = 32
TOK_T = 256
CUM_T = 512
VMEM_LIMIT = 56 * 1024 * 1024


def _cp(*sem):
    return pltpu.CompilerParams(dimension_semantics=sem, vmem_limit_bytes=VMEM_LIMIT)


def _sigmoid(x):
    return 1.0 / (1.0 + jnp.exp(-x))


def _split3(x):
    hi = x.astype(BF16)
    r1 = x - hi.astype(F32)
    mid = r1.astype(BF16)
    lo = (r1 - mid.astype(F32)).astype(BF16)
    return hi, mid, lo


def _dot(a, b):
    return jnp.dot(a, b, preferred_element_type=F32)


def _dot_nt(a, b):
    return lax.dot_general(a, b, (((1,), (1,)), ((), ())), preferred_element_type=F32)


def _dot_tn(a, b):
    return lax.dot_general(a, b, (((0,), (0,)), ((), ())), preferred_element_type=F32)


def _inproj_fwd(h, g, wm, wf):
    S = h.shape[0]

    def body(h_ref, g_ref, wm_ref, wf_ref, zm_ref, zf_ref, u_ref):
        x = h_ref[...]
        r = lax.rsqrt(jnp.mean(x * x, axis=-1, keepdims=True) + EPS)
        u = (x * r * g_ref[...]).astype(BF16)
        u_ref[...] = u
        zm_ref[...] = _dot(u, wm_ref[...])
        zf_ref[...] = _dot(u, wf_ref[...])

    return pl.pallas_call(
        body, name="inproj_fwd", grid=(S // TOK_T,),
        in_specs=[pl.BlockSpec((TOK_T, D_MODEL), lambda i: (i, 0)),
                  pl.BlockSpec((1, D_MODEL), lambda i: (0, 0)),
                  pl.BlockSpec((D_MODEL, N_MAIN), lambda i: (0, 0)),
                  pl.BlockSpec((D_MODEL, 128), lambda i: (0, 0))],
        out_specs=[pl.BlockSpec((TOK_T, N_MAIN), lambda i: (i, 0)),
                   pl.BlockSpec((TOK_T, 128), lambda i: (i, 0)),
                   pl.BlockSpec((TOK_T, D_MODEL), lambda i: (i, 0))],
        out_shape=[jax.ShapeDtypeStruct((S, N_MAIN), F32), jax.ShapeDtypeStruct((S, 128), F32),
                   jax.ShapeDtypeStruct((S, D_MODEL), BF16)],
        compiler_params=_cp("parallel"),
    )(h, g, wm, wf)


def _log_sigmoid(x):
    return jnp.minimum(x, 0.0) - jnp.log(1.0 + jnp.exp(-jnp.abs(x)))


def _forget_cumsum(zf, bf, tri):
    S = zf.shape[0]

    def body(zf_ref, b_ref, tri_ref, c_ref, carry):
        @pl.when(pl.program_id(0) == 0)
        def _():
            carry[...] = jnp.zeros_like(carry)

        lf = _log_sigmoid(zf_ref[...] + b_ref[...])
        hi, mid, lo = _split3(lf)
        t = tri_ref[...]
        cs = _dot(t, hi) + _dot(t, mid) + _dot(t, lo) + carry[...]
        c_ref[...] = cs
        carry[...] = cs[CUM_T - 1:CUM_T, :]

    return pl.pallas_call(
        body, name="forget_cumsum", grid=(S // CUM_T,),
        in_specs=[pl.BlockSpec((CUM_T, 128), lambda i: (i, 0)),
                  pl.BlockSpec((1, 128), lambda i: (0, 0)),
                  pl.BlockSpec((CUM_T, CUM_T), lambda i: (0, 0))],
        out_specs=pl.BlockSpec((CUM_T, 128), lambda i: (i, 0)),
        out_shape=jax.ShapeDtypeStruct((S, 128), F32),
        scratch_shapes=[pltpu.VMEM((1, 128), F32)],
        compiler_params=_cp("arbitrary"),
    )(zf, bf, tri)


def _forget_bwd(dc, zf, bf, triu):
    S = zf.shape[0]
    n = S // CUM_T

    def body(dc_ref, zf_ref, b_ref, tri_ref, dzf_ref, db_ref, carry):
        @pl.when(pl.program_id(0) == 0)
        def _():
            carry[...] = jnp.zeros_like(carry)
            db_ref[...] = jnp.zeros_like(db_ref)

        hi, mid, lo = _split3(dc_ref[...])
        t = tri_ref[...]
        dlf = _dot(t, hi) + _dot(t, mid) + _dot(t, lo) + carry[...]
        carry[...] = dlf[0:1, :]
        dfa = dlf * (1.0 - _sigmoid(zf_ref[...] + b_ref[...]))
        dzf_ref[...] = dfa.astype(BF16)
        db_ref[...] += jnp.sum(dfa, axis=0, keepdims=True)

    return pl.pallas_call(
        body, name="forget_bwd", grid=(n,),
        in_specs=[pl.BlockSpec((CUM_T, 128), lambda i: (n - 1 - i, 0)),
                  pl.BlockSpec((CUM_T, 128), lambda i: (n - 1 - i, 0)),
                  pl.BlockSpec((1, 128), lambda i: (0, 0)),
                  pl.BlockSpec((CUM_T, CUM_T), lambda i: (0, 0))],
        out_specs=[pl.BlockSpec((CUM_T, 128), lambda i: (n - 1 - i, 0)),
                   pl.BlockSpec((1, 128), lambda i: (0, 0))],
        out_shape=[jax.ShapeDtypeStruct((S, 128), BF16), jax.ShapeDtypeStruct((1, 128), F32)],
        scratch_shapes=[pltpu.VMEM((1, 128), F32)],
        compiler_params=_cp("arbitrary"),
    )(dc, zf, bf, triu)


def _pair_rsqrt(x, lo_half):
    x2 = x * x
    s0 = jnp.sum(jnp.where(lo_half, x2, 0.0), axis=1, keepdims=True)
    s1 = jnp.sum(jnp.where(lo_half, 0.0, x2), axis=1, keepdims=True)
    return lax.rsqrt(jnp.where(lo_half, s0, s1) * (1.0 / HEAD_DIM) + EPS)


def _pair_mean(x, lo_half):
    s0 = jnp.sum(jnp.where(lo_half, x, 0.0), axis=1, keepdims=True)
    s1 = jnp.sum(jnp.where(lo_half, 0.0, x), axis=1, keepdims=True)
    return jnp.where(lo_half, s0, s1) * (1.0 / HEAD_DIM)


def _prep_fwd(zm, c, qkg, rope_c, rope_a, rope_b):
    S = zm.shape[0]
    shp = jax.ShapeDtypeStruct((N_HEADS, S, HEAD_PAD), BF16)

    def body(z_ref, c_ref, g_ref, rc_ref, ra_ref, rb_ref, qa_ref, ka_ref, va_ref, qb_ref, kb_ref, vb_ref):
        lane = lax.broadcasted_iota(jnp.int32, (TOK_T, HEAD_PAD), 1)
        lo_half = lane < HEAD_DIM
        aug = (lane >= HEAD_DIM) & (lane < HEAD_DIM + 3)
        q_pad = jnp.where(aug, -1.0, 0.0)
        cs = c_ref[...]
        rc, ra, rb = rc_ref[...], ra_ref[...], rb_ref[...]

        def norm(col, gi):
            x = z_ref[:, col:col + HEAD_PAD]
            return x * _pair_rsqrt(x, lo_half) * g_ref[gi:gi + 1, :]

        def rope(y):
            return y * rc + pltpu.roll(y, HEAD_PAD - ROPE_HALF, 1) * ra + pltpu.roll(y, ROPE_HALF, 1) * rb

        def put(ref, pi, y, pad_even, pad_odd):
            ref[2 * pi] = jnp.where(lo_half, y, pad_even).astype(BF16)
            ref[2 * pi + 1] = jnp.where(lo_half, pltpu.roll(y, HEAD_DIM, 1), pad_odd).astype(BF16)

        def k_pad(h):
            ch = cs[:, h:h + 1]
            hi = ch.astype(BF16).astype(F32)
            mid = (ch - hi).astype(BF16).astype(F32)
            lo = ch - hi - mid
            ones = jnp.where(lane == HEAD_DIM + 3, 1.0, 0.0)
            return jnp.where(lane == HEAD_DIM, hi, jnp.where(lane == HEAD_DIM + 1, mid,
                                                              jnp.where(lane == HEAD_DIM + 2, lo, ones)))

        for pi in range(N_HEADS // 2):
            col = HEAD_PAD * pi
            put(qa_ref, pi, norm(col, 0) * SCALE, q_pad, q_pad)
            put(ka_ref, pi, norm(D_BRANCH + col, 1), k_pad(2 * pi), k_pad(2 * pi + 1))
            put(va_ref, pi, z_ref[:, 2 * D_BRANCH + col:2 * D_BRANCH + col + HEAD_PAD], 0.0, 0.0)
            put(qb_ref, pi, rope(norm(4 * D_BRANCH + col, 2)) * SCALE, 0.0, 0.0)
            put(kb_ref, pi, rope(norm(5 * D_BRANCH + col, 3)), 0.0, 0.0)
            put(vb_ref, pi, z_ref[:, 6 * D_BRANCH + col:6 * D_BRANCH + col + HEAD_PAD], 0.0, 0.0)

    tok = lambda w: pl.BlockSpec((TOK_T, w), lambda i: (i, 0))
    head = pl.BlockSpec((N_HEADS, TOK_T, HEAD_PAD), lambda i: (0, i, 0))
    return pl.pallas_call(
        body, name="prep_fwd", grid=(S // TOK_T,),
        in_specs=[tok(N_MAIN), tok(128), pl.BlockSpec((4, 128), lambda i: (0, 0)), tok(128), tok(128), tok(128)],
        out_specs=[head] * 6, out_shape=[shp] * 6,
        compiler_params=_cp("parallel"),
    )(zm, c, qkg, rope_c, rope_a, rope_b)


def _pair(ref, pi, lo_half):
    return jnp.where(lo_half, ref[2 * pi], pltpu.roll(ref[2 * pi + 1], HEAD_DIM, 1))


def _mid_fwd(oa, ob, zm, h0, p, w_out, w_pg, w_ple, g2):
    S = h0.shape[0]

    def body(oa_ref, ob_ref, ga_ref, gb_ref, h0_ref, p_ref, wo_ref, wg_ref, wp_ref, g2_ref,
             y_ref, h1_ref, h2_ref, u2_ref, e_ref, gate_ref):
        lane = lax.broadcasted_iota(jnp.int32, (TOK_T, HEAD_PAD), 1)
        lo_half = lane < HEAD_DIM
        parts = []
        for o_ref, g_ref in ((oa_ref, ga_ref), (ob_ref, gb_ref)):
            for pi in range(N_HEADS // 2):
                g = g_ref[:, HEAD_PAD * pi:HEAD_PAD * (pi + 1)]
                parts.append((_pair(o_ref, pi, lo_half) * (g * _sigmoid(g))).astype(BF16))
        y = jnp.concatenate(parts, axis=1)
        y_ref[...] = y
        h1 = h0_ref[...] + _dot(y, wo_ref[...])
        h1_ref[...] = h1
        r = lax.rsqrt(jnp.mean(h1 * h1, axis=-1, keepdims=True) + EPS)
        u2 = (h1 * r * g2_ref[...]).astype(BF16)
        u2_ref[...] = u2
        gate = _sigmoid(_dot(u2, wg_ref[...]))
        e = _dot(p_ref[...].astype(BF16), wp_ref[...])
        e_ref[...] = e
        gate_ref[...] = gate
        h2_ref[...] = h1 + e * gate

    tok = lambda w: pl.BlockSpec((TOK_T, w), lambda i: (i, 0))
    head = pl.BlockSpec((N_HEADS, TOK_T, HEAD_PAD), lambda i: (0, i, 0))
    full = lambda a, b: pl.BlockSpec((a, b), lambda i: (0, 0))
    act = lambda dt: jax.ShapeDtypeStruct((S, D_MODEL), dt)
    return pl.pallas_call(
        body, name="mid_fwd", grid=(S // TOK_T,),
        in_specs=[head, head,
                  pl.BlockSpec((TOK_T, D_BRANCH), lambda i: (i, 3)), pl.BlockSpec((TOK_T, D_BRANCH), lambda i: (i, 7)),
                  tok(D_MODEL), tok(PLE_DIM), full(D_MODEL, D_MODEL), full(D_MODEL, D_MODEL),
                  full(PLE_DIM, D_MODEL), full(1, D_MODEL)],
        out_specs=[tok(D_MODEL)] * 6,
        out_shape=[act(BF16), act(F32), act(F32), act(BF16), act(F32), act(F32)],
        compiler_params=_cp("parallel"),
    )(oa, ob, zm, zm, h0, p, w_out, w_pg, w_ple, g2)


def _loss_fwd_bwd(y, t):
    S = y.shape[0]

    def body(y_ref, t_ref, dy_ref, loss_ref):
        @pl.when(pl.program_id(0) == 0)
        def _():
            loss_ref[...] = jnp.zeros_like(loss_ref)

        err = y_ref[...] - t_ref[...]
        dy_ref[...] = err * (1.0 / D_MODEL)
        part = jnp.sum(jnp.sum(err * err, axis=1, keepdims=True), axis=0, keepdims=True)
        loss_ref[...] += part * (0.5 / D_MODEL)

    tok = pl.BlockSpec((TOK_T, D_MODEL), lambda i: (i, 0))
    return pl.pallas_call(
        body, name="loss", grid=(S // TOK_T,),
        in_specs=[tok, tok], out_specs=[tok, pl.BlockSpec((8, 128), lambda i: (0, 0))],
        out_shape=[jax.ShapeDtypeStruct((S, D_MODEL), F32), jax.ShapeDtypeStruct((8, 128), F32)],
        compiler_params=_cp("arbitrary"),
    )(y, t)


def _bias_tables(full_range):
    T = ATT_T
    nb = 1 if full_range else DILATED_PATTERNS[-1][0] // T + 1
    r = lax.broadcasted_iota(jnp.int32, (nb, T, T), 2)
    c = lax.broadcasted_iota(jnp.int32, (nb, T, T), 1)
    b = lax.broadcasted_iota(jnp.int32, (nb, T, T), 0)
    delta = T * b + r - c
    if full_range:
        bias = jnp.where(delta >= 0, 0.0, NEG).astype(F32)
    else:
        mult = jnp.zeros((nb, T, T), F32)
        for window, dil in DILATED_PATTERNS:
            ok = (delta >= 0) & (delta <= window) & (delta % dil == 0)
            mult = mult + ok.astype(F32)
        bias = jnp.where(mult > 0, jnp.log(jnp.maximum(mult, 1.0)), NEG).astype(F32)
    return bias


def _attn_fwd(q, k, v, table_t, full_range, name):
    H, S, _ = q.shape
    T = ATT_T
    nb = table_t.shape[0]
    HB = ATT_FWD_HEADS
    KC = ATT_CHUNK
    chunks = [slice(c, c + KC) for c in range(0, T, KC)]
    fold = lambda x, op: functools.reduce(op, [x[r:r + 8] for r in range(0, KC, 8)])

    def body(q_ref, k_ref, v_ref, tab_ref, o_ref, lse_ref, *scratch):
        st_refs, pt_refs, acc_refs = scratch[:HB], scratch[HB:2 * HB], scratch[2 * HB:]
        i = pl.program_id(1)
        rows = lambda j: pl.ds(pl.multiple_of(j * T, T), T)

        def scores(hh, j):
            st_refs[hh][...] = _dot_nt(k_ref[hh, rows(j), :], q_ref[hh])

        def block(j, b, nxt, stats):
            out = []
            for hh, (m, l) in enumerate(stats):
                st_ref, pt_ref, acc_ref = st_refs[hh], pt_refs[hh], acc_refs[hh]
                mx = None
                for ch in chunks:
                    x = st_ref[ch, :]
                    if b is not None:
                        x = x + tab_ref[b, ch, :]
                        st_ref[ch, :] = x
                    x = fold(x, jnp.maximum)
                    mx = x if mx is None else jnp.maximum(mx, x)
                m_new = jnp.maximum(m, jnp.max(mx, axis=0, keepdims=True))
                alpha = jnp.exp(m - m_new)
                ls = None
                for ch in chunks:
                    pc = jnp.exp(st_ref[ch, :] - m_new)
                    pt_ref[ch, :] = pc.astype(BF16)
                    pc = fold(pc, jnp.add)
                    ls = pc if ls is None else ls + pc
                if nxt is not None:
                    scores(hh, nxt)
                acc_ref[...] = alpha * acc_ref[...] + _dot_tn(v_ref[hh, rows(j), :], pt_ref[...])
                out.append((m_new, alpha * l + jnp.sum(ls, axis=0, keepdims=True)))
            return tuple(out)

        lo = 0 if full_range else jnp.maximum(i - (nb - 1), 0)
        for hh in range(HB):
            acc_refs[hh][...] = jnp.zeros_like(acc_refs[hh])
            scores(hh, lo)
        stats = lax.fori_loop(lo, i, lambda j, st: block(j, None if full_range else i - j, j + 1, st),
                              ((jnp.full((1, T), NEG, F32), jnp.zeros((1, T), F32)),) * HB)
        stats = block(i, 0, None, stats)
        for hh, (m, l) in enumerate(stats):
            o_ref[hh] = (acc_refs[hh][...] * (1.0 / l)).T
            lse_ref[hh, 0] = m + jnp.log(l)

    return pl.pallas_call(
        body, name=name, grid=(H // HB, S // T),
        in_specs=[pl.BlockSpec((HB, T, HEAD_PAD), lambda h, i: (h, i, 0)),
                  pl.BlockSpec((HB, S, HEAD_PAD), lambda h, i: (h, 0, 0)),
                  pl.BlockSpec((HB, S, HEAD_PAD), lambda h, i: (h, 0, 0)),
                  pl.BlockSpec((nb, T, T), lambda h, i: (0, 0, 0))],
        out_specs=[pl.BlockSpec((HB, T, HEAD_PAD), lambda h, i: (h, i, 0)),
                   pl.BlockSpec((HB, 1, 1, T), lambda h, i: (h, i, 0, 0))],
        out_shape=[jax.ShapeDtypeStruct((H, S, HEAD_PAD), F32), jax.ShapeDtypeStruct((H, S // T, 1, T), F32)],
        scratch_shapes=([pltpu.VMEM((T, T), F32)] * HB + [pltpu.VMEM((T, T), BF16)] * HB
                        + [pltpu.VMEM((HEAD_PAD, T), F32)] * HB),
        compiler_params=_cp("parallel", "arbitrary"),
    )(q, k, v, table_t)


def _attn_bwd(q, k, v, do, lse, dd, table_t, full_range, name):
    H, S, _ = q.shape
    T = ATT_T
    nq = S // T
    nb = table_t.shape[0]

    def body(q_ref, do_ref, lse_ref, dd_ref, k_ref, v_ref, tab_ref, dq_ref, dk_ref, dv_ref):
        j = pl.program_id(1)

        @pl.when(j == 0)
        def _():
            dq_ref[...] = jnp.zeros_like(dq_ref)

        kb = k_ref[0]
        vb = v_ref[0]

        def step(i, bias, carry):
            dk, dv = carry
            rows = pl.ds(pl.multiple_of(i * T, T), T)
            qb = q_ref[0, rows, :]
            dob = do_ref[0, rows, :]
            st = _dot_nt(kb, qb)
            if bias is not None:
                st = st + bias
            pt = jnp.exp(st - lse_ref[0, i])
            dpt = _dot_nt(vb, dob)
            dst = (pt * (dpt - dd_ref[0, i])).astype(BF16)
            dv = dv + _dot(pt.astype(BF16), dob)
            dk = dk + _dot(dst, qb)
            dq_ref[0, rows, :] += _dot_tn(dst, kb)
            return dk, dv

        zero = jnp.zeros((T, HEAD_PAD), F32)
        carry = step(j, tab_ref[0], (zero, zero))
        if full_range:
            carry = lax.fori_loop(j + 1, nq, lambda i, cr: step(i, None, cr), carry)
        else:
            carry = lax.fori_loop(j + 1, jnp.minimum(j + nb, nq), lambda i, cr: step(i, tab_ref[i - j], cr), carry)
        dk_ref[0] = carry[0]
        dv_ref[0] = carry[1]

    per_head = pl.BlockSpec((1, S, HEAD_PAD), lambda h, j: (h, 0, 0))
    rows = pl.BlockSpec((1, nq, 1, T), lambda h, j: (h, 0, 0, 0))
    blk = pl.BlockSpec((1, T, HEAD_PAD), lambda h, j: (h, j, 0))
    shp = jax.ShapeDtypeStruct((H, S, HEAD_PAD), F32)
    return pl.pallas_call(
        body, name=name, grid=(H, nq),
        in_specs=[per_head, per_head, rows, rows, blk, blk, pl.BlockSpec((nb, T, T), lambda h, j: (0, 0, 0))],
        out_specs=[per_head, blk, blk], out_shape=[shp, shp, shp],
        compiler_params=_cp("parallel", "arbitrary"),
    )(q, do, lse, dd, k, v, table_t)


def _mid_bwd(dh2, h1, e, gate, g2, w_pg_t, w_out_t, oa, ob, zm):
    S = dh2.shape[0]

    def body(dh2_ref, h1_ref, e_ref, gate_ref, g2_ref, wg_ref, wo_ref, oa_ref, ob_ref, ga_ref, gb_ref,
             dh1_ref, dh1b_ref, de_ref, dpre_ref, doa_ref, dob_ref, dga_ref, dgb_ref, dda_ref, ddb_ref, dg2_ref):
        @pl.when(pl.program_id(0) == 0)
        def _():
            dg2_ref[...] = jnp.zeros_like(dg2_ref)

        lane = lax.broadcasted_iota(jnp.int32, (TOK_T, HEAD_PAD), 1)
        lo_half = lane < HEAD_DIM
        dh2 = dh2_ref[...]
        gate = gate_ref[...]
        de_ref[...] = (dh2 * gate).astype(BF16)
        dpre = (dh2 * e_ref[...] * gate * (1.0 - gate)).astype(BF16)
        dpre_ref[...] = dpre
        du2 = _dot(dpre, wg_ref[...])
        h1 = h1_ref[...]
        r = lax.rsqrt(jnp.mean(h1 * h1, axis=-1, keepdims=True) + EPS)
        xh = h1 * r
        a = du2 * g2_ref[...]
        dh1 = dh2 + r * (a - xh * jnp.mean(a * xh, axis=-1, keepdims=True))
        dg2_ref[...] += jnp.sum(du2 * xh, axis=0, keepdims=True)
        dh1_ref[...] = dh1
        dh1b = dh1.astype(BF16)
        dh1b_ref[...] = dh1b
        dy = _dot(dh1b, wo_ref[...])
        for bi, (o_ref, g_ref, do_ref, dg_ref, dd_ref) in enumerate(
                ((oa_ref, ga_ref, doa_ref, dga_ref, dda_ref), (ob_ref, gb_ref, dob_ref, dgb_ref, ddb_ref))):
            for pi in range(N_HEADS // 2):
                col = bi * D_BRANCH + HEAD_PAD * pi
                dyp = dy[:, col:col + HEAD_PAD]
                g = g_ref[:, HEAD_PAD * pi:HEAD_PAD * (pi + 1)]
                sg = _sigmoid(g)
                dg_ref[:, HEAD_PAD * pi:HEAD_PAD * (pi + 1)] = (
                    dyp * _pair(o_ref, pi, lo_half) * (sg * (1.0 + g * (1.0 - sg)))).astype(BF16)
                dop = dyp * (g * sg)
                for hh, d_head in ((2 * pi, dop), (2 * pi + 1, pltpu.roll(dop, HEAD_DIM, 1))):
                    d_head = jnp.where(lo_half, d_head, 0.0)
                    do_ref[hh] = d_head.astype(BF16)
                    dd_ref[hh] = jnp.sum(d_head * o_ref[hh], axis=1, keepdims=True)

    tok = lambda w: pl.BlockSpec((TOK_T, w), lambda i: (i, 0))
    head = pl.BlockSpec((N_HEADS, TOK_T, HEAD_PAD), lambda i: (0, i, 0))
    head1 = pl.BlockSpec((N_HEADS, TOK_T, 1), lambda i: (0, i, 0))
    full = lambda a, b: pl.BlockSpec((a, b), lambda i: (0, 0))
    act = lambda w, dt: jax.ShapeDtypeStruct((S, w), dt)
    hshape = lambda w, dt: jax.ShapeDtypeStruct((N_HEADS, S, w), dt)
    return pl.pallas_call(
        body, name="mid_bwd", grid=(S // TOK_T,),
        in_specs=[tok(D_MODEL)] * 4 + [full(1, D_MODEL), full(D_MODEL, D_MODEL), full(D_MODEL, D_MODEL), head, head,
                                      pl.BlockSpec((TOK_T, D_BRANCH), lambda i: (i, 3)),
                                      pl.BlockSpec((TOK_T, D_BRANCH), lambda i: (i, 7))],
        out_specs=[tok(D_MODEL)] * 4 + [head, head, tok(D_BRANCH), tok(D_BRANCH), head1, head1, full(1, D_MODEL)],
        out_shape=[act(D_MODEL, F32), act(D_MODEL, BF16), act(D_MODEL, BF16), act(D_MODEL, BF16),
                   hshape(HEAD_PAD, BF16), hshape(HEAD_PAD, BF16), act(D_BRANCH, BF16), act(D_BRANCH, BF16),
                   hshape(1, F32), hshape(1, F32), jax.ShapeDtypeStruct((1, D_MODEL), F32)],
        compiler_params=_cp("arbitrary"),
    )(dh2, h1, e, gate, g2, w_pg_t, w_out_t, oa, ob, zm, zm)


def _prep_bwd(dqa, dka, dva, dqb, dkb, dvb, zm, qkg, rope_c, rope_a, rope_b, dga, dgb):
    S = zm.shape[0]

    def body(dqa_ref, dka_ref, dva_ref, dqb_ref, dkb_ref, dvb_ref, z_ref, g_ref, rc_ref, ra_ref, rb_ref,
             dga_ref, dgb_ref, dz_ref, dc_ref, dqkg_ref):
        @pl.when(pl.program_id(0) == 0)
        def _():
            dqkg_ref[...] = jnp.zeros_like(dqkg_ref)

        lane = lax.broadcasted_iota(jnp.int32, (TOK_T, HEAD_PAD), 1)
        lo_half = lane < HEAD_DIM
        rc, ra, rb = rc_ref[...], ra_ref[...], rb_ref[...]

        def unrope(dy):
            return dy * rc + pltpu.roll(dy * ra, ROPE_HALF, 1) + pltpu.roll(dy * rb, HEAD_PAD - ROPE_HALF, 1)

        def norm_bwd(col, gi, dy):
            x = z_ref[:, col:col + HEAD_PAD]
            r = _pair_rsqrt(x, lo_half)
            xh = x * r
            dqkg_ref[gi:gi + 1, :] += jnp.sum(dy * xh, axis=0, keepdims=True)
            a = dy * g_ref[gi:gi + 1, :]
            dz_ref[:, col:col + HEAD_PAD] = (r * (a - xh * _pair_mean(a * xh, lo_half))).astype(BF16)

        dc = jnp.zeros((TOK_T, HEAD_PAD), F32)
        for pi in range(N_HEADS // 2):
            col = HEAD_PAD * pi
            norm_bwd(col, 0, _pair(dqa_ref, pi, lo_half) * SCALE)
            norm_bwd(D_BRANCH + col, 1, _pair(dka_ref, pi, lo_half))
            dz_ref[:, 2 * D_BRANCH + col:2 * D_BRANCH + col + HEAD_PAD] = _pair(dva_ref, pi, lo_half).astype(BF16)
            norm_bwd(4 * D_BRANCH + col, 2, unrope(_pair(dqb_ref, pi, lo_half) * SCALE))
            norm_bwd(5 * D_BRANCH + col, 3, unrope(_pair(dkb_ref, pi, lo_half)))
            dz_ref[:, 6 * D_BRANCH + col:6 * D_BRANCH + col + HEAD_PAD] = _pair(dvb_ref, pi, lo_half).astype(BF16)
            for hh in (2 * pi, 2 * pi + 1):
                dch = dka_ref[hh][:, HEAD_DIM:HEAD_DIM + 1] + dqa_ref[hh][:, HEAD_DIM + 3:HEAD_DIM + 4]
                dc = dc + jnp.where(lane == hh, dch, 0.0)
        dz_ref[:, 3 * D_BRANCH:4 * D_BRANCH] = dga_ref[...]
        dz_ref[:, 7 * D_BRANCH:8 * D_BRANCH] = dgb_ref[...]
        dc_ref[...] = dc

    tok = lambda w: pl.BlockSpec((TOK_T, w), lambda i: (i, 0))
    head = pl.BlockSpec((N_HEADS, TOK_T, HEAD_PAD), lambda i: (0, i, 0))
    return pl.pallas_call(
        body, name="prep_bwd", grid=(S // TOK_T,),
        in_specs=[head] * 6 + [tok(N_MAIN), pl.BlockSpec((4, 128), lambda i: (0, 0)), tok(128), tok(128), tok(128),
                               tok(D_BRANCH), tok(D_BRANCH)],
        out_specs=[tok(N_MAIN), tok(128), pl.BlockSpec((4, 128), lambda i: (0, 0))],
        out_shape=[jax.ShapeDtypeStruct((S, N_MAIN), BF16), jax.ShapeDtypeStruct((S, 128), F32),
                   jax.ShapeDtypeStruct((4, 128), F32)],
        compiler_params=_cp("arbitrary"),
    )(dqa, dka, dva, dqb, dkb, dvb, zm, qkg, rope_c, rope_a, rope_b, dga, dgb)


def _inproj_bwd(dzm, dzf, wm_t, wf_t, h0, dh1, g):
    S = h0.shape[0]

    def body(dzm_ref, dzf_ref, wm_ref, wf_ref, h_ref, dh1_ref, g_ref, dh0_ref, dg_ref):
        @pl.when(pl.program_id(0) == 0)
        def _():
            dg_ref[...] = jnp.zeros_like(dg_ref)

        du = _dot(dzm_ref[...], wm_ref[...]) + _dot(dzf_ref[...], wf_ref[...])
        x = h_ref[...]
        r = lax.rsqrt(jnp.mean(x * x, axis=-1, keepdims=True) + EPS)
        xh = x * r
        a = du * g_ref[...]
        dh0_ref[...] = dh1_ref[...] + r * (a - xh * jnp.mean(a * xh, axis=-1, keepdims=True))
        dg_ref[...] += jnp.sum(du * xh, axis=0, keepdims=True)

    tok = lambda w: pl.BlockSpec((TOK_T, w), lambda i: (i, 0))
    full = lambda a, b: pl.BlockSpec((a, b), lambda i: (0, 0))
    return pl.pallas_call(
        body, name="inproj_bwd", grid=(S // TOK_T,),
        in_specs=[tok(N_MAIN), tok(128), full(N_MAIN, D_MODEL), full(128, D_MODEL), tok(D_MODEL), tok(D_MODEL),
                  full(1, D_MODEL)],
        out_specs=[tok(D_MODEL), full(1, D_MODEL)],
        out_shape=[jax.ShapeDtypeStruct((S, D_MODEL), F32), jax.ShapeDtypeStruct((1, D_MODEL), F32)],
        compiler_params=_cp("arbitrary"),
    )(dzm, dzf, wm_t, wf_t, h0, dh1, g)


def _wgrad(a, b, name):
    S, M = a.shape
    N = b.shape[1]
    tn = min(N, 2048)
    ts = 512
    last = S // ts - 1

    def body(a_ref, b_ref, o_ref, acc_ref):
        @pl.when(pl.program_id(1) == 0)
        def _():
            acc_ref[...] = jnp.zeros_like(acc_ref)

        acc_ref[...] += _dot_tn(a_ref[...].astype(BF16), b_ref[...])

        @pl.when(pl.program_id(1) == last)
        def _():
            o_ref[...] = acc_ref[...].astype(BF16)

    return pl.pallas_call(
        body, name=name, grid=(N // tn, S // ts),
        in_specs=[pl.BlockSpec((ts, M), lambda n, s: (s, 0)), pl.BlockSpec((ts, tn), lambda n, s: (s, n))],
        out_specs=pl.BlockSpec((M, tn), lambda n, s: (0, n)),
        out_shape=jax.ShapeDtypeStruct((M, N), BF16),
        scratch_shapes=[pltpu.VMEM((M, tn), F32)],
        compiler_params=_cp("parallel", "arbitrary"),
    )(a, b)


def _rope_tables(positions):
    inv_freq = ROPE_THETA ** (-jnp.arange(ROPE_HALF, dtype=F32) / ROPE_HALF)
    ang = positions.astype(F32)[:, None] * inv_freq
    cos, sin = jnp.cos(ang), jnp.sin(ang)
    S = positions.shape[0]
    one, zero = jnp.ones((S, HEAD_DIM - 2 * ROPE_HALF), F32), jnp.zeros((S, HEAD_DIM - 2 * ROPE_HALF), F32)
    z8 = jnp.zeros((S, ROPE_HALF), F32)
    rc = jnp.concatenate([cos, cos, one], axis=1)
    ra = jnp.concatenate([-sin, z8, zero], axis=1)
    rb = jnp.concatenate([z8, sin, zero], axis=1)
    return tuple(jnp.tile(t, (1, 2)) for t in (rc, ra, rb))


def _layer_weights(w_in, w_out, w_ple, w_pg):
    w_in = w_in.astype(BF16)
    wm = jnp.concatenate([w_in[:, :4 * D_BRANCH], w_in[:, 4 * D_BRANCH + N_HEADS:]], axis=1)
    wf = jnp.pad(w_in[:, 4 * D_BRANCH:4 * D_BRANCH + N_HEADS], ((0, 0), (0, 128 - N_HEADS)))
    w_out, w_ple, w_pg = w_out.astype(BF16), w_ple.astype(BF16), w_pg.astype(BF16)
    return dict(wm=wm, wf=wf, wm_t=wm.T, wf_t=wf.T, w_out=w_out, w_out_t=w_out.T, w_ple=w_ple, w_pg=w_pg,
                w_pg_t=w_pg.T)


def _row(v, width=128):
    v = v.reshape(1, -1).astype(F32)
    return jnp.pad(v, ((0, 0), (0, width - v.shape[1])))


def _layer_fwd(h0, p, rope, tabs, w, norm_g, b_f, qk_g, ple_g):
    S = h0.shape[0]
    g1 = norm_g.reshape(1, D_MODEL)
    g2 = ple_g.reshape(1, D_MODEL)
    qkg = jnp.tile(qk_g, (1, 2))
    bf = _row(b_f)
    zm, zf, u = _inproj_fwd(h0, g1, w["wm"], w["wf"])
    c = _forget_cumsum(zf, bf, tabs["tril"])
    qa, ka, va, qb, kb, vb = _prep_fwd(zm, c, qkg, *rope)
    oa, lse_a = _attn_fwd(qa, ka, va, tabs["fox"], True, "fox_fwd")
    ob, lse_b = _attn_fwd(qb, kb, vb, tabs["dil"], False, "dil_fwd")
    y, h1, h2, u2, e, gate = _mid_fwd(oa, ob, zm, h0, p, w["w_out"], w["w_pg"], w["w_ple"], g2)
    saved = dict(h0=h0, p=p, zm=zm, zf=zf, u=u, qa=qa, ka=ka, va=va, qb=qb, kb=kb, vb=vb, oa=oa, ob=ob,
                 lse_a=lse_a, lse_b=lse_b, y=y, h1=h1, u2=u2, e=e, gate=gate, g1=g1, g2=g2, qkg=qkg, bf=bf)
    return h2, saved


def _layer_bwd(dh2, sv, rope, tabs, w):
    S = dh2.shape[0]
    nq = S // ATT_T
    rows = lambda a: a.reshape(N_HEADS, nq, 1, ATT_T)
    (dh1, dh1b, de, dpre, doa, dob, dga, dgb, dda, ddb, dg2) = _mid_bwd(
        dh2, sv["h1"], sv["e"], sv["gate"], sv["g2"], w["w_pg_t"], w["w_out_t"], sv["oa"], sv["ob"], sv["zm"])
    dqa, dka, dva = _attn_bwd(sv["qa"], sv["ka"], sv["va"], doa, sv["lse_a"], rows(dda), tabs["fox"], True,
                              "fox_bwd")
    dqb, dkb, dvb = _attn_bwd(sv["qb"], sv["kb"], sv["vb"], dob, sv["lse_b"], rows(ddb), tabs["dil"], False,
                              "dil_bwd")
    dzm, dc, dqkg = _prep_bwd(dqa, dka, dva, dqb, dkb, dvb, sv["zm"], sv["qkg"], *rope, dga, dgb)
    dzf, dbf = _forget_bwd(dc, sv["zf"], sv["bf"], tabs["triu"])
    dh0, dg1 = _inproj_bwd(dzm, dzf, w["wm_t"], w["wf_t"], sv["h0"], dh1, sv["g1"])
    dwm = _wgrad(sv["u"], dzm, "wgrad_in")
    dwf = _wgrad(sv["u"], dzf, "wgrad_f")
    dw_in = jnp.concatenate([dwm[:, :4 * D_BRANCH], dwf[:, :N_HEADS], dwm[:, 4 * D_BRANCH:]], axis=1)
    grads = dict(
        norm_g=dg1.reshape(D_MODEL), w_in=dw_in, b_f=dbf[0, :N_HEADS],
        qk_norm_g=dqkg[:, :HEAD_DIM] + dqkg[:, HEAD_DIM:],
        w_out=_wgrad(sv["y"], dh1b, "wgrad_out"), w_ple=_wgrad(sv["p"], de, "wgrad_ple"),
        ple_norm_g=dg2.reshape(D_MODEL), w_ple_gate=_wgrad(sv["u2"], dpre, "wgrad_gate"))
    return dh0, grads


def _tables():
    T = CUM_T
    r = lax.broadcasted_iota(jnp.int32, (T, T), 0)
    c = lax.broadcasted_iota(jnp.int32, (T, T), 1)
    return dict(fox=_bias_tables(True), dil=_bias_tables(False),
                tril=(c <= r).astype(BF16), triu=(c >= r).astype(BF16))


def _local_step(x, p, positions, target, layers, small):
    rope = _rope_tables(positions)
    tabs = _tables()
    ws = [_layer_weights(*lw) for lw in layers]
    h = x
    saved = []
    for w, lp, sm in zip(ws, p, small):
        h, sv = _layer_fwd(h, lp, rope, tabs, w, *sm)
        saved.append(sv)
    dh, loss = _loss_fwd_bwd(h, target)
    grads = [None] * len(ws)
    for li in reversed(range(len(ws))):
        dh, grads[li] = _layer_bwd(dh, saved[li], rope, tabs, ws[li])
    return loss[0, 0], dh, grads


def _peers():
    x, y, c = lax.axis_index("x"), lax.axis_index("y"), lax.axis_index("c")
    me = 4 * x + 2 * y + c
    flip = lambda v, bit: 1 - v if bit else v
    return me, [(flip(x, k & 4), flip(y, k & 2), flip(c, k & 1)) for k in range(1, N_DEV)]


def _shard(ref, axis, d, size):
    return ref.at[(slice(None),) * axis + (pl.ds(pl.multiple_of(d * size, size), size),)]


def _exchange(name, arrays, out_shapes, src_of, dst_of):
    n = len(arrays)

    def body(*refs):
        ins, outs = refs[:n], refs[n:2 * n]
        send_sems, recv_sems, local_sems = refs[2 * n:]
        me, peers = _peers()
        local = [pltpu.make_async_copy(src_of(a, ins[a], me), dst_of(a, outs[a], me), local_sems.at[a])
                 for a in range(n)]
        for cp in local:
            cp.start()
        copies = [pltpu.make_async_remote_copy(
            src_ref=src_of(a, ins[a], 4 * px + 2 * py + pc), dst_ref=dst_of(a, outs[a], me),
            send_sem=send_sems.at[k, a], recv_sem=recv_sems.at[k, a],
            device_id=(px, py, pc), device_id_type=pl.DeviceIdType.MESH)
            for k, (px, py, pc) in enumerate(peers) for a in range(n)]
        for cp in copies:
            cp.start()
        for cp in copies:
            cp.wait()
        for cp in local:
            cp.wait()

    hbm = pl.BlockSpec(memory_space=pltpu.HBM)
    return pl.pallas_call(
        body, name=name, in_specs=[hbm] * n, out_specs=[hbm] * n, out_shape=out_shapes,
        scratch_shapes=[pltpu.SemaphoreType.DMA((N_DEV - 1, n)), pltpu.SemaphoreType.DMA((N_DEV - 1, n)),
                        pltpu.SemaphoreType.DMA((n,))],
    )(*arrays)


_SHARD_AXES = (None, (1, 128), (2, 128), (1, 128))


def _gather_weights(shards):
    full = [(N_DEV,) + shards[0].shape, (2, D_MODEL, D_MODEL), (2, PLE_DIM, D_MODEL), (2, D_MODEL, D_MODEL)]
    return _exchange(
        "gather_weights", shards, [jax.ShapeDtypeStruct(f, BF16) for f in full],
        src_of=lambda a, ref, peer: ref,
        dst_of=lambda a, ref, me: ref.at[me] if a == 0 else _shard(ref, _SHARD_AXES[a][0], me, _SHARD_AXES[a][1]))


def _exchange_grads(partials, small):
    shard_shapes = [partials[0].shape[1:], (2, 128, D_MODEL), (2, PLE_DIM, 128), (2, 128, D_MODEL), small.shape]

    def src_of(a, ref, peer):
        if a == 0:
            return ref.at[peer]
        return ref if a == len(partials) else _shard(ref, _SHARD_AXES[a][0], peer, _SHARD_AXES[a][1])

    arrays = list(partials) + [small]
    return _exchange(
        "exchange_grads", arrays, [jax.ShapeDtypeStruct((N_DEV,) + s, a.dtype) for s, a in zip(shard_shapes, arrays)],
        src_of=src_of, dst_of=lambda a, ref, me: ref.at[me])


def _adamw(name, parts, w, m, v, rows):
    L, R, C = w.shape

    def body(p_ref, w_ref, m_ref, v_ref, g_ref, d_ref, nm_ref, nv_ref):
        g = p_ref[0, 0].astype(F32)
        for s in range(1, N_DEV):
            g = g + p_ref[s, 0].astype(F32)
        g_ref[0] = g
        nm = ADAM_B1 * m_ref[0] + (1.0 - ADAM_B1) * g
        nv = ADAM_B2 * v_ref[0] + (1.0 - ADAM_B2) * (g * g)
        nm_ref[0] = nm
        nv_ref[0] = nv
        m_hat = nm / (1.0 - ADAM_B1 ** ADAM_STEP)
        v_hat = nv / (1.0 - ADAM_B2 ** ADAM_STEP)
        d_ref[0] = -ADAM_LR * (m_hat / (jnp.sqrt(v_hat) + ADAM_EPS) + ADAM_WD * w_ref[0])

    blk = pl.BlockSpec((1, rows, C), lambda l, i: (l, i, 0))
    shp = jax.ShapeDtypeStruct((L, R, C), F32)
    return pl.pallas_call(
        body, name=name, grid=(L, R // rows),
        in_specs=[pl.BlockSpec((N_DEV, 1, rows, C), lambda l, i: (0, l, i, 0)), blk, blk, blk],
        out_specs=[blk] * 4, out_shape=[shp] * 4,
        compiler_params=_cp("parallel", "parallel"),
    )(parts, w, m, v)


SMALL_ROWS = 40
LOSS_ROW = 37


def _pack_small(norm_g, ple_g, qk_g, b_f, last_row):
    rows = lambda a: a.astype(F32).reshape(-1, 128)
    flat = jnp.concatenate([rows(norm_g), rows(ple_g), rows(qk_g), _row(b_f.reshape(-1)), last_row], axis=0)
    return jnp.pad(flat, ((0, SMALL_ROWS - flat.shape[0]), (0, 0)))


def _unpack_small(flat):
    return (flat[0:16].reshape(2, D_MODEL), flat[16:32].reshape(2, D_MODEL), flat[32:36].reshape(2, 4, HEAD_DIM),
            flat[36, :2 * N_HEADS].reshape(2, N_HEADS))


def kernel(x, p, positions, norm_g, w_in, b_f, qk_norm_g, w_out, w_ple, ple_norm_g, w_ple_gate, loss_target, m_norm_g, m_w_in, m_b_f, m_qk_norm_g, m_w_out, m_w_ple, m_ple_norm_g, m_w_ple_gate, v_norm_g, v_w_in, v_b_f, v_qk_norm_g, v_w_out, v_w_ple, v_ple_norm_g, v_w_ple_gate):
    g_in, full_out, full_ple, full_pg = _gather_weights(
        [w_in.astype(BF16).reshape(W_IN_ROWS, 128)] + [a.astype(BF16) for a in (w_out, w_ple, w_ple_gate)])
    full_in = g_in.reshape(N_DEV, 2, D_MODEL, W_IN_SHARD).transpose(1, 2, 0, 3).reshape(2, D_MODEL, N_IN)
    layers = [(full_in[l], full_out[l], full_ple[l], full_pg[l]) for l in range(2)]
    small = [(norm_g[l], b_f[l], qk_norm_g[l], ple_norm_g[l]) for l in range(2)]
    loss, dx, grads = _local_step(x[0], p[:, 0], positions[0], loss_target[0], layers, small)

    stack = lambda name: jnp.stack([gl[name] for gl in grads], axis=0)
    d_in = stack("w_in").reshape(2, D_MODEL, N_DEV, W_IN_SHARD).transpose(2, 0, 1, 3).reshape(N_DEV, W_IN_ROWS, 128)
    small_part = _pack_small(stack("norm_g"), stack("ple_norm_g"), stack("qk_norm_g"), stack("b_f"),
                             _row(loss.reshape(1)))
    r_in, r_out, r_ple, r_pg, r_small = _exchange_grads([d_in, stack("w_out"), stack("w_ple"), stack("w_ple_gate")],
                                                        small_part)

    zero_row = jnp.zeros((1, 128), F32)
    small_of = lambda ng, pg, qk, bf: _pack_small(ng, pg, qk, bf, zero_row)[None]
    flat = lambda a: a.reshape(1, W_IN_ROWS, 128)
    outs = dict(
        w_in=[o.reshape(w_in.shape) for o in
              _adamw("adamw_in", r_in[:, None], flat(w_in), flat(m_w_in), flat(v_w_in), W_IN_TILE)],
        w_out=_adamw("adamw_out", r_out, w_out, m_w_out, v_w_out, 128),
        w_ple=_adamw("adamw_ple", r_ple, w_ple, m_w_ple, v_w_ple, 256),
        w_pg=_adamw("adamw_gate", r_pg, w_ple_gate, m_w_ple_gate, v_w_ple_gate, 128),
        small=_adamw("adamw_small", r_small[:, None], small_of(norm_g, ple_norm_g, qk_norm_g, b_f),
                     small_of(m_norm_g, m_ple_norm_g, m_qk_norm_g, m_b_f),
                     small_of(v_norm_g, v_ple_norm_g, v_qk_norm_g, v_b_f), SMALL_ROWS))
    leaves = []
    for kind in range(4):
        ng, pg, qk, bf = _unpack_small(outs["small"][kind][0])
        leaves += [ng, outs["w_in"][kind], bf, qk, outs["w_out"][kind], outs["w_ple"][kind], pg, outs["w_pg"][kind]]
    return (outs["small"][0][0, LOSS_ROW, 0], dx[None], *leaves)
```

```python
import functools

import jax
import jax.numpy as jnp
from jax import lax
from jax.experimental import pallas as pl
from jax.experimental.pallas import tpu as pltpu

F32 = jnp.float32
BF16 = jnp.bfloat16

D_MODEL = 1024
HEAD_DIM = 64
N_HEADS = 8
HEAD_PAD = 128
D_BRANCH = N_HEADS * HEAD_DIM
N_MAIN = 8 * D_BRANCH
N_IN = N_MAIN + N_HEADS
PLE_DIM = 256
ROPE_THETA = 500000.0
ROPE_HALF = 8
EPS = 1e-6
NEG = -1e30
SCALE = HEAD_DIM ** -0.5
DILATED_PATTERNS = ((128, 1), (512, 4), (2048, 16))
N_DEV = 8
W_IN_SHARD = N_IN // N_DEV
W_IN_ROWS = 2 * D_MODEL * W_IN_SHARD // 128
W_IN_TILE = W_IN_ROWS // 19

ADAM_LR = 0.001
ADAM_B1 = 0.9
ADAM_B2 = 0.999
ADAM_EPS = 1e-08
ADAM_WD = 0.01
ADAM_STEP = 10

ATT_T = 512
ATT_FWD_HEADS = 2
ATT_BWD_HEADS = 2
ATT_CHUNK = 32
TOK_T = 256
CUM_T = 512
VMEM_LIMIT = 56 * 1024 * 1024


def _cp(*sem):
    return pltpu.CompilerParams(dimension_semantics=sem, vmem_limit_bytes=VMEM_LIMIT)


def _sigmoid(x):
    return 1.0 / (1.0 + jnp.exp(-x))


def _split3(x):
    hi = x.astype(BF16)
    r1 = x - hi.astype(F32)
    mid = r1.astype(BF16)
    lo = (r1 - mid.astype(F32)).astype(BF16)
    return hi, mid, lo


def _dot(a, b):
    return jnp.dot(a, b, preferred_element_type=F32)


def _dot_nt(a, b):
    return lax.dot_general(a, b, (((1,), (1,)), ((), ())), preferred_element_type=F32)


def _dot_tn(a, b):
    return lax.dot_general(a, b, (((0,), (0,)), ((), ())), preferred_element_type=F32)


def _inproj_fwd(h, g, wm, wf):
    S = h.shape[0]

    def body(h_ref, g_ref, wm_ref, wf_ref, zm_ref, zf_ref, u_ref):
        x = h_ref[...]
        r = lax.rsqrt(jnp.mean(x * x, axis=-1, keepdims=True) + EPS)
        u = (x * r * g_ref[...]).astype(BF16)
        u_ref[...] = u
        zm_ref[...] = _dot(u, wm_ref[...])
        zf_ref[...] = _dot(u, wf_ref[...])

    return pl.pallas_call(
        body, name="inproj_fwd", grid=(S // TOK_T,),
        in_specs=[pl.BlockSpec((TOK_T, D_MODEL), lambda i: (i, 0)),
                  pl.BlockSpec((1, D_MODEL), lambda i: (0, 0)),
                  pl.BlockSpec((D_MODEL, N_MAIN), lambda i: (0, 0)),
                  pl.BlockSpec((D_MODEL, 128), lambda i: (0, 0))],
        out_specs=[pl.BlockSpec((TOK_T, N_MAIN), lambda i: (i, 0)),
                   pl.BlockSpec((TOK_T, 128), lambda i: (i, 0)),
                   pl.BlockSpec((TOK_T, D_MODEL), lambda i: (i, 0))],
        out_shape=[jax.ShapeDtypeStruct((S, N_MAIN), F32), jax.ShapeDtypeStruct((S, 128), F32),
                   jax.ShapeDtypeStruct((S, D_MODEL), BF16)],
        compiler_params=_cp("parallel"),
    )(h, g, wm, wf)


def _log_sigmoid(x):
    return jnp.minimum(x, 0.0) - jnp.log(1.0 + jnp.exp(-jnp.abs(x)))


def _forget_cumsum(zf, bf, tri):
    S = zf.shape[0]

    def body(zf_ref, b_ref, tri_ref, c_ref, carry):
        @pl.when(pl.program_id(0) == 0)
        def _():
            carry[...] = jnp.zeros_like(carry)

        lf = _log_sigmoid(zf_ref[...] + b_ref[...])
        hi, mid, lo = _split3(lf)
        t = tri_ref[...]
        cs = _dot(t, hi) + _dot(t, mid) + _dot(t, lo) + carry[...]
        c_ref[...] = cs
        carry[...] = cs[CUM_T - 1:CUM_T, :]

    return pl.pallas_call(
        body, name="forget_cumsum", grid=(S // CUM_T,),
        in_specs=[pl.BlockSpec((CUM_T, 128), lambda i: (i, 0)),
                  pl.BlockSpec((1, 128), lambda i: (0, 0)),
                  pl.BlockSpec((CUM_T, CUM_T), lambda i: (0, 0))],
        out_specs=pl.BlockSpec((CUM_T, 128), lambda i: (i, 0)),
        out_shape=jax.ShapeDtypeStruct((S, 128), F32),
        scratch_shapes=[pltpu.VMEM((1, 128), F32)],
        compiler_params=_cp("arbitrary"),
    )(zf, bf, tri)


def _forget_bwd(dc, zf, bf, triu):
    S = zf.shape[0]
    n = S // CUM_T

    def body(dc_ref, zf_ref, b_ref, tri_ref, dzf_ref, db_ref, carry):
        @pl.when(pl.program_id(0) == 0)
        def _():
            carry[...] = jnp.zeros_like(carry)
            db_ref[...] = jnp.zeros_like(db_ref)

        hi, mid, lo = _split3(dc_ref[...])
        t = tri_ref[...]
        dlf = _dot(t, hi) + _dot(t, mid) + _dot(t, lo) + carry[...]
        carry[...] = dlf[0:1, :]
        dfa = dlf * (1.0 - _sigmoid(zf_ref[...] + b_ref[...]))
        dzf_ref[...] = dfa.astype(BF16)
        db_ref[...] += jnp.sum(dfa, axis=0, keepdims=True)

    return pl.pallas_call(
        body, name="forget_bwd", grid=(n,),
        in_specs=[pl.BlockSpec((CUM_T, 128), lambda i: (n - 1 - i, 0)),
                  pl.BlockSpec((CUM_T, 128), lambda i: (n - 1 - i, 0)),
                  pl.BlockSpec((1, 128), lambda i: (0, 0)),
                  pl.BlockSpec((CUM_T, CUM_T), lambda i: (0, 0))],
        out_specs=[pl.BlockSpec((CUM_T, 128), lambda i: (n - 1 - i, 0)),
                   pl.BlockSpec((1, 128), lambda i: (0, 0))],
        out_shape=[jax.ShapeDtypeStruct((S, 128), BF16), jax.ShapeDtypeStruct((1, 128), F32)],
        scratch_shapes=[pltpu.VMEM((1, 128), F32)],
        compiler_params=_cp("arbitrary"),
    )(dc, zf, bf, triu)


def _pair_rsqrt(x, lo_half):
    x2 = x * x
    s0 = jnp.sum(jnp.where(lo_half, x2, 0.0), axis=1, keepdims=True)
    s1 = jnp.sum(jnp.where(lo_half, 0.0, x2), axis=1, keepdims=True)
    return lax.rsqrt(jnp.where(lo_half, s0, s1) * (1.0 / HEAD_DIM) + EPS)


def _pair_mean(x, lo_half):
    s0 = jnp.sum(jnp.where(lo_half, x, 0.0), axis=1, keepdims=True)
    s1 = jnp.sum(jnp.where(lo_half, 0.0, x), axis=1, keepdims=True)
    return jnp.where(lo_half, s0, s1) * (1.0 / HEAD_DIM)


def _prep_fwd(zm, c, qkg, rope_c, rope_a, rope_b):
    S = zm.shape[0]
    shp = jax.ShapeDtypeStruct((N_HEADS, S, HEAD_PAD), BF16)

    def body(z_ref, c_ref, g_ref, rc_ref, ra_ref, rb_ref, qa_ref, ka_ref, va_ref, qb_ref, kb_ref, vb_ref):
        lane = lax.broadcasted_iota(jnp.int32, (TOK_T, HEAD_PAD), 1)
        lo_half = lane < HEAD_DIM
        aug = (lane >= HEAD_DIM) & (lane < HEAD_DIM + 3)
        q_pad = jnp.where(aug, -1.0, 0.0)
        cs = c_ref[...]
        rc, ra, rb = rc_ref[...], ra_ref[...], rb_ref[...]

        def norm(col, gi):
            x = z_ref[:, col:col + HEAD_PAD]
            return x * _pair_rsqrt(x, lo_half) * g_ref[gi:gi + 1, :]

        def rope(y):
            return y * rc + pltpu.roll(y, HEAD_PAD - ROPE_HALF, 1) * ra + pltpu.roll(y, ROPE_HALF, 1) * rb

        def put(ref, pi, y, pad_even, pad_odd):
            ref[2 * pi] = jnp.where(lo_half, y, pad_even).astype(BF16)
            ref[2 * pi + 1] = jnp.where(lo_half, pltpu.roll(y, HEAD_DIM, 1), pad_odd).astype(BF16)

        def k_pad(h):
            ch = cs[:, h:h + 1]
            hi = ch.astype(BF16).astype(F32)
            mid = (ch - hi).astype(BF16).astype(F32)
            lo = ch - hi - mid
            ones = jnp.where(lane == HEAD_DIM + 3, 1.0, 0.0)
            return jnp.where(lane == HEAD_DIM, hi, jnp.where(lane == HEAD_DIM + 1, mid,
                                                              jnp.where(lane == HEAD_DIM + 2, lo, ones)))

        for pi in range(N_HEADS // 2):
            col = HEAD_PAD * pi
            put(qa_ref, pi, norm(col, 0) * SCALE, q_pad, q_pad)
            put(ka_ref, pi, norm(D_BRANCH + col, 1), k_pad(2 * pi), k_pad(2 * pi + 1))
            put(va_ref, pi, z_ref[:, 2 * D_BRANCH + col:2 * D_BRANCH + col + HEAD_PAD], 0.0, 0.0)
            put(qb_ref, pi, rope(norm(4 * D_BRANCH + col, 2)) * SCALE, 0.0, 0.0)
            put(kb_ref, pi, rope(norm(5 * D_BRANCH + col, 3)), 0.0, 0.0)
            put(vb_ref, pi, z_ref[:, 6 * D_BRANCH + col:6 * D_BRANCH + col + HEAD_PAD], 0.0, 0.0)

    tok = lambda w: pl.BlockSpec((TOK_T, w), lambda i: (i, 0))
    head = pl.BlockSpec((N_HEADS, TOK_T, HEAD_PAD), lambda i: (0, i, 0))
    return pl.pallas_call(
        body, name="prep_fwd", grid=(S // TOK_T,),
        in_specs=[tok(N_MAIN), tok(128), pl.BlockSpec((4, 128), lambda i: (0, 0)), tok(128), tok(128), tok(128)],
        out_specs=[head] * 6, out_shape=[shp] * 6,
        compiler_params=_cp("parallel"),
    )(zm, c, qkg, rope_c, rope_a, rope_b)


def _pair(ref, pi, lo_half):
    return jnp.where(lo_half, ref[2 * pi], pltpu.roll(ref[2 * pi + 1], HEAD_DIM, 1))


def _mid_fwd(oa, ob, zm, h0, p, w_out, w_pg, w_ple, g2):
    S = h0.shape[0]

    def body(oa_ref, ob_ref, ga_ref, gb_ref, h0_ref, p_ref, wo_ref, wg_ref, wp_ref, g2_ref,
             y_ref, h1_ref, h2_ref, u2_ref, e_ref, gate_ref):
        lane = lax.broadcasted_iota(jnp.int32, (TOK_T, HEAD_PAD), 1)
        lo_half = lane < HEAD_DIM
        parts = []
        for o_ref, g_ref in ((oa_ref, ga_ref), (ob_ref, gb_ref)):
            for pi in range(N_HEADS // 2):
                g = g_ref[:, HEAD_PAD * pi:HEAD_PAD * (pi + 1)]
                parts.append((_pair(o_ref, pi, lo_half) * (g * _sigmoid(g))).astype(BF16))
        y = jnp.concatenate(parts, axis=1)
        y_ref[...] = y
        h1 = h0_ref[...] + _dot(y, wo_ref[...])
        h1_ref[...] = h1
        r = lax.rsqrt(jnp.mean(h1 * h1, axis=-1, keepdims=True) + EPS)
        u2 = (h1 * r * g2_ref[...]).astype(BF16)
        u2_ref[...] = u2
        gate = _sigmoid(_dot(u2, wg_ref[...]))
        e = _dot(p_ref[...].astype(BF16), wp_ref[...])
        e_ref[...] = e
        gate_ref[...] = gate
        h2_ref[...] = h1 + e * gate

    tok = lambda w: pl.BlockSpec((TOK_T, w), lambda i: (i, 0))
    head = pl.BlockSpec((N_HEADS, TOK_T, HEAD_PAD), lambda i: (0, i, 0))
    full = lambda a, b: pl.BlockSpec((a, b), lambda i: (0, 0))
    act = lambda dt: jax.ShapeDtypeStruct((S, D_MODEL), dt)
    return pl.pallas_call(
        body, name="mid_fwd", grid=(S // TOK_T,),
        in_specs=[head, head,
                  pl.BlockSpec((TOK_T, D_BRANCH), lambda i: (i, 3)), pl.BlockSpec((TOK_T, D_BRANCH), lambda i: (i, 7)),
                  tok(D_MODEL), tok(PLE_DIM), full(D_MODEL, D_MODEL), full(D_MODEL, D_MODEL),
                  full(PLE_DIM, D_MODEL), full(1, D_MODEL)],
        out_specs=[tok(D_MODEL)] * 6,
        out_shape=[act(BF16), act(F32), act(F32), act(BF16), act(F32), act(F32)],
        compiler_params=_cp("parallel"),
    )(oa, ob, zm, zm, h0, p, w_out, w_pg, w_ple, g2)


def _loss_fwd_bwd(y, t):
    S = y.shape[0]

    def body(y_ref, t_ref, dy_ref, loss_ref):
        @pl.when(pl.program_id(0) == 0)
        def _():
            loss_ref[...] = jnp.zeros_like(loss_ref)

        err = y_ref[...] - t_ref[...]
        dy_ref[...] = err * (1.0 / D_MODEL)
        part = jnp.sum(jnp.sum(err * err, axis=1, keepdims=True), axis=0, keepdims=True)
        loss_ref[...] += part * (0.5 / D_MODEL)

    tok = pl.BlockSpec((TOK_T, D_MODEL), lambda i: (i, 0))
    return pl.pallas_call(
        body, name="loss", grid=(S // TOK_T,),
        in_specs=[tok, tok], out_specs=[tok, pl.BlockSpec((8, 128), lambda i: (0, 0))],
        out_shape=[jax.ShapeDtypeStruct((S, D_MODEL), F32), jax.ShapeDtypeStruct((8, 128), F32)],
        compiler_params=_cp("arbitrary"),
    )(y, t)


def _bias_tables(full_range):
    T = ATT_T
    nb = 1 if full_range else DILATED_PATTERNS[-1][0] // T + 1
    r = lax.broadcasted_iota(jnp.int32, (nb, T, T), 2)
    c = lax.broadcasted_iota(jnp.int32, (nb, T, T), 1)
    b = lax.broadcasted_iota(jnp.int32, (nb, T, T), 0)
    delta = T * b + r - c
    if full_range:
        bias = jnp.where(delta >= 0, 0.0, NEG).astype(F32)
    else:
        mult = jnp.zeros((nb, T, T), F32)
        for window, dil in DILATED_PATTERNS:
            ok = (delta >= 0) & (delta <= window) & (delta % dil == 0)
            mult = mult + ok.astype(F32)
        bias = jnp.where(mult > 0, jnp.log(jnp.maximum(mult, 1.0)), NEG).astype(F32)
    return bias


def _attn_fwd(q, k, v, table_t, full_range, name):
    H, S, _ = q.shape
    T = ATT_T
    nb = table_t.shape[0]
    HB = ATT_FWD_HEADS
    KC = ATT_CHUNK
    chunks = [slice(c, c + KC) for c in range(0, T, KC)]
    fold = lambda x, op: functools.reduce(op, [x[r:r + 8] for r in range(0, KC, 8)])

    def body(q_ref, k_ref, v_ref, tab_ref, o_ref, lse_ref, *scratch):
        st_refs, pt_refs, acc_refs = scratch[:HB], scratch[HB:2 * HB], scratch[2 * HB:]
        i = pl.program_id(1)
        rows = lambda j: pl.ds(pl.multiple_of(j * T, T), T)

        def scores(hh, j):
            st_refs[hh][...] = _dot_nt(k_ref[hh, rows(j), :], q_ref[hh])

        def block(j, b, nxt, stats):
            out = []
            for hh, (m, l) in enumerate(stats):
                st_ref, pt_ref, acc_ref = st_refs[hh], pt_refs[hh], acc_refs[hh]
                mx = None
                for ch in chunks:
                    x = st_ref[ch, :]
                    if b is not None:
                        x = x + tab_ref[b, ch, :]
                        st_ref[ch, :] = x
                    x = fold(x, jnp.maximum)
                    mx = x if mx is None else jnp.maximum(mx, x)
                m_new = jnp.maximum(m, jnp.max(mx, axis=0, keepdims=True))
                alpha = jnp.exp(m - m_new)
                ls = None
                for ch in chunks:
                    pc = jnp.exp(st_ref[ch, :] - m_new)
                    pt_ref[ch, :] = pc.astype(BF16)
                    pc = fold(pc, jnp.add)
                    ls = pc if ls is None else ls + pc
                if nxt is not None:
                    scores(hh, nxt)
                acc_ref[...] = alpha * acc_ref[...] + _dot_tn(v_ref[hh, rows(j), :], pt_ref[...])
                out.append((m_new, alpha * l + jnp.sum(ls, axis=0, keepdims=True)))
            return tuple(out)

        lo = 0 if full_range else jnp.maximum(i - (nb - 1), 0)
        for hh in range(HB):
            acc_refs[hh][...] = jnp.zeros_like(acc_refs[hh])
            scores(hh, lo)
        stats = lax.fori_loop(lo, i, lambda j, st: block(j, None if full_range else i - j, j + 1, st),
                              ((jnp.full((1, T), NEG, F32), jnp.zeros((1, T), F32)),) * HB)
        stats = block(i, 0, None, stats)
        for hh, (m, l) in enumerate(stats):
            o_ref[hh] = (acc_refs[hh][...] * (1.0 / l)).T
            lse_ref[hh, 0] = m + jnp.log(l)

    return pl.pallas_call(
        body, name=name, grid=(H // HB, S // T),
        in_specs=[pl.BlockSpec((HB, T, HEAD_PAD), lambda h, i: (h, i, 0)),
                  pl.BlockSpec((HB, S, HEAD_PAD), lambda h, i: (h, 0, 0)),
                  pl.BlockSpec((HB, S, HEAD_PAD), lambda h, i: (h, 0, 0)),
                  pl.BlockSpec((nb, T, T), lambda h, i: (0, 0, 0))],
        out_specs=[pl.BlockSpec((HB, T, HEAD_PAD), lambda h, i: (h, i, 0)),
                   pl.BlockSpec((HB, 1, 1, T), lambda h, i: (h, i, 0, 0))],
        out_shape=[jax.ShapeDtypeStruct((H, S, HEAD_PAD), F32), jax.ShapeDtypeStruct((H, S // T, 1, T), F32)],
        scratch_shapes=([pltpu.VMEM((T, T), F32)] * HB + [pltpu.VMEM((T, T), BF16)] * HB
                        + [pltpu.VMEM((HEAD_PAD, T), F32)] * HB),
        compiler_params=_cp("parallel", "arbitrary"),
    )(q, k, v, table_t)


def _attn_bwd(q, k, v, do, lse, dd, table_t, full_range, name):
    H, S, _ = q.shape
    T = ATT_T
    nq = S // T
    nb = table_t.shape[0]
    HB = ATT_BWD_HEADS
    KC = ATT_CHUNK
    chunks = [slice(c, c + KC) for c in range(0, T, KC)]

    def body(q_ref, do_ref, lse_ref, dd_ref, k_ref, v_ref, tab_ref, dq_hbm, dk_ref, dv_ref, *scratch):
        st_refs, dpt_refs, pt_refs, dst_refs = (scratch[n * HB:(n + 1) * HB] for n in range(4))
        dq_ref, dq_sem = scratch[4 * HB:]
        h = pl.program_id(0)
        j = pl.program_id(1)

        @pl.when(j == 0)
        def _():
            dq_ref[...] = jnp.zeros_like(dq_ref)

        dk_ref[...] = jnp.zeros_like(dk_ref)
        dv_ref[...] = jnp.zeros_like(dv_ref)

        def step(i, b):
            rows = pl.ds(pl.multiple_of(i * T, T), T)
            for hh in range(HB):
                st_refs[hh][...] = _dot_nt(k_ref[hh], q_ref[hh, rows, :])
                dpt_refs[hh][...] = _dot_nt(v_ref[hh], do_ref[hh, rows, :])
            for hh in range(HB):
                lse_i = lse_ref[hh, i]
                dd_i = dd_ref[hh, i]
                for ch in chunks:
                    x = st_refs[hh][ch, :]
                    if b is not None:
                        x = x + tab_ref[b, ch, :]
                    pc = jnp.exp(x - lse_i)
                    pt_refs[hh][ch, :] = pc.astype(BF16)
                    dst_refs[hh][ch, :] = (pc * (dpt_refs[hh][ch, :] - dd_i)).astype(BF16)
                dv_ref[hh] += _dot(pt_refs[hh][...], do_ref[hh, rows, :])
                dk_ref[hh] += _dot(dst_refs[hh][...], q_ref[hh, rows, :])
                dq_ref[hh, rows, :] += _dot_tn(dst_refs[hh][...], k_ref[hh])

        step(j, 0)
        if full_range:
            pl.loop(j + 1, nq)(lambda i: step(i, None))
        else:
            pl.loop(j + 1, jnp.minimum(j + nb, nq))(lambda i: step(i, i - j))

        @pl.when(j == nq - 1)
        def _():
            out = pltpu.make_async_copy(dq_ref, dq_hbm.at[pl.ds(h * HB, HB)], dq_sem)
            out.start()
            out.wait()

    once = dict(pipeline_mode=pl.Buffered(1))
    per_head = pl.BlockSpec((HB, S, HEAD_PAD), lambda h, j: (h, 0, 0), **once)
    rows = pl.BlockSpec((HB, nq, 1, T), lambda h, j: (h, 0, 0, 0))
    blk = pl.BlockSpec((HB, T, HEAD_PAD), lambda h, j: (h, j, 0))
    shp = jax.ShapeDtypeStruct((H, S, HEAD_PAD), F32)
    return pl.pallas_call(
        body, name=name, grid=(H // HB, nq),
        in_specs=[per_head, per_head, rows, rows, blk, blk,
                  pl.BlockSpec((nb, T, T), lambda h, j: (0, 0, 0), **once)],
        out_specs=[pl.BlockSpec(memory_space=pltpu.HBM), blk, blk], out_shape=[shp, shp, shp],
        scratch_shapes=([pltpu.VMEM((T, T), F32)] * (2 * HB) + [pltpu.VMEM((T, T), BF16)] * (2 * HB)
                        + [pltpu.VMEM((HB, S, HEAD_PAD), F32), pltpu.SemaphoreType.DMA]),
        compiler_params=_cp("parallel", "arbitrary"),
    )(q, do, lse, dd, k, v, table_t)


def _mid_bwd(dh2, h1, e, gate, g2, w_pg_t, w_out_t, oa, ob, zm):
    S = dh2.shape[0]

    def body(dh2_ref, h1_ref, e_ref, gate_ref, g2_ref, wg_ref, wo_ref, oa_ref, ob_ref, ga_ref, gb_ref,
             dh1_ref, dh1b_ref, de_ref, dpre_ref, doa_ref, dob_ref, dga_ref, dgb_ref, dda_ref, ddb_ref, dg2_ref):
        @pl.when(pl.program_id(0) == 0)
        def _():
            dg2_ref[...] = jnp.zeros_like(dg2_ref)

        lane = lax.broadcasted_iota(jnp.int32, (TOK_T, HEAD_PAD), 1)
        lo_half = lane < HEAD_DIM
        dh2 = dh2_ref[...]
        gate = gate_ref[...]
        de_ref[...] = (dh2 * gate).astype(BF16)
        dpre = (dh2 * e_ref[...] * gate * (1.0 - gate)).astype(BF16)
        dpre_ref[...] = dpre
        du2 = _dot(dpre, wg_ref[...])
        h1 = h1_ref[...]
        r = lax.rsqrt(jnp.mean(h1 * h1, axis=-1, keepdims=True) + EPS)
        xh = h1 * r
        a = du2 * g2_ref[...]
        dh1 = dh2 + r * (a - xh * jnp.mean(a * xh, axis=-1, keepdims=True))
        dg2_ref[...] += jnp.sum(du2 * xh, axis=0, keepdims=True)
        dh1_ref[...] = dh1
        dh1b = dh1.astype(BF16)
        dh1b_ref[...] = dh1b
        dy = _dot(dh1b, wo_ref[...])
        for bi, (o_ref, g_ref, do_ref, dg_ref, dd_ref) in enumerate(
                ((oa_ref, ga_ref, doa_ref, dga_ref, dda_ref), (ob_ref, gb_ref, dob_ref, dgb_ref, ddb_ref))):
            for pi in range(N_HEADS // 2):
                col = bi * D_BRANCH + HEAD_PAD * pi
                dyp = dy[:, col:col + HEAD_PAD]
                g = g_ref[:, HEAD_PAD * pi:HEAD_PAD * (pi + 1)]
                sg = _sigmoid(g)
                dg_ref[:, HEAD_PAD * pi:HEAD_PAD * (pi + 1)] = (
                    dyp * _pair(o_ref, pi, lo_half) * (sg * (1.0 + g * (1.0 - sg)))).astype(BF16)
                dop = dyp * (g * sg)
                for hh, d_head in ((2 * pi, dop), (2 * pi + 1, pltpu.roll(dop, HEAD_DIM, 1))):
                    d_head = jnp.where(lo_half, d_head, 0.0)
                    do_ref[hh] = d_head.astype(BF16)
                    dd_ref[hh] = jnp.sum(d_head * o_ref[hh], axis=1, keepdims=True)

    tok = lambda w: pl.BlockSpec((TOK_T, w), lambda i: (i, 0))
    head = pl.BlockSpec((N_HEADS, TOK_T, HEAD_PAD), lambda i: (0, i, 0))
    head1 = pl.BlockSpec((N_HEADS, TOK_T, 1), lambda i: (0, i, 0))
    full = lambda a, b: pl.BlockSpec((a, b), lambda i: (0, 0))
    act = lambda w, dt: jax.ShapeDtypeStruct((S, w), dt)
    hshape = lambda w, dt: jax.ShapeDtypeStruct((N_HEADS, S, w), dt)
    return pl.pallas_call(
        body, name="mid_bwd", grid=(S // TOK_T,),
        in_specs=[tok(D_MODEL)] * 4 + [full(1, D_MODEL), full(D_MODEL, D_MODEL), full(D_MODEL, D_MODEL), head, head,
                                      pl.BlockSpec((TOK_T, D_BRANCH), lambda i: (i, 3)),
                                      pl.BlockSpec((TOK_T, D_BRANCH), lambda i: (i, 7))],
        out_specs=[tok(D_MODEL)] * 4 + [head, head, tok(D_BRANCH), tok(D_BRANCH), head1, head1, full(1, D_MODEL)],
        out_shape=[act(D_MODEL, F32), act(D_MODEL, BF16), act(D_MODEL, BF16), act(D_MODEL, BF16),
                   hshape(HEAD_PAD, BF16), hshape(HEAD_PAD, BF16), act(D_BRANCH, BF16), act(D_BRANCH, BF16),
                   hshape(1, F32), hshape(1, F32), jax.ShapeDtypeStruct((1, D_MODEL), F32)],
        compiler_params=_cp("arbitrary"),
    )(dh2, h1, e, gate, g2, w_pg_t, w_out_t, oa, ob, zm, zm)


def _prep_bwd(dqa, dka, dva, dqb, dkb, dvb, zm, qkg, rope_c, rope_a, rope_b, dga, dgb):
    S = zm.shape[0]

    def body(dqa_ref, dka_ref, dva_ref, dqb_ref, dkb_ref, dvb_ref, z_ref, g_ref, rc_ref, ra_ref, rb_ref,
             dga_ref, dgb_ref, dz_ref, dc_ref, dqkg_ref):
        @pl.when(pl.program_id(0) == 0)
        def _():
            dqkg_ref[...] = jnp.zeros_like(dqkg_ref)

        lane = lax.broadcasted_iota(jnp.int32, (TOK_T, HEAD_PAD), 1)
        lo_half = lane < HEAD_DIM
        rc, ra, rb = rc_ref[...], ra_ref[...], rb_ref[...]

        def unrope(dy):
            return dy * rc + pltpu.roll(dy * ra, ROPE_HALF, 1) + pltpu.roll(dy * rb, HEAD_PAD - ROPE_HALF, 1)

        def norm_bwd(col, gi, dy):
            x = z_ref[:, col:col + HEAD_PAD]
            r = _pair_rsqrt(x, lo_half)
            xh = x * r
            dqkg_ref[gi:gi + 1, :] += jnp.sum(dy * xh, axis=0, keepdims=True)
            a = dy * g_ref[gi:gi + 1, :]
            dz_ref[:, col:col + HEAD_PAD] = (r * (a - xh * _pair_mean(a * xh, lo_half))).astype(BF16)

        dc = jnp.zeros((TOK_T, HEAD_PAD), F32)
        for pi in range(N_HEADS // 2):
            col = HEAD_PAD * pi
            norm_bwd(col, 0, _pair(dqa_ref, pi, lo_half) * SCALE)
            norm_bwd(D_BRANCH + col, 1, _pair(dka_ref, pi, lo_half))
            dz_ref[:, 2 * D_BRANCH + col:2 * D_BRANCH + col + HEAD_PAD] = _pair(dva_ref, pi, lo_half).astype(BF16)
            norm_bwd(4 * D_BRANCH + col, 2, unrope(_pair(dqb_ref, pi, lo_half) * SCALE))
            norm_bwd(5 * D_BRANCH + col, 3, unrope(_pair(dkb_ref, pi, lo_half)))
            dz_ref[:, 6 * D_BRANCH + col:6 * D_BRANCH + col + HEAD_PAD] = _pair(dvb_ref, pi, lo_half).astype(BF16)
            for hh in (2 * pi, 2 * pi + 1):
                dch = dka_ref[hh][:, HEAD_DIM:HEAD_DIM + 1] + dqa_ref[hh][:, HEAD_DIM + 3:HEAD_DIM + 4]
                dc = dc + jnp.where(lane == hh, dch, 0.0)
        dz_ref[:, 3 * D_BRANCH:4 * D_BRANCH] = dga_ref[...]
        dz_ref[:, 7 * D_BRANCH:8 * D_BRANCH] = dgb_ref[...]
        dc_ref[...] = dc

    tok = lambda w: pl.BlockSpec((TOK_T, w), lambda i: (i, 0))
    head = pl.BlockSpec((N_HEADS, TOK_T, HEAD_PAD), lambda i: (0, i, 0))
    return pl.pallas_call(
        body, name="prep_bwd", grid=(S // TOK_T,),
        in_specs=[head] * 6 + [tok(N_MAIN), pl.BlockSpec((4, 128), lambda i: (0, 0)), tok(128), tok(128), tok(128),
                               tok(D_BRANCH), tok(D_BRANCH)],
        out_specs=[tok(N_MAIN), tok(128), pl.BlockSpec((4, 128), lambda i: (0, 0))],
        out_shape=[jax.ShapeDtypeStruct((S, N_MAIN), BF16), jax.ShapeDtypeStruct((S, 128), F32),
                   jax.ShapeDtypeStruct((4, 128), F32)],
        compiler_params=_cp("arbitrary"),
    )(dqa, dka, dva, dqb, dkb, dvb, zm, qkg, rope_c, rope_a, rope_b, dga, dgb)


def _inproj_bwd(dzm, dzf, wm_t, wf_t, h0, dh1, g):
    S = h0.shape[0]

    def body(dzm_ref, dzf_ref, wm_ref, wf_ref, h_ref, dh1_ref, g_ref, dh0_ref, dg_ref):
        @pl.when(pl.program_id(0) == 0)
        def _():
            dg_ref[...] = jnp.zeros_like(dg_ref)

        du = _dot(dzm_ref[...], wm_ref[...]) + _dot(dzf_ref[...], wf_ref[...])
        x = h_ref[...]
        r = lax.rsqrt(jnp.mean(x * x, axis=-1, keepdims=True) + EPS)
        xh = x * r
        a = du * g_ref[...]
        dh0_ref[...] = dh1_ref[...] + r * (a - xh * jnp.mean(a * xh, axis=-1, keepdims=True))
        dg_ref[...] += jnp.sum(du * xh, axis=0, keepdims=True)

    tok = lambda w: pl.BlockSpec((TOK_T, w), lambda i: (i, 0))
    full = lambda a, b: pl.BlockSpec((a, b), lambda i: (0, 0))
    return pl.pallas_call(
        body, name="inproj_bwd", grid=(S // TOK_T,),
        in_specs=[tok(N_MAIN), tok(128), full(N_MAIN, D_MODEL), full(128, D_MODEL), tok(D_MODEL), tok(D_MODEL),
                  full(1, D_MODEL)],
        out_specs=[tok(D_MODEL), full(1, D_MODEL)],
        out_shape=[jax.ShapeDtypeStruct((S, D_MODEL), F32), jax.ShapeDtypeStruct((1, D_MODEL), F32)],
        compiler_params=_cp("arbitrary"),
    )(dzm, dzf, wm_t, wf_t, h0, dh1, g)


def _wgrad(a, b, name):
    S, M = a.shape
    N = b.shape[1]
    tn = min(N, 2048)
    ts = 512
    last = S // ts - 1

    def body(a_ref, b_ref, o_ref, acc_ref):
        @pl.when(pl.program_id(1) == 0)
        def _():
            acc_ref[...] = jnp.zeros_like(acc_ref)

        acc_ref[...] += _dot_tn(a_ref[...].astype(BF16), b_ref[...])

        @pl.when(pl.program_id(1) == last)
        def _():
            o_ref[...] = acc_ref[...].astype(BF16)

    return pl.pallas_call(
        body, name=name, grid=(N // tn, S // ts),
        in_specs=[pl.BlockSpec((ts, M), lambda n, s: (s, 0)), pl.BlockSpec((ts, tn), lambda n, s: (s, n))],
        out_specs=pl.BlockSpec((M, tn), lambda n, s: (0, n)),
        out_shape=jax.ShapeDtypeStruct((M, N), BF16),
        scratch_shapes=[pltpu.VMEM((M, tn), F32)],
        compiler_params=_cp("parallel", "arbitrary"),
    )(a, b)


def _rope_tables(positions):
    inv_freq = ROPE_THETA ** (-jnp.arange(ROPE_HALF, dtype=F32) / ROPE_HALF)
    ang = positions.astype(F32)[:, None] * inv_freq
    cos, sin = jnp.cos(ang), jnp.sin(ang)
    S = positions.shape[0]
    one, zero = jnp.ones((S, HEAD_DIM - 2 * ROPE_HALF), F32), jnp.zeros((S, HEAD_DIM - 2 * ROPE_HALF), F32)
    z8 = jnp.zeros((S, ROPE_HALF), F32)
    rc = jnp.concatenate([cos, cos, one], axis=1)
    ra = jnp.concatenate([-sin, z8, zero], axis=1)
    rb = jnp.concatenate([z8, sin, zero], axis=1)
    return tuple(jnp.tile(t, (1, 2)) for t in (rc, ra, rb))


def _layer_weights(w_in, w_out, w_ple, w_pg):
    w_in = w_in.astype(BF16)
    wm = jnp.concatenate([w_in[:, :4 * D_BRANCH], w_in[:, 4 * D_BRANCH + N_HEADS:]], axis=1)
    wf = jnp.pad(w_in[:, 4 * D_BRANCH:4 * D_BRANCH + N_HEADS], ((0, 0), (0, 128 - N_HEADS)))
    w_out, w_ple, w_pg = w_out.astype(BF16), w_ple.astype(BF16), w_pg.astype(BF16)
    return dict(wm=wm, wf=wf, wm_t=wm.T, wf_t=wf.T, w_out=w_out, w_out_t=w_out.T, w_ple=w_ple, w_pg=w_pg,
                w_pg_t=w_pg.T)


def _row(v, width=128):
    v = v.reshape(1, -1).astype(F32)
    return jnp.pad(v, ((0, 0), (0, width - v.shape[1])))


def _layer_fwd(h0, p, rope, tabs, w, norm_g, b_f, qk_g, ple_g):
    S = h0.shape[0]
    g1 = norm_g.reshape(1, D_MODEL)
    g2 = ple_g.reshape(1, D_MODEL)
    qkg = jnp.tile(qk_g, (1, 2))
    bf = _row(b_f)
    zm, zf, u = _inproj_fwd(h0, g1, w["wm"], w["wf"])
    c = _forget_cumsum(zf, bf, tabs["tril"])
    qa, ka, va, qb, kb, vb = _prep_fwd(zm, c, qkg, *rope)
    oa, lse_a = _attn_fwd(qa, ka, va, tabs["fox"], True, "fox_fwd")
    ob, lse_b = _attn_fwd(qb, kb, vb, tabs["dil"], False, "dil_fwd")
    y, h1, h2, u2, e, gate = _mid_fwd(oa, ob, zm, h0, p, w["w_out"], w["w_pg"], w["w_ple"], g2)
    saved = dict(h0=h0, p=p, zm=zm, zf=zf, u=u, qa=qa, ka=ka, va=va, qb=qb, kb=kb, vb=vb, oa=oa, ob=ob,
                 lse_a=lse_a, lse_b=lse_b, y=y, h1=h1, u2=u2, e=e, gate=gate, g1=g1, g2=g2, qkg=qkg, bf=bf)
    return h2, saved


def _layer_bwd(dh2, sv, rope, tabs, w):
    S = dh2.shape[0]
    nq = S // ATT_T
    rows = lambda a: a.reshape(N_HEADS, nq, 1, ATT_T)
    (dh1, dh1b, de, dpre, doa, dob, dga, dgb, dda, ddb, dg2) = _mid_bwd(
        dh2, sv["h1"], sv["e"], sv["gate"], sv["g2"], w["w_pg_t"], w["w_out_t"], sv["oa"], sv["ob"], sv["zm"])
    dqa, dka, dva = _attn_bwd(sv["qa"], sv["ka"], sv["va"], doa, sv["lse_a"], rows(dda), tabs["fox"], True,
                              "fox_bwd")
    dqb, dkb, dvb = _attn_bwd(sv["qb"], sv["kb"], sv["vb"], dob, sv["lse_b"], rows(ddb), tabs["dil"], False,
                              "dil_bwd")
    dzm, dc, dqkg = _prep_bwd(dqa, dka, dva, dqb, dkb, dvb, sv["zm"], sv["qkg"], *rope, dga, dgb)
    dzf, dbf = _forget_bwd(dc, sv["zf"], sv["bf"], tabs["triu"])
    dh0, dg1 = _inproj_bwd(dzm, dzf, w["wm_t"], w["wf_t"], sv["h0"], dh1, sv["g1"])
    dwm = _wgrad(sv["u"], dzm, "wgrad_in")
    dwf = _wgrad(sv["u"], dzf, "wgrad_f")
    dw_in = jnp.concatenate([dwm[:, :4 * D_BRANCH], dwf[:, :N_HEADS], dwm[:, 4 * D_BRANCH:]], axis=1)
    grads = dict(
        norm_g=dg1.reshape(D_MODEL), w_in=dw_in, b_f=dbf[0, :N_HEADS],
        qk_norm_g=dqkg[:, :HEAD_DIM] + dqkg[:, HEAD_DIM:],
        w_out=_wgrad(sv["y"], dh1b, "wgrad_out"), w_ple=_wgrad(sv["p"], de, "wgrad_ple"),
        ple_norm_g=dg2.reshape(D_MODEL), w_ple_gate=_wgrad(sv["u2"], dpre, "wgrad_gate"))
    return dh0, grads


def _tables():
    T = CUM_T
    r = lax.broadcasted_iota(jnp.int32, (T, T), 0)
    c = lax.broadcasted_iota(jnp.int32, (T, T), 1)
    return dict(fox=_bias_tables(True), dil=_bias_tables(False),
                tril=(c <= r).astype(BF16), triu=(c >= r).astype(BF16))


def _local_step(x, p, positions, target, layers, small):
    rope = _rope_tables(positions)
    tabs = _tables()
    ws = [_layer_weights(*lw) for lw in layers]
    h = x
    saved = []
    for w, lp, sm in zip(ws, p, small):
        h, sv = _layer_fwd(h, lp, rope, tabs, w, *sm)
        saved.append(sv)
    dh, loss = _loss_fwd_bwd(h, target)
    grads = [None] * len(ws)
    for li in reversed(range(len(ws))):
        dh, grads[li] = _layer_bwd(dh, saved[li], rope, tabs, ws[li])
    return loss[0, 0], dh, grads


def _peers():
    x, y, c = lax.axis_index("x"), lax.axis_index("y"), lax.axis_index("c")
    me = 4 * x + 2 * y + c
    flip = lambda v, bit: 1 - v if bit else v
    return me, [(flip(x, k & 4), flip(y, k & 2), flip(c, k & 1)) for k in range(1, N_DEV)]


def _shard(ref, axis, d, size):
    return ref.at[(slice(None),) * axis + (pl.ds(pl.multiple_of(d * size, size), size),)]


def _exchange(name, arrays, out_shapes, src_of, dst_of):
    n = len(arrays)

    def body(*refs):
        ins, outs = refs[:n], refs[n:2 * n]
        send_sems, recv_sems, local_sems = refs[2 * n:]
        me, peers = _peers()
        local = [pltpu.make_async_copy(src_of(a, ins[a], me), dst_of(a, outs[a], me), local_sems.at[a])
                 for a in range(n)]
        for cp in local:
            cp.start()
        copies = [pltpu.make_async_remote_copy(
            src_ref=src_of(a, ins[a], 4 * px + 2 * py + pc), dst_ref=dst_of(a, outs[a], me),
            send_sem=send_sems.at[k, a], recv_sem=recv_sems.at[k, a],
            device_id=(px, py, pc), device_id_type=pl.DeviceIdType.MESH)
            for k, (px, py, pc) in enumerate(peers) for a in range(n)]
        for cp in copies:
            cp.start()
        for cp in copies:
            cp.wait()
        for cp in local:
            cp.wait()

    hbm = pl.BlockSpec(memory_space=pltpu.HBM)
    return pl.pallas_call(
        body, name=name, in_specs=[hbm] * n, out_specs=[hbm] * n, out_shape=out_shapes,
        scratch_shapes=[pltpu.SemaphoreType.DMA((N_DEV - 1, n)), pltpu.SemaphoreType.DMA((N_DEV - 1, n)),
                        pltpu.SemaphoreType.DMA((n,))],
    )(*arrays)


_SHARD_AXES = (None, (1, 128), (2, 128), (1, 128))


def _gather_weights(shards):
    full = [(N_DEV,) + shards[0].shape, (2, D_MODEL, D_MODEL), (2, PLE_DIM, D_MODEL), (2, D_MODEL, D_MODEL)]
    return _exchange(
        "gather_weights", shards, [jax.ShapeDtypeStruct(f, BF16) for f in full],
        src_of=lambda a, ref, peer: ref,
        dst_of=lambda a, ref, me: ref.at[me] if a == 0 else _shard(ref, _SHARD_AXES[a][0], me, _SHARD_AXES[a][1]))


def _exchange_grads(partials, small):
    shard_shapes = [partials[0].shape[1:], (2, 128, D_MODEL), (2, PLE_DIM, 128), (2, 128, D_MODEL), small.shape]

    def src_of(a, ref, peer):
        if a == 0:
            return ref.at[peer]
        return ref if a == len(partials) else _shard(ref, _SHARD_AXES[a][0], peer, _SHARD_AXES[a][1])

    arrays = list(partials) + [small]
    return _exchange(
        "exchange_grads", arrays, [jax.ShapeDtypeStruct((N_DEV,) + s, a.dtype) for s, a in zip(shard_shapes, arrays)],
        src_of=src_of, dst_of=lambda a, ref, me: ref.at[me])


def _adamw(name, parts, w, m, v, rows):
    L, R, C = w.shape

    def body(p_ref, w_ref, m_ref, v_ref, g_ref, d_ref, nm_ref, nv_ref):
        g = p_ref[0, 0].astype(F32)
        for s in range(1, N_DEV):
            g = g + p_ref[s, 0].astype(F32)
        g_ref[0] = g
        nm = ADAM_B1 * m_ref[0] + (1.0 - ADAM_B1) * g
        nv = ADAM_B2 * v_ref[0] + (1.0 - ADAM_B2) * (g * g)
        nm_ref[0] = nm
        nv_ref[0] = nv
        m_hat = nm / (1.0 - ADAM_B1 ** ADAM_STEP)
        v_hat = nv / (1.0 - ADAM_B2 ** ADAM_STEP)
        d_ref[0] = -ADAM_LR * (m_hat / (jnp.sqrt(v_hat) + ADAM_EPS) + ADAM_WD * w_ref[0])

    blk = pl.BlockSpec((1, rows, C), lambda l, i: (l, i, 0))
    shp = jax.ShapeDtypeStruct((L, R, C), F32)
    return pl.pallas_call(
        body, name=name, grid=(L, R // rows),
        in_specs=[pl.BlockSpec((N_DEV, 1, rows, C), lambda l, i: (0, l, i, 0)), blk, blk, blk],
        out_specs=[blk] * 4, out_shape=[shp] * 4,
        compiler_params=_cp("parallel", "parallel"),
    )(parts, w, m, v)


SMALL_ROWS = 40
LOSS_ROW = 37


def _pack_small(norm_g, ple_g, qk_g, b_f, last_row):
    rows = lambda a: a.astype(F32).reshape(-1, 128)
    flat = jnp.concatenate([rows(norm_g), rows(ple_g), rows(qk_g), _row(b_f.reshape(-1)), last_row], axis=0)
    return jnp.pad(flat, ((0, SMALL_ROWS - flat.shape[0]), (0, 0)))


def _unpack_small(flat):
    return (flat[0:16].reshape(2, D_MODEL), flat[16:32].reshape(2, D_MODEL), flat[32:36].reshape(2, 4, HEAD_DIM),
            flat[36, :2 * N_HEADS].reshape(2, N_HEADS))


def kernel(x, p, positions, norm_g, w_in, b_f, qk_norm_g, w_out, w_ple, ple_norm_g, w_ple_gate, loss_target, m_norm_g, m_w_in, m_b_f, m_qk_norm_g, m_w_out, m_w_ple, m_ple_norm_g, m_w_ple_gate, v_norm_g, v_w_in, v_b_f, v_qk_norm_g, v_w_out, v_w_ple, v_ple_norm_g, v_w_ple_gate):
    g_in, full_out, full_ple, full_pg = _gather_weights(
        [w_in.astype(BF16).reshape(W_IN_ROWS, 128)] + [a.astype(BF16) for a in (w_out, w_ple, w_ple_gate)])
    full_in = g_in.reshape(N_DEV, 2, D_MODEL, W_IN_SHARD).transpose(1, 2, 0, 3).reshape(2, D_MODEL, N_IN)
    layers = [(full_in[l], full_out[l], full_ple[l], full_pg[l]) for l in range(2)]
    small = [(norm_g[l], b_f[l], qk_norm_g[l], ple_norm_g[l]) for l in range(2)]
    loss, dx, grads = _local_step(x[0], p[:, 0], positions[0], loss_target[0], layers, small)

    stack = lambda name: jnp.stack([gl[name] for gl in grads], axis=0)
    d_in = stack("w_in").reshape(2, D_MODEL, N_DEV, W_IN_SHARD).transpose(2, 0, 1, 3).reshape(N_DEV, W_IN_ROWS, 128)
    small_part = _pack_small(stack("norm_g"), stack("ple_norm_g"), stack("qk_norm_g"), stack("b_f"),
                             _row(loss.reshape(1)))
    r_in, r_out, r_ple, r_pg, r_small = _exchange_grads([d_in, stack("w_out"), stack("w_ple"), stack("w_ple_gate")],
                                                        small_part)

    zero_row = jnp.zeros((1, 128), F32)
    small_of = lambda ng, pg, qk, bf: _pack_small(ng, pg, qk, bf, zero_row)[None]
    flat = lambda a: a.reshape(1, W_IN_ROWS, 128)
    outs = dict(
        w_in=[o.reshape(w_in.shape) for o in
              _adamw("adamw_in", r_in[:, None], flat(w_in), flat(m_w_in), flat(v_w_in), W_IN_TILE)],
        w_out=_adamw("adamw_out", r_out, w_out, m_w_out, v_w_out, 128),
        w_ple=_adamw("adamw_ple", r_ple, w_ple, m_w_ple, v_w_ple, 256),
        w_pg=_adamw("adamw_gate", r_pg, w_ple_gate, m_w_ple_gate, v_w_ple_gate, 128),
        small=_adamw("adamw_small", r_small[:, None], small_of(norm_g, ple_norm_g, qk_norm_g, b_f),
                     small_of(m_norm_g, m_ple_norm_g, m_qk_norm_g, m_b_f),
                     small_of(v_norm_g, v_ple_norm_g, v_qk_norm_g, v_b_f), SMALL_ROWS))
    leaves = []
    for kind in range(4):
        ng, pg, qk, bf = _unpack_small(outs["small"][kind][0])
        leaves += [ng, outs["w_in"][kind], bf, qk, outs["w_out"][kind], outs["w_ple"][kind], pg, outs["w_pg"][kind]]
    return (outs["small"][0][0, LOSS_ROW, 0], dx[None], *leaves)
```

```python
import functools

import jax
import jax.numpy as jnp
from jax import lax
from jax.experimental import pallas as pl
from jax.experimental.pallas import tpu as pltpu

F32 = jnp.float32
BF16 = jnp.bfloat16

D_MODEL = 1024
HEAD_DIM = 64
N_HEADS = 8
HEAD_PAD = 128
D_BRANCH = N_HEADS * HEAD_DIM
N_MAIN = 8 * D_BRANCH
N_IN = N_MAIN + N_HEADS
PLE_DIM = 256
ROPE_THETA = 500000.0
ROPE_HALF = 8
EPS = 1e-6
NEG = -1e30
SCALE = HEAD_DIM ** -0.5
LOG2E = 1.4426950408889634
LN2 = 0.6931471805599453
DILATED_PATTERNS = ((128, 1), (512, 4), (2048, 16))
N_DEV = 8
W_IN_SHARD = N_IN // N_DEV
W_IN_ROWS = 2 * D_MODEL * W_IN_SHARD // 128
W_IN_TILE = W_IN_ROWS // 19

ADAM_LR = 0.001
ADAM_B1 = 0.9
ADAM_B2 = 0.999
ADAM_EPS = 1e-08
ADAM_WD = 0.01
ADAM_STEP = 10

ATT_T = 512
ATT_FWD_HEADS = 2
ATT_BWD_HEADS = 2
ATT_CHUNK = 32
TOK_T = 256
CUM_T = 512
VMEM_LIMIT = 56 * 1024 * 1024


def _cp(*sem):
    return pltpu.CompilerParams(dimension_semantics=sem, vmem_limit_bytes=VMEM_LIMIT)


def _sigmoid(x):
    return 1.0 / (1.0 + jnp.exp(-x))


def _split3(x):
    hi = x.astype(BF16)
    r1 = x - hi.astype(F32)
    mid = r1.astype(BF16)
    lo = (r1 - mid.astype(F32)).astype(BF16)
    return hi, mid, lo


def _dot(a, b):
    return jnp.dot(a, b, preferred_element_type=F32)


def _dot_nt(a, b):
    return lax.dot_general(a, b, (((1,), (1,)), ((), ())), preferred_element_type=F32)


def _dot_tn(a, b):
    return lax.dot_general(a, b, (((0,), (0,)), ((), ())), preferred_element_type=F32)


def _inproj_fwd(h, g, wm, wf):
    S = h.shape[0]

    def body(h_ref, g_ref, wm_ref, wf_ref, zm_ref, zf_ref, u_ref):
        x = h_ref[...]
        r = lax.rsqrt(jnp.mean(x * x, axis=-1, keepdims=True) + EPS)
        u = (x * r * g_ref[...]).astype(BF16)
        u_ref[...] = u
        zm_ref[...] = _dot(u, wm_ref[...])
        zf_ref[...] = _dot(u, wf_ref[...])

    return pl.pallas_call(
        body, name="inproj_fwd", grid=(S // TOK_T,),
        in_specs=[pl.BlockSpec((TOK_T, D_MODEL), lambda i: (i, 0)),
                  pl.BlockSpec((1, D_MODEL), lambda i: (0, 0)),
                  pl.BlockSpec((D_MODEL, N_MAIN), lambda i: (0, 0)),
                  pl.BlockSpec((D_MODEL, 128), lambda i: (0, 0))],
        out_specs=[pl.BlockSpec((TOK_T, N_MAIN), lambda i: (i, 0)),
                   pl.BlockSpec((TOK_T, 128), lambda i: (i, 0)),
                   pl.BlockSpec((TOK_T, D_MODEL), lambda i: (i, 0))],
        out_shape=[jax.ShapeDtypeStruct((S, N_MAIN), F32), jax.ShapeDtypeStruct((S, 128), F32),
                   jax.ShapeDtypeStruct((S, D_MODEL), BF16)],
        compiler_params=_cp("parallel"),
    )(h, g, wm, wf)


def _log_sigmoid(x):
    return jnp.minimum(x, 0.0) - jnp.log(1.0 + jnp.exp(-jnp.abs(x)))


def _forget_cumsum(zf, bf, tri):
    S = zf.shape[0]

    def body(zf_ref, b_ref, tri_ref, c_ref, carry):
        @pl.when(pl.program_id(0) == 0)
        def _():
            carry[...] = jnp.zeros_like(carry)

        lf = _log_sigmoid(zf_ref[...] + b_ref[...])
        hi, mid, lo = _split3(lf)
        t = tri_ref[...]
        cs = _dot(t, hi) + _dot(t, mid) + _dot(t, lo) + carry[...]
        c_ref[...] = cs
        carry[...] = cs[CUM_T - 1:CUM_T, :]

    return pl.pallas_call(
        body, name="forget_cumsum", grid=(S // CUM_T,),
        in_specs=[pl.BlockSpec((CUM_T, 128), lambda i: (i, 0)),
                  pl.BlockSpec((1, 128), lambda i: (0, 0)),
                  pl.BlockSpec((CUM_T, CUM_T), lambda i: (0, 0))],
        out_specs=pl.BlockSpec((CUM_T, 128), lambda i: (i, 0)),
        out_shape=jax.ShapeDtypeStruct((S, 128), F32),
        scratch_shapes=[pltpu.VMEM((1, 128), F32)],
        compiler_params=_cp("arbitrary"),
    )(zf, bf, tri)


def _forget_bwd(dc, zf, bf, triu):
    S = zf.shape[0]
    n = S // CUM_T

    def body(dc_ref, zf_ref, b_ref, tri_ref, dzf_ref, db_ref, carry):
        @pl.when(pl.program_id(0) == 0)
        def _():
            carry[...] = jnp.zeros_like(carry)
            db_ref[...] = jnp.zeros_like(db_ref)

        hi, mid, lo = _split3(dc_ref[...])
        t = tri_ref[...]
        dlf = _dot(t, hi) + _dot(t, mid) + _dot(t, lo) + carry[...]
        carry[...] = dlf[0:1, :]
        dfa = dlf * (1.0 - _sigmoid(zf_ref[...] + b_ref[...]))
        dzf_ref[...] = dfa.astype(BF16)
        db_ref[...] += jnp.sum(dfa, axis=0, keepdims=True)

    return pl.pallas_call(
        body, name="forget_bwd", grid=(n,),
        in_specs=[pl.BlockSpec((CUM_T, 128), lambda i: (n - 1 - i, 0)),
                  pl.BlockSpec((CUM_T, 128), lambda i: (n - 1 - i, 0)),
                  pl.BlockSpec((1, 128), lambda i: (0, 0)),
                  pl.BlockSpec((CUM_T, CUM_T), lambda i: (0, 0))],
        out_specs=[pl.BlockSpec((CUM_T, 128), lambda i: (n - 1 - i, 0)),
                   pl.BlockSpec((1, 128), lambda i: (0, 0))],
        out_shape=[jax.ShapeDtypeStruct((S, 128), BF16), jax.ShapeDtypeStruct((1, 128), F32)],
        scratch_shapes=[pltpu.VMEM((1, 128), F32)],
        compiler_params=_cp("arbitrary"),
    )(dc, zf, bf, triu)


def _pair_rsqrt(x, lo_half):
    x2 = x * x
    s0 = jnp.sum(jnp.where(lo_half, x2, 0.0), axis=1, keepdims=True)
    s1 = jnp.sum(jnp.where(lo_half, 0.0, x2), axis=1, keepdims=True)
    return lax.rsqrt(jnp.where(lo_half, s0, s1) * (1.0 / HEAD_DIM) + EPS)


def _pair_mean(x, lo_half):
    s0 = jnp.sum(jnp.where(lo_half, x, 0.0), axis=1, keepdims=True)
    s1 = jnp.sum(jnp.where(lo_half, 0.0, x), axis=1, keepdims=True)
    return jnp.where(lo_half, s0, s1) * (1.0 / HEAD_DIM)


def _prep_fwd(zm, c, qkg, rope_c, rope_a, rope_b):
    S = zm.shape[0]
    shp = jax.ShapeDtypeStruct((N_HEADS, S, HEAD_PAD), BF16)

    def body(z_ref, c_ref, g_ref, rc_ref, ra_ref, rb_ref, qa_ref, ka_ref, va_ref, qb_ref, kb_ref, vb_ref):
        lane = lax.broadcasted_iota(jnp.int32, (TOK_T, HEAD_PAD), 1)
        lo_half = lane < HEAD_DIM
        aug = (lane >= HEAD_DIM) & (lane < HEAD_DIM + 3)
        q_pad = jnp.where(aug, -1.0, 0.0)
        cs = c_ref[...]
        rc, ra, rb = rc_ref[...], ra_ref[...], rb_ref[...]

        def norm(col, gi):
            x = z_ref[:, col:col + HEAD_PAD]
            return x * _pair_rsqrt(x, lo_half) * g_ref[gi:gi + 1, :]

        def rope(y):
            return y * rc + pltpu.roll(y, HEAD_PAD - ROPE_HALF, 1) * ra + pltpu.roll(y, ROPE_HALF, 1) * rb

        def put(ref, pi, y, pad_even, pad_odd):
            ref[2 * pi] = jnp.where(lo_half, y, pad_even).astype(BF16)
            ref[2 * pi + 1] = jnp.where(lo_half, pltpu.roll(y, HEAD_DIM, 1), pad_odd).astype(BF16)

        def k_pad(h):
            ch = cs[:, h:h + 1] * LOG2E
            hi = ch.astype(BF16).astype(F32)
            mid = (ch - hi).astype(BF16).astype(F32)
            lo = ch - hi - mid
            ones = jnp.where(lane == HEAD_DIM + 3, 1.0, 0.0)
            return jnp.where(lane == HEAD_DIM, hi, jnp.where(lane == HEAD_DIM + 1, mid,
                                                              jnp.where(lane == HEAD_DIM + 2, lo, ones)))

        for pi in range(N_HEADS // 2):
            col = HEAD_PAD * pi
            put(qa_ref, pi, norm(col, 0) * (SCALE * LOG2E), q_pad, q_pad)
            put(ka_ref, pi, norm(D_BRANCH + col, 1), k_pad(2 * pi), k_pad(2 * pi + 1))
            put(va_ref, pi, z_ref[:, 2 * D_BRANCH + col:2 * D_BRANCH + col + HEAD_PAD], 0.0, 0.0)
            put(qb_ref, pi, rope(norm(4 * D_BRANCH + col, 2)) * (SCALE * LOG2E), 0.0, 0.0)
            put(kb_ref, pi, rope(norm(5 * D_BRANCH + col, 3)), 0.0, 0.0)
            put(vb_ref, pi, z_ref[:, 6 * D_BRANCH + col:6 * D_BRANCH + col + HEAD_PAD], 0.0, 0.0)

    tok = lambda w: pl.BlockSpec((TOK_T, w), lambda i: (i, 0))
    head = pl.BlockSpec((N_HEADS, TOK_T, HEAD_PAD), lambda i: (0, i, 0))
    return pl.pallas_call(
        body, name="prep_fwd", grid=(S // TOK_T,),
        in_specs=[tok(N_MAIN), tok(128), pl.BlockSpec((4, 128), lambda i: (0, 0)), tok(128), tok(128), tok(128)],
        out_specs=[head] * 6, out_shape=[shp] * 6,
        compiler_params=_cp("parallel"),
    )(zm, c, qkg, rope_c, rope_a, rope_b)


def _pair(ref, pi, lo_half):
    return jnp.where(lo_half, ref[2 * pi], pltpu.roll(ref[2 * pi + 1], HEAD_DIM, 1))


def _mid_fwd(oa, ob, zm, h0, p, w_out, w_pg, w_ple, g2):
    S = h0.shape[0]

    def body(oa_ref, ob_ref, ga_ref, gb_ref, h0_ref, p_ref, wo_ref, wg_ref, wp_ref, g2_ref,
             y_ref, h1_ref, h2_ref, u2_ref, e_ref, gate_ref):
        lane = lax.broadcasted_iota(jnp.int32, (TOK_T, HEAD_PAD), 1)
        lo_half = lane < HEAD_DIM
        parts = []
        for o_ref, g_ref in ((oa_ref, ga_ref), (ob_ref, gb_ref)):
            for pi in range(N_HEADS // 2):
                g = g_ref[:, HEAD_PAD * pi:HEAD_PAD * (pi + 1)]
                parts.append((_pair(o_ref, pi, lo_half) * (g * _sigmoid(g))).astype(BF16))
        y = jnp.concatenate(parts, axis=1)
        y_ref[...] = y
        h1 = h0_ref[...] + _dot(y, wo_ref[...])
        h1_ref[...] = h1
        r = lax.rsqrt(jnp.mean(h1 * h1, axis=-1, keepdims=True) + EPS)
        u2 = (h1 * r * g2_ref[...]).astype(BF16)
        u2_ref[...] = u2
        gate = _sigmoid(_dot(u2, wg_ref[...]))
        e = _dot(p_ref[...].astype(BF16), wp_ref[...])
        e_ref[...] = e
        gate_ref[...] = gate
        h2_ref[...] = h1 + e * gate

    tok = lambda w: pl.BlockSpec((TOK_T, w), lambda i: (i, 0))
    head = pl.BlockSpec((N_HEADS, TOK_T, HEAD_PAD), lambda i: (0, i, 0))
    full = lambda a, b: pl.BlockSpec((a, b), lambda i: (0, 0))
    act = lambda dt: jax.ShapeDtypeStruct((S, D_MODEL), dt)
    return pl.pallas_call(
        body, name="mid_fwd", grid=(S // TOK_T,),
        in_specs=[head, head,
                  pl.BlockSpec((TOK_T, D_BRANCH), lambda i: (i, 3)), pl.BlockSpec((TOK_T, D_BRANCH), lambda i: (i, 7)),
                  tok(D_MODEL), tok(PLE_DIM), full(D_MODEL, D_MODEL), full(D_MODEL, D_MODEL),
                  full(PLE_DIM, D_MODEL), full(1, D_MODEL)],
        out_specs=[tok(D_MODEL)] * 6,
        out_shape=[act(BF16), act(F32), act(F32), act(BF16), act(F32), act(F32)],
        compiler_params=_cp("parallel"),
    )(oa, ob, zm, zm, h0, p, w_out, w_pg, w_ple, g2)


def _loss_fwd_bwd(y, t):
    S = y.shape[0]

    def body(y_ref, t_ref, dy_ref, loss_ref):
        @pl.when(pl.program_id(0) == 0)
        def _():
            loss_ref[...] = jnp.zeros_like(loss_ref)

        err = y_ref[...] - t_ref[...]
        dy_ref[...] = err * (1.0 / D_MODEL)
        part = jnp.sum(jnp.sum(err * err, axis=1, keepdims=True), axis=0, keepdims=True)
        loss_ref[...] += part * (0.5 / D_MODEL)

    tok = pl.BlockSpec((TOK_T, D_MODEL), lambda i: (i, 0))
    return pl.pallas_call(
        body, name="loss", grid=(S // TOK_T,),
        in_specs=[tok, tok], out_specs=[tok, pl.BlockSpec((8, 128), lambda i: (0, 0))],
        out_shape=[jax.ShapeDtypeStruct((S, D_MODEL), F32), jax.ShapeDtypeStruct((8, 128), F32)],
        compiler_params=_cp("arbitrary"),
    )(y, t)


def _bias_tables(full_range):
    T = ATT_T
    nb = 1 if full_range else DILATED_PATTERNS[-1][0] // T + 1
    r = lax.broadcasted_iota(jnp.int32, (nb, T, T), 2)
    c = lax.broadcasted_iota(jnp.int32, (nb, T, T), 1)
    b = lax.broadcasted_iota(jnp.int32, (nb, T, T), 0)
    delta = T * b + r - c
    if full_range:
        bias = jnp.where(delta >= 0, 0.0, NEG).astype(F32)
    else:
        mult = jnp.zeros((nb, T, T), F32)
        for window, dil in DILATED_PATTERNS:
            ok = (delta >= 0) & (delta <= window) & (delta % dil == 0)
            mult = mult + ok.astype(F32)
        bias = jnp.where(mult > 0, jnp.log2(jnp.maximum(mult, 1.0)), NEG).astype(F32)
    return bias


def _attn_fwd(q, k, v, table_t, full_range, name):
    H, S, _ = q.shape
    T = ATT_T
    nb = table_t.shape[0]
    HB = ATT_FWD_HEADS
    KC = ATT_CHUNK
    chunks = [slice(c, c + KC) for c in range(0, T, KC)]
    fold = lambda x, op: functools.reduce(op, [x[r:r + 8] for r in range(0, KC, 8)])

    def body(q_ref, k_ref, v_ref, tab_ref, o_ref, lse_ref, *scratch):
        st_refs, pt_refs, acc_refs = scratch[:HB], scratch[HB:2 * HB], scratch[2 * HB:]
        i = pl.program_id(1)
        rows = lambda j: pl.ds(pl.multiple_of(j * T, T), T)

        def scores(hh, j):
            st_refs[hh][...] = _dot_nt(k_ref[hh, rows(j), :], q_ref[hh])

        def block(j, b, nxt, stats):
            out = []
            for hh, (m, l) in enumerate(stats):
                st_ref, pt_ref, acc_ref = st_refs[hh], pt_refs[hh], acc_refs[hh]
                mx = None
                for ch in chunks:
                    x = st_ref[ch, :]
                    if b is not None:
                        x = x + tab_ref[b, ch, :]
                        st_ref[ch, :] = x
                    x = fold(x, jnp.maximum)
                    mx = x if mx is None else jnp.maximum(mx, x)
                m_new = jnp.maximum(m, jnp.max(mx, axis=0, keepdims=True))
                alpha = jnp.exp2(m - m_new)
                ls = None
                for ch in chunks:
                    pc = jnp.exp2(st_ref[ch, :] - m_new)
                    pt_ref[ch, :] = pc.astype(BF16)
                    pc = fold(pc, jnp.add)
                    ls = pc if ls is None else ls + pc
                if nxt is not None:
                    scores(hh, nxt)
                acc_ref[...] = alpha * acc_ref[...] + _dot_tn(v_ref[hh, rows(j), :], pt_ref[...])
                out.append((m_new, alpha * l + jnp.sum(ls, axis=0, keepdims=True)))
            return tuple(out)

        lo = 0 if full_range else jnp.maximum(i - (nb - 1), 0)
        for hh in range(HB):
            acc_refs[hh][...] = jnp.zeros_like(acc_refs[hh])
            scores(hh, lo)
        stats = lax.fori_loop(lo, i, lambda j, st: block(j, None if full_range else i - j, j + 1, st),
                              ((jnp.full((1, T), NEG, F32), jnp.zeros((1, T), F32)),) * HB)
        stats = block(i, 0, None, stats)
        for hh, (m, l) in enumerate(stats):
            o_ref[hh] = (acc_refs[hh][...] * (1.0 / l)).T
            lse_ref[hh, 0] = m + jnp.log2(l)

    return pl.pallas_call(
        body, name=name, grid=(H // HB, S // T),
        in_specs=[pl.BlockSpec((HB, T, HEAD_PAD), lambda h, i: (h, i, 0)),
                  pl.BlockSpec((HB, S, HEAD_PAD), lambda h, i: (h, 0, 0)),
                  pl.BlockSpec((HB, S, HEAD_PAD), lambda h, i: (h, 0, 0)),
                  pl.BlockSpec((nb, T, T), lambda h, i: (0, 0, 0))],
        out_specs=[pl.BlockSpec((HB, T, HEAD_PAD), lambda h, i: (h, i, 0)),
                   pl.BlockSpec((HB, 1, 1, T), lambda h, i: (h, i, 0, 0))],
        out_shape=[jax.ShapeDtypeStruct((H, S, HEAD_PAD), F32), jax.ShapeDtypeStruct((H, S // T, 1, T), F32)],
        scratch_shapes=([pltpu.VMEM((T, T), F32)] * HB + [pltpu.VMEM((T, T), BF16)] * HB
                        + [pltpu.VMEM((HEAD_PAD, T), F32)] * HB),
        compiler_params=_cp("parallel", "arbitrary"),
    )(q, k, v, table_t)


def _attn_bwd(q, k, v, do, lse, dd, table_t, full_range, name):
    H, S, _ = q.shape
    T = ATT_T
    nq = S // T
    nb = table_t.shape[0]
    HB = ATT_BWD_HEADS
    KC = ATT_CHUNK
    chunks = [slice(c, c + KC) for c in range(0, T, KC)]

    def body(q_ref, do_ref, lse_ref, dd_ref, k_ref, v_ref, tab_ref, dq_hbm, dk_ref, dv_ref, *scratch):
        st_refs, dpt_refs, pt_refs, dst_refs = (scratch[n * HB:(n + 1) * HB] for n in range(4))
        dq_ref, dq_sem = scratch[4 * HB:]
        h = pl.program_id(0)
        j = pl.program_id(1)

        @pl.when(j == 0)
        def _():
            dq_ref[...] = jnp.zeros_like(dq_ref)

        dk_ref[...] = jnp.zeros_like(dk_ref)
        dv_ref[...] = jnp.zeros_like(dv_ref)

        def step(i, b):
            rows = pl.ds(pl.multiple_of(i * T, T), T)
            for hh in range(HB):
                st_refs[hh][...] = _dot_nt(k_ref[hh], q_ref[hh, rows, :])
                dpt_refs[hh][...] = _dot_nt(v_ref[hh], do_ref[hh, rows, :])
            for hh in range(HB):
                lse_i = lse_ref[hh, i]
                dd_i = dd_ref[hh, i]
                for ch in chunks:
                    x = st_refs[hh][ch, :]
                    if b is not None:
                        x = x + tab_ref[b, ch, :]
                    pc = jnp.exp2(x - lse_i)
                    pt_refs[hh][ch, :] = pc.astype(BF16)
                    dst_refs[hh][ch, :] = (pc * (dpt_refs[hh][ch, :] - dd_i)).astype(BF16)
                dv_ref[hh] += _dot(pt_refs[hh][...], do_ref[hh, rows, :])
                dk_ref[hh] += _dot(dst_refs[hh][...], q_ref[hh, rows, :])
                dq_ref[hh, rows, :] += _dot_tn(dst_refs[hh][...], k_ref[hh])

        step(j, 0)
        if full_range:
            pl.loop(j + 1, nq)(lambda i: step(i, None))
        else:
            pl.loop(j + 1, jnp.minimum(j + nb, nq))(lambda i: step(i, i - j))

        @pl.when(j == nq - 1)
        def _():
            out = pltpu.make_async_copy(dq_ref, dq_hbm.at[pl.ds(h * HB, HB)], dq_sem)
            out.start()
            out.wait()

    once = dict(pipeline_mode=pl.Buffered(1))
    per_head = pl.BlockSpec((HB, S, HEAD_PAD), lambda h, j: (h, 0, 0), **once)
    rows = pl.BlockSpec((HB, nq, 1, T), lambda h, j: (h, 0, 0, 0))
    blk = pl.BlockSpec((HB, T, HEAD_PAD), lambda h, j: (h, j, 0))
    shp = jax.ShapeDtypeStruct((H, S, HEAD_PAD), F32)
    return pl.pallas_call(
        body, name=name, grid=(H // HB, nq),
        in_specs=[per_head, per_head, rows, rows, blk, blk,
                  pl.BlockSpec((nb, T, T), lambda h, j: (0, 0, 0), **once)],
        out_specs=[pl.BlockSpec(memory_space=pltpu.HBM), blk, blk], out_shape=[shp, shp, shp],
        scratch_shapes=([pltpu.VMEM((T, T), F32)] * (2 * HB) + [pltpu.VMEM((T, T), BF16)] * (2 * HB)
                        + [pltpu.VMEM((HB, S, HEAD_PAD), F32), pltpu.SemaphoreType.DMA]),
        compiler_params=_cp("parallel", "arbitrary"),
    )(q, do, lse, dd, k, v, table_t)


def _mid_bwd(dh2, h1, e, gate, g2, w_pg, w_out, oa, ob, zm):
    S = dh2.shape[0]

    def body(dh2_ref, h1_ref, e_ref, gate_ref, g2_ref, wg_ref, wo_ref, oa_ref, ob_ref, ga_ref, gb_ref,
             dh1_ref, dh1b_ref, de_ref, dpre_ref, doa_ref, dob_ref, dga_ref, dgb_ref, dd_ref, dg2_ref):
        @pl.when(pl.program_id(0) == 0)
        def _():
            dg2_ref[...] = jnp.zeros_like(dg2_ref)

        lane = lax.broadcasted_iota(jnp.int32, (TOK_T, HEAD_PAD), 1)
        lo_half = lane < HEAD_DIM
        dh2 = dh2_ref[...]
        gate = gate_ref[...]
        de_ref[...] = (dh2 * gate).astype(BF16)
        dpre = (dh2 * e_ref[...] * gate * (1.0 - gate)).astype(BF16)
        dpre_ref[...] = dpre
        du2 = _dot_nt(dpre, wg_ref[...])
        h1 = h1_ref[...]
        r = lax.rsqrt(jnp.mean(h1 * h1, axis=-1, keepdims=True) + EPS)
        xh = h1 * r
        a = du2 * g2_ref[...]
        dh1 = dh2 + r * (a - xh * jnp.mean(a * xh, axis=-1, keepdims=True))
        dg2_ref[...] += jnp.sum(du2 * xh, axis=0, keepdims=True)
        dh1_ref[...] = dh1
        dh1b = dh1.astype(BF16)
        dh1b_ref[...] = dh1b
        dy = _dot_nt(dh1b, wo_ref[...])
        dd = jnp.zeros((TOK_T, HEAD_PAD), F32)
        for bi, (o_ref, g_ref, do_ref, dg_ref) in enumerate(
                ((oa_ref, ga_ref, doa_ref, dga_ref), (ob_ref, gb_ref, dob_ref, dgb_ref))):
            for pi in range(N_HEADS // 2):
                col = bi * D_BRANCH + HEAD_PAD * pi
                dyp = dy[:, col:col + HEAD_PAD]
                g = g_ref[:, HEAD_PAD * pi:HEAD_PAD * (pi + 1)]
                sg = _sigmoid(g)
                dg_ref[:, HEAD_PAD * pi:HEAD_PAD * (pi + 1)] = (
                    dyp * _pair(o_ref, pi, lo_half) * (sg * (1.0 + g * (1.0 - sg)))).astype(BF16)
                dop = dyp * (g * sg)
                for hh, d_head in ((2 * pi, dop), (2 * pi + 1, pltpu.roll(dop, HEAD_DIM, 1))):
                    d_head = jnp.where(lo_half, d_head, 0.0)
                    do_ref[hh] = d_head.astype(BF16)
                    dsum = jnp.sum(d_head * o_ref[hh], axis=1, keepdims=True)
                    dd = dd + jnp.where(lane == bi * N_HEADS + hh, dsum, 0.0)
        dd_ref[...] = dd.T[:2 * N_HEADS, :]

    tok = lambda w: pl.BlockSpec((TOK_T, w), lambda i: (i, 0))
    head = pl.BlockSpec((N_HEADS, TOK_T, HEAD_PAD), lambda i: (0, i, 0))
    full = lambda a, b: pl.BlockSpec((a, b), lambda i: (0, 0))
    act = lambda w, dt: jax.ShapeDtypeStruct((S, w), dt)
    hshape = lambda w, dt: jax.ShapeDtypeStruct((N_HEADS, S, w), dt)
    return pl.pallas_call(
        body, name="mid_bwd", grid=(S // TOK_T,),
        in_specs=[tok(D_MODEL)] * 4 + [full(1, D_MODEL), full(D_MODEL, D_MODEL), full(D_MODEL, D_MODEL), head, head,
                                      pl.BlockSpec((TOK_T, D_BRANCH), lambda i: (i, 3)),
                                      pl.BlockSpec((TOK_T, D_BRANCH), lambda i: (i, 7))],
        out_specs=[tok(D_MODEL)] * 4 + [head, head, tok(D_BRANCH), tok(D_BRANCH),
                                       pl.BlockSpec((2 * N_HEADS, TOK_T), lambda i: (0, i)), full(1, D_MODEL)],
        out_shape=[act(D_MODEL, F32), act(D_MODEL, BF16), act(D_MODEL, BF16), act(D_MODEL, BF16),
                   hshape(HEAD_PAD, BF16), hshape(HEAD_PAD, BF16), act(D_BRANCH, BF16), act(D_BRANCH, BF16),
                   jax.ShapeDtypeStruct((2 * N_HEADS, S), F32), jax.ShapeDtypeStruct((1, D_MODEL), F32)],
        compiler_params=_cp("arbitrary"),
    )(dh2, h1, e, gate, g2, w_pg, w_out, oa, ob, zm, zm)


def _prep_bwd(dqa, dka, dva, dqb, dkb, dvb, zm, qkg, rope_c, rope_a, rope_b, dga, dgb):
    S = zm.shape[0]

    def body(dqa_ref, dka_ref, dva_ref, dqb_ref, dkb_ref, dvb_ref, z_ref, g_ref, rc_ref, ra_ref, rb_ref,
             dga_ref, dgb_ref, dz_ref, dc_ref, dqkg_ref):
        @pl.when(pl.program_id(0) == 0)
        def _():
            dqkg_ref[...] = jnp.zeros_like(dqkg_ref)

        lane = lax.broadcasted_iota(jnp.int32, (TOK_T, HEAD_PAD), 1)
        lo_half = lane < HEAD_DIM
        rc, ra, rb = rc_ref[...], ra_ref[...], rb_ref[...]

        def unrope(dy):
            return dy * rc + pltpu.roll(dy * ra, ROPE_HALF, 1) + pltpu.roll(dy * rb, HEAD_PAD - ROPE_HALF, 1)

        def norm_bwd(col, gi, dy):
            x = z_ref[:, col:col + HEAD_PAD]
            r = _pair_rsqrt(x, lo_half)
            xh = x * r
            dqkg_ref[gi:gi + 1, :] += jnp.sum(dy * xh, axis=0, keepdims=True)
            a = dy * g_ref[gi:gi + 1, :]
            dz_ref[:, col:col + HEAD_PAD] = (r * (a - xh * _pair_mean(a * xh, lo_half))).astype(BF16)

        dc = jnp.zeros((TOK_T, HEAD_PAD), F32)
        for pi in range(N_HEADS // 2):
            col = HEAD_PAD * pi
            norm_bwd(col, 0, _pair(dqa_ref, pi, lo_half) * SCALE)
            norm_bwd(D_BRANCH + col, 1, _pair(dka_ref, pi, lo_half) * LN2)
            dz_ref[:, 2 * D_BRANCH + col:2 * D_BRANCH + col + HEAD_PAD] = _pair(dva_ref, pi, lo_half).astype(BF16)
            norm_bwd(4 * D_BRANCH + col, 2, unrope(_pair(dqb_ref, pi, lo_half) * SCALE))
            norm_bwd(5 * D_BRANCH + col, 3, unrope(_pair(dkb_ref, pi, lo_half) * LN2))
            dz_ref[:, 6 * D_BRANCH + col:6 * D_BRANCH + col + HEAD_PAD] = _pair(dvb_ref, pi, lo_half).astype(BF16)
            for hh in (2 * pi, 2 * pi + 1):
                dch = dka_ref[hh][:, HEAD_DIM:HEAD_DIM + 1] + dqa_ref[hh][:, HEAD_DIM + 3:HEAD_DIM + 4]
                dc = dc + jnp.where(lane == hh, dch, 0.0)
        dz_ref[:, 3 * D_BRANCH:4 * D_BRANCH] = dga_ref[...]
        dz_ref[:, 7 * D_BRANCH:8 * D_BRANCH] = dgb_ref[...]
        dc_ref[...] = dc

    tok = lambda w: pl.BlockSpec((TOK_T, w), lambda i: (i, 0))
    head = pl.BlockSpec((N_HEADS, TOK_T, HEAD_PAD), lambda i: (0, i, 0))
    return pl.pallas_call(
        body, name="prep_bwd", grid=(S // TOK_T,),
        in_specs=[head] * 6 + [tok(N_MAIN), pl.BlockSpec((4, 128), lambda i: (0, 0)), tok(128), tok(128), tok(128),
                               tok(D_BRANCH), tok(D_BRANCH)],
        out_specs=[tok(N_MAIN), tok(128), pl.BlockSpec((4, 128), lambda i: (0, 0))],
        out_shape=[jax.ShapeDtypeStruct((S, N_MAIN), BF16), jax.ShapeDtypeStruct((S, 128), F32),
                   jax.ShapeDtypeStruct((4, 128), F32)],
        compiler_params=_cp("arbitrary"),
    )(dqa, dka, dva, dqb, dkb, dvb, zm, qkg, rope_c, rope_a, rope_b, dga, dgb)


def _inproj_bwd(dzm, dzf, wm, wf, h0, dh1, g):
    S = h0.shape[0]

    def body(dzm_ref, dzf_ref, wm_ref, wf_ref, h_ref, dh1_ref, g_ref, dh0_ref, dg_ref):
        @pl.when(pl.program_id(0) == 0)
        def _():
            dg_ref[...] = jnp.zeros_like(dg_ref)

        du = _dot_nt(dzm_ref[...], wm_ref[...]) + _dot_nt(dzf_ref[...], wf_ref[...])
        x = h_ref[...]
        r = lax.rsqrt(jnp.mean(x * x, axis=-1, keepdims=True) + EPS)
        xh = x * r
        a = du * g_ref[...]
        dh0_ref[...] = dh1_ref[...] + r * (a - xh * jnp.mean(a * xh, axis=-1, keepdims=True))
        dg_ref[...] += jnp.sum(du * xh, axis=0, keepdims=True)

    tok = lambda w: pl.BlockSpec((TOK_T, w), lambda i: (i, 0))
    full = lambda a, b: pl.BlockSpec((a, b), lambda i: (0, 0))
    return pl.pallas_call(
        body, name="inproj_bwd", grid=(S // TOK_T,),
        in_specs=[tok(N_MAIN), tok(128), full(D_MODEL, N_MAIN), full(D_MODEL, 128), tok(D_MODEL), tok(D_MODEL),
                  full(1, D_MODEL)],
        out_specs=[tok(D_MODEL), full(1, D_MODEL)],
        out_shape=[jax.ShapeDtypeStruct((S, D_MODEL), F32), jax.ShapeDtypeStruct((1, D_MODEL), F32)],
        compiler_params=_cp("arbitrary"),
    )(dzm, dzf, wm, wf, h0, dh1, g)


def _wgrad(a, b, name):
    S, M = a.shape
    N = b.shape[1]
    tn = min(N, 2048)
    ts = 512
    last = S // ts - 1

    def body(a_ref, b_ref, o_ref, acc_ref):
        @pl.when(pl.program_id(1) == 0)
        def _():
            acc_ref[...] = jnp.zeros_like(acc_ref)

        acc_ref[...] += _dot_tn(a_ref[...].astype(BF16), b_ref[...])

        @pl.when(pl.program_id(1) == last)
        def _():
            o_ref[...] = acc_ref[...].astype(BF16)

    return pl.pallas_call(
        body, name=name, grid=(N // tn, S // ts),
        in_specs=[pl.BlockSpec((ts, M), lambda n, s: (s, 0)), pl.BlockSpec((ts, tn), lambda n, s: (s, n))],
        out_specs=pl.BlockSpec((M, tn), lambda n, s: (0, n)),
        out_shape=jax.ShapeDtypeStruct((M, N), BF16),
        scratch_shapes=[pltpu.VMEM((M, tn), F32)],
        compiler_params=_cp("parallel", "arbitrary"),
    )(a, b)


def _rope_tables(positions):
    inv_freq = ROPE_THETA ** (-jnp.arange(ROPE_HALF, dtype=F32) / ROPE_HALF)
    ang = positions.astype(F32)[:, None] * inv_freq
    cos, sin = jnp.cos(ang), jnp.sin(ang)
    S = positions.shape[0]
    one, zero = jnp.ones((S, HEAD_DIM - 2 * ROPE_HALF), F32), jnp.zeros((S, HEAD_DIM - 2 * ROPE_HALF), F32)
    z8 = jnp.zeros((S, ROPE_HALF), F32)
    rc = jnp.concatenate([cos, cos, one], axis=1)
    ra = jnp.concatenate([-sin, z8, zero], axis=1)
    rb = jnp.concatenate([z8, sin, zero], axis=1)
    return tuple(jnp.tile(t, (1, 2)) for t in (rc, ra, rb))


def _layer_weights(w_in, w_out, w_ple, w_pg):
    w_in = w_in.astype(BF16)
    wm = jnp.concatenate([w_in[:, :4 * D_BRANCH], w_in[:, 4 * D_BRANCH + N_HEADS:]], axis=1)
    wf = jnp.pad(w_in[:, 4 * D_BRANCH:4 * D_BRANCH + N_HEADS], ((0, 0), (0, 128 - N_HEADS)))
    w_out, w_ple, w_pg = w_out.astype(BF16), w_ple.astype(BF16), w_pg.astype(BF16)
    return dict(wm=wm, wf=wf, w_out=w_out, w_ple=w_ple, w_pg=w_pg)


def _row(v, width=128):
    v = v.reshape(1, -1).astype(F32)
    return jnp.pad(v, ((0, 0), (0, width - v.shape[1])))


def _layer_fwd(h0, p, rope, tabs, w, norm_g, b_f, qk_g, ple_g):
    S = h0.shape[0]
    g1 = norm_g.reshape(1, D_MODEL)
    g2 = ple_g.reshape(1, D_MODEL)
    qkg = jnp.tile(qk_g, (1, 2))
    bf = _row(b_f)
    zm, zf, u = _inproj_fwd(h0, g1, w["wm"], w["wf"])
    c = _forget_cumsum(zf, bf, tabs["tril"])
    qa, ka, va, qb, kb, vb = _prep_fwd(zm, c, qkg, *rope)
    oa, lse_a = _attn_fwd(qa, ka, va, tabs["fox"], True, "fox_fwd")
    ob, lse_b = _attn_fwd(qb, kb, vb, tabs["dil"], False, "dil_fwd")
    y, h1, h2, u2, e, gate = _mid_fwd(oa, ob, zm, h0, p, w["w_out"], w["w_pg"], w["w_ple"], g2)
    saved = dict(h0=h0, p=p, zm=zm, zf=zf, u=u, qa=qa, ka=ka, va=va, qb=qb, kb=kb, vb=vb, oa=oa, ob=ob,
                 lse_a=lse_a, lse_b=lse_b, y=y, h1=h1, u2=u2, e=e, gate=gate, g1=g1, g2=g2, qkg=qkg, bf=bf)
    return h2, saved


def _layer_bwd(dh2, sv, rope, tabs, w):
    S = dh2.shape[0]
    nq = S // ATT_T
    rows = lambda a: a.reshape(N_HEADS, nq, 1, ATT_T)
    (dh1, dh1b, de, dpre, doa, dob, dga, dgb, dd, dg2) = _mid_bwd(
        dh2, sv["h1"], sv["e"], sv["gate"], sv["g2"], w["w_pg"], w["w_out"], sv["oa"], sv["ob"], sv["zm"])
    dda, ddb = dd[:N_HEADS], dd[N_HEADS:]
    dqa, dka, dva = _attn_bwd(sv["qa"], sv["ka"], sv["va"], doa, sv["lse_a"], rows(dda), tabs["fox"], True,
                              "fox_bwd")
    dqb, dkb, dvb = _attn_bwd(sv["qb"], sv["kb"], sv["vb"], dob, sv["lse_b"], rows(ddb), tabs["dil"], False,
                              "dil_bwd")
    dzm, dc, dqkg = _prep_bwd(dqa, dka, dva, dqb, dkb, dvb, sv["zm"], sv["qkg"], *rope, dga, dgb)
    dzf, dbf = _forget_bwd(dc, sv["zf"], sv["bf"], tabs["triu"])
    dh0, dg1 = _inproj_bwd(dzm, dzf, w["wm"], w["wf"], sv["h0"], dh1, sv["g1"])
    dwm = _wgrad(sv["u"], dzm, "wgrad_in")
    dwf = _wgrad(sv["u"], dzf, "wgrad_f")
    dw_in = jnp.concatenate([dwm[:, :4 * D_BRANCH], dwf[:, :N_HEADS], dwm[:, 4 * D_BRANCH:]], axis=1)
    grads = dict(
        norm_g=dg1.reshape(D_MODEL), w_in=dw_in, b_f=dbf[0, :N_HEADS],
        qk_norm_g=dqkg[:, :HEAD_DIM] + dqkg[:, HEAD_DIM:],
        w_out=_wgrad(sv["y"], dh1b, "wgrad_out"), w_ple=_wgrad(sv["p"], de, "wgrad_ple"),
        ple_norm_g=dg2.reshape(D_MODEL), w_ple_gate=_wgrad(sv["u2"], dpre, "wgrad_gate"))
    return dh0, grads


def _tables():
    T = CUM_T
    r = lax.broadcasted_iota(jnp.int32, (T, T), 0)
    c = lax.broadcasted_iota(jnp.int32, (T, T), 1)
    return dict(fox=_bias_tables(True), dil=_bias_tables(False),
                tril=(c <= r).astype(BF16), triu=(c >= r).astype(BF16))


def _local_step(x, p, positions, target, layers, small):
    rope = _rope_tables(positions)
    tabs = _tables()
    ws = [_layer_weights(*lw) for lw in layers]
    h = x
    saved = []
    for w, lp, sm in zip(ws, p, small):
        h, sv = _layer_fwd(h, lp, rope, tabs, w, *sm)
        saved.append(sv)
    dh, loss = _loss_fwd_bwd(h, target)
    grads = [None] * len(ws)
    for li in reversed(range(len(ws))):
        dh, grads[li] = _layer_bwd(dh, saved[li], rope, tabs, ws[li])
    return loss[0, 0], dh, grads


def _peers():
    x, y, c = lax.axis_index("x"), lax.axis_index("y"), lax.axis_index("c")
    me = 4 * x + 2 * y + c
    flip = lambda v, bit: 1 - v if bit else v
    return me, [(flip(x, k & 4), flip(y, k & 2), flip(c, k & 1)) for k in range(1, N_DEV)]


def _shard(ref, axis, d, size):
    return ref.at[(slice(None),) * axis + (pl.ds(pl.multiple_of(d * size, size), size),)]


def _exchange(name, arrays, out_shapes, src_of, dst_of):
    n = len(arrays)

    def body(*refs):
        ins, outs = refs[:n], refs[n:2 * n]
        send_sems, recv_sems, local_sems = refs[2 * n:]
        me, peers = _peers()
        local = [pltpu.make_async_copy(src_of(a, ins[a], me), dst_of(a, outs[a], me), local_sems.at[a])
                 for a in range(n)]
        for cp in local:
            cp.start()
        copies = [pltpu.make_async_remote_copy(
            src_ref=src_of(a, ins[a], 4 * px + 2 * py + pc), dst_ref=dst_of(a, outs[a], me),
            send_sem=send_sems.at[k, a], recv_sem=recv_sems.at[k, a],
            device_id=(px, py, pc), device_id_type=pl.DeviceIdType.MESH)
            for k, (px, py, pc) in enumerate(peers) for a in range(n)]
        for cp in copies:
            cp.start()
        for cp in copies:
            cp.wait()
        for cp in local:
            cp.wait()

    hbm = pl.BlockSpec(memory_space=pltpu.HBM)
    return pl.pallas_call(
        body, name=name, in_specs=[hbm] * n, out_specs=[hbm] * n, out_shape=out_shapes,
        scratch_shapes=[pltpu.SemaphoreType.DMA((N_DEV - 1, n)), pltpu.SemaphoreType.DMA((N_DEV - 1, n)),
                        pltpu.SemaphoreType.DMA((n,))],
    )(*arrays)


_SHARD_AXES = (None, (1, 128), (2, 128), (1, 128))


def _gather_weights(shards):
    full = [(N_DEV,) + shards[0].shape, (2, D_MODEL, D_MODEL), (2, PLE_DIM, D_MODEL), (2, D_MODEL, D_MODEL)]
    return _exchange(
        "gather_weights", shards, [jax.ShapeDtypeStruct(f, BF16) for f in full],
        src_of=lambda a, ref, peer: ref,
        dst_of=lambda a, ref, me: ref.at[me] if a == 0 else _shard(ref, _SHARD_AXES[a][0], me, _SHARD_AXES[a][1]))


def _exchange_grads(partials, small):
    shard_shapes = [partials[0].shape[1:], (2, 128, D_MODEL), (2, PLE_DIM, 128), (2, 128, D_MODEL), small.shape]

    def src_of(a, ref, peer):
        if a == 0:
            return ref.at[peer]
        return ref if a == len(partials) else _shard(ref, _SHARD_AXES[a][0], peer, _SHARD_AXES[a][1])

    arrays = list(partials) + [small]
    return _exchange(
        "exchange_grads", arrays, [jax.ShapeDtypeStruct((N_DEV,) + s, a.dtype) for s, a in zip(shard_shapes, arrays)],
        src_of=src_of, dst_of=lambda a, ref, me: ref.at[me])


def _adamw(name, parts, w, m, v, rows):
    L, R, C = w.shape

    def body(p_ref, w_ref, m_ref, v_ref, g_ref, d_ref, nm_ref, nv_ref):
        g = p_ref[0, 0].astype(F32)
        for s in range(1, N_DEV):
            g = g + p_ref[s, 0].astype(F32)
        g_ref[0] = g
        nm = ADAM_B1 * m_ref[0] + (1.0 - ADAM_B1) * g
        nv = ADAM_B2 * v_ref[0] + (1.0 - ADAM_B2) * (g * g)
        nm_ref[0] = nm
        nv_ref[0] = nv
        m_hat = nm / (1.0 - ADAM_B1 ** ADAM_STEP)
        v_hat = nv / (1.0 - ADAM_B2 ** ADAM_STEP)
        d_ref[0] = -ADAM_LR * (m_hat / (jnp.sqrt(v_hat) + ADAM_EPS) + ADAM_WD * w_ref[0])

    blk = pl.BlockSpec((1, rows, C), lambda l, i: (l, i, 0))
    shp = jax.ShapeDtypeStruct((L, R, C), F32)
    return pl.pallas_call(
        body, name=name, grid=(L, R // rows),
        in_specs=[pl.BlockSpec((N_DEV, 1, rows, C), lambda l, i: (0, l, i, 0)), blk, blk, blk],
        out_specs=[blk] * 4, out_shape=[shp] * 4,
        compiler_params=_cp("parallel", "parallel"),
    )(parts, w, m, v)


SMALL_ROWS = 40
LOSS_ROW = 37


def _pack_small(norm_g, ple_g, qk_g, b_f, last_row):
    rows = lambda a: a.astype(F32).reshape(-1, 128)
    flat = jnp.concatenate([rows(norm_g), rows(ple_g), rows(qk_g), _row(b_f.reshape(-1)), last_row], axis=0)
    return jnp.pad(flat, ((0, SMALL_ROWS - flat.shape[0]), (0, 0)))


def _unpack_small(flat):
    return (flat[0:16].reshape(2, D_MODEL), flat[16:32].reshape(2, D_MODEL), flat[32:36].reshape(2, 4, HEAD_DIM),
            flat[36, :2 * N_HEADS].reshape(2, N_HEADS))


def kernel(x, p, positions, norm_g, w_in, b_f, qk_norm_g, w_out, w_ple, ple_norm_g, w_ple_gate, loss_target, m_norm_g, m_w_in, m_b_f, m_qk_norm_g, m_w_out, m_w_ple, m_ple_norm_g, m_w_ple_gate, v_norm_g, v_w_in, v_b_f, v_qk_norm_g, v_w_out, v_w_ple, v_ple_norm_g, v_w_ple_gate):
    g_in, full_out, full_ple, full_pg = _gather_weights(
        [w_in.astype(BF16).reshape(W_IN_ROWS, 128)] + [a.astype(BF16) for a in (w_out, w_ple, w_ple_gate)])
    full_in = g_in.reshape(N_DEV, 2, D_MODEL, W_IN_SHARD).transpose(1, 2, 0, 3).reshape(2, D_MODEL, N_IN)
    layers = [(full_in[l], full_out[l], full_ple[l], full_pg[l]) for l in range(2)]
    small = [(norm_g[l], b_f[l], qk_norm_g[l], ple_norm_g[l]) for l in range(2)]
    loss, dx, grads = _local_step(x[0], p[:, 0], positions[0], loss_target[0], layers, small)

    stack = lambda name: jnp.stack([gl[name] for gl in grads], axis=0)
    d_in = stack("w_in").reshape(2, D_MODEL, N_DEV, W_IN_SHARD).transpose(2, 0, 1, 3).reshape(N_DEV, W_IN_ROWS, 128)
    small_part = _pack_small(stack("norm_g"), stack("ple_norm_g"), stack("qk_norm_g"), stack("b_f"),
                             _row(loss.reshape(1)))
    r_in, r_out, r_ple, r_pg, r_small = _exchange_grads([d_in, stack("w_out"), stack("w_ple"), stack("w_ple_gate")],
                                                        small_part)

    zero_row = jnp.zeros((1, 128), F32)
    small_of = lambda ng, pg, qk, bf: _pack_small(ng, pg, qk, bf, zero_row)[None]
    flat = lambda a: a.reshape(1, W_IN_ROWS, 128)
    outs = dict(
        w_in=[o.reshape(w_in.shape) for o in
              _adamw("adamw_in", r_in[:, None], flat(w_in), flat(m_w_in), flat(v_w_in), W_IN_TILE)],
        w_out=_adamw("adamw_out", r_out, w_out, m_w_out, v_w_out, 128),
        w_ple=_adamw("adamw_ple", r_ple, w_ple, m_w_ple, v_w_ple, 256),
        w_pg=_adamw("adamw_gate", r_pg, w_ple_gate, m_w_ple_gate, v_w_ple_gate, 128),
        small=_adamw("adamw_small", r_small[:, None], small_of(norm_g, ple_norm_g, qk_norm_g, b_f),
                     small_of(m_norm_g, m_ple_norm_g, m_qk_norm_g, m_b_f),
                     small_of(v_norm_g, v_ple_norm_g, v_qk_norm_g, v_b_f), SMALL_ROWS))
    leaves = []
    for kind in range(4):
        ng, pg, qk, bf = _unpack_small(outs["small"][kind][0])
        leaves += [ng, outs["w_in"][kind], bf, qk, outs["w_out"][kind], outs["w_ple"][kind], pg, outs["w_pg"][kind]]
    return (outs["small"][0][0, LOSS_ROW, 0], dx[None], *leaves)
```

```python
import functools

import jax
import jax.numpy as jnp
from jax import lax
from jax.experimental import pallas as pl
from jax.experimental.pallas import tpu as pltpu

F32 = jnp.float32
BF16 = jnp.bfloat16

D_MODEL = 1024
HEAD_DIM = 64
N_HEADS = 8
HEAD_PAD = 128
D_BRANCH = N_HEADS * HEAD_DIM
N_MAIN = 8 * D_BRANCH
N_IN = N_MAIN + N_HEADS
PLE_DIM = 256
ROPE_THETA = 500000.0
ROPE_HALF = 8
EPS = 1e-6
NEG = -1e30
SCALE = HEAD_DIM ** -0.5
LOG2E = 1.4426950408889634
LN2 = 0.6931471805599453
DILATED_PATTERNS = ((128, 1), (512, 4), (2048, 16))
N_DEV = 8
W_IN_SHARD = N_IN // N_DEV
W_IN_ROWS = 2 * D_MODEL * W_IN_SHARD // 128
W_IN_TILE = W_IN_ROWS // 19

ADAM_LR = 0.001
ADAM_B1 = 0.9
ADAM_B2 = 0.999
ADAM_EPS = 1e-08
ADAM_WD = 0.01
ADAM_STEP = 10

ATT_T = 512
ATT_FWD_HEADS = 2
ATT_BWD_HEADS = 2
ATT_CHUNK = 32
TOK_T = 256
CUM_T = 512
VMEM_LIMIT = 56 * 1024 * 1024


def _cp(*sem):
    return pltpu.CompilerParams(dimension_semantics=sem, vmem_limit_bytes=VMEM_LIMIT)


def _sigmoid(x):
    return 1.0 / (1.0 + jnp.exp(-x))


def _split3(x):
    hi = x.astype(BF16)
    r1 = x - hi.astype(F32)
    mid = r1.astype(BF16)
    lo = (r1 - mid.astype(F32)).astype(BF16)
    return hi, mid, lo


def _dot(a, b):
    return jnp.dot(a, b, preferred_element_type=F32)


def _dot_nt(a, b):
    return lax.dot_general(a, b, (((1,), (1,)), ((), ())), preferred_element_type=F32)


def _dot_tn(a, b):
    return lax.dot_general(a, b, (((0,), (0,)), ((), ())), preferred_element_type=F32)


def _inproj_fwd(h, g, wm, wf):
    S = h.shape[0]

    def body(h_ref, g_ref, wm_ref, wf_ref, zm_ref, zf_ref, u_ref):
        x = h_ref[...]
        r = lax.rsqrt(jnp.mean(x * x, axis=-1, keepdims=True) + EPS)
        u = (x * r * g_ref[...]).astype(BF16)
        u_ref[...] = u
        zm_ref[...] = _dot(u, wm_ref[...])
        zf_ref[...] = _dot(u, wf_ref[...])

    return pl.pallas_call(
        body, name="inproj_fwd", grid=(S // TOK_T,),
        in_specs=[pl.BlockSpec((TOK_T, D_MODEL), lambda i: (i, 0)),
                  pl.BlockSpec((1, D_MODEL), lambda i: (0, 0)),
                  pl.BlockSpec((D_MODEL, N_MAIN), lambda i: (0, 0)),
                  pl.BlockSpec((D_MODEL, 128), lambda i: (0, 0))],
        out_specs=[pl.BlockSpec((TOK_T, N_MAIN), lambda i: (i, 0)),
                   pl.BlockSpec((TOK_T, 128), lambda i: (i, 0)),
                   pl.BlockSpec((TOK_T, D_MODEL), lambda i: (i, 0))],
        out_shape=[jax.ShapeDtypeStruct((S, N_MAIN), F32), jax.ShapeDtypeStruct((S, 128), F32),
                   jax.ShapeDtypeStruct((S, D_MODEL), BF16)],
        compiler_params=_cp("parallel"),
    )(h, g, wm, wf)


def _log_sigmoid(x):
    return jnp.minimum(x, 0.0) - jnp.log(1.0 + jnp.exp(-jnp.abs(x)))


def _forget_cumsum(zf, bf, tri):
    S = zf.shape[0]

    def body(zf_ref, b_ref, tri_ref, c_ref, carry):
        @pl.when(pl.program_id(0) == 0)
        def _():
            carry[...] = jnp.zeros_like(carry)

        lf = _log_sigmoid(zf_ref[...] + b_ref[...])
        hi, mid, lo = _split3(lf)
        t = tri_ref[...]
        cs = _dot(t, hi) + _dot(t, mid) + _dot(t, lo) + carry[...]
        c_ref[...] = cs
        carry[...] = cs[CUM_T - 1:CUM_T, :]

    return pl.pallas_call(
        body, name="forget_cumsum", grid=(S // CUM_T,),
        in_specs=[pl.BlockSpec((CUM_T, 128), lambda i: (i, 0)),
                  pl.BlockSpec((1, 128), lambda i: (0, 0)),
                  pl.BlockSpec((CUM_T, CUM_T), lambda i: (0, 0))],
        out_specs=pl.BlockSpec((CUM_T, 128), lambda i: (i, 0)),
        out_shape=jax.ShapeDtypeStruct((S, 128), F32),
        scratch_shapes=[pltpu.VMEM((1, 128), F32)],
        compiler_params=_cp("arbitrary"),
    )(zf, bf, tri)


def _forget_bwd(dc, zf, bf, triu):
    S = zf.shape[0]
    n = S // CUM_T

    def body(dc_ref, zf_ref, b_ref, tri_ref, dzf_ref, db_ref, carry):
        @pl.when(pl.program_id(0) == 0)
        def _():
            carry[...] = jnp.zeros_like(carry)
            db_ref[...] = jnp.zeros_like(db_ref)

        hi, mid, lo = _split3(dc_ref[...])
        t = tri_ref[...]
        dlf = _dot(t, hi) + _dot(t, mid) + _dot(t, lo) + carry[...]
        carry[...] = dlf[0:1, :]
        dfa = dlf * (1.0 - _sigmoid(zf_ref[...] + b_ref[...]))
        dzf_ref[...] = dfa.astype(BF16)
        db_ref[...] += jnp.sum(dfa, axis=0, keepdims=True)

    return pl.pallas_call(
        body, name="forget_bwd", grid=(n,),
        in_specs=[pl.BlockSpec((CUM_T, 128), lambda i: (n - 1 - i, 0)),
                  pl.BlockSpec((CUM_T, 128), lambda i: (n - 1 - i, 0)),
                  pl.BlockSpec((1, 128), lambda i: (0, 0)),
                  pl.BlockSpec((CUM_T, CUM_T), lambda i: (0, 0))],
        out_specs=[pl.BlockSpec((CUM_T, 128), lambda i: (n - 1 - i, 0)),
                   pl.BlockSpec((1, 128), lambda i: (0, 0))],
        out_shape=[jax.ShapeDtypeStruct((S, 128), BF16), jax.ShapeDtypeStruct((1, 128), F32)],
        scratch_shapes=[pltpu.VMEM((1, 128), F32)],
        compiler_params=_cp("arbitrary"),
    )(dc, zf, bf, triu)


def _same_head():
    r = lax.broadcasted_iota(jnp.int32, (HEAD_PAD, HEAD_PAD), 0) // HEAD_DIM
    c = lax.broadcasted_iota(jnp.int32, (HEAD_PAD, HEAD_PAD), 1) // HEAD_DIM
    return (r == c).astype(BF16)


def _pair_mean(x, same_head):
    hi = x.astype(BF16)
    lo = (x - hi.astype(F32)).astype(BF16)
    return (_dot(hi, same_head) + _dot(lo, same_head)) * (1.0 / HEAD_DIM)


def _pair_rsqrt(x, same_head):
    return lax.rsqrt(_pair_mean(x * x, same_head) + EPS)


def _prep_fwd(zm, c, qkg, rope_c, rope_a, rope_b):
    S = zm.shape[0]
    shp = jax.ShapeDtypeStruct((N_HEADS, S, HEAD_PAD), BF16)

    def body(z_ref, c_ref, g_ref, rc_ref, ra_ref, rb_ref, qa_ref, ka_ref, va_ref, qb_ref, kb_ref, vb_ref):
        lane = lax.broadcasted_iota(jnp.int32, (TOK_T, HEAD_PAD), 1)
        lo_half = lane < HEAD_DIM
        aug = (lane >= HEAD_DIM) & (lane < HEAD_DIM + 3)
        q_pad = jnp.where(aug, -1.0, 0.0)
        cs = c_ref[...]
        rc, ra, rb = rc_ref[...], ra_ref[...], rb_ref[...]
        same_head = _same_head()

        def norm(col, gi):
            x = z_ref[:, col:col + HEAD_PAD]
            return x * _pair_rsqrt(x, same_head) * g_ref[gi:gi + 1, :]

        def rope(y):
            return y * rc + pltpu.roll(y, HEAD_PAD - ROPE_HALF, 1) * ra + pltpu.roll(y, ROPE_HALF, 1) * rb

        def put(ref, pi, y, pad_even, pad_odd):
            ref[2 * pi] = jnp.where(lo_half, y, pad_even).astype(BF16)
            ref[2 * pi + 1] = jnp.where(lo_half, pltpu.roll(y, HEAD_DIM, 1), pad_odd).astype(BF16)

        def k_pad(h):
            ch = cs[:, h:h + 1] * LOG2E
            hi = ch.astype(BF16).astype(F32)
            mid = (ch - hi).astype(BF16).astype(F32)
            lo = ch - hi - mid
            ones = jnp.where(lane == HEAD_DIM + 3, 1.0, 0.0)
            return jnp.where(lane == HEAD_DIM, hi, jnp.where(lane == HEAD_DIM + 1, mid,
                                                              jnp.where(lane == HEAD_DIM + 2, lo, ones)))

        for pi in range(N_HEADS // 2):
            col = HEAD_PAD * pi
            put(qa_ref, pi, norm(col, 0) * (SCALE * LOG2E), q_pad, q_pad)
            put(ka_ref, pi, norm(D_BRANCH + col, 1), k_pad(2 * pi), k_pad(2 * pi + 1))
            put(va_ref, pi, z_ref[:, 2 * D_BRANCH + col:2 * D_BRANCH + col + HEAD_PAD], 0.0, 0.0)
            put(qb_ref, pi, rope(norm(4 * D_BRANCH + col, 2)) * (SCALE * LOG2E), 0.0, 0.0)
            put(kb_ref, pi, rope(norm(5 * D_BRANCH + col, 3)), 0.0, 0.0)
            put(vb_ref, pi, z_ref[:, 6 * D_BRANCH + col:6 * D_BRANCH + col + HEAD_PAD], 0.0, 0.0)

    tok = lambda w: pl.BlockSpec((TOK_T, w), lambda i: (i, 0))
    head = pl.BlockSpec((N_HEADS, TOK_T, HEAD_PAD), lambda i: (0, i, 0))
    return pl.pallas_call(
        body, name="prep_fwd", grid=(S // TOK_T,),
        in_specs=[tok(N_MAIN), tok(128), pl.BlockSpec((4, 128), lambda i: (0, 0)), tok(128), tok(128), tok(128)],
        out_specs=[head] * 6, out_shape=[shp] * 6,
        compiler_params=_cp("parallel"),
    )(zm, c, qkg, rope_c, rope_a, rope_b)


def _pair(ref, pi, lo_half):
    return jnp.where(lo_half, ref[2 * pi], pltpu.roll(ref[2 * pi + 1], HEAD_DIM, 1))


def _mid_fwd(oa, ob, zm, h0, p, w_out, w_pg, w_ple, g2):
    S = h0.shape[0]

    def body(oa_ref, ob_ref, ga_ref, gb_ref, h0_ref, p_ref, wo_ref, wg_ref, wp_ref, g2_ref,
             y_ref, h1_ref, h2_ref, u2_ref, e_ref, gate_ref):
        lane = lax.broadcasted_iota(jnp.int32, (TOK_T, HEAD_PAD), 1)
        lo_half = lane < HEAD_DIM
        parts = []
        for o_ref, g_ref in ((oa_ref, ga_ref), (ob_ref, gb_ref)):
            for pi in range(N_HEADS // 2):
                g = g_ref[:, HEAD_PAD * pi:HEAD_PAD * (pi + 1)]
                parts.append((_pair(o_ref, pi, lo_half) * (g * _sigmoid(g))).astype(BF16))
        y = jnp.concatenate(parts, axis=1)
        y_ref[...] = y
        h1 = h0_ref[...] + _dot(y, wo_ref[...])
        h1_ref[...] = h1
        r = lax.rsqrt(jnp.mean(h1 * h1, axis=-1, keepdims=True) + EPS)
        u2 = (h1 * r * g2_ref[...]).astype(BF16)
        u2_ref[...] = u2
        gate = _sigmoid(_dot(u2, wg_ref[...]))
        e = _dot(p_ref[...].astype(BF16), wp_ref[...])
        e_ref[...] = e
        gate_ref[...] = gate
        h2_ref[...] = h1 + e * gate

    tok = lambda w: pl.BlockSpec((TOK_T, w), lambda i: (i, 0))
    head = pl.BlockSpec((N_HEADS, TOK_T, HEAD_PAD), lambda i: (0, i, 0))
    full = lambda a, b: pl.BlockSpec((a, b), lambda i: (0, 0))
    act = lambda dt: jax.ShapeDtypeStruct((S, D_MODEL), dt)
    return pl.pallas_call(
        body, name="mid_fwd", grid=(S // TOK_T,),
        in_specs=[head, head,
                  pl.BlockSpec((TOK_T, D_BRANCH), lambda i: (i, 3)), pl.BlockSpec((TOK_T, D_BRANCH), lambda i: (i, 7)),
                  tok(D_MODEL), tok(PLE_DIM), full(D_MODEL, D_MODEL), full(D_MODEL, D_MODEL),
                  full(PLE_DIM, D_MODEL), full(1, D_MODEL)],
        out_specs=[tok(D_MODEL)] * 6,
        out_shape=[act(BF16), act(F32), act(F32), act(BF16), act(F32), act(F32)],
        compiler_params=_cp("parallel"),
    )(oa, ob, zm, zm, h0, p, w_out, w_pg, w_ple, g2)


def _loss_fwd_bwd(y, t):
    S = y.shape[0]

    def body(y_ref, t_ref, dy_ref, loss_ref):
        @pl.when(pl.program_id(0) == 0)
        def _():
            loss_ref[...] = jnp.zeros_like(loss_ref)

        err = y_ref[...] - t_ref[...]
        dy_ref[...] = err * (1.0 / D_MODEL)
        part = jnp.sum(jnp.sum(err * err, axis=1, keepdims=True), axis=0, keepdims=True)
        loss_ref[...] += part * (0.5 / D_MODEL)

    tok = pl.BlockSpec((TOK_T, D_MODEL), lambda i: (i, 0))
    return pl.pallas_call(
        body, name="loss", grid=(S // TOK_T,),
        in_specs=[tok, tok], out_specs=[tok, pl.BlockSpec((8, 128), lambda i: (0, 0))],
        out_shape=[jax.ShapeDtypeStruct((S, D_MODEL), F32), jax.ShapeDtypeStruct((8, 128), F32)],
        compiler_params=_cp("arbitrary"),
    )(y, t)


def _bias_tables(full_range):
    T = ATT_T
    nb = 1 if full_range else DILATED_PATTERNS[-1][0] // T + 1
    r = lax.broadcasted_iota(jnp.int32, (nb, T, T), 2)
    c = lax.broadcasted_iota(jnp.int32, (nb, T, T), 1)
    b = lax.broadcasted_iota(jnp.int32, (nb, T, T), 0)
    delta = T * b + r - c
    if full_range:
        bias = jnp.where(delta >= 0, 0.0, NEG).astype(F32)
    else:
        mult = jnp.zeros((nb, T, T), F32)
        for window, dil in DILATED_PATTERNS:
            ok = (delta >= 0) & (delta <= window) & (delta % dil == 0)
            mult = mult + ok.astype(F32)
        bias = jnp.where(mult > 0, jnp.log2(jnp.maximum(mult, 1.0)), NEG).astype(F32)
    return bias


def _attn_fwd(q, k, v, table_t, full_range, name):
    H, S, _ = q.shape
    T = ATT_T
    nb = table_t.shape[0]
    HB = ATT_FWD_HEADS
    KC = ATT_CHUNK
    chunks = [slice(c, c + KC) for c in range(0, T, KC)]
    fold = lambda x, op: functools.reduce(op, [x[r:r + 8] for r in range(0, KC, 8)])

    def body(q_ref, k_ref, v_ref, tab_ref, o_ref, lse_ref, *scratch):
        st_refs, pt_refs, acc_refs = scratch[:HB], scratch[HB:2 * HB], scratch[2 * HB:]
        i = pl.program_id(1)
        rows = lambda j: pl.ds(pl.multiple_of(j * T, T), T)

        def scores(hh, j):
            st_refs[hh][...] = _dot_nt(k_ref[hh, rows(j), :], q_ref[hh])

        def block(j, b, nxt, stats):
            out = []
            for hh, (m, l) in enumerate(stats):
                st_ref, pt_ref, acc_ref = st_refs[hh], pt_refs[hh], acc_refs[hh]
                mx = None
                for ch in chunks:
                    x = st_ref[ch, :]
                    if b is not None:
                        x = x + tab_ref[b, ch, :]
                        st_ref[ch, :] = x
                    x = fold(x, jnp.maximum)
                    mx = x if mx is None else jnp.maximum(mx, x)
                m_new = jnp.maximum(m, jnp.max(mx, axis=0, keepdims=True))
                alpha = jnp.exp2(m - m_new)
                ls = None
                for ch in chunks:
                    pc = jnp.exp2(st_ref[ch, :] - m_new)
                    pt_ref[ch, :] = pc.astype(BF16)
                    pc = fold(pc, jnp.add)
                    ls = pc if ls is None else ls + pc
                if nxt is not None:
                    scores(hh, nxt)
                acc_ref[...] = alpha * acc_ref[...] + _dot_tn(v_ref[hh, rows(j), :], pt_ref[...])
                out.append((m_new, alpha * l + jnp.sum(ls, axis=0, keepdims=True)))
            return tuple(out)

        lo = 0 if full_range else jnp.maximum(i - (nb - 1), 0)
        for hh in range(HB):
            acc_refs[hh][...] = jnp.zeros_like(acc_refs[hh])
            scores(hh, lo)
        stats = lax.fori_loop(lo, i, lambda j, st: block(j, None if full_range else i - j, j + 1, st),
                              ((jnp.full((1, T), NEG, F32), jnp.zeros((1, T), F32)),) * HB)
        stats = block(i, 0, None, stats)
        for hh, (m, l) in enumerate(stats):
            o_ref[hh] = (acc_refs[hh][...] * (1.0 / l)).T
            lse_ref[hh, 0] = m + jnp.log2(l)

    return pl.pallas_call(
        body, name=name, grid=(H // HB, S // T),
        in_specs=[pl.BlockSpec((HB, T, HEAD_PAD), lambda h, i: (h, i, 0)),
                  pl.BlockSpec((HB, S, HEAD_PAD), lambda h, i: (h, 0, 0)),
                  pl.BlockSpec((HB, S, HEAD_PAD), lambda h, i: (h, 0, 0)),
                  pl.BlockSpec((nb, T, T), lambda h, i: (0, 0, 0))],
        out_specs=[pl.BlockSpec((HB, T, HEAD_PAD), lambda h, i: (h, i, 0)),
                   pl.BlockSpec((HB, 1, 1, T), lambda h, i: (h, i, 0, 0))],
        out_shape=[jax.ShapeDtypeStruct((H, S, HEAD_PAD), F32), jax.ShapeDtypeStruct((H, S // T, 1, T), F32)],
        scratch_shapes=([pltpu.VMEM((T, T), F32)] * HB + [pltpu.VMEM((T, T), BF16)] * HB
                        + [pltpu.VMEM((HEAD_PAD, T), F32)] * HB),
        compiler_params=_cp("parallel", "arbitrary"),
    )(q, k, v, table_t)


def _attn_bwd(q, k, v, do, lse, dd, table_t, full_range, name):
    H, S, _ = q.shape
    T = ATT_T
    nq = S // T
    nb = table_t.shape[0]
    HB = ATT_BWD_HEADS
    KC = ATT_CHUNK
    chunks = [slice(c, c + KC) for c in range(0, T, KC)]

    def body(q_ref, do_ref, lse_ref, dd_ref, k_ref, v_ref, tab_ref, dq_hbm, dk_ref, dv_ref, *scratch):
        st_refs, dpt_refs, pt_refs, dst_refs = (scratch[n * HB:(n + 1) * HB] for n in range(4))
        dq_ref, dq_sem = scratch[4 * HB:]
        h = pl.program_id(0)
        j = pl.program_id(1)

        @pl.when(j == 0)
        def _():
            dq_ref[...] = jnp.zeros_like(dq_ref)

        dk_ref[...] = jnp.zeros_like(dk_ref)
        dv_ref[...] = jnp.zeros_like(dv_ref)

        def step(i, b):
            rows = pl.ds(pl.multiple_of(i * T, T), T)
            for hh in range(HB):
                st_refs[hh][...] = _dot_nt(k_ref[hh], q_ref[hh, rows, :])
                dpt_refs[hh][...] = _dot_nt(v_ref[hh], do_ref[hh, rows, :])
            for hh in range(HB):
                lse_i = lse_ref[hh, i]
                dd_i = dd_ref[hh, i]
                for ch in chunks:
                    x = st_refs[hh][ch, :]
                    if b is not None:
                        x = x + tab_ref[b, ch, :]
                    pc = jnp.exp2(x - lse_i)
                    pt_refs[hh][ch, :] = pc.astype(BF16)
                    dst_refs[hh][ch, :] = (pc * (dpt_refs[hh][ch, :] - dd_i)).astype(BF16)
                dv_ref[hh] += _dot(pt_refs[hh][...], do_ref[hh, rows, :])
                dk_ref[hh] += _dot(dst_refs[hh][...], q_ref[hh, rows, :])
                dq_ref[hh, rows, :] += _dot_tn(dst_refs[hh][...], k_ref[hh])

        step(j, 0)
        if full_range:
            pl.loop(j + 1, nq)(lambda i: step(i, None))
        else:
            pl.loop(j + 1, jnp.minimum(j + nb, nq))(lambda i: step(i, i - j))

        @pl.when(j == nq - 1)
        def _():
            out = pltpu.make_async_copy(dq_ref, dq_hbm.at[pl.ds(h * HB, HB)], dq_sem)
            out.start()
            out.wait()

    once = dict(pipeline_mode=pl.Buffered(1))
    per_head = pl.BlockSpec((HB, S, HEAD_PAD), lambda h, j: (h, 0, 0), **once)
    rows = pl.BlockSpec((HB, nq, 1, T), lambda h, j: (h, 0, 0, 0))
    blk = pl.BlockSpec((HB, T, HEAD_PAD), lambda h, j: (h, j, 0))
    shp = jax.ShapeDtypeStruct((H, S, HEAD_PAD), F32)
    return pl.pallas_call(
        body, name=name, grid=(H // HB, nq),
        in_specs=[per_head, per_head, rows, rows, blk, blk,
                  pl.BlockSpec((nb, T, T), lambda h, j: (0, 0, 0), **once)],
        out_specs=[pl.BlockSpec(memory_space=pltpu.HBM), blk, blk], out_shape=[shp, shp, shp],
        scratch_shapes=([pltpu.VMEM((T, T), F32)] * (2 * HB) + [pltpu.VMEM((T, T), BF16)] * (2 * HB)
                        + [pltpu.VMEM((HB, S, HEAD_PAD), F32), pltpu.SemaphoreType.DMA]),
        compiler_params=_cp("parallel", "arbitrary"),
    )(q, do, lse, dd, k, v, table_t)


def _mid_bwd(dh2, h1, e, gate, g2, w_pg, w_out, oa, ob, zm):
    S = dh2.shape[0]

    def body(dh2_ref, h1_ref, e_ref, gate_ref, g2_ref, wg_ref, wo_ref, oa_ref, ob_ref, ga_ref, gb_ref,
             dh1_ref, dh1b_ref, de_ref, dpre_ref, doa_ref, dob_ref, dga_ref, dgb_ref, dd_ref, dg2_ref):
        @pl.when(pl.program_id(0) == 0)
        def _():
            dg2_ref[...] = jnp.zeros_like(dg2_ref)

        lane = lax.broadcasted_iota(jnp.int32, (TOK_T, HEAD_PAD), 1)
        lo_half = lane < HEAD_DIM
        dh2 = dh2_ref[...]
        gate = gate_ref[...]
        de_ref[...] = (dh2 * gate).astype(BF16)
        dpre = (dh2 * e_ref[...] * gate * (1.0 - gate)).astype(BF16)
        dpre_ref[...] = dpre
        du2 = _dot_nt(dpre, wg_ref[...])
        h1 = h1_ref[...]
        r = lax.rsqrt(jnp.mean(h1 * h1, axis=-1, keepdims=True) + EPS)
        xh = h1 * r
        a = du2 * g2_ref[...]
        dh1 = dh2 + r * (a - xh * jnp.mean(a * xh, axis=-1, keepdims=True))
        dg2_ref[...] += jnp.sum(du2 * xh, axis=0, keepdims=True)
        dh1_ref[...] = dh1
        dh1b = dh1.astype(BF16)
        dh1b_ref[...] = dh1b
        dy = _dot_nt(dh1b, wo_ref[...])
        dd = jnp.zeros((TOK_T, HEAD_PAD), F32)
        for bi, (o_ref, g_ref, do_ref, dg_ref) in enumerate(
                ((oa_ref, ga_ref, doa_ref, dga_ref), (ob_ref, gb_ref, dob_ref, dgb_ref))):
            for pi in range(N_HEADS // 2):
                col = bi * D_BRANCH + HEAD_PAD * pi
                dyp = dy[:, col:col + HEAD_PAD]
                g = g_ref[:, HEAD_PAD * pi:HEAD_PAD * (pi + 1)]
                sg = _sigmoid(g)
                dg_ref[:, HEAD_PAD * pi:HEAD_PAD * (pi + 1)] = (
                    dyp * _pair(o_ref, pi, lo_half) * (sg * (1.0 + g * (1.0 - sg)))).astype(BF16)
                dop = dyp * (g * sg)
                for hh, d_head in ((2 * pi, dop), (2 * pi + 1, pltpu.roll(dop, HEAD_DIM, 1))):
                    d_head = jnp.where(lo_half, d_head, 0.0)
                    do_ref[hh] = d_head.astype(BF16)
                    dsum = jnp.sum(d_head * o_ref[hh], axis=1, keepdims=True)
                    dd = dd + jnp.where(lane == bi * N_HEADS + hh, dsum, 0.0)
        dd_ref[...] = dd.T[:2 * N_HEADS, :]

    tok = lambda w: pl.BlockSpec((TOK_T, w), lambda i: (i, 0))
    head = pl.BlockSpec((N_HEADS, TOK_T, HEAD_PAD), lambda i: (0, i, 0))
    full = lambda a, b: pl.BlockSpec((a, b), lambda i: (0, 0))
    act = lambda w, dt: jax.ShapeDtypeStruct((S, w), dt)
    hshape = lambda w, dt: jax.ShapeDtypeStruct((N_HEADS, S, w), dt)
    return pl.pallas_call(
        body, name="mid_bwd", grid=(S // TOK_T,),
        in_specs=[tok(D_MODEL)] * 4 + [full(1, D_MODEL), full(D_MODEL, D_MODEL), full(D_MODEL, D_MODEL), head, head,
                                      pl.BlockSpec((TOK_T, D_BRANCH), lambda i: (i, 3)),
                                      pl.BlockSpec((TOK_T, D_BRANCH), lambda i: (i, 7))],
        out_specs=[tok(D_MODEL)] * 4 + [head, head, tok(D_BRANCH), tok(D_BRANCH),
                                       pl.BlockSpec((2 * N_HEADS, TOK_T), lambda i: (0, i)), full(1, D_MODEL)],
        out_shape=[act(D_MODEL, F32), act(D_MODEL, BF16), act(D_MODEL, BF16), act(D_MODEL, BF16),
                   hshape(HEAD_PAD, BF16), hshape(HEAD_PAD, BF16), act(D_BRANCH, BF16), act(D_BRANCH, BF16),
                   jax.ShapeDtypeStruct((2 * N_HEADS, S), F32), jax.ShapeDtypeStruct((1, D_MODEL), F32)],
        compiler_params=_cp("arbitrary"),
    )(dh2, h1, e, gate, g2, w_pg, w_out, oa, ob, zm, zm)


def _prep_bwd(dqa, dka, dva, dqb, dkb, dvb, zm, qkg, rope_c, rope_a, rope_b, dga, dgb):
    S = zm.shape[0]

    def body(dqa_ref, dka_ref, dva_ref, dqb_ref, dkb_ref, dvb_ref, z_ref, g_ref, rc_ref, ra_ref, rb_ref,
             dga_ref, dgb_ref, dz_ref, dc_ref, dqkg_ref):
        @pl.when(pl.program_id(0) == 0)
        def _():
            dqkg_ref[...] = jnp.zeros_like(dqkg_ref)

        lane = lax.broadcasted_iota(jnp.int32, (TOK_T, HEAD_PAD), 1)
        lo_half = lane < HEAD_DIM
        rc, ra, rb = rc_ref[...], ra_ref[...], rb_ref[...]
        same_head = _same_head()

        def unrope(dy):
            return dy * rc + pltpu.roll(dy * ra, ROPE_HALF, 1) + pltpu.roll(dy * rb, HEAD_PAD - ROPE_HALF, 1)

        def norm_bwd(col, gi, dy):
            x = z_ref[:, col:col + HEAD_PAD]
            r = _pair_rsqrt(x, same_head)
            xh = x * r
            dqkg_ref[gi:gi + 1, :] += jnp.sum(dy * xh, axis=0, keepdims=True)
            a = dy * g_ref[gi:gi + 1, :]
            dz_ref[:, col:col + HEAD_PAD] = (r * (a - xh * _pair_mean(a * xh, same_head))).astype(BF16)

        dc = jnp.zeros((TOK_T, HEAD_PAD), F32)
        for pi in range(N_HEADS // 2):
            col = HEAD_PAD * pi
            norm_bwd(col, 0, _pair(dqa_ref, pi, lo_half) * SCALE)
            norm_bwd(D_BRANCH + col, 1, _pair(dka_ref, pi, lo_half) * LN2)
            dz_ref[:, 2 * D_BRANCH + col:2 * D_BRANCH + col + HEAD_PAD] = _pair(dva_ref, pi, lo_half).astype(BF16)
            norm_bwd(4 * D_BRANCH + col, 2, unrope(_pair(dqb_ref, pi, lo_half) * SCALE))
            norm_bwd(5 * D_BRANCH + col, 3, unrope(_pair(dkb_ref, pi, lo_half) * LN2))
            dz_ref[:, 6 * D_BRANCH + col:6 * D_BRANCH + col + HEAD_PAD] = _pair(dvb_ref, pi, lo_half).astype(BF16)
            for hh in (2 * pi, 2 * pi + 1):
                dch = dka_ref[hh][:, HEAD_DIM:HEAD_DIM + 1] + dqa_ref[hh][:, HEAD_DIM + 3:HEAD_DIM + 4]
                dc = dc + jnp.where(lane == hh, dch, 0.0)
        dz_ref[:, 3 * D_BRANCH:4 * D_BRANCH] = dga_ref[...]
        dz_ref[:, 7 * D_BRANCH:8 * D_BRANCH] = dgb_ref[...]
        dc_ref[...] = dc

    tok = lambda w: pl.BlockSpec((TOK_T, w), lambda i: (i, 0))
    head = pl.BlockSpec((N_HEADS, TOK_T, HEAD_PAD), lambda i: (0, i, 0))
    return pl.pallas_call(
        body, name="prep_bwd", grid=(S // TOK_T,),
        in_specs=[head] * 6 + [tok(N_MAIN), pl.BlockSpec((4, 128), lambda i: (0, 0)), tok(128), tok(128), tok(128),
                               tok(D_BRANCH), tok(D_BRANCH)],
        out_specs=[tok(N_MAIN), tok(128), pl.BlockSpec((4, 128), lambda i: (0, 0))],
        out_shape=[jax.ShapeDtypeStruct((S, N_MAIN), BF16), jax.ShapeDtypeStruct((S, 128), F32),
                   jax.ShapeDtypeStruct((4, 128), F32)],
        compiler_params=_cp("arbitrary"),
    )(dqa, dka, dva, dqb, dkb, dvb, zm, qkg, rope_c, rope_a, rope_b, dga, dgb)


def _inproj_bwd(dzm, dzf, wm, wf, h0, dh1, g):
    S = h0.shape[0]

    def body(dzm_ref, dzf_ref, wm_ref, wf_ref, h_ref, dh1_ref, g_ref, dh0_ref, dg_ref):
        @pl.when(pl.program_id(0) == 0)
        def _():
            dg_ref[...] = jnp.zeros_like(dg_ref)

        du = _dot_nt(dzm_ref[...], wm_ref[...]) + _dot_nt(dzf_ref[...], wf_ref[...])
        x = h_ref[...]
        r = lax.rsqrt(jnp.mean(x * x, axis=-1, keepdims=True) + EPS)
        xh = x * r
        a = du * g_ref[...]
        dh0_ref[...] = dh1_ref[...] + r * (a - xh * jnp.mean(a * xh, axis=-1, keepdims=True))
        dg_ref[...] += jnp.sum(du * xh, axis=0, keepdims=True)

    tok = lambda w: pl.BlockSpec((TOK_T, w), lambda i: (i, 0))
    full = lambda a, b: pl.BlockSpec((a, b), lambda i: (0, 0))
    return pl.pallas_call(
        body, name="inproj_bwd", grid=(S // TOK_T,),
        in_specs=[tok(N_MAIN), tok(128), full(D_MODEL, N_MAIN), full(D_MODEL, 128), tok(D_MODEL), tok(D_MODEL),
                  full(1, D_MODEL)],
        out_specs=[tok(D_MODEL), full(1, D_MODEL)],
        out_shape=[jax.ShapeDtypeStruct((S, D_MODEL), F32), jax.ShapeDtypeStruct((1, D_MODEL), F32)],
        compiler_params=_cp("arbitrary"),
    )(dzm, dzf, wm, wf, h0, dh1, g)


def _wgrad(a, b, name):
    S, M = a.shape
    N = b.shape[1]
    tn = min(N, 2048)
    ts = 512
    last = S // ts - 1

    def body(a_ref, b_ref, o_ref, acc_ref):
        @pl.when(pl.program_id(1) == 0)
        def _():
            acc_ref[...] = jnp.zeros_like(acc_ref)

        acc_ref[...] += _dot_tn(a_ref[...].astype(BF16), b_ref[...])

        @pl.when(pl.program_id(1) == last)
        def _():
            o_ref[...] = acc_ref[...].astype(BF16)

    return pl.pallas_call(
        body, name=name, grid=(N // tn, S // ts),
        in_specs=[pl.BlockSpec((ts, M), lambda n, s: (s, 0)), pl.BlockSpec((ts, tn), lambda n, s: (s, n))],
        out_specs=pl.BlockSpec((M, tn), lambda n, s: (0, n)),
        out_shape=jax.ShapeDtypeStruct((M, N), BF16),
        scratch_shapes=[pltpu.VMEM((M, tn), F32)],
        compiler_params=_cp("parallel", "arbitrary"),
    )(a, b)


def _rope_tables(positions):
    inv_freq = ROPE_THETA ** (-jnp.arange(ROPE_HALF, dtype=F32) / ROPE_HALF)
    ang = positions.astype(F32)[:, None] * inv_freq
    cos, sin = jnp.cos(ang), jnp.sin(ang)
    S = positions.shape[0]
    one, zero = jnp.ones((S, HEAD_DIM - 2 * ROPE_HALF), F32), jnp.zeros((S, HEAD_DIM - 2 * ROPE_HALF), F32)
    z8 = jnp.zeros((S, ROPE_HALF), F32)
    rc = jnp.concatenate([cos, cos, one], axis=1)
    ra = jnp.concatenate([-sin, z8, zero], axis=1)
    rb = jnp.concatenate([z8, sin, zero], axis=1)
    return tuple(jnp.tile(t, (1, 2)) for t in (rc, ra, rb))


def _layer_weights(w_in, w_out, w_ple, w_pg):
    w_in = w_in.astype(BF16)
    wm = jnp.concatenate([w_in[:, :4 * D_BRANCH], w_in[:, 4 * D_BRANCH + N_HEADS:]], axis=1)
    wf = jnp.pad(w_in[:, 4 * D_BRANCH:4 * D_BRANCH + N_HEADS], ((0, 0), (0, 128 - N_HEADS)))
    w_out, w_ple, w_pg = w_out.astype(BF16), w_ple.astype(BF16), w_pg.astype(BF16)
    return dict(wm=wm, wf=wf, w_out=w_out, w_ple=w_ple, w_pg=w_pg)


def _row(v, width=128):
    v = v.reshape(1, -1).astype(F32)
    return jnp.pad(v, ((0, 0), (0, width - v.shape[1])))


def _layer_fwd(h0, p, rope, tabs, w, norm_g, b_f, qk_g, ple_g):
    S = h0.shape[0]
    g1 = norm_g.reshape(1, D_MODEL)
    g2 = ple_g.reshape(1, D_MODEL)
    qkg = jnp.tile(qk_g, (1, 2))
    bf = _row(b_f)
    zm, zf, u = _inproj_fwd(h0, g1, w["wm"], w["wf"])
    c = _forget_cumsum(zf, bf, tabs["tril"])
    qa, ka, va, qb, kb, vb = _prep_fwd(zm, c, qkg, *rope)
    oa, lse_a = _attn_fwd(qa, ka, va, tabs["fox"], True, "fox_fwd")
    ob, lse_b = _attn_fwd(qb, kb, vb, tabs["dil"], False, "dil_fwd")
    y, h1, h2, u2, e, gate = _mid_fwd(oa, ob, zm, h0, p, w["w_out"], w["w_pg"], w["w_ple"], g2)
    saved = dict(h0=h0, p=p, zm=zm, zf=zf, u=u, qa=qa, ka=ka, va=va, qb=qb, kb=kb, vb=vb, oa=oa, ob=ob,
                 lse_a=lse_a, lse_b=lse_b, y=y, h1=h1, u2=u2, e=e, gate=gate, g1=g1, g2=g2, qkg=qkg, bf=bf)
    return h2, saved


def _layer_bwd(dh2, sv, rope, tabs, w):
    S = dh2.shape[0]
    nq = S // ATT_T
    rows = lambda a: a.reshape(N_HEADS, nq, 1, ATT_T)
    (dh1, dh1b, de, dpre, doa, dob, dga, dgb, dd, dg2) = _mid_bwd(
        dh2, sv["h1"], sv["e"], sv["gate"], sv["g2"], w["w_pg"], w["w_out"], sv["oa"], sv["ob"], sv["zm"])
    dda, ddb = dd[:N_HEADS], dd[N_HEADS:]
    dqa, dka, dva = _attn_bwd(sv["qa"], sv["ka"], sv["va"], doa, sv["lse_a"], rows(dda), tabs["fox"], True,
                              "fox_bwd")
    dqb, dkb, dvb = _attn_bwd(sv["qb"], sv["kb"], sv["vb"], dob, sv["lse_b"], rows(ddb), tabs["dil"], False,
                              "dil_bwd")
    dzm, dc, dqkg = _prep_bwd(dqa, dka, dva, dqb, dkb, dvb, sv["zm"], sv["qkg"], *rope, dga, dgb)
    dzf, dbf = _forget_bwd(dc, sv["zf"], sv["bf"], tabs["triu"])
    dh0, dg1 = _inproj_bwd(dzm, dzf, w["wm"], w["wf"], sv["h0"], dh1, sv["g1"])
    dwm = _wgrad(sv["u"], dzm, "wgrad_in")
    dwf = _wgrad(sv["u"], dzf, "wgrad_f")
    dw_in = jnp.concatenate([dwm[:, :4 * D_BRANCH], dwf[:, :N_HEADS], dwm[:, 4 * D_BRANCH:]], axis=1)
    grads = dict(
        norm_g=dg1.reshape(D_MODEL), w_in=dw_in, b_f=dbf[0, :N_HEADS],
        qk_norm_g=dqkg[:, :HEAD_DIM] + dqkg[:, HEAD_DIM:],
        w_out=_wgrad(sv["y"], dh1b, "wgrad_out"), w_ple=_wgrad(sv["p"], de, "wgrad_ple"),
        ple_norm_g=dg2.reshape(D_MODEL), w_ple_gate=_wgrad(sv["u2"], dpre, "wgrad_gate"))
    return dh0, grads


def _tables():
    T = CUM_T
    r = lax.broadcasted_iota(jnp.int32, (T, T), 0)
    c = lax.broadcasted_iota(jnp.int32, (T, T), 1)
    return dict(fox=_bias_tables(True), dil=_bias_tables(False),
                tril=(c <= r).astype(BF16), triu=(c >= r).astype(BF16))


def _local_step(x, p, positions, target, layers, small):
    rope = _rope_tables(positions)
    tabs = _tables()
    ws = [_layer_weights(*lw) for lw in layers]
    h = x
    saved = []
    for w, lp, sm in zip(ws, p, small):
        h, sv = _layer_fwd(h, lp, rope, tabs, w, *sm)
        saved.append(sv)
    dh, loss = _loss_fwd_bwd(h, target)
    grads = [None] * len(ws)
    for li in reversed(range(len(ws))):
        dh, grads[li] = _layer_bwd(dh, saved[li], rope, tabs, ws[li])
    return loss[0, 0], dh, grads


def _peers():
    x, y, c = lax.axis_index("x"), lax.axis_index("y"), lax.axis_index("c")
    me = 4 * x + 2 * y + c
    flip = lambda v, bit: 1 - v if bit else v
    return me, [(flip(x, k & 4), flip(y, k & 2), flip(c, k & 1)) for k in range(1, N_DEV)]


def _shard(ref, axis, d, size):
    return ref.at[(slice(None),) * axis + (pl.ds(pl.multiple_of(d * size, size), size),)]


def _exchange(name, arrays, out_shapes, src_of, dst_of):
    n = len(arrays)

    def body(*refs):
        ins, outs = refs[:n], refs[n:2 * n]
        send_sems, recv_sems, local_sems = refs[2 * n:]
        me, peers = _peers()
        local = [pltpu.make_async_copy(src_of(a, ins[a], me), dst_of(a, outs[a], me), local_sems.at[a])
                 for a in range(n)]
        for cp in local:
            cp.start()
        copies = [pltpu.make_async_remote_copy(
            src_ref=src_of(a, ins[a], 4 * px + 2 * py + pc), dst_ref=dst_of(a, outs[a], me),
            send_sem=send_sems.at[k, a], recv_sem=recv_sems.at[k, a],
            device_id=(px, py, pc), device_id_type=pl.DeviceIdType.MESH)
            for k, (px, py, pc) in enumerate(peers) for a in range(n)]
        for cp in copies:
            cp.start()
        for cp in copies:
            cp.wait()
        for cp in local:
            cp.wait()

    hbm = pl.BlockSpec(memory_space=pltpu.HBM)
    return pl.pallas_call(
        body, name=name, in_specs=[hbm] * n, out_specs=[hbm] * n, out_shape=out_shapes,
        scratch_shapes=[pltpu.SemaphoreType.DMA((N_DEV - 1, n)), pltpu.SemaphoreType.DMA((N_DEV - 1, n)),
                        pltpu.SemaphoreType.DMA((n,))],
    )(*arrays)


_SHARD_AXES = (None, (1, 128), (2, 128), (1, 128))


def _gather_weights(shards):
    full = [(N_DEV,) + shards[0].shape, (2, D_MODEL, D_MODEL), (2, PLE_DIM, D_MODEL), (2, D_MODEL, D_MODEL)]
    return _exchange(
        "gather_weights", shards, [jax.ShapeDtypeStruct(f, BF16) for f in full],
        src_of=lambda a, ref, peer: ref,
        dst_of=lambda a, ref, me: ref.at[me] if a == 0 else _shard(ref, _SHARD_AXES[a][0], me, _SHARD_AXES[a][1]))


def _exchange_grads(partials, small):
    shard_shapes = [partials[0].shape[1:], (2, 128, D_MODEL), (2, PLE_DIM, 128), (2, 128, D_MODEL), small.shape]

    def src_of(a, ref, peer):
        if a == 0:
            return ref.at[peer]
        return ref if a == len(partials) else _shard(ref, _SHARD_AXES[a][0], peer, _SHARD_AXES[a][1])

    arrays = list(partials) + [small]
    return _exchange(
        "exchange_grads", arrays, [jax.ShapeDtypeStruct((N_DEV,) + s, a.dtype) for s, a in zip(shard_shapes, arrays)],
        src_of=src_of, dst_of=lambda a, ref, me: ref.at[me])


def _adamw(name, parts, w, m, v, rows):
    L, R, C = w.shape

    def body(p_ref, w_ref, m_ref, v_ref, g_ref, d_ref, nm_ref, nv_ref):
        g = p_ref[0, 0].astype(F32)
        for s in range(1, N_DEV):
            g = g + p_ref[s, 0].astype(F32)
        g_ref[0] = g
        nm = ADAM_B1 * m_ref[0] + (1.0 - ADAM_B1) * g
        nv = ADAM_B2 * v_ref[0] + (1.0 - ADAM_B2) * (g * g)
        nm_ref[0] = nm
        nv_ref[0] = nv
        m_hat = nm / (1.0 - ADAM_B1 ** ADAM_STEP)
        v_hat = nv / (1.0 - ADAM_B2 ** ADAM_STEP)
        d_ref[0] = -ADAM_LR * (m_hat / (jnp.sqrt(v_hat) + ADAM_EPS) + ADAM_WD * w_ref[0])

    blk = pl.BlockSpec((1, rows, C), lambda l, i: (l, i, 0))
    shp = jax.ShapeDtypeStruct((L, R, C), F32)
    return pl.pallas_call(
        body, name=name, grid=(L, R // rows),
        in_specs=[pl.BlockSpec((N_DEV, 1, rows, C), lambda l, i: (0, l, i, 0)), blk, blk, blk],
        out_specs=[blk] * 4, out_shape=[shp] * 4,
        compiler_params=_cp("parallel", "parallel"),
    )(parts, w, m, v)


SMALL_ROWS = 40
LOSS_ROW = 37


def _pack_small(norm_g, ple_g, qk_g, b_f, last_row):
    rows = lambda a: a.astype(F32).reshape(-1, 128)
    flat = jnp.concatenate([rows(norm_g), rows(ple_g), rows(qk_g), _row(b_f.reshape(-1)), last_row], axis=0)
    return jnp.pad(flat, ((0, SMALL_ROWS - flat.shape[0]), (0, 0)))


def _unpack_small(flat):
    return (flat[0:16].reshape(2, D_MODEL), flat[16:32].reshape(2, D_MODEL), flat[32:36].reshape(2, 4, HEAD_DIM),
            flat[36, :2 * N_HEADS].reshape(2, N_HEADS))


def kernel(x, p, positions, norm_g, w_in, b_f, qk_norm_g, w_out, w_ple, ple_norm_g, w_ple_gate, loss_target, m_norm_g, m_w_in, m_b_f, m_qk_norm_g, m_w_out, m_w_ple, m_ple_norm_g, m_w_ple_gate, v_norm_g, v_w_in, v_b_f, v_qk_norm_g, v_w_out, v_w_ple, v_ple_norm_g, v_w_ple_gate):
    g_in, full_out, full_ple, full_pg = _gather_weights(
        [w_in.astype(BF16).reshape(W_IN_ROWS, 128)] + [a.astype(BF16) for a in (w_out, w_ple, w_ple_gate)])
    full_in = g_in.reshape(N_DEV, 2, D_MODEL, W_IN_SHARD).transpose(1, 2, 0, 3).reshape(2, D_MODEL, N_IN)
    layers = [(full_in[l], full_out[l], full_ple[l], full_pg[l]) for l in range(2)]
    small = [(norm_g[l], b_f[l], qk_norm_g[l], ple_norm_g[l]) for l in range(2)]
    loss, dx, grads = _local_step(x[0], p[:, 0], positions[0], loss_target[0], layers, small)

    stack = lambda name: jnp.stack([gl[name] for gl in grads], axis=0)
    d_in = stack("w_in").reshape(2, D_MODEL, N_DEV, W_IN_SHARD).transpose(2, 0, 1, 3).reshape(N_DEV, W_IN_ROWS, 128)
    small_part = _pack_small(stack("norm_g"), stack("ple_norm_g"), stack("qk_norm_g"), stack("b_f"),
                             _row(loss.reshape(1)))
    r_in, r_out, r_ple, r_pg, r_small = _exchange_grads([d_in, stack("w_out"), stack("w_ple"), stack("w_ple_gate")],
                                                        small_part)

    zero_row = jnp.zeros((1, 128), F32)
    small_of = lambda ng, pg, qk, bf: _pack_small(ng, pg, qk, bf, zero_row)[None]
    flat = lambda a: a.reshape(1, W_IN_ROWS, 128)
    outs = dict(
        w_in=[o.reshape(w_in.shape) for o in
              _adamw("adamw_in", r_in[:, None], flat(w_in), flat(m_w_in), flat(v_w_in), W_IN_TILE)],
        w_out=_adamw("adamw_out", r_out, w_out, m_w_out, v_w_out, 128),
        w_ple=_adamw("adamw_ple", r_ple, w_ple, m_w_ple, v_w_ple, 256),
        w_pg=_adamw("adamw_gate", r_pg, w_ple_gate, m_w_ple_gate, v_w_ple_gate, 128),
        small=_adamw("adamw_small", r_small[:, None], small_of(norm_g, ple_norm_g, qk_norm_g, b_f),
                     small_of(m_norm_g, m_ple_norm_g, m_qk_norm_g, m_b_f),
                     small_of(v_norm_g, v_ple_norm_g, v_qk_norm_g, v_b_f), SMALL_ROWS))
    leaves = []
    for kind in range(4):
        ng, pg, qk, bf = _unpack_small(outs["small"][kind][0])
        leaves += [ng, outs["w_in"][kind], bf, qk, outs["w_out"][kind], outs["w_ple"][kind], pg, outs["w_pg"][kind]]
    return (outs["small"][0][0, LOSS_ROW, 0], dx[None], *leaves)
```

```python
import functools

import jax
import jax.numpy as jnp
from jax import lax
from jax.experimental import pallas as pl
from jax.experimental.pallas import tpu as pltpu

F32 = jnp.float32
BF16 = jnp.bfloat16

D_MODEL = 1024
HEAD_DIM = 64
N_HEADS = 8
HEAD_PAD = 128
D_BRANCH = N_HEADS * HEAD_DIM
N_MAIN = 8 * D_BRANCH
N_IN = N_MAIN + N_HEADS
PLE_DIM = 256
ROPE_THETA = 500000.0
ROPE_HALF = 8
EPS = 1e-6
NEG = -1e30
SCALE = HEAD_DIM ** -0.5
LOG2E = 1.4426950408889634
LN2 = 0.6931471805599453
DILATED_PATTERNS = ((128, 1), (512, 4), (2048, 16))
N_DEV = 8
W_IN_SHARD = N_IN // N_DEV
W_IN_ROWS = 2 * D_MODEL * W_IN_SHARD // 128
W_IN_TILE = W_IN_ROWS // 19

ADAM_LR = 0.001
ADAM_B1 = 0.9
ADAM_B2 = 0.999
ADAM_EPS = 1e-08
ADAM_WD = 0.01
ADAM_STEP = 10

ATT_T = 512
ATT_FWD_HEADS = 2
ATT_BWD_HEADS = 2
ATT_CHUNK = 32
TOK_T = 256
CUM_T = 512
VMEM_LIMIT = 56 * 1024 * 1024


def _cp(*sem):
    return pltpu.CompilerParams(dimension_semantics=sem, vmem_limit_bytes=VMEM_LIMIT)


def _sigmoid(x):
    return 1.0 / (1.0 + jnp.exp(-x))


def _split3(x):
    hi = x.astype(BF16)
    r1 = x - hi.astype(F32)
    mid = r1.astype(BF16)
    lo = (r1 - mid.astype(F32)).astype(BF16)
    return hi, mid, lo


def _dot(a, b):
    return jnp.dot(a, b, preferred_element_type=F32)


def _dot_nt(a, b):
    return lax.dot_general(a, b, (((1,), (1,)), ((), ())), preferred_element_type=F32)


def _dot_tn(a, b):
    return lax.dot_general(a, b, (((0,), (0,)), ((), ())), preferred_element_type=F32)


def _inproj_fwd(h, g, wm, wf):
    S = h.shape[0]

    def body(h_ref, g_ref, wm_ref, wf_ref, zm_ref, zf_ref, u_ref):
        x = h_ref[...]
        r = lax.rsqrt(jnp.mean(x * x, axis=-1, keepdims=True) + EPS)
        u = (x * r * g_ref[...]).astype(BF16)
        u_ref[...] = u
        zm_ref[...] = _dot(u, wm_ref[...])
        zf_ref[...] = _dot(u, wf_ref[...])

    return pl.pallas_call(
        body, name="inproj_fwd", grid=(S // TOK_T,),
        in_specs=[pl.BlockSpec((TOK_T, D_MODEL), lambda i: (i, 0)),
                  pl.BlockSpec((1, D_MODEL), lambda i: (0, 0)),
                  pl.BlockSpec((D_MODEL, N_MAIN), lambda i: (0, 0)),
                  pl.BlockSpec((D_MODEL, 128), lambda i: (0, 0))],
        out_specs=[pl.BlockSpec((TOK_T, N_MAIN), lambda i: (i, 0)),
                   pl.BlockSpec((TOK_T, 128), lambda i: (i, 0)),
                   pl.BlockSpec((TOK_T, D_MODEL), lambda i: (i, 0))],
        out_shape=[jax.ShapeDtypeStruct((S, N_MAIN), F32), jax.ShapeDtypeStruct((S, 128), F32),
                   jax.ShapeDtypeStruct((S, D_MODEL), BF16)],
        compiler_params=_cp("parallel"),
    )(h, g, wm, wf)


def _log_sigmoid(x):
    return jnp.minimum(x, 0.0) - jnp.log(1.0 + jnp.exp(-jnp.abs(x)))


def _forget_cumsum(zf, bf, tri):
    S = zf.shape[0]

    def body(zf_ref, b_ref, tri_ref, c_ref, carry):
        @pl.when(pl.program_id(0) == 0)
        def _():
            carry[...] = jnp.zeros_like(carry)

        lf = _log_sigmoid(zf_ref[...] + b_ref[...])
        hi, mid, lo = _split3(lf)
        t = tri_ref[...]
        cs = _dot(t, hi) + _dot(t, mid) + _dot(t, lo) + carry[...]
        c_ref[...] = cs
        carry[...] = cs[CUM_T - 1:CUM_T, :]

    return pl.pallas_call(
        body, name="forget_cumsum", grid=(S // CUM_T,),
        in_specs=[pl.BlockSpec((CUM_T, 128), lambda i: (i, 0)),
                  pl.BlockSpec((1, 128), lambda i: (0, 0)),
                  pl.BlockSpec((CUM_T, CUM_T), lambda i: (0, 0))],
        out_specs=pl.BlockSpec((CUM_T, 128), lambda i: (i, 0)),
        out_shape=jax.ShapeDtypeStruct((S, 128), F32),
        scratch_shapes=[pltpu.VMEM((1, 128), F32)],
        compiler_params=_cp("arbitrary"),
    )(zf, bf, tri)


def _forget_bwd(dc, zf, bf, triu):
    S = zf.shape[0]
    n = S // CUM_T

    def body(dc_ref, zf_ref, b_ref, tri_ref, dzf_ref, db_ref, carry):
        @pl.when(pl.program_id(0) == 0)
        def _():
            carry[...] = jnp.zeros_like(carry)
            db_ref[...] = jnp.zeros_like(db_ref)

        hi, mid, lo = _split3(dc_ref[...])
        t = tri_ref[...]
        dlf = _dot(t, hi) + _dot(t, mid) + _dot(t, lo) + carry[...]
        carry[...] = dlf[0:1, :]
        dfa = dlf * (1.0 - _sigmoid(zf_ref[...] + b_ref[...]))
        dzf_ref[...] = dfa.astype(BF16)
        db_ref[...] += jnp.sum(dfa, axis=0, keepdims=True)

    return pl.pallas_call(
        body, name="forget_bwd", grid=(n,),
        in_specs=[pl.BlockSpec((CUM_T, 128), lambda i: (n - 1 - i, 0)),
                  pl.BlockSpec((CUM_T, 128), lambda i: (n - 1 - i, 0)),
                  pl.BlockSpec((1, 128), lambda i: (0, 0)),
                  pl.BlockSpec((CUM_T, CUM_T), lambda i: (0, 0))],
        out_specs=[pl.BlockSpec((CUM_T, 128), lambda i: (n - 1 - i, 0)),
                   pl.BlockSpec((1, 128), lambda i: (0, 0))],
        out_shape=[jax.ShapeDtypeStruct((S, 128), BF16), jax.ShapeDtypeStruct((1, 128), F32)],
        scratch_shapes=[pltpu.VMEM((1, 128), F32)],
        compiler_params=_cp("arbitrary"),
    )(dc, zf, bf, triu)


def _same_head():
    r = lax.broadcasted_iota(jnp.int32, (HEAD_PAD, HEAD_PAD), 0) // HEAD_DIM
    c = lax.broadcasted_iota(jnp.int32, (HEAD_PAD, HEAD_PAD), 1) // HEAD_DIM
    return (r == c).astype(BF16)


def _pair_mean(x, same_head):
    hi = x.astype(BF16)
    lo = (x - hi.astype(F32)).astype(BF16)
    return (_dot(hi, same_head) + _dot(lo, same_head)) * (1.0 / HEAD_DIM)


def _pair_rsqrt(x, same_head):
    return lax.rsqrt(_pair_mean(x * x, same_head) + EPS)


def _prep_fwd(zm, c, qkg, rope_c, rope_a, rope_b):
    S = zm.shape[0]
    shp = jax.ShapeDtypeStruct((N_HEADS, S, HEAD_PAD), BF16)

    def body(z_ref, c_ref, g_ref, rc_ref, ra_ref, rb_ref, qa_ref, ka_ref, va_ref, qb_ref, kb_ref, vb_ref):
        lane = lax.broadcasted_iota(jnp.int32, (TOK_T, HEAD_PAD), 1)
        lo_half = lane < HEAD_DIM
        aug = (lane >= HEAD_DIM) & (lane < HEAD_DIM + 3)
        q_pad = jnp.where(aug, -1.0, 0.0)
        cs = c_ref[...]
        rc, ra, rb = rc_ref[...], ra_ref[...], rb_ref[...]
        same_head = _same_head()

        def norm(col, gi):
            x = z_ref[:, col:col + HEAD_PAD]
            return x * _pair_rsqrt(x, same_head) * g_ref[gi:gi + 1, :]

        def rope(y):
            return y * rc + pltpu.roll(y, HEAD_PAD - ROPE_HALF, 1) * ra + pltpu.roll(y, ROPE_HALF, 1) * rb

        def put(ref, pi, y, pad_even, pad_odd):
            ref[2 * pi] = jnp.where(lo_half, y, pad_even).astype(BF16)
            ref[2 * pi + 1] = jnp.where(lo_half, pltpu.roll(y, HEAD_DIM, 1), pad_odd).astype(BF16)

        def k_pad(h):
            ch = cs[:, h:h + 1] * LOG2E
            hi = ch.astype(BF16).astype(F32)
            mid = (ch - hi).astype(BF16).astype(F32)
            lo = ch - hi - mid
            ones = jnp.where(lane == HEAD_DIM + 3, 1.0, 0.0)
            return jnp.where(lane == HEAD_DIM, hi, jnp.where(lane == HEAD_DIM + 1, mid,
                                                              jnp.where(lane == HEAD_DIM + 2, lo, ones)))

        for pi in range(N_HEADS // 2):
            col = HEAD_PAD * pi
            put(qa_ref, pi, norm(col, 0) * (SCALE * LOG2E), q_pad, q_pad)
            put(ka_ref, pi, norm(D_BRANCH + col, 1), k_pad(2 * pi), k_pad(2 * pi + 1))
            put(va_ref, pi, z_ref[:, 2 * D_BRANCH + col:2 * D_BRANCH + col + HEAD_PAD], 0.0, 0.0)
            put(qb_ref, pi, rope(norm(4 * D_BRANCH + col, 2)) * (SCALE * LOG2E), 0.0, 0.0)
            put(kb_ref, pi, rope(norm(5 * D_BRANCH + col, 3)), 0.0, 0.0)
            put(vb_ref, pi, z_ref[:, 6 * D_BRANCH + col:6 * D_BRANCH + col + HEAD_PAD], 0.0, 0.0)

    tok = lambda w: pl.BlockSpec((TOK_T, w), lambda i: (i, 0))
    head = pl.BlockSpec((N_HEADS, TOK_T, HEAD_PAD), lambda i: (0, i, 0))
    return pl.pallas_call(
        body, name="prep_fwd", grid=(S // TOK_T,),
        in_specs=[tok(N_MAIN), tok(128), pl.BlockSpec((4, 128), lambda i: (0, 0)), tok(128), tok(128), tok(128)],
        out_specs=[head] * 6, out_shape=[shp] * 6,
        compiler_params=_cp("parallel"),
    )(zm, c, qkg, rope_c, rope_a, rope_b)


def _pair(ref, pi, lo_half):
    return jnp.where(lo_half, ref[2 * pi], pltpu.roll(ref[2 * pi + 1], HEAD_DIM, 1))


def _mid_fwd(oa, ob, zm, h0, p, w_out, w_pg, w_ple, g2):
    S = h0.shape[0]

    def body(oa_ref, ob_ref, ga_ref, gb_ref, h0_ref, p_ref, wo_ref, wg_ref, wp_ref, g2_ref,
             y_ref, h1_ref, h2_ref, u2_ref, e_ref, gate_ref):
        lane = lax.broadcasted_iota(jnp.int32, (TOK_T, HEAD_PAD), 1)
        lo_half = lane < HEAD_DIM
        parts = []
        for o_ref, g_ref in ((oa_ref, ga_ref), (ob_ref, gb_ref)):
            for pi in range(N_HEADS // 2):
                g = g_ref[:, HEAD_PAD * pi:HEAD_PAD * (pi + 1)]
                parts.append((_pair(o_ref, pi, lo_half) * (g * _sigmoid(g))).astype(BF16))
        y = jnp.concatenate(parts, axis=1)
        y_ref[...] = y
        h1 = h0_ref[...] + _dot(y, wo_ref[...])
        h1_ref[...] = h1
        r = lax.rsqrt(jnp.mean(h1 * h1, axis=-1, keepdims=True) + EPS)
        u2 = (h1 * r * g2_ref[...]).astype(BF16)
        u2_ref[...] = u2
        gate = _sigmoid(_dot(u2, wg_ref[...]))
        e = _dot(p_ref[...].astype(BF16), wp_ref[...])
        e_ref[...] = e
        gate_ref[...] = gate
        h2_ref[...] = h1 + e * gate

    tok = lambda w: pl.BlockSpec((TOK_T, w), lambda i: (i, 0))
    head = pl.BlockSpec((N_HEADS, TOK_T, HEAD_PAD), lambda i: (0, i, 0))
    full = lambda a, b: pl.BlockSpec((a, b), lambda i: (0, 0))
    act = lambda dt: jax.ShapeDtypeStruct((S, D_MODEL), dt)
    return pl.pallas_call(
        body, name="mid_fwd", grid=(S // TOK_T,),
        in_specs=[head, head,
                  pl.BlockSpec((TOK_T, D_BRANCH), lambda i: (i, 3)), pl.BlockSpec((TOK_T, D_BRANCH), lambda i: (i, 7)),
                  tok(D_MODEL), tok(PLE_DIM), full(D_MODEL, D_MODEL), full(D_MODEL, D_MODEL),
                  full(PLE_DIM, D_MODEL), full(1, D_MODEL)],
        out_specs=[tok(D_MODEL)] * 6,
        out_shape=[act(BF16), act(F32), act(F32), act(BF16), act(F32), act(F32)],
        compiler_params=_cp("parallel"),
    )(oa, ob, zm, zm, h0, p, w_out, w_pg, w_ple, g2)


def _loss_fwd_bwd(y, t):
    S = y.shape[0]

    def body(y_ref, t_ref, dy_ref, loss_ref):
        @pl.when(pl.program_id(0) == 0)
        def _():
            loss_ref[...] = jnp.zeros_like(loss_ref)

        err = y_ref[...] - t_ref[...]
        dy_ref[...] = err * (1.0 / D_MODEL)
        part = jnp.sum(jnp.sum(err * err, axis=1, keepdims=True), axis=0, keepdims=True)
        loss_ref[...] += part * (0.5 / D_MODEL)

    tok = pl.BlockSpec((TOK_T, D_MODEL), lambda i: (i, 0))
    return pl.pallas_call(
        body, name="loss", grid=(S // TOK_T,),
        in_specs=[tok, tok], out_specs=[tok, pl.BlockSpec((8, 128), lambda i: (0, 0))],
        out_shape=[jax.ShapeDtypeStruct((S, D_MODEL), F32), jax.ShapeDtypeStruct((8, 128), F32)],
        compiler_params=_cp("arbitrary"),
    )(y, t)


def _bias_tables(full_range):
    T = ATT_T
    nb = 1 if full_range else DILATED_PATTERNS[-1][0] // T + 1
    r = lax.broadcasted_iota(jnp.int32, (nb, T, T), 2)
    c = lax.broadcasted_iota(jnp.int32, (nb, T, T), 1)
    b = lax.broadcasted_iota(jnp.int32, (nb, T, T), 0)
    delta = T * b + r - c
    if full_range:
        bias = jnp.where(delta >= 0, 0.0, NEG).astype(F32)
    else:
        mult = jnp.zeros((nb, T, T), F32)
        for window, dil in DILATED_PATTERNS:
            ok = (delta >= 0) & (delta <= window) & (delta % dil == 0)
            mult = mult + ok.astype(F32)
        bias = jnp.where(mult > 0, jnp.log2(jnp.maximum(mult, 1.0)), NEG).astype(F32)
    return bias


def _call_with_rider(body, name, grid, rider, in_specs, out_specs, out_shape, scratch_shapes, operands):
    if rider is None:
        return pl.pallas_call(body, name=name, grid=grid, in_specs=in_specs, out_specs=out_specs,
                              out_shape=out_shape, scratch_shapes=scratch_shapes,
                              compiler_params=_cp("parallel", "arbitrary"))(*operands)
    n, n_in, n_out = rider.n, len(in_specs), len(out_specs)

    def wrapped(*refs):
        ins, r_ins = refs[:n_in], refs[n_in:n_in + n]
        outs, r_outs = refs[n_in + n:n_in + n + n_out], refs[n_in + n + n_out:n_in + 2 * n + n_out]
        scratch, sems = refs[n_in + 2 * n + n_out:-3], refs[-3:]
        step = [pl.program_id(a) for a in range(len(grid))]

        @pl.when(functools.reduce(jnp.logical_and, [s == 0 for s in step]))
        def _():
            rider.start(r_ins, r_outs, sems)

        body(*ins, *outs, *scratch)

        @pl.when(functools.reduce(jnp.logical_and, [s == g - 1 for s, g in zip(step, grid)]))
        def _():
            rider.wait(r_ins, r_outs, sems)

    return pl.pallas_call(
        wrapped, name=name, grid=grid, in_specs=list(in_specs) + rider.in_specs,
        out_specs=list(out_specs) + rider.out_specs, out_shape=list(out_shape) + rider.out_shapes,
        scratch_shapes=list(scratch_shapes) + rider.scratch_shapes,
        compiler_params=_cp("arbitrary", "arbitrary"))(*operands, *rider.arrays)


def _attn_fwd(q, k, v, table_t, full_range, name, rider=None):
    H, S, _ = q.shape
    T = ATT_T
    nb = table_t.shape[0]
    HB = ATT_FWD_HEADS
    KC = ATT_CHUNK
    chunks = [slice(c, c + KC) for c in range(0, T, KC)]
    fold = lambda x, op: functools.reduce(op, [x[r:r + 8] for r in range(0, KC, 8)])

    def body(q_ref, k_ref, v_ref, tab_ref, o_ref, lse_ref, *scratch):
        st_refs, pt_refs, acc_refs = scratch[:HB], scratch[HB:2 * HB], scratch[2 * HB:]
        i = pl.program_id(1)
        rows = lambda j: pl.ds(pl.multiple_of(j * T, T), T)

        def scores(hh, j):
            st_refs[hh][...] = _dot_nt(k_ref[hh, rows(j), :], q_ref[hh])

        def block(j, b, nxt, stats):
            out = []
            for hh, (m, l) in enumerate(stats):
                st_ref, pt_ref, acc_ref = st_refs[hh], pt_refs[hh], acc_refs[hh]
                mx = None
                for ch in chunks:
                    x = st_ref[ch, :]
                    if b is not None:
                        x = x + tab_ref[b, ch, :]
                        st_ref[ch, :] = x
                    x = fold(x, jnp.maximum)
                    mx = x if mx is None else jnp.maximum(mx, x)
                m_new = jnp.maximum(m, jnp.max(mx, axis=0, keepdims=True))
                alpha = jnp.exp2(m - m_new)
                ls = None
                for ch in chunks:
                    pc = jnp.exp2(st_ref[ch, :] - m_new)
                    pt_ref[ch, :] = pc.astype(BF16)
                    pc = fold(pc, jnp.add)
                    ls = pc if ls is None else ls + pc
                if nxt is not None:
                    scores(hh, nxt)
                acc_ref[...] = alpha * acc_ref[...] + _dot_tn(v_ref[hh, rows(j), :], pt_ref[...])
                out.append((m_new, alpha * l + jnp.sum(ls, axis=0, keepdims=True)))
            return tuple(out)

        lo = 0 if full_range else jnp.maximum(i - (nb - 1), 0)
        for hh in range(HB):
            acc_refs[hh][...] = jnp.zeros_like(acc_refs[hh])
            scores(hh, lo)
        stats = lax.fori_loop(lo, i, lambda j, st: block(j, None if full_range else i - j, j + 1, st),
                              ((jnp.full((1, T), NEG, F32), jnp.zeros((1, T), F32)),) * HB)
        stats = block(i, 0, None, stats)
        for hh, (m, l) in enumerate(stats):
            o_ref[hh] = (acc_refs[hh][...] * (1.0 / l)).T
            lse_ref[hh, 0] = m + jnp.log2(l)

    return _call_with_rider(
        body, name, (H // HB, S // T), rider,
        in_specs=[pl.BlockSpec((HB, T, HEAD_PAD), lambda h, i: (h, i, 0)),
                  pl.BlockSpec((HB, S, HEAD_PAD), lambda h, i: (h, 0, 0)),
                  pl.BlockSpec((HB, S, HEAD_PAD), lambda h, i: (h, 0, 0)),
                  pl.BlockSpec((nb, T, T), lambda h, i: (0, 0, 0))],
        out_specs=[pl.BlockSpec((HB, T, HEAD_PAD), lambda h, i: (h, i, 0)),
                   pl.BlockSpec((HB, 1, 1, T), lambda h, i: (h, i, 0, 0))],
        out_shape=[jax.ShapeDtypeStruct((H, S, HEAD_PAD), F32), jax.ShapeDtypeStruct((H, S // T, 1, T), F32)],
        scratch_shapes=([pltpu.VMEM((T, T), F32)] * HB + [pltpu.VMEM((T, T), BF16)] * HB
                        + [pltpu.VMEM((HEAD_PAD, T), F32)] * HB),
        operands=(q, k, v, table_t))


def _attn_bwd(q, k, v, do, lse, dd, table_t, full_range, name, rider=None):
    H, S, _ = q.shape
    T = ATT_T
    nq = S // T
    nb = table_t.shape[0]
    HB = ATT_BWD_HEADS
    KC = ATT_CHUNK
    chunks = [slice(c, c + KC) for c in range(0, T, KC)]

    def body(q_ref, do_ref, lse_ref, dd_ref, k_ref, v_ref, tab_ref, dq_hbm, dk_ref, dv_ref, *scratch):
        st_refs, dpt_refs, pt_refs, dst_refs = (scratch[n * HB:(n + 1) * HB] for n in range(4))
        dq_ref, dq_sem = scratch[4 * HB:]
        h = pl.program_id(0)
        j = pl.program_id(1)

        @pl.when(j == 0)
        def _():
            dq_ref[...] = jnp.zeros_like(dq_ref)

        dk_ref[...] = jnp.zeros_like(dk_ref)
        dv_ref[...] = jnp.zeros_like(dv_ref)

        def step(i, b):
            rows = pl.ds(pl.multiple_of(i * T, T), T)
            for hh in range(HB):
                st_refs[hh][...] = _dot_nt(k_ref[hh], q_ref[hh, rows, :])
                dpt_refs[hh][...] = _dot_nt(v_ref[hh], do_ref[hh, rows, :])
            for hh in range(HB):
                lse_i = lse_ref[hh, i]
                dd_i = dd_ref[hh, i]
                for ch in chunks:
                    x = st_refs[hh][ch, :]
                    if b is not None:
                        x = x + tab_ref[b, ch, :]
                    pc = jnp.exp2(x - lse_i)
                    pt_refs[hh][ch, :] = pc.astype(BF16)
                    dst_refs[hh][ch, :] = (pc * (dpt_refs[hh][ch, :] - dd_i)).astype(BF16)
                dv_ref[hh] += _dot(pt_refs[hh][...], do_ref[hh, rows, :])
                dk_ref[hh] += _dot(dst_refs[hh][...], q_ref[hh, rows, :])
                dq_ref[hh, rows, :] += _dot_tn(dst_refs[hh][...], k_ref[hh])

        step(j, 0)
        if full_range:
            pl.loop(j + 1, nq)(lambda i: step(i, None))
        else:
            pl.loop(j + 1, jnp.minimum(j + nb, nq))(lambda i: step(i, i - j))

        @pl.when(j == nq - 1)
        def _():
            out = pltpu.make_async_copy(dq_ref, dq_hbm.at[pl.ds(h * HB, HB)], dq_sem)
            out.start()
            out.wait()

    once = dict(pipeline_mode=pl.Buffered(1))
    per_head = pl.BlockSpec((HB, S, HEAD_PAD), lambda h, j: (h, 0, 0), **once)
    rows = pl.BlockSpec((HB, nq, 1, T), lambda h, j: (h, 0, 0, 0))
    blk = pl.BlockSpec((HB, T, HEAD_PAD), lambda h, j: (h, j, 0))
    shp = jax.ShapeDtypeStruct((H, S, HEAD_PAD), F32)
    return _call_with_rider(
        body, name, (H // HB, nq), rider,
        in_specs=[per_head, per_head, rows, rows, blk, blk,
                  pl.BlockSpec((nb, T, T), lambda h, j: (0, 0, 0), **once)],
        out_specs=[pl.BlockSpec(memory_space=pltpu.HBM), blk, blk], out_shape=[shp, shp, shp],
        scratch_shapes=([pltpu.VMEM((T, T), F32)] * (2 * HB) + [pltpu.VMEM((T, T), BF16)] * (2 * HB)
                        + [pltpu.VMEM((HB, S, HEAD_PAD), F32), pltpu.SemaphoreType.DMA]),
        operands=(q, do, lse, dd, k, v, table_t))


def _mid_bwd(dh2, h1, e, gate, g2, w_pg, w_out, oa, ob, zm):
    S = dh2.shape[0]

    def body(dh2_ref, h1_ref, e_ref, gate_ref, g2_ref, wg_ref, wo_ref, oa_ref, ob_ref, ga_ref, gb_ref,
             dh1_ref, dh1b_ref, de_ref, dpre_ref, doa_ref, dob_ref, dga_ref, dgb_ref, dd_ref, dg2_ref):
        @pl.when(pl.program_id(0) == 0)
        def _():
            dg2_ref[...] = jnp.zeros_like(dg2_ref)

        lane = lax.broadcasted_iota(jnp.int32, (TOK_T, HEAD_PAD), 1)
        lo_half = lane < HEAD_DIM
        dh2 = dh2_ref[...]
        gate = gate_ref[...]
        de_ref[...] = (dh2 * gate).astype(BF16)
        dpre = (dh2 * e_ref[...] * gate * (1.0 - gate)).astype(BF16)
        dpre_ref[...] = dpre
        du2 = _dot_nt(dpre, wg_ref[...])
        h1 = h1_ref[...]
        r = lax.rsqrt(jnp.mean(h1 * h1, axis=-1, keepdims=True) + EPS)
        xh = h1 * r
        a = du2 * g2_ref[...]
        dh1 = dh2 + r * (a - xh * jnp.mean(a * xh, axis=-1, keepdims=True))
        dg2_ref[...] += jnp.sum(du2 * xh, axis=0, keepdims=True)
        dh1_ref[...] = dh1
        dh1b = dh1.astype(BF16)
        dh1b_ref[...] = dh1b
        dy = _dot_nt(dh1b, wo_ref[...])
        dd = jnp.zeros((TOK_T, HEAD_PAD), F32)
        for bi, (o_ref, g_ref, do_ref, dg_ref) in enumerate(
                ((oa_ref, ga_ref, doa_ref, dga_ref), (ob_ref, gb_ref, dob_ref, dgb_ref))):
            for pi in range(N_HEADS // 2):
                col = bi * D_BRANCH + HEAD_PAD * pi
                dyp = dy[:, col:col + HEAD_PAD]
                g = g_ref[:, HEAD_PAD * pi:HEAD_PAD * (pi + 1)]
                sg = _sigmoid(g)
                dg_ref[:, HEAD_PAD * pi:HEAD_PAD * (pi + 1)] = (
                    dyp * _pair(o_ref, pi, lo_half) * (sg * (1.0 + g * (1.0 - sg)))).astype(BF16)
                dop = dyp * (g * sg)
                for hh, d_head in ((2 * pi, dop), (2 * pi + 1, pltpu.roll(dop, HEAD_DIM, 1))):
                    d_head = jnp.where(lo_half, d_head, 0.0)
                    do_ref[hh] = d_head.astype(BF16)
                    dsum = jnp.sum(d_head * o_ref[hh], axis=1, keepdims=True)
                    dd = dd + jnp.where(lane == bi * N_HEADS + hh, dsum, 0.0)
        dd_ref[...] = dd.T[:2 * N_HEADS, :]

    tok = lambda w: pl.BlockSpec((TOK_T, w), lambda i: (i, 0))
    head = pl.BlockSpec((N_HEADS, TOK_T, HEAD_PAD), lambda i: (0, i, 0))
    full = lambda a, b: pl.BlockSpec((a, b), lambda i: (0, 0))
    act = lambda w, dt: jax.ShapeDtypeStruct((S, w), dt)
    hshape = lambda w, dt: jax.ShapeDtypeStruct((N_HEADS, S, w), dt)
    return pl.pallas_call(
        body, name="mid_bwd", grid=(S // TOK_T,),
        in_specs=[tok(D_MODEL)] * 4 + [full(1, D_MODEL), full(D_MODEL, D_MODEL), full(D_MODEL, D_MODEL), head, head,
                                      pl.BlockSpec((TOK_T, D_BRANCH), lambda i: (i, 3)),
                                      pl.BlockSpec((TOK_T, D_BRANCH), lambda i: (i, 7))],
        out_specs=[tok(D_MODEL)] * 4 + [head, head, tok(D_BRANCH), tok(D_BRANCH),
                                       pl.BlockSpec((2 * N_HEADS, TOK_T), lambda i: (0, i)), full(1, D_MODEL)],
        out_shape=[act(D_MODEL, F32), act(D_MODEL, BF16), act(D_MODEL, BF16), act(D_MODEL, BF16),
                   hshape(HEAD_PAD, BF16), hshape(HEAD_PAD, BF16), act(D_BRANCH, BF16), act(D_BRANCH, BF16),
                   jax.ShapeDtypeStruct((2 * N_HEADS, S), F32), jax.ShapeDtypeStruct((1, D_MODEL), F32)],
        compiler_params=_cp("arbitrary"),
    )(dh2, h1, e, gate, g2, w_pg, w_out, oa, ob, zm, zm)


def _prep_bwd(dqa, dka, dva, dqb, dkb, dvb, zm, qkg, rope_c, rope_a, rope_b, dga, dgb):
    S = zm.shape[0]

    def body(dqa_ref, dka_ref, dva_ref, dqb_ref, dkb_ref, dvb_ref, z_ref, g_ref, rc_ref, ra_ref, rb_ref,
             dga_ref, dgb_ref, dz_ref, dc_ref, dqkg_ref):
        @pl.when(pl.program_id(0) == 0)
        def _():
            dqkg_ref[...] = jnp.zeros_like(dqkg_ref)

        lane = lax.broadcasted_iota(jnp.int32, (TOK_T, HEAD_PAD), 1)
        lo_half = lane < HEAD_DIM
        rc, ra, rb = rc_ref[...], ra_ref[...], rb_ref[...]
        same_head = _same_head()

        def unrope(dy):
            return dy * rc + pltpu.roll(dy * ra, ROPE_HALF, 1) + pltpu.roll(dy * rb, HEAD_PAD - ROPE_HALF, 1)

        def norm_bwd(col, gi, dy):
            x = z_ref[:, col:col + HEAD_PAD]
            r = _pair_rsqrt(x, same_head)
            xh = x * r
            dqkg_ref[gi:gi + 1, :] += jnp.sum(dy * xh, axis=0, keepdims=True)
            a = dy * g_ref[gi:gi + 1, :]
            dz_ref[:, col:col + HEAD_PAD] = (r * (a - xh * _pair_mean(a * xh, same_head))).astype(BF16)

        dc = jnp.zeros((TOK_T, HEAD_PAD), F32)
        for pi in range(N_HEADS // 2):
            col = HEAD_PAD * pi
            norm_bwd(col, 0, _pair(dqa_ref, pi, lo_half) * SCALE)
            norm_bwd(D_BRANCH + col, 1, _pair(dka_ref, pi, lo_half) * LN2)
            dz_ref[:, 2 * D_BRANCH + col:2 * D_BRANCH + col + HEAD_PAD] = _pair(dva_ref, pi, lo_half).astype(BF16)
            norm_bwd(4 * D_BRANCH + col, 2, unrope(_pair(dqb_ref, pi, lo_half) * SCALE))
            norm_bwd(5 * D_BRANCH + col, 3, unrope(_pair(dkb_ref, pi, lo_half) * LN2))
            dz_ref[:, 6 * D_BRANCH + col:6 * D_BRANCH + col + HEAD_PAD] = _pair(dvb_ref, pi, lo_half).astype(BF16)
            for hh in (2 * pi, 2 * pi + 1):
                dch = dka_ref[hh][:, HEAD_DIM:HEAD_DIM + 1] + dqa_ref[hh][:, HEAD_DIM + 3:HEAD_DIM + 4]
                dc = dc + jnp.where(lane == hh, dch, 0.0)
        dz_ref[:, 3 * D_BRANCH:4 * D_BRANCH] = dga_ref[...]
        dz_ref[:, 7 * D_BRANCH:8 * D_BRANCH] = dgb_ref[...]
        dc_ref[...] = dc

    tok = lambda w: pl.BlockSpec((TOK_T, w), lambda i: (i, 0))
    head = pl.BlockSpec((N_HEADS, TOK_T, HEAD_PAD), lambda i: (0, i, 0))
    return pl.pallas_call(
        body, name="prep_bwd", grid=(S // TOK_T,),
        in_specs=[head] * 6 + [tok(N_MAIN), pl.BlockSpec((4, 128), lambda i: (0, 0)), tok(128), tok(128), tok(128),
                               tok(D_BRANCH), tok(D_BRANCH)],
        out_specs=[tok(N_MAIN), tok(128), pl.BlockSpec((4, 128), lambda i: (0, 0))],
        out_shape=[jax.ShapeDtypeStruct((S, N_MAIN), BF16), jax.ShapeDtypeStruct((S, 128), F32),
                   jax.ShapeDtypeStruct((4, 128), F32)],
        compiler_params=_cp("arbitrary"),
    )(dqa, dka, dva, dqb, dkb, dvb, zm, qkg, rope_c, rope_a, rope_b, dga, dgb)


def _inproj_bwd(dzm, dzf, wm, wf, h0, dh1, g):
    S = h0.shape[0]

    def body(dzm_ref, dzf_ref, wm_ref, wf_ref, h_ref, dh1_ref, g_ref, dh0_ref, dg_ref):
        @pl.when(pl.program_id(0) == 0)
        def _():
            dg_ref[...] = jnp.zeros_like(dg_ref)

        du = _dot_nt(dzm_ref[...], wm_ref[...]) + _dot_nt(dzf_ref[...], wf_ref[...])
        x = h_ref[...]
        r = lax.rsqrt(jnp.mean(x * x, axis=-1, keepdims=True) + EPS)
        xh = x * r
        a = du * g_ref[...]
        dh0_ref[...] = dh1_ref[...] + r * (a - xh * jnp.mean(a * xh, axis=-1, keepdims=True))
        dg_ref[...] += jnp.sum(du * xh, axis=0, keepdims=True)

    tok = lambda w: pl.BlockSpec((TOK_T, w), lambda i: (i, 0))
    full = lambda a, b: pl.BlockSpec((a, b), lambda i: (0, 0))
    return pl.pallas_call(
        body, name="inproj_bwd", grid=(S // TOK_T,),
        in_specs=[tok(N_MAIN), tok(128), full(D_MODEL, N_MAIN), full(D_MODEL, 128), tok(D_MODEL), tok(D_MODEL),
                  full(1, D_MODEL)],
        out_specs=[tok(D_MODEL), full(1, D_MODEL)],
        out_shape=[jax.ShapeDtypeStruct((S, D_MODEL), F32), jax.ShapeDtypeStruct((1, D_MODEL), F32)],
        compiler_params=_cp("arbitrary"),
    )(dzm, dzf, wm, wf, h0, dh1, g)


def _wgrad(a, b, name):
    S, M = a.shape
    N = b.shape[1]
    tn = min(N, 2048)
    ts = 512
    last = S // ts - 1

    def body(a_ref, b_ref, o_ref, acc_ref):
        @pl.when(pl.program_id(1) == 0)
        def _():
            acc_ref[...] = jnp.zeros_like(acc_ref)

        acc_ref[...] += _dot_tn(a_ref[...].astype(BF16), b_ref[...])

        @pl.when(pl.program_id(1) == last)
        def _():
            o_ref[...] = acc_ref[...].astype(BF16)

    return pl.pallas_call(
        body, name=name, grid=(N // tn, S // ts),
        in_specs=[pl.BlockSpec((ts, M), lambda n, s: (s, 0)), pl.BlockSpec((ts, tn), lambda n, s: (s, n))],
        out_specs=pl.BlockSpec((M, tn), lambda n, s: (0, n)),
        out_shape=jax.ShapeDtypeStruct((M, N), BF16),
        scratch_shapes=[pltpu.VMEM((M, tn), F32)],
        compiler_params=_cp("parallel", "arbitrary"),
    )(a, b)


def _rope_tables(positions):
    inv_freq = ROPE_THETA ** (-jnp.arange(ROPE_HALF, dtype=F32) / ROPE_HALF)
    ang = positions.astype(F32)[:, None] * inv_freq
    cos, sin = jnp.cos(ang), jnp.sin(ang)
    S = positions.shape[0]
    one, zero = jnp.ones((S, HEAD_DIM - 2 * ROPE_HALF), F32), jnp.zeros((S, HEAD_DIM - 2 * ROPE_HALF), F32)
    z8 = jnp.zeros((S, ROPE_HALF), F32)
    rc = jnp.concatenate([cos, cos, one], axis=1)
    ra = jnp.concatenate([-sin, z8, zero], axis=1)
    rb = jnp.concatenate([z8, sin, zero], axis=1)
    return tuple(jnp.tile(t, (1, 2)) for t in (rc, ra, rb))


def _in_operands(w_in):
    w_in = w_in.astype(BF16)
    wm = jnp.concatenate([w_in[:, :4 * D_BRANCH], w_in[:, 4 * D_BRANCH + N_HEADS:]], axis=1)
    wf = jnp.pad(w_in[:, 4 * D_BRANCH:4 * D_BRANCH + N_HEADS], ((0, 0), (0, 128 - N_HEADS)))
    return dict(wm=wm, wf=wf)


def _layer_weights(w_in, w_out, w_ple, w_pg):
    return dict(_in_operands(w_in), w_out=w_out.astype(BF16), w_ple=w_ple.astype(BF16), w_pg=w_pg.astype(BF16))


def _row(v, width=128):
    v = v.reshape(1, -1).astype(F32)
    return jnp.pad(v, ((0, 0), (0, width - v.shape[1])))


def _layer_fwd(h0, p, rope, tabs, w, norm_g, b_f, qk_g, ple_g, rider=None, late=None):
    g1 = norm_g.reshape(1, D_MODEL)
    g2 = ple_g.reshape(1, D_MODEL)
    qkg = jnp.tile(qk_g, (1, 2))
    bf = _row(b_f)
    zm, zf, u = _inproj_fwd(h0, g1, w["wm"], w["wf"])
    c = _forget_cumsum(zf, bf, tabs["tril"])
    qa, ka, va, qb, kb, vb = _prep_fwd(zm, c, qkg, *rope)
    oa, lse_a, *arrivals = _attn_fwd(qa, ka, va, tabs["fox"], True, "fox_fwd", rider)
    if late is not None:
        w = {**w, **late(arrivals)}
    ob, lse_b = _attn_fwd(qb, kb, vb, tabs["dil"], False, "dil_fwd")
    y, h1, h2, u2, e, gate = _mid_fwd(oa, ob, zm, h0, p, w["w_out"], w["w_pg"], w["w_ple"], g2)
    saved = dict(h0=h0, p=p, zm=zm, zf=zf, u=u, qa=qa, ka=ka, va=va, qb=qb, kb=kb, vb=vb, oa=oa, ob=ob,
                 lse_a=lse_a, lse_b=lse_b, y=y, h1=h1, u2=u2, e=e, gate=gate, g1=g1, g2=g2, qkg=qkg, bf=bf, w=w)
    return h2, saved, arrivals


def _layer_bwd(dh2, sv, rope, tabs, make_rider=None):
    S = dh2.shape[0]
    nq = S // ATT_T
    w = sv["w"]
    rows = lambda a: a.reshape(N_HEADS, nq, 1, ATT_T)
    (dh1, dh1b, de, dpre, doa, dob, dga, dgb, dd, dg2) = _mid_bwd(
        dh2, sv["h1"], sv["e"], sv["gate"], sv["g2"], w["w_pg"], w["w_out"], sv["oa"], sv["ob"], sv["zm"])
    dda, ddb = dd[:N_HEADS], dd[N_HEADS:]
    early = dict(w_out=_wgrad(sv["y"], dh1b, "wgrad_out"), w_ple=_wgrad(sv["p"], de, "wgrad_ple"),
                 w_ple_gate=_wgrad(sv["u2"], dpre, "wgrad_gate"))
    rider = None if make_rider is None else make_rider(early)
    dqa, dka, dva, *arrivals = _attn_bwd(sv["qa"], sv["ka"], sv["va"], doa, sv["lse_a"], rows(dda), tabs["fox"],
                                         True, "fox_bwd", rider)
    dqb, dkb, dvb = _attn_bwd(sv["qb"], sv["kb"], sv["vb"], dob, sv["lse_b"], rows(ddb), tabs["dil"], False,
                              "dil_bwd")
    dzm, dc, dqkg = _prep_bwd(dqa, dka, dva, dqb, dkb, dvb, sv["zm"], sv["qkg"], *rope, dga, dgb)
    dzf, dbf = _forget_bwd(dc, sv["zf"], sv["bf"], tabs["triu"])
    dh0, dg1 = _inproj_bwd(dzm, dzf, w["wm"], w["wf"], sv["h0"], dh1, sv["g1"])
    dwm = _wgrad(sv["u"], dzm, "wgrad_in")
    dwf = _wgrad(sv["u"], dzf, "wgrad_f")
    dw_in = jnp.concatenate([dwm[:, :4 * D_BRANCH], dwf[:, :N_HEADS], dwm[:, 4 * D_BRANCH:]], axis=1)
    grads = dict(norm_g=dg1.reshape(D_MODEL), w_in=dw_in, b_f=dbf[0, :N_HEADS],
                 qk_norm_g=dqkg[:, :HEAD_DIM] + dqkg[:, HEAD_DIM:], ple_norm_g=dg2.reshape(D_MODEL), **early)
    return dh0, grads, arrivals


def _tables():
    T = CUM_T
    r = lax.broadcasted_iota(jnp.int32, (T, T), 0)
    c = lax.broadcasted_iota(jnp.int32, (T, T), 1)
    return dict(fox=_bias_tables(True), dil=_bias_tables(False),
                tril=(c <= r).astype(BF16), triu=(c >= r).astype(BF16))


def _local_step(x, p, positions, target, layers, small):
    rope = _rope_tables(positions)
    tabs = _tables()
    ws = [_layer_weights(*lw) for lw in layers]
    h = x
    saved = []
    for w, lp, sm in zip(ws, p, small):
        h, sv, _ = _layer_fwd(h, lp, rope, tabs, w, *sm)
        saved.append(sv)
    dh, loss = _loss_fwd_bwd(h, target)
    grads = [None] * len(ws)
    for li in reversed(range(len(ws))):
        dh, grads[li], _ = _layer_bwd(dh, saved[li], rope, tabs)
    return loss[0, 0], dh, grads


def _peers():
    x, y, c = lax.axis_index("x"), lax.axis_index("y"), lax.axis_index("c")
    me = 4 * x + 2 * y + c
    flip = lambda v, bit: 1 - v if bit else v
    return me, [(flip(x, k & 4), flip(y, k & 2), flip(c, k & 1)) for k in range(1, N_DEV)]


def _sel(ref, kind, d):
    if kind == "whole":
        return ref
    if kind == "slot":
        return ref.at[d]
    block = pl.ds(pl.multiple_of(d * 128, 128), 128)
    return ref.at[block, :] if kind == "rows" else ref.at[:, block]


class _Pushes:
    def __init__(self, arrays, src_kinds, dst_kinds, out_shapes):
        self.arrays, self.n = list(arrays), len(arrays)
        self.src_kinds, self.dst_kinds = src_kinds, dst_kinds
        self.out_shapes = [jax.ShapeDtypeStruct(s, a.dtype) for s, a in zip(out_shapes, arrays)]
        hbm = pl.BlockSpec(memory_space=pltpu.HBM)
        self.in_specs, self.out_specs = [hbm] * self.n, [hbm] * self.n
        self.scratch_shapes = [pltpu.SemaphoreType.DMA((N_DEV - 1, self.n)),
                               pltpu.SemaphoreType.DMA((N_DEV - 1, self.n)), pltpu.SemaphoreType.DMA((self.n,))]

    def _copies(self, ins, outs, sems):
        send_sems, recv_sems, local_sems = sems
        me, peers = _peers()
        src = lambda a, d: _sel(ins[a], self.src_kinds[a], d)
        dst = lambda a: _sel(outs[a], self.dst_kinds[a], me)
        local = [pltpu.make_async_copy(src(a, me), dst(a), local_sems.at[a]) for a in range(self.n)]
        remote = [pltpu.make_async_remote_copy(
            src_ref=src(a, 4 * px + 2 * py + pc), dst_ref=dst(a), send_sem=send_sems.at[k, a],
            recv_sem=recv_sems.at[k, a], device_id=(px, py, pc), device_id_type=pl.DeviceIdType.MESH)
            for k, (px, py, pc) in enumerate(peers) for a in range(self.n)]
        return local + remote

    def start(self, ins, outs, sems):
        for cp in self._copies(ins, outs, sems):
            cp.start()

    def wait(self, ins, outs, sems):
        for cp in self._copies(ins, outs, sems):
            cp.wait()


def _exchange(name, pushes):
    n = pushes.n

    def body(*refs):
        pushes.start(refs[:n], refs[n:2 * n], refs[2 * n:])
        pushes.wait(refs[:n], refs[n:2 * n], refs[2 * n:])

    return pl.pallas_call(body, name=name, in_specs=pushes.in_specs, out_specs=pushes.out_specs,
                          out_shape=pushes.out_shapes, scratch_shapes=pushes.scratch_shapes)(*pushes.arrays)


def _gather_pushes(shards, kinds):
    full = {"slot": lambda s: (N_DEV,) + s, "rows": lambda s: (N_DEV * s[0], s[1]),
            "cols": lambda s: (s[0], N_DEV * s[1])}
    return _Pushes(shards, ["whole"] * len(shards), kinds, [full[k](a.shape) for a, k in zip(shards, kinds)])


def _scatter_pushes(partials, kinds):
    part = {"slot": lambda s: s[1:], "rows": lambda s: (128, s[1]), "cols": lambda s: (s[0], 128),
            "whole": lambda s: s}
    return _Pushes(partials, kinds, ["slot"] * len(partials),
                   [(N_DEV,) + part[k](a.shape) for a, k in zip(partials, kinds)])


def _adamw(name, parts, w, m, v, rows):
    L, R, C = w.shape

    def body(p_ref, w_ref, m_ref, v_ref, g_ref, d_ref, nm_ref, nv_ref):
        g = p_ref[0, 0].astype(F32)
        for s in range(1, N_DEV):
            g = g + p_ref[s, 0].astype(F32)
        g_ref[0] = g
        nm = ADAM_B1 * m_ref[0] + (1.0 - ADAM_B1) * g
        nv = ADAM_B2 * v_ref[0] + (1.0 - ADAM_B2) * (g * g)
        nm_ref[0] = nm
        nv_ref[0] = nv
        m_hat = nm / (1.0 - ADAM_B1 ** ADAM_STEP)
        v_hat = nv / (1.0 - ADAM_B2 ** ADAM_STEP)
        d_ref[0] = -ADAM_LR * (m_hat / (jnp.sqrt(v_hat) + ADAM_EPS) + ADAM_WD * w_ref[0])

    blk = pl.BlockSpec((1, rows, C), lambda l, i: (l, i, 0))
    shp = jax.ShapeDtypeStruct((L, R, C), F32)
    return pl.pallas_call(
        body, name=name, grid=(L, R // rows),
        in_specs=[pl.BlockSpec((N_DEV, 1, rows, C), lambda l, i: (0, l, i, 0)), blk, blk, blk],
        out_specs=[blk] * 4, out_shape=[shp] * 4,
        compiler_params=_cp("parallel", "parallel"),
    )(parts, w, m, v)


SMALL_ROWS = 40
LOSS_ROW = 37


def _pack_small(norm_g, ple_g, qk_g, b_f, last_row):
    rows = lambda a: a.astype(F32).reshape(-1, 128)
    flat = jnp.concatenate([rows(norm_g), rows(ple_g), rows(qk_g), _row(b_f.reshape(-1)), last_row], axis=0)
    return jnp.pad(flat, ((0, SMALL_ROWS - flat.shape[0]), (0, 0)))


def _unpack_small(flat):
    return (flat[0:16].reshape(2, D_MODEL), flat[16:32].reshape(2, D_MODEL), flat[32:36].reshape(2, 4, HEAD_DIM),
            flat[36, :2 * N_HEADS].reshape(2, N_HEADS))


def kernel(x, p, positions, norm_g, w_in, b_f, qk_norm_g, w_out, w_ple, ple_norm_g, w_ple_gate, loss_target, m_norm_g, m_w_in, m_b_f, m_qk_norm_g, m_w_out, m_w_ple, m_ple_norm_g, m_w_ple_gate, v_norm_g, v_w_in, v_b_f, v_qk_norm_g, v_w_out, v_w_ple, v_ple_norm_g, v_w_ple_gate):
    bf16 = lambda a: a.astype(BF16)
    rows_in = W_IN_ROWS // 2
    flat_in = lambda a: bf16(a).reshape(rows_in, 128)
    full_in = lambda g: g.reshape(N_DEV, D_MODEL, W_IN_SHARD).transpose(1, 0, 2).reshape(D_MODEL, N_IN)
    small = [(norm_g[l], b_f[l], qk_norm_g[l], ple_norm_g[l]) for l in range(2)]
    rope = _rope_tables(positions[0])
    tabs = _tables()

    (g_in0,) = _exchange("gather_first", _gather_pushes([flat_in(w_in[0])], ["slot"]))
    rest = _gather_pushes([flat_in(w_in[1])] + [bf16(a[l]) for l in range(2) for a in (w_out, w_ple, w_ple_gate)],
                          ["slot"] + ["rows", "cols", "rows"] * 2)
    late = lambda got: dict(w_out=got[1], w_ple=got[2], w_pg=got[3])
    h, sv0, got = _layer_fwd(x[0], p[0, 0], rope, tabs, _in_operands(full_in(g_in0)), *small[0], rest, late)
    w1 = dict(_in_operands(full_in(got[0])), w_out=got[4], w_ple=got[5], w_pg=got[6])
    h, sv1, _ = _layer_fwd(h, p[1, 0], rope, tabs, w1, *small[1])
    dh, loss = _loss_fwd_bwd(h, loss_target[0])
    dh, gr1, _ = _layer_bwd(dh, sv1, rope, tabs)

    by_dest = lambda d: d.reshape(D_MODEL, N_DEV, W_IN_SHARD).transpose(1, 0, 2).reshape(N_DEV, rows_in, 128)
    big = ("w_out", "w_ple", "w_ple_gate")
    riding = lambda early: _scatter_pushes([by_dest(gr1["w_in"])] + [gr1[n] for n in big] + [early[n] for n in big],
                                           ["slot"] + ["rows", "cols", "rows"] * 2)
    dx, gr0, (r_in1, *r_big) = _layer_bwd(dh, sv0, rope, tabs, riding)
    grads = (gr0, gr1)
    stack = lambda name: jnp.stack([gl[name] for gl in grads], axis=0)
    small_part = _pack_small(stack("norm_g"), stack("ple_norm_g"), stack("qk_norm_g"), stack("b_f"),
                             _row(loss[0, 0].reshape(1)))
    r_in0, r_small = _exchange("exchange_last", _scatter_pushes([by_dest(gr0["w_in"]), small_part], ["slot", "whole"]))
    r_in = jnp.concatenate([r_in0, r_in1], axis=1)
    r_out, r_ple, r_pg = (jnp.stack([r_big[3 + k], r_big[k]], axis=1) for k in range(3))

    zero_row = jnp.zeros((1, 128), F32)
    small_of = lambda ng, pg, qk, bf: _pack_small(ng, pg, qk, bf, zero_row)[None]
    flat = lambda a: a.reshape(1, W_IN_ROWS, 128)
    outs = dict(
        w_in=[o.reshape(w_in.shape) for o in
              _adamw("adamw_in", r_in[:, None], flat(w_in), flat(m_w_in), flat(v_w_in), W_IN_TILE)],
        w_out=_adamw("adamw_out", r_out, w_out, m_w_out, v_w_out, 128),
        w_ple=_adamw("adamw_ple", r_ple, w_ple, m_w_ple, v_w_ple, 256),
        w_pg=_adamw("adamw_gate", r_pg, w_ple_gate, m_w_ple_gate, v_w_ple_gate, 128),
        small=_adamw("adamw_small", r_small[:, None], small_of(norm_g, ple_norm_g, qk_norm_g, b_f),
                     small_of(m_norm_g, m_ple_norm_g, m_qk_norm_g, m_b_f),
                     small_of(v_norm_g, v_ple_norm_g, v_qk_norm_g, v_b_f), SMALL_ROWS))
    leaves = []
    for kind in range(4):
        ng, pg, qk, bf = _unpack_small(outs["small"][kind][0])
        leaves += [ng, outs["w_in"][kind], bf, qk, outs["w_out"][kind], outs["w_ple"][kind], pg, outs["w_pg"][kind]]
    return (outs["small"][0][0, LOSS_ROW, 0], dx[None], *leaves)
```

```python
import functools

import jax
import jax.numpy as jnp
from jax import lax
from jax.experimental import pallas as pl
from jax.experimental.pallas import tpu as pltpu

F32 = jnp.float32
BF16 = jnp.bfloat16

D_MODEL = 1024
HEAD_DIM = 64
N_HEADS = 8
HEAD_PAD = 128
D_BRANCH = N_HEADS * HEAD_DIM
N_MAIN = 8 * D_BRANCH
N_IN = N_MAIN + N_HEADS
PLE_DIM = 256
ROPE_THETA = 500000.0
ROPE_HALF = 8
EPS = 1e-6
NEG = -1e30
SCALE = HEAD_DIM ** -0.5
LOG2E = 1.4426950408889634
LN2 = 0.6931471805599453
DILATED_PATTERNS = ((128, 1), (512, 4), (2048, 16))
N_DEV = 8
W_IN_SHARD = N_IN // N_DEV
W_IN_ROWS = 2 * D_MODEL * W_IN_SHARD // 128
W_IN_TILE = W_IN_ROWS // 19

ADAM_LR = 0.001
ADAM_B1 = 0.9
ADAM_B2 = 0.999
ADAM_EPS = 1e-08
ADAM_WD = 0.01
ADAM_STEP = 10

ATT_T = 512
ATT_FWD_HEADS = 4
ATT_BWD_HEADS = 2
ATT_CHUNK = 32
TOK_T = 256
CUM_T = 512
VMEM_LIMIT = 56 * 1024 * 1024


def _slab_spec(lead, block, index):
    return pl.BlockSpec((None,) * len(lead) + block, lambda *g: (*lead, *index(*g)))


def _cp(*sem):
    return pltpu.CompilerParams(dimension_semantics=sem, vmem_limit_bytes=VMEM_LIMIT)


def _sigmoid(x):
    return 1.0 / (1.0 + jnp.exp(-x))


def _split3(x):
    hi = x.astype(BF16)
    r1 = x - hi.astype(F32)
    mid = r1.astype(BF16)
    lo = (r1 - mid.astype(F32)).astype(BF16)
    return hi, mid, lo


def _dot(a, b):
    return jnp.dot(a, b, preferred_element_type=F32)


def _dot_nt(a, b):
    return lax.dot_general(a, b, (((1,), (1,)), ((), ())), preferred_element_type=F32)


def _dot_tn(a, b):
    return lax.dot_general(a, b, (((0,), (0,)), ((), ())), preferred_element_type=F32)


def _inproj_fwd(h, g, wm, wf):
    S = h.shape[0]

    def body(h_ref, g_ref, wm_ref, wf_ref, zm_ref, zf_ref, u_ref):
        x = h_ref[...]
        r = lax.rsqrt(jnp.mean(x * x, axis=-1, keepdims=True) + EPS)
        u = (x * r * g_ref[...]).astype(BF16)
        u_ref[...] = u
        zm_ref[...] = _dot(u, wm_ref[...])
        zf_ref[...] = _dot(u, wf_ref[...])

    return pl.pallas_call(
        body, name="inproj_fwd", grid=(S // TOK_T,),
        in_specs=[pl.BlockSpec((TOK_T, D_MODEL), lambda i: (i, 0)),
                  pl.BlockSpec((1, D_MODEL), lambda i: (0, 0)),
                  pl.BlockSpec((D_MODEL, N_MAIN), lambda i: (0, 0)),
                  pl.BlockSpec((D_MODEL, 128), lambda i: (0, 0))],
        out_specs=[pl.BlockSpec((TOK_T, N_MAIN), lambda i: (i, 0)),
                   pl.BlockSpec((TOK_T, 128), lambda i: (i, 0)),
                   pl.BlockSpec((TOK_T, D_MODEL), lambda i: (i, 0))],
        out_shape=[jax.ShapeDtypeStruct((S, N_MAIN), F32), jax.ShapeDtypeStruct((S, 128), F32),
                   jax.ShapeDtypeStruct((S, D_MODEL), BF16)],
        compiler_params=_cp("parallel"),
    )(h, g, wm, wf)


def _log_sigmoid(x):
    return jnp.minimum(x, 0.0) - jnp.log(1.0 + jnp.exp(-jnp.abs(x)))


def _forget_cumsum(zf, bf, tri):
    S = zf.shape[0]

    def body(zf_ref, b_ref, tri_ref, c_ref, carry):
        @pl.when(pl.program_id(0) == 0)
        def _():
            carry[...] = jnp.zeros_like(carry)

        lf = _log_sigmoid(zf_ref[...] + b_ref[...])
        hi, mid, lo = _split3(lf)
        t = tri_ref[...]
        cs = _dot(t, hi) + _dot(t, mid) + _dot(t, lo) + carry[...]
        c_ref[...] = cs
        carry[...] = cs[CUM_T - 1:CUM_T, :]

    return pl.pallas_call(
        body, name="forget_cumsum", grid=(S // CUM_T,),
        in_specs=[pl.BlockSpec((CUM_T, 128), lambda i: (i, 0)),
                  pl.BlockSpec((1, 128), lambda i: (0, 0)),
                  pl.BlockSpec((CUM_T, CUM_T), lambda i: (0, 0))],
        out_specs=pl.BlockSpec((CUM_T, 128), lambda i: (i, 0)),
        out_shape=jax.ShapeDtypeStruct((S, 128), F32),
        scratch_shapes=[pltpu.VMEM((1, 128), F32)],
        compiler_params=_cp("arbitrary"),
    )(zf, bf, tri)


def _forget_bwd(dc, zf, bf, triu):
    S = zf.shape[0]
    n = S // CUM_T

    def body(dc_ref, zf_ref, b_ref, tri_ref, dzf_ref, db_ref, carry):
        @pl.when(pl.program_id(0) == 0)
        def _():
            carry[...] = jnp.zeros_like(carry)
            db_ref[...] = jnp.zeros_like(db_ref)

        hi, mid, lo = _split3(dc_ref[...])
        t = tri_ref[...]
        dlf = _dot(t, hi) + _dot(t, mid) + _dot(t, lo) + carry[...]
        carry[...] = dlf[0:1, :]
        dfa = dlf * (1.0 - _sigmoid(zf_ref[...] + b_ref[...]))
        dzf_ref[...] = dfa.astype(BF16)
        db_ref[...] += jnp.sum(dfa, axis=0, keepdims=True)

    return pl.pallas_call(
        body, name="forget_bwd", grid=(n,),
        in_specs=[pl.BlockSpec((CUM_T, 128), lambda i: (n - 1 - i, 0)),
                  pl.BlockSpec((CUM_T, 128), lambda i: (n - 1 - i, 0)),
                  pl.BlockSpec((1, 128), lambda i: (0, 0)),
                  pl.BlockSpec((CUM_T, CUM_T), lambda i: (0, 0))],
        out_specs=[pl.BlockSpec((CUM_T, 128), lambda i: (n - 1 - i, 0)),
                   pl.BlockSpec((1, 128), lambda i: (0, 0))],
        out_shape=[jax.ShapeDtypeStruct((S, 128), BF16), jax.ShapeDtypeStruct((1, 128), F32)],
        scratch_shapes=[pltpu.VMEM((1, 128), F32)],
        compiler_params=_cp("arbitrary"),
    )(dc, zf, bf, triu)


def _same_head():
    r = lax.broadcasted_iota(jnp.int32, (HEAD_PAD, HEAD_PAD), 0) // HEAD_DIM
    c = lax.broadcasted_iota(jnp.int32, (HEAD_PAD, HEAD_PAD), 1) // HEAD_DIM
    return (r == c).astype(BF16)


def _pair_mean(x, same_head):
    hi = x.astype(BF16)
    lo = (x - hi.astype(F32)).astype(BF16)
    return (_dot(hi, same_head) + _dot(lo, same_head)) * (1.0 / HEAD_DIM)


def _pair_rsqrt(x, same_head):
    return lax.rsqrt(_pair_mean(x * x, same_head) + EPS)


def _prep_fwd(zm, c, qkg, rope_c, rope_a, rope_b):
    S = zm.shape[0]
    shp = jax.ShapeDtypeStruct((N_HEADS, S, HEAD_PAD), BF16)

    def body(z_ref, c_ref, g_ref, rc_ref, ra_ref, rb_ref, qa_ref, ka_ref, va_ref, qb_ref, kb_ref, vb_ref):
        lane = lax.broadcasted_iota(jnp.int32, (TOK_T, HEAD_PAD), 1)
        lo_half = lane < HEAD_DIM
        aug = (lane >= HEAD_DIM) & (lane < HEAD_DIM + 3)
        q_pad = jnp.where(aug, -1.0, 0.0)
        cs = c_ref[...]
        rc, ra, rb = rc_ref[...], ra_ref[...], rb_ref[...]
        same_head = _same_head()

        def norm(col, gi):
            x = z_ref[:, col:col + HEAD_PAD]
            return x * _pair_rsqrt(x, same_head) * g_ref[gi:gi + 1, :]

        def rope(y):
            return y * rc + pltpu.roll(y, HEAD_PAD - ROPE_HALF, 1) * ra + pltpu.roll(y, ROPE_HALF, 1) * rb

        def put(ref, pi, y, pad_even, pad_odd):
            ref[2 * pi] = jnp.where(lo_half, y, pad_even).astype(BF16)
            ref[2 * pi + 1] = jnp.where(lo_half, pltpu.roll(y, HEAD_DIM, 1), pad_odd).astype(BF16)

        def k_pad(h):
            ch = cs[:, h:h + 1] * LOG2E
            hi = ch.astype(BF16).astype(F32)
            mid = (ch - hi).astype(BF16).astype(F32)
            lo = ch - hi - mid
            ones = jnp.where(lane == HEAD_DIM + 3, 1.0, 0.0)
            return jnp.where(lane == HEAD_DIM, hi, jnp.where(lane == HEAD_DIM + 1, mid,
                                                              jnp.where(lane == HEAD_DIM + 2, lo, ones)))

        for pi in range(N_HEADS // 2):
            col = HEAD_PAD * pi
            put(qa_ref, pi, norm(col, 0) * (SCALE * LOG2E), q_pad, q_pad)
            put(ka_ref, pi, norm(D_BRANCH + col, 1), k_pad(2 * pi), k_pad(2 * pi + 1))
            put(va_ref, pi, z_ref[:, 2 * D_BRANCH + col:2 * D_BRANCH + col + HEAD_PAD], 0.0, 0.0)
            put(qb_ref, pi, rope(norm(4 * D_BRANCH + col, 2)) * (SCALE * LOG2E), 0.0, 0.0)
            put(kb_ref, pi, rope(norm(5 * D_BRANCH + col, 3)), 0.0, 0.0)
            put(vb_ref, pi, z_ref[:, 6 * D_BRANCH + col:6 * D_BRANCH + col + HEAD_PAD], 0.0, 0.0)

    tok = lambda w: pl.BlockSpec((TOK_T, w), lambda i: (i, 0))
    head = pl.BlockSpec((N_HEADS, TOK_T, HEAD_PAD), lambda i: (0, i, 0))
    return pl.pallas_call(
        body, name="prep_fwd", grid=(S // TOK_T,),
        in_specs=[tok(N_MAIN), tok(128), pl.BlockSpec((4, 128), lambda i: (0, 0)), tok(128), tok(128), tok(128)],
        out_specs=[head] * 6, out_shape=[shp] * 6,
        compiler_params=_cp("parallel"),
    )(zm, c, qkg, rope_c, rope_a, rope_b)


def _pair(ref, pi, lo_half):
    return jnp.where(lo_half, ref[2 * pi], pltpu.roll(ref[2 * pi + 1], HEAD_DIM, 1))


def _mid_fwd(oa, ob, zm, h0, p, w_out, w_pg, w_ple, g2):
    S = h0.shape[0]
    p, p_lead = p

    def body(oa_ref, ob_ref, ga_ref, gb_ref, h0_ref, p_ref, wo_ref, wg_ref, wp_ref, g2_ref,
             y_ref, h1_ref, h2_ref, u2_ref, e_ref, gate_ref):
        lane = lax.broadcasted_iota(jnp.int32, (TOK_T, HEAD_PAD), 1)
        lo_half = lane < HEAD_DIM
        parts = []
        for o_ref, g_ref in ((oa_ref, ga_ref), (ob_ref, gb_ref)):
            for pi in range(N_HEADS // 2):
                g = g_ref[:, HEAD_PAD * pi:HEAD_PAD * (pi + 1)]
                parts.append((_pair(o_ref, pi, lo_half) * (g * _sigmoid(g))).astype(BF16))
        y = jnp.concatenate(parts, axis=1)
        y_ref[...] = y
        h1 = h0_ref[...] + _dot(y, wo_ref[...])
        h1_ref[...] = h1
        r = lax.rsqrt(jnp.mean(h1 * h1, axis=-1, keepdims=True) + EPS)
        u2 = (h1 * r * g2_ref[...]).astype(BF16)
        u2_ref[...] = u2
        gate = _sigmoid(_dot(u2, wg_ref[...]))
        e = _dot(p_ref[...].astype(BF16), wp_ref[...])
        e_ref[...] = e.astype(BF16)
        gate_ref[...] = gate.astype(BF16)
        h2_ref[...] = h1 + e * gate

    tok = lambda w: pl.BlockSpec((TOK_T, w), lambda i: (i, 0))
    head = pl.BlockSpec((N_HEADS, TOK_T, HEAD_PAD), lambda i: (0, i, 0))
    full = lambda a, b: pl.BlockSpec((a, b), lambda i: (0, 0))
    act = lambda dt: jax.ShapeDtypeStruct((S, D_MODEL), dt)
    return pl.pallas_call(
        body, name="mid_fwd", grid=(S // TOK_T,),
        in_specs=[head, head,
                  pl.BlockSpec((TOK_T, D_BRANCH), lambda i: (i, 3)), pl.BlockSpec((TOK_T, D_BRANCH), lambda i: (i, 7)),
                  tok(D_MODEL), _slab_spec(p_lead, (TOK_T, PLE_DIM), lambda i: (i, 0)), full(D_MODEL, D_MODEL),
                  full(D_MODEL, D_MODEL), full(PLE_DIM, D_MODEL), full(1, D_MODEL)],
        out_specs=[tok(D_MODEL)] * 6,
        out_shape=[act(BF16), act(F32), act(F32), act(BF16), act(BF16), act(BF16)],
        compiler_params=_cp("parallel"),
    )(oa, ob, zm, zm, h0, p, w_out, w_pg, w_ple, g2)


def _loss_fwd_bwd(y, t):
    S = y.shape[0]

    def body(y_ref, t_ref, dy_ref, loss_ref):
        @pl.when(pl.program_id(0) == 0)
        def _():
            loss_ref[...] = jnp.zeros_like(loss_ref)

        err = y_ref[...] - t_ref[...]
        dy_ref[...] = err * (1.0 / D_MODEL)
        part = jnp.sum(jnp.sum(err * err, axis=1, keepdims=True), axis=0, keepdims=True)
        loss_ref[...] += part * (0.5 / D_MODEL)

    tok = pl.BlockSpec((TOK_T, D_MODEL), lambda i: (i, 0))
    return pl.pallas_call(
        body, name="loss", grid=(S // TOK_T,),
        in_specs=[tok, tok], out_specs=[tok, pl.BlockSpec((8, 128), lambda i: (0, 0))],
        out_shape=[jax.ShapeDtypeStruct((S, D_MODEL), F32), jax.ShapeDtypeStruct((8, 128), F32)],
        compiler_params=_cp("arbitrary"),
    )(y, t)


def _bias_tables(full_range):
    T = ATT_T
    nb = 1 if full_range else DILATED_PATTERNS[-1][0] // T + 1
    r = lax.broadcasted_iota(jnp.int32, (nb, T, T), 2)
    c = lax.broadcasted_iota(jnp.int32, (nb, T, T), 1)
    b = lax.broadcasted_iota(jnp.int32, (nb, T, T), 0)
    delta = T * b + r - c
    if full_range:
        bias = jnp.where(delta >= 0, 0.0, NEG).astype(F32)
    else:
        mult = jnp.zeros((nb, T, T), F32)
        for window, dil in DILATED_PATTERNS:
            ok = (delta >= 0) & (delta <= window) & (delta % dil == 0)
            mult = mult + ok.astype(F32)
        bias = jnp.where(mult > 0, jnp.log2(jnp.maximum(mult, 1.0)), NEG).astype(F32)
    return bias


def _call_with_rider(body, name, grid, rider, in_specs, out_specs, out_shape, scratch_shapes, operands):
    if rider is None:
        return pl.pallas_call(body, name=name, grid=grid, in_specs=in_specs, out_specs=out_specs,
                              out_shape=out_shape, scratch_shapes=scratch_shapes,
                              compiler_params=_cp("parallel", "arbitrary"))(*operands)
    n, n_in, n_out = rider.n, len(in_specs), len(out_specs)

    def wrapped(*refs):
        ins, r_ins = refs[:n_in], refs[n_in:n_in + n]
        outs, r_outs = refs[n_in + n:n_in + n + n_out], refs[n_in + n + n_out:n_in + 2 * n + n_out]
        scratch, sems = refs[n_in + 2 * n + n_out:-3], refs[-3:]
        step = [pl.program_id(a) for a in range(len(grid))]

        @pl.when(functools.reduce(jnp.logical_and, [s == 0 for s in step]))
        def _():
            rider.start(r_ins, r_outs, sems)

        body(*ins, *outs, *scratch)

        @pl.when(functools.reduce(jnp.logical_and, [s == g - 1 for s, g in zip(step, grid)]))
        def _():
            rider.wait(r_ins, r_outs, sems)

    return pl.pallas_call(
        wrapped, name=name, grid=grid, in_specs=list(in_specs) + rider.in_specs,
        out_specs=list(out_specs) + rider.out_specs, out_shape=list(out_shape) + rider.out_shapes,
        scratch_shapes=list(scratch_shapes) + rider.scratch_shapes,
        compiler_params=_cp("arbitrary", "arbitrary"))(*operands, *rider.arrays)


def _attn_fwd(q, k, v, table_t, full_range, name, rider=None):
    H, S, _ = q.shape
    T = ATT_T
    nb = table_t.shape[0]
    HB = ATT_FWD_HEADS
    KC = ATT_CHUNK
    chunks = [slice(c, c + KC) for c in range(0, T, KC)]
    fold = lambda x, op: functools.reduce(op, [x[r:r + 8] for r in range(0, KC, 8)])

    def body(q_ref, k_ref, v_ref, tab_ref, o_ref, lse_ref, *scratch):
        st_refs, pt_refs, acc_refs = scratch[:HB], scratch[HB:2 * HB], scratch[2 * HB:]
        i = pl.program_id(1)
        rows = lambda j: pl.ds(pl.multiple_of(j * T, T), T)

        def scores(hh, j):
            st_refs[hh][...] = _dot_nt(k_ref[hh, rows(j), :], q_ref[hh])

        def block(j, b, nxt, stats):
            out = []
            for hh, (m, l) in enumerate(stats):
                st_ref, pt_ref, acc_ref = st_refs[hh], pt_refs[hh], acc_refs[hh]
                mx = None
                for ch in chunks:
                    x = st_ref[ch, :]
                    if b is not None:
                        x = x + tab_ref[b, ch, :]
                        st_ref[ch, :] = x
                    x = fold(x, jnp.maximum)
                    mx = x if mx is None else jnp.maximum(mx, x)
                m_new = jnp.maximum(m, jnp.max(mx, axis=0, keepdims=True))
                alpha = jnp.exp2(m - m_new)
                ls = None
                for ch in chunks:
                    pc = jnp.exp2(st_ref[ch, :] - m_new)
                    pt_ref[ch, :] = pc.astype(BF16)
                    pc = fold(pc, jnp.add)
                    ls = pc if ls is None else ls + pc
                if nxt is not None:
                    scores(hh, nxt)
                acc_ref[...] = alpha * acc_ref[...] + _dot_tn(v_ref[hh, rows(j), :], pt_ref[...])
                out.append((m_new, alpha * l + jnp.sum(ls, axis=0, keepdims=True)))
            return tuple(out)

        lo = 0 if full_range else jnp.maximum(i - (nb - 1), 0)
        for hh in range(HB):
            acc_refs[hh][...] = jnp.zeros_like(acc_refs[hh])
            scores(hh, lo)
        stats = lax.fori_loop(lo, i, lambda j, st: block(j, None if full_range else i - j, j + 1, st),
                              ((jnp.full((1, T), NEG, F32), jnp.zeros((1, T), F32)),) * HB)
        stats = block(i, 0, None, stats)
        for hh, (m, l) in enumerate(stats):
            o_ref[hh] = (acc_refs[hh][...] * (1.0 / l)).T
            lse_ref[hh, 0] = m + jnp.log2(l)

    return _call_with_rider(
        body, name, (H // HB, S // T), rider,
        in_specs=[pl.BlockSpec((HB, T, HEAD_PAD), lambda h, i: (h, i, 0)),
                  pl.BlockSpec((HB, S, HEAD_PAD), lambda h, i: (h, 0, 0), pipeline_mode=pl.Buffered(1)),
                  pl.BlockSpec((HB, S, HEAD_PAD), lambda h, i: (h, 0, 0), pipeline_mode=pl.Buffered(1)),
                  pl.BlockSpec((nb, T, T), lambda h, i: (0, 0, 0), pipeline_mode=pl.Buffered(1))],
        out_specs=[pl.BlockSpec((HB, T, HEAD_PAD), lambda h, i: (h, i, 0)),
                   pl.BlockSpec((HB, 1, 1, T), lambda h, i: (h, i, 0, 0))],
        out_shape=[jax.ShapeDtypeStruct((H, S, HEAD_PAD), F32), jax.ShapeDtypeStruct((H, S // T, 1, T), F32)],
        scratch_shapes=([pltpu.VMEM((T, T), F32)] * HB + [pltpu.VMEM((T, T), BF16)] * HB
                        + [pltpu.VMEM((HEAD_PAD, T), F32)] * HB),
        operands=(q, k, v, table_t))


def _attn_bwd(q, k, v, do, lse, dd, table_t, full_range, name, rider=None):
    H, S, _ = q.shape
    T = ATT_T
    nq = S // T
    nb = table_t.shape[0]
    HB = ATT_BWD_HEADS
    KC = ATT_CHUNK
    chunks = [slice(c, c + KC) for c in range(0, T, KC)]

    def body(q_ref, do_ref, lse_ref, dd_ref, k_ref, v_ref, tab_ref, dq_hbm, dk_ref, dv_ref, *scratch):
        st_refs, dpt_refs, pt_refs, dst_refs = (scratch[n * HB:(n + 1) * HB] for n in range(4))
        dq_ref, dq_sem = scratch[4 * HB:]
        h = pl.program_id(0)
        j = pl.program_id(1)

        @pl.when(j == 0)
        def _():
            dq_ref[...] = jnp.zeros_like(dq_ref)

        dk_ref[...] = jnp.zeros_like(dk_ref)
        dv_ref[...] = jnp.zeros_like(dv_ref)

        def step(i, b):
            rows = pl.ds(pl.multiple_of(i * T, T), T)
            for hh in range(HB):
                st_refs[hh][...] = _dot_nt(k_ref[hh], q_ref[hh, rows, :])
                dpt_refs[hh][...] = _dot_nt(v_ref[hh], do_ref[hh, rows, :])
            for hh in range(HB):
                lse_i = lse_ref[hh, i]
                dd_i = dd_ref[hh, i]
                for ch in chunks:
                    x = st_refs[hh][ch, :]
                    if b is not None:
                        x = x + tab_ref[b, ch, :]
                    pc = jnp.exp2(x - lse_i)
                    pt_refs[hh][ch, :] = pc.astype(BF16)
                    dst_refs[hh][ch, :] = (pc * (dpt_refs[hh][ch, :] - dd_i)).astype(BF16)
                dv_ref[hh] += _dot(pt_refs[hh][...], do_ref[hh, rows, :])
                dk_ref[hh] += _dot(dst_refs[hh][...], q_ref[hh, rows, :])
                dq_ref[hh, rows, :] += _dot_tn(dst_refs[hh][...], k_ref[hh])

        step(j, 0)
        if full_range:
            pl.loop(j + 1, nq)(lambda i: step(i, None))
        else:
            pl.loop(j + 1, jnp.minimum(j + nb, nq))(lambda i: step(i, i - j))

        @pl.when(j == nq - 1)
        def _():
            out = pltpu.make_async_copy(dq_ref, dq_hbm.at[pl.ds(h * HB, HB)], dq_sem)
            out.start()
            out.wait()

    once = dict(pipeline_mode=pl.Buffered(1))
    per_head = pl.BlockSpec((HB, S, HEAD_PAD), lambda h, j: (h, 0, 0), **once)
    rows = pl.BlockSpec((HB, nq, 1, T), lambda h, j: (h, 0, 0, 0))
    blk = pl.BlockSpec((HB, T, HEAD_PAD), lambda h, j: (h, j, 0))
    shp = jax.ShapeDtypeStruct((H, S, HEAD_PAD), F32)
    return _call_with_rider(
        body, name, (H // HB, nq), rider,
        in_specs=[per_head, per_head, rows, rows, blk, blk,
                  pl.BlockSpec((nb, T, T), lambda h, j: (0, 0, 0), **once)],
        out_specs=[pl.BlockSpec(memory_space=pltpu.HBM), blk, blk], out_shape=[shp, shp, shp],
        scratch_shapes=([pltpu.VMEM((T, T), F32)] * (2 * HB) + [pltpu.VMEM((T, T), BF16)] * (2 * HB)
                        + [pltpu.VMEM((HB, S, HEAD_PAD), F32), pltpu.SemaphoreType.DMA]),
        operands=(q, do, lse, dd, k, v, table_t))


def _mid_bwd(dh2, h1, e, gate, g2, w_pg, w_out, oa, ob, zm):
    S = dh2.shape[0]

    def body(dh2_ref, h1_ref, e_ref, gate_ref, g2_ref, wg_ref, wo_ref, oa_ref, ob_ref, ga_ref, gb_ref,
             dh1_ref, dh1b_ref, de_ref, dpre_ref, doa_ref, dob_ref, dga_ref, dgb_ref, dd_ref, dg2_ref):
        @pl.when(pl.program_id(0) == 0)
        def _():
            dg2_ref[...] = jnp.zeros_like(dg2_ref)

        lane = lax.broadcasted_iota(jnp.int32, (TOK_T, HEAD_PAD), 1)
        lo_half = lane < HEAD_DIM
        dh2 = dh2_ref[...]
        gate = gate_ref[...]
        de_ref[...] = (dh2 * gate).astype(BF16)
        dpre = (dh2 * e_ref[...] * gate * (1.0 - gate)).astype(BF16)
        dpre_ref[...] = dpre
        du2 = _dot_nt(dpre, wg_ref[...])
        h1 = h1_ref[...]
        r = lax.rsqrt(jnp.mean(h1 * h1, axis=-1, keepdims=True) + EPS)
        xh = h1 * r
        a = du2 * g2_ref[...]
        dh1 = dh2 + r * (a - xh * jnp.mean(a * xh, axis=-1, keepdims=True))
        dg2_ref[...] += jnp.sum(du2 * xh, axis=0, keepdims=True)
        dh1_ref[...] = dh1
        dh1b = dh1.astype(BF16)
        dh1b_ref[...] = dh1b
        dy = _dot_nt(dh1b, wo_ref[...])
        dd = jnp.zeros((TOK_T, HEAD_PAD), F32)
        for bi, (o_ref, g_ref, do_ref, dg_ref) in enumerate(
                ((oa_ref, ga_ref, doa_ref, dga_ref), (ob_ref, gb_ref, dob_ref, dgb_ref))):
            for pi in range(N_HEADS // 2):
                col = bi * D_BRANCH + HEAD_PAD * pi
                dyp = dy[:, col:col + HEAD_PAD]
                g = g_ref[:, HEAD_PAD * pi:HEAD_PAD * (pi + 1)]
                sg = _sigmoid(g)
                dg_ref[:, HEAD_PAD * pi:HEAD_PAD * (pi + 1)] = (
                    dyp * _pair(o_ref, pi, lo_half) * (sg * (1.0 + g * (1.0 - sg)))).astype(BF16)
                dop = dyp * (g * sg)
                for hh, d_head in ((2 * pi, dop), (2 * pi + 1, pltpu.roll(dop, HEAD_DIM, 1))):
                    d_head = jnp.where(lo_half, d_head, 0.0)
                    do_ref[hh] = d_head.astype(BF16)
                    dsum = jnp.sum(d_head * o_ref[hh], axis=1, keepdims=True)
                    dd = dd + jnp.where(lane == bi * N_HEADS + hh, dsum, 0.0)
        dd_ref[...] = dd.T[:2 * N_HEADS, :]

    tok = lambda w: pl.BlockSpec((TOK_T, w), lambda i: (i, 0))
    head = pl.BlockSpec((N_HEADS, TOK_T, HEAD_PAD), lambda i: (0, i, 0))
    full = lambda a, b: pl.BlockSpec((a, b), lambda i: (0, 0))
    act = lambda w, dt: jax.ShapeDtypeStruct((S, w), dt)
    hshape = lambda w, dt: jax.ShapeDtypeStruct((N_HEADS, S, w), dt)
    return pl.pallas_call(
        body, name="mid_bwd", grid=(S // TOK_T,),
        in_specs=[tok(D_MODEL)] * 4 + [full(1, D_MODEL), full(D_MODEL, D_MODEL), full(D_MODEL, D_MODEL), head, head,
                                      pl.BlockSpec((TOK_T, D_BRANCH), lambda i: (i, 3)),
                                      pl.BlockSpec((TOK_T, D_BRANCH), lambda i: (i, 7))],
        out_specs=[tok(D_MODEL)] * 4 + [head, head, tok(D_BRANCH), tok(D_BRANCH),
                                       pl.BlockSpec((2 * N_HEADS, TOK_T), lambda i: (0, i)), full(1, D_MODEL)],
        out_shape=[act(D_MODEL, F32), act(D_MODEL, BF16), act(D_MODEL, BF16), act(D_MODEL, BF16),
                   hshape(HEAD_PAD, BF16), hshape(HEAD_PAD, BF16), act(D_BRANCH, BF16), act(D_BRANCH, BF16),
                   jax.ShapeDtypeStruct((2 * N_HEADS, S), F32), jax.ShapeDtypeStruct((1, D_MODEL), F32)],
        compiler_params=_cp("arbitrary"),
    )(dh2, h1, e, gate, g2, w_pg, w_out, oa, ob, zm, zm)


def _prep_bwd(dqa, dka, dva, dqb, dkb, dvb, zm, qkg, rope_c, rope_a, rope_b, dga, dgb):
    S = zm.shape[0]

    def body(dqa_ref, dka_ref, dva_ref, dqb_ref, dkb_ref, dvb_ref, z_ref, g_ref, rc_ref, ra_ref, rb_ref,
             dga_ref, dgb_ref, dz_ref, dc_ref, dqkg_ref):
        @pl.when(pl.program_id(0) == 0)
        def _():
            dqkg_ref[...] = jnp.zeros_like(dqkg_ref)

        lane = lax.broadcasted_iota(jnp.int32, (TOK_T, HEAD_PAD), 1)
        lo_half = lane < HEAD_DIM
        rc, ra, rb = rc_ref[...], ra_ref[...], rb_ref[...]
        same_head = _same_head()

        def unrope(dy):
            return dy * rc + pltpu.roll(dy * ra, ROPE_HALF, 1) + pltpu.roll(dy * rb, HEAD_PAD - ROPE_HALF, 1)

        def norm_bwd(col, gi, dy):
            x = z_ref[:, col:col + HEAD_PAD]
            r = _pair_rsqrt(x, same_head)
            xh = x * r
            dqkg_ref[gi:gi + 1, :] += jnp.sum(dy * xh, axis=0, keepdims=True)
            a = dy * g_ref[gi:gi + 1, :]
            dz_ref[:, col:col + HEAD_PAD] = (r * (a - xh * _pair_mean(a * xh, same_head))).astype(BF16)

        dc = jnp.zeros((TOK_T, HEAD_PAD), F32)
        for pi in range(N_HEADS // 2):
            col = HEAD_PAD * pi
            norm_bwd(col, 0, _pair(dqa_ref, pi, lo_half) * SCALE)
            norm_bwd(D_BRANCH + col, 1, _pair(dka_ref, pi, lo_half) * LN2)
            dz_ref[:, 2 * D_BRANCH + col:2 * D_BRANCH + col + HEAD_PAD] = _pair(dva_ref, pi, lo_half).astype(BF16)
            norm_bwd(4 * D_BRANCH + col, 2, unrope(_pair(dqb_ref, pi, lo_half) * SCALE))
            norm_bwd(5 * D_BRANCH + col, 3, unrope(_pair(dkb_ref, pi, lo_half) * LN2))
            dz_ref[:, 6 * D_BRANCH + col:6 * D_BRANCH + col + HEAD_PAD] = _pair(dvb_ref, pi, lo_half).astype(BF16)
            for hh in (2 * pi, 2 * pi + 1):
                dch = dka_ref[hh][:, HEAD_DIM:HEAD_DIM + 1] + dqa_ref[hh][:, HEAD_DIM + 3:HEAD_DIM + 4]
                dc = dc + jnp.where(lane == hh, dch, 0.0)
        dz_ref[:, 3 * D_BRANCH:4 * D_BRANCH] = dga_ref[...]
        dz_ref[:, 7 * D_BRANCH:8 * D_BRANCH] = dgb_ref[...]
        dc_ref[...] = dc

    tok = lambda w: pl.BlockSpec((TOK_T, w), lambda i: (i, 0))
    head = pl.BlockSpec((N_HEADS, TOK_T, HEAD_PAD), lambda i: (0, i, 0))
    return pl.pallas_call(
        body, name="prep_bwd", grid=(S // TOK_T,),
        in_specs=[head] * 6 + [tok(N_MAIN), pl.BlockSpec((4, 128), lambda i: (0, 0)), tok(128), tok(128), tok(128),
                               tok(D_BRANCH), tok(D_BRANCH)],
        out_specs=[tok(N_MAIN), tok(128), pl.BlockSpec((4, 128), lambda i: (0, 0))],
        out_shape=[jax.ShapeDtypeStruct((S, N_MAIN), BF16), jax.ShapeDtypeStruct((S, 128), F32),
                   jax.ShapeDtypeStruct((4, 128), F32)],
        compiler_params=_cp("arbitrary"),
    )(dqa, dka, dva, dqb, dkb, dvb, zm, qkg, rope_c, rope_a, rope_b, dga, dgb)


def _inproj_bwd(dzm, dzf, wm, wf, h0, dh1, g):
    S = h0.shape[0]

    def body(dzm_ref, dzf_ref, wm_ref, wf_ref, h_ref, dh1_ref, g_ref, dh0_ref, dg_ref):
        @pl.when(pl.program_id(0) == 0)
        def _():
            dg_ref[...] = jnp.zeros_like(dg_ref)

        du = _dot_nt(dzm_ref[...], wm_ref[...]) + _dot_nt(dzf_ref[...], wf_ref[...])
        x = h_ref[...]
        r = lax.rsqrt(jnp.mean(x * x, axis=-1, keepdims=True) + EPS)
        xh = x * r
        a = du * g_ref[...]
        dh0_ref[...] = dh1_ref[...] + r * (a - xh * jnp.mean(a * xh, axis=-1, keepdims=True))
        dg_ref[...] += jnp.sum(du * xh, axis=0, keepdims=True)

    tok = lambda w: pl.BlockSpec((TOK_T, w), lambda i: (i, 0))
    full = lambda a, b: pl.BlockSpec((a, b), lambda i: (0, 0))
    return pl.pallas_call(
        body, name="inproj_bwd", grid=(S // TOK_T,),
        in_specs=[tok(N_MAIN), tok(128), full(D_MODEL, N_MAIN), full(D_MODEL, 128), tok(D_MODEL), tok(D_MODEL),
                  full(1, D_MODEL)],
        out_specs=[tok(D_MODEL), full(1, D_MODEL)],
        out_shape=[jax.ShapeDtypeStruct((S, D_MODEL), F32), jax.ShapeDtypeStruct((1, D_MODEL), F32)],
        compiler_params=_cp("arbitrary"),
    )(dzm, dzf, wm, wf, h0, dh1, g)


def _wgrad(a, b, name, a_lead=()):
    S, M = a.shape[len(a_lead):]
    N = b.shape[1]
    tn = min(N, 2048)
    ts = 512
    last = S // ts - 1

    def body(a_ref, b_ref, o_ref, acc_ref):
        @pl.when(pl.program_id(1) == 0)
        def _():
            acc_ref[...] = jnp.zeros_like(acc_ref)

        acc_ref[...] += _dot_tn(a_ref[...].astype(BF16), b_ref[...])

        @pl.when(pl.program_id(1) == last)
        def _():
            o_ref[...] = acc_ref[...].astype(BF16)

    return pl.pallas_call(
        body, name=name, grid=(N // tn, S // ts),
        in_specs=[_slab_spec(a_lead, (ts, M), lambda n, s: (s, 0)), pl.BlockSpec((ts, tn), lambda n, s: (s, n))],
        out_specs=pl.BlockSpec((M, tn), lambda n, s: (0, n)),
        out_shape=jax.ShapeDtypeStruct((M, N), BF16),
        scratch_shapes=[pltpu.VMEM((M, tn), F32)],
        compiler_params=_cp("parallel", "arbitrary"),
    )(a, b)


def _rope_tables(positions):
    inv_freq = ROPE_THETA ** (-jnp.arange(ROPE_HALF, dtype=F32) / ROPE_HALF)
    ang = positions.astype(F32)[:, None] * inv_freq
    cos, sin = jnp.cos(ang), jnp.sin(ang)
    S = positions.shape[0]
    one, zero = jnp.ones((S, HEAD_DIM - 2 * ROPE_HALF), F32), jnp.zeros((S, HEAD_DIM - 2 * ROPE_HALF), F32)
    z8 = jnp.zeros((S, ROPE_HALF), F32)
    rc = jnp.concatenate([cos, cos, one], axis=1)
    ra = jnp.concatenate([-sin, z8, zero], axis=1)
    rb = jnp.concatenate([z8, sin, zero], axis=1)
    return tuple(jnp.tile(t, (1, 2)) for t in (rc, ra, rb))


def _in_operands(w_in):
    w_in = w_in.astype(BF16)
    wm = jnp.concatenate([w_in[:, :4 * D_BRANCH], w_in[:, 4 * D_BRANCH + N_HEADS:]], axis=1)
    wf = jnp.pad(w_in[:, 4 * D_BRANCH:4 * D_BRANCH + N_HEADS], ((0, 0), (0, 128 - N_HEADS)))
    return dict(wm=wm, wf=wf)


def _layer_weights(w_in, w_out, w_ple, w_pg):
    return dict(_in_operands(w_in), w_out=w_out.astype(BF16), w_ple=w_ple.astype(BF16), w_pg=w_pg.astype(BF16))


def _row(v, width=128):
    v = v.reshape(1, -1).astype(F32)
    return jnp.pad(v, ((0, 0), (0, width - v.shape[1])))


def _layer_fwd(h0, p, rope, tabs, w, norm_g, b_f, qk_g, ple_g, rider=None, late=None):
    g1 = norm_g.reshape(1, D_MODEL)
    g2 = ple_g.reshape(1, D_MODEL)
    qkg = jnp.tile(qk_g, (1, 2))
    bf = _row(b_f)
    zm, zf, u = _inproj_fwd(h0, g1, w["wm"], w["wf"])
    c = _forget_cumsum(zf, bf, tabs["tril"])
    qa, ka, va, qb, kb, vb = _prep_fwd(zm, c, qkg, *rope)
    oa, lse_a, *arrivals = _attn_fwd(qa, ka, va, tabs["fox"], True, "fox_fwd", rider)
    if late is not None:
        w = {**w, **late(arrivals)}
    ob, lse_b = _attn_fwd(qb, kb, vb, tabs["dil"], False, "dil_fwd")
    y, h1, h2, u2, e, gate = _mid_fwd(oa, ob, zm, h0, p, w["w_out"], w["w_pg"], w["w_ple"], g2)
    saved = dict(h0=h0, p=p, zm=zm, zf=zf, u=u, qa=qa, ka=ka, va=va, qb=qb, kb=kb, vb=vb, oa=oa, ob=ob,
                 lse_a=lse_a, lse_b=lse_b, y=y, h1=h1, u2=u2, e=e, gate=gate, g1=g1, g2=g2, qkg=qkg, bf=bf, w=w)
    return h2, saved, arrivals


def _layer_bwd(dh2, sv, rope, tabs, make_rider=None):
    S = dh2.shape[0]
    nq = S // ATT_T
    w = sv["w"]
    rows = lambda a: a.reshape(N_HEADS, nq, 1, ATT_T)
    (dh1, dh1b, de, dpre, doa, dob, dga, dgb, dd, dg2) = _mid_bwd(
        dh2, sv["h1"], sv["e"], sv["gate"], sv["g2"], w["w_pg"], w["w_out"], sv["oa"], sv["ob"], sv["zm"])
    dda, ddb = dd[:N_HEADS], dd[N_HEADS:]
    early = dict(w_out=_wgrad(sv["y"], dh1b, "wgrad_out"), w_ple=_wgrad(sv["p"][0], de, "wgrad_ple", sv["p"][1]),
                 w_ple_gate=_wgrad(sv["u2"], dpre, "wgrad_gate"))
    rider = None if make_rider is None else make_rider(early)
    dqa, dka, dva, *arrivals = _attn_bwd(sv["qa"], sv["ka"], sv["va"], doa, sv["lse_a"], rows(dda), tabs["fox"],
                                         True, "fox_bwd", rider)
    dqb, dkb, dvb = _attn_bwd(sv["qb"], sv["kb"], sv["vb"], dob, sv["lse_b"], rows(ddb), tabs["dil"], False,
                              "dil_bwd")
    dzm, dc, dqkg = _prep_bwd(dqa, dka, dva, dqb, dkb, dvb, sv["zm"], sv["qkg"], *rope, dga, dgb)
    dzf, dbf = _forget_bwd(dc, sv["zf"], sv["bf"], tabs["triu"])
    dh0, dg1 = _inproj_bwd(dzm, dzf, w["wm"], w["wf"], sv["h0"], dh1, sv["g1"])
    dwm = _wgrad(sv["u"], dzm, "wgrad_in")
    dwf = _wgrad(sv["u"], dzf, "wgrad_f")
    dw_in = jnp.concatenate([dwm[:, :4 * D_BRANCH], dwf[:, :N_HEADS], dwm[:, 4 * D_BRANCH:]], axis=1)
    grads = dict(norm_g=dg1.reshape(D_MODEL), w_in=dw_in, b_f=dbf[0, :N_HEADS],
                 qk_norm_g=dqkg[:, :HEAD_DIM] + dqkg[:, HEAD_DIM:], ple_norm_g=dg2.reshape(D_MODEL), **early)
    return dh0, grads, arrivals


def _tables():
    T = CUM_T
    r = lax.broadcasted_iota(jnp.int32, (T, T), 0)
    c = lax.broadcasted_iota(jnp.int32, (T, T), 1)
    return dict(fox=_bias_tables(True), dil=_bias_tables(False),
                tril=(c <= r).astype(BF16), triu=(c >= r).astype(BF16))


def _local_step(x, p, positions, target, layers, small):
    rope = _rope_tables(positions)
    tabs = _tables()
    ws = [_layer_weights(*lw) for lw in layers]
    h = x
    saved = []
    for w, lp, sm in zip(ws, p, small):
        h, sv, _ = _layer_fwd(h, (lp, ()), rope, tabs, w, *sm)
        saved.append(sv)
    dh, loss = _loss_fwd_bwd(h, target)
    grads = [None] * len(ws)
    for li in reversed(range(len(ws))):
        dh, grads[li], _ = _layer_bwd(dh, saved[li], rope, tabs)
    return loss[0, 0], dh, grads


def _peers():
    x, y, c = lax.axis_index("x"), lax.axis_index("y"), lax.axis_index("c")
    me = 4 * x + 2 * y + c
    flip = lambda v, bit: 1 - v if bit else v
    return me, [(flip(x, k & 4), flip(y, k & 2), flip(c, k & 1)) for k in range(1, N_DEV)]


def _sel(ref, kind, d):
    if kind == "whole":
        return ref
    if kind == "slot":
        return ref.at[d]
    block = pl.ds(pl.multiple_of(d * 128, 128), 128)
    return ref.at[block, :] if kind == "rows" else ref.at[:, block]


class _Pushes:
    def __init__(self, arrays, src_kinds, dst_kinds, out_shapes):
        self.arrays, self.n = list(arrays), len(arrays)
        self.src_kinds, self.dst_kinds = src_kinds, dst_kinds
        self.out_shapes = [jax.ShapeDtypeStruct(s, a.dtype) for s, a in zip(out_shapes, arrays)]
        hbm = pl.BlockSpec(memory_space=pltpu.HBM)
        self.in_specs, self.out_specs = [hbm] * self.n, [hbm] * self.n
        self.scratch_shapes = [pltpu.SemaphoreType.DMA((N_DEV - 1, self.n)),
                               pltpu.SemaphoreType.DMA((N_DEV - 1, self.n)), pltpu.SemaphoreType.DMA((self.n,))]

    def _copies(self, ins, outs, sems):
        send_sems, recv_sems, local_sems = sems
        me, peers = _peers()
        src = lambda a, d: _sel(ins[a], self.src_kinds[a], d)
        dst = lambda a: _sel(outs[a], self.dst_kinds[a], me)
        local = [pltpu.make_async_copy(src(a, me), dst(a), local_sems.at[a]) for a in range(self.n)]
        remote = [pltpu.make_async_remote_copy(
            src_ref=src(a, 4 * px + 2 * py + pc), dst_ref=dst(a), send_sem=send_sems.at[k, a],
            recv_sem=recv_sems.at[k, a], device_id=(px, py, pc), device_id_type=pl.DeviceIdType.MESH)
            for k, (px, py, pc) in enumerate(peers) for a in range(self.n)]
        return local + remote

    def start(self, ins, outs, sems):
        for cp in self._copies(ins, outs, sems):
            cp.start()

    def wait(self, ins, outs, sems):
        for cp in self._copies(ins, outs, sems):
            cp.wait()


def _exchange(name, pushes):
    n = pushes.n

    def body(*refs):
        pushes.start(refs[:n], refs[n:2 * n], refs[2 * n:])
        pushes.wait(refs[:n], refs[n:2 * n], refs[2 * n:])

    return pl.pallas_call(body, name=name, in_specs=pushes.in_specs, out_specs=pushes.out_specs,
                          out_shape=pushes.out_shapes, scratch_shapes=pushes.scratch_shapes)(*pushes.arrays)


def _gather_pushes(shards, kinds):
    full = {"slot": lambda s: (N_DEV,) + s, "rows": lambda s: (N_DEV * s[0], s[1]),
            "cols": lambda s: (s[0], N_DEV * s[1])}
    return _Pushes(shards, ["whole"] * len(shards), kinds, [full[k](a.shape) for a, k in zip(shards, kinds)])


def _scatter_pushes(partials, kinds):
    part = {"slot": lambda s: s[1:], "rows": lambda s: (128, s[1]), "cols": lambda s: (s[0], 128),
            "whole": lambda s: s}
    return _Pushes(partials, kinds, ["slot"] * len(partials),
                   [(N_DEV,) + part[k](a.shape) for a, k in zip(partials, kinds)])


def _adamw(name, parts, w, m, v, rows):
    L, R, C = w.shape

    def body(p_ref, w_ref, m_ref, v_ref, g_ref, d_ref, nm_ref, nv_ref):
        g = p_ref[0, 0].astype(F32)
        for s in range(1, N_DEV):
            g = g + p_ref[s, 0].astype(F32)
        g_ref[0] = g
        nm = ADAM_B1 * m_ref[0] + (1.0 - ADAM_B1) * g
        nv = ADAM_B2 * v_ref[0] + (1.0 - ADAM_B2) * (g * g)
        nm_ref[0] = nm
        nv_ref[0] = nv
        m_hat = nm / (1.0 - ADAM_B1 ** ADAM_STEP)
        v_hat = nv / (1.0 - ADAM_B2 ** ADAM_STEP)
        d_ref[0] = -ADAM_LR * (m_hat / (jnp.sqrt(v_hat) + ADAM_EPS) + ADAM_WD * w_ref[0])

    blk = pl.BlockSpec((1, rows, C), lambda l, i: (l, i, 0))
    shp = jax.ShapeDtypeStruct((L, R, C), F32)
    return pl.pallas_call(
        body, name=name, grid=(L, R // rows),
        in_specs=[pl.BlockSpec((N_DEV, 1, rows, C), lambda l, i: (0, l, i, 0)), blk, blk, blk],
        out_specs=[blk] * 4, out_shape=[shp] * 4,
        compiler_params=_cp("parallel", "parallel"),
    )(parts, w, m, v)


SMALL_ROWS = 40
LOSS_ROW = 37


def _pack_small(norm_g, ple_g, qk_g, b_f, last_row):
    rows = lambda a: a.astype(F32).reshape(-1, 128)
    flat = jnp.concatenate([rows(norm_g), rows(ple_g), rows(qk_g), _row(b_f.reshape(-1)), last_row], axis=0)
    return jnp.pad(flat, ((0, SMALL_ROWS - flat.shape[0]), (0, 0)))


def _unpack_small(flat):
    return (flat[0:16].reshape(2, D_MODEL), flat[16:32].reshape(2, D_MODEL), flat[32:36].reshape(2, 4, HEAD_DIM),
            flat[36, :2 * N_HEADS].reshape(2, N_HEADS))


def kernel(x, p, positions, norm_g, w_in, b_f, qk_norm_g, w_out, w_ple, ple_norm_g, w_ple_gate, loss_target, m_norm_g, m_w_in, m_b_f, m_qk_norm_g, m_w_out, m_w_ple, m_ple_norm_g, m_w_ple_gate, v_norm_g, v_w_in, v_b_f, v_qk_norm_g, v_w_out, v_w_ple, v_ple_norm_g, v_w_ple_gate):
    bf16 = lambda a: a.astype(BF16)
    rows_in = W_IN_ROWS // 2
    flat_in = lambda a: bf16(a).reshape(rows_in, 128)
    full_in = lambda g: g.reshape(N_DEV, D_MODEL, W_IN_SHARD).transpose(1, 0, 2).reshape(D_MODEL, N_IN)
    small = [(norm_g[l], b_f[l], qk_norm_g[l], ple_norm_g[l]) for l in range(2)]
    rope = _rope_tables(positions[0])
    tabs = _tables()

    (g_in0,) = _exchange("gather_first", _gather_pushes([flat_in(w_in[0])], ["slot"]))
    rest = _gather_pushes([flat_in(w_in[1])] + [bf16(a[l]) for l in range(2) for a in (w_out, w_ple, w_ple_gate)],
                          ["slot"] + ["rows", "cols", "rows"] * 2)
    late = lambda got: dict(w_out=got[1], w_ple=got[2], w_pg=got[3])
    h, sv0, got = _layer_fwd(x[0], (p, (0, 0)), rope, tabs, _in_operands(full_in(g_in0)), *small[0], rest, late)
    w1 = dict(_in_operands(full_in(got[0])), w_out=got[4], w_ple=got[5], w_pg=got[6])
    h, sv1, _ = _layer_fwd(h, (p, (1, 0)), rope, tabs, w1, *small[1])
    dh, loss = _loss_fwd_bwd(h, loss_target[0])
    dh, gr1, _ = _layer_bwd(dh, sv1, rope, tabs)

    by_dest = lambda d: d.reshape(D_MODEL, N_DEV, W_IN_SHARD).transpose(1, 0, 2).reshape(N_DEV, rows_in, 128)
    big = ("w_out", "w_ple", "w_ple_gate")
    riding = lambda early: _scatter_pushes([by_dest(gr1["w_in"])] + [gr1[n] for n in big] + [early[n] for n in big],
                                           ["slot"] + ["rows", "cols", "rows"] * 2)
    dx, gr0, (r_in1, *r_big) = _layer_bwd(dh, sv0, rope, tabs, riding)
    grads = (gr0, gr1)
    stack = lambda name: jnp.stack([gl[name] for gl in grads], axis=0)
    small_part = _pack_small(stack("norm_g"), stack("ple_norm_g"), stack("qk_norm_g"), stack("b_f"),
                             _row(loss[0, 0].reshape(1)))
    r_in0, r_small = _exchange("exchange_last", _scatter_pushes([by_dest(gr0["w_in"]), small_part], ["slot", "whole"]))
    r_in = jnp.concatenate([r_in0, r_in1], axis=1)
    r_out, r_ple, r_pg = (jnp.stack([r_big[3 + k], r_big[k]], axis=1) for k in range(3))

    zero_row = jnp.zeros((1, 128), F32)
    small_of = lambda ng, pg, qk, bf: _pack_small(ng, pg, qk, bf, zero_row)[None]
    flat = lambda a: a.reshape(1, W_IN_ROWS, 128)
    outs = dict(
        w_in=[o.reshape(w_in.shape) for o in
              _adamw("adamw_in", r_in[:, None], flat(w_in), flat(m_w_in), flat(v_w_in), W_IN_TILE)],
        w_out=_adamw("adamw_out", r_out, w_out, m_w_out, v_w_out, 128),
        w_ple=_adamw("adamw_ple", r_ple, w_ple, m_w_ple, v_w_ple, 256),
        w_pg=_adamw("adamw_gate", r_pg, w_ple_gate, m_w_ple_gate, v_w_ple_gate, 128),
        small=_adamw("adamw_small", r_small[:, None], small_of(norm_g, ple_norm_g, qk_norm_g, b_f),
                     small_of(m_norm_g, m_ple_norm_g, m_qk_norm_g, m_b_f),
                     small_of(v_norm_g, v_ple_norm_g, v_qk_norm_g, v_b_f), SMALL_ROWS))
    leaves = []
    for kind in range(4):
        ng, pg, qk, bf = _unpack_small(outs["small"][kind][0])
        leaves += [ng, outs["w_in"][kind], bf, qk, outs["w_out"][kind], outs["w_ple"][kind], pg, outs["w_pg"][kind]]
    return (outs["small"][0][0, LOSS_ROW, 0], dx[None], *leaves)
```

```python
import functools

import jax
import jax.numpy as jnp
from jax import lax
from jax.experimental import pallas as pl
from jax.experimental.pallas import tpu as pltpu

F32 = jnp.float32
BF16 = jnp.bfloat16

D_MODEL = 1024
HEAD_DIM = 64
N_HEADS = 8
HEAD_PAD = 128
D_BRANCH = N_HEADS * HEAD_DIM
N_MAIN = 8 * D_BRANCH
N_IN = N_MAIN + N_HEADS
PLE_DIM = 256
ROPE_THETA = 500000.0
ROPE_HALF = 8
EPS = 1e-6
NEG = -1e30
SCALE = HEAD_DIM ** -0.5
LOG2E = 1.4426950408889634
LN2 = 0.6931471805599453
DILATED_PATTERNS = ((128, 1), (512, 4), (2048, 16))
FAR_DILATION = DILATED_PATTERNS[-1][1]
N_DEV = 8
W_IN_SHARD = N_IN // N_DEV
W_IN_ROWS = 2 * D_MODEL * W_IN_SHARD // 128
W_IN_TILE = W_IN_ROWS // 19

ADAM_LR = 0.001
ADAM_B1 = 0.9
ADAM_B2 = 0.999
ADAM_EPS = 1e-08
ADAM_WD = 0.01
ADAM_STEP = 10

ATT_T = 512
ATT_FWD_HEADS = 4
ATT_BWD_HEADS = 2
ATT_CHUNK = 32
TOK_T = 256
CUM_T = 512
VMEM_LIMIT = 56 * 1024 * 1024


def _slab_spec(lead, block, index):
    return pl.BlockSpec((None,) * len(lead) + block, lambda *g: (*lead, *index(*g)))


def _cp(*sem):
    return pltpu.CompilerParams(dimension_semantics=sem, vmem_limit_bytes=VMEM_LIMIT)


def _sigmoid(x):
    return 1.0 / (1.0 + jnp.exp(-x))


def _split3(x):
    hi = x.astype(BF16)
    r1 = x - hi.astype(F32)
    mid = r1.astype(BF16)
    lo = (r1 - mid.astype(F32)).astype(BF16)
    return hi, mid, lo


def _dot(a, b):
    return jnp.dot(a, b, preferred_element_type=F32)


def _dot_nt(a, b):
    return lax.dot_general(a, b, (((1,), (1,)), ((), ())), preferred_element_type=F32)


def _dot_tn(a, b):
    return lax.dot_general(a, b, (((0,), (0,)), ((), ())), preferred_element_type=F32)


def _inproj_fwd(h, g, wm, wf):
    S = h.shape[0]

    def body(h_ref, g_ref, wm_ref, wf_ref, zm_ref, zf_ref, u_ref):
        x = h_ref[...]
        r = lax.rsqrt(jnp.mean(x * x, axis=-1, keepdims=True) + EPS)
        u = (x * r * g_ref[...]).astype(BF16)
        u_ref[...] = u
        zm_ref[...] = _dot(u, wm_ref[...])
        zf_ref[...] = _dot(u, wf_ref[...])

    return pl.pallas_call(
        body, name="inproj_fwd", grid=(S // TOK_T,),
        in_specs=[pl.BlockSpec((TOK_T, D_MODEL), lambda i: (i, 0)),
                  pl.BlockSpec((1, D_MODEL), lambda i: (0, 0)),
                  pl.BlockSpec((D_MODEL, N_MAIN), lambda i: (0, 0)),
                  pl.BlockSpec((D_MODEL, 128), lambda i: (0, 0))],
        out_specs=[pl.BlockSpec((TOK_T, N_MAIN), lambda i: (i, 0)),
                   pl.BlockSpec((TOK_T, 128), lambda i: (i, 0)),
                   pl.BlockSpec((TOK_T, D_MODEL), lambda i: (i, 0))],
        out_shape=[jax.ShapeDtypeStruct((S, N_MAIN), F32), jax.ShapeDtypeStruct((S, 128), F32),
                   jax.ShapeDtypeStruct((S, D_MODEL), BF16)],
        compiler_params=_cp("parallel"),
    )(h, g, wm, wf)


def _log_sigmoid(x):
    return jnp.minimum(x, 0.0) - jnp.log(1.0 + jnp.exp(-jnp.abs(x)))


def _forget_cumsum(zf, bf, tri):
    S = zf.shape[0]

    def body(zf_ref, b_ref, tri_ref, c_ref, carry):
        @pl.when(pl.program_id(0) == 0)
        def _():
            carry[...] = jnp.zeros_like(carry)

        lf = _log_sigmoid(zf_ref[...] + b_ref[...])
        hi, mid, lo = _split3(lf)
        t = tri_ref[...]
        cs = _dot(t, hi) + _dot(t, mid) + _dot(t, lo) + carry[...]
        c_ref[...] = cs
        carry[...] = cs[CUM_T - 1:CUM_T, :]

    return pl.pallas_call(
        body, name="forget_cumsum", grid=(S // CUM_T,),
        in_specs=[pl.BlockSpec((CUM_T, 128), lambda i: (i, 0)),
                  pl.BlockSpec((1, 128), lambda i: (0, 0)),
                  pl.BlockSpec((CUM_T, CUM_T), lambda i: (0, 0))],
        out_specs=pl.BlockSpec((CUM_T, 128), lambda i: (i, 0)),
        out_shape=jax.ShapeDtypeStruct((S, 128), F32),
        scratch_shapes=[pltpu.VMEM((1, 128), F32)],
        compiler_params=_cp("arbitrary"),
    )(zf, bf, tri)


def _forget_bwd(dc, zf, bf, triu):
    S = zf.shape[0]
    n = S // CUM_T

    def body(dc_ref, zf_ref, b_ref, tri_ref, dzf_ref, db_ref, carry):
        @pl.when(pl.program_id(0) == 0)
        def _():
            carry[...] = jnp.zeros_like(carry)
            db_ref[...] = jnp.zeros_like(db_ref)

        hi, mid, lo = _split3(dc_ref[...])
        t = tri_ref[...]
        dlf = _dot(t, hi) + _dot(t, mid) + _dot(t, lo) + carry[...]
        carry[...] = dlf[0:1, :]
        dfa = dlf * (1.0 - _sigmoid(zf_ref[...] + b_ref[...]))
        dzf_ref[...] = dfa.astype(BF16)
        db_ref[...] += jnp.sum(dfa, axis=0, keepdims=True)

    return pl.pallas_call(
        body, name="forget_bwd", grid=(n,),
        in_specs=[pl.BlockSpec((CUM_T, 128), lambda i: (n - 1 - i, 0)),
                  pl.BlockSpec((CUM_T, 128), lambda i: (n - 1 - i, 0)),
                  pl.BlockSpec((1, 128), lambda i: (0, 0)),
                  pl.BlockSpec((CUM_T, CUM_T), lambda i: (0, 0))],
        out_specs=[pl.BlockSpec((CUM_T, 128), lambda i: (n - 1 - i, 0)),
                   pl.BlockSpec((1, 128), lambda i: (0, 0))],
        out_shape=[jax.ShapeDtypeStruct((S, 128), BF16), jax.ShapeDtypeStruct((1, 128), F32)],
        scratch_shapes=[pltpu.VMEM((1, 128), F32)],
        compiler_params=_cp("arbitrary"),
    )(dc, zf, bf, triu)


def _same_head():
    r = lax.broadcasted_iota(jnp.int32, (HEAD_PAD, HEAD_PAD), 0) // HEAD_DIM
    c = lax.broadcasted_iota(jnp.int32, (HEAD_PAD, HEAD_PAD), 1) // HEAD_DIM
    return (r == c).astype(BF16)


def _pair_mean(x, same_head):
    hi = x.astype(BF16)
    lo = (x - hi.astype(F32)).astype(BF16)
    return (_dot(hi, same_head) + _dot(lo, same_head)) * (1.0 / HEAD_DIM)


def _pair_rsqrt(x, same_head):
    return lax.rsqrt(_pair_mean(x * x, same_head) + EPS)


def _prep_fwd(zm, c, qkg, rope_c, rope_a, rope_b):
    S = zm.shape[0]
    shp = jax.ShapeDtypeStruct((N_HEADS, S, HEAD_PAD), BF16)

    def body(z_ref, c_ref, g_ref, rc_ref, ra_ref, rb_ref, qa_ref, ka_ref, va_ref, qb_ref, kb_ref, vb_ref):
        lane = lax.broadcasted_iota(jnp.int32, (TOK_T, HEAD_PAD), 1)
        lo_half = lane < HEAD_DIM
        aug = (lane >= HEAD_DIM) & (lane < HEAD_DIM + 3)
        q_pad = jnp.where(aug, -1.0, 0.0)
        cs = c_ref[...]
        rc, ra, rb = rc_ref[...], ra_ref[...], rb_ref[...]
        same_head = _same_head()

        def norm(col, gi):
            x = z_ref[:, col:col + HEAD_PAD]
            return x * _pair_rsqrt(x, same_head) * g_ref[gi:gi + 1, :]

        def rope(y):
            return y * rc + pltpu.roll(y, HEAD_PAD - ROPE_HALF, 1) * ra + pltpu.roll(y, ROPE_HALF, 1) * rb

        def put(ref, pi, y, pad_even, pad_odd):
            ref[2 * pi] = jnp.where(lo_half, y, pad_even).astype(BF16)
            ref[2 * pi + 1] = jnp.where(lo_half, pltpu.roll(y, HEAD_DIM, 1), pad_odd).astype(BF16)

        def k_pad(h):
            ch = cs[:, h:h + 1] * LOG2E
            hi = ch.astype(BF16).astype(F32)
            mid = (ch - hi).astype(BF16).astype(F32)
            lo = ch - hi - mid
            ones = jnp.where(lane == HEAD_DIM + 3, 1.0, 0.0)
            return jnp.where(lane == HEAD_DIM, hi, jnp.where(lane == HEAD_DIM + 1, mid,
                                                              jnp.where(lane == HEAD_DIM + 2, lo, ones)))

        for pi in range(N_HEADS // 2):
            col = HEAD_PAD * pi
            put(qa_ref, pi, norm(col, 0) * (SCALE * LOG2E), q_pad, q_pad)
            put(ka_ref, pi, norm(D_BRANCH + col, 1), k_pad(2 * pi), k_pad(2 * pi + 1))
            put(va_ref, pi, z_ref[:, 2 * D_BRANCH + col:2 * D_BRANCH + col + HEAD_PAD], 0.0, 0.0)
            put(qb_ref, pi, rope(norm(4 * D_BRANCH + col, 2)) * (SCALE * LOG2E), 0.0, 0.0)
            put(kb_ref, pi, rope(norm(5 * D_BRANCH + col, 3)), 0.0, 0.0)
            put(vb_ref, pi, z_ref[:, 6 * D_BRANCH + col:6 * D_BRANCH + col + HEAD_PAD], 0.0, 0.0)

    tok = lambda w: pl.BlockSpec((TOK_T, w), lambda i: (i, 0))
    head = pl.BlockSpec((N_HEADS, TOK_T, HEAD_PAD), lambda i: (0, i, 0))
    return pl.pallas_call(
        body, name="prep_fwd", grid=(S // TOK_T,),
        in_specs=[tok(N_MAIN), tok(128), pl.BlockSpec((4, 128), lambda i: (0, 0)), tok(128), tok(128), tok(128)],
        out_specs=[head] * 6, out_shape=[shp] * 6,
        compiler_params=_cp("parallel"),
    )(zm, c, qkg, rope_c, rope_a, rope_b)


def _pair(ref, pi, lo_half):
    return jnp.where(lo_half, ref[2 * pi], pltpu.roll(ref[2 * pi + 1], HEAD_DIM, 1))


def _mid_fwd(oa, ob, zm, h0, p, w_out, w_pg, w_ple, g2):
    S = h0.shape[0]
    p, p_lead = p

    def body(oa_ref, ob_ref, ga_ref, gb_ref, h0_ref, p_ref, wo_ref, wg_ref, wp_ref, g2_ref,
             y_ref, h1_ref, h2_ref, u2_ref, e_ref, gate_ref):
        lane = lax.broadcasted_iota(jnp.int32, (TOK_T, HEAD_PAD), 1)
        lo_half = lane < HEAD_DIM
        parts = []
        for o_ref, g_ref in ((oa_ref, ga_ref), (ob_ref, gb_ref)):
            for pi in range(N_HEADS // 2):
                g = g_ref[:, HEAD_PAD * pi:HEAD_PAD * (pi + 1)]
                parts.append((_pair(o_ref, pi, lo_half) * (g * _sigmoid(g))).astype(BF16))
        y = jnp.concatenate(parts, axis=1)
        y_ref[...] = y
        h1 = h0_ref[...] + _dot(y, wo_ref[...])
        h1_ref[...] = h1
        r = lax.rsqrt(jnp.mean(h1 * h1, axis=-1, keepdims=True) + EPS)
        u2 = (h1 * r * g2_ref[...]).astype(BF16)
        u2_ref[...] = u2
        gate = _sigmoid(_dot(u2, wg_ref[...]))
        e = _dot(p_ref[...].astype(BF16), wp_ref[...])
        e_ref[...] = e.astype(BF16)
        gate_ref[...] = gate.astype(BF16)
        h2_ref[...] = h1 + e * gate

    tok = lambda w: pl.BlockSpec((TOK_T, w), lambda i: (i, 0))
    head = pl.BlockSpec((N_HEADS, TOK_T, HEAD_PAD), lambda i: (0, i, 0))
    full = lambda a, b: pl.BlockSpec((a, b), lambda i: (0, 0))
    act = lambda dt: jax.ShapeDtypeStruct((S, D_MODEL), dt)
    return pl.pallas_call(
        body, name="mid_fwd", grid=(S // TOK_T,),
        in_specs=[head, head,
                  pl.BlockSpec((TOK_T, D_BRANCH), lambda i: (i, 3)), pl.BlockSpec((TOK_T, D_BRANCH), lambda i: (i, 7)),
                  tok(D_MODEL), _slab_spec(p_lead, (TOK_T, PLE_DIM), lambda i: (i, 0)), full(D_MODEL, D_MODEL),
                  full(D_MODEL, D_MODEL), full(PLE_DIM, D_MODEL), full(1, D_MODEL)],
        out_specs=[tok(D_MODEL)] * 6,
        out_shape=[act(BF16), act(F32), act(F32), act(BF16), act(BF16), act(BF16)],
        compiler_params=_cp("parallel"),
    )(oa, ob, zm, zm, h0, p, w_out, w_pg, w_ple, g2)


def _loss_fwd_bwd(y, t):
    S = y.shape[0]

    def body(y_ref, t_ref, dy_ref, loss_ref):
        @pl.when(pl.program_id(0) == 0)
        def _():
            loss_ref[...] = jnp.zeros_like(loss_ref)

        err = y_ref[...] - t_ref[...]
        dy_ref[...] = err * (1.0 / D_MODEL)
        part = jnp.sum(jnp.sum(err * err, axis=1, keepdims=True), axis=0, keepdims=True)
        loss_ref[...] += part * (0.5 / D_MODEL)

    tok = pl.BlockSpec((TOK_T, D_MODEL), lambda i: (i, 0))
    return pl.pallas_call(
        body, name="loss", grid=(S // TOK_T,),
        in_specs=[tok, tok], out_specs=[tok, pl.BlockSpec((8, 128), lambda i: (0, 0))],
        out_shape=[jax.ShapeDtypeStruct((S, D_MODEL), F32), jax.ShapeDtypeStruct((8, 128), F32)],
        compiler_params=_cp("arbitrary"),
    )(y, t)


def _bias_tables(kind, stream_len=None):
    T = ATT_T
    *local, (far_window, far_dil) = DILATED_PATTERNS
    nb = {"fox": 1, "stream": 1, "local": local[-1][0] // T + 1}[kind]
    r = lax.broadcasted_iota(jnp.int32, (nb, T, T), 2)
    c = lax.broadcasted_iota(jnp.int32, (nb, T, T), 1)
    b = lax.broadcasted_iota(jnp.int32, (nb, T, T), 0)
    delta = T * b + r - c
    if kind == "fox":
        return jnp.where(delta >= 0, 0.0, NEG).astype(F32)
    if kind == "stream":
        ok = (r // stream_len == c // stream_len) & (delta >= 0) & (delta <= far_window // far_dil)
        return jnp.where(ok, 0.0, NEG).astype(F32)
    mult = jnp.zeros((nb, T, T), F32)
    for window, dil in local:
        ok = (delta >= 0) & (delta <= window) & (delta % dil == 0)
        mult = mult + ok.astype(F32)
    return jnp.where(mult > 0, jnp.log2(jnp.maximum(mult, 1.0)), NEG).astype(F32)


def _call_with_rider(body, name, grid, rider, in_specs, out_specs, out_shape, scratch_shapes, operands):
    if rider is None:
        return pl.pallas_call(body, name=name, grid=grid, in_specs=in_specs, out_specs=out_specs,
                              out_shape=out_shape, scratch_shapes=scratch_shapes,
                              compiler_params=_cp("parallel", "arbitrary"))(*operands)
    n, n_in, n_out = rider.n, len(in_specs), len(out_specs)

    def wrapped(*refs):
        ins, r_ins = refs[:n_in], refs[n_in:n_in + n]
        outs, r_outs = refs[n_in + n:n_in + n + n_out], refs[n_in + n + n_out:n_in + 2 * n + n_out]
        scratch, sems = refs[n_in + 2 * n + n_out:-3], refs[-3:]
        step = [pl.program_id(a) for a in range(len(grid))]

        @pl.when(functools.reduce(jnp.logical_and, [s == 0 for s in step]))
        def _():
            rider.start(r_ins, r_outs, sems)

        body(*ins, *outs, *scratch)

        @pl.when(functools.reduce(jnp.logical_and, [s == g - 1 for s, g in zip(step, grid)]))
        def _():
            rider.wait(r_ins, r_outs, sems)

    return pl.pallas_call(
        wrapped, name=name, grid=grid, in_specs=list(in_specs) + rider.in_specs,
        out_specs=list(out_specs) + rider.out_specs, out_shape=list(out_shape) + rider.out_shapes,
        scratch_shapes=list(scratch_shapes) + rider.scratch_shapes,
        compiler_params=_cp("arbitrary", "arbitrary"))(*operands, *rider.arrays)


def _attn_fwd(q, k, v, table_t, full_range, name, rider=None, merge=None):
    H, S, _ = q.shape
    T = ATT_T
    nb = table_t.shape[0]
    HB = ATT_FWD_HEADS
    KC = ATT_CHUNK
    chunks = [slice(c, c + KC) for c in range(0, T, KC)]
    fold = lambda x, op: functools.reduce(op, [x[r:r + 8] for r in range(0, KC, 8)])

    def body(q_ref, k_ref, v_ref, tab_ref, *rest):
        (o2_ref, lse2_ref), rest = (rest[:2], rest[2:]) if merge else ((None, None), rest)
        o_ref, lse_ref, *scratch = rest
        st_refs, pt_refs, acc_refs = scratch[:HB], scratch[HB:2 * HB], scratch[2 * HB:]
        i = pl.program_id(1)
        rows = lambda j: pl.ds(pl.multiple_of(j * T, T), T)

        def scores(hh, j):
            st_refs[hh][...] = _dot_nt(k_ref[hh, rows(j), :], q_ref[hh])

        def block(j, b, nxt, stats):
            out = []
            for hh, (m, l) in enumerate(stats):
                st_ref, pt_ref, acc_ref = st_refs[hh], pt_refs[hh], acc_refs[hh]
                mx = None
                for ch in chunks:
                    x = st_ref[ch, :]
                    if b is not None:
                        x = x + tab_ref[b, ch, :]
                        st_ref[ch, :] = x
                    x = fold(x, jnp.maximum)
                    mx = x if mx is None else jnp.maximum(mx, x)
                m_new = jnp.maximum(m, jnp.max(mx, axis=0, keepdims=True))
                alpha = jnp.exp2(m - m_new)
                ls = None
                for ch in chunks:
                    pc = jnp.exp2(st_ref[ch, :] - m_new)
                    pt_ref[ch, :] = pc.astype(BF16)
                    pc = fold(pc, jnp.add)
                    ls = pc if ls is None else ls + pc
                if nxt is not None:
                    scores(hh, nxt)
                acc_ref[...] = alpha * acc_ref[...] + _dot_tn(v_ref[hh, rows(j), :], pt_ref[...])
                out.append((m_new, alpha * l + jnp.sum(ls, axis=0, keepdims=True)))
            return tuple(out)

        lo = 0 if full_range else jnp.maximum(i - (nb - 1), 0)
        for hh in range(HB):
            acc_refs[hh][...] = jnp.zeros_like(acc_refs[hh])
            scores(hh, lo)
        stats = lax.fori_loop(lo, i, lambda j, st: block(j, None if full_range else i - j, j + 1, st),
                              ((jnp.full((1, T), NEG, F32), jnp.zeros((1, T), F32)),) * HB)
        stats = block(i, 0, None, stats)
        for hh, (m, l) in enumerate(stats):
            o_t = acc_refs[hh][...] * (1.0 / l)
            lse = m + jnp.log2(l)
            if merge:
                lse2 = lse2_ref[hh, 0]
                top = jnp.maximum(lse, lse2)
                w1, w2 = jnp.exp2(lse - top), jnp.exp2(lse2 - top)
                o_t = (o_t * w1 + o2_ref[hh].T * w2) * (1.0 / (w1 + w2))
                lse = top + jnp.log2(w1 + w2)
            o_ref[hh] = o_t.T
            lse_ref[hh, 0] = lse

    o_spec = pl.BlockSpec((HB, T, HEAD_PAD), lambda h, i: (h, i, 0))
    lse_spec = pl.BlockSpec((HB, 1, 1, T), lambda h, i: (h, i, 0, 0))
    return _call_with_rider(
        body, name, (H // HB, S // T), rider,
        in_specs=[pl.BlockSpec((HB, T, HEAD_PAD), lambda h, i: (h, i, 0)),
                  pl.BlockSpec((HB, S, HEAD_PAD), lambda h, i: (h, 0, 0), pipeline_mode=pl.Buffered(1)),
                  pl.BlockSpec((HB, S, HEAD_PAD), lambda h, i: (h, 0, 0), pipeline_mode=pl.Buffered(1)),
                  pl.BlockSpec((nb, T, T), lambda h, i: (0, 0, 0), pipeline_mode=pl.Buffered(1))]
        + ([o_spec, lse_spec] if merge else []),
        out_specs=[o_spec, lse_spec],
        out_shape=[jax.ShapeDtypeStruct((H, S, HEAD_PAD), F32), jax.ShapeDtypeStruct((H, S // T, 1, T), F32)],
        scratch_shapes=([pltpu.VMEM((T, T), F32)] * HB + [pltpu.VMEM((T, T), BF16)] * HB
                        + [pltpu.VMEM((HEAD_PAD, T), F32)] * HB),
        operands=(q, k, v, table_t) + tuple(merge or ()))


def _attn_bwd(q, k, v, do, lse, dd, table_t, full_range, name, rider=None):
    H, S, _ = q.shape
    T = ATT_T
    nq = S // T
    nb = table_t.shape[0]
    HB = ATT_BWD_HEADS
    KC = ATT_CHUNK
    chunks = [slice(c, c + KC) for c in range(0, T, KC)]

    def body(q_ref, do_ref, lse_ref, dd_ref, k_ref, v_ref, tab_ref, dq_hbm, dk_ref, dv_ref, *scratch):
        st_refs, dpt_refs, pt_refs, dst_refs = (scratch[n * HB:(n + 1) * HB] for n in range(4))
        dq_ref, dq_sem = scratch[4 * HB:]
        h = pl.program_id(0)
        j = pl.program_id(1)

        @pl.when(j == 0)
        def _():
            dq_ref[...] = jnp.zeros_like(dq_ref)

        dk_ref[...] = jnp.zeros_like(dk_ref)
        dv_ref[...] = jnp.zeros_like(dv_ref)

        def step(i, b):
            rows = pl.ds(pl.multiple_of(i * T, T), T)
            for hh in range(HB):
                st_refs[hh][...] = _dot_nt(k_ref[hh], q_ref[hh, rows, :])
                dpt_refs[hh][...] = _dot_nt(v_ref[hh], do_ref[hh, rows, :])
            for hh in range(HB):
                lse_i = lse_ref[hh, i]
                dd_i = dd_ref[hh, i]
                for ch in chunks:
                    x = st_refs[hh][ch, :]
                    if b is not None:
                        x = x + tab_ref[b, ch, :]
                    pc = jnp.exp2(x - lse_i)
                    pt_refs[hh][ch, :] = pc.astype(BF16)
                    dst_refs[hh][ch, :] = (pc * (dpt_refs[hh][ch, :] - dd_i)).astype(BF16)
                dv_ref[hh] += _dot(pt_refs[hh][...], do_ref[hh, rows, :])
                dk_ref[hh] += _dot(dst_refs[hh][...], q_ref[hh, rows, :])
                dq_ref[hh, rows, :] += _dot_tn(dst_refs[hh][...], k_ref[hh])

        step(j, 0)
        if full_range:
            pl.loop(j + 1, nq)(lambda i: step(i, None))
        else:
            pl.loop(j + 1, jnp.minimum(j + nb, nq))(lambda i: step(i, i - j))

        @pl.when(j == nq - 1)
        def _():
            out = pltpu.make_async_copy(dq_ref, dq_hbm.at[pl.ds(h * HB, HB)], dq_sem)
            out.start()
            out.wait()

    once = dict(pipeline_mode=pl.Buffered(1))
    per_head = pl.BlockSpec((HB, S, HEAD_PAD), lambda h, j: (h, 0, 0), **once)
    rows = pl.BlockSpec((HB, nq, 1, T), lambda h, j: (h, 0, 0, 0))
    blk = pl.BlockSpec((HB, T, HEAD_PAD), lambda h, j: (h, j, 0))
    shp = jax.ShapeDtypeStruct((H, S, HEAD_PAD), F32)
    return _call_with_rider(
        body, name, (H // HB, nq), rider,
        in_specs=[per_head, per_head, rows, rows, blk, blk,
                  pl.BlockSpec((nb, T, T), lambda h, j: (0, 0, 0), **once)],
        out_specs=[pl.BlockSpec(memory_space=pltpu.HBM), blk, blk], out_shape=[shp, shp, shp],
        scratch_shapes=([pltpu.VMEM((T, T), F32)] * (2 * HB) + [pltpu.VMEM((T, T), BF16)] * (2 * HB)
                        + [pltpu.VMEM((HB, S, HEAD_PAD), F32), pltpu.SemaphoreType.DMA]),
        operands=(q, do, lse, dd, k, v, table_t))


def _mid_bwd(dh2, h1, e, gate, g2, w_pg, w_out, oa, ob, zm):
    S = dh2.shape[0]

    def body(dh2_ref, h1_ref, e_ref, gate_ref, g2_ref, wg_ref, wo_ref, oa_ref, ob_ref, ga_ref, gb_ref,
             dh1_ref, dh1b_ref, de_ref, dpre_ref, doa_ref, dob_ref, dga_ref, dgb_ref, dd_ref, dg2_ref):
        @pl.when(pl.program_id(0) == 0)
        def _():
            dg2_ref[...] = jnp.zeros_like(dg2_ref)

        lane = lax.broadcasted_iota(jnp.int32, (TOK_T, HEAD_PAD), 1)
        lo_half = lane < HEAD_DIM
        dh2 = dh2_ref[...]
        gate = gate_ref[...]
        de_ref[...] = (dh2 * gate).astype(BF16)
        dpre = (dh2 * e_ref[...] * gate * (1.0 - gate)).astype(BF16)
        dpre_ref[...] = dpre
        du2 = _dot_nt(dpre, wg_ref[...])
        h1 = h1_ref[...]
        r = lax.rsqrt(jnp.mean(h1 * h1, axis=-1, keepdims=True) + EPS)
        xh = h1 * r
        a = du2 * g2_ref[...]
        dh1 = dh2 + r * (a - xh * jnp.mean(a * xh, axis=-1, keepdims=True))
        dg2_ref[...] += jnp.sum(du2 * xh, axis=0, keepdims=True)
        dh1_ref[...] = dh1
        dh1b = dh1.astype(BF16)
        dh1b_ref[...] = dh1b
        dy = _dot_nt(dh1b, wo_ref[...])
        dd = jnp.zeros((TOK_T, HEAD_PAD), F32)
        for bi, (o_ref, g_ref, do_ref, dg_ref) in enumerate(
                ((oa_ref, ga_ref, doa_ref, dga_ref), (ob_ref, gb_ref, dob_ref, dgb_ref))):
            for pi in range(N_HEADS // 2):
                col = bi * D_BRANCH + HEAD_PAD * pi
                dyp = dy[:, col:col + HEAD_PAD]
                g = g_ref[:, HEAD_PAD * pi:HEAD_PAD * (pi + 1)]
                sg = _sigmoid(g)
                dg_ref[:, HEAD_PAD * pi:HEAD_PAD * (pi + 1)] = (
                    dyp * _pair(o_ref, pi, lo_half) * (sg * (1.0 + g * (1.0 - sg)))).astype(BF16)
                dop = dyp * (g * sg)
                for hh, d_head in ((2 * pi, dop), (2 * pi + 1, pltpu.roll(dop, HEAD_DIM, 1))):
                    d_head = jnp.where(lo_half, d_head, 0.0)
                    do_ref[hh] = d_head.astype(BF16)
                    dsum = jnp.sum(d_head * o_ref[hh], axis=1, keepdims=True)
                    dd = dd + jnp.where(lane == bi * N_HEADS + hh, dsum, 0.0)
        dd_ref[...] = dd.T[:2 * N_HEADS, :]

    tok = lambda w: pl.BlockSpec((TOK_T, w), lambda i: (i, 0))
    head = pl.BlockSpec((N_HEADS, TOK_T, HEAD_PAD), lambda i: (0, i, 0))
    full = lambda a, b: pl.BlockSpec((a, b), lambda i: (0, 0))
    act = lambda w, dt: jax.ShapeDtypeStruct((S, w), dt)
    hshape = lambda w, dt: jax.ShapeDtypeStruct((N_HEADS, S, w), dt)
    return pl.pallas_call(
        body, name="mid_bwd", grid=(S // TOK_T,),
        in_specs=[tok(D_MODEL)] * 4 + [full(1, D_MODEL), full(D_MODEL, D_MODEL), full(D_MODEL, D_MODEL), head, head,
                                      pl.BlockSpec((TOK_T, D_BRANCH), lambda i: (i, 3)),
                                      pl.BlockSpec((TOK_T, D_BRANCH), lambda i: (i, 7))],
        out_specs=[tok(D_MODEL)] * 4 + [head, head, tok(D_BRANCH), tok(D_BRANCH),
                                       pl.BlockSpec((2 * N_HEADS, TOK_T), lambda i: (0, i)), full(1, D_MODEL)],
        out_shape=[act(D_MODEL, F32), act(D_MODEL, BF16), act(D_MODEL, BF16), act(D_MODEL, BF16),
                   hshape(HEAD_PAD, BF16), hshape(HEAD_PAD, BF16), act(D_BRANCH, BF16), act(D_BRANCH, BF16),
                   jax.ShapeDtypeStruct((2 * N_HEADS, S), F32), jax.ShapeDtypeStruct((1, D_MODEL), F32)],
        compiler_params=_cp("arbitrary"),
    )(dh2, h1, e, gate, g2, w_pg, w_out, oa, ob, zm, zm)


def _prep_bwd(dqa, dka, dva, dqb, dkb, dvb, zm, qkg, rope_c, rope_a, rope_b, dga, dgb):
    S = zm.shape[0]

    def body(dqa_ref, dka_ref, dva_ref, dqb_ref, dkb_ref, dvb_ref, z_ref, g_ref, rc_ref, ra_ref, rb_ref,
             dga_ref, dgb_ref, dz_ref, dc_ref, dqkg_ref):
        @pl.when(pl.program_id(0) == 0)
        def _():
            dqkg_ref[...] = jnp.zeros_like(dqkg_ref)

        lane = lax.broadcasted_iota(jnp.int32, (TOK_T, HEAD_PAD), 1)
        lo_half = lane < HEAD_DIM
        rc, ra, rb = rc_ref[...], ra_ref[...], rb_ref[...]
        same_head = _same_head()

        def unrope(dy):
            return dy * rc + pltpu.roll(dy * ra, ROPE_HALF, 1) + pltpu.roll(dy * rb, HEAD_PAD - ROPE_HALF, 1)

        def norm_bwd(col, gi, dy):
            x = z_ref[:, col:col + HEAD_PAD]
            r = _pair_rsqrt(x, same_head)
            xh = x * r
            dqkg_ref[gi:gi + 1, :] += jnp.sum(dy * xh, axis=0, keepdims=True)
            a = dy * g_ref[gi:gi + 1, :]
            dz_ref[:, col:col + HEAD_PAD] = (r * (a - xh * _pair_mean(a * xh, same_head))).astype(BF16)

        dc = jnp.zeros((TOK_T, HEAD_PAD), F32)
        for pi in range(N_HEADS // 2):
            col = HEAD_PAD * pi
            norm_bwd(col, 0, _pair(dqa_ref, pi, lo_half) * SCALE)
            norm_bwd(D_BRANCH + col, 1, _pair(dka_ref, pi, lo_half) * LN2)
            dz_ref[:, 2 * D_BRANCH + col:2 * D_BRANCH + col + HEAD_PAD] = _pair(dva_ref, pi, lo_half).astype(BF16)
            norm_bwd(4 * D_BRANCH + col, 2, unrope(_pair(dqb_ref, pi, lo_half) * SCALE))
            norm_bwd(5 * D_BRANCH + col, 3, unrope(_pair(dkb_ref, pi, lo_half) * LN2))
            dz_ref[:, 6 * D_BRANCH + col:6 * D_BRANCH + col + HEAD_PAD] = _pair(dvb_ref, pi, lo_half).astype(BF16)
            for hh in (2 * pi, 2 * pi + 1):
                dch = dka_ref[hh][:, HEAD_DIM:HEAD_DIM + 1] + dqa_ref[hh][:, HEAD_DIM + 3:HEAD_DIM + 4]
                dc = dc + jnp.where(lane == hh, dch, 0.0)
        dz_ref[:, 3 * D_BRANCH:4 * D_BRANCH] = dga_ref[...]
        dz_ref[:, 7 * D_BRANCH:8 * D_BRANCH] = dgb_ref[...]
        dc_ref[...] = dc

    tok = lambda w: pl.BlockSpec((TOK_T, w), lambda i: (i, 0))
    head = pl.BlockSpec((N_HEADS, TOK_T, HEAD_PAD), lambda i: (0, i, 0))
    return pl.pallas_call(
        body, name="prep_bwd", grid=(S // TOK_T,),
        in_specs=[head] * 6 + [tok(N_MAIN), pl.BlockSpec((4, 128), lambda i: (0, 0)), tok(128), tok(128), tok(128),
                               tok(D_BRANCH), tok(D_BRANCH)],
        out_specs=[tok(N_MAIN), tok(128), pl.BlockSpec((4, 128), lambda i: (0, 0))],
        out_shape=[jax.ShapeDtypeStruct((S, N_MAIN), BF16), jax.ShapeDtypeStruct((S, 128), F32),
                   jax.ShapeDtypeStruct((4, 128), F32)],
        compiler_params=_cp("arbitrary"),
    )(dqa, dka, dva, dqb, dkb, dvb, zm, qkg, rope_c, rope_a, rope_b, dga, dgb)


def _inproj_bwd(dzm, dzf, wm, wf, h0, dh1, g):
    S = h0.shape[0]

    def body(dzm_ref, dzf_ref, wm_ref, wf_ref, h_ref, dh1_ref, g_ref, dh0_ref, dg_ref):
        @pl.when(pl.program_id(0) == 0)
        def _():
            dg_ref[...] = jnp.zeros_like(dg_ref)

        du = _dot_nt(dzm_ref[...], wm_ref[...]) + _dot_nt(dzf_ref[...], wf_ref[...])
        x = h_ref[...]
        r = lax.rsqrt(jnp.mean(x * x, axis=-1, keepdims=True) + EPS)
        xh = x * r
        a = du * g_ref[...]
        dh0_ref[...] = dh1_ref[...] + r * (a - xh * jnp.mean(a * xh, axis=-1, keepdims=True))
        dg_ref[...] += jnp.sum(du * xh, axis=0, keepdims=True)

    tok = lambda w: pl.BlockSpec((TOK_T, w), lambda i: (i, 0))
    full = lambda a, b: pl.BlockSpec((a, b), lambda i: (0, 0))
    return pl.pallas_call(
        body, name="inproj_bwd", grid=(S // TOK_T,),
        in_specs=[tok(N_MAIN), tok(128), full(D_MODEL, N_MAIN), full(D_MODEL, 128), tok(D_MODEL), tok(D_MODEL),
                  full(1, D_MODEL)],
        out_specs=[tok(D_MODEL), full(1, D_MODEL)],
        out_shape=[jax.ShapeDtypeStruct((S, D_MODEL), F32), jax.ShapeDtypeStruct((1, D_MODEL), F32)],
        compiler_params=_cp("arbitrary"),
    )(dzm, dzf, wm, wf, h0, dh1, g)


def _wgrad(a, b, name, a_lead=()):
    S, M = a.shape[len(a_lead):]
    N = b.shape[1]
    tn = min(N, 2048)
    ts = 512
    last = S // ts - 1

    def body(a_ref, b_ref, o_ref, acc_ref):
        @pl.when(pl.program_id(1) == 0)
        def _():
            acc_ref[...] = jnp.zeros_like(acc_ref)

        acc_ref[...] += _dot_tn(a_ref[...].astype(BF16), b_ref[...])

        @pl.when(pl.program_id(1) == last)
        def _():
            o_ref[...] = acc_ref[...].astype(BF16)

    return pl.pallas_call(
        body, name=name, grid=(N // tn, S // ts),
        in_specs=[_slab_spec(a_lead, (ts, M), lambda n, s: (s, 0)), pl.BlockSpec((ts, tn), lambda n, s: (s, n))],
        out_specs=pl.BlockSpec((M, tn), lambda n, s: (0, n)),
        out_shape=jax.ShapeDtypeStruct((M, N), BF16),
        scratch_shapes=[pltpu.VMEM((M, tn), F32)],
        compiler_params=_cp("parallel", "arbitrary"),
    )(a, b)


def _rope_tables(positions):
    inv_freq = ROPE_THETA ** (-jnp.arange(ROPE_HALF, dtype=F32) / ROPE_HALF)
    ang = positions.astype(F32)[:, None] * inv_freq
    cos, sin = jnp.cos(ang), jnp.sin(ang)
    S = positions.shape[0]
    one, zero = jnp.ones((S, HEAD_DIM - 2 * ROPE_HALF), F32), jnp.zeros((S, HEAD_DIM - 2 * ROPE_HALF), F32)
    z8 = jnp.zeros((S, ROPE_HALF), F32)
    rc = jnp.concatenate([cos, cos, one], axis=1)
    ra = jnp.concatenate([-sin, z8, zero], axis=1)
    rb = jnp.concatenate([z8, sin, zero], axis=1)
    return tuple(jnp.tile(t, (1, 2)) for t in (rc, ra, rb))


def _in_operands(w_in):
    w_in = w_in.astype(BF16)
    wm = jnp.concatenate([w_in[:, :4 * D_BRANCH], w_in[:, 4 * D_BRANCH + N_HEADS:]], axis=1)
    wf = jnp.pad(w_in[:, 4 * D_BRANCH:4 * D_BRANCH + N_HEADS], ((0, 0), (0, 128 - N_HEADS)))
    return dict(wm=wm, wf=wf)


def _layer_weights(w_in, w_out, w_ple, w_pg):
    return dict(_in_operands(w_in), w_out=w_out.astype(BF16), w_ple=w_ple.astype(BF16), w_pg=w_pg.astype(BF16))


def _row(v, width=128):
    v = v.reshape(1, -1).astype(F32)
    return jnp.pad(v, ((0, 0), (0, width - v.shape[1])))


def _layer_fwd(h0, p, rope, tabs, w, norm_g, b_f, qk_g, ple_g, rider=None, late=None):
    g1 = norm_g.reshape(1, D_MODEL)
    g2 = ple_g.reshape(1, D_MODEL)
    qkg = jnp.tile(qk_g, (1, 2))
    bf = _row(b_f)
    zm, zf, u = _inproj_fwd(h0, g1, w["wm"], w["wf"])
    c = _forget_cumsum(zf, bf, tabs["tril"])
    qa, ka, va, qb, kb, vb = _prep_fwd(zm, c, qkg, *rope)
    oa, lse_a, *arrivals = _attn_fwd(qa, ka, va, tabs["fox"], True, "fox_fwd", rider)
    if late is not None:
        w = {**w, **late(arrivals)}
    qs, ks, vs = _to_streams(qb), _to_streams(kb), _to_streams(vb)
    o_far, lse_far = _attn_fwd(qs, ks, vs, tabs["stream"], False, "dil_far_fwd")
    ob, lse_b = _attn_fwd(qb, kb, vb, tabs["local"], False, "dil_fwd",
                          merge=(_from_streams(o_far), _rows_like(_from_streams, lse_far)))
    y, h1, h2, u2, e, gate = _mid_fwd(oa, ob, zm, h0, p, w["w_out"], w["w_pg"], w["w_ple"], g2)
    saved = dict(h0=h0, p=p, zm=zm, zf=zf, u=u, qa=qa, ka=ka, va=va, qb=qb, kb=kb, vb=vb, qs=qs, ks=ks, vs=vs, oa=oa, ob=ob,
                 lse_a=lse_a, lse_b=lse_b, y=y, h1=h1, u2=u2, e=e, gate=gate, g1=g1, g2=g2, qkg=qkg, bf=bf, w=w)
    return h2, saved, arrivals


def _layer_bwd(dh2, sv, rope, tabs, make_rider=None):
    S = dh2.shape[0]
    nq = S // ATT_T
    w = sv["w"]
    rows = lambda a: a.reshape(N_HEADS, nq, 1, ATT_T)
    (dh1, dh1b, de, dpre, doa, dob, dga, dgb, dd, dg2) = _mid_bwd(
        dh2, sv["h1"], sv["e"], sv["gate"], sv["g2"], w["w_pg"], w["w_out"], sv["oa"], sv["ob"], sv["zm"])
    dda, ddb = dd[:N_HEADS], dd[N_HEADS:]
    early = dict(w_out=_wgrad(sv["y"], dh1b, "wgrad_out"), w_ple=_wgrad(sv["p"][0], de, "wgrad_ple", sv["p"][1]),
                 w_ple_gate=_wgrad(sv["u2"], dpre, "wgrad_gate"))
    rider = None if make_rider is None else make_rider(early)
    dqa, dka, dva, *arrivals = _attn_bwd(sv["qa"], sv["ka"], sv["va"], doa, sv["lse_a"], rows(dda), tabs["fox"],
                                         True, "fox_bwd", rider)
    near = _attn_bwd(sv["qb"], sv["kb"], sv["vb"], dob, sv["lse_b"], rows(ddb), tabs["local"], False, "dil_bwd")
    far = _attn_bwd(sv["qs"], sv["ks"], sv["vs"], _to_streams(dob), _rows_like(_to_streams, sv["lse_b"]),
                    _rows_like(_to_streams, rows(ddb)), tabs["stream"], False, "dil_far_bwd")
    dqb, dkb, dvb = (a + _from_streams(b) for a, b in zip(near, far))
    dzm, dc, dqkg = _prep_bwd(dqa, dka, dva, dqb, dkb, dvb, sv["zm"], sv["qkg"], *rope, dga, dgb)
    dzf, dbf = _forget_bwd(dc, sv["zf"], sv["bf"], tabs["triu"])
    dh0, dg1 = _inproj_bwd(dzm, dzf, w["wm"], w["wf"], sv["h0"], dh1, sv["g1"])
    dwm = _wgrad(sv["u"], dzm, "wgrad_in")
    dwf = _wgrad(sv["u"], dzf, "wgrad_f")
    dw_in = jnp.concatenate([dwm[:, :4 * D_BRANCH], dwf[:, :N_HEADS], dwm[:, 4 * D_BRANCH:]], axis=1)
    grads = dict(norm_g=dg1.reshape(D_MODEL), w_in=dw_in, b_f=dbf[0, :N_HEADS],
                 qk_norm_g=dqkg[:, :HEAD_DIM] + dqkg[:, HEAD_DIM:], ple_norm_g=dg2.reshape(D_MODEL), **early)
    return dh0, grads, arrivals


def _to_streams(a):
    H, S, C = a.shape
    return a.reshape(H, S // FAR_DILATION, FAR_DILATION, C).transpose(0, 2, 1, 3).reshape(H, S, C)


def _from_streams(a):
    H, S, C = a.shape
    return a.reshape(H, FAR_DILATION, S // FAR_DILATION, C).transpose(0, 2, 1, 3).reshape(H, S, C)


def _rows_like(reorder, rows):
    H = rows.shape[0]
    return reorder(rows.reshape(H, -1, 1)).reshape(rows.shape)


def _tables(S):
    T = CUM_T
    stream_len = S // FAR_DILATION
    assert ATT_T % stream_len == 0, "a stream-major block must hold whole streams"
    r = lax.broadcasted_iota(jnp.int32, (T, T), 0)
    c = lax.broadcasted_iota(jnp.int32, (T, T), 1)
    return dict(fox=_bias_tables("fox"), local=_bias_tables("local"), stream=_bias_tables("stream", stream_len),
                tril=(c <= r).astype(BF16), triu=(c >= r).astype(BF16))


def _local_step(x, p, positions, target, layers, small):
    rope = _rope_tables(positions)
    tabs = _tables(x.shape[0])
    ws = [_layer_weights(*lw) for lw in layers]
    h = x
    saved = []
    for w, lp, sm in zip(ws, p, small):
        h, sv, _ = _layer_fwd(h, (lp, ()), rope, tabs, w, *sm)
        saved.append(sv)
    dh, loss = _loss_fwd_bwd(h, target)
    grads = [None] * len(ws)
    for li in reversed(range(len(ws))):
        dh, grads[li], _ = _layer_bwd(dh, saved[li], rope, tabs)
    return loss[0, 0], dh, grads


def _peers():
    x, y, c = lax.axis_index("x"), lax.axis_index("y"), lax.axis_index("c")
    me = 4 * x + 2 * y + c
    flip = lambda v, bit: 1 - v if bit else v
    return me, [(flip(x, k & 4), flip(y, k & 2), flip(c, k & 1)) for k in range(1, N_DEV)]


def _sel(ref, kind, d):
    if kind == "whole":
        return ref
    if kind == "slot":
        return ref.at[d]
    block = pl.ds(pl.multiple_of(d * 128, 128), 128)
    return ref.at[block, :] if kind == "rows" else ref.at[:, block]


class _Pushes:
    def __init__(self, arrays, src_kinds, dst_kinds, out_shapes):
        self.arrays, self.n = list(arrays), len(arrays)
        self.src_kinds, self.dst_kinds = src_kinds, dst_kinds
        self.out_shapes = [jax.ShapeDtypeStruct(s, a.dtype) for s, a in zip(out_shapes, arrays)]
        hbm = pl.BlockSpec(memory_space=pltpu.HBM)
        self.in_specs, self.out_specs = [hbm] * self.n, [hbm] * self.n
        self.scratch_shapes = [pltpu.SemaphoreType.DMA((N_DEV - 1, self.n)),
                               pltpu.SemaphoreType.DMA((N_DEV - 1, self.n)), pltpu.SemaphoreType.DMA((self.n,))]

    def _copies(self, ins, outs, sems):
        send_sems, recv_sems, local_sems = sems
        me, peers = _peers()
        src = lambda a, d: _sel(ins[a], self.src_kinds[a], d)
        dst = lambda a: _sel(outs[a], self.dst_kinds[a], me)
        local = [pltpu.make_async_copy(src(a, me), dst(a), local_sems.at[a]) for a in range(self.n)]
        remote = [pltpu.make_async_remote_copy(
            src_ref=src(a, 4 * px + 2 * py + pc), dst_ref=dst(a), send_sem=send_sems.at[k, a],
            recv_sem=recv_sems.at[k, a], device_id=(px, py, pc), device_id_type=pl.DeviceIdType.MESH)
            for k, (px, py, pc) in enumerate(peers) for a in range(self.n)]
        return local + remote

    def start(self, ins, outs, sems):
        for cp in self._copies(ins, outs, sems):
            cp.start()

    def wait(self, ins, outs, sems):
        for cp in self._copies(ins, outs, sems):
            cp.wait()


def _exchange(name, pushes):
    n = pushes.n

    def body(*refs):
        pushes.start(refs[:n], refs[n:2 * n], refs[2 * n:])
        pushes.wait(refs[:n], refs[n:2 * n], refs[2 * n:])

    return pl.pallas_call(body, name=name, in_specs=pushes.in_specs, out_specs=pushes.out_specs,
                          out_shape=pushes.out_shapes, scratch_shapes=pushes.scratch_shapes)(*pushes.arrays)


def _gather_pushes(shards, kinds):
    full = {"slot": lambda s: (N_DEV,) + s, "rows": lambda s: (N_DEV * s[0], s[1]),
            "cols": lambda s: (s[0], N_DEV * s[1])}
    return _Pushes(shards, ["whole"] * len(shards), kinds, [full[k](a.shape) for a, k in zip(shards, kinds)])


def _scatter_pushes(partials, kinds):
    part = {"slot": lambda s: s[1:], "rows": lambda s: (128, s[1]), "cols": lambda s: (s[0], 128),
            "whole": lambda s: s}
    return _Pushes(partials, kinds, ["slot"] * len(partials),
                   [(N_DEV,) + part[k](a.shape) for a, k in zip(partials, kinds)])


def _adamw(name, parts, w, m, v, rows):
    L, R, C = w.shape

    def body(p_ref, w_ref, m_ref, v_ref, g_ref, d_ref, nm_ref, nv_ref):
        g = p_ref[0, 0].astype(F32)
        for s in range(1, N_DEV):
            g = g + p_ref[s, 0].astype(F32)
        g_ref[0] = g
        nm = ADAM_B1 * m_ref[0] + (1.0 - ADAM_B1) * g
        nv = ADAM_B2 * v_ref[0] + (1.0 - ADAM_B2) * (g * g)
        nm_ref[0] = nm
        nv_ref[0] = nv
        m_hat = nm / (1.0 - ADAM_B1 ** ADAM_STEP)
        v_hat = nv / (1.0 - ADAM_B2 ** ADAM_STEP)
        d_ref[0] = -ADAM_LR * (m_hat / (jnp.sqrt(v_hat) + ADAM_EPS) + ADAM_WD * w_ref[0])

    blk = pl.BlockSpec((1, rows, C), lambda l, i: (l, i, 0))
    shp = jax.ShapeDtypeStruct((L, R, C), F32)
    return pl.pallas_call(
        body, name=name, grid=(L, R // rows),
        in_specs=[pl.BlockSpec((N_DEV, 1, rows, C), lambda l, i: (0, l, i, 0)), blk, blk, blk],
        out_specs=[blk] * 4, out_shape=[shp] * 4,
        compiler_params=_cp("parallel", "parallel"),
    )(parts, w, m, v)


SMALL_ROWS = 40
LOSS_ROW = 37


def _pack_small(norm_g, ple_g, qk_g, b_f, last_row):
    rows = lambda a: a.astype(F32).reshape(-1, 128)
    flat = jnp.concatenate([rows(norm_g), rows(ple_g), rows(qk_g), _row(b_f.reshape(-1)), last_row], axis=0)
    return jnp.pad(flat, ((0, SMALL_ROWS - flat.shape[0]), (0, 0)))


def _unpack_small(flat):
    return (flat[0:16].reshape(2, D_MODEL), flat[16:32].reshape(2, D_MODEL), flat[32:36].reshape(2, 4, HEAD_DIM),
            flat[36, :2 * N_HEADS].reshape(2, N_HEADS))


def kernel(x, p, positions, norm_g, w_in, b_f, qk_norm_g, w_out, w_ple, ple_norm_g, w_ple_gate, loss_target, m_norm_g, m_w_in, m_b_f, m_qk_norm_g, m_w_out, m_w_ple, m_ple_norm_g, m_w_ple_gate, v_norm_g, v_w_in, v_b_f, v_qk_norm_g, v_w_out, v_w_ple, v_ple_norm_g, v_w_ple_gate):
    bf16 = lambda a: a.astype(BF16)
    rows_in = W_IN_ROWS // 2
    flat_in = lambda a: bf16(a).reshape(rows_in, 128)
    full_in = lambda g: g.reshape(N_DEV, D_MODEL, W_IN_SHARD).transpose(1, 0, 2).reshape(D_MODEL, N_IN)
    small = [(norm_g[l], b_f[l], qk_norm_g[l], ple_norm_g[l]) for l in range(2)]
    rope = _rope_tables(positions[0])
    tabs = _tables(x.shape[1])

    (g_in0,) = _exchange("gather_first", _gather_pushes([flat_in(w_in[0])], ["slot"]))
    rest = _gather_pushes([flat_in(w_in[1])] + [bf16(a[l]) for l in range(2) for a in (w_out, w_ple, w_ple_gate)],
                          ["slot"] + ["rows", "cols", "rows"] * 2)
    late = lambda got: dict(w_out=got[1], w_ple=got[2], w_pg=got[3])
    h, sv0, got = _layer_fwd(x[0], (p, (0, 0)), rope, tabs, _in_operands(full_in(g_in0)), *small[0], rest, late)
    w1 = dict(_in_operands(full_in(got[0])), w_out=got[4], w_ple=got[5], w_pg=got[6])
    h, sv1, _ = _layer_fwd(h, (p, (1, 0)), rope, tabs, w1, *small[1])
    dh, loss = _loss_fwd_bwd(h, loss_target[0])
    dh, gr1, _ = _layer_bwd(dh, sv1, rope, tabs)

    by_dest = lambda d: d.reshape(D_MODEL, N_DEV, W_IN_SHARD).transpose(1, 0, 2).reshape(N_DEV, rows_in, 128)
    big = ("w_out", "w_ple", "w_ple_gate")
    riding = lambda early: _scatter_pushes([by_dest(gr1["w_in"])] + [gr1[n] for n in big] + [early[n] for n in big],
                                           ["slot"] + ["rows", "cols", "rows"] * 2)
    dx, gr0, (r_in1, *r_big) = _layer_bwd(dh, sv0, rope, tabs, riding)
    grads = (gr0, gr1)
    stack = lambda name: jnp.stack([gl[name] for gl in grads], axis=0)
    small_part = _pack_small(stack("norm_g"), stack("ple_norm_g"), stack("qk_norm_g"), stack("b_f"),
                             _row(loss[0, 0].reshape(1)))
    r_in0, r_small = _exchange("exchange_last", _scatter_pushes([by_dest(gr0["w_in"]), small_part], ["slot", "whole"]))
    r_in = jnp.concatenate([r_in0, r_in1], axis=1)
    r_out, r_ple, r_pg = (jnp.stack([r_big[3 + k], r_big[k]], axis=1) for k in range(3))

    zero_row = jnp.zeros((1, 128), F32)
    small_of = lambda ng, pg, qk, bf: _pack_small(ng, pg, qk, bf, zero_row)[None]
    flat = lambda a: a.reshape(1, W_IN_ROWS, 128)
    outs = dict(
        w_in=[o.reshape(w_in.shape) for o in
              _adamw("adamw_in", r_in[:, None], flat(w_in), flat(m_w_in), flat(v_w_in), W_IN_TILE)],
        w_out=_adamw("adamw_out", r_out, w_out, m_w_out, v_w_out, 128),
        w_ple=_adamw("adamw_ple", r_ple, w_ple, m_w_ple, v_w_ple, 256),
        w_pg=_adamw("adamw_gate", r_pg, w_ple_gate, m_w_ple_gate, v_w_ple_gate, 128),
        small=_adamw("adamw_small", r_small[:, None], small_of(norm_g, ple_norm_g, qk_norm_g, b_f),
                     small_of(m_norm_g, m_ple_norm_g, m_qk_norm_g, m_b_f),
                     small_of(v_norm_g, v_ple_norm_g, v_qk_norm_g, v_b_f), SMALL_ROWS))
    leaves = []
    for kind in range(4):
        ng, pg, qk, bf = _unpack_small(outs["small"][kind][0])
        leaves += [ng, outs["w_in"][kind], bf, qk, outs["w_out"][kind], outs["w_ple"][kind], pg, outs["w_pg"][kind]]
    return (outs["small"][0][0, LOSS_ROW, 0], dx[None], *leaves)
```

```python
import functools

import jax
import jax.numpy as jnp
from jax import lax
from jax.experimental import pallas as pl
from jax.experimental.pallas import tpu as pltpu

F32 = jnp.float32
BF16 = jnp.bfloat16

D_MODEL = 1024
HEAD_DIM = 64
N_HEADS = 8
HEAD_PAD = 128
D_BRANCH = N_HEADS * HEAD_DIM
N_MAIN = 8 * D_BRANCH
N_IN = N_MAIN + N_HEADS
PLE_DIM = 256
ROPE_THETA = 500000.0
ROPE_HALF = 8
EPS = 1e-6
NEG = -1e30
SCALE = HEAD_DIM ** -0.5
LOG2E = 1.4426950408889634
LN2 = 0.6931471805599453
DILATED_PATTERNS = ((128, 1), (512, 4), (2048, 16))
N_DEV = 8
W_IN_SHARD = N_IN // N_DEV
W_IN_ROWS = 2 * D_MODEL * W_IN_SHARD // 128
W_IN_TILE = W_IN_ROWS // 19

ADAM_LR = 0.001
ADAM_B1 = 0.9
ADAM_B2 = 0.999
ADAM_EPS = 1e-08
ADAM_WD = 0.01
ADAM_STEP = 10

ATT_T = 512
ATT_FWD_HEADS = 4
ATT_BWD_HEADS = 2
ATT_CHUNK = 32
TOK_T = 256
CUM_T = 512
VMEM_LIMIT = 56 * 1024 * 1024


def _slab_spec(lead, block, index):
    return pl.BlockSpec((None,) * len(lead) + block, lambda *g: (*lead, *index(*g)))


def _cp(*sem):
    return pltpu.CompilerParams(dimension_semantics=sem, vmem_limit_bytes=VMEM_LIMIT)


def _sigmoid(x):
    return 1.0 / (1.0 + jnp.exp(-x))


def _split3(x):
    hi = x.astype(BF16)
    r1 = x - hi.astype(F32)
    mid = r1.astype(BF16)
    lo = (r1 - mid.astype(F32)).astype(BF16)
    return hi, mid, lo


def _dot(a, b):
    return jnp.dot(a, b, preferred_element_type=F32)


def _dot_nt(a, b):
    return lax.dot_general(a, b, (((1,), (1,)), ((), ())), preferred_element_type=F32)


def _dot_tn(a, b):
    return lax.dot_general(a, b, (((0,), (0,)), ((), ())), preferred_element_type=F32)


def _inproj_fwd(h, g, wm, wf):
    S = h.shape[0]

    def body(h_ref, g_ref, wm_ref, wf_ref, zm_ref, zf_ref, u_ref):
        x = h_ref[...]
        r = lax.rsqrt(jnp.mean(x * x, axis=-1, keepdims=True) + EPS)
        u = (x * r * g_ref[...]).astype(BF16)
        u_ref[...] = u
        zm_ref[...] = _dot(u, wm_ref[...])
        zf_ref[...] = _dot(u, wf_ref[...])

    return pl.pallas_call(
        body, name="inproj_fwd", grid=(S // TOK_T,),
        in_specs=[pl.BlockSpec((TOK_T, D_MODEL), lambda i: (i, 0)),
                  pl.BlockSpec((1, D_MODEL), lambda i: (0, 0)),
                  pl.BlockSpec((D_MODEL, N_MAIN), lambda i: (0, 0)),
                  pl.BlockSpec((D_MODEL, 128), lambda i: (0, 0))],
        out_specs=[pl.BlockSpec((TOK_T, N_MAIN), lambda i: (i, 0)),
                   pl.BlockSpec((TOK_T, 128), lambda i: (i, 0)),
                   pl.BlockSpec((TOK_T, D_MODEL), lambda i: (i, 0))],
        out_shape=[jax.ShapeDtypeStruct((S, N_MAIN), F32), jax.ShapeDtypeStruct((S, 128), F32),
                   jax.ShapeDtypeStruct((S, D_MODEL), BF16)],
        compiler_params=_cp("parallel"),
    )(h, g, wm, wf)


def _log_sigmoid(x):
    return jnp.minimum(x, 0.0) - jnp.log(1.0 + jnp.exp(-jnp.abs(x)))


def _forget_cumsum(zf, bf, tri):
    S = zf.shape[0]

    def body(zf_ref, b_ref, tri_ref, c_ref, carry):
        @pl.when(pl.program_id(0) == 0)
        def _():
            carry[...] = jnp.zeros_like(carry)

        lf = _log_sigmoid(zf_ref[...] + b_ref[...])
        hi, mid, lo = _split3(lf)
        t = tri_ref[...]
        cs = _dot(t, hi) + _dot(t, mid) + _dot(t, lo) + carry[...]
        c_ref[...] = cs
        carry[...] = cs[CUM_T - 1:CUM_T, :]

    return pl.pallas_call(
        body, name="forget_cumsum", grid=(S // CUM_T,),
        in_specs=[pl.BlockSpec((CUM_T, 128), lambda i: (i, 0)),
                  pl.BlockSpec((1, 128), lambda i: (0, 0)),
                  pl.BlockSpec((CUM_T, CUM_T), lambda i: (0, 0))],
        out_specs=pl.BlockSpec((CUM_T, 128), lambda i: (i, 0)),
        out_shape=jax.ShapeDtypeStruct((S, 128), F32),
        scratch_shapes=[pltpu.VMEM((1, 128), F32)],
        compiler_params=_cp("arbitrary"),
    )(zf, bf, tri)


def _forget_bwd(dc, zf, bf, triu):
    S = zf.shape[0]
    n = S // CUM_T

    def body(dc_ref, zf_ref, b_ref, tri_ref, dzf_ref, db_ref, carry):
        @pl.when(pl.program_id(0) == 0)
        def _():
            carry[...] = jnp.zeros_like(carry)
            db_ref[...] = jnp.zeros_like(db_ref)

        hi, mid, lo = _split3(dc_ref[...])
        t = tri_ref[...]
        dlf = _dot(t, hi) + _dot(t, mid) + _dot(t, lo) + carry[...]
        carry[...] = dlf[0:1, :]
        dfa = dlf * (1.0 - _sigmoid(zf_ref[...] + b_ref[...]))
        dzf_ref[...] = dfa.astype(BF16)
        db_ref[...] += jnp.sum(dfa, axis=0, keepdims=True)

    return pl.pallas_call(
        body, name="forget_bwd", grid=(n,),
        in_specs=[pl.BlockSpec((CUM_T, 128), lambda i: (n - 1 - i, 0)),
                  pl.BlockSpec((CUM_T, 128), lambda i: (n - 1 - i, 0)),
                  pl.BlockSpec((1, 128), lambda i: (0, 0)),
                  pl.BlockSpec((CUM_T, CUM_T), lambda i: (0, 0))],
        out_specs=[pl.BlockSpec((CUM_T, 128), lambda i: (n - 1 - i, 0)),
                   pl.BlockSpec((1, 128), lambda i: (0, 0))],
        out_shape=[jax.ShapeDtypeStruct((S, 128), BF16), jax.ShapeDtypeStruct((1, 128), F32)],
        scratch_shapes=[pltpu.VMEM((1, 128), F32)],
        compiler_params=_cp("arbitrary"),
    )(dc, zf, bf, triu)


def _same_head():
    r = lax.broadcasted_iota(jnp.int32, (HEAD_PAD, HEAD_PAD), 0) // HEAD_DIM
    c = lax.broadcasted_iota(jnp.int32, (HEAD_PAD, HEAD_PAD), 1) // HEAD_DIM
    return (r == c).astype(BF16)


def _pair_mean(x, same_head):
    hi = x.astype(BF16)
    lo = (x - hi.astype(F32)).astype(BF16)
    return (_dot(hi, same_head) + _dot(lo, same_head)) * (1.0 / HEAD_DIM)


def _pair_rsqrt(x, same_head):
    return lax.rsqrt(_pair_mean(x * x, same_head) + EPS)


def _prep_fwd(zm, c, qkg, rope_c, rope_a, rope_b):
    S = zm.shape[0]
    shp = jax.ShapeDtypeStruct((N_HEADS, S, HEAD_PAD), BF16)

    def body(z_ref, c_ref, g_ref, rc_ref, ra_ref, rb_ref, qa_ref, ka_ref, va_ref, qb_ref, kb_ref, vb_ref):
        lane = lax.broadcasted_iota(jnp.int32, (TOK_T, HEAD_PAD), 1)
        lo_half = lane < HEAD_DIM
        aug = (lane >= HEAD_DIM) & (lane < HEAD_DIM + 3)
        q_pad = jnp.where(aug, -1.0, 0.0)
        cs = c_ref[...]
        rc, ra, rb = rc_ref[...], ra_ref[...], rb_ref[...]
        same_head = _same_head()

        def norm(col, gi):
            x = z_ref[:, col:col + HEAD_PAD]
            return x * _pair_rsqrt(x, same_head) * g_ref[gi:gi + 1, :]

        def rope(y):
            return y * rc + pltpu.roll(y, HEAD_PAD - ROPE_HALF, 1) * ra + pltpu.roll(y, ROPE_HALF, 1) * rb

        def put(ref, pi, y, pad_even, pad_odd):
            ref[2 * pi] = jnp.where(lo_half, y, pad_even).astype(BF16)
            ref[2 * pi + 1] = jnp.where(lo_half, pltpu.roll(y, HEAD_DIM, 1), pad_odd).astype(BF16)

        def k_pad(h):
            ch = cs[:, h:h + 1] * LOG2E
            hi = ch.astype(BF16).astype(F32)
            mid = (ch - hi).astype(BF16).astype(F32)
            lo = ch - hi - mid
            ones = jnp.where(lane == HEAD_DIM + 3, 1.0, 0.0)
            return jnp.where(lane == HEAD_DIM, hi, jnp.where(lane == HEAD_DIM + 1, mid,
                                                              jnp.where(lane == HEAD_DIM + 2, lo, ones)))

        for pi in range(N_HEADS // 2):
            col = HEAD_PAD * pi
            put(qa_ref, pi, norm(col, 0) * (SCALE * LOG2E), q_pad, q_pad)
            put(ka_ref, pi, norm(D_BRANCH + col, 1), k_pad(2 * pi), k_pad(2 * pi + 1))
            put(va_ref, pi, z_ref[:, 2 * D_BRANCH + col:2 * D_BRANCH + col + HEAD_PAD], 0.0, 0.0)
            put(qb_ref, pi, rope(norm(4 * D_BRANCH + col, 2)) * (SCALE * LOG2E), 0.0, 0.0)
            put(kb_ref, pi, rope(norm(5 * D_BRANCH + col, 3)), 0.0, 0.0)
            put(vb_ref, pi, z_ref[:, 6 * D_BRANCH + col:6 * D_BRANCH + col + HEAD_PAD], 0.0, 0.0)

    tok = lambda w: pl.BlockSpec((TOK_T, w), lambda i: (i, 0))
    head = pl.BlockSpec((N_HEADS, TOK_T, HEAD_PAD), lambda i: (0, i, 0))
    return pl.pallas_call(
        body, name="prep_fwd", grid=(S // TOK_T,),
        in_specs=[tok(N_MAIN), tok(128), pl.BlockSpec((4, 128), lambda i: (0, 0)), tok(128), tok(128), tok(128)],
        out_specs=[head] * 6, out_shape=[shp] * 6,
        compiler_params=_cp("parallel"),
    )(zm, c, qkg, rope_c, rope_a, rope_b)


def _pair(ref, pi, lo_half):
    return jnp.where(lo_half, ref[2 * pi], pltpu.roll(ref[2 * pi + 1], HEAD_DIM, 1))


def _mid_fwd(oa, ob, zm, h0, p, w_out, w_pg, w_ple, g2):
    S = h0.shape[0]
    p, p_lead = p

    def body(oa_ref, ob_ref, ga_ref, gb_ref, h0_ref, p_ref, wo_ref, wg_ref, wp_ref, g2_ref,
             y_ref, h1_ref, h2_ref, u2_ref, e_ref, gate_ref):
        lane = lax.broadcasted_iota(jnp.int32, (TOK_T, HEAD_PAD), 1)
        lo_half = lane < HEAD_DIM
        parts = []
        for o_ref, g_ref in ((oa_ref, ga_ref), (ob_ref, gb_ref)):
            for pi in range(N_HEADS // 2):
                g = g_ref[:, HEAD_PAD * pi:HEAD_PAD * (pi + 1)]
                parts.append((_pair(o_ref, pi, lo_half) * (g * _sigmoid(g))).astype(BF16))
        y = jnp.concatenate(parts, axis=1)
        y_ref[...] = y
        h1 = h0_ref[...] + _dot(y, wo_ref[...])
        h1_ref[...] = h1
        r = lax.rsqrt(jnp.mean(h1 * h1, axis=-1, keepdims=True) + EPS)
        u2 = (h1 * r * g2_ref[...]).astype(BF16)
        u2_ref[...] = u2
        gate = _sigmoid(_dot(u2, wg_ref[...]))
        e = _dot(p_ref[...].astype(BF16), wp_ref[...])
        e_ref[...] = e.astype(BF16)
        gate_ref[...] = gate.astype(BF16)
        h2_ref[...] = h1 + e * gate

    tok = lambda w: pl.BlockSpec((TOK_T, w), lambda i: (i, 0))
    head = pl.BlockSpec((N_HEADS, TOK_T, HEAD_PAD), lambda i: (0, i, 0))
    full = lambda a, b: pl.BlockSpec((a, b), lambda i: (0, 0))
    act = lambda dt: jax.ShapeDtypeStruct((S, D_MODEL), dt)
    return pl.pallas_call(
        body, name="mid_fwd", grid=(S // TOK_T,),
        in_specs=[head, head,
                  pl.BlockSpec((TOK_T, D_BRANCH), lambda i: (i, 3)), pl.BlockSpec((TOK_T, D_BRANCH), lambda i: (i, 7)),
                  tok(D_MODEL), _slab_spec(p_lead, (TOK_T, PLE_DIM), lambda i: (i, 0)), full(D_MODEL, D_MODEL),
                  full(D_MODEL, D_MODEL), full(PLE_DIM, D_MODEL), full(1, D_MODEL)],
        out_specs=[tok(D_MODEL)] * 6,
        out_shape=[act(BF16), act(F32), act(F32), act(BF16), act(BF16), act(BF16)],
        compiler_params=_cp("parallel"),
    )(oa, ob, zm, zm, h0, p, w_out, w_pg, w_ple, g2)


def _loss_fwd_bwd(y, t):
    S = y.shape[0]

    def body(y_ref, t_ref, dy_ref, loss_ref):
        @pl.when(pl.program_id(0) == 0)
        def _():
            loss_ref[...] = jnp.zeros_like(loss_ref)

        err = y_ref[...] - t_ref[...]
        dy_ref[...] = err * (1.0 / D_MODEL)
        part = jnp.sum(jnp.sum(err * err, axis=1, keepdims=True), axis=0, keepdims=True)
        loss_ref[...] += part * (0.5 / D_MODEL)

    tok = pl.BlockSpec((TOK_T, D_MODEL), lambda i: (i, 0))
    return pl.pallas_call(
        body, name="loss", grid=(S // TOK_T,),
        in_specs=[tok, tok], out_specs=[tok, pl.BlockSpec((8, 128), lambda i: (0, 0))],
        out_shape=[jax.ShapeDtypeStruct((S, D_MODEL), F32), jax.ShapeDtypeStruct((8, 128), F32)],
        compiler_params=_cp("arbitrary"),
    )(y, t)


def _bias_tables(full_range):
    T = ATT_T
    nb = 1 if full_range else DILATED_PATTERNS[-1][0] // T + 1
    r = lax.broadcasted_iota(jnp.int32, (nb, T, T), 2)
    c = lax.broadcasted_iota(jnp.int32, (nb, T, T), 1)
    b = lax.broadcasted_iota(jnp.int32, (nb, T, T), 0)
    delta = T * b + r - c
    if full_range:
        bias = jnp.where(delta >= 0, 0.0, NEG).astype(F32)
    else:
        mult = jnp.zeros((nb, T, T), F32)
        for window, dil in DILATED_PATTERNS:
            ok = (delta >= 0) & (delta <= window) & (delta % dil == 0)
            mult = mult + ok.astype(F32)
        bias = jnp.where(mult > 0, jnp.log2(jnp.maximum(mult, 1.0)), NEG).astype(F32)
    return bias


def _call_with_rider(body, name, grid, rider, in_specs, out_specs, out_shape, scratch_shapes, operands,
                     semantics=("parallel", "arbitrary")):
    if rider is None:
        return pl.pallas_call(body, name=name, grid=grid, in_specs=in_specs, out_specs=out_specs,
                              out_shape=out_shape, scratch_shapes=scratch_shapes,
                              compiler_params=_cp(*semantics))(*operands)
    n, n_in, n_out = rider.n, len(in_specs), len(out_specs)

    def wrapped(*refs):
        ins, r_ins = refs[:n_in], refs[n_in:n_in + n]
        outs, r_outs = refs[n_in + n:n_in + n + n_out], refs[n_in + n + n_out:n_in + 2 * n + n_out]
        scratch, sems = refs[n_in + 2 * n + n_out:-3], refs[-3:]
        step = [pl.program_id(a) for a in range(len(grid))]

        @pl.when(functools.reduce(jnp.logical_and, [s == 0 for s in step]))
        def _():
            rider.start(r_ins, r_outs, sems)

        body(*ins, *outs, *scratch)

        @pl.when(functools.reduce(jnp.logical_and, [s == g - 1 for s, g in zip(step, grid)]))
        def _():
            rider.wait(r_ins, r_outs, sems)

    return pl.pallas_call(
        wrapped, name=name, grid=grid, in_specs=list(in_specs) + rider.in_specs,
        out_specs=list(out_specs) + rider.out_specs, out_shape=list(out_shape) + rider.out_shapes,
        scratch_shapes=list(scratch_shapes) + rider.scratch_shapes,
        compiler_params=_cp(*["arbitrary"] * len(grid)))(*operands, *rider.arrays)


def _attn_fwd(q, k, v, table_t, full_range, name, rider=None):
    H, S, _ = q.shape
    T = ATT_T
    nb = table_t.shape[0]
    HB = ATT_FWD_HEADS
    KC = ATT_CHUNK
    chunks = [slice(c, c + KC) for c in range(0, T, KC)]
    fold = lambda x, op: functools.reduce(op, [x[r:r + 8] for r in range(0, KC, 8)])

    def body(q_ref, k_ref, v_ref, tab_ref, o_ref, lse_ref, *scratch):
        st_refs, pt_refs, acc_refs = scratch[:HB], scratch[HB:2 * HB], scratch[2 * HB:]
        i = pl.program_id(1)
        rows = lambda j: pl.ds(pl.multiple_of(j * T, T), T)

        def scores(hh, j):
            st_refs[hh][...] = _dot_nt(k_ref[hh, rows(j), :], q_ref[hh])

        def block(j, b, nxt, stats):
            out = []
            for hh, (m, l) in enumerate(stats):
                st_ref, pt_ref, acc_ref = st_refs[hh], pt_refs[hh], acc_refs[hh]
                mx = None
                for ch in chunks:
                    x = st_ref[ch, :]
                    if b is not None:
                        x = x + tab_ref[b, ch, :]
                        st_ref[ch, :] = x
                    x = fold(x, jnp.maximum)
                    mx = x if mx is None else jnp.maximum(mx, x)
                m_new = jnp.maximum(m, jnp.max(mx, axis=0, keepdims=True))
                alpha = jnp.exp2(m - m_new)
                ls = None
                for ch in chunks:
                    pc = jnp.exp2(st_ref[ch, :] - m_new)
                    pt_ref[ch, :] = pc.astype(BF16)
                    pc = fold(pc, jnp.add)
                    ls = pc if ls is None else ls + pc
                if nxt is not None:
                    scores(hh, nxt)
                acc_ref[...] = alpha * acc_ref[...] + _dot_tn(v_ref[hh, rows(j), :], pt_ref[...])
                out.append((m_new, alpha * l + jnp.sum(ls, axis=0, keepdims=True)))
            return tuple(out)

        lo = 0 if full_range else jnp.maximum(i - (nb - 1), 0)
        for hh in range(HB):
            acc_refs[hh][...] = jnp.zeros_like(acc_refs[hh])
            scores(hh, lo)
        stats = lax.fori_loop(lo, i, lambda j, st: block(j, None if full_range else i - j, j + 1, st),
                              ((jnp.full((1, T), NEG, F32), jnp.zeros((1, T), F32)),) * HB)
        stats = block(i, 0, None, stats)
        for hh, (m, l) in enumerate(stats):
            o_ref[hh] = (acc_refs[hh][...] * (1.0 / l)).T
            lse_ref[hh, 0] = m + jnp.log2(l)

    return _call_with_rider(
        body, name, (H // HB, S // T), rider,
        in_specs=[pl.BlockSpec((HB, T, HEAD_PAD), lambda h, i: (h, i, 0)),
                  pl.BlockSpec((HB, S, HEAD_PAD), lambda h, i: (h, 0, 0), pipeline_mode=pl.Buffered(1)),
                  pl.BlockSpec((HB, S, HEAD_PAD), lambda h, i: (h, 0, 0), pipeline_mode=pl.Buffered(1)),
                  pl.BlockSpec((nb, T, T), lambda h, i: (0, 0, 0), pipeline_mode=pl.Buffered(1))],
        out_specs=[pl.BlockSpec((HB, T, HEAD_PAD), lambda h, i: (h, i, 0)),
                   pl.BlockSpec((HB, 1, 1, T), lambda h, i: (h, i, 0, 0))],
        out_shape=[jax.ShapeDtypeStruct((H, S, HEAD_PAD), F32), jax.ShapeDtypeStruct((H, S // T, 1, T), F32)],
        scratch_shapes=([pltpu.VMEM((T, T), F32)] * HB + [pltpu.VMEM((T, T), BF16)] * HB
                        + [pltpu.VMEM((HEAD_PAD, T), F32)] * HB),
        operands=(q, k, v, table_t))


def _attn_bwd(q, k, v, do, lse, dd, table_t, full_range, name, rider=None):
    H, S, _ = q.shape
    T = ATT_T
    nq = S // T
    nb = table_t.shape[0]
    HB = ATT_BWD_HEADS
    KC = ATT_CHUNK
    chunks = [slice(c, c + KC) for c in range(0, T, KC)]

    def body(q_ref, do_ref, lse_ref, dd_ref, k_ref, v_ref, tab_ref, dq_hbm, dk_ref, dv_ref, *scratch):
        st_refs, dpt_refs, pt_refs, dst_refs = (scratch[n * HB:(n + 1) * HB] for n in range(4))
        dq_ref, dq_sem = scratch[4 * HB:]
        h = pl.program_id(0)
        j = pl.program_id(1)

        @pl.when(j == 0)
        def _():
            dq_ref[...] = jnp.zeros_like(dq_ref)

        dk_ref[...] = jnp.zeros_like(dk_ref)
        dv_ref[...] = jnp.zeros_like(dv_ref)

        def step(i, b):
            rows = pl.ds(pl.multiple_of(i * T, T), T)
            for hh in range(HB):
                st_refs[hh][...] = _dot_nt(k_ref[hh], q_ref[hh, rows, :])
                dpt_refs[hh][...] = _dot_nt(v_ref[hh], do_ref[hh, rows, :])
            for hh in range(HB):
                lse_i = lse_ref[hh, i]
                dd_i = dd_ref[hh, i]
                for ch in chunks:
                    x = st_refs[hh][ch, :]
                    if b is not None:
                        x = x + tab_ref[b, ch, :]
                    pc = jnp.exp2(x - lse_i)
                    pt_refs[hh][ch, :] = pc.astype(BF16)
                    dst_refs[hh][ch, :] = (pc * (dpt_refs[hh][ch, :] - dd_i)).astype(BF16)
                dv_ref[hh] += _dot(pt_refs[hh][...], do_ref[hh, rows, :])
                dk_ref[hh] += _dot(dst_refs[hh][...], q_ref[hh, rows, :])
                dq_ref[hh, rows, :] += _dot_tn(dst_refs[hh][...], k_ref[hh])

        step(j, 0)
        if full_range:
            pl.loop(j + 1, nq)(lambda i: step(i, None))
        else:
            pl.loop(j + 1, jnp.minimum(j + nb, nq))(lambda i: step(i, i - j))

        @pl.when(j == nq - 1)
        def _():
            out = pltpu.make_async_copy(dq_ref, dq_hbm.at[pl.ds(h * HB, HB)], dq_sem)
            out.start()
            out.wait()

    once = dict(pipeline_mode=pl.Buffered(1))
    per_head = pl.BlockSpec((HB, S, HEAD_PAD), lambda h, j: (h, 0, 0), **once)
    rows = pl.BlockSpec((HB, nq, 1, T), lambda h, j: (h, 0, 0, 0))
    blk = pl.BlockSpec((HB, T, HEAD_PAD), lambda h, j: (h, j, 0))
    shp = jax.ShapeDtypeStruct((H, S, HEAD_PAD), F32)
    return _call_with_rider(
        body, name, (H // HB, nq), rider,
        in_specs=[per_head, per_head, rows, rows, blk, blk,
                  pl.BlockSpec((nb, T, T), lambda h, j: (0, 0, 0), **once)],
        out_specs=[pl.BlockSpec(memory_space=pltpu.HBM), blk, blk], out_shape=[shp, shp, shp],
        scratch_shapes=([pltpu.VMEM((T, T), F32)] * (2 * HB) + [pltpu.VMEM((T, T), BF16)] * (2 * HB)
                        + [pltpu.VMEM((HB, S, HEAD_PAD), F32), pltpu.SemaphoreType.DMA]),
        operands=(q, do, lse, dd, k, v, table_t))


def _mid_bwd(dh2, h1, e, gate, g2, w_pg, w_out, oa, ob, zm):
    S = dh2.shape[0]

    def body(dh2_ref, h1_ref, e_ref, gate_ref, g2_ref, wg_ref, wo_ref, oa_ref, ob_ref, ga_ref, gb_ref,
             dh1_ref, dh1b_ref, de_ref, dpre_ref, doa_ref, dob_ref, dga_ref, dgb_ref, dd_ref, dg2_ref):
        @pl.when(pl.program_id(0) == 0)
        def _():
            dg2_ref[...] = jnp.zeros_like(dg2_ref)

        lane = lax.broadcasted_iota(jnp.int32, (TOK_T, HEAD_PAD), 1)
        lo_half = lane < HEAD_DIM
        dh2 = dh2_ref[...]
        gate = gate_ref[...]
        de_ref[...] = (dh2 * gate).astype(BF16)
        dpre = (dh2 * e_ref[...] * gate * (1.0 - gate)).astype(BF16)
        dpre_ref[...] = dpre
        du2 = _dot_nt(dpre, wg_ref[...])
        h1 = h1_ref[...]
        r = lax.rsqrt(jnp.mean(h1 * h1, axis=-1, keepdims=True) + EPS)
        xh = h1 * r
        a = du2 * g2_ref[...]
        dh1 = dh2 + r * (a - xh * jnp.mean(a * xh, axis=-1, keepdims=True))
        dg2_ref[...] += jnp.sum(du2 * xh, axis=0, keepdims=True)
        dh1_ref[...] = dh1
        dh1b = dh1.astype(BF16)
        dh1b_ref[...] = dh1b
        dy = _dot_nt(dh1b, wo_ref[...])
        dd = jnp.zeros((TOK_T, HEAD_PAD), F32)
        for bi, (o_ref, g_ref, do_ref, dg_ref) in enumerate(
                ((oa_ref, ga_ref, doa_ref, dga_ref), (ob_ref, gb_ref, dob_ref, dgb_ref))):
            for pi in range(N_HEADS // 2):
                col = bi * D_BRANCH + HEAD_PAD * pi
                dyp = dy[:, col:col + HEAD_PAD]
                g = g_ref[:, HEAD_PAD * pi:HEAD_PAD * (pi + 1)]
                sg = _sigmoid(g)
                dg_ref[:, HEAD_PAD * pi:HEAD_PAD * (pi + 1)] = (
                    dyp * _pair(o_ref, pi, lo_half) * (sg * (1.0 + g * (1.0 - sg)))).astype(BF16)
                dop = dyp * (g * sg)
                for hh, d_head in ((2 * pi, dop), (2 * pi + 1, pltpu.roll(dop, HEAD_DIM, 1))):
                    d_head = jnp.where(lo_half, d_head, 0.0)
                    do_ref[hh] = d_head.astype(BF16)
                    dsum = jnp.sum(d_head * o_ref[hh], axis=1, keepdims=True)
                    dd = dd + jnp.where(lane == bi * N_HEADS + hh, dsum, 0.0)
        dd_ref[...] = dd.T[:2 * N_HEADS, :]

    tok = lambda w: pl.BlockSpec((TOK_T, w), lambda i: (i, 0))
    head = pl.BlockSpec((N_HEADS, TOK_T, HEAD_PAD), lambda i: (0, i, 0))
    full = lambda a, b: pl.BlockSpec((a, b), lambda i: (0, 0))
    act = lambda w, dt: jax.ShapeDtypeStruct((S, w), dt)
    hshape = lambda w, dt: jax.ShapeDtypeStruct((N_HEADS, S, w), dt)
    return pl.pallas_call(
        body, name="mid_bwd", grid=(S // TOK_T,),
        in_specs=[tok(D_MODEL)] * 4 + [full(1, D_MODEL), full(D_MODEL, D_MODEL), full(D_MODEL, D_MODEL), head, head,
                                      pl.BlockSpec((TOK_T, D_BRANCH), lambda i: (i, 3)),
                                      pl.BlockSpec((TOK_T, D_BRANCH), lambda i: (i, 7))],
        out_specs=[tok(D_MODEL)] * 4 + [head, head, tok(D_BRANCH), tok(D_BRANCH),
                                       pl.BlockSpec((2 * N_HEADS, TOK_T), lambda i: (0, i)), full(1, D_MODEL)],
        out_shape=[act(D_MODEL, F32), act(D_MODEL, BF16), act(D_MODEL, BF16), act(D_MODEL, BF16),
                   hshape(HEAD_PAD, BF16), hshape(HEAD_PAD, BF16), act(D_BRANCH, BF16), act(D_BRANCH, BF16),
                   jax.ShapeDtypeStruct((2 * N_HEADS, S), F32), jax.ShapeDtypeStruct((1, D_MODEL), F32)],
        compiler_params=_cp("arbitrary"),
    )(dh2, h1, e, gate, g2, w_pg, w_out, oa, ob, zm, zm)


def _prep_bwd(dqa, dka, dva, dqb, dkb, dvb, zm, qkg, rope_c, rope_a, rope_b, dga, dgb):
    S = zm.shape[0]

    def body(dqa_ref, dka_ref, dva_ref, dqb_ref, dkb_ref, dvb_ref, z_ref, g_ref, rc_ref, ra_ref, rb_ref,
             dga_ref, dgb_ref, dz_ref, dc_ref, dqkg_ref):
        @pl.when(pl.program_id(0) == 0)
        def _():
            dqkg_ref[...] = jnp.zeros_like(dqkg_ref)

        lane = lax.broadcasted_iota(jnp.int32, (TOK_T, HEAD_PAD), 1)
        lo_half = lane < HEAD_DIM
        rc, ra, rb = rc_ref[...], ra_ref[...], rb_ref[...]
        same_head = _same_head()

        def unrope(dy):
            return dy * rc + pltpu.roll(dy * ra, ROPE_HALF, 1) + pltpu.roll(dy * rb, HEAD_PAD - ROPE_HALF, 1)

        def norm_bwd(col, gi, dy):
            x = z_ref[:, col:col + HEAD_PAD]
            r = _pair_rsqrt(x, same_head)
            xh = x * r
            dqkg_ref[gi:gi + 1, :] += jnp.sum(dy * xh, axis=0, keepdims=True)
            a = dy * g_ref[gi:gi + 1, :]
            dz_ref[:, col:col + HEAD_PAD] = (r * (a - xh * _pair_mean(a * xh, same_head))).astype(BF16)

        dc = jnp.zeros((TOK_T, HEAD_PAD), F32)
        for pi in range(N_HEADS // 2):
            col = HEAD_PAD * pi
            norm_bwd(col, 0, _pair(dqa_ref, pi, lo_half) * SCALE)
            norm_bwd(D_BRANCH + col, 1, _pair(dka_ref, pi, lo_half) * LN2)
            dz_ref[:, 2 * D_BRANCH + col:2 * D_BRANCH + col + HEAD_PAD] = _pair(dva_ref, pi, lo_half).astype(BF16)
            norm_bwd(4 * D_BRANCH + col, 2, unrope(_pair(dqb_ref, pi, lo_half) * SCALE))
            norm_bwd(5 * D_BRANCH + col, 3, unrope(_pair(dkb_ref, pi, lo_half) * LN2))
            dz_ref[:, 6 * D_BRANCH + col:6 * D_BRANCH + col + HEAD_PAD] = _pair(dvb_ref, pi, lo_half).astype(BF16)
            for hh in (2 * pi, 2 * pi + 1):
                dch = dka_ref[hh][:, HEAD_DIM:HEAD_DIM + 1] + dqa_ref[hh][:, HEAD_DIM + 3:HEAD_DIM + 4]
                dc = dc + jnp.where(lane == hh, dch, 0.0)
        dz_ref[:, 3 * D_BRANCH:4 * D_BRANCH] = dga_ref[...]
        dz_ref[:, 7 * D_BRANCH:8 * D_BRANCH] = dgb_ref[...]
        dc_ref[...] = dc

    tok = lambda w: pl.BlockSpec((TOK_T, w), lambda i: (i, 0))
    head = pl.BlockSpec((N_HEADS, TOK_T, HEAD_PAD), lambda i: (0, i, 0))
    return pl.pallas_call(
        body, name="prep_bwd", grid=(S // TOK_T,),
        in_specs=[head] * 6 + [tok(N_MAIN), pl.BlockSpec((4, 128), lambda i: (0, 0)), tok(128), tok(128), tok(128),
                               tok(D_BRANCH), tok(D_BRANCH)],
        out_specs=[tok(N_MAIN), tok(128), pl.BlockSpec((4, 128), lambda i: (0, 0))],
        out_shape=[jax.ShapeDtypeStruct((S, N_MAIN), BF16), jax.ShapeDtypeStruct((S, 128), F32),
                   jax.ShapeDtypeStruct((4, 128), F32)],
        compiler_params=_cp("arbitrary"),
    )(dqa, dka, dva, dqb, dkb, dvb, zm, qkg, rope_c, rope_a, rope_b, dga, dgb)


def _inproj_bwd(dzm, dzf, wm, wf, h0, dh1, g, rider=None):
    S = h0.shape[0]

    def body(dzm_ref, dzf_ref, wm_ref, wf_ref, h_ref, dh1_ref, g_ref, dh0_ref, dg_ref):
        @pl.when(pl.program_id(0) == 0)
        def _():
            dg_ref[...] = jnp.zeros_like(dg_ref)

        du = _dot_nt(dzm_ref[...], wm_ref[...]) + _dot_nt(dzf_ref[...], wf_ref[...])
        x = h_ref[...]
        r = lax.rsqrt(jnp.mean(x * x, axis=-1, keepdims=True) + EPS)
        xh = x * r
        a = du * g_ref[...]
        dh0_ref[...] = dh1_ref[...] + r * (a - xh * jnp.mean(a * xh, axis=-1, keepdims=True))
        dg_ref[...] += jnp.sum(du * xh, axis=0, keepdims=True)

    tok = lambda w: pl.BlockSpec((TOK_T, w), lambda i: (i, 0))
    full = lambda a, b: pl.BlockSpec((a, b), lambda i: (0, 0))
    return _call_with_rider(
        body, "inproj_bwd", (S // TOK_T,), rider,
        in_specs=[tok(N_MAIN), tok(128), full(D_MODEL, N_MAIN), full(D_MODEL, 128), tok(D_MODEL), tok(D_MODEL),
                  full(1, D_MODEL)],
        out_specs=[tok(D_MODEL), full(1, D_MODEL)],
        out_shape=[jax.ShapeDtypeStruct((S, D_MODEL), F32), jax.ShapeDtypeStruct((1, D_MODEL), F32)],
        scratch_shapes=[], operands=(dzm, dzf, wm, wf, h0, dh1, g), semantics=("arbitrary",))


def _wgrad(a, b, name, a_lead=()):
    S, M = a.shape[len(a_lead):]
    N = b.shape[1]
    tn = min(N, 2048)
    ts = 512
    last = S // ts - 1

    def body(a_ref, b_ref, o_ref, acc_ref):
        @pl.when(pl.program_id(1) == 0)
        def _():
            acc_ref[...] = jnp.zeros_like(acc_ref)

        acc_ref[...] += _dot_tn(a_ref[...].astype(BF16), b_ref[...])

        @pl.when(pl.program_id(1) == last)
        def _():
            o_ref[...] = acc_ref[...].astype(BF16)

    return pl.pallas_call(
        body, name=name, grid=(N // tn, S // ts),
        in_specs=[_slab_spec(a_lead, (ts, M), lambda n, s: (s, 0)), pl.BlockSpec((ts, tn), lambda n, s: (s, n))],
        out_specs=pl.BlockSpec((M, tn), lambda n, s: (0, n)),
        out_shape=jax.ShapeDtypeStruct((M, N), BF16),
        scratch_shapes=[pltpu.VMEM((M, tn), F32)],
        compiler_params=_cp("parallel", "arbitrary"),
    )(a, b)


def _rope_tables(positions):
    inv_freq = ROPE_THETA ** (-jnp.arange(ROPE_HALF, dtype=F32) / ROPE_HALF)
    ang = positions.astype(F32)[:, None] * inv_freq
    cos, sin = jnp.cos(ang), jnp.sin(ang)
    S = positions.shape[0]
    one, zero = jnp.ones((S, HEAD_DIM - 2 * ROPE_HALF), F32), jnp.zeros((S, HEAD_DIM - 2 * ROPE_HALF), F32)
    z8 = jnp.zeros((S, ROPE_HALF), F32)
    rc = jnp.concatenate([cos, cos, one], axis=1)
    ra = jnp.concatenate([-sin, z8, zero], axis=1)
    rb = jnp.concatenate([z8, sin, zero], axis=1)
    return tuple(jnp.tile(t, (1, 2)) for t in (rc, ra, rb))


def _in_operands(w_in):
    w_in = w_in.astype(BF16)
    wm = jnp.concatenate([w_in[:, :4 * D_BRANCH], w_in[:, 4 * D_BRANCH + N_HEADS:]], axis=1)
    wf = jnp.pad(w_in[:, 4 * D_BRANCH:4 * D_BRANCH + N_HEADS], ((0, 0), (0, 128 - N_HEADS)))
    return dict(wm=wm, wf=wf)


def _layer_weights(w_in, w_out, w_ple, w_pg):
    return dict(_in_operands(w_in), w_out=w_out.astype(BF16), w_ple=w_ple.astype(BF16), w_pg=w_pg.astype(BF16))


def _row(v, width=128):
    v = v.reshape(1, -1).astype(F32)
    return jnp.pad(v, ((0, 0), (0, width - v.shape[1])))


def _layer_fwd(h0, p, rope, tabs, w, norm_g, b_f, qk_g, ple_g, rider=None, late=None):
    g1 = norm_g.reshape(1, D_MODEL)
    g2 = ple_g.reshape(1, D_MODEL)
    qkg = jnp.tile(qk_g, (1, 2))
    bf = _row(b_f)
    zm, zf, u = _inproj_fwd(h0, g1, w["wm"], w["wf"])
    c = _forget_cumsum(zf, bf, tabs["tril"])
    qa, ka, va, qb, kb, vb = _prep_fwd(zm, c, qkg, *rope)
    oa, lse_a, *arrivals = _attn_fwd(qa, ka, va, tabs["fox"], True, "fox_fwd", rider)
    if late is not None:
        w = {**w, **late(arrivals)}
    ob, lse_b = _attn_fwd(qb, kb, vb, tabs["dil"], False, "dil_fwd")
    y, h1, h2, u2, e, gate = _mid_fwd(oa, ob, zm, h0, p, w["w_out"], w["w_pg"], w["w_ple"], g2)
    saved = dict(h0=h0, p=p, zm=zm, zf=zf, u=u, qa=qa, ka=ka, va=va, qb=qb, kb=kb, vb=vb, oa=oa, ob=ob,
                 lse_a=lse_a, lse_b=lse_b, y=y, h1=h1, u2=u2, e=e, gate=gate, g1=g1, g2=g2, qkg=qkg, bf=bf, w=w)
    return h2, saved, arrivals


def _layer_bwd(dh2, sv, rope, tabs, make_rider=None, make_last_rider=None):
    S = dh2.shape[0]
    nq = S // ATT_T
    w = sv["w"]
    rows = lambda a: a.reshape(N_HEADS, nq, 1, ATT_T)
    (dh1, dh1b, de, dpre, doa, dob, dga, dgb, dd, dg2) = _mid_bwd(
        dh2, sv["h1"], sv["e"], sv["gate"], sv["g2"], w["w_pg"], w["w_out"], sv["oa"], sv["ob"], sv["zm"])
    dda, ddb = dd[:N_HEADS], dd[N_HEADS:]
    early = dict(w_out=_wgrad(sv["y"], dh1b, "wgrad_out"), w_ple=_wgrad(sv["p"][0], de, "wgrad_ple", sv["p"][1]),
                 w_ple_gate=_wgrad(sv["u2"], dpre, "wgrad_gate"))
    rider = None if make_rider is None else make_rider(early)
    dqa, dka, dva, *arrivals = _attn_bwd(sv["qa"], sv["ka"], sv["va"], doa, sv["lse_a"], rows(dda), tabs["fox"],
                                         True, "fox_bwd", rider)
    dqb, dkb, dvb = _attn_bwd(sv["qb"], sv["kb"], sv["vb"], dob, sv["lse_b"], rows(ddb), tabs["dil"], False,
                              "dil_bwd")
    dzm, dc, dqkg = _prep_bwd(dqa, dka, dva, dqb, dkb, dvb, sv["zm"], sv["qkg"], *rope, dga, dgb)
    dzf, dbf = _forget_bwd(dc, sv["zf"], sv["bf"], tabs["triu"])
    dwm = _wgrad(sv["u"], dzm, "wgrad_in")
    dwf = _wgrad(sv["u"], dzf, "wgrad_f")
    dw_in = jnp.concatenate([dwm[:, :4 * D_BRANCH], dwf[:, :N_HEADS], dwm[:, 4 * D_BRANCH:]], axis=1)
    last_rider = None if make_last_rider is None else make_last_rider(dw_in)
    dh0, dg1, *last_arrivals = _inproj_bwd(dzm, dzf, w["wm"], w["wf"], sv["h0"], dh1, sv["g1"], last_rider)
    grads = dict(norm_g=dg1.reshape(D_MODEL), w_in=dw_in, b_f=dbf[0, :N_HEADS],
                 qk_norm_g=dqkg[:, :HEAD_DIM] + dqkg[:, HEAD_DIM:], ple_norm_g=dg2.reshape(D_MODEL), **early)
    return dh0, grads, arrivals + last_arrivals


def _tables():
    T = CUM_T
    r = lax.broadcasted_iota(jnp.int32, (T, T), 0)
    c = lax.broadcasted_iota(jnp.int32, (T, T), 1)
    return dict(fox=_bias_tables(True), dil=_bias_tables(False),
                tril=(c <= r).astype(BF16), triu=(c >= r).astype(BF16))


def _local_step(x, p, positions, target, layers, small):
    rope = _rope_tables(positions)
    tabs = _tables()
    ws = [_layer_weights(*lw) for lw in layers]
    h = x
    saved = []
    for w, lp, sm in zip(ws, p, small):
        h, sv, _ = _layer_fwd(h, (lp, ()), rope, tabs, w, *sm)
        saved.append(sv)
    dh, loss = _loss_fwd_bwd(h, target)
    grads = [None] * len(ws)
    for li in reversed(range(len(ws))):
        dh, grads[li], _ = _layer_bwd(dh, saved[li], rope, tabs)
    return loss[0, 0], dh, grads


def _peers():
    x, y, c = lax.axis_index("x"), lax.axis_index("y"), lax.axis_index("c")
    me = 4 * x + 2 * y + c
    flip = lambda v, bit: 1 - v if bit else v
    return me, [(flip(x, k & 4), flip(y, k & 2), flip(c, k & 1)) for k in range(1, N_DEV)]


def _sel(ref, kind, d):
    if kind == "whole":
        return ref
    if kind == "slot":
        return ref.at[d]
    block = pl.ds(pl.multiple_of(d * 128, 128), 128)
    return ref.at[block, :] if kind == "rows" else ref.at[:, block]


class _Pushes:
    def __init__(self, arrays, src_kinds, dst_kinds, out_shapes):
        self.arrays, self.n = list(arrays), len(arrays)
        self.src_kinds, self.dst_kinds = src_kinds, dst_kinds
        self.out_shapes = [jax.ShapeDtypeStruct(s, a.dtype) for s, a in zip(out_shapes, arrays)]
        hbm = pl.BlockSpec(memory_space=pltpu.HBM)
        self.in_specs, self.out_specs = [hbm] * self.n, [hbm] * self.n
        self.scratch_shapes = [pltpu.SemaphoreType.DMA((N_DEV - 1, self.n)),
                               pltpu.SemaphoreType.DMA((N_DEV - 1, self.n)), pltpu.SemaphoreType.DMA((self.n,))]

    def _copies(self, ins, outs, sems):
        send_sems, recv_sems, local_sems = sems
        me, peers = _peers()
        src = lambda a, d: _sel(ins[a], self.src_kinds[a], d)
        dst = lambda a: _sel(outs[a], self.dst_kinds[a], me)
        local = [pltpu.make_async_copy(src(a, me), dst(a), local_sems.at[a]) for a in range(self.n)]
        remote = [pltpu.make_async_remote_copy(
            src_ref=src(a, 4 * px + 2 * py + pc), dst_ref=dst(a), send_sem=send_sems.at[k, a],
            recv_sem=recv_sems.at[k, a], device_id=(px, py, pc), device_id_type=pl.DeviceIdType.MESH)
            for k, (px, py, pc) in enumerate(peers) for a in range(self.n)]
        return local + remote

    def start(self, ins, outs, sems):
        for cp in self._copies(ins, outs, sems):
            cp.start()

    def wait(self, ins, outs, sems):
        for cp in self._copies(ins, outs, sems):
            cp.wait()


def _exchange(name, pushes):
    n = pushes.n

    def body(*refs):
        pushes.start(refs[:n], refs[n:2 * n], refs[2 * n:])
        pushes.wait(refs[:n], refs[n:2 * n], refs[2 * n:])

    return pl.pallas_call(body, name=name, in_specs=pushes.in_specs, out_specs=pushes.out_specs,
                          out_shape=pushes.out_shapes, scratch_shapes=pushes.scratch_shapes)(*pushes.arrays)


def _gather_pushes(shards, kinds):
    full = {"slot": lambda s: (N_DEV,) + s, "rows": lambda s: (N_DEV * s[0], s[1]),
            "cols": lambda s: (s[0], N_DEV * s[1])}
    return _Pushes(shards, ["whole"] * len(shards), kinds, [full[k](a.shape) for a, k in zip(shards, kinds)])


def _scatter_pushes(partials, kinds):
    part = {"slot": lambda s: s[1:], "rows": lambda s: (128, s[1]), "cols": lambda s: (s[0], 128),
            "whole": lambda s: s}
    return _Pushes(partials, kinds, ["slot"] * len(partials),
                   [(N_DEV,) + part[k](a.shape) for a, k in zip(partials, kinds)])


def _adamw(name, parts, w, m, v, rows):
    L, R, C = w.shape

    def body(p_ref, w_ref, m_ref, v_ref, g_ref, d_ref, nm_ref, nv_ref):
        g = p_ref[0, 0].astype(F32)
        for s in range(1, N_DEV):
            g = g + p_ref[s, 0].astype(F32)
        g_ref[0] = g
        nm = ADAM_B1 * m_ref[0] + (1.0 - ADAM_B1) * g
        nv = ADAM_B2 * v_ref[0] + (1.0 - ADAM_B2) * (g * g)
        nm_ref[0] = nm
        nv_ref[0] = nv
        m_hat = nm / (1.0 - ADAM_B1 ** ADAM_STEP)
        v_hat = nv / (1.0 - ADAM_B2 ** ADAM_STEP)
        d_ref[0] = -ADAM_LR * (m_hat / (jnp.sqrt(v_hat) + ADAM_EPS) + ADAM_WD * w_ref[0])

    blk = pl.BlockSpec((1, rows, C), lambda l, i: (l, i, 0))
    shp = jax.ShapeDtypeStruct((L, R, C), F32)
    return pl.pallas_call(
        body, name=name, grid=(L, R // rows),
        in_specs=[pl.BlockSpec((N_DEV, 1, rows, C), lambda l, i: (0, l, i, 0)), blk, blk, blk],
        out_specs=[blk] * 4, out_shape=[shp] * 4,
        compiler_params=_cp("parallel", "parallel"),
    )(parts, w, m, v)


SMALL_ROWS = 40
LOSS_ROW = 37


def _pack_small(norm_g, ple_g, qk_g, b_f, last_row):
    rows = lambda a: a.astype(F32).reshape(-1, 128)
    flat = jnp.concatenate([rows(norm_g), rows(ple_g), rows(qk_g), _row(b_f.reshape(-1)), last_row], axis=0)
    return jnp.pad(flat, ((0, SMALL_ROWS - flat.shape[0]), (0, 0)))


def _unpack_small(flat):
    return (flat[0:16].reshape(2, D_MODEL), flat[16:32].reshape(2, D_MODEL), flat[32:36].reshape(2, 4, HEAD_DIM),
            flat[36, :2 * N_HEADS].reshape(2, N_HEADS))


def kernel(x, p, positions, norm_g, w_in, b_f, qk_norm_g, w_out, w_ple, ple_norm_g, w_ple_gate, loss_target, m_norm_g, m_w_in, m_b_f, m_qk_norm_g, m_w_out, m_w_ple, m_ple_norm_g, m_w_ple_gate, v_norm_g, v_w_in, v_b_f, v_qk_norm_g, v_w_out, v_w_ple, v_ple_norm_g, v_w_ple_gate):
    bf16 = lambda a: a.astype(BF16)
    rows_in = W_IN_ROWS // 2
    flat_in = lambda a: bf16(a).reshape(rows_in, 128)
    full_in = lambda g: g.reshape(N_DEV, D_MODEL, W_IN_SHARD).transpose(1, 0, 2).reshape(D_MODEL, N_IN)
    small = [(norm_g[l], b_f[l], qk_norm_g[l], ple_norm_g[l]) for l in range(2)]
    rope = _rope_tables(positions[0])
    tabs = _tables()

    (g_in0,) = _exchange("gather_first", _gather_pushes([flat_in(w_in[0])], ["slot"]))
    rest = _gather_pushes([flat_in(w_in[1])] + [bf16(a[l]) for l in range(2) for a in (w_out, w_ple, w_ple_gate)],
                          ["slot"] + ["rows", "cols", "rows"] * 2)
    late = lambda got: dict(w_out=got[1], w_ple=got[2], w_pg=got[3])
    h, sv0, got = _layer_fwd(x[0], (p, (0, 0)), rope, tabs, _in_operands(full_in(g_in0)), *small[0], rest, late)
    w1 = dict(_in_operands(full_in(got[0])), w_out=got[4], w_ple=got[5], w_pg=got[6])
    h, sv1, _ = _layer_fwd(h, (p, (1, 0)), rope, tabs, w1, *small[1])
    dh, loss = _loss_fwd_bwd(h, loss_target[0])
    dh, gr1, _ = _layer_bwd(dh, sv1, rope, tabs)

    by_dest = lambda d: d.reshape(D_MODEL, N_DEV, W_IN_SHARD).transpose(1, 0, 2).reshape(N_DEV, rows_in, 128)
    big = ("w_out", "w_ple", "w_ple_gate")
    riding = lambda early: _scatter_pushes([by_dest(gr1["w_in"])] + [gr1[n] for n in big] + [early[n] for n in big],
                                           ["slot"] + ["rows", "cols", "rows"] * 2)
    riding_last = lambda dw_in: _scatter_pushes([by_dest(dw_in)], ["slot"])
    dx, gr0, (r_in1, *r_big, r_in0) = _layer_bwd(dh, sv0, rope, tabs, riding, riding_last)
    grads = (gr0, gr1)
    stack = lambda name: jnp.stack([gl[name] for gl in grads], axis=0)
    small_part = _pack_small(stack("norm_g"), stack("ple_norm_g"), stack("qk_norm_g"), stack("b_f"),
                             _row(loss[0, 0].reshape(1)))
    (r_small,) = _exchange("exchange_small", _scatter_pushes([small_part], ["whole"]))
    r_in = jnp.concatenate([r_in0, r_in1], axis=1)
    r_out, r_ple, r_pg = (jnp.stack([r_big[3 + k], r_big[k]], axis=1) for k in range(3))

    zero_row = jnp.zeros((1, 128), F32)
    small_of = lambda ng, pg, qk, bf: _pack_small(ng, pg, qk, bf, zero_row)[None]
    flat = lambda a: a.reshape(1, W_IN_ROWS, 128)
    outs = dict(
        w_in=[o.reshape(w_in.shape) for o in
              _adamw("adamw_in", r_in[:, None], flat(w_in), flat(m_w_in), flat(v_w_in), W_IN_TILE)],
        w_out=_adamw("adamw_out", r_out, w_out, m_w_out, v_w_out, 128),
        w_ple=_adamw("adamw_ple", r_ple, w_ple, m_w_ple, v_w_ple, 256),
        w_pg=_adamw("adamw_gate", r_pg, w_ple_gate, m_w_ple_gate, v_w_ple_gate, 128),
        small=_adamw("adamw_small", r_small[:, None], small_of(norm_g, ple_norm_g, qk_norm_g, b_f),
                     small_of(m_norm_g, m_ple_norm_g, m_qk_norm_g, m_b_f),
                     small_of(v_norm_g, v_ple_norm_g, v_qk_norm_g, v_b_f), SMALL_ROWS))
    leaves = []
    for kind in range(4):
        ng, pg, qk, bf = _unpack_small(outs["small"][kind][0])
        leaves += [ng, outs["w_in"][kind], bf, qk, outs["w_out"][kind], outs["w_ple"][kind], pg, outs["w_pg"][kind]]
    return (outs["small"][0][0, LOSS_ROW, 0], dx[None], *leaves)
```

```python
import functools

import jax
import jax.numpy as jnp
from jax import lax
from jax.experimental import pallas as pl
from jax.experimental.pallas import tpu as pltpu

F32 = jnp.float32
BF16 = jnp.bfloat16

D_MODEL = 1024
HEAD_DIM = 64
N_HEADS = 8
HEAD_PAD = 128
D_BRANCH = N_HEADS * HEAD_DIM
N_MAIN = 8 * D_BRANCH
N_IN = N_MAIN + N_HEADS
PLE_DIM = 256
ROPE_THETA = 500000.0
ROPE_HALF = 8
EPS = 1e-6
NEG = -1e30
SCALE = HEAD_DIM ** -0.5
LOG2E = 1.4426950408889634
LN2 = 0.6931471805599453
DILATED_PATTERNS = ((128, 1), (512, 4), (2048, 16))
N_DEV = 8
W_IN_SHARD = N_IN // N_DEV
W_IN_ROWS = 2 * D_MODEL * W_IN_SHARD // 128
W_IN_TILE = W_IN_ROWS // 19

ADAM_LR = 0.001
ADAM_B1 = 0.9
ADAM_B2 = 0.999
ADAM_EPS = 1e-08
ADAM_WD = 0.01
ADAM_STEP = 10

ATT_T = 512
ATT_FWD_HEADS = 4
ATT_BWD_HEADS = 2
ATT_CHUNK = 32
TOK_T = 256
CUM_T = 512
VMEM_LIMIT = 56 * 1024 * 1024


def _slab_spec(lead, block, index):
    return pl.BlockSpec((None,) * len(lead) + block, lambda *g: (*lead, *index(*g)))


def _cp(*sem):
    return pltpu.CompilerParams(dimension_semantics=sem, vmem_limit_bytes=VMEM_LIMIT)


def _sigmoid(x):
    return 1.0 / (1.0 + jnp.exp(-x))


def _split3(x):
    hi = x.astype(BF16)
    r1 = x - hi.astype(F32)
    mid = r1.astype(BF16)
    lo = (r1 - mid.astype(F32)).astype(BF16)
    return hi, mid, lo


def _dot(a, b):
    return jnp.dot(a, b, preferred_element_type=F32)


def _dot_nt(a, b):
    return lax.dot_general(a, b, (((1,), (1,)), ((), ())), preferred_element_type=F32)


def _dot_tn(a, b):
    return lax.dot_general(a, b, (((0,), (0,)), ((), ())), preferred_element_type=F32)


def _inproj_fwd(h, g, wm, wf):
    S = h.shape[0]

    def body(h_ref, g_ref, wm_ref, wf_ref, zm_ref, zf_ref, u_ref):
        x = h_ref[...]
        r = lax.rsqrt(jnp.mean(x * x, axis=-1, keepdims=True) + EPS)
        u = (x * r * g_ref[...]).astype(BF16)
        u_ref[...] = u
        zm_ref[...] = _dot(u, wm_ref[...])
        zf_ref[...] = _dot(u, wf_ref[...])

    return pl.pallas_call(
        body, name="inproj_fwd", grid=(S // TOK_T,),
        in_specs=[pl.BlockSpec((TOK_T, D_MODEL), lambda i: (i, 0)),
                  pl.BlockSpec((1, D_MODEL), lambda i: (0, 0)),
                  pl.BlockSpec((D_MODEL, N_MAIN), lambda i: (0, 0)),
                  pl.BlockSpec((D_MODEL, 128), lambda i: (0, 0))],
        out_specs=[pl.BlockSpec((TOK_T, N_MAIN), lambda i: (i, 0)),
                   pl.BlockSpec((TOK_T, 128), lambda i: (i, 0)),
                   pl.BlockSpec((TOK_T, D_MODEL), lambda i: (i, 0))],
        out_shape=[jax.ShapeDtypeStruct((S, N_MAIN), F32), jax.ShapeDtypeStruct((S, 128), F32),
                   jax.ShapeDtypeStruct((S, D_MODEL), BF16)],
        compiler_params=_cp("parallel"),
    )(h, g, wm, wf)


def _log_sigmoid(x):
    return jnp.minimum(x, 0.0) - jnp.log(1.0 + jnp.exp(-jnp.abs(x)))


def _forget_cumsum(zf, bf, tri):
    S = zf.shape[0]

    def body(zf_ref, b_ref, tri_ref, c_ref, carry):
        @pl.when(pl.program_id(0) == 0)
        def _():
            carry[...] = jnp.zeros_like(carry)

        lf = _log_sigmoid(zf_ref[...] + b_ref[...])
        hi, mid, lo = _split3(lf)
        t = tri_ref[...]
        cs = _dot(t, hi) + _dot(t, mid) + _dot(t, lo) + carry[...]
        c_ref[...] = cs
        carry[...] = cs[CUM_T - 1:CUM_T, :]

    return pl.pallas_call(
        body, name="forget_cumsum", grid=(S // CUM_T,),
        in_specs=[pl.BlockSpec((CUM_T, 128), lambda i: (i, 0)),
                  pl.BlockSpec((1, 128), lambda i: (0, 0)),
                  pl.BlockSpec((CUM_T, CUM_T), lambda i: (0, 0))],
        out_specs=pl.BlockSpec((CUM_T, 128), lambda i: (i, 0)),
        out_shape=jax.ShapeDtypeStruct((S, 128), F32),
        scratch_shapes=[pltpu.VMEM((1, 128), F32)],
        compiler_params=_cp("arbitrary"),
    )(zf, bf, tri)


def _forget_bwd(dc, zf, bf, triu):
    S = zf.shape[0]
    n = S // CUM_T

    def body(dc_ref, zf_ref, b_ref, tri_ref, dzf_ref, db_ref, carry):
        @pl.when(pl.program_id(0) == 0)
        def _():
            carry[...] = jnp.zeros_like(carry)
            db_ref[...] = jnp.zeros_like(db_ref)

        hi, mid, lo = _split3(dc_ref[...])
        t = tri_ref[...]
        dlf = _dot(t, hi) + _dot(t, mid) + _dot(t, lo) + carry[...]
        carry[...] = dlf[0:1, :]
        dfa = dlf * (1.0 - _sigmoid(zf_ref[...] + b_ref[...]))
        dzf_ref[...] = dfa.astype(BF16)
        db_ref[...] += jnp.sum(dfa, axis=0, keepdims=True)

    return pl.pallas_call(
        body, name="forget_bwd", grid=(n,),
        in_specs=[pl.BlockSpec((CUM_T, 128), lambda i: (n - 1 - i, 0)),
                  pl.BlockSpec((CUM_T, 128), lambda i: (n - 1 - i, 0)),
                  pl.BlockSpec((1, 128), lambda i: (0, 0)),
                  pl.BlockSpec((CUM_T, CUM_T), lambda i: (0, 0))],
        out_specs=[pl.BlockSpec((CUM_T, 128), lambda i: (n - 1 - i, 0)),
                   pl.BlockSpec((1, 128), lambda i: (0, 0))],
        out_shape=[jax.ShapeDtypeStruct((S, 128), BF16), jax.ShapeDtypeStruct((1, 128), F32)],
        scratch_shapes=[pltpu.VMEM((1, 128), F32)],
        compiler_params=_cp("arbitrary"),
    )(dc, zf, bf, triu)


def _same_head():
    r = lax.broadcasted_iota(jnp.int32, (HEAD_PAD, HEAD_PAD), 0) // HEAD_DIM
    c = lax.broadcasted_iota(jnp.int32, (HEAD_PAD, HEAD_PAD), 1) // HEAD_DIM
    return (r == c).astype(BF16)


def _pair_mean(x, same_head):
    hi = x.astype(BF16)
    lo = (x - hi.astype(F32)).astype(BF16)
    return (_dot(hi, same_head) + _dot(lo, same_head)) * (1.0 / HEAD_DIM)


def _pair_rsqrt(x, same_head):
    return lax.rsqrt(_pair_mean(x * x, same_head) + EPS)


def _prep_fwd(zm, c, qkg, rope_c, rope_a, rope_b):
    S = zm.shape[0]
    shp = jax.ShapeDtypeStruct((N_HEADS, S, HEAD_PAD), BF16)

    def body(z_ref, c_ref, g_ref, rc_ref, ra_ref, rb_ref, qa_ref, ka_ref, va_ref, qb_ref, kb_ref, vb_ref):
        lane = lax.broadcasted_iota(jnp.int32, (TOK_T, HEAD_PAD), 1)
        lo_half = lane < HEAD_DIM
        aug = (lane >= HEAD_DIM) & (lane < HEAD_DIM + 3)
        q_pad = jnp.where(aug, -1.0, 0.0)
        cs = c_ref[...]
        rc, ra, rb = rc_ref[...], ra_ref[...], rb_ref[...]
        same_head = _same_head()

        def norm(col, gi):
            x = z_ref[:, col:col + HEAD_PAD]
            return x * _pair_rsqrt(x, same_head) * g_ref[gi:gi + 1, :]

        def rope(y):
            return y * rc + pltpu.roll(y, HEAD_PAD - ROPE_HALF, 1) * ra + pltpu.roll(y, ROPE_HALF, 1) * rb

        def put(ref, pi, y, pad_even, pad_odd):
            ref[2 * pi] = jnp.where(lo_half, y, pad_even).astype(BF16)
            ref[2 * pi + 1] = jnp.where(lo_half, pltpu.roll(y, HEAD_DIM, 1), pad_odd).astype(BF16)

        def k_pad(h):
            ch = cs[:, h:h + 1] * LOG2E
            hi = ch.astype(BF16).astype(F32)
            mid = (ch - hi).astype(BF16).astype(F32)
            lo = ch - hi - mid
            ones = jnp.where(lane == HEAD_DIM + 3, 1.0, 0.0)
            return jnp.where(lane == HEAD_DIM, hi, jnp.where(lane == HEAD_DIM + 1, mid,
                                                              jnp.where(lane == HEAD_DIM + 2, lo, ones)))

        for pi in range(N_HEADS // 2):
            col = HEAD_PAD * pi
            put(qa_ref, pi, norm(col, 0) * (SCALE * LOG2E), q_pad, q_pad)
            put(ka_ref, pi, norm(D_BRANCH + col, 1), k_pad(2 * pi), k_pad(2 * pi + 1))
            put(va_ref, pi, z_ref[:, 2 * D_BRANCH + col:2 * D_BRANCH + col + HEAD_PAD], 0.0, 0.0)
            put(qb_ref, pi, rope(norm(4 * D_BRANCH + col, 2)) * (SCALE * LOG2E), 0.0, 0.0)
            put(kb_ref, pi, rope(norm(5 * D_BRANCH + col, 3)), 0.0, 0.0)
            put(vb_ref, pi, z_ref[:, 6 * D_BRANCH + col:6 * D_BRANCH + col + HEAD_PAD], 0.0, 0.0)

    tok = lambda w: pl.BlockSpec((TOK_T, w), lambda i: (i, 0))
    head = pl.BlockSpec((N_HEADS, TOK_T, HEAD_PAD), lambda i: (0, i, 0))
    return pl.pallas_call(
        body, name="prep_fwd", grid=(S // TOK_T,),
        in_specs=[tok(N_MAIN), tok(128), pl.BlockSpec((4, 128), lambda i: (0, 0)), tok(128), tok(128), tok(128)],
        out_specs=[head] * 6, out_shape=[shp] * 6,
        compiler_params=_cp("parallel"),
    )(zm, c, qkg, rope_c, rope_a, rope_b)


def _pair(ref, pi, lo_half):
    return jnp.where(lo_half, ref[2 * pi], pltpu.roll(ref[2 * pi + 1], HEAD_DIM, 1))


def _mid_fwd(oa, ob, zm, h0, p, w_out, w_pg, w_ple, g2):
    S = h0.shape[0]
    p, p_lead = p

    def body(oa_ref, ob_ref, ga_ref, gb_ref, h0_ref, p_ref, wo_ref, wg_ref, wp_ref, g2_ref,
             y_ref, h1_ref, h2_ref, u2_ref, e_ref, gate_ref):
        lane = lax.broadcasted_iota(jnp.int32, (TOK_T, HEAD_PAD), 1)
        lo_half = lane < HEAD_DIM
        parts = []
        for o_ref, g_ref in ((oa_ref, ga_ref), (ob_ref, gb_ref)):
            for pi in range(N_HEADS // 2):
                g = g_ref[:, HEAD_PAD * pi:HEAD_PAD * (pi + 1)]
                parts.append((_pair(o_ref, pi, lo_half) * (g * _sigmoid(g))).astype(BF16))
        y = jnp.concatenate(parts, axis=1)
        y_ref[...] = y
        h1 = h0_ref[...] + _dot(y, wo_ref[...])
        h1_ref[...] = h1
        r = lax.rsqrt(jnp.mean(h1 * h1, axis=-1, keepdims=True) + EPS)
        u2 = (h1 * r * g2_ref[...]).astype(BF16)
        u2_ref[...] = u2
        gate = _sigmoid(_dot(u2, wg_ref[...]))
        e = _dot(p_ref[...].astype(BF16), wp_ref[...])
        e_ref[...] = e.astype(BF16)
        gate_ref[...] = gate.astype(BF16)
        h2_ref[...] = h1 + e * gate

    tok = lambda w: pl.BlockSpec((TOK_T, w), lambda i: (i, 0))
    head = pl.BlockSpec((N_HEADS, TOK_T, HEAD_PAD), lambda i: (0, i, 0))
    full = lambda a, b: pl.BlockSpec((a, b), lambda i: (0, 0))
    act = lambda dt: jax.ShapeDtypeStruct((S, D_MODEL), dt)
    return pl.pallas_call(
        body, name="mid_fwd", grid=(S // TOK_T,),
        in_specs=[head, head,
                  pl.BlockSpec((TOK_T, D_BRANCH), lambda i: (i, 3)), pl.BlockSpec((TOK_T, D_BRANCH), lambda i: (i, 7)),
                  tok(D_MODEL), _slab_spec(p_lead, (TOK_T, PLE_DIM), lambda i: (i, 0)), full(D_MODEL, D_MODEL),
                  full(D_MODEL, D_MODEL), full(PLE_DIM, D_MODEL), full(1, D_MODEL)],
        out_specs=[tok(D_MODEL)] * 6,
        out_shape=[act(BF16), act(F32), act(F32), act(BF16), act(BF16), act(BF16)],
        compiler_params=_cp("parallel"),
    )(oa, ob, zm, zm, h0, p, w_out, w_pg, w_ple, g2)


def _loss_fwd_bwd(y, t):
    S = y.shape[0]

    def body(y_ref, t_ref, dy_ref, loss_ref):
        @pl.when(pl.program_id(0) == 0)
        def _():
            loss_ref[...] = jnp.zeros_like(loss_ref)

        err = y_ref[...] - t_ref[...]
        dy_ref[...] = err * (1.0 / D_MODEL)
        part = jnp.sum(jnp.sum(err * err, axis=1, keepdims=True), axis=0, keepdims=True)
        loss_ref[...] += part * (0.5 / D_MODEL)

    tok = pl.BlockSpec((TOK_T, D_MODEL), lambda i: (i, 0))
    return pl.pallas_call(
        body, name="loss", grid=(S // TOK_T,),
        in_specs=[tok, tok], out_specs=[tok, pl.BlockSpec((8, 128), lambda i: (0, 0))],
        out_shape=[jax.ShapeDtypeStruct((S, D_MODEL), F32), jax.ShapeDtypeStruct((8, 128), F32)],
        compiler_params=_cp("arbitrary"),
    )(y, t)


def _bias_tables(full_range):
    T = ATT_T
    nb = 1 if full_range else DILATED_PATTERNS[-1][0] // T + 1
    r = lax.broadcasted_iota(jnp.int32, (nb, T, T), 2)
    c = lax.broadcasted_iota(jnp.int32, (nb, T, T), 1)
    b = lax.broadcasted_iota(jnp.int32, (nb, T, T), 0)
    delta = T * b + r - c
    if full_range:
        bias = jnp.where(delta >= 0, 0.0, NEG).astype(F32)
    else:
        mult = jnp.zeros((nb, T, T), F32)
        for window, dil in DILATED_PATTERNS:
            ok = (delta >= 0) & (delta <= window) & (delta % dil == 0)
            mult = mult + ok.astype(F32)
        bias = jnp.where(mult > 0, jnp.log2(jnp.maximum(mult, 1.0)), NEG).astype(F32)
    return bias


def _call_with_rider(body, name, grid, rider, in_specs, out_specs, out_shape, scratch_shapes, operands,
                     semantics=("parallel", "arbitrary")):
    if rider is None:
        return pl.pallas_call(body, name=name, grid=grid, in_specs=in_specs, out_specs=out_specs,
                              out_shape=out_shape, scratch_shapes=scratch_shapes,
                              compiler_params=_cp(*semantics))(*operands)
    n, n_in, n_out = rider.n, len(in_specs), len(out_specs)

    def wrapped(*refs):
        ins, r_ins = refs[:n_in], refs[n_in:n_in + n]
        outs, r_outs = refs[n_in + n:n_in + n + n_out], refs[n_in + n + n_out:n_in + 2 * n + n_out]
        scratch, sems = refs[n_in + 2 * n + n_out:-3], refs[-3:]
        step = [pl.program_id(a) for a in range(len(grid))]

        @pl.when(functools.reduce(jnp.logical_and, [s == 0 for s in step]))
        def _():
            rider.start(r_ins, r_outs, sems)

        body(*ins, *outs, *scratch)

        @pl.when(functools.reduce(jnp.logical_and, [s == g - 1 for s, g in zip(step, grid)]))
        def _():
            rider.wait(r_ins, r_outs, sems)

    return pl.pallas_call(
        wrapped, name=name, grid=grid, in_specs=list(in_specs) + rider.in_specs,
        out_specs=list(out_specs) + rider.out_specs, out_shape=list(out_shape) + rider.out_shapes,
        scratch_shapes=list(scratch_shapes) + rider.scratch_shapes,
        compiler_params=_cp(*["arbitrary"] * len(grid)))(*operands, *rider.arrays)


def _attn_fwd(q, k, v, table_t, full_range, name, rider=None):
    H, S, _ = q.shape
    T = ATT_T
    nb = table_t.shape[0]
    HB = ATT_FWD_HEADS
    KC = ATT_CHUNK
    chunks = [slice(c, c + KC) for c in range(0, T, KC)]
    fold = lambda x, op: functools.reduce(op, [x[r:r + 8] for r in range(0, KC, 8)])

    def body(q_ref, k_ref, v_ref, tab_ref, o_ref, lse_ref, *scratch):
        st_refs, pt_refs, acc_refs = scratch[:HB], scratch[HB:2 * HB], scratch[2 * HB:]
        i = pl.program_id(1)
        rows = lambda j: pl.ds(pl.multiple_of(j * T, T), T)

        def scores(hh, j):
            st_refs[hh][...] = _dot_nt(k_ref[hh, rows(j), :], q_ref[hh])

        def block(j, b, nxt, stats):
            out = []
            for hh, (m, l) in enumerate(stats):
                st_ref, pt_ref, acc_ref = st_refs[hh], pt_refs[hh], acc_refs[hh]
                mx = None
                for ch in chunks:
                    x = st_ref[ch, :]
                    if b is not None:
                        x = x + tab_ref[b, ch, :]
                        st_ref[ch, :] = x
                    x = fold(x, jnp.maximum)
                    mx = x if mx is None else jnp.maximum(mx, x)
                m_new = jnp.maximum(m, jnp.max(mx, axis=0, keepdims=True))
                alpha = jnp.exp2(m - m_new)
                ls = None
                for ch in chunks:
                    pc = jnp.exp2(st_ref[ch, :] - m_new)
                    pt_ref[ch, :] = pc.astype(BF16)
                    pc = fold(pc, jnp.add)
                    ls = pc if ls is None else ls + pc
                if nxt is not None:
                    scores(hh, nxt)
                acc_ref[...] = alpha * acc_ref[...] + _dot_tn(v_ref[hh, rows(j), :], pt_ref[...])
                out.append((m_new, alpha * l + jnp.sum(ls, axis=0, keepdims=True)))
            return tuple(out)

        lo = 0 if full_range else jnp.maximum(i - (nb - 1), 0)
        for hh in range(HB):
            acc_refs[hh][...] = jnp.zeros_like(acc_refs[hh])
            scores(hh, lo)
        stats = lax.fori_loop(lo, i, lambda j, st: block(j, None if full_range else i - j, j + 1, st),
                              ((jnp.full((1, T), NEG, F32), jnp.zeros((1, T), F32)),) * HB)
        stats = block(i, 0, None, stats)
        for hh, (m, l) in enumerate(stats):
            o_ref[hh] = (acc_refs[hh][...] * (1.0 / l)).T
            lse_ref[hh, 0] = m + jnp.log2(l)

    return _call_with_rider(
        body, name, (H // HB, S // T), rider,
        in_specs=[pl.BlockSpec((HB, T, HEAD_PAD), lambda h, i: (h, i, 0)),
                  pl.BlockSpec((HB, S, HEAD_PAD), lambda h, i: (h, 0, 0), pipeline_mode=pl.Buffered(1)),
                  pl.BlockSpec((HB, S, HEAD_PAD), lambda h, i: (h, 0, 0), pipeline_mode=pl.Buffered(1)),
                  pl.BlockSpec((nb, T, T), lambda h, i: (0, 0, 0), pipeline_mode=pl.Buffered(1))],
        out_specs=[pl.BlockSpec((HB, T, HEAD_PAD), lambda h, i: (h, i, 0)),
                   pl.BlockSpec((HB, 1, 1, T), lambda h, i: (h, i, 0, 0))],
        out_shape=[jax.ShapeDtypeStruct((H, S, HEAD_PAD), F32), jax.ShapeDtypeStruct((H, S // T, 1, T), F32)],
        scratch_shapes=([pltpu.VMEM((T, T), F32)] * HB + [pltpu.VMEM((T, T), BF16)] * HB
                        + [pltpu.VMEM((HEAD_PAD, T), F32)] * HB),
        operands=(q, k, v, table_t))


def _attn_bwd(q, k, v, do, lse, dd, table_t, full_range, name, rider=None):
    H, S, _ = q.shape
    T = ATT_T
    nq = S // T
    nb = table_t.shape[0]
    HB = ATT_BWD_HEADS
    KC = ATT_CHUNK
    chunks = [slice(c, c + KC) for c in range(0, T, KC)]

    def body(q_ref, do_ref, lse_ref, dd_ref, k_ref, v_ref, tab_ref, dq_hbm, dk_ref, dv_ref, *scratch):
        st_refs, dpt_refs, pt_refs, dst_refs = (scratch[n * HB:(n + 1) * HB] for n in range(4))
        dq_ref, dq_sem = scratch[4 * HB:]
        h = pl.program_id(0)
        j = pl.program_id(1)

        @pl.when(j == 0)
        def _():
            dq_ref[...] = jnp.zeros_like(dq_ref)

        dk_ref[...] = jnp.zeros_like(dk_ref)
        dv_ref[...] = jnp.zeros_like(dv_ref)

        def step(i, b):
            rows = pl.ds(pl.multiple_of(i * T, T), T)
            for hh in range(HB):
                st_refs[hh][...] = _dot_nt(k_ref[hh], q_ref[hh, rows, :])
                dpt_refs[hh][...] = _dot_nt(v_ref[hh], do_ref[hh, rows, :])
            for hh in range(HB):
                lse_i = lse_ref[hh, i]
                dd_i = dd_ref[hh, i]
                for ch in chunks:
                    x = st_refs[hh][ch, :]
                    if b is not None:
                        x = x + tab_ref[b, ch, :]
                    pc = jnp.exp2(x - lse_i)
                    pt_refs[hh][ch, :] = pc.astype(BF16)
                    dst_refs[hh][ch, :] = (pc * (dpt_refs[hh][ch, :] - dd_i)).astype(BF16)
                dv_ref[hh] += _dot(pt_refs[hh][...], do_ref[hh, rows, :])
                dk_ref[hh] += _dot(dst_refs[hh][...], q_ref[hh, rows, :])
                dq_ref[hh, rows, :] += _dot_tn(dst_refs[hh][...], k_ref[hh])

        step(j, 0)
        if full_range:
            pl.loop(j + 1, nq)(lambda i: step(i, None))
        else:
            pl.loop(j + 1, jnp.minimum(j + nb, nq))(lambda i: step(i, i - j))

        @pl.when(j == nq - 1)
        def _():
            out = pltpu.make_async_copy(dq_ref, dq_hbm.at[pl.ds(h * HB, HB)], dq_sem)
            out.start()
            out.wait()

    once = dict(pipeline_mode=pl.Buffered(1))
    per_head = pl.BlockSpec((HB, S, HEAD_PAD), lambda h, j: (h, 0, 0), **once)
    rows = pl.BlockSpec((HB, nq, 1, T), lambda h, j: (h, 0, 0, 0))
    blk = pl.BlockSpec((HB, T, HEAD_PAD), lambda h, j: (h, j, 0))
    shp = jax.ShapeDtypeStruct((H, S, HEAD_PAD), F32)
    return _call_with_rider(
        body, name, (H // HB, nq), rider,
        in_specs=[per_head, per_head, rows, rows, blk, blk,
                  pl.BlockSpec((nb, T, T), lambda h, j: (0, 0, 0), **once)],
        out_specs=[pl.BlockSpec(memory_space=pltpu.HBM), blk, blk], out_shape=[shp, shp, shp],
        scratch_shapes=([pltpu.VMEM((T, T), F32)] * (2 * HB) + [pltpu.VMEM((T, T), BF16)] * (2 * HB)
                        + [pltpu.VMEM((HB, S, HEAD_PAD), F32), pltpu.SemaphoreType.DMA]),
        operands=(q, do, lse, dd, k, v, table_t))


def _mid_bwd(dh2, h1, e, gate, g2, w_pg, w_out, oa, ob, zm):
    S = dh2.shape[0]

    def body(dh2_ref, h1_ref, e_ref, gate_ref, g2_ref, wg_ref, wo_ref, oa_ref, ob_ref, ga_ref, gb_ref,
             dh1_ref, dh1b_ref, de_ref, dpre_ref, doa_ref, dob_ref, dga_ref, dgb_ref, dd_ref, dg2_ref):
        @pl.when(pl.program_id(0) == 0)
        def _():
            dg2_ref[...] = jnp.zeros_like(dg2_ref)

        lane = lax.broadcasted_iota(jnp.int32, (TOK_T, HEAD_PAD), 1)
        lo_half = lane < HEAD_DIM
        dh2 = dh2_ref[...]
        gate = gate_ref[...]
        de_ref[...] = (dh2 * gate).astype(BF16)
        dpre = (dh2 * e_ref[...] * gate * (1.0 - gate)).astype(BF16)
        dpre_ref[...] = dpre
        du2 = _dot_nt(dpre, wg_ref[...])
        h1 = h1_ref[...]
        r = lax.rsqrt(jnp.mean(h1 * h1, axis=-1, keepdims=True) + EPS)
        xh = h1 * r
        a = du2 * g2_ref[...]
        dh1 = dh2 + r * (a - xh * jnp.mean(a * xh, axis=-1, keepdims=True))
        dg2_ref[...] += jnp.sum(du2 * xh, axis=0, keepdims=True)
        dh1_ref[...] = dh1
        dh1b = dh1.astype(BF16)
        dh1b_ref[...] = dh1b
        dy = _dot_nt(dh1b, wo_ref[...])
        dd = jnp.zeros((TOK_T, HEAD_PAD), F32)
        for bi, (o_ref, g_ref, do_ref, dg_ref) in enumerate(
                ((oa_ref, ga_ref, doa_ref, dga_ref), (ob_ref, gb_ref, dob_ref, dgb_ref))):
            for pi in range(N_HEADS // 2):
                col = bi * D_BRANCH + HEAD_PAD * pi
                dyp = dy[:, col:col + HEAD_PAD]
                g = g_ref[:, HEAD_PAD * pi:HEAD_PAD * (pi + 1)]
                sg = _sigmoid(g)
                dg_ref[:, HEAD_PAD * pi:HEAD_PAD * (pi + 1)] = (
                    dyp * _pair(o_ref, pi, lo_half) * (sg * (1.0 + g * (1.0 - sg)))).astype(BF16)
                dop = dyp * (g * sg)
                for hh, d_head in ((2 * pi, dop), (2 * pi + 1, pltpu.roll(dop, HEAD_DIM, 1))):
                    d_head = jnp.where(lo_half, d_head, 0.0)
                    do_ref[hh] = d_head.astype(BF16)
                    dsum = jnp.sum(d_head * o_ref[hh], axis=1, keepdims=True)
                    dd = dd + jnp.where(lane == bi * N_HEADS + hh, dsum, 0.0)
        dd_ref[...] = dd.T[:2 * N_HEADS, :]

    tok = lambda w: pl.BlockSpec((TOK_T, w), lambda i: (i, 0))
    head = pl.BlockSpec((N_HEADS, TOK_T, HEAD_PAD), lambda i: (0, i, 0))
    full = lambda a, b: pl.BlockSpec((a, b), lambda i: (0, 0))
    act = lambda w, dt: jax.ShapeDtypeStruct((S, w), dt)
    hshape = lambda w, dt: jax.ShapeDtypeStruct((N_HEADS, S, w), dt)
    return pl.pallas_call(
        body, name="mid_bwd", grid=(S // TOK_T,),
        in_specs=[tok(D_MODEL)] * 4 + [full(1, D_MODEL), full(D_MODEL, D_MODEL), full(D_MODEL, D_MODEL), head, head,
                                      pl.BlockSpec((TOK_T, D_BRANCH), lambda i: (i, 3)),
                                      pl.BlockSpec((TOK_T, D_BRANCH), lambda i: (i, 7))],
        out_specs=[tok(D_MODEL)] * 4 + [head, head, tok(D_BRANCH), tok(D_BRANCH),
                                       pl.BlockSpec((2 * N_HEADS, TOK_T), lambda i: (0, i)), full(1, D_MODEL)],
        out_shape=[act(D_MODEL, F32), act(D_MODEL, BF16), act(D_MODEL, BF16), act(D_MODEL, BF16),
                   hshape(HEAD_PAD, BF16), hshape(HEAD_PAD, BF16), act(D_BRANCH, BF16), act(D_BRANCH, BF16),
                   jax.ShapeDtypeStruct((2 * N_HEADS, S), F32), jax.ShapeDtypeStruct((1, D_MODEL), F32)],
        compiler_params=_cp("arbitrary"),
    )(dh2, h1, e, gate, g2, w_pg, w_out, oa, ob, zm, zm)


def _prep_bwd(dqa, dka, dva, dqb, dkb, dvb, zm, qkg, rope_c, rope_a, rope_b, dga, dgb):
    S = zm.shape[0]

    def body(dqa_ref, dka_ref, dva_ref, dqb_ref, dkb_ref, dvb_ref, z_ref, g_ref, rc_ref, ra_ref, rb_ref,
             dga_ref, dgb_ref, dz_ref, dc_ref, dqkg_ref):
        @pl.when(pl.program_id(0) == 0)
        def _():
            dqkg_ref[...] = jnp.zeros_like(dqkg_ref)

        lane = lax.broadcasted_iota(jnp.int32, (TOK_T, HEAD_PAD), 1)
        lo_half = lane < HEAD_DIM
        rc, ra, rb = rc_ref[...], ra_ref[...], rb_ref[...]
        same_head = _same_head()

        def unrope(dy):
            return dy * rc + pltpu.roll(dy * ra, ROPE_HALF, 1) + pltpu.roll(dy * rb, HEAD_PAD - ROPE_HALF, 1)

        def norm_bwd(col, gi, dy):
            x = z_ref[:, col:col + HEAD_PAD]
            r = _pair_rsqrt(x, same_head)
            xh = x * r
            dqkg_ref[gi:gi + 1, :] += jnp.sum(dy * xh, axis=0, keepdims=True)
            a = dy * g_ref[gi:gi + 1, :]
            dz_ref[:, col:col + HEAD_PAD] = (r * (a - xh * _pair_mean(a * xh, same_head))).astype(BF16)

        dc = jnp.zeros((TOK_T, HEAD_PAD), F32)
        for pi in range(N_HEADS // 2):
            col = HEAD_PAD * pi
            norm_bwd(col, 0, _pair(dqa_ref, pi, lo_half) * SCALE)
            norm_bwd(D_BRANCH + col, 1, _pair(dka_ref, pi, lo_half) * LN2)
            dz_ref[:, 2 * D_BRANCH + col:2 * D_BRANCH + col + HEAD_PAD] = _pair(dva_ref, pi, lo_half).astype(BF16)
            norm_bwd(4 * D_BRANCH + col, 2, unrope(_pair(dqb_ref, pi, lo_half) * SCALE))
            norm_bwd(5 * D_BRANCH + col, 3, unrope(_pair(dkb_ref, pi, lo_half) * LN2))
            dz_ref[:, 6 * D_BRANCH + col:6 * D_BRANCH + col + HEAD_PAD] = _pair(dvb_ref, pi, lo_half).astype(BF16)
            for hh in (2 * pi, 2 * pi + 1):
                dch = dka_ref[hh][:, HEAD_DIM:HEAD_DIM + 1] + dqa_ref[hh][:, HEAD_DIM + 3:HEAD_DIM + 4]
                dc = dc + jnp.where(lane == hh, dch, 0.0)
        dz_ref[:, 3 * D_BRANCH:4 * D_BRANCH] = dga_ref[...]
        dz_ref[:, 7 * D_BRANCH:8 * D_BRANCH] = dgb_ref[...]
        dc_ref[...] = dc

    tok = lambda w: pl.BlockSpec((TOK_T, w), lambda i: (i, 0))
    head = pl.BlockSpec((N_HEADS, TOK_T, HEAD_PAD), lambda i: (0, i, 0))
    return pl.pallas_call(
        body, name="prep_bwd", grid=(S // TOK_T,),
        in_specs=[head] * 6 + [tok(N_MAIN), pl.BlockSpec((4, 128), lambda i: (0, 0)), tok(128), tok(128), tok(128),
                               tok(D_BRANCH), tok(D_BRANCH)],
        out_specs=[tok(N_MAIN), tok(128), pl.BlockSpec((4, 128), lambda i: (0, 0))],
        out_shape=[jax.ShapeDtypeStruct((S, N_MAIN), BF16), jax.ShapeDtypeStruct((S, 128), F32),
                   jax.ShapeDtypeStruct((4, 128), F32)],
        compiler_params=_cp("arbitrary"),
    )(dqa, dka, dva, dqb, dkb, dvb, zm, qkg, rope_c, rope_a, rope_b, dga, dgb)


def _inproj_bwd(dzm, dzf, wm, wf, h0, dh1, g, rider=None):
    S = h0.shape[0]

    def body(dzm_ref, dzf_ref, wm_ref, wf_ref, h_ref, dh1_ref, g_ref, dh0_ref, dg_ref):
        @pl.when(pl.program_id(0) == 0)
        def _():
            dg_ref[...] = jnp.zeros_like(dg_ref)

        du = _dot_nt(dzm_ref[...], wm_ref[...]) + _dot_nt(dzf_ref[...], wf_ref[...])
        x = h_ref[...]
        r = lax.rsqrt(jnp.mean(x * x, axis=-1, keepdims=True) + EPS)
        xh = x * r
        a = du * g_ref[...]
        dh0_ref[...] = dh1_ref[...] + r * (a - xh * jnp.mean(a * xh, axis=-1, keepdims=True))
        dg_ref[...] += jnp.sum(du * xh, axis=0, keepdims=True)

    tok = lambda w: pl.BlockSpec((TOK_T, w), lambda i: (i, 0))
    full = lambda a, b: pl.BlockSpec((a, b), lambda i: (0, 0))
    return _call_with_rider(
        body, "inproj_bwd", (S // TOK_T,), rider,
        in_specs=[tok(N_MAIN), tok(128), full(D_MODEL, N_MAIN), full(D_MODEL, 128), tok(D_MODEL), tok(D_MODEL),
                  full(1, D_MODEL)],
        out_specs=[tok(D_MODEL), full(1, D_MODEL)],
        out_shape=[jax.ShapeDtypeStruct((S, D_MODEL), F32), jax.ShapeDtypeStruct((1, D_MODEL), F32)],
        scratch_shapes=[], operands=(dzm, dzf, wm, wf, h0, dh1, g), semantics=("arbitrary",))


def _wgrad(a, b, name, a_lead=()):
    S, M = a.shape[len(a_lead):]
    N = b.shape[1]
    tn = min(N, 2048)
    ts = 512
    last = S // ts - 1

    def body(a_ref, b_ref, o_ref, acc_ref):
        @pl.when(pl.program_id(1) == 0)
        def _():
            acc_ref[...] = jnp.zeros_like(acc_ref)

        acc_ref[...] += _dot_tn(a_ref[...].astype(BF16), b_ref[...])

        @pl.when(pl.program_id(1) == last)
        def _():
            o_ref[...] = acc_ref[...].astype(BF16)

    return pl.pallas_call(
        body, name=name, grid=(N // tn, S // ts),
        in_specs=[_slab_spec(a_lead, (ts, M), lambda n, s: (s, 0)), pl.BlockSpec((ts, tn), lambda n, s: (s, n))],
        out_specs=pl.BlockSpec((M, tn), lambda n, s: (0, n)),
        out_shape=jax.ShapeDtypeStruct((M, N), BF16),
        scratch_shapes=[pltpu.VMEM((M, tn), F32)],
        compiler_params=_cp("parallel", "arbitrary"),
    )(a, b)


def _rope_tables(positions):
    inv_freq = ROPE_THETA ** (-jnp.arange(ROPE_HALF, dtype=F32) / ROPE_HALF)
    ang = positions.astype(F32)[:, None] * inv_freq
    cos, sin = jnp.cos(ang), jnp.sin(ang)
    S = positions.shape[0]
    one, zero = jnp.ones((S, HEAD_DIM - 2 * ROPE_HALF), F32), jnp.zeros((S, HEAD_DIM - 2 * ROPE_HALF), F32)
    z8 = jnp.zeros((S, ROPE_HALF), F32)
    rc = jnp.concatenate([cos, cos, one], axis=1)
    ra = jnp.concatenate([-sin, z8, zero], axis=1)
    rb = jnp.concatenate([z8, sin, zero], axis=1)
    return tuple(jnp.tile(t, (1, 2)) for t in (rc, ra, rb))


def _in_operands(w_in):
    w_in = w_in.astype(BF16)
    wm = jnp.concatenate([w_in[:, :4 * D_BRANCH], w_in[:, 4 * D_BRANCH + N_HEADS:]], axis=1)
    wf = jnp.pad(w_in[:, 4 * D_BRANCH:4 * D_BRANCH + N_HEADS], ((0, 0), (0, 128 - N_HEADS)))
    return dict(wm=wm, wf=wf)


def _layer_weights(w_in, w_out, w_ple, w_pg):
    return dict(_in_operands(w_in), w_out=w_out.astype(BF16), w_ple=w_ple.astype(BF16), w_pg=w_pg.astype(BF16))


def _row(v, width=128):
    v = v.reshape(1, -1).astype(F32)
    return jnp.pad(v, ((0, 0), (0, width - v.shape[1])))


def _layer_fwd(h0, p, rope, tabs, w, norm_g, b_f, qk_g, ple_g, rider=None, late=None):
    g1 = norm_g.reshape(1, D_MODEL)
    g2 = ple_g.reshape(1, D_MODEL)
    qkg = jnp.tile(qk_g, (1, 2))
    bf = _row(b_f)
    zm, zf, u = _inproj_fwd(h0, g1, w["wm"], w["wf"])
    c = _forget_cumsum(zf, bf, tabs["tril"])
    qa, ka, va, qb, kb, vb = _prep_fwd(zm, c, qkg, *rope)
    oa, lse_a, *arrivals = _attn_fwd(qa, ka, va, tabs["fox"], True, "fox_fwd", rider)
    if late is not None:
        w = {**w, **late(arrivals)}
    ob, lse_b = _attn_fwd(qb, kb, vb, tabs["dil"], False, "dil_fwd")
    y, h1, h2, u2, e, gate = _mid_fwd(oa, ob, zm, h0, p, w["w_out"], w["w_pg"], w["w_ple"], g2)
    saved = dict(h0=h0, p=p, zm=zm, zf=zf, u=u, qa=qa, ka=ka, va=va, qb=qb, kb=kb, vb=vb, oa=oa, ob=ob,
                 lse_a=lse_a, lse_b=lse_b, y=y, h1=h1, u2=u2, e=e, gate=gate, g1=g1, g2=g2, qkg=qkg, bf=bf, w=w)
    return h2, saved, arrivals


def _layer_bwd(dh2, sv, rope, tabs, make_rider=None, make_last_rider=None):
    S = dh2.shape[0]
    nq = S // ATT_T
    w = sv["w"]
    rows = lambda a: a.reshape(N_HEADS, nq, 1, ATT_T)
    (dh1, dh1b, de, dpre, doa, dob, dga, dgb, dd, dg2) = _mid_bwd(
        dh2, sv["h1"], sv["e"], sv["gate"], sv["g2"], w["w_pg"], w["w_out"], sv["oa"], sv["ob"], sv["zm"])
    dda, ddb = dd[:N_HEADS], dd[N_HEADS:]
    early = dict(w_out=_wgrad(sv["y"], dh1b, "wgrad_out"), w_ple=_wgrad(sv["p"][0], de, "wgrad_ple", sv["p"][1]),
                 w_ple_gate=_wgrad(sv["u2"], dpre, "wgrad_gate"))
    rider = None if make_rider is None else make_rider(early)
    dqa, dka, dva, *arrivals = _attn_bwd(sv["qa"], sv["ka"], sv["va"], doa, sv["lse_a"], rows(dda), tabs["fox"],
                                         True, "fox_bwd", rider)
    dqb, dkb, dvb = _attn_bwd(sv["qb"], sv["kb"], sv["vb"], dob, sv["lse_b"], rows(ddb), tabs["dil"], False,
                              "dil_bwd")
    dzm, dc, dqkg = _prep_bwd(dqa, dka, dva, dqb, dkb, dvb, sv["zm"], sv["qkg"], *rope, dga, dgb)
    dzf, dbf = _forget_bwd(dc, sv["zf"], sv["bf"], tabs["triu"])
    dwm = _wgrad(sv["u"], dzm, "wgrad_in")
    dwf = _wgrad(sv["u"], dzf, "wgrad_f")
    dw_in = jnp.concatenate([dwm[:, :4 * D_BRANCH], dwf[:, :N_HEADS], dwm[:, 4 * D_BRANCH:]], axis=1)
    last_rider = None if make_last_rider is None else make_last_rider(dw_in)
    dh0, dg1, *last_arrivals = _inproj_bwd(dzm, dzf, w["wm"], w["wf"], sv["h0"], dh1, sv["g1"], last_rider)
    grads = dict(norm_g=dg1.reshape(D_MODEL), w_in=dw_in, b_f=dbf[0, :N_HEADS],
                 qk_norm_g=dqkg[:, :HEAD_DIM] + dqkg[:, HEAD_DIM:], ple_norm_g=dg2.reshape(D_MODEL), **early)
    return dh0, grads, arrivals + last_arrivals


def _tables():
    T = CUM_T
    r = lax.broadcasted_iota(jnp.int32, (T, T), 0)
    c = lax.broadcasted_iota(jnp.int32, (T, T), 1)
    return dict(fox=_bias_tables(True), dil=_bias_tables(False),
                tril=(c <= r).astype(BF16), triu=(c >= r).astype(BF16))


def _local_step(x, p, positions, target, layers, small):
    rope = _rope_tables(positions)
    tabs = _tables()
    ws = [_layer_weights(*lw) for lw in layers]
    h = x
    saved = []
    for w, lp, sm in zip(ws, p, small):
        h, sv, _ = _layer_fwd(h, (lp, ()), rope, tabs, w, *sm)
        saved.append(sv)
    dh, loss = _loss_fwd_bwd(h, target)
    grads = [None] * len(ws)
    for li in reversed(range(len(ws))):
        dh, grads[li], _ = _layer_bwd(dh, saved[li], rope, tabs)
    return loss[0, 0], dh, grads


def _peers():
    x, y, c = lax.axis_index("x"), lax.axis_index("y"), lax.axis_index("c")
    me = 4 * x + 2 * y + c
    flip = lambda v, bit: 1 - v if bit else v
    return me, [(flip(x, k & 4), flip(y, k & 2), flip(c, k & 1)) for k in range(1, N_DEV)]


def _sel(ref, kind, d):
    if kind == "whole":
        return ref
    if kind == "slot":
        return ref.at[d]
    block = pl.ds(pl.multiple_of(d * 128, 128), 128)
    return ref.at[block, :] if kind == "rows" else ref.at[:, block]


class _Pushes:
    def __init__(self, arrays, src_kinds, dst_kinds, out_shapes):
        self.arrays, self.n = list(arrays), len(arrays)
        self.src_kinds, self.dst_kinds = src_kinds, dst_kinds
        self.out_shapes = [jax.ShapeDtypeStruct(s, a.dtype) for s, a in zip(out_shapes, arrays)]
        hbm = pl.BlockSpec(memory_space=pltpu.HBM)
        self.in_specs, self.out_specs = [hbm] * self.n, [hbm] * self.n
        self.scratch_shapes = [pltpu.SemaphoreType.DMA((N_DEV - 1, self.n)),
                               pltpu.SemaphoreType.DMA((N_DEV - 1, self.n)), pltpu.SemaphoreType.DMA((self.n,))]

    def _copies(self, ins, outs, sems):
        send_sems, recv_sems, local_sems = sems
        me, peers = _peers()
        src = lambda a, d: _sel(ins[a], self.src_kinds[a], d)
        dst = lambda a: _sel(outs[a], self.dst_kinds[a], me)
        local = [pltpu.make_async_copy(src(a, me), dst(a), local_sems.at[a]) for a in range(self.n)]
        remote = [pltpu.make_async_remote_copy(
            src_ref=src(a, 4 * px + 2 * py + pc), dst_ref=dst(a), send_sem=send_sems.at[k, a],
            recv_sem=recv_sems.at[k, a], device_id=(px, py, pc), device_id_type=pl.DeviceIdType.MESH)
            for k, (px, py, pc) in enumerate(peers) for a in range(self.n)]
        return local + remote

    def start(self, ins, outs, sems):
        for cp in self._copies(ins, outs, sems):
            cp.start()

    def wait(self, ins, outs, sems):
        for cp in self._copies(ins, outs, sems):
            cp.wait()


def _exchange(name, pushes):
    n = pushes.n

    def body(*refs):
        pushes.start(refs[:n], refs[n:2 * n], refs[2 * n:])
        pushes.wait(refs[:n], refs[n:2 * n], refs[2 * n:])

    return pl.pallas_call(body, name=name, in_specs=pushes.in_specs, out_specs=pushes.out_specs,
                          out_shape=pushes.out_shapes, scratch_shapes=pushes.scratch_shapes)(*pushes.arrays)


def _gather_two_level(shard, name):
    def body(x_ref, out_ref, send_sems, recv_sems, local_sem):
        x, y, c = lax.axis_index("x"), lax.axis_index("y"), lax.axis_index("c")
        me, sibling = (x, y, c), (x, y, 1 - c)
        chips = [(1 - x, y), (x, 1 - y), (1 - x, 1 - y)]
        slot = lambda px, py, pc: out_ref.at[4 * px + 2 * py + pc]

        def copy(k, block, to, src=None):
            return pltpu.make_async_remote_copy(
                src_ref=slot(*block) if src is None else src, dst_ref=slot(*block), send_sem=send_sems.at[k],
                recv_sem=recv_sems.at[k], device_id=to, device_id_type=pl.DeviceIdType.MESH)

        mine = pltpu.make_async_copy(x_ref, slot(*me), local_sem)
        mine.start()
        first = [copy(0, me, sibling, src=x_ref)] + [copy(1 + j, me, (*chip, c), src=x_ref)
                                                     for j, chip in enumerate(chips)]
        for cp in first:
            cp.start()
        passed = [copy(4 + j, (*chip, c), sibling) for j, chip in enumerate(chips)]
        for j, chip in enumerate(chips):
            copy(1 + j, (*chip, c), me).wait_recv()
            passed[j].start()
        copy(0, sibling, me).wait_recv()
        for j, chip in enumerate(chips):
            copy(4 + j, (*chip, 1 - c), me).wait_recv()
        for cp in first + passed:
            cp.wait_send()
        mine.wait()

    hbm = pl.BlockSpec(memory_space=pltpu.HBM)
    return pl.pallas_call(
        body, name=name, in_specs=[hbm], out_specs=hbm,
        out_shape=jax.ShapeDtypeStruct((N_DEV,) + shard.shape, shard.dtype),
        scratch_shapes=[pltpu.SemaphoreType.DMA((N_DEV - 1,)), pltpu.SemaphoreType.DMA((N_DEV - 1,)),
                        pltpu.SemaphoreType.DMA],
    )(shard)


def _gather_pushes(shards, kinds):
    full = {"slot": lambda s: (N_DEV,) + s, "rows": lambda s: (N_DEV * s[0], s[1]),
            "cols": lambda s: (s[0], N_DEV * s[1])}
    return _Pushes(shards, ["whole"] * len(shards), kinds, [full[k](a.shape) for a, k in zip(shards, kinds)])


def _scatter_pushes(partials, kinds):
    part = {"slot": lambda s: s[1:], "rows": lambda s: (128, s[1]), "cols": lambda s: (s[0], 128),
            "whole": lambda s: s}
    return _Pushes(partials, kinds, ["slot"] * len(partials),
                   [(N_DEV,) + part[k](a.shape) for a, k in zip(partials, kinds)])


def _adamw(name, parts, w, m, v, rows):
    L, R, C = w.shape

    def body(p_ref, w_ref, m_ref, v_ref, g_ref, d_ref, nm_ref, nv_ref):
        g = p_ref[0, 0].astype(F32)
        for s in range(1, N_DEV):
            g = g + p_ref[s, 0].astype(F32)
        g_ref[0] = g
        nm = ADAM_B1 * m_ref[0] + (1.0 - ADAM_B1) * g
        nv = ADAM_B2 * v_ref[0] + (1.0 - ADAM_B2) * (g * g)
        nm_ref[0] = nm
        nv_ref[0] = nv
        m_hat = nm / (1.0 - ADAM_B1 ** ADAM_STEP)
        v_hat = nv / (1.0 - ADAM_B2 ** ADAM_STEP)
        d_ref[0] = -ADAM_LR * (m_hat / (jnp.sqrt(v_hat) + ADAM_EPS) + ADAM_WD * w_ref[0])

    blk = pl.BlockSpec((1, rows, C), lambda l, i: (l, i, 0))
    shp = jax.ShapeDtypeStruct((L, R, C), F32)
    return pl.pallas_call(
        body, name=name, grid=(L, R // rows),
        in_specs=[pl.BlockSpec((N_DEV, 1, rows, C), lambda l, i: (0, l, i, 0)), blk, blk, blk],
        out_specs=[blk] * 4, out_shape=[shp] * 4,
        compiler_params=_cp("parallel", "parallel"),
    )(parts, w, m, v)


SMALL_ROWS = 40
LOSS_ROW = 37


def _pack_small(norm_g, ple_g, qk_g, b_f, last_row):
    rows = lambda a: a.astype(F32).reshape(-1, 128)
    flat = jnp.concatenate([rows(norm_g), rows(ple_g), rows(qk_g), _row(b_f.reshape(-1)), last_row], axis=0)
    return jnp.pad(flat, ((0, SMALL_ROWS - flat.shape[0]), (0, 0)))


def _unpack_small(flat):
    return (flat[0:16].reshape(2, D_MODEL), flat[16:32].reshape(2, D_MODEL), flat[32:36].reshape(2, 4, HEAD_DIM),
            flat[36, :2 * N_HEADS].reshape(2, N_HEADS))


def kernel(x, p, positions, norm_g, w_in, b_f, qk_norm_g, w_out, w_ple, ple_norm_g, w_ple_gate, loss_target, m_norm_g, m_w_in, m_b_f, m_qk_norm_g, m_w_out, m_w_ple, m_ple_norm_g, m_w_ple_gate, v_norm_g, v_w_in, v_b_f, v_qk_norm_g, v_w_out, v_w_ple, v_ple_norm_g, v_w_ple_gate):
    bf16 = lambda a: a.astype(BF16)
    rows_in = W_IN_ROWS // 2
    flat_in = lambda a: bf16(a).reshape(rows_in, 128)
    full_in = lambda g: g.reshape(N_DEV, D_MODEL, W_IN_SHARD).transpose(1, 0, 2).reshape(D_MODEL, N_IN)
    small = [(norm_g[l], b_f[l], qk_norm_g[l], ple_norm_g[l]) for l in range(2)]
    rope = _rope_tables(positions[0])
    tabs = _tables()

    g_in0 = _gather_two_level(flat_in(w_in[0]), "gather_first")
    rest = _gather_pushes([flat_in(w_in[1])] + [bf16(a[l]) for l in range(2) for a in (w_out, w_ple, w_ple_gate)],
                          ["slot"] + ["rows", "cols", "rows"] * 2)
    late = lambda got: dict(w_out=got[1], w_ple=got[2], w_pg=got[3])
    h, sv0, got = _layer_fwd(x[0], (p, (0, 0)), rope, tabs, _in_operands(full_in(g_in0)), *small[0], rest, late)
    w1 = dict(_in_operands(full_in(got[0])), w_out=got[4], w_ple=got[5], w_pg=got[6])
    h, sv1, _ = _layer_fwd(h, (p, (1, 0)), rope, tabs, w1, *small[1])
    dh, loss = _loss_fwd_bwd(h, loss_target[0])
    dh, gr1, _ = _layer_bwd(dh, sv1, rope, tabs)

    by_dest = lambda d: d.reshape(D_MODEL, N_DEV, W_IN_SHARD).transpose(1, 0, 2).reshape(N_DEV, rows_in, 128)
    big = ("w_out", "w_ple", "w_ple_gate")
    riding = lambda early: _scatter_pushes([by_dest(gr1["w_in"])] + [gr1[n] for n in big] + [early[n] for n in big],
                                           ["slot"] + ["rows", "cols", "rows"] * 2)
    riding_last = lambda dw_in: _scatter_pushes([by_dest(dw_in)], ["slot"])
    dx, gr0, (r_in1, *r_big, r_in0) = _layer_bwd(dh, sv0, rope, tabs, riding, riding_last)
    grads = (gr0, gr1)
    stack = lambda name: jnp.stack([gl[name] for gl in grads], axis=0)
    small_part = _pack_small(stack("norm_g"), stack("ple_norm_g"), stack("qk_norm_g"), stack("b_f"),
                             _row(loss[0, 0].reshape(1)))
    (r_small,) = _exchange("exchange_small", _scatter_pushes([small_part], ["whole"]))
    r_in = jnp.concatenate([r_in0, r_in1], axis=1)
    r_out, r_ple, r_pg = (jnp.stack([r_big[3 + k], r_big[k]], axis=1) for k in range(3))

    zero_row = jnp.zeros((1, 128), F32)
    small_of = lambda ng, pg, qk, bf: _pack_small(ng, pg, qk, bf, zero_row)[None]
    flat = lambda a: a.reshape(1, W_IN_ROWS, 128)
    outs = dict(
        w_in=[o.reshape(w_in.shape) for o in
              _adamw("adamw_in", r_in[:, None], flat(w_in), flat(m_w_in), flat(v_w_in), W_IN_TILE)],
        w_out=_adamw("adamw_out", r_out, w_out, m_w_out, v_w_out, 128),
        w_ple=_adamw("adamw_ple", r_ple, w_ple, m_w_ple, v_w_ple, 256),
        w_pg=_adamw("adamw_gate", r_pg, w_ple_gate, m_w_ple_gate, v_w_ple_gate, 128),
        small=_adamw("adamw_small", r_small[:, None], small_of(norm_g, ple_norm_g, qk_norm_g, b_f),
                     small_of(m_norm_g, m_ple_norm_g, m_qk_norm_g, m_b_f),
                     small_of(v_norm_g, v_ple_norm_g, v_qk_norm_g, v_b_f), SMALL_ROWS))
    leaves = []
    for kind in range(4):
        ng, pg, qk, bf = _unpack_small(outs["small"][kind][0])
        leaves += [ng, outs["w_in"][kind], bf, qk, outs["w_out"][kind], outs["w_ple"][kind], pg, outs["w_pg"][kind]]
    return (outs["small"][0][0, LOSS_ROW, 0], dx[None], *leaves)
```

```python
import functools

import jax
import jax.numpy as jnp
from jax import lax
from jax.experimental import pallas as pl
from jax.experimental.pallas import tpu as pltpu

F32 = jnp.float32
BF16 = jnp.bfloat16

D_MODEL = 1024
HEAD_DIM = 64
N_HEADS = 8
HEAD_PAD = 128
D_BRANCH = N_HEADS * HEAD_DIM
N_MAIN = 8 * D_BRANCH
N_IN = N_MAIN + N_HEADS
PLE_DIM = 256
ROPE_THETA = 500000.0
ROPE_HALF = 8
EPS = 1e-6
NEG = -1e30
SCALE = HEAD_DIM ** -0.5
LOG2E = 1.4426950408889634
LN2 = 0.6931471805599453
DILATED_PATTERNS = ((128, 1), (512, 4), (2048, 16))
N_DEV = 8
W_IN_SHARD = N_IN // N_DEV
W_IN_ROWS = 2 * D_MODEL * W_IN_SHARD // 128
W_IN_TILE = W_IN_ROWS // 19

ADAM_LR = 0.001
ADAM_B1 = 0.9
ADAM_B2 = 0.999
ADAM_EPS = 1e-08
ADAM_WD = 0.01
ADAM_STEP = 10

ATT_T = 512
ATT_FWD_HEADS = 4
ATT_BWD_HEADS = 2
ATT_CHUNK = 32
TOK_T = 256
CUM_T = 512
VMEM_LIMIT = 56 * 1024 * 1024


def _slab_spec(lead, block, index):
    return pl.BlockSpec((None,) * len(lead) + block, lambda *g: (*lead, *index(*g)))


def _cp(*sem):
    return pltpu.CompilerParams(dimension_semantics=sem, vmem_limit_bytes=VMEM_LIMIT)


def _sigmoid(x):
    return 1.0 / (1.0 + jnp.exp(-x))


def _split3(x):
    hi = x.astype(BF16)
    r1 = x - hi.astype(F32)
    mid = r1.astype(BF16)
    lo = (r1 - mid.astype(F32)).astype(BF16)
    return hi, mid, lo


def _dot(a, b):
    return jnp.dot(a, b, preferred_element_type=F32)


def _dot_nt(a, b):
    return lax.dot_general(a, b, (((1,), (1,)), ((), ())), preferred_element_type=F32)


def _dot_tn(a, b):
    return lax.dot_general(a, b, (((0,), (0,)), ((), ())), preferred_element_type=F32)


def _inproj_fwd(h, g, wm, wf):
    S = h.shape[0]

    def body(h_ref, g_ref, wm_ref, wf_ref, zm_ref, zf_ref, u_ref):
        x = h_ref[...]
        r = lax.rsqrt(jnp.mean(x * x, axis=-1, keepdims=True) + EPS)
        u = (x * r * g_ref[...]).astype(BF16)
        u_ref[...] = u
        zm_ref[...] = _dot(u, wm_ref[...])
        zf_ref[...] = _dot(u, wf_ref[...])

    return pl.pallas_call(
        body, name="inproj_fwd", grid=(S // TOK_T,),
        in_specs=[pl.BlockSpec((TOK_T, D_MODEL), lambda i: (i, 0)),
                  pl.BlockSpec((1, D_MODEL), lambda i: (0, 0)),
                  pl.BlockSpec((D_MODEL, N_MAIN), lambda i: (0, 0)),
                  pl.BlockSpec((D_MODEL, 128), lambda i: (0, 0))],
        out_specs=[pl.BlockSpec((TOK_T, N_MAIN), lambda i: (i, 0)),
                   pl.BlockSpec((TOK_T, 128), lambda i: (i, 0)),
                   pl.BlockSpec((TOK_T, D_MODEL), lambda i: (i, 0))],
        out_shape=[jax.ShapeDtypeStruct((S, N_MAIN), F32), jax.ShapeDtypeStruct((S, 128), F32),
                   jax.ShapeDtypeStruct((S, D_MODEL), BF16)],
        compiler_params=_cp("parallel"),
    )(h, g, wm, wf)


def _log_sigmoid(x):
    return jnp.minimum(x, 0.0) - jnp.log(1.0 + jnp.exp(-jnp.abs(x)))


def _forget_cumsum(zf, bf, tri):
    S = zf.shape[0]

    def body(zf_ref, b_ref, tri_ref, c_ref, carry):
        @pl.when(pl.program_id(0) == 0)
        def _():
            carry[...] = jnp.zeros_like(carry)

        lf = _log_sigmoid(zf_ref[...] + b_ref[...])
        hi, mid, lo = _split3(lf)
        t = tri_ref[...]
        cs = _dot(t, hi) + _dot(t, mid) + _dot(t, lo) + carry[...]
        c_ref[...] = cs
        carry[...] = cs[CUM_T - 1:CUM_T, :]

    return pl.pallas_call(
        body, name="forget_cumsum", grid=(S // CUM_T,),
        in_specs=[pl.BlockSpec((CUM_T, 128), lambda i: (i, 0)),
                  pl.BlockSpec((1, 128), lambda i: (0, 0)),
                  pl.BlockSpec((CUM_T, CUM_T), lambda i: (0, 0))],
        out_specs=pl.BlockSpec((CUM_T, 128), lambda i: (i, 0)),
        out_shape=jax.ShapeDtypeStruct((S, 128), F32),
        scratch_shapes=[pltpu.VMEM((1, 128), F32)],
        compiler_params=_cp("arbitrary"),
    )(zf, bf, tri)


def _forget_bwd(dc, zf, bf, triu):
    S = zf.shape[0]
    n = S // CUM_T

    def body(dc_ref, zf_ref, b_ref, tri_ref, dzf_ref, db_ref, carry):
        @pl.when(pl.program_id(0) == 0)
        def _():
            carry[...] = jnp.zeros_like(carry)
            db_ref[...] = jnp.zeros_like(db_ref)

        hi, mid, lo = _split3(dc_ref[...])
        t = tri_ref[...]
        dlf = _dot(t, hi) + _dot(t, mid) + _dot(t, lo) + carry[...]
        carry[...] = dlf[0:1, :]
        dfa = dlf * (1.0 - _sigmoid(zf_ref[...] + b_ref[...]))
        dzf_ref[...] = dfa.astype(BF16)
        db_ref[...] += jnp.sum(dfa, axis=0, keepdims=True)

    return pl.pallas_call(
        body, name="forget_bwd", grid=(n,),
        in_specs=[pl.BlockSpec((CUM_T, 128), lambda i: (n - 1 - i, 0)),
                  pl.BlockSpec((CUM_T, 128), lambda i: (n - 1 - i, 0)),
                  pl.BlockSpec((1, 128), lambda i: (0, 0)),
                  pl.BlockSpec((CUM_T, CUM_T), lambda i: (0, 0))],
        out_specs=[pl.BlockSpec((CUM_T, 128), lambda i: (n - 1 - i, 0)),
                   pl.BlockSpec((1, 128), lambda i: (0, 0))],
        out_shape=[jax.ShapeDtypeStruct((S, 128), BF16), jax.ShapeDtypeStruct((1, 128), F32)],
        scratch_shapes=[pltpu.VMEM((1, 128), F32)],
        compiler_params=_cp("arbitrary"),
    )(dc, zf, bf, triu)


def _same_head():
    r = lax.broadcasted_iota(jnp.int32, (HEAD_PAD, HEAD_PAD), 0) // HEAD_DIM
    c = lax.broadcasted_iota(jnp.int32, (HEAD_PAD, HEAD_PAD), 1) // HEAD_DIM
    return (r == c).astype(BF16)


def _pair_mean(x, same_head):
    hi = x.astype(BF16)
    lo = (x - hi.astype(F32)).astype(BF16)
    return (_dot(hi, same_head) + _dot(lo, same_head)) * (1.0 / HEAD_DIM)


def _pair_rsqrt(x, same_head):
    return lax.rsqrt(_pair_mean(x * x, same_head) + EPS)


def _prep_fwd(zm, c, qkg, rope_c, rope_a, rope_b):
    S = zm.shape[0]
    shp = jax.ShapeDtypeStruct((N_HEADS, S, HEAD_PAD), BF16)

    def body(z_ref, c_ref, g_ref, rc_ref, ra_ref, rb_ref, qa_ref, ka_ref, va_ref, qb_ref, kb_ref, vb_ref):
        lane = lax.broadcasted_iota(jnp.int32, (TOK_T, HEAD_PAD), 1)
        lo_half = lane < HEAD_DIM
        aug = (lane >= HEAD_DIM) & (lane < HEAD_DIM + 3)
        q_pad = jnp.where(aug, -1.0, 0.0)
        cs = c_ref[...]
        rc, ra, rb = rc_ref[...], ra_ref[...], rb_ref[...]
        same_head = _same_head()

        def norm(col, gi):
            x = z_ref[:, col:col + HEAD_PAD]
            return x * _pair_rsqrt(x, same_head) * g_ref[gi:gi + 1, :]

        def rope(y):
            return y * rc + pltpu.roll(y, HEAD_PAD - ROPE_HALF, 1) * ra + pltpu.roll(y, ROPE_HALF, 1) * rb

        def put(ref, pi, y, pad_even, pad_odd):
            ref[2 * pi] = jnp.where(lo_half, y, pad_even).astype(BF16)
            ref[2 * pi + 1] = jnp.where(lo_half, pltpu.roll(y, HEAD_DIM, 1), pad_odd).astype(BF16)

        def k_pad(h):
            ch = cs[:, h:h + 1] * LOG2E
            hi = ch.astype(BF16).astype(F32)
            mid = (ch - hi).astype(BF16).astype(F32)
            lo = ch - hi - mid
            ones = jnp.where(lane == HEAD_DIM + 3, 1.0, 0.0)
            return jnp.where(lane == HEAD_DIM, hi, jnp.where(lane == HEAD_DIM + 1, mid,
                                                              jnp.where(lane == HEAD_DIM + 2, lo, ones)))

        for pi in range(N_HEADS // 2):
            col = HEAD_PAD * pi
            put(qa_ref, pi, norm(col, 0) * (SCALE * LOG2E), q_pad, q_pad)
            put(ka_ref, pi, norm(D_BRANCH + col, 1), k_pad(2 * pi), k_pad(2 * pi + 1))
            put(va_ref, pi, z_ref[:, 2 * D_BRANCH + col:2 * D_BRANCH + col + HEAD_PAD], 0.0, 0.0)
            put(qb_ref, pi, rope(norm(4 * D_BRANCH + col, 2)) * (SCALE * LOG2E), 0.0, 0.0)
            put(kb_ref, pi, rope(norm(5 * D_BRANCH + col, 3)), 0.0, 0.0)
            put(vb_ref, pi, z_ref[:, 6 * D_BRANCH + col:6 * D_BRANCH + col + HEAD_PAD], 0.0, 0.0)

    tok = lambda w: pl.BlockSpec((TOK_T, w), lambda i: (i, 0))
    head = pl.BlockSpec((N_HEADS, TOK_T, HEAD_PAD), lambda i: (0, i, 0))
    return pl.pallas_call(
        body, name="prep_fwd", grid=(S // TOK_T,),
        in_specs=[tok(N_MAIN), tok(128), pl.BlockSpec((4, 128), lambda i: (0, 0)), tok(128), tok(128), tok(128)],
        out_specs=[head] * 6, out_shape=[shp] * 6,
        compiler_params=_cp("parallel"),
    )(zm, c, qkg, rope_c, rope_a, rope_b)


def _pair(ref, pi, lo_half):
    return jnp.where(lo_half, ref[2 * pi], pltpu.roll(ref[2 * pi + 1], HEAD_DIM, 1))


def _mid_fwd(oa, ob, zm, h0, p, w_out, w_pg, w_ple, g2, target=None):
    S = h0.shape[0]
    p, p_lead = p

    def body(oa_ref, ob_ref, ga_ref, gb_ref, h0_ref, p_ref, wo_ref, wg_ref, wp_ref, g2_ref, *rest):
        t_ref, rest = (rest[0], rest[1:]) if target is not None else (None, rest)
        y_ref, h1_ref, h2_ref, u2_ref, e_ref, gate_ref, *loss_ref = rest
        parts = []
        for o_ref, g_ref in ((oa_ref, ga_ref), (ob_ref, gb_ref)):
            for pi in range(N_HEADS // 2):
                g = g_ref[:, HEAD_PAD * pi:HEAD_PAD * (pi + 1)]
                parts.append((o_ref[pi] * (g * _sigmoid(g))).astype(BF16))
        y = jnp.concatenate(parts, axis=1)
        y_ref[...] = y
        h1 = h0_ref[...] + _dot(y, wo_ref[...])
        h1_ref[...] = h1
        r = lax.rsqrt(jnp.mean(h1 * h1, axis=-1, keepdims=True) + EPS)
        u2 = (h1 * r * g2_ref[...]).astype(BF16)
        u2_ref[...] = u2
        gate = _sigmoid(_dot(u2, wg_ref[...]))
        e = _dot(p_ref[...].astype(BF16), wp_ref[...])
        e_ref[...] = e.astype(BF16)
        gate_ref[...] = gate.astype(BF16)
        h2 = h1 + e * gate
        if target is None:
            h2_ref[...] = h2
        else:
            @pl.when(pl.program_id(0) == 0)
            def _():
                loss_ref[0][...] = jnp.zeros_like(loss_ref[0])

            err = h2 - t_ref[...]
            h2_ref[...] = err * (1.0 / D_MODEL)
            part = jnp.sum(jnp.sum(err * err, axis=1, keepdims=True), axis=0, keepdims=True)
            loss_ref[0][...] += part * (0.5 / D_MODEL)

    tok = lambda w: pl.BlockSpec((TOK_T, w), lambda i: (i, 0))
    head = pl.BlockSpec((N_HEADS // 2, TOK_T, HEAD_PAD), lambda i: (0, i, 0))
    full = lambda a, b: pl.BlockSpec((a, b), lambda i: (0, 0))
    act = lambda dt: jax.ShapeDtypeStruct((S, D_MODEL), dt)
    fused = target is not None
    return pl.pallas_call(
        body, name="mid_fwd_loss" if fused else "mid_fwd", grid=(S // TOK_T,),
        in_specs=[head, head,
                  pl.BlockSpec((TOK_T, D_BRANCH), lambda i: (i, 3)), pl.BlockSpec((TOK_T, D_BRANCH), lambda i: (i, 7)),
                  tok(D_MODEL), _slab_spec(p_lead, (TOK_T, PLE_DIM), lambda i: (i, 0)), full(D_MODEL, D_MODEL),
                  full(D_MODEL, D_MODEL), full(PLE_DIM, D_MODEL), full(1, D_MODEL)] + [tok(D_MODEL)] * fused,
        out_specs=[tok(D_MODEL)] * 6 + [full(8, 128)] * fused,
        out_shape=[act(BF16), act(F32), act(F32), act(BF16), act(BF16), act(BF16)]
        + [jax.ShapeDtypeStruct((8, 128), F32)] * fused,
        compiler_params=_cp("arbitrary" if fused else "parallel"),
    )(oa, ob, zm, zm, h0, p, w_out, w_pg, w_ple, g2, *([target] * fused))


def _bias_tables(full_range):
    T = ATT_T
    nb = 1 if full_range else DILATED_PATTERNS[-1][0] // T + 1
    r = lax.broadcasted_iota(jnp.int32, (nb, T, T), 2)
    c = lax.broadcasted_iota(jnp.int32, (nb, T, T), 1)
    b = lax.broadcasted_iota(jnp.int32, (nb, T, T), 0)
    delta = T * b + r - c
    if full_range:
        bias = jnp.where(delta >= 0, 0.0, NEG).astype(F32)
    else:
        mult = jnp.zeros((nb, T, T), F32)
        for window, dil in DILATED_PATTERNS:
            ok = (delta >= 0) & (delta <= window) & (delta % dil == 0)
            mult = mult + ok.astype(F32)
        bias = jnp.where(mult > 0, jnp.log2(jnp.maximum(mult, 1.0)), NEG).astype(F32)
    return bias


def _call_with_rider(body, name, grid, rider, in_specs, out_specs, out_shape, scratch_shapes, operands,
                     semantics=("parallel", "arbitrary")):
    if rider is None:
        return pl.pallas_call(body, name=name, grid=grid, in_specs=in_specs, out_specs=out_specs,
                              out_shape=out_shape, scratch_shapes=scratch_shapes,
                              compiler_params=_cp(*semantics))(*operands)
    n, n_in, n_out = rider.n, len(in_specs), len(out_specs)

    def wrapped(*refs):
        ins, r_ins = refs[:n_in], refs[n_in:n_in + n]
        outs, r_outs = refs[n_in + n:n_in + n + n_out], refs[n_in + n + n_out:n_in + 2 * n + n_out]
        scratch, sems = refs[n_in + 2 * n + n_out:-3], refs[-3:]
        step = [pl.program_id(a) for a in range(len(grid))]

        @pl.when(functools.reduce(jnp.logical_and, [s == 0 for s in step]))
        def _():
            rider.start(r_ins, r_outs, sems)

        body(*ins, *outs, *scratch)

        @pl.when(functools.reduce(jnp.logical_and, [s == g - 1 for s, g in zip(step, grid)]))
        def _():
            rider.wait(r_ins, r_outs, sems)

    return pl.pallas_call(
        wrapped, name=name, grid=grid, in_specs=list(in_specs) + rider.in_specs,
        out_specs=list(out_specs) + rider.out_specs, out_shape=list(out_shape) + rider.out_shapes,
        scratch_shapes=list(scratch_shapes) + rider.scratch_shapes,
        compiler_params=_cp(*["arbitrary"] * len(grid)))(*operands, *rider.arrays)


def _attn_fwd(q, k, v, table_t, full_range, name, rider=None):
    H, S, _ = q.shape
    T = ATT_T
    nb = table_t.shape[0]
    HB = ATT_FWD_HEADS
    KC = ATT_CHUNK
    chunks = [slice(c, c + KC) for c in range(0, T, KC)]
    fold = lambda x, op: functools.reduce(op, [x[r:r + 8] for r in range(0, KC, 8)])

    def body(q_ref, k_ref, v_ref, tab_ref, o_ref, lse_ref, *scratch):
        st_refs, pt_refs, acc_refs = scratch[:HB], scratch[HB:2 * HB], scratch[2 * HB:]
        i = pl.program_id(1)
        rows = lambda j: pl.ds(pl.multiple_of(j * T, T), T)

        def scores(hh, j):
            st_refs[hh][...] = _dot_nt(k_ref[hh, rows(j), :], q_ref[hh])

        def block(j, b, nxt, stats):
            out = []
            for hh, (m, l) in enumerate(stats):
                st_ref, pt_ref, acc_ref = st_refs[hh], pt_refs[hh], acc_refs[hh]
                mx = None
                for ch in chunks:
                    x = st_ref[ch, :]
                    if b is not None:
                        x = x + tab_ref[b, ch, :]
                        st_ref[ch, :] = x
                    x = fold(x, jnp.maximum)
                    mx = x if mx is None else jnp.maximum(mx, x)
                m_new = jnp.maximum(m, jnp.max(mx, axis=0, keepdims=True))
                alpha = jnp.exp2(m - m_new)
                ls = None
                for ch in chunks:
                    pc = jnp.exp2(st_ref[ch, :] - m_new)
                    pt_ref[ch, :] = pc.astype(BF16)
                    pc = fold(pc, jnp.add)
                    ls = pc if ls is None else ls + pc
                if nxt is not None:
                    scores(hh, nxt)
                acc_ref[...] = alpha * acc_ref[...] + _dot_tn(v_ref[hh, rows(j), :], pt_ref[...])
                out.append((m_new, alpha * l + jnp.sum(ls, axis=0, keepdims=True)))
            return tuple(out)

        lo = 0 if full_range else jnp.maximum(i - (nb - 1), 0)
        for hh in range(HB):
            acc_refs[hh][...] = jnp.zeros_like(acc_refs[hh])
            scores(hh, lo)
        stats = lax.fori_loop(lo, i, lambda j, st: block(j, None if full_range else i - j, j + 1, st),
                              ((jnp.full((1, T), NEG, F32), jnp.zeros((1, T), F32)),) * HB)
        stats = block(i, 0, None, stats)
        o_t = [acc_refs[hh][...] * (1.0 / l) for hh, (m, l) in enumerate(stats)]
        for hh, (m, l) in enumerate(stats):
            lse_ref[hh, 0] = m + jnp.log2(l)
        for hp in range(HB // 2):
            o_ref[hp] = jnp.concatenate([o_t[2 * hp][:HEAD_DIM], o_t[2 * hp + 1][:HEAD_DIM]], axis=0).T

    return _call_with_rider(
        body, name, (H // HB, S // T), rider,
        in_specs=[pl.BlockSpec((HB, T, HEAD_PAD), lambda h, i: (h, i, 0)),
                  pl.BlockSpec((HB, S, HEAD_PAD), lambda h, i: (h, 0, 0), pipeline_mode=pl.Buffered(1)),
                  pl.BlockSpec((HB, S, HEAD_PAD), lambda h, i: (h, 0, 0), pipeline_mode=pl.Buffered(1)),
                  pl.BlockSpec((nb, T, T), lambda h, i: (0, 0, 0), pipeline_mode=pl.Buffered(1))],
        out_specs=[pl.BlockSpec((HB // 2, T, HEAD_PAD), lambda h, i: (h, i, 0)),
                   pl.BlockSpec((HB, 1, 1, T), lambda h, i: (h, i, 0, 0))],
        out_shape=[jax.ShapeDtypeStruct((H // 2, S, HEAD_PAD), F32), jax.ShapeDtypeStruct((H, S // T, 1, T), F32)],
        scratch_shapes=([pltpu.VMEM((T, T), F32)] * HB + [pltpu.VMEM((T, T), BF16)] * HB
                        + [pltpu.VMEM((HEAD_PAD, T), F32)] * HB),
        operands=(q, k, v, table_t))


def _attn_bwd(q, k, v, do, lse, dd, table_t, full_range, name, rider=None):
    H, S, _ = q.shape
    T = ATT_T
    nq = S // T
    nb = table_t.shape[0]
    HB = ATT_BWD_HEADS
    KC = ATT_CHUNK
    chunks = [slice(c, c + KC) for c in range(0, T, KC)]

    def body(q_ref, do_ref, lse_ref, dd_ref, k_ref, v_ref, tab_ref, dq_hbm, dk_ref, dv_ref, *scratch):
        st_refs, dpt_refs, pt_refs, dst_refs = (scratch[n * HB:(n + 1) * HB] for n in range(4))
        dq_ref, dq_sem = scratch[4 * HB:]
        h = pl.program_id(0)
        j = pl.program_id(1)

        @pl.when(j == 0)
        def _():
            dq_ref[...] = jnp.zeros_like(dq_ref)

        dk_ref[...] = jnp.zeros_like(dk_ref)
        dv_ref[...] = jnp.zeros_like(dv_ref)

        def step(i, b):
            rows = pl.ds(pl.multiple_of(i * T, T), T)
            for hh in range(HB):
                st_refs[hh][...] = _dot_nt(k_ref[hh], q_ref[hh, rows, :])
                dpt_refs[hh][...] = _dot_nt(v_ref[hh], do_ref[hh, rows, :])
            for hh in range(HB):
                lse_i = lse_ref[hh, i]
                dd_i = dd_ref[hh, i]
                for ch in chunks:
                    x = st_refs[hh][ch, :]
                    if b is not None:
                        x = x + tab_ref[b, ch, :]
                    pc = jnp.exp2(x - lse_i)
                    pt_refs[hh][ch, :] = pc.astype(BF16)
                    dst_refs[hh][ch, :] = (pc * (dpt_refs[hh][ch, :] - dd_i)).astype(BF16)
                dv_ref[hh] += _dot(pt_refs[hh][...], do_ref[hh, rows, :])
                dk_ref[hh] += _dot(dst_refs[hh][...], q_ref[hh, rows, :])
                dq_ref[hh, rows, :] += _dot_tn(dst_refs[hh][...], k_ref[hh])

        step(j, 0)
        if full_range:
            pl.loop(j + 1, nq)(lambda i: step(i, None))
        else:
            pl.loop(j + 1, jnp.minimum(j + nb, nq))(lambda i: step(i, i - j))

        @pl.when(j == nq - 1)
        def _():
            out = pltpu.make_async_copy(dq_ref, dq_hbm.at[pl.ds(h * HB, HB)], dq_sem)
            out.start()
            out.wait()

    once = dict(pipeline_mode=pl.Buffered(1))
    per_head = pl.BlockSpec((HB, S, HEAD_PAD), lambda h, j: (h, 0, 0), **once)
    rows = pl.BlockSpec((HB, nq, 1, T), lambda h, j: (h, 0, 0, 0))
    blk = pl.BlockSpec((HB, T, HEAD_PAD), lambda h, j: (h, j, 0))
    shp = jax.ShapeDtypeStruct((H, S, HEAD_PAD), F32)
    return _call_with_rider(
        body, name, (H // HB, nq), rider,
        in_specs=[per_head, per_head, rows, rows, blk, blk,
                  pl.BlockSpec((nb, T, T), lambda h, j: (0, 0, 0), **once)],
        out_specs=[pl.BlockSpec(memory_space=pltpu.HBM), blk, blk], out_shape=[shp, shp, shp],
        scratch_shapes=([pltpu.VMEM((T, T), F32)] * (2 * HB) + [pltpu.VMEM((T, T), BF16)] * (2 * HB)
                        + [pltpu.VMEM((HB, S, HEAD_PAD), F32), pltpu.SemaphoreType.DMA]),
        operands=(q, do, lse, dd, k, v, table_t))


def _mid_bwd(dh2, h1, e, gate, g2, w_pg, w_out, oa, ob, zm):
    S = dh2.shape[0]

    def body(dh2_ref, h1_ref, e_ref, gate_ref, g2_ref, wg_ref, wo_ref, oa_ref, ob_ref, ga_ref, gb_ref,
             dh1_ref, dh1b_ref, de_ref, dpre_ref, doa_ref, dob_ref, dga_ref, dgb_ref, dd_ref, dg2_ref):
        @pl.when(pl.program_id(0) == 0)
        def _():
            dg2_ref[...] = jnp.zeros_like(dg2_ref)

        lane = lax.broadcasted_iota(jnp.int32, (TOK_T, HEAD_PAD), 1)
        lo_half = lane < HEAD_DIM
        dh2 = dh2_ref[...]
        gate = gate_ref[...]
        de_ref[...] = (dh2 * gate).astype(BF16)
        dpre = (dh2 * e_ref[...] * gate * (1.0 - gate)).astype(BF16)
        dpre_ref[...] = dpre
        du2 = _dot_nt(dpre, wg_ref[...])
        h1 = h1_ref[...]
        r = lax.rsqrt(jnp.mean(h1 * h1, axis=-1, keepdims=True) + EPS)
        xh = h1 * r
        a = du2 * g2_ref[...]
        dh1 = dh2 + r * (a - xh * jnp.mean(a * xh, axis=-1, keepdims=True))
        dg2_ref[...] += jnp.sum(du2 * xh, axis=0, keepdims=True)
        dh1_ref[...] = dh1
        dh1b = dh1.astype(BF16)
        dh1b_ref[...] = dh1b
        dy = _dot_nt(dh1b, wo_ref[...])
        dd = jnp.zeros((TOK_T, HEAD_PAD), F32)
        for bi, (o_ref, g_ref, do_ref, dg_ref) in enumerate(
                ((oa_ref, ga_ref, doa_ref, dga_ref), (ob_ref, gb_ref, dob_ref, dgb_ref))):
            for pi in range(N_HEADS // 2):
                col = bi * D_BRANCH + HEAD_PAD * pi
                dyp = dy[:, col:col + HEAD_PAD]
                g = g_ref[:, HEAD_PAD * pi:HEAD_PAD * (pi + 1)]
                sg = _sigmoid(g)
                o_pair = o_ref[pi]
                dg_ref[:, HEAD_PAD * pi:HEAD_PAD * (pi + 1)] = (
                    dyp * o_pair * (sg * (1.0 + g * (1.0 - sg)))).astype(BF16)
                dop = dyp * (g * sg)
                prod = dop * o_pair
                for hh, d_head, mine in ((2 * pi, dop, lo_half),
                                         (2 * pi + 1, pltpu.roll(dop, HEAD_DIM, 1), ~lo_half)):
                    do_ref[hh] = jnp.where(lo_half, d_head, 0.0).astype(BF16)
                    dsum = jnp.sum(jnp.where(mine, prod, 0.0), axis=1, keepdims=True)
                    dd = dd + jnp.where(lane == bi * N_HEADS + hh, dsum, 0.0)
        dd_ref[...] = dd.T[:2 * N_HEADS, :]

    tok = lambda w: pl.BlockSpec((TOK_T, w), lambda i: (i, 0))
    head = pl.BlockSpec((N_HEADS, TOK_T, HEAD_PAD), lambda i: (0, i, 0))
    pairs = pl.BlockSpec((N_HEADS // 2, TOK_T, HEAD_PAD), lambda i: (0, i, 0))
    full = lambda a, b: pl.BlockSpec((a, b), lambda i: (0, 0))
    act = lambda w, dt: jax.ShapeDtypeStruct((S, w), dt)
    hshape = lambda w, dt: jax.ShapeDtypeStruct((N_HEADS, S, w), dt)
    return pl.pallas_call(
        body, name="mid_bwd", grid=(S // TOK_T,),
        in_specs=[tok(D_MODEL)] * 4 + [full(1, D_MODEL), full(D_MODEL, D_MODEL), full(D_MODEL, D_MODEL), pairs, pairs,
                                      pl.BlockSpec((TOK_T, D_BRANCH), lambda i: (i, 3)),
                                      pl.BlockSpec((TOK_T, D_BRANCH), lambda i: (i, 7))],
        out_specs=[tok(D_MODEL)] * 4 + [head, head, tok(D_BRANCH), tok(D_BRANCH),
                                       pl.BlockSpec((2 * N_HEADS, TOK_T), lambda i: (0, i)), full(1, D_MODEL)],
        out_shape=[act(D_MODEL, F32), act(D_MODEL, BF16), act(D_MODEL, BF16), act(D_MODEL, BF16),
                   hshape(HEAD_PAD, BF16), hshape(HEAD_PAD, BF16), act(D_BRANCH, BF16), act(D_BRANCH, BF16),
                   jax.ShapeDtypeStruct((2 * N_HEADS, S), F32), jax.ShapeDtypeStruct((1, D_MODEL), F32)],
        compiler_params=_cp("arbitrary"),
    )(dh2, h1, e, gate, g2, w_pg, w_out, oa, ob, zm, zm)


def _prep_bwd(dqa, dka, dva, dqb, dkb, dvb, zm, qkg, rope_c, rope_a, rope_b, dga, dgb):
    S = zm.shape[0]

    def body(dqa_ref, dka_ref, dva_ref, dqb_ref, dkb_ref, dvb_ref, z_ref, g_ref, rc_ref, ra_ref, rb_ref,
             dga_ref, dgb_ref, dz_ref, dc_ref, dqkg_ref):
        @pl.when(pl.program_id(0) == 0)
        def _():
            dqkg_ref[...] = jnp.zeros_like(dqkg_ref)

        lane = lax.broadcasted_iota(jnp.int32, (TOK_T, HEAD_PAD), 1)
        lo_half = lane < HEAD_DIM
        rc, ra, rb = rc_ref[...], ra_ref[...], rb_ref[...]
        same_head = _same_head()

        def unrope(dy):
            return dy * rc + pltpu.roll(dy * ra, ROPE_HALF, 1) + pltpu.roll(dy * rb, HEAD_PAD - ROPE_HALF, 1)

        def norm_bwd(col, gi, dy):
            x = z_ref[:, col:col + HEAD_PAD]
            r = _pair_rsqrt(x, same_head)
            xh = x * r
            dqkg_ref[gi:gi + 1, :] += jnp.sum(dy * xh, axis=0, keepdims=True)
            a = dy * g_ref[gi:gi + 1, :]
            dz_ref[:, col:col + HEAD_PAD] = (r * (a - xh * _pair_mean(a * xh, same_head))).astype(BF16)

        dc = jnp.zeros((TOK_T, HEAD_PAD), F32)
        for pi in range(N_HEADS // 2):
            col = HEAD_PAD * pi
            norm_bwd(col, 0, _pair(dqa_ref, pi, lo_half) * SCALE)
            norm_bwd(D_BRANCH + col, 1, _pair(dka_ref, pi, lo_half) * LN2)
            dz_ref[:, 2 * D_BRANCH + col:2 * D_BRANCH + col + HEAD_PAD] = _pair(dva_ref, pi, lo_half).astype(BF16)
            norm_bwd(4 * D_BRANCH + col, 2, unrope(_pair(dqb_ref, pi, lo_half) * SCALE))
            norm_bwd(5 * D_BRANCH + col, 3, unrope(_pair(dkb_ref, pi, lo_half) * LN2))
            dz_ref[:, 6 * D_BRANCH + col:6 * D_BRANCH + col + HEAD_PAD] = _pair(dvb_ref, pi, lo_half).astype(BF16)
            for hh in (2 * pi, 2 * pi + 1):
                dch = dka_ref[hh][:, HEAD_DIM:HEAD_DIM + 1] + dqa_ref[hh][:, HEAD_DIM + 3:HEAD_DIM + 4]
                dc = dc + jnp.where(lane == hh, dch, 0.0)
        dz_ref[:, 3 * D_BRANCH:4 * D_BRANCH] = dga_ref[...]
        dz_ref[:, 7 * D_BRANCH:8 * D_BRANCH] = dgb_ref[...]
        dc_ref[...] = dc

    tok = lambda w: pl.BlockSpec((TOK_T, w), lambda i: (i, 0))
    head = pl.BlockSpec((N_HEADS, TOK_T, HEAD_PAD), lambda i: (0, i, 0))
    return pl.pallas_call(
        body, name="prep_bwd", grid=(S // TOK_T,),
        in_specs=[head] * 6 + [tok(N_MAIN), pl.BlockSpec((4, 128), lambda i: (0, 0)), tok(128), tok(128), tok(128),
                               tok(D_BRANCH), tok(D_BRANCH)],
        out_specs=[tok(N_MAIN), tok(128), pl.BlockSpec((4, 128), lambda i: (0, 0))],
        out_shape=[jax.ShapeDtypeStruct((S, N_MAIN), BF16), jax.ShapeDtypeStruct((S, 128), F32),
                   jax.ShapeDtypeStruct((4, 128), F32)],
        compiler_params=_cp("arbitrary"),
    )(dqa, dka, dva, dqb, dkb, dvb, zm, qkg, rope_c, rope_a, rope_b, dga, dgb)


def _inproj_bwd(dzm, dzf, wm, wf, h0, dh1, g, rider=None):
    S = h0.shape[0]

    def body(dzm_ref, dzf_ref, wm_ref, wf_ref, h_ref, dh1_ref, g_ref, dh0_ref, dg_ref):
        @pl.when(pl.program_id(0) == 0)
        def _():
            dg_ref[...] = jnp.zeros_like(dg_ref)

        du = _dot_nt(dzm_ref[...], wm_ref[...]) + _dot_nt(dzf_ref[...], wf_ref[...])
        x = h_ref[...]
        r = lax.rsqrt(jnp.mean(x * x, axis=-1, keepdims=True) + EPS)
        xh = x * r
        a = du * g_ref[...]
        dh0_ref[...] = dh1_ref[...] + r * (a - xh * jnp.mean(a * xh, axis=-1, keepdims=True))
        dg_ref[...] += jnp.sum(du * xh, axis=0, keepdims=True)

    tok = lambda w: pl.BlockSpec((TOK_T, w), lambda i: (i, 0))
    full = lambda a, b: pl.BlockSpec((a, b), lambda i: (0, 0))
    return _call_with_rider(
        body, "inproj_bwd", (S // TOK_T,), rider,
        in_specs=[tok(N_MAIN), tok(128), full(D_MODEL, N_MAIN), full(D_MODEL, 128), tok(D_MODEL), tok(D_MODEL),
                  full(1, D_MODEL)],
        out_specs=[tok(D_MODEL), full(1, D_MODEL)],
        out_shape=[jax.ShapeDtypeStruct((S, D_MODEL), F32), jax.ShapeDtypeStruct((1, D_MODEL), F32)],
        scratch_shapes=[], operands=(dzm, dzf, wm, wf, h0, dh1, g), semantics=("arbitrary",))


def _wgrad(a, b, name, a_lead=()):
    S, M = a.shape[len(a_lead):]
    N = b.shape[1]
    tn = min(N, 2048)
    ts = 512
    last = S // ts - 1

    def body(a_ref, b_ref, o_ref, acc_ref):
        @pl.when(pl.program_id(1) == 0)
        def _():
            acc_ref[...] = jnp.zeros_like(acc_ref)

        acc_ref[...] += _dot_tn(a_ref[...].astype(BF16), b_ref[...])

        @pl.when(pl.program_id(1) == last)
        def _():
            o_ref[...] = acc_ref[...].astype(BF16)

    return pl.pallas_call(
        body, name=name, grid=(N // tn, S // ts),
        in_specs=[_slab_spec(a_lead, (ts, M), lambda n, s: (s, 0)), pl.BlockSpec((ts, tn), lambda n, s: (s, n))],
        out_specs=pl.BlockSpec((M, tn), lambda n, s: (0, n)),
        out_shape=jax.ShapeDtypeStruct((M, N), BF16),
        scratch_shapes=[pltpu.VMEM((M, tn), F32)],
        compiler_params=_cp("parallel", "arbitrary"),
    )(a, b)


def _rope_tables(positions):
    inv_freq = ROPE_THETA ** (-jnp.arange(ROPE_HALF, dtype=F32) / ROPE_HALF)
    ang = positions.astype(F32)[:, None] * inv_freq
    cos, sin = jnp.cos(ang), jnp.sin(ang)
    S = positions.shape[0]
    one, zero = jnp.ones((S, HEAD_DIM - 2 * ROPE_HALF), F32), jnp.zeros((S, HEAD_DIM - 2 * ROPE_HALF), F32)
    z8 = jnp.zeros((S, ROPE_HALF), F32)
    rc = jnp.concatenate([cos, cos, one], axis=1)
    ra = jnp.concatenate([-sin, z8, zero], axis=1)
    rb = jnp.concatenate([z8, sin, zero], axis=1)
    return tuple(jnp.tile(t, (1, 2)) for t in (rc, ra, rb))


def _in_operands(w_in):
    w_in = w_in.astype(BF16)
    wm = jnp.concatenate([w_in[:, :4 * D_BRANCH], w_in[:, 4 * D_BRANCH + N_HEADS:]], axis=1)
    wf = jnp.pad(w_in[:, 4 * D_BRANCH:4 * D_BRANCH + N_HEADS], ((0, 0), (0, 128 - N_HEADS)))
    return dict(wm=wm, wf=wf)


def _layer_weights(w_in, w_out, w_ple, w_pg):
    return dict(_in_operands(w_in), w_out=w_out.astype(BF16), w_ple=w_ple.astype(BF16), w_pg=w_pg.astype(BF16))


def _row(v, width=128):
    v = v.reshape(1, -1).astype(F32)
    return jnp.pad(v, ((0, 0), (0, width - v.shape[1])))


def _layer_fwd(h0, p, rope, tabs, w, norm_g, b_f, qk_g, ple_g, rider=None, late=None, target=None):
    g1 = norm_g.reshape(1, D_MODEL)
    g2 = ple_g.reshape(1, D_MODEL)
    qkg = jnp.tile(qk_g, (1, 2))
    bf = _row(b_f)
    zm, zf, u = _inproj_fwd(h0, g1, w["wm"], w["wf"])
    c = _forget_cumsum(zf, bf, tabs["tril"])
    qa, ka, va, qb, kb, vb = _prep_fwd(zm, c, qkg, *rope)
    oa, lse_a, *arrivals = _attn_fwd(qa, ka, va, tabs["fox"], True, "fox_fwd", rider)
    if late is not None:
        w = {**w, **late(arrivals)}
    ob, lse_b = _attn_fwd(qb, kb, vb, tabs["dil"], False, "dil_fwd")
    y, h1, h2, u2, e, gate, *loss = _mid_fwd(oa, ob, zm, h0, p, w["w_out"], w["w_pg"], w["w_ple"], g2, target)
    saved = dict(h0=h0, p=p, zm=zm, zf=zf, u=u, qa=qa, ka=ka, va=va, qb=qb, kb=kb, vb=vb, oa=oa, ob=ob,
                 lse_a=lse_a, lse_b=lse_b, y=y, h1=h1, u2=u2, e=e, gate=gate, g1=g1, g2=g2, qkg=qkg, bf=bf, w=w)
    return (h2, saved, arrivals, *loss)


def _layer_bwd(dh2, sv, rope, tabs, make_rider=None, make_last_rider=None):
    S = dh2.shape[0]
    nq = S // ATT_T
    w = sv["w"]
    rows = lambda a: a.reshape(N_HEADS, nq, 1, ATT_T)
    (dh1, dh1b, de, dpre, doa, dob, dga, dgb, dd, dg2) = _mid_bwd(
        dh2, sv["h1"], sv["e"], sv["gate"], sv["g2"], w["w_pg"], w["w_out"], sv["oa"], sv["ob"], sv["zm"])
    dda, ddb = dd[:N_HEADS], dd[N_HEADS:]
    early = dict(w_out=_wgrad(sv["y"], dh1b, "wgrad_out"), w_ple=_wgrad(sv["p"][0], de, "wgrad_ple", sv["p"][1]),
                 w_ple_gate=_wgrad(sv["u2"], dpre, "wgrad_gate"))
    rider = None if make_rider is None else make_rider(early)
    dqa, dka, dva, *arrivals = _attn_bwd(sv["qa"], sv["ka"], sv["va"], doa, sv["lse_a"], rows(dda), tabs["fox"],
                                         True, "fox_bwd", rider)
    dqb, dkb, dvb = _attn_bwd(sv["qb"], sv["kb"], sv["vb"], dob, sv["lse_b"], rows(ddb), tabs["dil"], False,
                              "dil_bwd")
    dzm, dc, dqkg = _prep_bwd(dqa, dka, dva, dqb, dkb, dvb, sv["zm"], sv["qkg"], *rope, dga, dgb)
    dzf, dbf = _forget_bwd(dc, sv["zf"], sv["bf"], tabs["triu"])
    dwm = _wgrad(sv["u"], dzm, "wgrad_in")
    dwf = _wgrad(sv["u"], dzf, "wgrad_f")
    dw_in = jnp.concatenate([dwm[:, :4 * D_BRANCH], dwf[:, :N_HEADS], dwm[:, 4 * D_BRANCH:]], axis=1)
    last_rider = None if make_last_rider is None else make_last_rider(dw_in)
    dh0, dg1, *last_arrivals = _inproj_bwd(dzm, dzf, w["wm"], w["wf"], sv["h0"], dh1, sv["g1"], last_rider)
    grads = dict(norm_g=dg1.reshape(D_MODEL), w_in=dw_in, b_f=dbf[0, :N_HEADS],
                 qk_norm_g=dqkg[:, :HEAD_DIM] + dqkg[:, HEAD_DIM:], ple_norm_g=dg2.reshape(D_MODEL), **early)
    return dh0, grads, arrivals + last_arrivals


def _tables():
    T = CUM_T
    r = lax.broadcasted_iota(jnp.int32, (T, T), 0)
    c = lax.broadcasted_iota(jnp.int32, (T, T), 1)
    return dict(fox=_bias_tables(True), dil=_bias_tables(False),
                tril=(c <= r).astype(BF16), triu=(c >= r).astype(BF16))


def _local_step(x, p, positions, target, layers, small):
    rope = _rope_tables(positions)
    tabs = _tables()
    ws = [_layer_weights(*lw) for lw in layers]
    h = x
    saved = []
    for li, (w, lp, sm) in enumerate(zip(ws, p, small)):
        h, sv, _, *loss = _layer_fwd(h, (lp, ()), rope, tabs, w, *sm, target=target if li == len(ws) - 1 else None)
        saved.append(sv)
    dh, (loss,) = h, loss
    grads = [None] * len(ws)
    for li in reversed(range(len(ws))):
        dh, grads[li], _ = _layer_bwd(dh, saved[li], rope, tabs)
    return loss[0, 0], dh, grads


def _peers():
    x, y, c = lax.axis_index("x"), lax.axis_index("y"), lax.axis_index("c")
    me = 4 * x + 2 * y + c
    flip = lambda v, bit: 1 - v if bit else v
    return me, [(flip(x, k & 4), flip(y, k & 2), flip(c, k & 1)) for k in range(1, N_DEV)]


def _sel(ref, kind, d):
    if kind == "whole":
        return ref
    if kind == "slot":
        return ref.at[d]
    block = pl.ds(pl.multiple_of(d * 128, 128), 128)
    return ref.at[block, :] if kind == "rows" else ref.at[:, block]


class _Pushes:
    def __init__(self, arrays, src_kinds, dst_kinds, out_shapes):
        self.arrays, self.n = list(arrays), len(arrays)
        self.src_kinds, self.dst_kinds = src_kinds, dst_kinds
        self.out_shapes = [jax.ShapeDtypeStruct(s, a.dtype) for s, a in zip(out_shapes, arrays)]
        hbm = pl.BlockSpec(memory_space=pltpu.HBM)
        self.in_specs, self.out_specs = [hbm] * self.n, [hbm] * self.n
        self.scratch_shapes = [pltpu.SemaphoreType.DMA((N_DEV - 1, self.n)),
                               pltpu.SemaphoreType.DMA((N_DEV - 1, self.n)), pltpu.SemaphoreType.DMA((self.n,))]

    def _copies(self, ins, outs, sems):
        send_sems, recv_sems, local_sems = sems
        me, peers = _peers()
        src = lambda a, d: _sel(ins[a], self.src_kinds[a], d)
        dst = lambda a: _sel(outs[a], self.dst_kinds[a], me)
        local = [pltpu.make_async_copy(src(a, me), dst(a), local_sems.at[a]) for a in range(self.n)]
        remote = [pltpu.make_async_remote_copy(
            src_ref=src(a, 4 * px + 2 * py + pc), dst_ref=dst(a), send_sem=send_sems.at[k, a],
            recv_sem=recv_sems.at[k, a], device_id=(px, py, pc), device_id_type=pl.DeviceIdType.MESH)
            for k, (px, py, pc) in enumerate(peers) for a in range(self.n)]
        return local + remote

    def start(self, ins, outs, sems):
        for cp in self._copies(ins, outs, sems):
            cp.start()

    def wait(self, ins, outs, sems):
        for cp in self._copies(ins, outs, sems):
            cp.wait()


def _exchange(name, pushes):
    n = pushes.n

    def body(*refs):
        pushes.start(refs[:n], refs[n:2 * n], refs[2 * n:])
        pushes.wait(refs[:n], refs[n:2 * n], refs[2 * n:])

    return pl.pallas_call(body, name=name, in_specs=pushes.in_specs, out_specs=pushes.out_specs,
                          out_shape=pushes.out_shapes, scratch_shapes=pushes.scratch_shapes)(*pushes.arrays)


def _gather_two_level(shard, name):
    def body(x_ref, out_ref, send_sems, recv_sems, local_sem):
        x, y, c = lax.axis_index("x"), lax.axis_index("y"), lax.axis_index("c")
        me, sibling = (x, y, c), (x, y, 1 - c)
        chips = [(1 - x, y), (x, 1 - y), (1 - x, 1 - y)]
        slot = lambda px, py, pc: out_ref.at[4 * px + 2 * py + pc]

        def copy(k, block, to, src=None):
            return pltpu.make_async_remote_copy(
                src_ref=slot(*block) if src is None else src, dst_ref=slot(*block), send_sem=send_sems.at[k],
                recv_sem=recv_sems.at[k], device_id=to, device_id_type=pl.DeviceIdType.MESH)

        mine = pltpu.make_async_copy(x_ref, slot(*me), local_sem)
        mine.start()
        first = [copy(0, me, sibling, src=x_ref)] + [copy(1 + j, me, (*chip, c), src=x_ref)
                                                     for j, chip in enumerate(chips)]
        for cp in first:
            cp.start()
        passed = [copy(4 + j, (*chip, c), sibling) for j, chip in enumerate(chips)]
        for j, chip in enumerate(chips):
            copy(1 + j, (*chip, c), me).wait_recv()
            passed[j].start()
        copy(0, sibling, me).wait_recv()
        for j, chip in enumerate(chips):
            copy(4 + j, (*chip, 1 - c), me).wait_recv()
        for cp in first + passed:
            cp.wait_send()
        mine.wait()

    hbm = pl.BlockSpec(memory_space=pltpu.HBM)
    return pl.pallas_call(
        body, name=name, in_specs=[hbm], out_specs=hbm,
        out_shape=jax.ShapeDtypeStruct((N_DEV,) + shard.shape, shard.dtype),
        scratch_shapes=[pltpu.SemaphoreType.DMA((N_DEV - 1,)), pltpu.SemaphoreType.DMA((N_DEV - 1,)),
                        pltpu.SemaphoreType.DMA],
    )(shard)


def _gather_pushes(shards, kinds):
    full = {"slot": lambda s: (N_DEV,) + s, "rows": lambda s: (N_DEV * s[0], s[1]),
            "cols": lambda s: (s[0], N_DEV * s[1])}
    return _Pushes(shards, ["whole"] * len(shards), kinds, [full[k](a.shape) for a, k in zip(shards, kinds)])


def _scatter_pushes(partials, kinds):
    part = {"slot": lambda s: s[1:], "rows": lambda s: (128, s[1]), "cols": lambda s: (s[0], 128),
            "whole": lambda s: s}
    return _Pushes(partials, kinds, ["slot"] * len(partials),
                   [(N_DEV,) + part[k](a.shape) for a, k in zip(partials, kinds)])


def _adamw(name, parts, w, m, v, rows):
    L, R, C = w.shape

    def body(p_ref, w_ref, m_ref, v_ref, g_ref, d_ref, nm_ref, nv_ref):
        g = p_ref[0, 0].astype(F32)
        for s in range(1, N_DEV):
            g = g + p_ref[s, 0].astype(F32)
        g_ref[0] = g
        nm = ADAM_B1 * m_ref[0] + (1.0 - ADAM_B1) * g
        nv = ADAM_B2 * v_ref[0] + (1.0 - ADAM_B2) * (g * g)
        nm_ref[0] = nm
        nv_ref[0] = nv
        m_hat = nm / (1.0 - ADAM_B1 ** ADAM_STEP)
        v_hat = nv / (1.0 - ADAM_B2 ** ADAM_STEP)
        d_ref[0] = -ADAM_LR * (m_hat / (jnp.sqrt(v_hat) + ADAM_EPS) + ADAM_WD * w_ref[0])

    blk = pl.BlockSpec((1, rows, C), lambda l, i: (l, i, 0))
    shp = jax.ShapeDtypeStruct((L, R, C), F32)
    return pl.pallas_call(
        body, name=name, grid=(L, R // rows),
        in_specs=[pl.BlockSpec((N_DEV, 1, rows, C), lambda l, i: (0, l, i, 0)), blk, blk, blk],
        out_specs=[blk] * 4, out_shape=[shp] * 4,
        compiler_params=_cp("parallel", "parallel"),
    )(parts, w, m, v)


SMALL_ROWS = 40
LOSS_ROW = 37


def _pack_small(norm_g, ple_g, qk_g, b_f, last_row):
    rows = lambda a: a.astype(F32).reshape(-1, 128)
    flat = jnp.concatenate([rows(norm_g), rows(ple_g), rows(qk_g), _row(b_f.reshape(-1)), last_row], axis=0)
    return jnp.pad(flat, ((0, SMALL_ROWS - flat.shape[0]), (0, 0)))


def _unpack_small(flat):
    return (flat[0:16].reshape(2, D_MODEL), flat[16:32].reshape(2, D_MODEL), flat[32:36].reshape(2, 4, HEAD_DIM),
            flat[36, :2 * N_HEADS].reshape(2, N_HEADS))


def kernel(x, p, positions, norm_g, w_in, b_f, qk_norm_g, w_out, w_ple, ple_norm_g, w_ple_gate, loss_target, m_norm_g, m_w_in, m_b_f, m_qk_norm_g, m_w_out, m_w_ple, m_ple_norm_g, m_w_ple_gate, v_norm_g, v_w_in, v_b_f, v_qk_norm_g, v_w_out, v_w_ple, v_ple_norm_g, v_w_ple_gate):
    bf16 = lambda a: a.astype(BF16)
    rows_in = W_IN_ROWS // 2
    flat_in = lambda a: bf16(a).reshape(rows_in, 128)
    full_in = lambda g: g.reshape(N_DEV, D_MODEL, W_IN_SHARD).transpose(1, 0, 2).reshape(D_MODEL, N_IN)
    small = [(norm_g[l], b_f[l], qk_norm_g[l], ple_norm_g[l]) for l in range(2)]
    rope = _rope_tables(positions[0])
    tabs = _tables()

    g_in0 = _gather_two_level(flat_in(w_in[0]), "gather_first")
    rest = _gather_pushes([flat_in(w_in[1])] + [bf16(a[l]) for l in range(2) for a in (w_out, w_ple, w_ple_gate)],
                          ["slot"] + ["rows", "cols", "rows"] * 2)
    late = lambda got: dict(w_out=got[1], w_ple=got[2], w_pg=got[3])
    h, sv0, got = _layer_fwd(x[0], (p, (0, 0)), rope, tabs, _in_operands(full_in(g_in0)), *small[0], rest, late)
    w1 = dict(_in_operands(full_in(got[0])), w_out=got[4], w_ple=got[5], w_pg=got[6])
    dh, sv1, _, loss = _layer_fwd(h, (p, (1, 0)), rope, tabs, w1, *small[1], target=loss_target[0])
    dh, gr1, _ = _layer_bwd(dh, sv1, rope, tabs)

    by_dest = lambda d: d.reshape(D_MODEL, N_DEV, W_IN_SHARD).transpose(1, 0, 2).reshape(N_DEV, rows_in, 128)
    big = ("w_out", "w_ple", "w_ple_gate")
    riding = lambda early: _scatter_pushes([by_dest(gr1["w_in"])] + [gr1[n] for n in big] + [early[n] for n in big],
                                           ["slot"] + ["rows", "cols", "rows"] * 2)
    riding_last = lambda dw_in: _scatter_pushes([by_dest(dw_in)], ["slot"])
    dx, gr0, (r_in1, *r_big, r_in0) = _layer_bwd(dh, sv0, rope, tabs, riding, riding_last)
    grads = (gr0, gr1)
    stack = lambda name: jnp.stack([gl[name] for gl in grads], axis=0)
    small_part = _pack_small(stack("norm_g"), stack("ple_norm_g"), stack("qk_norm_g"), stack("b_f"),
                             _row(loss[0, 0].reshape(1)))
    (r_small,) = _exchange("exchange_small", _scatter_pushes([small_part], ["whole"]))
    r_in = jnp.concatenate([r_in0, r_in1], axis=1)
    r_out, r_ple, r_pg = (jnp.stack([r_big[3 + k], r_big[k]], axis=1) for k in range(3))

    zero_row = jnp.zeros((1, 128), F32)
    small_of = lambda ng, pg, qk, bf: _pack_small(ng, pg, qk, bf, zero_row)[None]
    flat = lambda a: a.reshape(1, W_IN_ROWS, 128)
    outs = dict(
        w_in=[o.reshape(w_in.shape) for o in
              _adamw("adamw_in", r_in[:, None], flat(w_in), flat(m_w_in), flat(v_w_in), W_IN_TILE)],
        w_out=_adamw("adamw_out", r_out, w_out, m_w_out, v_w_out, 128),
        w_ple=_adamw("adamw_ple", r_ple, w_ple, m_w_ple, v_w_ple, 256),
        w_pg=_adamw("adamw_gate", r_pg, w_ple_gate, m_w_ple_gate, v_w_ple_gate, 128),
        small=_adamw("adamw_small", r_small[:, None], small_of(norm_g, ple_norm_g, qk_norm_g, b_f),
                     small_of(m_norm_g, m_ple_norm_g, m_qk_norm_g, m_b_f),
                     small_of(v_norm_g, v_ple_norm_g, v_qk_norm_g, v_b_f), SMALL_ROWS))
    leaves = []
    for kind in range(4):
        ng, pg, qk, bf = _unpack_small(outs["small"][kind][0])
        leaves += [ng, outs["w_in"][kind], bf, qk, outs["w_out"][kind], outs["w_ple"][kind], pg, outs["w_pg"][kind]]
    return (outs["small"][0][0, LOSS_ROW, 0], dx[None], *leaves)
```

```python
import functools

import jax
import jax.numpy as jnp
from jax import lax
from jax.experimental import pallas as pl
from jax.experimental.pallas import tpu as pltpu

F32 = jnp.float32
BF16 = jnp.bfloat16

D_MODEL = 1024
HEAD_DIM = 64
N_HEADS = 8
HEAD_PAD = 128
D_BRANCH = N_HEADS * HEAD_DIM
N_MAIN = 8 * D_BRANCH
N_IN = N_MAIN + N_HEADS
PLE_DIM = 256
ROPE_THETA = 500000.0
ROPE_HALF = 8
EPS = 1e-6
NEG = -1e30
SCALE = HEAD_DIM ** -0.5
LOG2E = 1.4426950408889634
LN2 = 0.6931471805599453
DILATED_PATTERNS = ((128, 1), (512, 4), (2048, 16))
N_DEV = 8
W_IN_SHARD = N_IN // N_DEV
W_IN_ROWS = 2 * D_MODEL * W_IN_SHARD // 128
W_IN_TILE = W_IN_ROWS // 19

ADAM_LR = 0.001
ADAM_B1 = 0.9
ADAM_B2 = 0.999
ADAM_EPS = 1e-08
ADAM_WD = 0.01
ADAM_STEP = 10

ATT_T = 512
ATT_FWD_HEADS = 4
ATT_BWD_HEADS = 2
ATT_CHUNK = 32
TOK_T = 256
VMEM_LIMIT = 56 * 1024 * 1024


def _slab_spec(lead, block, index):
    return pl.BlockSpec((None,) * len(lead) + block, lambda *g: (*lead, *index(*g)))


def _cp(*sem):
    return pltpu.CompilerParams(dimension_semantics=sem, vmem_limit_bytes=VMEM_LIMIT)


def _sigmoid(x):
    return 1.0 / (1.0 + jnp.exp(-x))


def _split3(x):
    hi = x.astype(BF16)
    r1 = x - hi.astype(F32)
    mid = r1.astype(BF16)
    lo = (r1 - mid.astype(F32)).astype(BF16)
    return hi, mid, lo


def _dot(a, b):
    return jnp.dot(a, b, preferred_element_type=F32)


def _dot_nt(a, b):
    return lax.dot_general(a, b, (((1,), (1,)), ((), ())), preferred_element_type=F32)


def _dot_tn(a, b):
    return lax.dot_general(a, b, (((0,), (0,)), ((), ())), preferred_element_type=F32)


def _inproj_fwd(h, g, wm, wf):
    S = h.shape[0]

    def body(h_ref, g_ref, wm_ref, wf_ref, zm_ref, zf_ref, u_ref):
        x = h_ref[...]
        r = lax.rsqrt(jnp.mean(x * x, axis=-1, keepdims=True) + EPS)
        u = (x * r * g_ref[...]).astype(BF16)
        u_ref[...] = u
        zm_ref[...] = _dot(u, wm_ref[...])
        zf_ref[...] = _dot(u, wf_ref[...])

    return pl.pallas_call(
        body, name="inproj_fwd", grid=(S // TOK_T,),
        in_specs=[pl.BlockSpec((TOK_T, D_MODEL), lambda i: (i, 0)),
                  pl.BlockSpec((1, D_MODEL), lambda i: (0, 0)),
                  pl.BlockSpec((D_MODEL, N_MAIN), lambda i: (0, 0)),
                  pl.BlockSpec((D_MODEL, 128), lambda i: (0, 0))],
        out_specs=[pl.BlockSpec((TOK_T, N_MAIN), lambda i: (i, 0)),
                   pl.BlockSpec((TOK_T, 128), lambda i: (i, 0)),
                   pl.BlockSpec((TOK_T, D_MODEL), lambda i: (i, 0))],
        out_shape=[jax.ShapeDtypeStruct((S, N_MAIN), F32), jax.ShapeDtypeStruct((S, 128), F32),
                   jax.ShapeDtypeStruct((S, D_MODEL), BF16)],
        compiler_params=_cp("parallel"),
    )(h, g, wm, wf)


def _log_sigmoid(x):
    return jnp.minimum(x, 0.0) - jnp.log(1.0 + jnp.exp(-jnp.abs(x)))


def _same_head():
    r = lax.broadcasted_iota(jnp.int32, (HEAD_PAD, HEAD_PAD), 0) // HEAD_DIM
    c = lax.broadcasted_iota(jnp.int32, (HEAD_PAD, HEAD_PAD), 1) // HEAD_DIM
    return (r == c).astype(BF16)


def _pair_mean(x, same_head):
    hi = x.astype(BF16)
    lo = (x - hi.astype(F32)).astype(BF16)
    return (_dot(hi, same_head) + _dot(lo, same_head)) * (1.0 / HEAD_DIM)


def _pair_rsqrt(x, same_head):
    return lax.rsqrt(_pair_mean(x * x, same_head) + EPS)


def _prep_fwd(zm, zf, bf, tril, qkg, rope_c, rope_a, rope_b):
    S = zm.shape[0]
    shp = jax.ShapeDtypeStruct((N_HEADS, S, HEAD_PAD), BF16)

    def body(z_ref, zf_ref, b_ref, tri_ref, g_ref, rc_ref, ra_ref, rb_ref,
             qa_ref, ka_ref, va_ref, qb_ref, kb_ref, vb_ref, carry):
        @pl.when(pl.program_id(0) == 0)
        def _():
            carry[...] = jnp.zeros_like(carry)

        tri = tri_ref[...]
        cs = sum(_dot(tri, part) for part in _split3(_log_sigmoid(zf_ref[...] + b_ref[...]))) + carry[...]
        carry[...] = cs[TOK_T - 1:TOK_T, :]
        lane = lax.broadcasted_iota(jnp.int32, (TOK_T, HEAD_PAD), 1)
        lo_half = lane < HEAD_DIM
        aug = (lane >= HEAD_DIM) & (lane < HEAD_DIM + 3)
        q_pad = jnp.where(aug, -1.0, 0.0)
        rc, ra, rb = rc_ref[...], ra_ref[...], rb_ref[...]
        same_head = _same_head()

        def norm(col, gi):
            x = z_ref[:, col:col + HEAD_PAD]
            return x * _pair_rsqrt(x, same_head) * g_ref[gi:gi + 1, :]

        def rope(y):
            return y * rc + pltpu.roll(y, HEAD_PAD - ROPE_HALF, 1) * ra + pltpu.roll(y, ROPE_HALF, 1) * rb

        def put(ref, pi, y, pad_even, pad_odd):
            ref[2 * pi] = jnp.where(lo_half, y, pad_even).astype(BF16)
            ref[2 * pi + 1] = jnp.where(lo_half, pltpu.roll(y, HEAD_DIM, 1), pad_odd).astype(BF16)

        def k_pad(h):
            ch = cs[:, h:h + 1] * LOG2E
            hi = ch.astype(BF16).astype(F32)
            mid = (ch - hi).astype(BF16).astype(F32)
            lo = ch - hi - mid
            ones = jnp.where(lane == HEAD_DIM + 3, 1.0, 0.0)
            return jnp.where(lane == HEAD_DIM, hi, jnp.where(lane == HEAD_DIM + 1, mid,
                                                              jnp.where(lane == HEAD_DIM + 2, lo, ones)))

        for pi in range(N_HEADS // 2):
            col = HEAD_PAD * pi
            put(qa_ref, pi, norm(col, 0) * (SCALE * LOG2E), q_pad, q_pad)
            put(ka_ref, pi, norm(D_BRANCH + col, 1), k_pad(2 * pi), k_pad(2 * pi + 1))
            put(va_ref, pi, z_ref[:, 2 * D_BRANCH + col:2 * D_BRANCH + col + HEAD_PAD], 0.0, 0.0)
            put(qb_ref, pi, rope(norm(4 * D_BRANCH + col, 2)) * (SCALE * LOG2E), 0.0, 0.0)
            put(kb_ref, pi, rope(norm(5 * D_BRANCH + col, 3)), 0.0, 0.0)
            put(vb_ref, pi, z_ref[:, 6 * D_BRANCH + col:6 * D_BRANCH + col + HEAD_PAD], 0.0, 0.0)

    tok = lambda w: pl.BlockSpec((TOK_T, w), lambda i: (i, 0))
    head = pl.BlockSpec((N_HEADS, TOK_T, HEAD_PAD), lambda i: (0, i, 0))
    return pl.pallas_call(
        body, name="prep_fwd", grid=(S // TOK_T,),
        in_specs=[tok(N_MAIN), tok(128), pl.BlockSpec((1, 128), lambda i: (0, 0)),
                  pl.BlockSpec((TOK_T, TOK_T), lambda i: (0, 0)), pl.BlockSpec((4, 128), lambda i: (0, 0)),
                  tok(128), tok(128), tok(128)],
        out_specs=[head] * 6, out_shape=[shp] * 6,
        scratch_shapes=[pltpu.VMEM((1, 128), F32)],
        compiler_params=_cp("arbitrary"),
    )(zm, zf, bf, tril, qkg, rope_c, rope_a, rope_b)


def _pair(ref, pi, lo_half):
    return jnp.where(lo_half, ref[2 * pi], pltpu.roll(ref[2 * pi + 1], HEAD_DIM, 1))


def _mid_fwd(oa, ob, zm, h0, p, w_out, w_pg, w_ple, g2, target=None):
    S = h0.shape[0]
    p, p_lead = p

    def body(oa_ref, ob_ref, ga_ref, gb_ref, h0_ref, p_ref, wo_ref, wg_ref, wp_ref, g2_ref, *rest):
        t_ref, rest = (rest[0], rest[1:]) if target is not None else (None, rest)
        y_ref, h1_ref, h2_ref, u2_ref, e_ref, gate_ref, *loss_ref = rest
        parts = []
        for o_ref, g_ref in ((oa_ref, ga_ref), (ob_ref, gb_ref)):
            for pi in range(N_HEADS // 2):
                g = g_ref[:, HEAD_PAD * pi:HEAD_PAD * (pi + 1)]
                parts.append((o_ref[pi] * (g * _sigmoid(g))).astype(BF16))
        y = jnp.concatenate(parts, axis=1)
        y_ref[...] = y
        h1 = h0_ref[...] + _dot(y, wo_ref[...])
        h1_ref[...] = h1
        r = lax.rsqrt(jnp.mean(h1 * h1, axis=-1, keepdims=True) + EPS)
        u2 = (h1 * r * g2_ref[...]).astype(BF16)
        u2_ref[...] = u2
        gate = _sigmoid(_dot(u2, wg_ref[...]))
        e = _dot(p_ref[...].astype(BF16), wp_ref[...])
        e_ref[...] = e.astype(BF16)
        gate_ref[...] = gate.astype(BF16)
        h2 = h1 + e * gate
        if target is None:
            h2_ref[...] = h2
        else:
            @pl.when(pl.program_id(0) == 0)
            def _():
                loss_ref[0][...] = jnp.zeros_like(loss_ref[0])

            err = h2 - t_ref[...]
            h2_ref[...] = err * (1.0 / D_MODEL)
            part = jnp.sum(jnp.sum(err * err, axis=1, keepdims=True), axis=0, keepdims=True)
            loss_ref[0][...] += part * (0.5 / D_MODEL)

    tok = lambda w: pl.BlockSpec((TOK_T, w), lambda i: (i, 0))
    head = pl.BlockSpec((N_HEADS // 2, TOK_T, HEAD_PAD), lambda i: (0, i, 0))
    full = lambda a, b: pl.BlockSpec((a, b), lambda i: (0, 0))
    act = lambda dt: jax.ShapeDtypeStruct((S, D_MODEL), dt)
    fused = target is not None
    return pl.pallas_call(
        body, name="mid_fwd_loss" if fused else "mid_fwd", grid=(S // TOK_T,),
        in_specs=[head, head,
                  pl.BlockSpec((TOK_T, D_BRANCH), lambda i: (i, 3)), pl.BlockSpec((TOK_T, D_BRANCH), lambda i: (i, 7)),
                  tok(D_MODEL), _slab_spec(p_lead, (TOK_T, PLE_DIM), lambda i: (i, 0)), full(D_MODEL, D_MODEL),
                  full(D_MODEL, D_MODEL), full(PLE_DIM, D_MODEL), full(1, D_MODEL)] + [tok(D_MODEL)] * fused,
        out_specs=[tok(D_MODEL)] * 6 + [full(8, 128)] * fused,
        out_shape=[act(BF16), act(F32), act(F32), act(BF16), act(BF16), act(BF16)]
        + [jax.ShapeDtypeStruct((8, 128), F32)] * fused,
        compiler_params=_cp("arbitrary" if fused else "parallel"),
    )(oa, ob, zm, zm, h0, p, w_out, w_pg, w_ple, g2, *([target] * fused))


def _bias_tables(full_range):
    T = ATT_T
    nb = 1 if full_range else DILATED_PATTERNS[-1][0] // T + 1
    r = lax.broadcasted_iota(jnp.int32, (nb, T, T), 2)
    c = lax.broadcasted_iota(jnp.int32, (nb, T, T), 1)
    b = lax.broadcasted_iota(jnp.int32, (nb, T, T), 0)
    delta = T * b + r - c
    if full_range:
        bias = jnp.where(delta >= 0, 0.0, NEG).astype(F32)
    else:
        mult = jnp.zeros((nb, T, T), F32)
        for window, dil in DILATED_PATTERNS:
            ok = (delta >= 0) & (delta <= window) & (delta % dil == 0)
            mult = mult + ok.astype(F32)
        bias = jnp.where(mult > 0, jnp.log2(jnp.maximum(mult, 1.0)), NEG).astype(F32)
    return bias


def _call_with_rider(body, name, grid, rider, in_specs, out_specs, out_shape, scratch_shapes, operands,
                     semantics=("parallel", "arbitrary")):
    if rider is None:
        return pl.pallas_call(body, name=name, grid=grid, in_specs=in_specs, out_specs=out_specs,
                              out_shape=out_shape, scratch_shapes=scratch_shapes,
                              compiler_params=_cp(*semantics))(*operands)
    n, n_in, n_out = rider.n, len(in_specs), len(out_specs)

    def wrapped(*refs):
        ins, r_ins = refs[:n_in], refs[n_in:n_in + n]
        outs, r_outs = refs[n_in + n:n_in + n + n_out], refs[n_in + n + n_out:n_in + 2 * n + n_out]
        scratch, sems = refs[n_in + 2 * n + n_out:-3], refs[-3:]
        step = [pl.program_id(a) for a in range(len(grid))]

        @pl.when(functools.reduce(jnp.logical_and, [s == 0 for s in step]))
        def _():
            rider.start(r_ins, r_outs, sems)

        body(*ins, *outs, *scratch)

        @pl.when(functools.reduce(jnp.logical_and, [s == g - 1 for s, g in zip(step, grid)]))
        def _():
            rider.wait(r_ins, r_outs, sems)

    return pl.pallas_call(
        wrapped, name=name, grid=grid, in_specs=list(in_specs) + rider.in_specs,
        out_specs=list(out_specs) + rider.out_specs, out_shape=list(out_shape) + rider.out_shapes,
        scratch_shapes=list(scratch_shapes) + rider.scratch_shapes,
        compiler_params=_cp(*["arbitrary"] * len(grid)))(*operands, *rider.arrays)


def _attn_fwd(q, k, v, table_t, full_range, name, rider=None):
    H, S, _ = q.shape
    T = ATT_T
    nb = table_t.shape[0]
    HB = ATT_FWD_HEADS
    KC = ATT_CHUNK
    chunks = [slice(c, c + KC) for c in range(0, T, KC)]
    fold = lambda x, op: functools.reduce(op, [x[r:r + 8] for r in range(0, KC, 8)])

    def body(q_ref, k_ref, v_ref, tab_ref, o_ref, lse_ref, *scratch):
        st_refs, pt_refs, acc_refs = scratch[:HB], scratch[HB:2 * HB], scratch[2 * HB:]
        i = pl.program_id(1)
        rows = lambda j: pl.ds(pl.multiple_of(j * T, T), T)

        def scores(hh, j):
            st_refs[hh][...] = _dot_nt(k_ref[hh, rows(j), :], q_ref[hh])

        def block(j, b, nxt, stats):
            out = []
            for hh, (m, l) in enumerate(stats):
                st_ref, pt_ref, acc_ref = st_refs[hh], pt_refs[hh], acc_refs[hh]
                mx = None
                for ch in chunks:
                    x = st_ref[ch, :]
                    if b is not None:
                        x = x + tab_ref[b, ch, :]
                        st_ref[ch, :] = x
                    x = fold(x, jnp.maximum)
                    mx = x if mx is None else jnp.maximum(mx, x)
                m_new = jnp.maximum(m, jnp.max(mx, axis=0, keepdims=True))
                alpha = jnp.exp2(m - m_new)
                ls = None
                for ch in chunks:
                    pc = jnp.exp2(st_ref[ch, :] - m_new)
                    pt_ref[ch, :] = pc.astype(BF16)
                    pc = fold(pc, jnp.add)
                    ls = pc if ls is None else ls + pc
                if nxt is not None:
                    scores(hh, nxt)
                acc_ref[...] = alpha * acc_ref[...] + _dot_tn(v_ref[hh, rows(j), :], pt_ref[...])
                out.append((m_new, alpha * l + jnp.sum(ls, axis=0, keepdims=True)))
            return tuple(out)

        lo = 0 if full_range else jnp.maximum(i - (nb - 1), 0)
        for hh in range(HB):
            acc_refs[hh][...] = jnp.zeros_like(acc_refs[hh])
            scores(hh, lo)
        stats = lax.fori_loop(lo, i, lambda j, st: block(j, None if full_range else i - j, j + 1, st),
                              ((jnp.full((1, T), NEG, F32), jnp.zeros((1, T), F32)),) * HB)
        stats = block(i, 0, None, stats)
        o_t = [acc_refs[hh][...] * (1.0 / l) for hh, (m, l) in enumerate(stats)]
        for hh, (m, l) in enumerate(stats):
            lse_ref[hh, 0] = m + jnp.log2(l)
        for hp in range(HB // 2):
            o_ref[hp] = jnp.concatenate([o_t[2 * hp][:HEAD_DIM], o_t[2 * hp + 1][:HEAD_DIM]], axis=0).T

    return _call_with_rider(
        body, name, (H // HB, S // T), rider,
        in_specs=[pl.BlockSpec((HB, T, HEAD_PAD), lambda h, i: (h, i, 0)),
                  pl.BlockSpec((HB, S, HEAD_PAD), lambda h, i: (h, 0, 0), pipeline_mode=pl.Buffered(1)),
                  pl.BlockSpec((HB, S, HEAD_PAD), lambda h, i: (h, 0, 0), pipeline_mode=pl.Buffered(1)),
                  pl.BlockSpec((nb, T, T), lambda h, i: (0, 0, 0), pipeline_mode=pl.Buffered(1))],
        out_specs=[pl.BlockSpec((HB // 2, T, HEAD_PAD), lambda h, i: (h, i, 0)),
                   pl.BlockSpec((HB, 1, 1, T), lambda h, i: (h, i, 0, 0))],
        out_shape=[jax.ShapeDtypeStruct((H // 2, S, HEAD_PAD), F32), jax.ShapeDtypeStruct((H, S // T, 1, T), F32)],
        scratch_shapes=([pltpu.VMEM((T, T), F32)] * HB + [pltpu.VMEM((T, T), BF16)] * HB
                        + [pltpu.VMEM((HEAD_PAD, T), F32)] * HB),
        operands=(q, k, v, table_t))


def _attn_bwd(q, k, v, do, lse, dd, table_t, full_range, name, rider=None):
    H, S, _ = q.shape
    T = ATT_T
    nq = S // T
    nb = table_t.shape[0]
    HB = ATT_BWD_HEADS
    KC = ATT_CHUNK
    chunks = [slice(c, c + KC) for c in range(0, T, KC)]

    def body(q_ref, do_ref, lse_ref, dd_ref, k_ref, v_ref, tab_ref, dq_hbm, dk_ref, dv_ref, *scratch):
        st_refs, dpt_refs, pt_refs, dst_refs = (scratch[n * HB:(n + 1) * HB] for n in range(4))
        dq_ref, dq_sem = scratch[4 * HB:]
        h = pl.program_id(0)
        j = pl.program_id(1)

        @pl.when(j == 0)
        def _():
            dq_ref[...] = jnp.zeros_like(dq_ref)

        dk_ref[...] = jnp.zeros_like(dk_ref)
        dv_ref[...] = jnp.zeros_like(dv_ref)

        def step(i, b):
            rows = pl.ds(pl.multiple_of(i * T, T), T)
            for hh in range(HB):
                st_refs[hh][...] = _dot_nt(k_ref[hh], q_ref[hh, rows, :])
                dpt_refs[hh][...] = _dot_nt(v_ref[hh], do_ref[hh, rows, :])
            for hh in range(HB):
                lse_i = lse_ref[hh, i]
                dd_i = dd_ref[hh, i]
                for ch in chunks:
                    x = st_refs[hh][ch, :]
                    if b is not None:
                        x = x + tab_ref[b, ch, :]
                    pc = jnp.exp2(x - lse_i)
                    pt_refs[hh][ch, :] = pc.astype(BF16)
                    dst_refs[hh][ch, :] = (pc * (dpt_refs[hh][ch, :] - dd_i)).astype(BF16)
                dv_ref[hh] += _dot(pt_refs[hh][...], do_ref[hh, rows, :])
                dk_ref[hh] += _dot(dst_refs[hh][...], q_ref[hh, rows, :])
                dq_ref[hh, rows, :] += _dot_tn(dst_refs[hh][...], k_ref[hh])

        step(j, 0)
        if full_range:
            pl.loop(j + 1, nq)(lambda i: step(i, None))
        else:
            pl.loop(j + 1, jnp.minimum(j + nb, nq))(lambda i: step(i, i - j))

        @pl.when(j == nq - 1)
        def _():
            out = pltpu.make_async_copy(dq_ref, dq_hbm.at[pl.ds(h * HB, HB)], dq_sem)
            out.start()
            out.wait()

    once = dict(pipeline_mode=pl.Buffered(1))
    per_head = pl.BlockSpec((HB, S, HEAD_PAD), lambda h, j: (h, 0, 0), **once)
    rows = pl.BlockSpec((HB, nq, 1, T), lambda h, j: (h, 0, 0, 0))
    blk = pl.BlockSpec((HB, T, HEAD_PAD), lambda h, j: (h, j, 0))
    shp = jax.ShapeDtypeStruct((H, S, HEAD_PAD), F32)
    return _call_with_rider(
        body, name, (H // HB, nq), rider,
        in_specs=[per_head, per_head, rows, rows, blk, blk,
                  pl.BlockSpec((nb, T, T), lambda h, j: (0, 0, 0), **once)],
        out_specs=[pl.BlockSpec(memory_space=pltpu.HBM), blk, blk], out_shape=[shp, shp, shp],
        scratch_shapes=([pltpu.VMEM((T, T), F32)] * (2 * HB) + [pltpu.VMEM((T, T), BF16)] * (2 * HB)
                        + [pltpu.VMEM((HB, S, HEAD_PAD), F32), pltpu.SemaphoreType.DMA]),
        operands=(q, do, lse, dd, k, v, table_t))


def _mid_bwd(dh2, h1, e, gate, g2, w_pg, w_out, oa, ob, zm):
    S = dh2.shape[0]

    def body(dh2_ref, h1_ref, e_ref, gate_ref, g2_ref, wg_ref, wo_ref, oa_ref, ob_ref, ga_ref, gb_ref,
             dh1_ref, dh1b_ref, de_ref, dpre_ref, doa_ref, dob_ref, dga_ref, dgb_ref, dd_ref, dg2_ref):
        @pl.when(pl.program_id(0) == 0)
        def _():
            dg2_ref[...] = jnp.zeros_like(dg2_ref)

        lane = lax.broadcasted_iota(jnp.int32, (TOK_T, HEAD_PAD), 1)
        lo_half = lane < HEAD_DIM
        dh2 = dh2_ref[...]
        gate = gate_ref[...]
        de_ref[...] = (dh2 * gate).astype(BF16)
        dpre = (dh2 * e_ref[...] * gate * (1.0 - gate)).astype(BF16)
        dpre_ref[...] = dpre
        du2 = _dot_nt(dpre, wg_ref[...])
        h1 = h1_ref[...]
        r = lax.rsqrt(jnp.mean(h1 * h1, axis=-1, keepdims=True) + EPS)
        xh = h1 * r
        a = du2 * g2_ref[...]
        dh1 = dh2 + r * (a - xh * jnp.mean(a * xh, axis=-1, keepdims=True))
        dg2_ref[...] += jnp.sum(du2 * xh, axis=0, keepdims=True)
        dh1_ref[...] = dh1
        dh1b = dh1.astype(BF16)
        dh1b_ref[...] = dh1b
        dy = _dot_nt(dh1b, wo_ref[...])
        dd = jnp.zeros((TOK_T, HEAD_PAD), F32)
        for bi, (o_ref, g_ref, do_ref, dg_ref) in enumerate(
                ((oa_ref, ga_ref, doa_ref, dga_ref), (ob_ref, gb_ref, dob_ref, dgb_ref))):
            for pi in range(N_HEADS // 2):
                col = bi * D_BRANCH + HEAD_PAD * pi
                dyp = dy[:, col:col + HEAD_PAD]
                g = g_ref[:, HEAD_PAD * pi:HEAD_PAD * (pi + 1)]
                sg = _sigmoid(g)
                o_pair = o_ref[pi]
                dg_ref[:, HEAD_PAD * pi:HEAD_PAD * (pi + 1)] = (
                    dyp * o_pair * (sg * (1.0 + g * (1.0 - sg)))).astype(BF16)
                dop = dyp * (g * sg)
                prod = dop * o_pair
                for hh, d_head, mine in ((2 * pi, dop, lo_half),
                                         (2 * pi + 1, pltpu.roll(dop, HEAD_DIM, 1), ~lo_half)):
                    do_ref[hh] = jnp.where(lo_half, d_head, 0.0).astype(BF16)
                    dsum = jnp.sum(jnp.where(mine, prod, 0.0), axis=1, keepdims=True)
                    dd = dd + jnp.where(lane == bi * N_HEADS + hh, dsum, 0.0)
        dd_ref[...] = dd.T[:2 * N_HEADS, :]

    tok = lambda w: pl.BlockSpec((TOK_T, w), lambda i: (i, 0))
    head = pl.BlockSpec((N_HEADS, TOK_T, HEAD_PAD), lambda i: (0, i, 0))
    pairs = pl.BlockSpec((N_HEADS // 2, TOK_T, HEAD_PAD), lambda i: (0, i, 0))
    full = lambda a, b: pl.BlockSpec((a, b), lambda i: (0, 0))
    act = lambda w, dt: jax.ShapeDtypeStruct((S, w), dt)
    hshape = lambda w, dt: jax.ShapeDtypeStruct((N_HEADS, S, w), dt)
    return pl.pallas_call(
        body, name="mid_bwd", grid=(S // TOK_T,),
        in_specs=[tok(D_MODEL)] * 4 + [full(1, D_MODEL), full(D_MODEL, D_MODEL), full(D_MODEL, D_MODEL), pairs, pairs,
                                      pl.BlockSpec((TOK_T, D_BRANCH), lambda i: (i, 3)),
                                      pl.BlockSpec((TOK_T, D_BRANCH), lambda i: (i, 7))],
        out_specs=[tok(D_MODEL)] * 4 + [head, head, tok(D_BRANCH), tok(D_BRANCH),
                                       pl.BlockSpec((2 * N_HEADS, TOK_T), lambda i: (0, i)), full(1, D_MODEL)],
        out_shape=[act(D_MODEL, F32), act(D_MODEL, BF16), act(D_MODEL, BF16), act(D_MODEL, BF16),
                   hshape(HEAD_PAD, BF16), hshape(HEAD_PAD, BF16), act(D_BRANCH, BF16), act(D_BRANCH, BF16),
                   jax.ShapeDtypeStruct((2 * N_HEADS, S), F32), jax.ShapeDtypeStruct((1, D_MODEL), F32)],
        compiler_params=_cp("arbitrary"),
    )(dh2, h1, e, gate, g2, w_pg, w_out, oa, ob, zm, zm)


def _prep_bwd(dqa, dka, dva, dqb, dkb, dvb, zm, qkg, rope_c, rope_a, rope_b, dga, dgb, zf, bf, triu):
    S = zm.shape[0]
    n = S // TOK_T

    def body(dqa_ref, dka_ref, dva_ref, dqb_ref, dkb_ref, dvb_ref, z_ref, g_ref, rc_ref, ra_ref, rb_ref,
             dga_ref, dgb_ref, zf_ref, b_ref, tri_ref, dz_ref, dzf_ref, dqkg_ref, db_ref, carry):
        @pl.when(pl.program_id(0) == 0)
        def _():
            dqkg_ref[...] = jnp.zeros_like(dqkg_ref)
            db_ref[...] = jnp.zeros_like(db_ref)
            carry[...] = jnp.zeros_like(carry)

        lane = lax.broadcasted_iota(jnp.int32, (TOK_T, HEAD_PAD), 1)
        lo_half = lane < HEAD_DIM
        rc, ra, rb = rc_ref[...], ra_ref[...], rb_ref[...]
        same_head = _same_head()

        def unrope(dy):
            return dy * rc + pltpu.roll(dy * ra, ROPE_HALF, 1) + pltpu.roll(dy * rb, HEAD_PAD - ROPE_HALF, 1)

        def norm_bwd(col, gi, dy):
            x = z_ref[:, col:col + HEAD_PAD]
            r = _pair_rsqrt(x, same_head)
            xh = x * r
            dqkg_ref[gi:gi + 1, :] += jnp.sum(dy * xh, axis=0, keepdims=True)
            a = dy * g_ref[gi:gi + 1, :]
            dz_ref[:, col:col + HEAD_PAD] = (r * (a - xh * _pair_mean(a * xh, same_head))).astype(BF16)

        dc = jnp.zeros((TOK_T, HEAD_PAD), F32)
        for pi in range(N_HEADS // 2):
            col = HEAD_PAD * pi
            norm_bwd(col, 0, _pair(dqa_ref, pi, lo_half) * SCALE)
            norm_bwd(D_BRANCH + col, 1, _pair(dka_ref, pi, lo_half) * LN2)
            dz_ref[:, 2 * D_BRANCH + col:2 * D_BRANCH + col + HEAD_PAD] = _pair(dva_ref, pi, lo_half).astype(BF16)
            norm_bwd(4 * D_BRANCH + col, 2, unrope(_pair(dqb_ref, pi, lo_half) * SCALE))
            norm_bwd(5 * D_BRANCH + col, 3, unrope(_pair(dkb_ref, pi, lo_half) * LN2))
            dz_ref[:, 6 * D_BRANCH + col:6 * D_BRANCH + col + HEAD_PAD] = _pair(dvb_ref, pi, lo_half).astype(BF16)
            for hh in (2 * pi, 2 * pi + 1):
                dch = dka_ref[hh][:, HEAD_DIM:HEAD_DIM + 1] + dqa_ref[hh][:, HEAD_DIM + 3:HEAD_DIM + 4]
                dc = dc + jnp.where(lane == hh, dch, 0.0)
        dz_ref[:, 3 * D_BRANCH:4 * D_BRANCH] = dga_ref[...]
        dz_ref[:, 7 * D_BRANCH:8 * D_BRANCH] = dgb_ref[...]
        tri = tri_ref[...]
        dlf = sum(_dot(tri, part) for part in _split3(dc)) + carry[...]
        carry[...] = dlf[0:1, :]
        dfa = dlf * (1.0 - _sigmoid(zf_ref[...] + b_ref[...]))
        dzf_ref[...] = dfa.astype(BF16)
        db_ref[...] += jnp.sum(dfa, axis=0, keepdims=True)

    tok = lambda w: pl.BlockSpec((TOK_T, w), lambda i: (n - 1 - i, 0))
    head = pl.BlockSpec((N_HEADS, TOK_T, HEAD_PAD), lambda i: (0, n - 1 - i, 0))
    fixed = lambda a, b: pl.BlockSpec((a, b), lambda i: (0, 0))
    return pl.pallas_call(
        body, name="prep_bwd", grid=(n,),
        in_specs=[head] * 6 + [tok(N_MAIN), fixed(4, 128), tok(128), tok(128), tok(128), tok(D_BRANCH), tok(D_BRANCH),
                               tok(128), fixed(1, 128), fixed(TOK_T, TOK_T)],
        out_specs=[tok(N_MAIN), tok(128), fixed(4, 128), fixed(1, 128)],
        out_shape=[jax.ShapeDtypeStruct((S, N_MAIN), BF16), jax.ShapeDtypeStruct((S, 128), BF16),
                   jax.ShapeDtypeStruct((4, 128), F32), jax.ShapeDtypeStruct((1, 128), F32)],
        scratch_shapes=[pltpu.VMEM((1, 128), F32)],
        compiler_params=_cp("arbitrary"),
    )(dqa, dka, dva, dqb, dkb, dvb, zm, qkg, rope_c, rope_a, rope_b, dga, dgb, zf, bf, triu)


def _inproj_bwd(dzm, dzf, wm, wf, h0, dh1, g, rider=None):
    S = h0.shape[0]

    def body(dzm_ref, dzf_ref, wm_ref, wf_ref, h_ref, dh1_ref, g_ref, dh0_ref, dg_ref):
        @pl.when(pl.program_id(0) == 0)
        def _():
            dg_ref[...] = jnp.zeros_like(dg_ref)

        du = _dot_nt(dzm_ref[...], wm_ref[...]) + _dot_nt(dzf_ref[...], wf_ref[...])
        x = h_ref[...]
        r = lax.rsqrt(jnp.mean(x * x, axis=-1, keepdims=True) + EPS)
        xh = x * r
        a = du * g_ref[...]
        dh0_ref[...] = dh1_ref[...] + r * (a - xh * jnp.mean(a * xh, axis=-1, keepdims=True))
        dg_ref[...] += jnp.sum(du * xh, axis=0, keepdims=True)

    tok = lambda w: pl.BlockSpec((TOK_T, w), lambda i: (i, 0))
    full = lambda a, b: pl.BlockSpec((a, b), lambda i: (0, 0))
    return _call_with_rider(
        body, "inproj_bwd", (S // TOK_T,), rider,
        in_specs=[tok(N_MAIN), tok(128), full(D_MODEL, N_MAIN), full(D_MODEL, 128), tok(D_MODEL), tok(D_MODEL),
                  full(1, D_MODEL)],
        out_specs=[tok(D_MODEL), full(1, D_MODEL)],
        out_shape=[jax.ShapeDtypeStruct((S, D_MODEL), F32), jax.ShapeDtypeStruct((1, D_MODEL), F32)],
        scratch_shapes=[], operands=(dzm, dzf, wm, wf, h0, dh1, g), semantics=("arbitrary",))


def _wgrad(a, b, name, a_lead=()):
    S, M = a.shape[len(a_lead):]
    N = b.shape[1]
    tn = min(N, 2048)
    ts = 512
    last = S // ts - 1

    def body(a_ref, b_ref, o_ref, acc_ref):
        @pl.when(pl.program_id(1) == 0)
        def _():
            acc_ref[...] = jnp.zeros_like(acc_ref)

        acc_ref[...] += _dot_tn(a_ref[...].astype(BF16), b_ref[...])

        @pl.when(pl.program_id(1) == last)
        def _():
            o_ref[...] = acc_ref[...].astype(BF16)

    return pl.pallas_call(
        body, name=name, grid=(N // tn, S // ts),
        in_specs=[_slab_spec(a_lead, (ts, M), lambda n, s: (s, 0)), pl.BlockSpec((ts, tn), lambda n, s: (s, n))],
        out_specs=pl.BlockSpec((M, tn), lambda n, s: (0, n)),
        out_shape=jax.ShapeDtypeStruct((M, N), BF16),
        scratch_shapes=[pltpu.VMEM((M, tn), F32)],
        compiler_params=_cp("parallel", "arbitrary"),
    )(a, b)


def _rope_tables(positions):
    inv_freq = ROPE_THETA ** (-jnp.arange(ROPE_HALF, dtype=F32) / ROPE_HALF)
    ang = positions.astype(F32)[:, None] * inv_freq
    cos, sin = jnp.cos(ang), jnp.sin(ang)
    S = positions.shape[0]
    one, zero = jnp.ones((S, HEAD_DIM - 2 * ROPE_HALF), F32), jnp.zeros((S, HEAD_DIM - 2 * ROPE_HALF), F32)
    z8 = jnp.zeros((S, ROPE_HALF), F32)
    rc = jnp.concatenate([cos, cos, one], axis=1)
    ra = jnp.concatenate([-sin, z8, zero], axis=1)
    rb = jnp.concatenate([z8, sin, zero], axis=1)
    return tuple(jnp.tile(t, (1, 2)) for t in (rc, ra, rb))


def _in_operands(w_in):
    w_in = w_in.astype(BF16)
    wm = jnp.concatenate([w_in[:, :4 * D_BRANCH], w_in[:, 4 * D_BRANCH + N_HEADS:]], axis=1)
    wf = jnp.pad(w_in[:, 4 * D_BRANCH:4 * D_BRANCH + N_HEADS], ((0, 0), (0, 128 - N_HEADS)))
    return dict(wm=wm, wf=wf)


def _layer_weights(w_in, w_out, w_ple, w_pg):
    return dict(_in_operands(w_in), w_out=w_out.astype(BF16), w_ple=w_ple.astype(BF16), w_pg=w_pg.astype(BF16))


def _row(v, width=128):
    v = v.reshape(1, -1).astype(F32)
    return jnp.pad(v, ((0, 0), (0, width - v.shape[1])))


def _layer_fwd(h0, p, rope, tabs, w, norm_g, b_f, qk_g, ple_g, rider=None, late=None, target=None):
    g1 = norm_g.reshape(1, D_MODEL)
    g2 = ple_g.reshape(1, D_MODEL)
    qkg = jnp.tile(qk_g, (1, 2))
    bf = _row(b_f)
    zm, zf, u = _inproj_fwd(h0, g1, w["wm"], w["wf"])
    qa, ka, va, qb, kb, vb = _prep_fwd(zm, zf, bf, tabs["tril"], qkg, *rope)
    oa, lse_a, *arrivals = _attn_fwd(qa, ka, va, tabs["fox"], True, "fox_fwd", rider)
    if late is not None:
        w = {**w, **late(arrivals)}
    ob, lse_b = _attn_fwd(qb, kb, vb, tabs["dil"], False, "dil_fwd")
    y, h1, h2, u2, e, gate, *loss = _mid_fwd(oa, ob, zm, h0, p, w["w_out"], w["w_pg"], w["w_ple"], g2, target)
    saved = dict(h0=h0, p=p, zm=zm, zf=zf, u=u, qa=qa, ka=ka, va=va, qb=qb, kb=kb, vb=vb, oa=oa, ob=ob,
                 lse_a=lse_a, lse_b=lse_b, y=y, h1=h1, u2=u2, e=e, gate=gate, g1=g1, g2=g2, qkg=qkg, bf=bf, w=w)
    return (h2, saved, arrivals, *loss)


def _layer_bwd(dh2, sv, rope, tabs, make_rider=None, make_last_rider=None):
    S = dh2.shape[0]
    nq = S // ATT_T
    w = sv["w"]
    rows = lambda a: a.reshape(N_HEADS, nq, 1, ATT_T)
    (dh1, dh1b, de, dpre, doa, dob, dga, dgb, dd, dg2) = _mid_bwd(
        dh2, sv["h1"], sv["e"], sv["gate"], sv["g2"], w["w_pg"], w["w_out"], sv["oa"], sv["ob"], sv["zm"])
    dda, ddb = dd[:N_HEADS], dd[N_HEADS:]
    early = dict(w_out=_wgrad(sv["y"], dh1b, "wgrad_out"), w_ple=_wgrad(sv["p"][0], de, "wgrad_ple", sv["p"][1]),
                 w_ple_gate=_wgrad(sv["u2"], dpre, "wgrad_gate"))
    rider = None if make_rider is None else make_rider(early)
    dqa, dka, dva, *arrivals = _attn_bwd(sv["qa"], sv["ka"], sv["va"], doa, sv["lse_a"], rows(dda), tabs["fox"],
                                         True, "fox_bwd", rider)
    dqb, dkb, dvb = _attn_bwd(sv["qb"], sv["kb"], sv["vb"], dob, sv["lse_b"], rows(ddb), tabs["dil"], False,
                              "dil_bwd")
    dzm, dzf, dqkg, dbf = _prep_bwd(dqa, dka, dva, dqb, dkb, dvb, sv["zm"], sv["qkg"], *rope, dga, dgb,
                                    sv["zf"], sv["bf"], tabs["triu"])
    dwm = _wgrad(sv["u"], dzm, "wgrad_in")
    dwf = _wgrad(sv["u"], dzf, "wgrad_f")
    dw_in = jnp.concatenate([dwm[:, :4 * D_BRANCH], dwf[:, :N_HEADS], dwm[:, 4 * D_BRANCH:]], axis=1)
    last_rider = None if make_last_rider is None else make_last_rider(dw_in)
    dh0, dg1, *last_arrivals = _inproj_bwd(dzm, dzf, w["wm"], w["wf"], sv["h0"], dh1, sv["g1"], last_rider)
    grads = dict(norm_g=dg1.reshape(D_MODEL), w_in=dw_in, b_f=dbf[0, :N_HEADS],
                 qk_norm_g=dqkg[:, :HEAD_DIM] + dqkg[:, HEAD_DIM:], ple_norm_g=dg2.reshape(D_MODEL), **early)
    return dh0, grads, arrivals + last_arrivals


def _tables():
    T = TOK_T
    r = lax.broadcasted_iota(jnp.int32, (T, T), 0)
    c = lax.broadcasted_iota(jnp.int32, (T, T), 1)
    return dict(fox=_bias_tables(True), dil=_bias_tables(False),
                tril=(c <= r).astype(BF16), triu=(c >= r).astype(BF16))


def _local_step(x, p, positions, target, layers, small):
    rope = _rope_tables(positions)
    tabs = _tables()
    ws = [_layer_weights(*lw) for lw in layers]
    h = x
    saved = []
    for li, (w, lp, sm) in enumerate(zip(ws, p, small)):
        h, sv, _, *loss = _layer_fwd(h, (lp, ()), rope, tabs, w, *sm, target=target if li == len(ws) - 1 else None)
        saved.append(sv)
    dh, (loss,) = h, loss
    grads = [None] * len(ws)
    for li in reversed(range(len(ws))):
        dh, grads[li], _ = _layer_bwd(dh, saved[li], rope, tabs)
    return loss[0, 0], dh, grads


def _peers():
    x, y, c = lax.axis_index("x"), lax.axis_index("y"), lax.axis_index("c")
    me = 4 * x + 2 * y + c
    flip = lambda v, bit: 1 - v if bit else v
    return me, [(flip(x, k & 4), flip(y, k & 2), flip(c, k & 1)) for k in range(1, N_DEV)]


def _sel(ref, kind, d):
    if kind == "whole":
        return ref
    if kind == "slot":
        return ref.at[d]
    block = pl.ds(pl.multiple_of(d * 128, 128), 128)
    return ref.at[block, :] if kind == "rows" else ref.at[:, block]


class _Pushes:
    def __init__(self, arrays, src_kinds, dst_kinds, out_shapes):
        self.arrays, self.n = list(arrays), len(arrays)
        self.src_kinds, self.dst_kinds = src_kinds, dst_kinds
        self.out_shapes = [jax.ShapeDtypeStruct(s, a.dtype) for s, a in zip(out_shapes, arrays)]
        hbm = pl.BlockSpec(memory_space=pltpu.HBM)
        self.in_specs, self.out_specs = [hbm] * self.n, [hbm] * self.n
        self.scratch_shapes = [pltpu.SemaphoreType.DMA((N_DEV - 1, self.n)),
                               pltpu.SemaphoreType.DMA((N_DEV - 1, self.n)), pltpu.SemaphoreType.DMA((self.n,))]

    def _copies(self, ins, outs, sems):
        send_sems, recv_sems, local_sems = sems
        me, peers = _peers()
        src = lambda a, d: _sel(ins[a], self.src_kinds[a], d)
        dst = lambda a: _sel(outs[a], self.dst_kinds[a], me)
        local = [pltpu.make_async_copy(src(a, me), dst(a), local_sems.at[a]) for a in range(self.n)]
        remote = [pltpu.make_async_remote_copy(
            src_ref=src(a, 4 * px + 2 * py + pc), dst_ref=dst(a), send_sem=send_sems.at[k, a],
            recv_sem=recv_sems.at[k, a], device_id=(px, py, pc), device_id_type=pl.DeviceIdType.MESH)
            for k, (px, py, pc) in enumerate(peers) for a in range(self.n)]
        return local + remote

    def start(self, ins, outs, sems):
        for cp in self._copies(ins, outs, sems):
            cp.start()

    def wait(self, ins, outs, sems):
        for cp in self._copies(ins, outs, sems):
            cp.wait()


def _exchange(name, pushes):
    n = pushes.n

    def body(*refs):
        pushes.start(refs[:n], refs[n:2 * n], refs[2 * n:])
        pushes.wait(refs[:n], refs[n:2 * n], refs[2 * n:])

    return pl.pallas_call(body, name=name, in_specs=pushes.in_specs, out_specs=pushes.out_specs,
                          out_shape=pushes.out_shapes, scratch_shapes=pushes.scratch_shapes)(*pushes.arrays)


def _gather_two_level(shard, name):
    def body(x_ref, out_ref, send_sems, recv_sems, local_sem):
        x, y, c = lax.axis_index("x"), lax.axis_index("y"), lax.axis_index("c")
        me, sibling = (x, y, c), (x, y, 1 - c)
        chips = [(1 - x, y), (x, 1 - y), (1 - x, 1 - y)]
        slot = lambda px, py, pc: out_ref.at[4 * px + 2 * py + pc]

        def copy(k, block, to, src=None):
            return pltpu.make_async_remote_copy(
                src_ref=slot(*block) if src is None else src, dst_ref=slot(*block), send_sem=send_sems.at[k],
                recv_sem=recv_sems.at[k], device_id=to, device_id_type=pl.DeviceIdType.MESH)

        mine = pltpu.make_async_copy(x_ref, slot(*me), local_sem)
        mine.start()
        first = [copy(0, me, sibling, src=x_ref)] + [copy(1 + j, me, (*chip, c), src=x_ref)
                                                     for j, chip in enumerate(chips)]
        for cp in first:
            cp.start()
        passed = [copy(4 + j, (*chip, c), sibling) for j, chip in enumerate(chips)]
        for j, chip in enumerate(chips):
            copy(1 + j, (*chip, c), me).wait_recv()
            passed[j].start()
        copy(0, sibling, me).wait_recv()
        for j, chip in enumerate(chips):
            copy(4 + j, (*chip, 1 - c), me).wait_recv()
        for cp in first + passed:
            cp.wait_send()
        mine.wait()

    hbm = pl.BlockSpec(memory_space=pltpu.HBM)
    return pl.pallas_call(
        body, name=name, in_specs=[hbm], out_specs=hbm,
        out_shape=jax.ShapeDtypeStruct((N_DEV,) + shard.shape, shard.dtype),
        scratch_shapes=[pltpu.SemaphoreType.DMA((N_DEV - 1,)), pltpu.SemaphoreType.DMA((N_DEV - 1,)),
                        pltpu.SemaphoreType.DMA],
    )(shard)


def _gather_pushes(shards, kinds):
    full = {"slot": lambda s: (N_DEV,) + s, "rows": lambda s: (N_DEV * s[0], s[1]),
            "cols": lambda s: (s[0], N_DEV * s[1])}
    return _Pushes(shards, ["whole"] * len(shards), kinds, [full[k](a.shape) for a, k in zip(shards, kinds)])


def _scatter_pushes(partials, kinds):
    part = {"slot": lambda s: s[1:], "rows": lambda s: (128, s[1]), "cols": lambda s: (s[0], 128),
            "whole": lambda s: s}
    return _Pushes(partials, kinds, ["slot"] * len(partials),
                   [(N_DEV,) + part[k](a.shape) for a, k in zip(partials, kinds)])


def _adamw(name, parts, w, m, v, rows):
    L, R, C = w.shape

    def body(p_ref, w_ref, m_ref, v_ref, g_ref, d_ref, nm_ref, nv_ref):
        g = p_ref[0, 0].astype(F32)
        for s in range(1, N_DEV):
            g = g + p_ref[s, 0].astype(F32)
        g_ref[0] = g
        nm = ADAM_B1 * m_ref[0] + (1.0 - ADAM_B1) * g
        nv = ADAM_B2 * v_ref[0] + (1.0 - ADAM_B2) * (g * g)
        nm_ref[0] = nm
        nv_ref[0] = nv
        m_hat = nm / (1.0 - ADAM_B1 ** ADAM_STEP)
        v_hat = nv / (1.0 - ADAM_B2 ** ADAM_STEP)
        d_ref[0] = -ADAM_LR * (m_hat / (jnp.sqrt(v_hat) + ADAM_EPS) + ADAM_WD * w_ref[0])

    blk = pl.BlockSpec((1, rows, C), lambda l, i: (l, i, 0))
    shp = jax.ShapeDtypeStruct((L, R, C), F32)
    return pl.pallas_call(
        body, name=name, grid=(L, R // rows),
        in_specs=[pl.BlockSpec((N_DEV, 1, rows, C), lambda l, i: (0, l, i, 0)), blk, blk, blk],
        out_specs=[blk] * 4, out_shape=[shp] * 4,
        compiler_params=_cp("parallel", "parallel"),
    )(parts, w, m, v)


SMALL_ROWS = 40
LOSS_ROW = 37


def _pack_small(norm_g, ple_g, qk_g, b_f, last_row):
    rows = lambda a: a.astype(F32).reshape(-1, 128)
    flat = jnp.concatenate([rows(norm_g), rows(ple_g), rows(qk_g), _row(b_f.reshape(-1)), last_row], axis=0)
    return jnp.pad(flat, ((0, SMALL_ROWS - flat.shape[0]), (0, 0)))


def _unpack_small(flat):
    return (flat[0:16].reshape(2, D_MODEL), flat[16:32].reshape(2, D_MODEL), flat[32:36].reshape(2, 4, HEAD_DIM),
            flat[36, :2 * N_HEADS].reshape(2, N_HEADS))


def kernel(x, p, positions, norm_g, w_in, b_f, qk_norm_g, w_out, w_ple, ple_norm_g, w_ple_gate, loss_target, m_norm_g, m_w_in, m_b_f, m_qk_norm_g, m_w_out, m_w_ple, m_ple_norm_g, m_w_ple_gate, v_norm_g, v_w_in, v_b_f, v_qk_norm_g, v_w_out, v_w_ple, v_ple_norm_g, v_w_ple_gate):
    bf16 = lambda a: a.astype(BF16)
    rows_in = W_IN_ROWS // 2
    flat_in = lambda a: bf16(a).reshape(rows_in, 128)
    full_in = lambda g: g.reshape(N_DEV, D_MODEL, W_IN_SHARD).transpose(1, 0, 2).reshape(D_MODEL, N_IN)
    small = [(norm_g[l], b_f[l], qk_norm_g[l], ple_norm_g[l]) for l in range(2)]
    rope = _rope_tables(positions[0])
    tabs = _tables()

    g_in0 = _gather_two_level(flat_in(w_in[0]), "gather_first")
    rest = _gather_pushes([flat_in(w_in[1])] + [bf16(a[l]) for l in range(2) for a in (w_out, w_ple, w_ple_gate)],
                          ["slot"] + ["rows", "cols", "rows"] * 2)
    late = lambda got: dict(w_out=got[1], w_ple=got[2], w_pg=got[3])
    h, sv0, got = _layer_fwd(x[0], (p, (0, 0)), rope, tabs, _in_operands(full_in(g_in0)), *small[0], rest, late)
    w1 = dict(_in_operands(full_in(got[0])), w_out=got[4], w_ple=got[5], w_pg=got[6])
    dh, sv1, _, loss = _layer_fwd(h, (p, (1, 0)), rope, tabs, w1, *small[1], target=loss_target[0])
    dh, gr1, _ = _layer_bwd(dh, sv1, rope, tabs)

    by_dest = lambda d: d.reshape(D_MODEL, N_DEV, W_IN_SHARD).transpose(1, 0, 2).reshape(N_DEV, rows_in, 128)
    big = ("w_out", "w_ple", "w_ple_gate")
    riding = lambda early: _scatter_pushes([by_dest(gr1["w_in"])] + [gr1[n] for n in big] + [early[n] for n in big],
                                           ["slot"] + ["rows", "cols", "rows"] * 2)
    riding_last = lambda dw_in: _scatter_pushes([by_dest(dw_in)], ["slot"])
    dx, gr0, (r_in1, *r_big, r_in0) = _layer_bwd(dh, sv0, rope, tabs, riding, riding_last)
    grads = (gr0, gr1)
    stack = lambda name: jnp.stack([gl[name] for gl in grads], axis=0)
    small_part = _pack_small(stack("norm_g"), stack("ple_norm_g"), stack("qk_norm_g"), stack("b_f"),
                             _row(loss[0, 0].reshape(1)))
    (r_small,) = _exchange("exchange_small", _scatter_pushes([small_part], ["whole"]))
    r_in = jnp.concatenate([r_in0, r_in1], axis=1)
    r_out, r_ple, r_pg = (jnp.stack([r_big[3 + k], r_big[k]], axis=1) for k in range(3))

    zero_row = jnp.zeros((1, 128), F32)
    small_of = lambda ng, pg, qk, bf: _pack_small(ng, pg, qk, bf, zero_row)[None]
    flat = lambda a: a.reshape(1, W_IN_ROWS, 128)
    outs = dict(
        w_in=[o.reshape(w_in.shape) for o in
              _adamw("adamw_in", r_in[:, None], flat(w_in), flat(m_w_in), flat(v_w_in), W_IN_TILE)],
        w_out=_adamw("adamw_out", r_out, w_out, m_w_out, v_w_out, 128),
        w_ple=_adamw("adamw_ple", r_ple, w_ple, m_w_ple, v_w_ple, 256),
        w_pg=_adamw("adamw_gate", r_pg, w_ple_gate, m_w_ple_gate, v_w_ple_gate, 128),
        small=_adamw("adamw_small", r_small[:, None], small_of(norm_g, ple_norm_g, qk_norm_g, b_f),
                     small_of(m_norm_g, m_ple_norm_g, m_qk_norm_g, m_b_f),
                     small_of(v_norm_g, v_ple_norm_g, v_qk_norm_g, v_b_f), SMALL_ROWS))
    leaves = []
    for kind in range(4):
        ng, pg, qk, bf = _unpack_small(outs["small"][kind][0])
        leaves += [ng, outs["w_in"][kind], bf, qk, outs["w_out"][kind], outs["w_ple"][kind], pg, outs["w_pg"][kind]]
    return (outs["small"][0][0, LOSS_ROW, 0], dx[None], *leaves)
```

```python
import functools

import jax
import jax.numpy as jnp
from jax import lax
from jax.experimental import pallas as pl
from jax.experimental.pallas import tpu as pltpu

F32 = jnp.float32
BF16 = jnp.bfloat16

D_MODEL = 1024
HEAD_DIM = 64
N_HEADS = 8
HEAD_PAD = 128
D_BRANCH = N_HEADS * HEAD_DIM
N_MAIN = 8 * D_BRANCH
N_IN = N_MAIN + N_HEADS
PLE_DIM = 256
ROPE_THETA = 500000.0
ROPE_HALF = 8
EPS = 1e-6
NEG = -1e30
SCALE = HEAD_DIM ** -0.5
LOG2E = 1.4426950408889634
LN2 = 0.6931471805599453
DILATED_PATTERNS = ((128, 1), (512, 4), (2048, 16))
N_DEV = 8
W_IN_SHARD = N_IN // N_DEV
W_IN_ROWS = 2 * D_MODEL * W_IN_SHARD // 128
W_IN_TILE = W_IN_ROWS // 19

ADAM_LR = 0.001
ADAM_B1 = 0.9
ADAM_B2 = 0.999
ADAM_EPS = 1e-08
ADAM_WD = 0.01
ADAM_STEP = 10

ATT_T = 512
ATT_FWD_HEADS = 4
ATT_BWD_HEADS = 2
ATT_CHUNK = 32
TOK_T = 256
VMEM_LIMIT = 56 * 1024 * 1024


def _slab_spec(lead, block, index):
    return pl.BlockSpec((None,) * len(lead) + block, lambda *g: (*lead, *index(*g)))


def _cp(*sem):
    return pltpu.CompilerParams(dimension_semantics=sem, vmem_limit_bytes=VMEM_LIMIT)


def _sigmoid(x):
    return 1.0 / (1.0 + jnp.exp(-x))


def _split3(x):
    hi = x.astype(BF16)
    r1 = x - hi.astype(F32)
    mid = r1.astype(BF16)
    lo = (r1 - mid.astype(F32)).astype(BF16)
    return hi, mid, lo


def _dot(a, b):
    return jnp.dot(a, b, preferred_element_type=F32)


def _dot_nt(a, b):
    return lax.dot_general(a, b, (((1,), (1,)), ((), ())), preferred_element_type=F32)


def _dot_tn(a, b):
    return lax.dot_general(a, b, (((0,), (0,)), ((), ())), preferred_element_type=F32)


def _inproj_fwd(h, g, wm, wf):
    S = h.shape[0]

    def body(h_ref, g_ref, wm_ref, wf_ref, zm_ref, zf_ref, u_ref):
        x = h_ref[...]
        r = lax.rsqrt(jnp.mean(x * x, axis=-1, keepdims=True) + EPS)
        u = (x * r * g_ref[...]).astype(BF16)
        u_ref[...] = u
        zm_ref[...] = _dot(u, wm_ref[...]).astype(BF16)
        zf_ref[...] = _dot(u, wf_ref[...])

    return pl.pallas_call(
        body, name="inproj_fwd", grid=(S // TOK_T,),
        in_specs=[pl.BlockSpec((TOK_T, D_MODEL), lambda i: (i, 0)),
                  pl.BlockSpec((1, D_MODEL), lambda i: (0, 0)),
                  pl.BlockSpec((D_MODEL, N_MAIN), lambda i: (0, 0)),
                  pl.BlockSpec((D_MODEL, 128), lambda i: (0, 0))],
        out_specs=[pl.BlockSpec((TOK_T, N_MAIN), lambda i: (i, 0)),
                   pl.BlockSpec((TOK_T, 128), lambda i: (i, 0)),
                   pl.BlockSpec((TOK_T, D_MODEL), lambda i: (i, 0))],
        out_shape=[jax.ShapeDtypeStruct((S, N_MAIN), BF16), jax.ShapeDtypeStruct((S, 128), F32),
                   jax.ShapeDtypeStruct((S, D_MODEL), BF16)],
        compiler_params=_cp("parallel"),
    )(h, g, wm, wf)


def _log_sigmoid(x):
    return jnp.minimum(x, 0.0) - jnp.log(1.0 + jnp.exp(-jnp.abs(x)))


def _same_head():
    r = lax.broadcasted_iota(jnp.int32, (HEAD_PAD, HEAD_PAD), 0) // HEAD_DIM
    c = lax.broadcasted_iota(jnp.int32, (HEAD_PAD, HEAD_PAD), 1) // HEAD_DIM
    return (r == c).astype(BF16)


def _pair_mean(x, same_head):
    hi = x.astype(BF16)
    lo = (x - hi.astype(F32)).astype(BF16)
    return (_dot(hi, same_head) + _dot(lo, same_head)) * (1.0 / HEAD_DIM)


def _pair_rsqrt(x, same_head):
    return lax.rsqrt(_pair_mean(x * x, same_head) + EPS)


def _prep_fwd(zm, zf, bf, tril, qkg, rope_c, rope_a, rope_b):
    S = zm.shape[0]
    shp = jax.ShapeDtypeStruct((N_HEADS, S, HEAD_PAD), BF16)

    def body(z_ref, zf_ref, b_ref, tri_ref, g_ref, rc_ref, ra_ref, rb_ref,
             qa_ref, ka_ref, va_ref, qb_ref, kb_ref, vb_ref, carry):
        @pl.when(pl.program_id(0) == 0)
        def _():
            carry[...] = jnp.zeros_like(carry)

        tri = tri_ref[...]
        cs = sum(_dot(tri, part) for part in _split3(_log_sigmoid(zf_ref[...] + b_ref[...]))) + carry[...]
        carry[...] = cs[TOK_T - 1:TOK_T, :]
        lane = lax.broadcasted_iota(jnp.int32, (TOK_T, HEAD_PAD), 1)
        lo_half = lane < HEAD_DIM
        aug = (lane >= HEAD_DIM) & (lane < HEAD_DIM + 3)
        q_pad = jnp.where(aug, -1.0, 0.0)
        rc, ra, rb = rc_ref[...], ra_ref[...], rb_ref[...]
        same_head = _same_head()

        def norm(col, gi):
            x = z_ref[:, col:col + HEAD_PAD].astype(F32)
            return x * _pair_rsqrt(x, same_head) * g_ref[gi:gi + 1, :]

        def rope(y):
            return y * rc + pltpu.roll(y, HEAD_PAD - ROPE_HALF, 1) * ra + pltpu.roll(y, ROPE_HALF, 1) * rb

        def put(ref, pi, y, pad_even, pad_odd):
            ref[2 * pi] = jnp.where(lo_half, y, pad_even).astype(BF16)
            ref[2 * pi + 1] = jnp.where(lo_half, pltpu.roll(y, HEAD_DIM, 1), pad_odd).astype(BF16)

        def k_pad(h):
            ch = cs[:, h:h + 1] * LOG2E
            hi = ch.astype(BF16).astype(F32)
            mid = (ch - hi).astype(BF16).astype(F32)
            lo = ch - hi - mid
            ones = jnp.where(lane == HEAD_DIM + 3, 1.0, 0.0)
            return jnp.where(lane == HEAD_DIM, hi, jnp.where(lane == HEAD_DIM + 1, mid,
                                                              jnp.where(lane == HEAD_DIM + 2, lo, ones)))

        for pi in range(N_HEADS // 2):
            col = HEAD_PAD * pi
            put(qa_ref, pi, norm(col, 0) * (SCALE * LOG2E), q_pad, q_pad)
            put(ka_ref, pi, norm(D_BRANCH + col, 1), k_pad(2 * pi), k_pad(2 * pi + 1))
            put(va_ref, pi, z_ref[:, 2 * D_BRANCH + col:2 * D_BRANCH + col + HEAD_PAD].astype(F32), 0.0, 0.0)
            put(qb_ref, pi, rope(norm(4 * D_BRANCH + col, 2)) * (SCALE * LOG2E), 0.0, 0.0)
            put(kb_ref, pi, rope(norm(5 * D_BRANCH + col, 3)), 0.0, 0.0)
            put(vb_ref, pi, z_ref[:, 6 * D_BRANCH + col:6 * D_BRANCH + col + HEAD_PAD].astype(F32), 0.0, 0.0)

    tok = lambda w: pl.BlockSpec((TOK_T, w), lambda i: (i, 0))
    head = pl.BlockSpec((N_HEADS, TOK_T, HEAD_PAD), lambda i: (0, i, 0))
    return pl.pallas_call(
        body, name="prep_fwd", grid=(S // TOK_T,),
        in_specs=[tok(N_MAIN), tok(128), pl.BlockSpec((1, 128), lambda i: (0, 0)),
                  pl.BlockSpec((TOK_T, TOK_T), lambda i: (0, 0)), pl.BlockSpec((4, 128), lambda i: (0, 0)),
                  tok(128), tok(128), tok(128)],
        out_specs=[head] * 6, out_shape=[shp] * 6,
        scratch_shapes=[pltpu.VMEM((1, 128), F32)],
        compiler_params=_cp("arbitrary"),
    )(zm, zf, bf, tril, qkg, rope_c, rope_a, rope_b)


def _pair(ref, pi, lo_half):
    return jnp.where(lo_half, ref[2 * pi], pltpu.roll(ref[2 * pi + 1], HEAD_DIM, 1))


def _mid_fwd(oa, ob, zm, h0, p, w_out, w_pg, w_ple, g2, target=None):
    S = h0.shape[0]
    p, p_lead = p

    def body(oa_ref, ob_ref, ga_ref, gb_ref, h0_ref, p_ref, wo_ref, wg_ref, wp_ref, g2_ref, *rest):
        t_ref, rest = (rest[0], rest[1:]) if target is not None else (None, rest)
        y_ref, h1_ref, h2_ref, u2_ref, e_ref, gate_ref, *loss_ref = rest
        parts = []
        for o_ref, g_ref in ((oa_ref, ga_ref), (ob_ref, gb_ref)):
            for pi in range(N_HEADS // 2):
                g = g_ref[:, HEAD_PAD * pi:HEAD_PAD * (pi + 1)].astype(F32)
                parts.append((o_ref[pi] * (g * _sigmoid(g))).astype(BF16))
        y = jnp.concatenate(parts, axis=1)
        y_ref[...] = y
        h1 = h0_ref[...] + _dot(y, wo_ref[...])
        h1_ref[...] = h1
        r = lax.rsqrt(jnp.mean(h1 * h1, axis=-1, keepdims=True) + EPS)
        u2 = (h1 * r * g2_ref[...]).astype(BF16)
        u2_ref[...] = u2
        gate = _sigmoid(_dot(u2, wg_ref[...]))
        e = _dot(p_ref[...].astype(BF16), wp_ref[...])
        e_ref[...] = e.astype(BF16)
        gate_ref[...] = gate.astype(BF16)
        h2 = h1 + e * gate
        if target is None:
            h2_ref[...] = h2
        else:
            @pl.when(pl.program_id(0) == 0)
            def _():
                loss_ref[0][...] = jnp.zeros_like(loss_ref[0])

            err = h2 - t_ref[...]
            h2_ref[...] = err * (1.0 / D_MODEL)
            part = jnp.sum(jnp.sum(err * err, axis=1, keepdims=True), axis=0, keepdims=True)
            loss_ref[0][...] += part * (0.5 / D_MODEL)

    tok = lambda w: pl.BlockSpec((TOK_T, w), lambda i: (i, 0))
    head = pl.BlockSpec((N_HEADS // 2, TOK_T, HEAD_PAD), lambda i: (0, i, 0))
    full = lambda a, b: pl.BlockSpec((a, b), lambda i: (0, 0))
    act = lambda dt: jax.ShapeDtypeStruct((S, D_MODEL), dt)
    fused = target is not None
    return pl.pallas_call(
        body, name="mid_fwd_loss" if fused else "mid_fwd", grid=(S // TOK_T,),
        in_specs=[head, head,
                  pl.BlockSpec((TOK_T, D_BRANCH), lambda i: (i, 3)), pl.BlockSpec((TOK_T, D_BRANCH), lambda i: (i, 7)),
                  tok(D_MODEL), _slab_spec(p_lead, (TOK_T, PLE_DIM), lambda i: (i, 0)), full(D_MODEL, D_MODEL),
                  full(D_MODEL, D_MODEL), full(PLE_DIM, D_MODEL), full(1, D_MODEL)] + [tok(D_MODEL)] * fused,
        out_specs=[tok(D_MODEL)] * 6 + [full(8, 128)] * fused,
        out_shape=[act(BF16), act(F32), act(F32), act(BF16), act(BF16), act(BF16)]
        + [jax.ShapeDtypeStruct((8, 128), F32)] * fused,
        compiler_params=_cp("arbitrary" if fused else "parallel"),
    )(oa, ob, zm, zm, h0, p, w_out, w_pg, w_ple, g2, *([target] * fused))


def _bias_tables(full_range):
    T = ATT_T
    nb = 1 if full_range else DILATED_PATTERNS[-1][0] // T + 1
    r = lax.broadcasted_iota(jnp.int32, (nb, T, T), 2)
    c = lax.broadcasted_iota(jnp.int32, (nb, T, T), 1)
    b = lax.broadcasted_iota(jnp.int32, (nb, T, T), 0)
    delta = T * b + r - c
    if full_range:
        bias = jnp.where(delta >= 0, 0.0, NEG).astype(F32)
    else:
        mult = jnp.zeros((nb, T, T), F32)
        for window, dil in DILATED_PATTERNS:
            ok = (delta >= 0) & (delta <= window) & (delta % dil == 0)
            mult = mult + ok.astype(F32)
        bias = jnp.where(mult > 0, jnp.log2(jnp.maximum(mult, 1.0)), NEG).astype(F32)
    return bias


def _call_with_rider(body, name, grid, rider, in_specs, out_specs, out_shape, scratch_shapes, operands,
                     semantics=("parallel", "arbitrary")):
    if rider is None:
        return pl.pallas_call(body, name=name, grid=grid, in_specs=in_specs, out_specs=out_specs,
                              out_shape=out_shape, scratch_shapes=scratch_shapes,
                              compiler_params=_cp(*semantics))(*operands)
    n, n_in, n_out = rider.n, len(in_specs), len(out_specs)

    def wrapped(*refs):
        ins, r_ins = refs[:n_in], refs[n_in:n_in + n]
        outs, r_outs = refs[n_in + n:n_in + n + n_out], refs[n_in + n + n_out:n_in + 2 * n + n_out]
        scratch, sems = refs[n_in + 2 * n + n_out:-3], refs[-3:]
        step = [pl.program_id(a) for a in range(len(grid))]

        @pl.when(functools.reduce(jnp.logical_and, [s == 0 for s in step]))
        def _():
            rider.start(r_ins, r_outs, sems)

        body(*ins, *outs, *scratch)

        @pl.when(functools.reduce(jnp.logical_and, [s == g - 1 for s, g in zip(step, grid)]))
        def _():
            rider.wait(r_ins, r_outs, sems)

    return pl.pallas_call(
        wrapped, name=name, grid=grid, in_specs=list(in_specs) + rider.in_specs,
        out_specs=list(out_specs) + rider.out_specs, out_shape=list(out_shape) + rider.out_shapes,
        scratch_shapes=list(scratch_shapes) + rider.scratch_shapes,
        compiler_params=_cp(*["arbitrary"] * len(grid)))(*operands, *rider.arrays)


def _attn_fwd(q, k, v, table_t, full_range, name, rider=None):
    H, S, _ = q.shape
    T = ATT_T
    nb = table_t.shape[0]
    HB = ATT_FWD_HEADS
    KC = ATT_CHUNK
    chunks = [slice(c, c + KC) for c in range(0, T, KC)]
    fold = lambda x, op: functools.reduce(op, [x[r:r + 8] for r in range(0, KC, 8)])

    def body(q_ref, k_ref, v_ref, tab_ref, o_ref, lse_ref, *scratch):
        st_refs, pt_refs, acc_refs = scratch[:HB], scratch[HB:2 * HB], scratch[2 * HB:]
        i = pl.program_id(1)
        rows = lambda j: pl.ds(pl.multiple_of(j * T, T), T)

        def scores(hh, j):
            st_refs[hh][...] = _dot_nt(k_ref[hh, rows(j), :], q_ref[hh])

        def block(j, b, nxt, stats):
            out = []
            for hh, (m, l) in enumerate(stats):
                st_ref, pt_ref, acc_ref = st_refs[hh], pt_refs[hh], acc_refs[hh]
                mx = None
                for ch in chunks:
                    x = st_ref[ch, :]
                    if b is not None:
                        x = x + tab_ref[b, ch, :]
                        st_ref[ch, :] = x
                    x = fold(x, jnp.maximum)
                    mx = x if mx is None else jnp.maximum(mx, x)
                m_new = jnp.maximum(m, jnp.max(mx, axis=0, keepdims=True))
                alpha = jnp.exp2(m - m_new)
                ls = None
                for ch in chunks:
                    pc = jnp.exp2(st_ref[ch, :] - m_new)
                    pt_ref[ch, :] = pc.astype(BF16)
                    pc = fold(pc, jnp.add)
                    ls = pc if ls is None else ls + pc
                if nxt is not None:
                    scores(hh, nxt)
                acc_ref[...] = alpha * acc_ref[...] + _dot_tn(v_ref[hh, rows(j), :], pt_ref[...])
                out.append((m_new, alpha * l + jnp.sum(ls, axis=0, keepdims=True)))
            return tuple(out)

        lo = 0 if full_range else jnp.maximum(i - (nb - 1), 0)
        for hh in range(HB):
            acc_refs[hh][...] = jnp.zeros_like(acc_refs[hh])
            scores(hh, lo)
        stats = lax.fori_loop(lo, i, lambda j, st: block(j, None if full_range else i - j, j + 1, st),
                              ((jnp.full((1, T), NEG, F32), jnp.zeros((1, T), F32)),) * HB)
        stats = block(i, 0, None, stats)
        o_t = [acc_refs[hh][...] * (1.0 / l) for hh, (m, l) in enumerate(stats)]
        for hh, (m, l) in enumerate(stats):
            lse_ref[hh, 0] = m + jnp.log2(l)
        for hp in range(HB // 2):
            o_ref[hp] = jnp.concatenate([o_t[2 * hp][:HEAD_DIM], o_t[2 * hp + 1][:HEAD_DIM]], axis=0).T

    return _call_with_rider(
        body, name, (H // HB, S // T), rider,
        in_specs=[pl.BlockSpec((HB, T, HEAD_PAD), lambda h, i: (h, i, 0)),
                  pl.BlockSpec((HB, S, HEAD_PAD), lambda h, i: (h, 0, 0), pipeline_mode=pl.Buffered(1)),
                  pl.BlockSpec((HB, S, HEAD_PAD), lambda h, i: (h, 0, 0), pipeline_mode=pl.Buffered(1)),
                  pl.BlockSpec((nb, T, T), lambda h, i: (0, 0, 0), pipeline_mode=pl.Buffered(1))],
        out_specs=[pl.BlockSpec((HB // 2, T, HEAD_PAD), lambda h, i: (h, i, 0)),
                   pl.BlockSpec((HB, 1, 1, T), lambda h, i: (h, i, 0, 0))],
        out_shape=[jax.ShapeDtypeStruct((H // 2, S, HEAD_PAD), F32), jax.ShapeDtypeStruct((H, S // T, 1, T), F32)],
        scratch_shapes=([pltpu.VMEM((T, T), F32)] * HB + [pltpu.VMEM((T, T), BF16)] * HB
                        + [pltpu.VMEM((HEAD_PAD, T), F32)] * HB),
        operands=(q, k, v, table_t))


def _attn_bwd(q, k, v, do, lse, dd, table_t, full_range, name, rider=None):
    H, S, _ = q.shape
    T = ATT_T
    nq = S // T
    nb = table_t.shape[0]
    HB = ATT_BWD_HEADS
    KC = ATT_CHUNK
    chunks = [slice(c, c + KC) for c in range(0, T, KC)]

    def body(q_ref, do_ref, lse_ref, dd_ref, k_ref, v_ref, tab_ref, dq_hbm, dk_ref, dv_ref, *scratch):
        st_refs, dpt_refs, pt_refs, dst_refs = (scratch[n * HB:(n + 1) * HB] for n in range(4))
        dq_ref, dq_sem = scratch[4 * HB:]
        h = pl.program_id(0)
        j = pl.program_id(1)

        @pl.when(j == 0)
        def _():
            dq_ref[...] = jnp.zeros_like(dq_ref)

        dk_ref[...] = jnp.zeros_like(dk_ref)
        dv_ref[...] = jnp.zeros_like(dv_ref)

        def step(i, b):
            rows = pl.ds(pl.multiple_of(i * T, T), T)
            for hh in range(HB):
                st_refs[hh][...] = _dot_nt(k_ref[hh], q_ref[hh, rows, :])
                dpt_refs[hh][...] = _dot_nt(v_ref[hh], do_ref[hh, rows, :])
            for hh in range(HB):
                lse_i = lse_ref[hh, i]
                dd_i = dd_ref[hh, i]
                for ch in chunks:
                    x = st_refs[hh][ch, :]
                    if b is not None:
                        x = x + tab_ref[b, ch, :]
                    pc = jnp.exp2(x - lse_i)
                    pt_refs[hh][ch, :] = pc.astype(BF16)
                    dst_refs[hh][ch, :] = (pc * (dpt_refs[hh][ch, :] - dd_i)).astype(BF16)
                dv_ref[hh] += _dot(pt_refs[hh][...], do_ref[hh, rows, :])
                dk_ref[hh] += _dot(dst_refs[hh][...], q_ref[hh, rows, :])
                dq_ref[hh, rows, :] += _dot_tn(dst_refs[hh][...], k_ref[hh])

        step(j, 0)
        if full_range:
            pl.loop(j + 1, nq)(lambda i: step(i, None))
        else:
            pl.loop(j + 1, jnp.minimum(j + nb, nq))(lambda i: step(i, i - j))

        @pl.when(j == nq - 1)
        def _():
            out = pltpu.make_async_copy(dq_ref, dq_hbm.at[pl.ds(h * HB, HB)], dq_sem)
            out.start()
            out.wait()

    once = dict(pipeline_mode=pl.Buffered(1))
    per_head = pl.BlockSpec((HB, S, HEAD_PAD), lambda h, j: (h, 0, 0), **once)
    rows = pl.BlockSpec((HB, nq, 1, T), lambda h, j: (h, 0, 0, 0))
    blk = pl.BlockSpec((HB, T, HEAD_PAD), lambda h, j: (h, j, 0))
    shp = jax.ShapeDtypeStruct((H, S, HEAD_PAD), F32)
    return _call_with_rider(
        body, name, (H // HB, nq), rider,
        in_specs=[per_head, per_head, rows, rows, blk, blk,
                  pl.BlockSpec((nb, T, T), lambda h, j: (0, 0, 0), **once)],
        out_specs=[pl.BlockSpec(memory_space=pltpu.HBM), blk, blk], out_shape=[shp, shp, shp],
        scratch_shapes=([pltpu.VMEM((T, T), F32)] * (2 * HB) + [pltpu.VMEM((T, T), BF16)] * (2 * HB)
                        + [pltpu.VMEM((HB, S, HEAD_PAD), F32), pltpu.SemaphoreType.DMA]),
        operands=(q, do, lse, dd, k, v, table_t))


def _mid_bwd(dh2, h1, e, gate, g2, w_pg, w_out, oa, ob, zm):
    S = dh2.shape[0]

    def body(dh2_ref, h1_ref, e_ref, gate_ref, g2_ref, wg_ref, wo_ref, oa_ref, ob_ref, ga_ref, gb_ref,
             dh1_ref, dh1b_ref, de_ref, dpre_ref, doa_ref, dob_ref, dga_ref, dgb_ref, dd_ref, dg2_ref):
        @pl.when(pl.program_id(0) == 0)
        def _():
            dg2_ref[...] = jnp.zeros_like(dg2_ref)

        lane = lax.broadcasted_iota(jnp.int32, (TOK_T, HEAD_PAD), 1)
        lo_half = lane < HEAD_DIM
        dh2 = dh2_ref[...]
        gate = gate_ref[...]
        de_ref[...] = (dh2 * gate).astype(BF16)
        dpre = (dh2 * e_ref[...] * gate * (1.0 - gate)).astype(BF16)
        dpre_ref[...] = dpre
        du2 = _dot_nt(dpre, wg_ref[...])
        h1 = h1_ref[...]
        r = lax.rsqrt(jnp.mean(h1 * h1, axis=-1, keepdims=True) + EPS)
        xh = h1 * r
        a = du2 * g2_ref[...]
        dh1 = dh2 + r * (a - xh * jnp.mean(a * xh, axis=-1, keepdims=True))
        dg2_ref[...] += jnp.sum(du2 * xh, axis=0, keepdims=True)
        dh1_ref[...] = dh1
        dh1b = dh1.astype(BF16)
        dh1b_ref[...] = dh1b
        dy = _dot_nt(dh1b, wo_ref[...])
        dd = jnp.zeros((TOK_T, HEAD_PAD), F32)
        for bi, (o_ref, g_ref, do_ref, dg_ref) in enumerate(
                ((oa_ref, ga_ref, doa_ref, dga_ref), (ob_ref, gb_ref, dob_ref, dgb_ref))):
            for pi in range(N_HEADS // 2):
                col = bi * D_BRANCH + HEAD_PAD * pi
                dyp = dy[:, col:col + HEAD_PAD]
                g = g_ref[:, HEAD_PAD * pi:HEAD_PAD * (pi + 1)].astype(F32)
                sg = _sigmoid(g)
                o_pair = o_ref[pi]
                dg_ref[:, HEAD_PAD * pi:HEAD_PAD * (pi + 1)] = (
                    dyp * o_pair * (sg * (1.0 + g * (1.0 - sg)))).astype(BF16)
                dop = dyp * (g * sg)
                prod = dop * o_pair
                for hh, d_head, mine in ((2 * pi, dop, lo_half),
                                         (2 * pi + 1, pltpu.roll(dop, HEAD_DIM, 1), ~lo_half)):
                    do_ref[hh] = jnp.where(lo_half, d_head, 0.0).astype(BF16)
                    dsum = jnp.sum(jnp.where(mine, prod, 0.0), axis=1, keepdims=True)
                    dd = dd + jnp.where(lane == bi * N_HEADS + hh, dsum, 0.0)
        dd_ref[...] = dd.T[:2 * N_HEADS, :]

    tok = lambda w: pl.BlockSpec((TOK_T, w), lambda i: (i, 0))
    head = pl.BlockSpec((N_HEADS, TOK_T, HEAD_PAD), lambda i: (0, i, 0))
    pairs = pl.BlockSpec((N_HEADS // 2, TOK_T, HEAD_PAD), lambda i: (0, i, 0))
    full = lambda a, b: pl.BlockSpec((a, b), lambda i: (0, 0))
    act = lambda w, dt: jax.ShapeDtypeStruct((S, w), dt)
    hshape = lambda w, dt: jax.ShapeDtypeStruct((N_HEADS, S, w), dt)
    return pl.pallas_call(
        body, name="mid_bwd", grid=(S // TOK_T,),
        in_specs=[tok(D_MODEL)] * 4 + [full(1, D_MODEL), full(D_MODEL, D_MODEL), full(D_MODEL, D_MODEL), pairs, pairs,
                                      pl.BlockSpec((TOK_T, D_BRANCH), lambda i: (i, 3)),
                                      pl.BlockSpec((TOK_T, D_BRANCH), lambda i: (i, 7))],
        out_specs=[tok(D_MODEL)] * 4 + [head, head, tok(D_BRANCH), tok(D_BRANCH),
                                       pl.BlockSpec((2 * N_HEADS, TOK_T), lambda i: (0, i)), full(1, D_MODEL)],
        out_shape=[act(D_MODEL, F32), act(D_MODEL, BF16), act(D_MODEL, BF16), act(D_MODEL, BF16),
                   hshape(HEAD_PAD, BF16), hshape(HEAD_PAD, BF16), act(D_BRANCH, BF16), act(D_BRANCH, BF16),
                   jax.ShapeDtypeStruct((2 * N_HEADS, S), F32), jax.ShapeDtypeStruct((1, D_MODEL), F32)],
        compiler_params=_cp("arbitrary"),
    )(dh2, h1, e, gate, g2, w_pg, w_out, oa, ob, zm, zm)


def _prep_bwd(dqa, dka, dva, dqb, dkb, dvb, zm, qkg, rope_c, rope_a, rope_b, dga, dgb, zf, bf, triu):
    S = zm.shape[0]
    n = S // TOK_T

    def body(dqa_ref, dka_ref, dva_ref, dqb_ref, dkb_ref, dvb_ref, z_ref, g_ref, rc_ref, ra_ref, rb_ref,
             dga_ref, dgb_ref, zf_ref, b_ref, tri_ref, dz_ref, dzf_ref, dqkg_ref, db_ref, carry):
        @pl.when(pl.program_id(0) == 0)
        def _():
            dqkg_ref[...] = jnp.zeros_like(dqkg_ref)
            db_ref[...] = jnp.zeros_like(db_ref)
            carry[...] = jnp.zeros_like(carry)

        lane = lax.broadcasted_iota(jnp.int32, (TOK_T, HEAD_PAD), 1)
        lo_half = lane < HEAD_DIM
        rc, ra, rb = rc_ref[...], ra_ref[...], rb_ref[...]
        same_head = _same_head()

        def unrope(dy):
            return dy * rc + pltpu.roll(dy * ra, ROPE_HALF, 1) + pltpu.roll(dy * rb, HEAD_PAD - ROPE_HALF, 1)

        def norm_bwd(col, gi, dy):
            x = z_ref[:, col:col + HEAD_PAD].astype(F32)
            r = _pair_rsqrt(x, same_head)
            xh = x * r
            dqkg_ref[gi:gi + 1, :] += jnp.sum(dy * xh, axis=0, keepdims=True)
            a = dy * g_ref[gi:gi + 1, :]
            dz_ref[:, col:col + HEAD_PAD] = (r * (a - xh * _pair_mean(a * xh, same_head))).astype(BF16)

        dc = jnp.zeros((TOK_T, HEAD_PAD), F32)
        for pi in range(N_HEADS // 2):
            col = HEAD_PAD * pi
            norm_bwd(col, 0, _pair(dqa_ref, pi, lo_half) * SCALE)
            norm_bwd(D_BRANCH + col, 1, _pair(dka_ref, pi, lo_half) * LN2)
            dz_ref[:, 2 * D_BRANCH + col:2 * D_BRANCH + col + HEAD_PAD] = _pair(dva_ref, pi, lo_half).astype(BF16)
            norm_bwd(4 * D_BRANCH + col, 2, unrope(_pair(dqb_ref, pi, lo_half) * SCALE))
            norm_bwd(5 * D_BRANCH + col, 3, unrope(_pair(dkb_ref, pi, lo_half) * LN2))
            dz_ref[:, 6 * D_BRANCH + col:6 * D_BRANCH + col + HEAD_PAD] = _pair(dvb_ref, pi, lo_half).astype(BF16)
            for hh in (2 * pi, 2 * pi + 1):
                dch = dka_ref[hh][:, HEAD_DIM:HEAD_DIM + 1] + dqa_ref[hh][:, HEAD_DIM + 3:HEAD_DIM + 4]
                dc = dc + jnp.where(lane == hh, dch, 0.0)
        dz_ref[:, 3 * D_BRANCH:4 * D_BRANCH] = dga_ref[...]
        dz_ref[:, 7 * D_BRANCH:8 * D_BRANCH] = dgb_ref[...]
        tri = tri_ref[...]
        dlf = sum(_dot(tri, part) for part in _split3(dc)) + carry[...]
        carry[...] = dlf[0:1, :]
        dfa = dlf * (1.0 - _sigmoid(zf_ref[...] + b_ref[...]))
        dzf_ref[...] = dfa.astype(BF16)
        db_ref[...] += jnp.sum(dfa, axis=0, keepdims=True)

    tok = lambda w: pl.BlockSpec((TOK_T, w), lambda i: (n - 1 - i, 0))
    head = pl.BlockSpec((N_HEADS, TOK_T, HEAD_PAD), lambda i: (0, n - 1 - i, 0))
    fixed = lambda a, b: pl.BlockSpec((a, b), lambda i: (0, 0))
    return pl.pallas_call(
        body, name="prep_bwd", grid=(n,),
        in_specs=[head] * 6 + [tok(N_MAIN), fixed(4, 128), tok(128), tok(128), tok(128), tok(D_BRANCH), tok(D_BRANCH),
                               tok(128), fixed(1, 128), fixed(TOK_T, TOK_T)],
        out_specs=[tok(N_MAIN), tok(128), fixed(4, 128), fixed(1, 128)],
        out_shape=[jax.ShapeDtypeStruct((S, N_MAIN), BF16), jax.ShapeDtypeStruct((S, 128), BF16),
                   jax.ShapeDtypeStruct((4, 128), F32), jax.ShapeDtypeStruct((1, 128), F32)],
        scratch_shapes=[pltpu.VMEM((1, 128), F32)],
        compiler_params=_cp("arbitrary"),
    )(dqa, dka, dva, dqb, dkb, dvb, zm, qkg, rope_c, rope_a, rope_b, dga, dgb, zf, bf, triu)


def _inproj_bwd(dzm, dzf, wm, wf, h0, dh1, g, rider=None):
    S = h0.shape[0]

    def body(dzm_ref, dzf_ref, wm_ref, wf_ref, h_ref, dh1_ref, g_ref, dh0_ref, dg_ref):
        @pl.when(pl.program_id(0) == 0)
        def _():
            dg_ref[...] = jnp.zeros_like(dg_ref)

        du = _dot_nt(dzm_ref[...], wm_ref[...]) + _dot_nt(dzf_ref[...], wf_ref[...])
        x = h_ref[...]
        r = lax.rsqrt(jnp.mean(x * x, axis=-1, keepdims=True) + EPS)
        xh = x * r
        a = du * g_ref[...]
        dh0_ref[...] = dh1_ref[...] + r * (a - xh * jnp.mean(a * xh, axis=-1, keepdims=True))
        dg_ref[...] += jnp.sum(du * xh, axis=0, keepdims=True)

    tok = lambda w: pl.BlockSpec((TOK_T, w), lambda i: (i, 0))
    full = lambda a, b: pl.BlockSpec((a, b), lambda i: (0, 0))
    return _call_with_rider(
        body, "inproj_bwd", (S // TOK_T,), rider,
        in_specs=[tok(N_MAIN), tok(128), full(D_MODEL, N_MAIN), full(D_MODEL, 128), tok(D_MODEL), tok(D_MODEL),
                  full(1, D_MODEL)],
        out_specs=[tok(D_MODEL), full(1, D_MODEL)],
        out_shape=[jax.ShapeDtypeStruct((S, D_MODEL), F32), jax.ShapeDtypeStruct((1, D_MODEL), F32)],
        scratch_shapes=[], operands=(dzm, dzf, wm, wf, h0, dh1, g), semantics=("arbitrary",))


def _wgrad(a, b, name, a_lead=()):
    S, M = a.shape[len(a_lead):]
    N = b.shape[1]
    tn = min(N, 2048)
    ts = 512
    last = S // ts - 1

    def body(a_ref, b_ref, o_ref, acc_ref):
        @pl.when(pl.program_id(1) == 0)
        def _():
            acc_ref[...] = jnp.zeros_like(acc_ref)

        acc_ref[...] += _dot_tn(a_ref[...].astype(BF16), b_ref[...])

        @pl.when(pl.program_id(1) == last)
        def _():
            o_ref[...] = acc_ref[...].astype(BF16)

    return pl.pallas_call(
        body, name=name, grid=(N // tn, S // ts),
        in_specs=[_slab_spec(a_lead, (ts, M), lambda n, s: (s, 0)), pl.BlockSpec((ts, tn), lambda n, s: (s, n))],
        out_specs=pl.BlockSpec((M, tn), lambda n, s: (0, n)),
        out_shape=jax.ShapeDtypeStruct((M, N), BF16),
        scratch_shapes=[pltpu.VMEM((M, tn), F32)],
        compiler_params=_cp("parallel", "arbitrary"),
    )(a, b)


def _rope_tables(positions):
    inv_freq = ROPE_THETA ** (-jnp.arange(ROPE_HALF, dtype=F32) / ROPE_HALF)
    ang = positions.astype(F32)[:, None] * inv_freq
    cos, sin = jnp.cos(ang), jnp.sin(ang)
    S = positions.shape[0]
    one, zero = jnp.ones((S, HEAD_DIM - 2 * ROPE_HALF), F32), jnp.zeros((S, HEAD_DIM - 2 * ROPE_HALF), F32)
    z8 = jnp.zeros((S, ROPE_HALF), F32)
    rc = jnp.concatenate([cos, cos, one], axis=1)
    ra = jnp.concatenate([-sin, z8, zero], axis=1)
    rb = jnp.concatenate([z8, sin, zero], axis=1)
    return tuple(jnp.tile(t, (1, 2)) for t in (rc, ra, rb))


def _in_operands(w_in):
    w_in = w_in.astype(BF16)
    wm = jnp.concatenate([w_in[:, :4 * D_BRANCH], w_in[:, 4 * D_BRANCH + N_HEADS:]], axis=1)
    wf = jnp.pad(w_in[:, 4 * D_BRANCH:4 * D_BRANCH + N_HEADS], ((0, 0), (0, 128 - N_HEADS)))
    return dict(wm=wm, wf=wf)


def _layer_weights(w_in, w_out, w_ple, w_pg):
    return dict(_in_operands(w_in), w_out=w_out.astype(BF16), w_ple=w_ple.astype(BF16), w_pg=w_pg.astype(BF16))


def _row(v, width=128):
    v = v.reshape(1, -1).astype(F32)
    return jnp.pad(v, ((0, 0), (0, width - v.shape[1])))


def _layer_fwd(h0, p, rope, tabs, w, norm_g, b_f, qk_g, ple_g, rider=None, late=None, target=None):
    g1 = norm_g.reshape(1, D_MODEL)
    g2 = ple_g.reshape(1, D_MODEL)
    qkg = jnp.tile(qk_g, (1, 2))
    bf = _row(b_f)
    zm, zf, u = _inproj_fwd(h0, g1, w["wm"], w["wf"])
    qa, ka, va, qb, kb, vb = _prep_fwd(zm, zf, bf, tabs["tril"], qkg, *rope)
    oa, lse_a, *arrivals = _attn_fwd(qa, ka, va, tabs["fox"], True, "fox_fwd", rider)
    if late is not None:
        w = {**w, **late(arrivals)}
    ob, lse_b = _attn_fwd(qb, kb, vb, tabs["dil"], False, "dil_fwd")
    y, h1, h2, u2, e, gate, *loss = _mid_fwd(oa, ob, zm, h0, p, w["w_out"], w["w_pg"], w["w_ple"], g2, target)
    saved = dict(h0=h0, p=p, zm=zm, zf=zf, u=u, qa=qa, ka=ka, va=va, qb=qb, kb=kb, vb=vb, oa=oa, ob=ob,
                 lse_a=lse_a, lse_b=lse_b, y=y, h1=h1, u2=u2, e=e, gate=gate, g1=g1, g2=g2, qkg=qkg, bf=bf, w=w)
    return (h2, saved, arrivals, *loss)


def _layer_bwd(dh2, sv, rope, tabs, make_rider=None, make_last_rider=None):
    S = dh2.shape[0]
    nq = S // ATT_T
    w = sv["w"]
    rows = lambda a: a.reshape(N_HEADS, nq, 1, ATT_T)
    (dh1, dh1b, de, dpre, doa, dob, dga, dgb, dd, dg2) = _mid_bwd(
        dh2, sv["h1"], sv["e"], sv["gate"], sv["g2"], w["w_pg"], w["w_out"], sv["oa"], sv["ob"], sv["zm"])
    dda, ddb = dd[:N_HEADS], dd[N_HEADS:]
    early = dict(w_out=_wgrad(sv["y"], dh1b, "wgrad_out"), w_ple=_wgrad(sv["p"][0], de, "wgrad_ple", sv["p"][1]),
                 w_ple_gate=_wgrad(sv["u2"], dpre, "wgrad_gate"))
    rider = None if make_rider is None else make_rider(early)
    dqa, dka, dva, *arrivals = _attn_bwd(sv["qa"], sv["ka"], sv["va"], doa, sv["lse_a"], rows(dda), tabs["fox"],
                                         True, "fox_bwd", rider)
    dqb, dkb, dvb = _attn_bwd(sv["qb"], sv["kb"], sv["vb"], dob, sv["lse_b"], rows(ddb), tabs["dil"], False,
                              "dil_bwd")
    dzm, dzf, dqkg, dbf = _prep_bwd(dqa, dka, dva, dqb, dkb, dvb, sv["zm"], sv["qkg"], *rope, dga, dgb,
                                    sv["zf"], sv["bf"], tabs["triu"])
    dwm = _wgrad(sv["u"], dzm, "wgrad_in")
    dwf = _wgrad(sv["u"], dzf, "wgrad_f")
    dw_in = jnp.concatenate([dwm[:, :4 * D_BRANCH], dwf[:, :N_HEADS], dwm[:, 4 * D_BRANCH:]], axis=1)
    last_rider = None if make_last_rider is None else make_last_rider(dw_in)
    dh0, dg1, *last_arrivals = _inproj_bwd(dzm, dzf, w["wm"], w["wf"], sv["h0"], dh1, sv["g1"], last_rider)
    grads = dict(norm_g=dg1.reshape(D_MODEL), w_in=dw_in, b_f=dbf[0, :N_HEADS],
                 qk_norm_g=dqkg[:, :HEAD_DIM] + dqkg[:, HEAD_DIM:], ple_norm_g=dg2.reshape(D_MODEL), **early)
    return dh0, grads, arrivals + last_arrivals


def _tables():
    T = TOK_T
    r = lax.broadcasted_iota(jnp.int32, (T, T), 0)
    c = lax.broadcasted_iota(jnp.int32, (T, T), 1)
    return dict(fox=_bias_tables(True), dil=_bias_tables(False),
                tril=(c <= r).astype(BF16), triu=(c >= r).astype(BF16))


def _local_step(x, p, positions, target, layers, small):
    rope = _rope_tables(positions)
    tabs = _tables()
    ws = [_layer_weights(*lw) for lw in layers]
    h = x
    saved = []
    for li, (w, lp, sm) in enumerate(zip(ws, p, small)):
        h, sv, _, *loss = _layer_fwd(h, (lp, ()), rope, tabs, w, *sm, target=target if li == len(ws) - 1 else None)
        saved.append(sv)
    dh, (loss,) = h, loss
    grads = [None] * len(ws)
    for li in reversed(range(len(ws))):
        dh, grads[li], _ = _layer_bwd(dh, saved[li], rope, tabs)
    return loss[0, 0], dh, grads


def _peers():
    x, y, c = lax.axis_index("x"), lax.axis_index("y"), lax.axis_index("c")
    me = 4 * x + 2 * y + c
    flip = lambda v, bit: 1 - v if bit else v
    return me, [(flip(x, k & 4), flip(y, k & 2), flip(c, k & 1)) for k in range(1, N_DEV)]


def _sel(ref, kind, d):
    if kind == "whole":
        return ref
    if kind == "slot":
        return ref.at[d]
    block = pl.ds(pl.multiple_of(d * 128, 128), 128)
    return ref.at[block, :] if kind == "rows" else ref.at[:, block]


class _Pushes:
    def __init__(self, arrays, src_kinds, dst_kinds, out_shapes):
        self.arrays, self.n = list(arrays), len(arrays)
        self.src_kinds, self.dst_kinds = src_kinds, dst_kinds
        self.out_shapes = [jax.ShapeDtypeStruct(s, a.dtype) for s, a in zip(out_shapes, arrays)]
        hbm = pl.BlockSpec(memory_space=pltpu.HBM)
        self.in_specs, self.out_specs = [hbm] * self.n, [hbm] * self.n
        self.scratch_shapes = [pltpu.SemaphoreType.DMA((N_DEV - 1, self.n)),
                               pltpu.SemaphoreType.DMA((N_DEV - 1, self.n)), pltpu.SemaphoreType.DMA((self.n,))]

    def _copies(self, ins, outs, sems):
        send_sems, recv_sems, local_sems = sems
        me, peers = _peers()
        src = lambda a, d: _sel(ins[a], self.src_kinds[a], d)
        dst = lambda a: _sel(outs[a], self.dst_kinds[a], me)
        local = [pltpu.make_async_copy(src(a, me), dst(a), local_sems.at[a]) for a in range(self.n)]
        remote = [pltpu.make_async_remote_copy(
            src_ref=src(a, 4 * px + 2 * py + pc), dst_ref=dst(a), send_sem=send_sems.at[k, a],
            recv_sem=recv_sems.at[k, a], device_id=(px, py, pc), device_id_type=pl.DeviceIdType.MESH)
            for k, (px, py, pc) in enumerate(peers) for a in range(self.n)]
        return local + remote

    def start(self, ins, outs, sems):
        for cp in self._copies(ins, outs, sems):
            cp.start()

    def wait(self, ins, outs, sems):
        for cp in self._copies(ins, outs, sems):
            cp.wait()


def _exchange(name, pushes):
    n = pushes.n

    def body(*refs):
        pushes.start(refs[:n], refs[n:2 * n], refs[2 * n:])
        pushes.wait(refs[:n], refs[n:2 * n], refs[2 * n:])

    return pl.pallas_call(body, name=name, in_specs=pushes.in_specs, out_specs=pushes.out_specs,
                          out_shape=pushes.out_shapes, scratch_shapes=pushes.scratch_shapes)(*pushes.arrays)


def _gather_two_level(shard, name):
    def body(x_ref, out_ref, send_sems, recv_sems, local_sem):
        x, y, c = lax.axis_index("x"), lax.axis_index("y"), lax.axis_index("c")
        me, sibling = (x, y, c), (x, y, 1 - c)
        chips = [(1 - x, y), (x, 1 - y), (1 - x, 1 - y)]
        slot = lambda px, py, pc: out_ref.at[4 * px + 2 * py + pc]

        def copy(k, block, to, src=None):
            return pltpu.make_async_remote_copy(
                src_ref=slot(*block) if src is None else src, dst_ref=slot(*block), send_sem=send_sems.at[k],
                recv_sem=recv_sems.at[k], device_id=to, device_id_type=pl.DeviceIdType.MESH)

        mine = pltpu.make_async_copy(x_ref, slot(*me), local_sem)
        mine.start()
        first = [copy(0, me, sibling, src=x_ref)] + [copy(1 + j, me, (*chip, c), src=x_ref)
                                                     for j, chip in enumerate(chips)]
        for cp in first:
            cp.start()
        passed = [copy(4 + j, (*chip, c), sibling) for j, chip in enumerate(chips)]
        for j, chip in enumerate(chips):
            copy(1 + j, (*chip, c), me).wait_recv()
            passed[j].start()
        copy(0, sibling, me).wait_recv()
        for j, chip in enumerate(chips):
            copy(4 + j, (*chip, 1 - c), me).wait_recv()
        for cp in first + passed:
            cp.wait_send()
        mine.wait()

    hbm = pl.BlockSpec(memory_space=pltpu.HBM)
    return pl.pallas_call(
        body, name=name, in_specs=[hbm], out_specs=hbm,
        out_shape=jax.ShapeDtypeStruct((N_DEV,) + shard.shape, shard.dtype),
        scratch_shapes=[pltpu.SemaphoreType.DMA((N_DEV - 1,)), pltpu.SemaphoreType.DMA((N_DEV - 1,)),
                        pltpu.SemaphoreType.DMA],
    )(shard)


def _gather_pushes(shards, kinds):
    full = {"slot": lambda s: (N_DEV,) + s, "rows": lambda s: (N_DEV * s[0], s[1]),
            "cols": lambda s: (s[0], N_DEV * s[1])}
    return _Pushes(shards, ["whole"] * len(shards), kinds, [full[k](a.shape) for a, k in zip(shards, kinds)])


def _scatter_pushes(partials, kinds):
    part = {"slot": lambda s: s[1:], "rows": lambda s: (128, s[1]), "cols": lambda s: (s[0], 128),
            "whole": lambda s: s}
    return _Pushes(partials, kinds, ["slot"] * len(partials),
                   [(N_DEV,) + part[k](a.shape) for a, k in zip(partials, kinds)])


def _adamw(name, parts, w, m, v, rows):
    L, R, C = w.shape

    def body(p_ref, w_ref, m_ref, v_ref, g_ref, d_ref, nm_ref, nv_ref):
        g = p_ref[0, 0].astype(F32)
        for s in range(1, N_DEV):
            g = g + p_ref[s, 0].astype(F32)
        g_ref[0] = g
        nm = ADAM_B1 * m_ref[0] + (1.0 - ADAM_B1) * g
        nv = ADAM_B2 * v_ref[0] + (1.0 - ADAM_B2) * (g * g)
        nm_ref[0] = nm
        nv_ref[0] = nv
        m_hat = nm / (1.0 - ADAM_B1 ** ADAM_STEP)
        v_hat = nv / (1.0 - ADAM_B2 ** ADAM_STEP)
        d_ref[0] = -ADAM_LR * (m_hat / (jnp.sqrt(v_hat) + ADAM_EPS) + ADAM_WD * w_ref[0])

    blk = pl.BlockSpec((1, rows, C), lambda l, i: (l, i, 0))
    shp = jax.ShapeDtypeStruct((L, R, C), F32)
    return pl.pallas_call(
        body, name=name, grid=(L, R // rows),
        in_specs=[pl.BlockSpec((N_DEV, 1, rows, C), lambda l, i: (0, l, i, 0)), blk, blk, blk],
        out_specs=[blk] * 4, out_shape=[shp] * 4,
        compiler_params=_cp("parallel", "parallel"),
    )(parts, w, m, v)


SMALL_ROWS = 40
LOSS_ROW = 37


def _pack_small(norm_g, ple_g, qk_g, b_f, last_row):
    rows = lambda a: a.astype(F32).reshape(-1, 128)
    flat = jnp.concatenate([rows(norm_g), rows(ple_g), rows(qk_g), _row(b_f.reshape(-1)), last_row], axis=0)
    return jnp.pad(flat, ((0, SMALL_ROWS - flat.shape[0]), (0, 0)))


def _unpack_small(flat):
    return (flat[0:16].reshape(2, D_MODEL), flat[16:32].reshape(2, D_MODEL), flat[32:36].reshape(2, 4, HEAD_DIM),
            flat[36, :2 * N_HEADS].reshape(2, N_HEADS))


def kernel(x, p, positions, norm_g, w_in, b_f, qk_norm_g, w_out, w_ple, ple_norm_g, w_ple_gate, loss_target, m_norm_g, m_w_in, m_b_f, m_qk_norm_g, m_w_out, m_w_ple, m_ple_norm_g, m_w_ple_gate, v_norm_g, v_w_in, v_b_f, v_qk_norm_g, v_w_out, v_w_ple, v_ple_norm_g, v_w_ple_gate):
    bf16 = lambda a: a.astype(BF16)
    rows_in = W_IN_ROWS // 2
    flat_in = lambda a: bf16(a).reshape(rows_in, 128)
    full_in = lambda g: g.reshape(N_DEV, D_MODEL, W_IN_SHARD).transpose(1, 0, 2).reshape(D_MODEL, N_IN)
    small = [(norm_g[l], b_f[l], qk_norm_g[l], ple_norm_g[l]) for l in range(2)]
    rope = _rope_tables(positions[0])
    tabs = _tables()

    g_in0 = _gather_two_level(flat_in(w_in[0]), "gather_first")
    rest = _gather_pushes([flat_in(w_in[1])] + [bf16(a[l]) for l in range(2) for a in (w_out, w_ple, w_ple_gate)],
                          ["slot"] + ["rows", "cols", "rows"] * 2)
    late = lambda got: dict(w_out=got[1], w_ple=got[2], w_pg=got[3])
    h, sv0, got = _layer_fwd(x[0], (p, (0, 0)), rope, tabs, _in_operands(full_in(g_in0)), *small[0], rest, late)
    w1 = dict(_in_operands(full_in(got[0])), w_out=got[4], w_ple=got[5], w_pg=got[6])
    dh, sv1, _, loss = _layer_fwd(h, (p, (1, 0)), rope, tabs, w1, *small[1], target=loss_target[0])
    dh, gr1, _ = _layer_bwd(dh, sv1, rope, tabs)

    by_dest = lambda d: d.reshape(D_MODEL, N_DEV, W_IN_SHARD).transpose(1, 0, 2).reshape(N_DEV, rows_in, 128)
    big = ("w_out", "w_ple", "w_ple_gate")
    riding = lambda early: _scatter_pushes([by_dest(gr1["w_in"])] + [gr1[n] for n in big] + [early[n] for n in big],
                                           ["slot"] + ["rows", "cols", "rows"] * 2)
    riding_last = lambda dw_in: _scatter_pushes([by_dest(dw_in)], ["slot"])
    dx, gr0, (r_in1, *r_big, r_in0) = _layer_bwd(dh, sv0, rope, tabs, riding, riding_last)
    grads = (gr0, gr1)
    stack = lambda name: jnp.stack([gl[name] for gl in grads], axis=0)
    small_part = _pack_small(stack("norm_g"), stack("ple_norm_g"), stack("qk_norm_g"), stack("b_f"),
                             _row(loss[0, 0].reshape(1)))
    (r_small,) = _exchange("exchange_small", _scatter_pushes([small_part], ["whole"]))
    r_in = jnp.concatenate([r_in0, r_in1], axis=1)
    r_out, r_ple, r_pg = (jnp.stack([r_big[3 + k], r_big[k]], axis=1) for k in range(3))

    zero_row = jnp.zeros((1, 128), F32)
    small_of = lambda ng, pg, qk, bf: _pack_small(ng, pg, qk, bf, zero_row)[None]
    flat = lambda a: a.reshape(1, W_IN_ROWS, 128)
    outs = dict(
        w_in=[o.reshape(w_in.shape) for o in
              _adamw("adamw_in", r_in[:, None], flat(w_in), flat(m_w_in), flat(v_w_in), W_IN_TILE)],
        w_out=_adamw("adamw_out", r_out, w_out, m_w_out, v_w_out, 128),
        w_ple=_adamw("adamw_ple", r_ple, w_ple, m_w_ple, v_w_ple, 256),
        w_pg=_adamw("adamw_gate", r_pg, w_ple_gate, m_w_ple_gate, v_w_ple_gate, 128),
        small=_adamw("adamw_small", r_small[:, None], small_of(norm_g, ple_norm_g, qk_norm_g, b_f),
                     small_of(m_norm_g, m_ple_norm_g, m_qk_norm_g, m_b_f),
                     small_of(v_norm_g, v_ple_norm_g, v_qk_norm_g, v_b_f), SMALL_ROWS))
    leaves = []
    for kind in range(4):
        ng, pg, qk, bf = _unpack_small(outs["small"][kind][0])
        leaves += [ng, outs["w_in"][kind], bf, qk, outs["w_out"][kind], outs["w_ple"][kind], pg, outs["w_pg"][kind]]
    return (outs["small"][0][0, LOSS_ROW, 0], dx[None], *leaves)
```

```python
import functools

import jax
import jax.numpy as jnp
from jax import lax
from jax.experimental import pallas as pl
from jax.experimental.pallas import tpu as pltpu

F32 = jnp.float32
BF16 = jnp.bfloat16

D_MODEL = 1024
HEAD_DIM = 64
N_HEADS = 8
HEAD_PAD = 128
D_BRANCH = N_HEADS * HEAD_DIM
N_MAIN = 8 * D_BRANCH
N_IN = N_MAIN + N_HEADS
PLE_DIM = 256
ROPE_THETA = 500000.0
ROPE_HALF = 8
EPS = 1e-6
NEG = -1e30
SCALE = HEAD_DIM ** -0.5
LOG2E = 1.4426950408889634
LN2 = 0.6931471805599453
DILATED_PATTERNS = ((128, 1), (512, 4), (2048, 16))
N_DEV = 8
W_IN_SHARD = N_IN // N_DEV
W_IN_ROWS = 2 * D_MODEL * W_IN_SHARD // 128
W_IN_TILE = W_IN_ROWS // 19

ADAM_LR = 0.001
ADAM_B1 = 0.9
ADAM_B2 = 0.999
ADAM_EPS = 1e-08
ADAM_WD = 0.01
ADAM_STEP = 10

ATT_T = 512
ATT_FWD_HEADS = 4
ATT_BWD_HEADS = 2
ATT_CHUNK = 32
TOK_T = 256
VMEM_LIMIT = 56 * 1024 * 1024


def _slab_spec(lead, block, index):
    return pl.BlockSpec((None,) * len(lead) + block, lambda *g: (*lead, *index(*g)))


def _cp(*sem):
    return pltpu.CompilerParams(dimension_semantics=sem, vmem_limit_bytes=VMEM_LIMIT)


def _sigmoid(x):
    return 1.0 / (1.0 + jnp.exp(-x))


def _split3(x):
    hi = x.astype(BF16)
    r1 = x - hi.astype(F32)
    mid = r1.astype(BF16)
    lo = (r1 - mid.astype(F32)).astype(BF16)
    return hi, mid, lo


def _dot(a, b):
    return jnp.dot(a, b, preferred_element_type=F32)


def _dot_nt(a, b):
    return lax.dot_general(a, b, (((1,), (1,)), ((), ())), preferred_element_type=F32)


def _dot_tn(a, b):
    return lax.dot_general(a, b, (((0,), (0,)), ((), ())), preferred_element_type=F32)


def _inproj_fwd(h, g, wm, wf):
    S = h.shape[0]

    def body(h_ref, g_ref, wm_ref, wf_ref, zm_ref, zf_ref, u_ref):
        x = h_ref[...]
        r = lax.rsqrt(jnp.mean(x * x, axis=-1, keepdims=True) + EPS)
        u = (x * r * g_ref[...]).astype(BF16)
        u_ref[...] = u
        zm_ref[...] = _dot(u, wm_ref[...]).astype(BF16)
        zf_ref[...] = _dot(u, wf_ref[...])

    return pl.pallas_call(
        body, name="inproj_fwd", grid=(S // TOK_T,),
        in_specs=[pl.BlockSpec((TOK_T, D_MODEL), lambda i: (i, 0)),
                  pl.BlockSpec((1, D_MODEL), lambda i: (0, 0)),
                  pl.BlockSpec((D_MODEL, N_MAIN), lambda i: (0, 0)),
                  pl.BlockSpec((D_MODEL, 128), lambda i: (0, 0))],
        out_specs=[pl.BlockSpec((TOK_T, N_MAIN), lambda i: (i, 0)),
                   pl.BlockSpec((TOK_T, 128), lambda i: (i, 0)),
                   pl.BlockSpec((TOK_T, D_MODEL), lambda i: (i, 0))],
        out_shape=[jax.ShapeDtypeStruct((S, N_MAIN), BF16), jax.ShapeDtypeStruct((S, 128), F32),
                   jax.ShapeDtypeStruct((S, D_MODEL), BF16)],
        compiler_params=_cp("parallel"),
    )(h, g, wm, wf)


def _log_sigmoid(x):
    return jnp.minimum(x, 0.0) - jnp.log(1.0 + jnp.exp(-jnp.abs(x)))


def _same_head():
    r = lax.broadcasted_iota(jnp.int32, (HEAD_PAD, HEAD_PAD), 0) // HEAD_DIM
    c = lax.broadcasted_iota(jnp.int32, (HEAD_PAD, HEAD_PAD), 1) // HEAD_DIM
    return (r == c).astype(BF16)


def _pair_mean(x, same_head):
    hi = x.astype(BF16)
    lo = (x - hi.astype(F32)).astype(BF16)
    return (_dot(hi, same_head) + _dot(lo, same_head)) * (1.0 / HEAD_DIM)


def _pair_rsqrt(x, same_head):
    return lax.rsqrt(_pair_mean(x * x, same_head) + EPS)


def _prep_fwd(zm, zf, bf, tril, qkg, rope_c, rope_a, rope_b):
    S = zm.shape[0]
    shp = jax.ShapeDtypeStruct((N_HEADS, S, HEAD_PAD), BF16)

    def body(z_ref, zf_ref, b_ref, tri_ref, g_ref, rc_ref, ra_ref, rb_ref,
             qa_ref, ka_ref, va_ref, qb_ref, kb_ref, vb_ref, carry):
        @pl.when(pl.program_id(0) == 0)
        def _():
            carry[...] = jnp.zeros_like(carry)

        tri = tri_ref[...]
        cs = sum(_dot(tri, part) for part in _split3(_log_sigmoid(zf_ref[...] + b_ref[...]))) + carry[...]
        carry[...] = cs[TOK_T - 1:TOK_T, :]
        lane = lax.broadcasted_iota(jnp.int32, (TOK_T, HEAD_PAD), 1)
        lo_half = lane < HEAD_DIM
        aug = (lane >= HEAD_DIM) & (lane < HEAD_DIM + 3)
        q_pad = jnp.where(aug, -1.0, 0.0)
        rc, ra, rb = rc_ref[...], ra_ref[...], rb_ref[...]
        same_head = _same_head()

        def norm(col, gi):
            x = z_ref[:, col:col + HEAD_PAD].astype(F32)
            return x * _pair_rsqrt(x, same_head) * g_ref[gi:gi + 1, :]

        def rope(y):
            return y * rc + pltpu.roll(y, HEAD_PAD - ROPE_HALF, 1) * ra + pltpu.roll(y, ROPE_HALF, 1) * rb

        def put(ref, pi, y, pad_even, pad_odd):
            ref[2 * pi] = jnp.where(lo_half, y, pad_even).astype(BF16)
            ref[2 * pi + 1] = jnp.where(lo_half, pltpu.roll(y, HEAD_DIM, 1), pad_odd).astype(BF16)

        def k_pad(h):
            ch = cs[:, h:h + 1] * LOG2E
            hi = ch.astype(BF16).astype(F32)
            mid = (ch - hi).astype(BF16).astype(F32)
            lo = ch - hi - mid
            ones = jnp.where(lane == HEAD_DIM + 3, 1.0, 0.0)
            return jnp.where(lane == HEAD_DIM, hi, jnp.where(lane == HEAD_DIM + 1, mid,
                                                              jnp.where(lane == HEAD_DIM + 2, lo, ones)))

        for pi in range(N_HEADS // 2):
            col = HEAD_PAD * pi
            put(qa_ref, pi, norm(col, 0) * (SCALE * LOG2E), q_pad, q_pad)
            put(ka_ref, pi, norm(D_BRANCH + col, 1), k_pad(2 * pi), k_pad(2 * pi + 1))
            put(va_ref, pi, z_ref[:, 2 * D_BRANCH + col:2 * D_BRANCH + col + HEAD_PAD].astype(F32), 0.0, 0.0)
            put(qb_ref, pi, rope(norm(4 * D_BRANCH + col, 2)) * (SCALE * LOG2E), 0.0, 0.0)
            put(kb_ref, pi, rope(norm(5 * D_BRANCH + col, 3)), 0.0, 0.0)
            put(vb_ref, pi, z_ref[:, 6 * D_BRANCH + col:6 * D_BRANCH + col + HEAD_PAD].astype(F32), 0.0, 0.0)

    tok = lambda w: pl.BlockSpec((TOK_T, w), lambda i: (i, 0))
    head = pl.BlockSpec((N_HEADS, TOK_T, HEAD_PAD), lambda i: (0, i, 0))
    return pl.pallas_call(
        body, name="prep_fwd", grid=(S // TOK_T,),
        in_specs=[tok(N_MAIN), tok(128), pl.BlockSpec((1, 128), lambda i: (0, 0)),
                  pl.BlockSpec((TOK_T, TOK_T), lambda i: (0, 0)), pl.BlockSpec((4, 128), lambda i: (0, 0)),
                  tok(128), tok(128), tok(128)],
        out_specs=[head] * 6, out_shape=[shp] * 6,
        scratch_shapes=[pltpu.VMEM((1, 128), F32)],
        compiler_params=_cp("arbitrary"),
    )(zm, zf, bf, tril, qkg, rope_c, rope_a, rope_b)


def _pair(ref, pi, lo_half):
    return jnp.where(lo_half, ref[2 * pi].astype(F32), pltpu.roll(ref[2 * pi + 1].astype(F32), HEAD_DIM, 1))


def _mid_fwd(oa, ob, zm, h0, p, w_out, w_pg, w_ple, g2, target=None):
    S = h0.shape[0]
    p, p_lead = p

    def body(oa_ref, ob_ref, ga_ref, gb_ref, h0_ref, p_ref, wo_ref, wg_ref, wp_ref, g2_ref, *rest):
        t_ref, rest = (rest[0], rest[1:]) if target is not None else (None, rest)
        y_ref, h1_ref, h2_ref, u2_ref, e_ref, gate_ref, *loss_ref = rest
        parts = []
        for o_ref, g_ref in ((oa_ref, ga_ref), (ob_ref, gb_ref)):
            for pi in range(N_HEADS // 2):
                g = g_ref[:, HEAD_PAD * pi:HEAD_PAD * (pi + 1)].astype(F32)
                parts.append((o_ref[pi] * (g * _sigmoid(g))).astype(BF16))
        y = jnp.concatenate(parts, axis=1)
        y_ref[...] = y
        h1 = h0_ref[...] + _dot(y, wo_ref[...])
        h1_ref[...] = h1
        r = lax.rsqrt(jnp.mean(h1 * h1, axis=-1, keepdims=True) + EPS)
        u2 = (h1 * r * g2_ref[...]).astype(BF16)
        u2_ref[...] = u2
        gate = _sigmoid(_dot(u2, wg_ref[...]))
        e = _dot(p_ref[...].astype(BF16), wp_ref[...])
        e_ref[...] = e.astype(BF16)
        gate_ref[...] = gate.astype(BF16)
        h2 = h1 + e * gate
        if target is None:
            h2_ref[...] = h2
        else:
            @pl.when(pl.program_id(0) == 0)
            def _():
                loss_ref[0][...] = jnp.zeros_like(loss_ref[0])

            err = h2 - t_ref[...]
            h2_ref[...] = err * (1.0 / D_MODEL)
            part = jnp.sum(jnp.sum(err * err, axis=1, keepdims=True), axis=0, keepdims=True)
            loss_ref[0][...] += part * (0.5 / D_MODEL)

    tok = lambda w: pl.BlockSpec((TOK_T, w), lambda i: (i, 0))
    head = pl.BlockSpec((N_HEADS // 2, TOK_T, HEAD_PAD), lambda i: (0, i, 0))
    full = lambda a, b: pl.BlockSpec((a, b), lambda i: (0, 0))
    act = lambda dt: jax.ShapeDtypeStruct((S, D_MODEL), dt)
    fused = target is not None
    return pl.pallas_call(
        body, name="mid_fwd_loss" if fused else "mid_fwd", grid=(S // TOK_T,),
        in_specs=[head, head,
                  pl.BlockSpec((TOK_T, D_BRANCH), lambda i: (i, 3)), pl.BlockSpec((TOK_T, D_BRANCH), lambda i: (i, 7)),
                  tok(D_MODEL), _slab_spec(p_lead, (TOK_T, PLE_DIM), lambda i: (i, 0)), full(D_MODEL, D_MODEL),
                  full(D_MODEL, D_MODEL), full(PLE_DIM, D_MODEL), full(1, D_MODEL)] + [tok(D_MODEL)] * fused,
        out_specs=[tok(D_MODEL)] * 6 + [full(8, 128)] * fused,
        out_shape=[act(BF16), act(F32), act(F32), act(BF16), act(BF16), act(BF16)]
        + [jax.ShapeDtypeStruct((8, 128), F32)] * fused,
        compiler_params=_cp("arbitrary" if fused else "parallel"),
    )(oa, ob, zm, zm, h0, p, w_out, w_pg, w_ple, g2, *([target] * fused))


def _bias_tables(full_range):
    T = ATT_T
    nb = 1 if full_range else DILATED_PATTERNS[-1][0] // T + 1
    r = lax.broadcasted_iota(jnp.int32, (nb, T, T), 2)
    c = lax.broadcasted_iota(jnp.int32, (nb, T, T), 1)
    b = lax.broadcasted_iota(jnp.int32, (nb, T, T), 0)
    delta = T * b + r - c
    if full_range:
        bias = jnp.where(delta >= 0, 0.0, NEG).astype(F32)
    else:
        mult = jnp.zeros((nb, T, T), F32)
        for window, dil in DILATED_PATTERNS:
            ok = (delta >= 0) & (delta <= window) & (delta % dil == 0)
            mult = mult + ok.astype(F32)
        bias = jnp.where(mult > 0, jnp.log2(jnp.maximum(mult, 1.0)), NEG).astype(F32)
    return bias


def _call_with_rider(body, name, grid, rider, in_specs, out_specs, out_shape, scratch_shapes, operands,
                     semantics=("parallel", "arbitrary")):
    if rider is None:
        return pl.pallas_call(body, name=name, grid=grid, in_specs=in_specs, out_specs=out_specs,
                              out_shape=out_shape, scratch_shapes=scratch_shapes,
                              compiler_params=_cp(*semantics))(*operands)
    n, n_in, n_out = rider.n, len(in_specs), len(out_specs)

    def wrapped(*refs):
        ins, r_ins = refs[:n_in], refs[n_in:n_in + n]
        outs, r_outs = refs[n_in + n:n_in + n + n_out], refs[n_in + n + n_out:n_in + 2 * n + n_out]
        scratch, sems = refs[n_in + 2 * n + n_out:-3], refs[-3:]
        step = [pl.program_id(a) for a in range(len(grid))]

        @pl.when(functools.reduce(jnp.logical_and, [s == 0 for s in step]))
        def _():
            rider.start(r_ins, r_outs, sems)

        body(*ins, *outs, *scratch)

        @pl.when(functools.reduce(jnp.logical_and, [s == g - 1 for s, g in zip(step, grid)]))
        def _():
            rider.wait(r_ins, r_outs, sems)

    return pl.pallas_call(
        wrapped, name=name, grid=grid, in_specs=list(in_specs) + rider.in_specs,
        out_specs=list(out_specs) + rider.out_specs, out_shape=list(out_shape) + rider.out_shapes,
        scratch_shapes=list(scratch_shapes) + rider.scratch_shapes,
        compiler_params=_cp(*["arbitrary"] * len(grid)))(*operands, *rider.arrays)


def _attn_fwd(q, k, v, table_t, full_range, name, rider=None):
    H, S, _ = q.shape
    T = ATT_T
    nb = table_t.shape[0]
    HB = ATT_FWD_HEADS
    KC = ATT_CHUNK
    chunks = [slice(c, c + KC) for c in range(0, T, KC)]
    fold = lambda x, op: functools.reduce(op, [x[r:r + 8] for r in range(0, KC, 8)])

    def body(q_ref, k_ref, v_ref, tab_ref, o_ref, lse_ref, *scratch):
        st_refs, pt_refs, acc_refs = scratch[:HB], scratch[HB:2 * HB], scratch[2 * HB:]
        i = pl.program_id(1)
        rows = lambda j: pl.ds(pl.multiple_of(j * T, T), T)

        def scores(hh, j):
            st_refs[hh][...] = _dot_nt(k_ref[hh, rows(j), :], q_ref[hh])

        def block(j, b, nxt, stats):
            out = []
            for hh, (m, l) in enumerate(stats):
                st_ref, pt_ref, acc_ref = st_refs[hh], pt_refs[hh], acc_refs[hh]
                mx = None
                for ch in chunks:
                    x = st_ref[ch, :]
                    if b is not None:
                        x = x + tab_ref[b, ch, :]
                        st_ref[ch, :] = x
                    x = fold(x, jnp.maximum)
                    mx = x if mx is None else jnp.maximum(mx, x)
                m_new = jnp.maximum(m, jnp.max(mx, axis=0, keepdims=True))
                alpha = jnp.exp2(m - m_new)
                ls = None
                for ch in chunks:
                    pc = jnp.exp2(st_ref[ch, :] - m_new)
                    pt_ref[ch, :] = pc.astype(BF16)
                    pc = fold(pc, jnp.add)
                    ls = pc if ls is None else ls + pc
                if nxt is not None:
                    scores(hh, nxt)
                acc_ref[...] = alpha * acc_ref[...] + _dot_tn(v_ref[hh, rows(j), :], pt_ref[...])
                out.append((m_new, alpha * l + jnp.sum(ls, axis=0, keepdims=True)))
            return tuple(out)

        lo = 0 if full_range else jnp.maximum(i - (nb - 1), 0)
        for hh in range(HB):
            acc_refs[hh][...] = jnp.zeros_like(acc_refs[hh])
            scores(hh, lo)
        stats = lax.fori_loop(lo, i, lambda j, st: block(j, None if full_range else i - j, j + 1, st),
                              ((jnp.full((1, T), NEG, F32), jnp.zeros((1, T), F32)),) * HB)
        stats = block(i, 0, None, stats)
        o_t = [acc_refs[hh][...] * (1.0 / l) for hh, (m, l) in enumerate(stats)]
        for hh, (m, l) in enumerate(stats):
            lse_ref[hh, 0] = m + jnp.log2(l)
        for hp in range(HB // 2):
            o_ref[hp] = jnp.concatenate([o_t[2 * hp][:HEAD_DIM], o_t[2 * hp + 1][:HEAD_DIM]], axis=0).T

    return _call_with_rider(
        body, name, (H // HB, S // T), rider,
        in_specs=[pl.BlockSpec((HB, T, HEAD_PAD), lambda h, i: (h, i, 0)),
                  pl.BlockSpec((HB, S, HEAD_PAD), lambda h, i: (h, 0, 0), pipeline_mode=pl.Buffered(1)),
                  pl.BlockSpec((HB, S, HEAD_PAD), lambda h, i: (h, 0, 0), pipeline_mode=pl.Buffered(1)),
                  pl.BlockSpec((nb, T, T), lambda h, i: (0, 0, 0), pipeline_mode=pl.Buffered(1))],
        out_specs=[pl.BlockSpec((HB // 2, T, HEAD_PAD), lambda h, i: (h, i, 0)),
                   pl.BlockSpec((HB, 1, 1, T), lambda h, i: (h, i, 0, 0))],
        out_shape=[jax.ShapeDtypeStruct((H // 2, S, HEAD_PAD), F32), jax.ShapeDtypeStruct((H, S // T, 1, T), F32)],
        scratch_shapes=([pltpu.VMEM((T, T), F32)] * HB + [pltpu.VMEM((T, T), BF16)] * HB
                        + [pltpu.VMEM((HEAD_PAD, T), F32)] * HB),
        operands=(q, k, v, table_t))


def _attn_bwd(q, k, v, do, lse, dd, table_t, full_range, name, rider=None, narrow=(False, False, False)):
    H, S, _ = q.shape
    T = ATT_T
    nq = S // T
    nb = table_t.shape[0]
    HB = ATT_BWD_HEADS
    KC = ATT_CHUNK
    chunks = [slice(c, c + KC) for c in range(0, T, KC)]

    def body(q_ref, do_ref, lse_ref, dd_ref, k_ref, v_ref, tab_ref, dq_hbm, dk_ref, dv_ref, *scratch):
        st_refs, dpt_refs, pt_refs, dst_refs = (scratch[n * HB:(n + 1) * HB] for n in range(4))
        dq_ref, dk_acc, dv_acc, *dq_cast, dq_sem = scratch[4 * HB:]
        h = pl.program_id(0)
        j = pl.program_id(1)

        @pl.when(j == 0)
        def _():
            dq_ref[...] = jnp.zeros_like(dq_ref)

        dk_acc[...] = jnp.zeros_like(dk_acc)
        dv_acc[...] = jnp.zeros_like(dv_acc)

        def step(i, b):
            rows = pl.ds(pl.multiple_of(i * T, T), T)
            for hh in range(HB):
                st_refs[hh][...] = _dot_nt(k_ref[hh], q_ref[hh, rows, :])
                dpt_refs[hh][...] = _dot_nt(v_ref[hh], do_ref[hh, rows, :])
            for hh in range(HB):
                lse_i = lse_ref[hh, i]
                dd_i = dd_ref[hh, i]
                for ch in chunks:
                    x = st_refs[hh][ch, :]
                    if b is not None:
                        x = x + tab_ref[b, ch, :]
                    pc = jnp.exp2(x - lse_i)
                    pt_refs[hh][ch, :] = pc.astype(BF16)
                    dst_refs[hh][ch, :] = (pc * (dpt_refs[hh][ch, :] - dd_i)).astype(BF16)
                dv_acc[hh] += _dot(pt_refs[hh][...], do_ref[hh, rows, :])
                dk_acc[hh] += _dot(dst_refs[hh][...], q_ref[hh, rows, :])
                dq_ref[hh, rows, :] += _dot_tn(dst_refs[hh][...], k_ref[hh])

        step(j, 0)
        if full_range:
            pl.loop(j + 1, nq)(lambda i: step(i, None))
        else:
            pl.loop(j + 1, jnp.minimum(j + nb, nq))(lambda i: step(i, i - j))
        dk_ref[...] = dk_acc[...].astype(dk_ref.dtype)
        dv_ref[...] = dv_acc[...].astype(dv_ref.dtype)

        @pl.when(j == nq - 1)
        def _():
            src = dq_ref
            if narrow[0]:
                src, = dq_cast
                src[...] = dq_ref[...].astype(BF16)
            out = pltpu.make_async_copy(src, dq_hbm.at[pl.ds(h * HB, HB)], dq_sem)
            out.start()
            out.wait()

    once = dict(pipeline_mode=pl.Buffered(1))
    per_head = pl.BlockSpec((HB, S, HEAD_PAD), lambda h, j: (h, 0, 0), **once)
    rows = pl.BlockSpec((HB, nq, 1, T), lambda h, j: (h, 0, 0, 0))
    blk = pl.BlockSpec((HB, T, HEAD_PAD), lambda h, j: (h, j, 0))
    shp = [jax.ShapeDtypeStruct((H, S, HEAD_PAD), BF16 if nar else F32) for nar in narrow]
    acc = pltpu.VMEM((HB, T, HEAD_PAD), F32)
    return _call_with_rider(
        body, name, (H // HB, nq), rider,
        in_specs=[per_head, per_head, rows, rows, blk, blk,
                  pl.BlockSpec((nb, T, T), lambda h, j: (0, 0, 0), **once)],
        out_specs=[pl.BlockSpec(memory_space=pltpu.HBM), blk, blk], out_shape=shp,
        scratch_shapes=([pltpu.VMEM((T, T), F32)] * (2 * HB) + [pltpu.VMEM((T, T), BF16)] * (2 * HB)
                        + [pltpu.VMEM((HB, S, HEAD_PAD), F32), acc, acc]
                        + [pltpu.VMEM((HB, S, HEAD_PAD), BF16)] * narrow[0] + [pltpu.SemaphoreType.DMA]),
        operands=(q, do, lse, dd, k, v, table_t))


def _mid_bwd(dh2, h1, e, gate, g2, w_pg, w_out, oa, ob, zm):
    S = dh2.shape[0]

    def body(dh2_ref, h1_ref, e_ref, gate_ref, g2_ref, wg_ref, wo_ref, oa_ref, ob_ref, ga_ref, gb_ref,
             dh1_ref, dh1b_ref, de_ref, dpre_ref, doa_ref, dob_ref, dga_ref, dgb_ref, dd_ref, dg2_ref):
        @pl.when(pl.program_id(0) == 0)
        def _():
            dg2_ref[...] = jnp.zeros_like(dg2_ref)

        lane = lax.broadcasted_iota(jnp.int32, (TOK_T, HEAD_PAD), 1)
        lo_half = lane < HEAD_DIM
        dh2 = dh2_ref[...]
        gate = gate_ref[...]
        de_ref[...] = (dh2 * gate).astype(BF16)
        dpre = (dh2 * e_ref[...] * gate * (1.0 - gate)).astype(BF16)
        dpre_ref[...] = dpre
        du2 = _dot_nt(dpre, wg_ref[...])
        h1 = h1_ref[...]
        r = lax.rsqrt(jnp.mean(h1 * h1, axis=-1, keepdims=True) + EPS)
        xh = h1 * r
        a = du2 * g2_ref[...]
        dh1 = dh2 + r * (a - xh * jnp.mean(a * xh, axis=-1, keepdims=True))
        dg2_ref[...] += jnp.sum(du2 * xh, axis=0, keepdims=True)
        dh1_ref[...] = dh1
        dh1b = dh1.astype(BF16)
        dh1b_ref[...] = dh1b
        dy = _dot_nt(dh1b, wo_ref[...])
        dd = jnp.zeros((TOK_T, HEAD_PAD), F32)
        for bi, (o_ref, g_ref, do_ref, dg_ref) in enumerate(
                ((oa_ref, ga_ref, doa_ref, dga_ref), (ob_ref, gb_ref, dob_ref, dgb_ref))):
            for pi in range(N_HEADS // 2):
                col = bi * D_BRANCH + HEAD_PAD * pi
                dyp = dy[:, col:col + HEAD_PAD]
                g = g_ref[:, HEAD_PAD * pi:HEAD_PAD * (pi + 1)].astype(F32)
                sg = _sigmoid(g)
                o_pair = o_ref[pi]
                dg_ref[:, HEAD_PAD * pi:HEAD_PAD * (pi + 1)] = (
                    dyp * o_pair * (sg * (1.0 + g * (1.0 - sg)))).astype(BF16)
                dop = dyp * (g * sg)
                prod = dop * o_pair
                for hh, d_head, mine in ((2 * pi, dop, lo_half),
                                         (2 * pi + 1, pltpu.roll(dop, HEAD_DIM, 1), ~lo_half)):
                    do_ref[hh] = jnp.where(lo_half, d_head, 0.0).astype(BF16)
                    dsum = jnp.sum(jnp.where(mine, prod, 0.0), axis=1, keepdims=True)
                    dd = dd + jnp.where(lane == bi * N_HEADS + hh, dsum, 0.0)
        dd_ref[...] = dd.T[:2 * N_HEADS, :]

    tok = lambda w: pl.BlockSpec((TOK_T, w), lambda i: (i, 0))
    head = pl.BlockSpec((N_HEADS, TOK_T, HEAD_PAD), lambda i: (0, i, 0))
    pairs = pl.BlockSpec((N_HEADS // 2, TOK_T, HEAD_PAD), lambda i: (0, i, 0))
    full = lambda a, b: pl.BlockSpec((a, b), lambda i: (0, 0))
    act = lambda w, dt: jax.ShapeDtypeStruct((S, w), dt)
    hshape = lambda w, dt: jax.ShapeDtypeStruct((N_HEADS, S, w), dt)
    return pl.pallas_call(
        body, name="mid_bwd", grid=(S // TOK_T,),
        in_specs=[tok(D_MODEL)] * 4 + [full(1, D_MODEL), full(D_MODEL, D_MODEL), full(D_MODEL, D_MODEL), pairs, pairs,
                                      pl.BlockSpec((TOK_T, D_BRANCH), lambda i: (i, 3)),
                                      pl.BlockSpec((TOK_T, D_BRANCH), lambda i: (i, 7))],
        out_specs=[tok(D_MODEL)] * 4 + [head, head, tok(D_BRANCH), tok(D_BRANCH),
                                       pl.BlockSpec((2 * N_HEADS, TOK_T), lambda i: (0, i)), full(1, D_MODEL)],
        out_shape=[act(D_MODEL, F32), act(D_MODEL, BF16), act(D_MODEL, BF16), act(D_MODEL, BF16),
                   hshape(HEAD_PAD, BF16), hshape(HEAD_PAD, BF16), act(D_BRANCH, BF16), act(D_BRANCH, BF16),
                   jax.ShapeDtypeStruct((2 * N_HEADS, S), F32), jax.ShapeDtypeStruct((1, D_MODEL), F32)],
        compiler_params=_cp("arbitrary"),
    )(dh2, h1, e, gate, g2, w_pg, w_out, oa, ob, zm, zm)


def _prep_bwd(dqa, dka, dva, dqb, dkb, dvb, zm, qkg, rope_c, rope_a, rope_b, dga, dgb, zf, bf, triu):
    S = zm.shape[0]
    n = S // TOK_T

    def body(dqa_ref, dka_ref, dva_ref, dqb_ref, dkb_ref, dvb_ref, z_ref, g_ref, rc_ref, ra_ref, rb_ref,
             dga_ref, dgb_ref, zf_ref, b_ref, tri_ref, dz_ref, dzf_ref, dqkg_ref, db_ref, carry):
        @pl.when(pl.program_id(0) == 0)
        def _():
            dqkg_ref[...] = jnp.zeros_like(dqkg_ref)
            db_ref[...] = jnp.zeros_like(db_ref)
            carry[...] = jnp.zeros_like(carry)

        lane = lax.broadcasted_iota(jnp.int32, (TOK_T, HEAD_PAD), 1)
        lo_half = lane < HEAD_DIM
        rc, ra, rb = rc_ref[...], ra_ref[...], rb_ref[...]
        same_head = _same_head()

        def unrope(dy):
            return dy * rc + pltpu.roll(dy * ra, ROPE_HALF, 1) + pltpu.roll(dy * rb, HEAD_PAD - ROPE_HALF, 1)

        def norm_bwd(col, gi, dy):
            x = z_ref[:, col:col + HEAD_PAD].astype(F32)
            r = _pair_rsqrt(x, same_head)
            xh = x * r
            dqkg_ref[gi:gi + 1, :] += jnp.sum(dy * xh, axis=0, keepdims=True)
            a = dy * g_ref[gi:gi + 1, :]
            dz_ref[:, col:col + HEAD_PAD] = (r * (a - xh * _pair_mean(a * xh, same_head))).astype(BF16)

        dc = jnp.zeros((TOK_T, HEAD_PAD), F32)
        for pi in range(N_HEADS // 2):
            col = HEAD_PAD * pi
            norm_bwd(col, 0, _pair(dqa_ref, pi, lo_half) * SCALE)
            norm_bwd(D_BRANCH + col, 1, _pair(dka_ref, pi, lo_half) * LN2)
            dz_ref[:, 2 * D_BRANCH + col:2 * D_BRANCH + col + HEAD_PAD] = _pair(dva_ref, pi, lo_half).astype(BF16)
            norm_bwd(4 * D_BRANCH + col, 2, unrope(_pair(dqb_ref, pi, lo_half) * SCALE))
            norm_bwd(5 * D_BRANCH + col, 3, unrope(_pair(dkb_ref, pi, lo_half) * LN2))
            dz_ref[:, 6 * D_BRANCH + col:6 * D_BRANCH + col + HEAD_PAD] = _pair(dvb_ref, pi, lo_half).astype(BF16)
            for hh in (2 * pi, 2 * pi + 1):
                dch = dka_ref[hh][:, HEAD_DIM:HEAD_DIM + 1] + dqa_ref[hh][:, HEAD_DIM + 3:HEAD_DIM + 4]
                dc = dc + jnp.where(lane == hh, dch, 0.0)
        dz_ref[:, 3 * D_BRANCH:4 * D_BRANCH] = dga_ref[...]
        dz_ref[:, 7 * D_BRANCH:8 * D_BRANCH] = dgb_ref[...]
        tri = tri_ref[...]
        dlf = sum(_dot(tri, part) for part in _split3(dc)) + carry[...]
        carry[...] = dlf[0:1, :]
        dfa = dlf * (1.0 - _sigmoid(zf_ref[...] + b_ref[...]))
        dzf_ref[...] = dfa.astype(BF16)
        db_ref[...] += jnp.sum(dfa, axis=0, keepdims=True)

    tok = lambda w: pl.BlockSpec((TOK_T, w), lambda i: (n - 1 - i, 0))
    head = pl.BlockSpec((N_HEADS, TOK_T, HEAD_PAD), lambda i: (0, n - 1 - i, 0))
    fixed = lambda a, b: pl.BlockSpec((a, b), lambda i: (0, 0))
    return pl.pallas_call(
        body, name="prep_bwd", grid=(n,),
        in_specs=[head] * 6 + [tok(N_MAIN), fixed(4, 128), tok(128), tok(128), tok(128), tok(D_BRANCH), tok(D_BRANCH),
                               tok(128), fixed(1, 128), fixed(TOK_T, TOK_T)],
        out_specs=[tok(N_MAIN), tok(128), fixed(4, 128), fixed(1, 128)],
        out_shape=[jax.ShapeDtypeStruct((S, N_MAIN), BF16), jax.ShapeDtypeStruct((S, 128), BF16),
                   jax.ShapeDtypeStruct((4, 128), F32), jax.ShapeDtypeStruct((1, 128), F32)],
        scratch_shapes=[pltpu.VMEM((1, 128), F32)],
        compiler_params=_cp("arbitrary"),
    )(dqa, dka, dva, dqb, dkb, dvb, zm, qkg, rope_c, rope_a, rope_b, dga, dgb, zf, bf, triu)


def _inproj_bwd(dzm, dzf, wm, wf, h0, dh1, g, rider=None):
    S = h0.shape[0]

    def body(dzm_ref, dzf_ref, wm_ref, wf_ref, h_ref, dh1_ref, g_ref, dh0_ref, dg_ref):
        @pl.when(pl.program_id(0) == 0)
        def _():
            dg_ref[...] = jnp.zeros_like(dg_ref)

        du = _dot_nt(dzm_ref[...], wm_ref[...]) + _dot_nt(dzf_ref[...], wf_ref[...])
        x = h_ref[...]
        r = lax.rsqrt(jnp.mean(x * x, axis=-1, keepdims=True) + EPS)
        xh = x * r
        a = du * g_ref[...]
        dh0_ref[...] = dh1_ref[...] + r * (a - xh * jnp.mean(a * xh, axis=-1, keepdims=True))
        dg_ref[...] += jnp.sum(du * xh, axis=0, keepdims=True)

    tok = lambda w: pl.BlockSpec((TOK_T, w), lambda i: (i, 0))
    full = lambda a, b: pl.BlockSpec((a, b), lambda i: (0, 0))
    return _call_with_rider(
        body, "inproj_bwd", (S // TOK_T,), rider,
        in_specs=[tok(N_MAIN), tok(128), full(D_MODEL, N_MAIN), full(D_MODEL, 128), tok(D_MODEL), tok(D_MODEL),
                  full(1, D_MODEL)],
        out_specs=[tok(D_MODEL), full(1, D_MODEL)],
        out_shape=[jax.ShapeDtypeStruct((S, D_MODEL), F32), jax.ShapeDtypeStruct((1, D_MODEL), F32)],
        scratch_shapes=[], operands=(dzm, dzf, wm, wf, h0, dh1, g), semantics=("arbitrary",))


def _wgrad(a, b, name, a_lead=()):
    S, M = a.shape[len(a_lead):]
    N = b.shape[1]
    tn = min(N, 2048)
    ts = 512
    last = S // ts - 1

    def body(a_ref, b_ref, o_ref, acc_ref):
        @pl.when(pl.program_id(1) == 0)
        def _():
            acc_ref[...] = jnp.zeros_like(acc_ref)

        acc_ref[...] += _dot_tn(a_ref[...].astype(BF16), b_ref[...])

        @pl.when(pl.program_id(1) == last)
        def _():
            o_ref[...] = acc_ref[...].astype(BF16)

    return pl.pallas_call(
        body, name=name, grid=(N // tn, S // ts),
        in_specs=[_slab_spec(a_lead, (ts, M), lambda n, s: (s, 0)), pl.BlockSpec((ts, tn), lambda n, s: (s, n))],
        out_specs=pl.BlockSpec((M, tn), lambda n, s: (0, n)),
        out_shape=jax.ShapeDtypeStruct((M, N), BF16),
        scratch_shapes=[pltpu.VMEM((M, tn), F32)],
        compiler_params=_cp("parallel", "arbitrary"),
    )(a, b)


def _rope_tables(positions):
    inv_freq = ROPE_THETA ** (-jnp.arange(ROPE_HALF, dtype=F32) / ROPE_HALF)
    ang = positions.astype(F32)[:, None] * inv_freq
    cos, sin = jnp.cos(ang), jnp.sin(ang)
    S = positions.shape[0]
    one, zero = jnp.ones((S, HEAD_DIM - 2 * ROPE_HALF), F32), jnp.zeros((S, HEAD_DIM - 2 * ROPE_HALF), F32)
    z8 = jnp.zeros((S, ROPE_HALF), F32)
    rc = jnp.concatenate([cos, cos, one], axis=1)
    ra = jnp.concatenate([-sin, z8, zero], axis=1)
    rb = jnp.concatenate([z8, sin, zero], axis=1)
    return tuple(jnp.tile(t, (1, 2)) for t in (rc, ra, rb))


def _in_operands(w_in):
    w_in = w_in.astype(BF16)
    wm = jnp.concatenate([w_in[:, :4 * D_BRANCH], w_in[:, 4 * D_BRANCH + N_HEADS:]], axis=1)
    wf = jnp.pad(w_in[:, 4 * D_BRANCH:4 * D_BRANCH + N_HEADS], ((0, 0), (0, 128 - N_HEADS)))
    return dict(wm=wm, wf=wf)


def _layer_weights(w_in, w_out, w_ple, w_pg):
    return dict(_in_operands(w_in), w_out=w_out.astype(BF16), w_ple=w_ple.astype(BF16), w_pg=w_pg.astype(BF16))


def _row(v, width=128):
    v = v.reshape(1, -1).astype(F32)
    return jnp.pad(v, ((0, 0), (0, width - v.shape[1])))


def _layer_fwd(h0, p, rope, tabs, w, norm_g, b_f, qk_g, ple_g, rider=None, late=None, target=None):
    g1 = norm_g.reshape(1, D_MODEL)
    g2 = ple_g.reshape(1, D_MODEL)
    qkg = jnp.tile(qk_g, (1, 2))
    bf = _row(b_f)
    zm, zf, u = _inproj_fwd(h0, g1, w["wm"], w["wf"])
    qa, ka, va, qb, kb, vb = _prep_fwd(zm, zf, bf, tabs["tril"], qkg, *rope)
    oa, lse_a, *arrivals = _attn_fwd(qa, ka, va, tabs["fox"], True, "fox_fwd", rider)
    if late is not None:
        w = {**w, **late(arrivals)}
    ob, lse_b = _attn_fwd(qb, kb, vb, tabs["dil"], False, "dil_fwd")
    y, h1, h2, u2, e, gate, *loss = _mid_fwd(oa, ob, zm, h0, p, w["w_out"], w["w_pg"], w["w_ple"], g2, target)
    saved = dict(h0=h0, p=p, zm=zm, zf=zf, u=u, qa=qa, ka=ka, va=va, qb=qb, kb=kb, vb=vb, oa=oa, ob=ob,
                 lse_a=lse_a, lse_b=lse_b, y=y, h1=h1, u2=u2, e=e, gate=gate, g1=g1, g2=g2, qkg=qkg, bf=bf, w=w)
    return (h2, saved, arrivals, *loss)


def _layer_bwd(dh2, sv, rope, tabs, make_rider=None, make_last_rider=None):
    S = dh2.shape[0]
    nq = S // ATT_T
    w = sv["w"]
    rows = lambda a: a.reshape(N_HEADS, nq, 1, ATT_T)
    (dh1, dh1b, de, dpre, doa, dob, dga, dgb, dd, dg2) = _mid_bwd(
        dh2, sv["h1"], sv["e"], sv["gate"], sv["g2"], w["w_pg"], w["w_out"], sv["oa"], sv["ob"], sv["zm"])
    dda, ddb = dd[:N_HEADS], dd[N_HEADS:]
    early = dict(w_out=_wgrad(sv["y"], dh1b, "wgrad_out"), w_ple=_wgrad(sv["p"][0], de, "wgrad_ple", sv["p"][1]),
                 w_ple_gate=_wgrad(sv["u2"], dpre, "wgrad_gate"))
    rider = None if make_rider is None else make_rider(early)
    dqa, dka, dva, *arrivals = _attn_bwd(sv["qa"], sv["ka"], sv["va"], doa, sv["lse_a"], rows(dda), tabs["fox"],
                                         True, "fox_bwd", rider, narrow=(False, False, True))
    dqb, dkb, dvb = _attn_bwd(sv["qb"], sv["kb"], sv["vb"], dob, sv["lse_b"], rows(ddb), tabs["dil"], False,
                              "dil_bwd", narrow=(True, True, True))
    dzm, dzf, dqkg, dbf = _prep_bwd(dqa, dka, dva, dqb, dkb, dvb, sv["zm"], sv["qkg"], *rope, dga, dgb,
                                    sv["zf"], sv["bf"], tabs["triu"])
    dwm = _wgrad(sv["u"], dzm, "wgrad_in")
    dwf = _wgrad(sv["u"], dzf, "wgrad_f")
    dw_in = jnp.concatenate([dwm[:, :4 * D_BRANCH], dwf[:, :N_HEADS], dwm[:, 4 * D_BRANCH:]], axis=1)
    last_rider = None if make_last_rider is None else make_last_rider(dw_in)
    dh0, dg1, *last_arrivals = _inproj_bwd(dzm, dzf, w["wm"], w["wf"], sv["h0"], dh1, sv["g1"], last_rider)
    grads = dict(norm_g=dg1.reshape(D_MODEL), w_in=dw_in, b_f=dbf[0, :N_HEADS],
                 qk_norm_g=dqkg[:, :HEAD_DIM] + dqkg[:, HEAD_DIM:], ple_norm_g=dg2.reshape(D_MODEL), **early)
    return dh0, grads, arrivals + last_arrivals


def _tables():
    T = TOK_T
    r = lax.broadcasted_iota(jnp.int32, (T, T), 0)
    c = lax.broadcasted_iota(jnp.int32, (T, T), 1)
    return dict(fox=_bias_tables(True), dil=_bias_tables(False),
                tril=(c <= r).astype(BF16), triu=(c >= r).astype(BF16))


def _local_step(x, p, positions, target, layers, small):
    rope = _rope_tables(positions)
    tabs = _tables()
    ws = [_layer_weights(*lw) for lw in layers]
    h = x
    saved = []
    for li, (w, lp, sm) in enumerate(zip(ws, p, small)):
        h, sv, _, *loss = _layer_fwd(h, (lp, ()), rope, tabs, w, *sm, target=target if li == len(ws) - 1 else None)
        saved.append(sv)
    dh, (loss,) = h, loss
    grads = [None] * len(ws)
    for li in reversed(range(len(ws))):
        dh, grads[li], _ = _layer_bwd(dh, saved[li], rope, tabs)
    return loss[0, 0], dh, grads


def _peers():
    x, y, c = lax.axis_index("x"), lax.axis_index("y"), lax.axis_index("c")
    me = 4 * x + 2 * y + c
    flip = lambda v, bit: 1 - v if bit else v
    return me, [(flip(x, k & 4), flip(y, k & 2), flip(c, k & 1)) for k in range(1, N_DEV)]


def _sel(ref, kind, d):
    if kind == "whole":
        return ref
    if kind == "slot":
        return ref.at[d]
    block = pl.ds(pl.multiple_of(d * 128, 128), 128)
    return ref.at[block, :] if kind == "rows" else ref.at[:, block]


class _Pushes:
    def __init__(self, arrays, src_kinds, dst_kinds, out_shapes):
        self.arrays, self.n = list(arrays), len(arrays)
        self.src_kinds, self.dst_kinds = src_kinds, dst_kinds
        self.out_shapes = [jax.ShapeDtypeStruct(s, a.dtype) for s, a in zip(out_shapes, arrays)]
        hbm = pl.BlockSpec(memory_space=pltpu.HBM)
        self.in_specs, self.out_specs = [hbm] * self.n, [hbm] * self.n
        self.scratch_shapes = [pltpu.SemaphoreType.DMA((N_DEV - 1, self.n)),
                               pltpu.SemaphoreType.DMA((N_DEV - 1, self.n)), pltpu.SemaphoreType.DMA((self.n,))]

    def _copies(self, ins, outs, sems):
        send_sems, recv_sems, local_sems = sems
        me, peers = _peers()
        src = lambda a, d: _sel(ins[a], self.src_kinds[a], d)
        dst = lambda a: _sel(outs[a], self.dst_kinds[a], me)
        local = [pltpu.make_async_copy(src(a, me), dst(a), local_sems.at[a]) for a in range(self.n)]
        remote = [pltpu.make_async_remote_copy(
            src_ref=src(a, 4 * px + 2 * py + pc), dst_ref=dst(a), send_sem=send_sems.at[k, a],
            recv_sem=recv_sems.at[k, a], device_id=(px, py, pc), device_id_type=pl.DeviceIdType.MESH)
            for k, (px, py, pc) in enumerate(peers) for a in range(self.n)]
        return local + remote

    def start(self, ins, outs, sems):
        for cp in self._copies(ins, outs, sems):
            cp.start()

    def wait(self, ins, outs, sems):
        for cp in self._copies(ins, outs, sems):
            cp.wait()


def _exchange(name, pushes):
    n = pushes.n

    def body(*refs):
        pushes.start(refs[:n], refs[n:2 * n], refs[2 * n:])
        pushes.wait(refs[:n], refs[n:2 * n], refs[2 * n:])

    return pl.pallas_call(body, name=name, in_specs=pushes.in_specs, out_specs=pushes.out_specs,
                          out_shape=pushes.out_shapes, scratch_shapes=pushes.scratch_shapes)(*pushes.arrays)


def _gather_two_level(shard, name):
    def body(x_ref, out_ref, send_sems, recv_sems, local_sem):
        x, y, c = lax.axis_index("x"), lax.axis_index("y"), lax.axis_index("c")
        me, sibling = (x, y, c), (x, y, 1 - c)
        chips = [(1 - x, y), (x, 1 - y), (1 - x, 1 - y)]
        slot = lambda px, py, pc: out_ref.at[4 * px + 2 * py + pc]

        def copy(k, block, to, src=None):
            return pltpu.make_async_remote_copy(
                src_ref=slot(*block) if src is None else src, dst_ref=slot(*block), send_sem=send_sems.at[k],
                recv_sem=recv_sems.at[k], device_id=to, device_id_type=pl.DeviceIdType.MESH)

        mine = pltpu.make_async_copy(x_ref, slot(*me), local_sem)
        mine.start()
        first = [copy(0, me, sibling, src=x_ref)] + [copy(1 + j, me, (*chip, c), src=x_ref)
                                                     for j, chip in enumerate(chips)]
        for cp in first:
            cp.start()
        passed = [copy(4 + j, (*chip, c), sibling) for j, chip in enumerate(chips)]
        for j, chip in enumerate(chips):
            copy(1 + j, (*chip, c), me).wait_recv()
            passed[j].start()
        copy(0, sibling, me).wait_recv()
        for j, chip in enumerate(chips):
            copy(4 + j, (*chip, 1 - c), me).wait_recv()
        for cp in first + passed:
            cp.wait_send()
        mine.wait()

    hbm = pl.BlockSpec(memory_space=pltpu.HBM)
    return pl.pallas_call(
        body, name=name, in_specs=[hbm], out_specs=hbm,
        out_shape=jax.ShapeDtypeStruct((N_DEV,) + shard.shape, shard.dtype),
        scratch_shapes=[pltpu.SemaphoreType.DMA((N_DEV - 1,)), pltpu.SemaphoreType.DMA((N_DEV - 1,)),
                        pltpu.SemaphoreType.DMA],
    )(shard)


def _gather_pushes(shards, kinds):
    full = {"slot": lambda s: (N_DEV,) + s, "rows": lambda s: (N_DEV * s[0], s[1]),
            "cols": lambda s: (s[0], N_DEV * s[1])}
    return _Pushes(shards, ["whole"] * len(shards), kinds, [full[k](a.shape) for a, k in zip(shards, kinds)])


def _scatter_pushes(partials, kinds):
    part = {"slot": lambda s: s[1:], "rows": lambda s: (128, s[1]), "cols": lambda s: (s[0], 128),
            "whole": lambda s: s}
    return _Pushes(partials, kinds, ["slot"] * len(partials),
                   [(N_DEV,) + part[k](a.shape) for a, k in zip(partials, kinds)])


def _adamw(name, parts, w, m, v, rows):
    L, R, C = w.shape

    def body(p_ref, w_ref, m_ref, v_ref, g_ref, d_ref, nm_ref, nv_ref):
        g = p_ref[0, 0].astype(F32)
        for s in range(1, N_DEV):
            g = g + p_ref[s, 0].astype(F32)
        g_ref[0] = g
        nm = ADAM_B1 * m_ref[0] + (1.0 - ADAM_B1) * g
        nv = ADAM_B2 * v_ref[0] + (1.0 - ADAM_B2) * (g * g)
        nm_ref[0] = nm
        nv_ref[0] = nv
        m_hat = nm / (1.0 - ADAM_B1 ** ADAM_STEP)
        v_hat = nv / (1.0 - ADAM_B2 ** ADAM_STEP)
        d_ref[0] = -ADAM_LR * (m_hat / (jnp.sqrt(v_hat) + ADAM_EPS) + ADAM_WD * w_ref[0])

    blk = pl.BlockSpec((1, rows, C), lambda l, i: (l, i, 0))
    shp = jax.ShapeDtypeStruct((L, R, C), F32)
    return pl.pallas_call(
        body, name=name, grid=(L, R // rows),
        in_specs=[pl.BlockSpec((N_DEV, 1, rows, C), lambda l, i: (0, l, i, 0)), blk, blk, blk],
        out_specs=[blk] * 4, out_shape=[shp] * 4,
        compiler_params=_cp("parallel", "parallel"),
    )(parts, w, m, v)


SMALL_ROWS = 40
LOSS_ROW = 37


def _pack_small(norm_g, ple_g, qk_g, b_f, last_row):
    rows = lambda a: a.astype(F32).reshape(-1, 128)
    flat = jnp.concatenate([rows(norm_g), rows(ple_g), rows(qk_g), _row(b_f.reshape(-1)), last_row], axis=0)
    return jnp.pad(flat, ((0, SMALL_ROWS - flat.shape[0]), (0, 0)))


def _unpack_small(flat):
    return (flat[0:16].reshape(2, D_MODEL), flat[16:32].reshape(2, D_MODEL), flat[32:36].reshape(2, 4, HEAD_DIM),
            flat[36, :2 * N_HEADS].reshape(2, N_HEADS))


def kernel(x, p, positions, norm_g, w_in, b_f, qk_norm_g, w_out, w_ple, ple_norm_g, w_ple_gate, loss_target, m_norm_g, m_w_in, m_b_f, m_qk_norm_g, m_w_out, m_w_ple, m_ple_norm_g, m_w_ple_gate, v_norm_g, v_w_in, v_b_f, v_qk_norm_g, v_w_out, v_w_ple, v_ple_norm_g, v_w_ple_gate):
    bf16 = lambda a: a.astype(BF16)
    rows_in = W_IN_ROWS // 2
    flat_in = lambda a: bf16(a).reshape(rows_in, 128)
    full_in = lambda g: g.reshape(N_DEV, D_MODEL, W_IN_SHARD).transpose(1, 0, 2).reshape(D_MODEL, N_IN)
    small = [(norm_g[l], b_f[l], qk_norm_g[l], ple_norm_g[l]) for l in range(2)]
    rope = _rope_tables(positions[0])
    tabs = _tables()

    g_in0 = _gather_two_level(flat_in(w_in[0]), "gather_first")
    rest = _gather_pushes([flat_in(w_in[1])] + [bf16(a[l]) for l in range(2) for a in (w_out, w_ple, w_ple_gate)],
                          ["slot"] + ["rows", "cols", "rows"] * 2)
    late = lambda got: dict(w_out=got[1], w_ple=got[2], w_pg=got[3])
    h, sv0, got = _layer_fwd(x[0], (p, (0, 0)), rope, tabs, _in_operands(full_in(g_in0)), *small[0], rest, late)
    w1 = dict(_in_operands(full_in(got[0])), w_out=got[4], w_ple=got[5], w_pg=got[6])
    dh, sv1, _, loss = _layer_fwd(h, (p, (1, 0)), rope, tabs, w1, *small[1], target=loss_target[0])
    dh, gr1, _ = _layer_bwd(dh, sv1, rope, tabs)

    by_dest = lambda d: d.reshape(D_MODEL, N_DEV, W_IN_SHARD).transpose(1, 0, 2).reshape(N_DEV, rows_in, 128)
    big = ("w_out", "w_ple", "w_ple_gate")
    riding = lambda early: _scatter_pushes([by_dest(gr1["w_in"])] + [gr1[n] for n in big] + [early[n] for n in big],
                                           ["slot"] + ["rows", "cols", "rows"] * 2)
    riding_last = lambda dw_in: _scatter_pushes([by_dest(dw_in)], ["slot"])
    dx, gr0, (r_in1, *r_big, r_in0) = _layer_bwd(dh, sv0, rope, tabs, riding, riding_last)
    grads = (gr0, gr1)
    stack = lambda name: jnp.stack([gl[name] for gl in grads], axis=0)
    small_part = _pack_small(stack("norm_g"), stack("ple_norm_g"), stack("qk_norm_g"), stack("b_f"),
                             _row(loss[0, 0].reshape(1)))
    (r_small,) = _exchange("exchange_small", _scatter_pushes([small_part], ["whole"]))
    r_in = jnp.concatenate([r_in0, r_in1], axis=1)
    r_out, r_ple, r_pg = (jnp.stack([r_big[3 + k], r_big[k]], axis=1) for k in range(3))

    zero_row = jnp.zeros((1, 128), F32)
    small_of = lambda ng, pg, qk, bf: _pack_small(ng, pg, qk, bf, zero_row)[None]
    flat = lambda a: a.reshape(1, W_IN_ROWS, 128)
    outs = dict(
        w_in=[o.reshape(w_in.shape) for o in
              _adamw("adamw_in", r_in[:, None], flat(w_in), flat(m_w_in), flat(v_w_in), W_IN_TILE)],
        w_out=_adamw("adamw_out", r_out, w_out, m_w_out, v_w_out, 128),
        w_ple=_adamw("adamw_ple", r_ple, w_ple, m_w_ple, v_w_ple, 256),
        w_pg=_adamw("adamw_gate", r_pg, w_ple_gate, m_w_ple_gate, v_w_ple_gate, 128),
        small=_adamw("adamw_small", r_small[:, None], small_of(norm_g, ple_norm_g, qk_norm_g, b_f),
                     small_of(m_norm_g, m_ple_norm_g, m_qk_norm_g, m_b_f),
                     small_of(v_norm_g, v_ple_norm_g, v_qk_norm_g, v_b_f), SMALL_ROWS))
    leaves = []
    for kind in range(4):
        ng, pg, qk, bf = _unpack_small(outs["small"][kind][0])
        leaves += [ng, outs["w_in"][kind], bf, qk, outs["w_out"][kind], outs["w_ple"][kind], pg, outs["w_pg"][kind]]
    return (outs["small"][0][0, LOSS_ROW, 0], dx[None], *leaves)
```

```python
import functools

import jax
import jax.numpy as jnp
from jax import lax
from jax.experimental import pallas as pl
from jax.experimental.pallas import tpu as pltpu

F32 = jnp.float32
BF16 = jnp.bfloat16

D_MODEL = 1024
HEAD_DIM = 64
N_HEADS = 8
HEAD_PAD = 128
D_BRANCH = N_HEADS * HEAD_DIM
N_MAIN = 8 * D_BRANCH
N_IN = N_MAIN + N_HEADS
PLE_DIM = 256
ROPE_THETA = 500000.0
ROPE_HALF = 8
EPS = 1e-6
NEG = -1e30
SCALE = HEAD_DIM ** -0.5
LOG2E = 1.4426950408889634
LN2 = 0.6931471805599453
DILATED_PATTERNS = ((128, 1), (512, 4), (2048, 16))
N_DEV = 8
W_IN_SHARD = N_IN // N_DEV
W_IN_ROWS = 2 * D_MODEL * W_IN_SHARD // 128
W_IN_TILE = W_IN_ROWS // 19

ADAM_LR = 0.001
ADAM_B1 = 0.9
ADAM_B2 = 0.999
ADAM_EPS = 1e-08
ADAM_WD = 0.01
ADAM_STEP = 10

ATT_T = 512
ATT_FWD_HEADS = 4
ATT_FWD_HEADS_CAUSAL = 8
ATT_BWD_HEADS = 2
ATT_CHUNK = 32
TOK_T = 256
VMEM_LIMIT = 56 * 1024 * 1024


def _slab_spec(lead, block, index):
    return pl.BlockSpec((None,) * len(lead) + block, lambda *g: (*lead, *index(*g)))


def _cp(*sem):
    return pltpu.CompilerParams(dimension_semantics=sem, vmem_limit_bytes=VMEM_LIMIT)


def _sigmoid(x):
    return 1.0 / (1.0 + jnp.exp(-x))


def _split3(x):
    hi = x.astype(BF16)
    r1 = x - hi.astype(F32)
    mid = r1.astype(BF16)
    lo = (r1 - mid.astype(F32)).astype(BF16)
    return hi, mid, lo


def _dot(a, b):
    return jnp.dot(a, b, preferred_element_type=F32)


def _dot_nt(a, b):
    return lax.dot_general(a, b, (((1,), (1,)), ((), ())), preferred_element_type=F32)


def _dot_tn(a, b):
    return lax.dot_general(a, b, (((0,), (0,)), ((), ())), preferred_element_type=F32)


def _inproj_fwd(h, g, wm, wf):
    S = h.shape[0]

    def body(h_ref, g_ref, wm_ref, wf_ref, zm_ref, zf_ref, u_ref):
        x = h_ref[...]
        r = lax.rsqrt(jnp.mean(x * x, axis=-1, keepdims=True) + EPS)
        u = (x * r * g_ref[...]).astype(BF16)
        u_ref[...] = u
        zm_ref[...] = _dot(u, wm_ref[...]).astype(BF16)
        zf_ref[...] = _dot(u, wf_ref[...])

    return pl.pallas_call(
        body, name="inproj_fwd", grid=(S // TOK_T,),
        in_specs=[pl.BlockSpec((TOK_T, D_MODEL), lambda i: (i, 0)),
                  pl.BlockSpec((1, D_MODEL), lambda i: (0, 0)),
                  pl.BlockSpec((D_MODEL, N_MAIN), lambda i: (0, 0)),
                  pl.BlockSpec((D_MODEL, 128), lambda i: (0, 0))],
        out_specs=[pl.BlockSpec((TOK_T, N_MAIN), lambda i: (i, 0)),
                   pl.BlockSpec((TOK_T, 128), lambda i: (i, 0)),
                   pl.BlockSpec((TOK_T, D_MODEL), lambda i: (i, 0))],
        out_shape=[jax.ShapeDtypeStruct((S, N_MAIN), BF16), jax.ShapeDtypeStruct((S, 128), F32),
                   jax.ShapeDtypeStruct((S, D_MODEL), BF16)],
        compiler_params=_cp("parallel"),
    )(h, g, wm, wf)


def _log_sigmoid(x):
    return jnp.minimum(x, 0.0) - jnp.log(1.0 + jnp.exp(-jnp.abs(x)))


def _same_head():
    r = lax.broadcasted_iota(jnp.int32, (HEAD_PAD, HEAD_PAD), 0) // HEAD_DIM
    c = lax.broadcasted_iota(jnp.int32, (HEAD_PAD, HEAD_PAD), 1) // HEAD_DIM
    return (r == c).astype(BF16)


def _pair_mean(x, same_head):
    hi = x.astype(BF16)
    lo = (x - hi.astype(F32)).astype(BF16)
    return (_dot(hi, same_head) + _dot(lo, same_head)) * (1.0 / HEAD_DIM)


def _pair_rsqrt(x, same_head):
    return lax.rsqrt(_pair_mean(x * x, same_head) + EPS)


def _prep_fwd(zm, zf, bf, tril, qkg, rope_c, rope_a, rope_b):
    S = zm.shape[0]
    shp = jax.ShapeDtypeStruct((N_HEADS, S, HEAD_PAD), BF16)

    def body(z_ref, zf_ref, b_ref, tri_ref, g_ref, rc_ref, ra_ref, rb_ref,
             qa_ref, ka_ref, va_ref, qb_ref, kb_ref, vb_ref, carry):
        @pl.when(pl.program_id(0) == 0)
        def _():
            carry[...] = jnp.zeros_like(carry)

        tri = tri_ref[...]
        cs = sum(_dot(tri, part) for part in _split3(_log_sigmoid(zf_ref[...] + b_ref[...]))) + carry[...]
        carry[...] = cs[TOK_T - 1:TOK_T, :]
        lane = lax.broadcasted_iota(jnp.int32, (TOK_T, HEAD_PAD), 1)
        lo_half = lane < HEAD_DIM
        aug = (lane >= HEAD_DIM) & (lane < HEAD_DIM + 3)
        q_pad = jnp.where(aug, -1.0, 0.0)
        rc, ra, rb = rc_ref[...], ra_ref[...], rb_ref[...]
        same_head = _same_head()

        def norm(col, gi):
            x = z_ref[:, col:col + HEAD_PAD].astype(F32)
            return x * _pair_rsqrt(x, same_head) * g_ref[gi:gi + 1, :]

        def rope(y):
            return y * rc + pltpu.roll(y, HEAD_PAD - ROPE_HALF, 1) * ra + pltpu.roll(y, ROPE_HALF, 1) * rb

        def put(ref, pi, y, pad_even, pad_odd):
            ref[2 * pi] = jnp.where(lo_half, y, pad_even).astype(BF16)
            ref[2 * pi + 1] = jnp.where(lo_half, pltpu.roll(y, HEAD_DIM, 1), pad_odd).astype(BF16)

        def k_pad(h):
            ch = cs[:, h:h + 1] * LOG2E
            hi = ch.astype(BF16).astype(F32)
            mid = (ch - hi).astype(BF16).astype(F32)
            lo = ch - hi - mid
            ones = jnp.where(lane == HEAD_DIM + 3, 1.0, 0.0)
            return jnp.where(lane == HEAD_DIM, hi, jnp.where(lane == HEAD_DIM + 1, mid,
                                                              jnp.where(lane == HEAD_DIM + 2, lo, ones)))

        for pi in range(N_HEADS // 2):
            col = HEAD_PAD * pi
            put(qa_ref, pi, norm(col, 0) * (SCALE * LOG2E), q_pad, q_pad)
            put(ka_ref, pi, norm(D_BRANCH + col, 1), k_pad(2 * pi), k_pad(2 * pi + 1))
            put(va_ref, pi, z_ref[:, 2 * D_BRANCH + col:2 * D_BRANCH + col + HEAD_PAD].astype(F32), 0.0, 0.0)
            put(qb_ref, pi, rope(norm(4 * D_BRANCH + col, 2)) * (SCALE * LOG2E), 0.0, 0.0)
            put(kb_ref, pi, rope(norm(5 * D_BRANCH + col, 3)), 0.0, 0.0)
            put(vb_ref, pi, z_ref[:, 6 * D_BRANCH + col:6 * D_BRANCH + col + HEAD_PAD].astype(F32), 0.0, 0.0)

    tok = lambda w: pl.BlockSpec((TOK_T, w), lambda i: (i, 0))
    head = pl.BlockSpec((N_HEADS, TOK_T, HEAD_PAD), lambda i: (0, i, 0))
    return pl.pallas_call(
        body, name="prep_fwd", grid=(S // TOK_T,),
        in_specs=[tok(N_MAIN), tok(128), pl.BlockSpec((1, 128), lambda i: (0, 0)),
                  pl.BlockSpec((TOK_T, TOK_T), lambda i: (0, 0)), pl.BlockSpec((4, 128), lambda i: (0, 0)),
                  tok(128), tok(128), tok(128)],
        out_specs=[head] * 6, out_shape=[shp] * 6,
        scratch_shapes=[pltpu.VMEM((1, 128), F32)],
        compiler_params=_cp("arbitrary"),
    )(zm, zf, bf, tril, qkg, rope_c, rope_a, rope_b)


def _pair(ref, pi, lo_half):
    return jnp.where(lo_half, ref[2 * pi].astype(F32), pltpu.roll(ref[2 * pi + 1].astype(F32), HEAD_DIM, 1))


def _mid_fwd(oa, ob, zm, h0, p, w_out, w_pg, w_ple, g2, target=None):
    S = h0.shape[0]
    p, p_lead = p

    def body(oa_ref, ob_ref, ga_ref, gb_ref, h0_ref, p_ref, wo_ref, wg_ref, wp_ref, g2_ref, *rest):
        t_ref, rest = (rest[0], rest[1:]) if target is not None else (None, rest)
        y_ref, h1_ref, h2_ref, u2_ref, e_ref, gate_ref, *loss_ref = rest
        parts = []
        for o_ref, g_ref in ((oa_ref, ga_ref), (ob_ref, gb_ref)):
            for pi in range(N_HEADS // 2):
                g = g_ref[:, HEAD_PAD * pi:HEAD_PAD * (pi + 1)].astype(F32)
                parts.append((o_ref[pi] * (g * _sigmoid(g))).astype(BF16))
        y = jnp.concatenate(parts, axis=1)
        y_ref[...] = y
        h1 = h0_ref[...] + _dot(y, wo_ref[...])
        h1_ref[...] = h1
        r = lax.rsqrt(jnp.mean(h1 * h1, axis=-1, keepdims=True) + EPS)
        u2 = (h1 * r * g2_ref[...]).astype(BF16)
        u2_ref[...] = u2
        gate = _sigmoid(_dot(u2, wg_ref[...]))
        e = _dot(p_ref[...].astype(BF16), wp_ref[...])
        e_ref[...] = e.astype(BF16)
        gate_ref[...] = gate.astype(BF16)
        h2 = h1 + e * gate
        if target is None:
            h2_ref[...] = h2
        else:
            @pl.when(pl.program_id(0) == 0)
            def _():
                loss_ref[0][...] = jnp.zeros_like(loss_ref[0])

            err = h2 - t_ref[...]
            h2_ref[...] = err * (1.0 / D_MODEL)
            part = jnp.sum(jnp.sum(err * err, axis=1, keepdims=True), axis=0, keepdims=True)
            loss_ref[0][...] += part * (0.5 / D_MODEL)

    tok = lambda w: pl.BlockSpec((TOK_T, w), lambda i: (i, 0))
    head = pl.BlockSpec((N_HEADS // 2, TOK_T, HEAD_PAD), lambda i: (0, i, 0))
    full = lambda a, b: pl.BlockSpec((a, b), lambda i: (0, 0))
    act = lambda dt: jax.ShapeDtypeStruct((S, D_MODEL), dt)
    fused = target is not None
    return pl.pallas_call(
        body, name="mid_fwd_loss" if fused else "mid_fwd", grid=(S // TOK_T,),
        in_specs=[head, head,
                  pl.BlockSpec((TOK_T, D_BRANCH), lambda i: (i, 3)), pl.BlockSpec((TOK_T, D_BRANCH), lambda i: (i, 7)),
                  tok(D_MODEL), _slab_spec(p_lead, (TOK_T, PLE_DIM), lambda i: (i, 0)), full(D_MODEL, D_MODEL),
                  full(D_MODEL, D_MODEL), full(PLE_DIM, D_MODEL), full(1, D_MODEL)] + [tok(D_MODEL)] * fused,
        out_specs=[tok(D_MODEL)] * 6 + [full(8, 128)] * fused,
        out_shape=[act(BF16), act(F32), act(F32), act(BF16), act(BF16), act(BF16)]
        + [jax.ShapeDtypeStruct((8, 128), F32)] * fused,
        compiler_params=_cp("arbitrary" if fused else "parallel"),
    )(oa, ob, zm, zm, h0, p, w_out, w_pg, w_ple, g2, *([target] * fused))


def _bias_tables(full_range):
    T = ATT_T
    nb = 1 if full_range else DILATED_PATTERNS[-1][0] // T + 1
    r = lax.broadcasted_iota(jnp.int32, (nb, T, T), 2)
    c = lax.broadcasted_iota(jnp.int32, (nb, T, T), 1)
    b = lax.broadcasted_iota(jnp.int32, (nb, T, T), 0)
    delta = T * b + r - c
    if full_range:
        bias = jnp.where(delta >= 0, 0.0, NEG).astype(F32)
    else:
        mult = jnp.zeros((nb, T, T), F32)
        for window, dil in DILATED_PATTERNS:
            ok = (delta >= 0) & (delta <= window) & (delta % dil == 0)
            mult = mult + ok.astype(F32)
        bias = jnp.where(mult > 0, jnp.log2(jnp.maximum(mult, 1.0)), NEG).astype(F32)
    return bias


def _call_with_rider(body, name, grid, rider, in_specs, out_specs, out_shape, scratch_shapes, operands,
                     semantics=("parallel", "arbitrary")):
    if rider is None:
        return pl.pallas_call(body, name=name, grid=grid, in_specs=in_specs, out_specs=out_specs,
                              out_shape=out_shape, scratch_shapes=scratch_shapes,
                              compiler_params=_cp(*semantics))(*operands)
    n, n_in, n_out = rider.n, len(in_specs), len(out_specs)

    def wrapped(*refs):
        ins, r_ins = refs[:n_in], refs[n_in:n_in + n]
        outs, r_outs = refs[n_in + n:n_in + n + n_out], refs[n_in + n + n_out:n_in + 2 * n + n_out]
        scratch, sems = refs[n_in + 2 * n + n_out:-3], refs[-3:]
        step = [pl.program_id(a) for a in range(len(grid))]

        @pl.when(functools.reduce(jnp.logical_and, [s == 0 for s in step]))
        def _():
            rider.start(r_ins, r_outs, sems)

        body(*ins, *outs, *scratch)

        @pl.when(functools.reduce(jnp.logical_and, [s == g - 1 for s, g in zip(step, grid)]))
        def _():
            rider.wait(r_ins, r_outs, sems)

    return pl.pallas_call(
        wrapped, name=name, grid=grid, in_specs=list(in_specs) + rider.in_specs,
        out_specs=list(out_specs) + rider.out_specs, out_shape=list(out_shape) + rider.out_shapes,
        scratch_shapes=list(scratch_shapes) + rider.scratch_shapes,
        compiler_params=_cp(*["arbitrary"] * len(grid)))(*operands, *rider.arrays)


def _attn_fwd(q, k, v, table_t, full_range, name, rider=None):
    H, S, _ = q.shape
    T = ATT_T
    nb = table_t.shape[0]
    HB = ATT_FWD_HEADS_CAUSAL if full_range else ATT_FWD_HEADS
    KC = ATT_CHUNK
    chunks = [slice(c, c + KC) for c in range(0, T, KC)]
    fold = lambda x, op: functools.reduce(op, [x[r:r + 8] for r in range(0, KC, 8)])

    def body(q_ref, k_ref, v_ref, tab_ref, o_ref, lse_ref, *scratch):
        st_refs, pt_refs, acc_refs = scratch[:HB], scratch[HB:2 * HB], scratch[2 * HB:]
        i = pl.program_id(1)
        rows = lambda j: pl.ds(pl.multiple_of(j * T, T), T)

        def scores(hh, j):
            st_refs[hh][...] = _dot_nt(k_ref[hh, rows(j), :], q_ref[hh])

        def block(j, b, nxt, stats):
            out = []
            for hh, (m, l) in enumerate(stats):
                st_ref, pt_ref, acc_ref = st_refs[hh], pt_refs[hh], acc_refs[hh]
                mx = None
                for ch in chunks:
                    x = st_ref[ch, :]
                    if b is not None:
                        x = x + tab_ref[b, ch, :]
                        st_ref[ch, :] = x
                    x = fold(x, jnp.maximum)
                    mx = x if mx is None else jnp.maximum(mx, x)
                m_new = jnp.maximum(m, jnp.max(mx, axis=0, keepdims=True))
                alpha = jnp.exp2(m - m_new)
                ls = None
                for ch in chunks:
                    pc = jnp.exp2(st_ref[ch, :] - m_new)
                    pt_ref[ch, :] = pc.astype(BF16)
                    pc = fold(pc, jnp.add)
                    ls = pc if ls is None else ls + pc
                if nxt is not None:
                    scores(hh, nxt)
                acc_ref[...] = alpha * acc_ref[...] + _dot_tn(v_ref[hh, rows(j), :], pt_ref[...])
                out.append((m_new, alpha * l + jnp.sum(ls, axis=0, keepdims=True)))
            return tuple(out)

        lo = 0 if full_range else jnp.maximum(i - (nb - 1), 0)
        for hh in range(HB):
            acc_refs[hh][...] = jnp.zeros_like(acc_refs[hh])
            scores(hh, lo)
        stats = lax.fori_loop(lo, i, lambda j, st: block(j, None if full_range else i - j, j + 1, st),
                              ((jnp.full((1, T), NEG, F32), jnp.zeros((1, T), F32)),) * HB)
        stats = block(i, 0, None, stats)
        o_t = [acc_refs[hh][...] * (1.0 / l) for hh, (m, l) in enumerate(stats)]
        for hh, (m, l) in enumerate(stats):
            lse_ref[hh, 0] = m + jnp.log2(l)
        for hp in range(HB // 2):
            o_ref[hp] = jnp.concatenate([o_t[2 * hp][:HEAD_DIM], o_t[2 * hp + 1][:HEAD_DIM]], axis=0).T

    return _call_with_rider(
        body, name, (H // HB, S // T), rider,
        in_specs=[pl.BlockSpec((HB, T, HEAD_PAD), lambda h, i: (h, i, 0)),
                  pl.BlockSpec((HB, S, HEAD_PAD), lambda h, i: (h, 0, 0), pipeline_mode=pl.Buffered(1)),
                  pl.BlockSpec((HB, S, HEAD_PAD), lambda h, i: (h, 0, 0), pipeline_mode=pl.Buffered(1)),
                  pl.BlockSpec((nb, T, T), lambda h, i: (0, 0, 0), pipeline_mode=pl.Buffered(1))],
        out_specs=[pl.BlockSpec((HB // 2, T, HEAD_PAD), lambda h, i: (h, i, 0)),
                   pl.BlockSpec((HB, 1, 1, T), lambda h, i: (h, i, 0, 0))],
        out_shape=[jax.ShapeDtypeStruct((H // 2, S, HEAD_PAD), F32), jax.ShapeDtypeStruct((H, S // T, 1, T), F32)],
        scratch_shapes=([pltpu.VMEM((T, T), F32)] * HB + [pltpu.VMEM((T, T), BF16)] * HB
                        + [pltpu.VMEM((HEAD_PAD, T), F32)] * HB),
        operands=(q, k, v, table_t))


def _attn_bwd(q, k, v, do, lse, dd, table_t, full_range, name, rider=None, narrow=(False, False, False)):
    H, S, _ = q.shape
    T = ATT_T
    nq = S // T
    nb = table_t.shape[0]
    HB = ATT_BWD_HEADS
    KC = ATT_CHUNK
    chunks = [slice(c, c + KC) for c in range(0, T, KC)]

    def body(q_ref, do_ref, lse_ref, dd_ref, k_ref, v_ref, tab_ref, dq_hbm, dk_ref, dv_ref, *scratch):
        st_refs, dpt_refs, pt_refs, dst_refs = (scratch[n * HB:(n + 1) * HB] for n in range(4))
        dq_ref, dk_acc, dv_acc, *dq_cast, dq_sem = scratch[4 * HB:]
        h = pl.program_id(0)
        j = pl.program_id(1)

        @pl.when(j == 0)
        def _():
            dq_ref[...] = jnp.zeros_like(dq_ref)

        dk_acc[...] = jnp.zeros_like(dk_acc)
        dv_acc[...] = jnp.zeros_like(dv_acc)

        def step(i, b):
            rows = pl.ds(pl.multiple_of(i * T, T), T)
            for hh in range(HB):
                st_refs[hh][...] = _dot_nt(k_ref[hh], q_ref[hh, rows, :])
                dpt_refs[hh][...] = _dot_nt(v_ref[hh], do_ref[hh, rows, :])
            for hh in range(HB):
                lse_i = lse_ref[hh, i]
                dd_i = dd_ref[hh, i]
                for ch in chunks:
                    x = st_refs[hh][ch, :]
                    if b is not None:
                        x = x + tab_ref[b, ch, :]
                    pc = jnp.exp2(x - lse_i)
                    pt_refs[hh][ch, :] = pc.astype(BF16)
                    dst_refs[hh][ch, :] = (pc * (dpt_refs[hh][ch, :] - dd_i)).astype(BF16)
                dv_acc[hh] += _dot(pt_refs[hh][...], do_ref[hh, rows, :])
                dk_acc[hh] += _dot(dst_refs[hh][...], q_ref[hh, rows, :])
                dq_ref[hh, rows, :] += _dot_tn(dst_refs[hh][...], k_ref[hh])

        step(j, 0)
        if full_range:
            pl.loop(j + 1, nq)(lambda i: step(i, None))
        else:
            pl.loop(j + 1, jnp.minimum(j + nb, nq))(lambda i: step(i, i - j))
        dk_ref[...] = dk_acc[...].astype(dk_ref.dtype)
        dv_ref[...] = dv_acc[...].astype(dv_ref.dtype)

        @pl.when(j == nq - 1)
        def _():
            src = dq_ref
            if narrow[0]:
                src, = dq_cast
                src[...] = dq_ref[...].astype(BF16)
            out = pltpu.make_async_copy(src, dq_hbm.at[pl.ds(h * HB, HB)], dq_sem)
            out.start()
            out.wait()

    once = dict(pipeline_mode=pl.Buffered(1))
    per_head = pl.BlockSpec((HB, S, HEAD_PAD), lambda h, j: (h, 0, 0), **once)
    rows = pl.BlockSpec((HB, nq, 1, T), lambda h, j: (h, 0, 0, 0))
    blk = pl.BlockSpec((HB, T, HEAD_PAD), lambda h, j: (h, j, 0))
    shp = [jax.ShapeDtypeStruct((H, S, HEAD_PAD), BF16 if nar else F32) for nar in narrow]
    acc = pltpu.VMEM((HB, T, HEAD_PAD), F32)
    return _call_with_rider(
        body, name, (H // HB, nq), rider,
        in_specs=[per_head, per_head, rows, rows, blk, blk,
                  pl.BlockSpec((nb, T, T), lambda h, j: (0, 0, 0), **once)],
        out_specs=[pl.BlockSpec(memory_space=pltpu.HBM), blk, blk], out_shape=shp,
        scratch_shapes=([pltpu.VMEM((T, T), F32)] * (2 * HB) + [pltpu.VMEM((T, T), BF16)] * (2 * HB)
                        + [pltpu.VMEM((HB, S, HEAD_PAD), F32), acc, acc]
                        + [pltpu.VMEM((HB, S, HEAD_PAD), BF16)] * narrow[0] + [pltpu.SemaphoreType.DMA]),
        operands=(q, do, lse, dd, k, v, table_t))


def _mid_bwd(dh2, h1, e, gate, g2, w_pg, w_out, oa, ob, zm):
    S = dh2.shape[0]

    def body(dh2_ref, h1_ref, e_ref, gate_ref, g2_ref, wg_ref, wo_ref, oa_ref, ob_ref, ga_ref, gb_ref,
             dh1_ref, dh1b_ref, de_ref, dpre_ref, doa_ref, dob_ref, dga_ref, dgb_ref, dd_ref, dg2_ref):
        @pl.when(pl.program_id(0) == 0)
        def _():
            dg2_ref[...] = jnp.zeros_like(dg2_ref)

        lane = lax.broadcasted_iota(jnp.int32, (TOK_T, HEAD_PAD), 1)
        lo_half = lane < HEAD_DIM
        dh2 = dh2_ref[...]
        gate = gate_ref[...]
        de_ref[...] = (dh2 * gate).astype(BF16)
        dpre = (dh2 * e_ref[...] * gate * (1.0 - gate)).astype(BF16)
        dpre_ref[...] = dpre
        du2 = _dot_nt(dpre, wg_ref[...])
        h1 = h1_ref[...]
        r = lax.rsqrt(jnp.mean(h1 * h1, axis=-1, keepdims=True) + EPS)
        xh = h1 * r
        a = du2 * g2_ref[...]
        dh1 = dh2 + r * (a - xh * jnp.mean(a * xh, axis=-1, keepdims=True))
        dg2_ref[...] += jnp.sum(du2 * xh, axis=0, keepdims=True)
        dh1_ref[...] = dh1
        dh1b = dh1.astype(BF16)
        dh1b_ref[...] = dh1b
        dy = _dot_nt(dh1b, wo_ref[...])
        dd = jnp.zeros((TOK_T, HEAD_PAD), F32)
        for bi, (o_ref, g_ref, do_ref, dg_ref) in enumerate(
                ((oa_ref, ga_ref, doa_ref, dga_ref), (ob_ref, gb_ref, dob_ref, dgb_ref))):
            for pi in range(N_HEADS // 2):
                col = bi * D_BRANCH + HEAD_PAD * pi
                dyp = dy[:, col:col + HEAD_PAD]
                g = g_ref[:, HEAD_PAD * pi:HEAD_PAD * (pi + 1)].astype(F32)
                sg = _sigmoid(g)
                o_pair = o_ref[pi]
                dg_ref[:, HEAD_PAD * pi:HEAD_PAD * (pi + 1)] = (
                    dyp * o_pair * (sg * (1.0 + g * (1.0 - sg)))).astype(BF16)
                dop = dyp * (g * sg)
                prod = dop * o_pair
                for hh, d_head, mine in ((2 * pi, dop, lo_half),
                                         (2 * pi + 1, pltpu.roll(dop, HEAD_DIM, 1), ~lo_half)):
                    do_ref[hh] = jnp.where(lo_half, d_head, 0.0).astype(BF16)
                    dsum = jnp.sum(jnp.where(mine, prod, 0.0), axis=1, keepdims=True)
                    dd = dd + jnp.where(lane == bi * N_HEADS + hh, dsum, 0.0)
        dd_ref[...] = dd.T[:2 * N_HEADS, :]

    tok = lambda w: pl.BlockSpec((TOK_T, w), lambda i: (i, 0))
    head = pl.BlockSpec((N_HEADS, TOK_T, HEAD_PAD), lambda i: (0, i, 0))
    pairs = pl.BlockSpec((N_HEADS // 2, TOK_T, HEAD_PAD), lambda i: (0, i, 0))
    full = lambda a, b: pl.BlockSpec((a, b), lambda i: (0, 0))
    act = lambda w, dt: jax.ShapeDtypeStruct((S, w), dt)
    hshape = lambda w, dt: jax.ShapeDtypeStruct((N_HEADS, S, w), dt)
    return pl.pallas_call(
        body, name="mid_bwd", grid=(S // TOK_T,),
        in_specs=[tok(D_MODEL)] * 4 + [full(1, D_MODEL), full(D_MODEL, D_MODEL), full(D_MODEL, D_MODEL), pairs, pairs,
                                      pl.BlockSpec((TOK_T, D_BRANCH), lambda i: (i, 3)),
                                      pl.BlockSpec((TOK_T, D_BRANCH), lambda i: (i, 7))],
        out_specs=[tok(D_MODEL)] * 4 + [head, head, tok(D_BRANCH), tok(D_BRANCH),
                                       pl.BlockSpec((2 * N_HEADS, TOK_T), lambda i: (0, i)), full(1, D_MODEL)],
        out_shape=[act(D_MODEL, F32), act(D_MODEL, BF16), act(D_MODEL, BF16), act(D_MODEL, BF16),
                   hshape(HEAD_PAD, BF16), hshape(HEAD_PAD, BF16), act(D_BRANCH, BF16), act(D_BRANCH, BF16),
                   jax.ShapeDtypeStruct((2 * N_HEADS, S), F32), jax.ShapeDtypeStruct((1, D_MODEL), F32)],
        compiler_params=_cp("arbitrary"),
    )(dh2, h1, e, gate, g2, w_pg, w_out, oa, ob, zm, zm)


def _prep_bwd(dqa, dka, dva, dqb, dkb, dvb, zm, qkg, rope_c, rope_a, rope_b, dga, dgb, zf, bf, triu):
    S = zm.shape[0]
    n = S // TOK_T

    def body(dqa_ref, dka_ref, dva_ref, dqb_ref, dkb_ref, dvb_ref, z_ref, g_ref, rc_ref, ra_ref, rb_ref,
             dga_ref, dgb_ref, zf_ref, b_ref, tri_ref, dz_ref, dzf_ref, dqkg_ref, db_ref, carry):
        @pl.when(pl.program_id(0) == 0)
        def _():
            dqkg_ref[...] = jnp.zeros_like(dqkg_ref)
            db_ref[...] = jnp.zeros_like(db_ref)
            carry[...] = jnp.zeros_like(carry)

        lane = lax.broadcasted_iota(jnp.int32, (TOK_T, HEAD_PAD), 1)
        lo_half = lane < HEAD_DIM
        rc, ra, rb = rc_ref[...], ra_ref[...], rb_ref[...]
        same_head = _same_head()

        def unrope(dy):
            return dy * rc + pltpu.roll(dy * ra, ROPE_HALF, 1) + pltpu.roll(dy * rb, HEAD_PAD - ROPE_HALF, 1)

        def norm_bwd(col, gi, dy):
            x = z_ref[:, col:col + HEAD_PAD].astype(F32)
            r = _pair_rsqrt(x, same_head)
            xh = x * r
            dqkg_ref[gi:gi + 1, :] += jnp.sum(dy * xh, axis=0, keepdims=True)
            a = dy * g_ref[gi:gi + 1, :]
            dz_ref[:, col:col + HEAD_PAD] = (r * (a - xh * _pair_mean(a * xh, same_head))).astype(BF16)

        dc = jnp.zeros((TOK_T, HEAD_PAD), F32)
        for pi in range(N_HEADS // 2):
            col = HEAD_PAD * pi
            norm_bwd(col, 0, _pair(dqa_ref, pi, lo_half) * SCALE)
            norm_bwd(D_BRANCH + col, 1, _pair(dka_ref, pi, lo_half) * LN2)
            dz_ref[:, 2 * D_BRANCH + col:2 * D_BRANCH + col + HEAD_PAD] = _pair(dva_ref, pi, lo_half).astype(BF16)
            norm_bwd(4 * D_BRANCH + col, 2, unrope(_pair(dqb_ref, pi, lo_half) * SCALE))
            norm_bwd(5 * D_BRANCH + col, 3, unrope(_pair(dkb_ref, pi, lo_half) * LN2))
            dz_ref[:, 6 * D_BRANCH + col:6 * D_BRANCH + col + HEAD_PAD] = _pair(dvb_ref, pi, lo_half).astype(BF16)
            for hh in (2 * pi, 2 * pi + 1):
                dch = dka_ref[hh][:, HEAD_DIM:HEAD_DIM + 1] + dqa_ref[hh][:, HEAD_DIM + 3:HEAD_DIM + 4]
                dc = dc + jnp.where(lane == hh, dch, 0.0)
        dz_ref[:, 3 * D_BRANCH:4 * D_BRANCH] = dga_ref[...]
        dz_ref[:, 7 * D_BRANCH:8 * D_BRANCH] = dgb_ref[...]
        tri = tri_ref[...]
        dlf = sum(_dot(tri, part) for part in _split3(dc)) + carry[...]
        carry[...] = dlf[0:1, :]
        dfa = dlf * (1.0 - _sigmoid(zf_ref[...] + b_ref[...]))
        dzf_ref[...] = dfa.astype(BF16)
        db_ref[...] += jnp.sum(dfa, axis=0, keepdims=True)

    tok = lambda w: pl.BlockSpec((TOK_T, w), lambda i: (n - 1 - i, 0))
    head = pl.BlockSpec((N_HEADS, TOK_T, HEAD_PAD), lambda i: (0, n - 1 - i, 0))
    fixed = lambda a, b: pl.BlockSpec((a, b), lambda i: (0, 0))
    return pl.pallas_call(
        body, name="prep_bwd", grid=(n,),
        in_specs=[head] * 6 + [tok(N_MAIN), fixed(4, 128), tok(128), tok(128), tok(128), tok(D_BRANCH), tok(D_BRANCH),
                               tok(128), fixed(1, 128), fixed(TOK_T, TOK_T)],
        out_specs=[tok(N_MAIN), tok(128), fixed(4, 128), fixed(1, 128)],
        out_shape=[jax.ShapeDtypeStruct((S, N_MAIN), BF16), jax.ShapeDtypeStruct((S, 128), BF16),
                   jax.ShapeDtypeStruct((4, 128), F32), jax.ShapeDtypeStruct((1, 128), F32)],
        scratch_shapes=[pltpu.VMEM((1, 128), F32)],
        compiler_params=_cp("arbitrary"),
    )(dqa, dka, dva, dqb, dkb, dvb, zm, qkg, rope_c, rope_a, rope_b, dga, dgb, zf, bf, triu)


def _inproj_bwd(dzm, dzf, wm, wf, h0, dh1, g, rider=None):
    S = h0.shape[0]

    def body(dzm_ref, dzf_ref, wm_ref, wf_ref, h_ref, dh1_ref, g_ref, dh0_ref, dg_ref):
        @pl.when(pl.program_id(0) == 0)
        def _():
            dg_ref[...] = jnp.zeros_like(dg_ref)

        du = _dot_nt(dzm_ref[...], wm_ref[...]) + _dot_nt(dzf_ref[...], wf_ref[...])
        x = h_ref[...]
        r = lax.rsqrt(jnp.mean(x * x, axis=-1, keepdims=True) + EPS)
        xh = x * r
        a = du * g_ref[...]
        dh0_ref[...] = dh1_ref[...] + r * (a - xh * jnp.mean(a * xh, axis=-1, keepdims=True))
        dg_ref[...] += jnp.sum(du * xh, axis=0, keepdims=True)

    tok = lambda w: pl.BlockSpec((TOK_T, w), lambda i: (i, 0))
    full = lambda a, b: pl.BlockSpec((a, b), lambda i: (0, 0))
    return _call_with_rider(
        body, "inproj_bwd", (S // TOK_T,), rider,
        in_specs=[tok(N_MAIN), tok(128), full(D_MODEL, N_MAIN), full(D_MODEL, 128), tok(D_MODEL), tok(D_MODEL),
                  full(1, D_MODEL)],
        out_specs=[tok(D_MODEL), full(1, D_MODEL)],
        out_shape=[jax.ShapeDtypeStruct((S, D_MODEL), F32), jax.ShapeDtypeStruct((1, D_MODEL), F32)],
        scratch_shapes=[], operands=(dzm, dzf, wm, wf, h0, dh1, g), semantics=("arbitrary",))


def _wgrad(a, b, name, a_lead=()):
    S, M = a.shape[len(a_lead):]
    N = b.shape[1]
    tn = min(N, 2048)
    ts = 512
    last = S // ts - 1

    def body(a_ref, b_ref, o_ref, acc_ref):
        @pl.when(pl.program_id(1) == 0)
        def _():
            acc_ref[...] = jnp.zeros_like(acc_ref)

        acc_ref[...] += _dot_tn(a_ref[...].astype(BF16), b_ref[...])

        @pl.when(pl.program_id(1) == last)
        def _():
            o_ref[...] = acc_ref[...].astype(BF16)

    return pl.pallas_call(
        body, name=name, grid=(N // tn, S // ts),
        in_specs=[_slab_spec(a_lead, (ts, M), lambda n, s: (s, 0)), pl.BlockSpec((ts, tn), lambda n, s: (s, n))],
        out_specs=pl.BlockSpec((M, tn), lambda n, s: (0, n)),
        out_shape=jax.ShapeDtypeStruct((M, N), BF16),
        scratch_shapes=[pltpu.VMEM((M, tn), F32)],
        compiler_params=_cp("parallel", "arbitrary"),
    )(a, b)


def _rope_tables(positions):
    inv_freq = ROPE_THETA ** (-jnp.arange(ROPE_HALF, dtype=F32) / ROPE_HALF)
    ang = positions.astype(F32)[:, None] * inv_freq
    cos, sin = jnp.cos(ang), jnp.sin(ang)
    S = positions.shape[0]
    one, zero = jnp.ones((S, HEAD_DIM - 2 * ROPE_HALF), F32), jnp.zeros((S, HEAD_DIM - 2 * ROPE_HALF), F32)
    z8 = jnp.zeros((S, ROPE_HALF), F32)
    rc = jnp.concatenate([cos, cos, one], axis=1)
    ra = jnp.concatenate([-sin, z8, zero], axis=1)
    rb = jnp.concatenate([z8, sin, zero], axis=1)
    return tuple(jnp.tile(t, (1, 2)) for t in (rc, ra, rb))


def _in_operands(w_in):
    w_in = w_in.astype(BF16)
    wm = jnp.concatenate([w_in[:, :4 * D_BRANCH], w_in[:, 4 * D_BRANCH + N_HEADS:]], axis=1)
    wf = jnp.pad(w_in[:, 4 * D_BRANCH:4 * D_BRANCH + N_HEADS], ((0, 0), (0, 128 - N_HEADS)))
    return dict(wm=wm, wf=wf)


def _layer_weights(w_in, w_out, w_ple, w_pg):
    return dict(_in_operands(w_in), w_out=w_out.astype(BF16), w_ple=w_ple.astype(BF16), w_pg=w_pg.astype(BF16))


def _row(v, width=128):
    v = v.reshape(1, -1).astype(F32)
    return jnp.pad(v, ((0, 0), (0, width - v.shape[1])))


def _layer_fwd(h0, p, rope, tabs, w, norm_g, b_f, qk_g, ple_g, rider=None, late=None, target=None):
    g1 = norm_g.reshape(1, D_MODEL)
    g2 = ple_g.reshape(1, D_MODEL)
    qkg = jnp.tile(qk_g, (1, 2))
    bf = _row(b_f)
    zm, zf, u = _inproj_fwd(h0, g1, w["wm"], w["wf"])
    qa, ka, va, qb, kb, vb = _prep_fwd(zm, zf, bf, tabs["tril"], qkg, *rope)
    oa, lse_a, *arrivals = _attn_fwd(qa, ka, va, tabs["fox"], True, "fox_fwd", rider)
    if late is not None:
        w = {**w, **late(arrivals)}
    ob, lse_b = _attn_fwd(qb, kb, vb, tabs["dil"], False, "dil_fwd")
    y, h1, h2, u2, e, gate, *loss = _mid_fwd(oa, ob, zm, h0, p, w["w_out"], w["w_pg"], w["w_ple"], g2, target)
    saved = dict(h0=h0, p=p, zm=zm, zf=zf, u=u, qa=qa, ka=ka, va=va, qb=qb, kb=kb, vb=vb, oa=oa, ob=ob,
                 lse_a=lse_a, lse_b=lse_b, y=y, h1=h1, u2=u2, e=e, gate=gate, g1=g1, g2=g2, qkg=qkg, bf=bf, w=w)
    return (h2, saved, arrivals, *loss)


def _layer_bwd(dh2, sv, rope, tabs, make_rider=None, make_last_rider=None):
    S = dh2.shape[0]
    nq = S // ATT_T
    w = sv["w"]
    rows = lambda a: a.reshape(N_HEADS, nq, 1, ATT_T)
    (dh1, dh1b, de, dpre, doa, dob, dga, dgb, dd, dg2) = _mid_bwd(
        dh2, sv["h1"], sv["e"], sv["gate"], sv["g2"], w["w_pg"], w["w_out"], sv["oa"], sv["ob"], sv["zm"])
    dda, ddb = dd[:N_HEADS], dd[N_HEADS:]
    early = dict(w_out=_wgrad(sv["y"], dh1b, "wgrad_out"), w_ple=_wgrad(sv["p"][0], de, "wgrad_ple", sv["p"][1]),
                 w_ple_gate=_wgrad(sv["u2"], dpre, "wgrad_gate"))
    rider = None if make_rider is None else make_rider(early)
    dqa, dka, dva, *arrivals = _attn_bwd(sv["qa"], sv["ka"], sv["va"], doa, sv["lse_a"], rows(dda), tabs["fox"],
                                         True, "fox_bwd", rider, narrow=(False, False, True))
    dqb, dkb, dvb = _attn_bwd(sv["qb"], sv["kb"], sv["vb"], dob, sv["lse_b"], rows(ddb), tabs["dil"], False,
                              "dil_bwd", narrow=(True, True, True))
    dzm, dzf, dqkg, dbf = _prep_bwd(dqa, dka, dva, dqb, dkb, dvb, sv["zm"], sv["qkg"], *rope, dga, dgb,
                                    sv["zf"], sv["bf"], tabs["triu"])
    dwm = _wgrad(sv["u"], dzm, "wgrad_in")
    dwf = _wgrad(sv["u"], dzf, "wgrad_f")
    dw_in = jnp.concatenate([dwm[:, :4 * D_BRANCH], dwf[:, :N_HEADS], dwm[:, 4 * D_BRANCH:]], axis=1)
    last_rider = None if make_last_rider is None else make_last_rider(dw_in)
    dh0, dg1, *last_arrivals = _inproj_bwd(dzm, dzf, w["wm"], w["wf"], sv["h0"], dh1, sv["g1"], last_rider)
    grads = dict(norm_g=dg1.reshape(D_MODEL), w_in=dw_in, b_f=dbf[0, :N_HEADS],
                 qk_norm_g=dqkg[:, :HEAD_DIM] + dqkg[:, HEAD_DIM:], ple_norm_g=dg2.reshape(D_MODEL), **early)
    return dh0, grads, arrivals + last_arrivals


def _tables():
    T = TOK_T
    r = lax.broadcasted_iota(jnp.int32, (T, T), 0)
    c = lax.broadcasted_iota(jnp.int32, (T, T), 1)
    return dict(fox=_bias_tables(True), dil=_bias_tables(False),
                tril=(c <= r).astype(BF16), triu=(c >= r).astype(BF16))


def _local_step(x, p, positions, target, layers, small):
    rope = _rope_tables(positions)
    tabs = _tables()
    ws = [_layer_weights(*lw) for lw in layers]
    h = x
    saved = []
    for li, (w, lp, sm) in enumerate(zip(ws, p, small)):
        h, sv, _, *loss = _layer_fwd(h, (lp, ()), rope, tabs, w, *sm, target=target if li == len(ws) - 1 else None)
        saved.append(sv)
    dh, (loss,) = h, loss
    grads = [None] * len(ws)
    for li in reversed(range(len(ws))):
        dh, grads[li], _ = _layer_bwd(dh, saved[li], rope, tabs)
    return loss[0, 0], dh, grads


def _peers():
    x, y, c = lax.axis_index("x"), lax.axis_index("y"), lax.axis_index("c")
    me = 4 * x + 2 * y + c
    flip = lambda v, bit: 1 - v if bit else v
    return me, [(flip(x, k & 4), flip(y, k & 2), flip(c, k & 1)) for k in range(1, N_DEV)]


def _sel(ref, kind, d):
    if kind == "whole":
        return ref
    if kind == "slot":
        return ref.at[d]
    block = pl.ds(pl.multiple_of(d * 128, 128), 128)
    return ref.at[block, :] if kind == "rows" else ref.at[:, block]


class _Pushes:
    def __init__(self, arrays, src_kinds, dst_kinds, out_shapes):
        self.arrays, self.n = list(arrays), len(arrays)
        self.src_kinds, self.dst_kinds = src_kinds, dst_kinds
        self.out_shapes = [jax.ShapeDtypeStruct(s, a.dtype) for s, a in zip(out_shapes, arrays)]
        hbm = pl.BlockSpec(memory_space=pltpu.HBM)
        self.in_specs, self.out_specs = [hbm] * self.n, [hbm] * self.n
        self.scratch_shapes = [pltpu.SemaphoreType.DMA((N_DEV - 1, self.n)),
                               pltpu.SemaphoreType.DMA((N_DEV - 1, self.n)), pltpu.SemaphoreType.DMA((self.n,))]

    def _copies(self, ins, outs, sems):
        send_sems, recv_sems, local_sems = sems
        me, peers = _peers()
        src = lambda a, d: _sel(ins[a], self.src_kinds[a], d)
        dst = lambda a: _sel(outs[a], self.dst_kinds[a], me)
        local = [pltpu.make_async_copy(src(a, me), dst(a), local_sems.at[a]) for a in range(self.n)]
        remote = [pltpu.make_async_remote_copy(
            src_ref=src(a, 4 * px + 2 * py + pc), dst_ref=dst(a), send_sem=send_sems.at[k, a],
            recv_sem=recv_sems.at[k, a], device_id=(px, py, pc), device_id_type=pl.DeviceIdType.MESH)
            for k, (px, py, pc) in enumerate(peers) for a in range(self.n)]
        return local + remote

    def start(self, ins, outs, sems):
        for cp in self._copies(ins, outs, sems):
            cp.start()

    def wait(self, ins, outs, sems):
        for cp in self._copies(ins, outs, sems):
            cp.wait()


def _exchange(name, pushes):
    n = pushes.n

    def body(*refs):
        pushes.start(refs[:n], refs[n:2 * n], refs[2 * n:])
        pushes.wait(refs[:n], refs[n:2 * n], refs[2 * n:])

    return pl.pallas_call(body, name=name, in_specs=pushes.in_specs, out_specs=pushes.out_specs,
                          out_shape=pushes.out_shapes, scratch_shapes=pushes.scratch_shapes)(*pushes.arrays)


def _gather_two_level(shard, name):
    def body(x_ref, out_ref, send_sems, recv_sems, local_sem):
        x, y, c = lax.axis_index("x"), lax.axis_index("y"), lax.axis_index("c")
        me, sibling = (x, y, c), (x, y, 1 - c)
        chips = [(1 - x, y), (x, 1 - y), (1 - x, 1 - y)]
        slot = lambda px, py, pc: out_ref.at[4 * px + 2 * py + pc]

        def copy(k, block, to, src=None):
            return pltpu.make_async_remote_copy(
                src_ref=slot(*block) if src is None else src, dst_ref=slot(*block), send_sem=send_sems.at[k],
                recv_sem=recv_sems.at[k], device_id=to, device_id_type=pl.DeviceIdType.MESH)

        mine = pltpu.make_async_copy(x_ref, slot(*me), local_sem)
        mine.start()
        first = [copy(0, me, sibling, src=x_ref)] + [copy(1 + j, me, (*chip, c), src=x_ref)
                                                     for j, chip in enumerate(chips)]
        for cp in first:
            cp.start()
        passed = [copy(4 + j, (*chip, c), sibling) for j, chip in enumerate(chips)]
        for j, chip in enumerate(chips):
            copy(1 + j, (*chip, c), me).wait_recv()
            passed[j].start()
        copy(0, sibling, me).wait_recv()
        for j, chip in enumerate(chips):
            copy(4 + j, (*chip, 1 - c), me).wait_recv()
        for cp in first + passed:
            cp.wait_send()
        mine.wait()

    hbm = pl.BlockSpec(memory_space=pltpu.HBM)
    return pl.pallas_call(
        body, name=name, in_specs=[hbm], out_specs=hbm,
        out_shape=jax.ShapeDtypeStruct((N_DEV,) + shard.shape, shard.dtype),
        scratch_shapes=[pltpu.SemaphoreType.DMA((N_DEV - 1,)), pltpu.SemaphoreType.DMA((N_DEV - 1,)),
                        pltpu.SemaphoreType.DMA],
    )(shard)


def _gather_pushes(shards, kinds):
    full = {"slot": lambda s: (N_DEV,) + s, "rows": lambda s: (N_DEV * s[0], s[1]),
            "cols": lambda s: (s[0], N_DEV * s[1])}
    return _Pushes(shards, ["whole"] * len(shards), kinds, [full[k](a.shape) for a, k in zip(shards, kinds)])


def _scatter_pushes(partials, kinds):
    part = {"slot": lambda s: s[1:], "rows": lambda s: (128, s[1]), "cols": lambda s: (s[0], 128),
            "whole": lambda s: s}
    return _Pushes(partials, kinds, ["slot"] * len(partials),
                   [(N_DEV,) + part[k](a.shape) for a, k in zip(partials, kinds)])


def _adamw(name, parts, w, m, v, rows):
    L, R, C = w.shape

    def body(p_ref, w_ref, m_ref, v_ref, g_ref, d_ref, nm_ref, nv_ref):
        g = p_ref[0, 0].astype(F32)
        for s in range(1, N_DEV):
            g = g + p_ref[s, 0].astype(F32)
        g_ref[0] = g
        nm = ADAM_B1 * m_ref[0] + (1.0 - ADAM_B1) * g
        nv = ADAM_B2 * v_ref[0] + (1.0 - ADAM_B2) * (g * g)
        nm_ref[0] = nm
        nv_ref[0] = nv
        m_hat = nm / (1.0 - ADAM_B1 ** ADAM_STEP)
        v_hat = nv / (1.0 - ADAM_B2 ** ADAM_STEP)
        d_ref[0] = -ADAM_LR * (m_hat / (jnp.sqrt(v_hat) + ADAM_EPS) + ADAM_WD * w_ref[0])

    blk = pl.BlockSpec((1, rows, C), lambda l, i: (l, i, 0))
    shp = jax.ShapeDtypeStruct((L, R, C), F32)
    return pl.pallas_call(
        body, name=name, grid=(L, R // rows),
        in_specs=[pl.BlockSpec((N_DEV, 1, rows, C), lambda l, i: (0, l, i, 0)), blk, blk, blk],
        out_specs=[blk] * 4, out_shape=[shp] * 4,
        compiler_params=_cp("parallel", "parallel"),
    )(parts, w, m, v)


SMALL_ROWS = 40
LOSS_ROW = 37


def _pack_small(norm_g, ple_g, qk_g, b_f, last_row):
    rows = lambda a: a.astype(F32).reshape(-1, 128)
    flat = jnp.concatenate([rows(norm_g), rows(ple_g), rows(qk_g), _row(b_f.reshape(-1)), last_row], axis=0)
    return jnp.pad(flat, ((0, SMALL_ROWS - flat.shape[0]), (0, 0)))


def _unpack_small(flat):
    return (flat[0:16].reshape(2, D_MODEL), flat[16:32].reshape(2, D_MODEL), flat[32:36].reshape(2, 4, HEAD_DIM),
            flat[36, :2 * N_HEADS].reshape(2, N_HEADS))


def kernel(x, p, positions, norm_g, w_in, b_f, qk_norm_g, w_out, w_ple, ple_norm_g, w_ple_gate, loss_target, m_norm_g, m_w_in, m_b_f, m_qk_norm_g, m_w_out, m_w_ple, m_ple_norm_g, m_w_ple_gate, v_norm_g, v_w_in, v_b_f, v_qk_norm_g, v_w_out, v_w_ple, v_ple_norm_g, v_w_ple_gate):
    bf16 = lambda a: a.astype(BF16)
    rows_in = W_IN_ROWS // 2
    flat_in = lambda a: bf16(a).reshape(rows_in, 128)
    full_in = lambda g: g.reshape(N_DEV, D_MODEL, W_IN_SHARD).transpose(1, 0, 2).reshape(D_MODEL, N_IN)
    small = [(norm_g[l], b_f[l], qk_norm_g[l], ple_norm_g[l]) for l in range(2)]
    rope = _rope_tables(positions[0])
    tabs = _tables()

    g_in0 = _gather_two_level(flat_in(w_in[0]), "gather_first")
    rest = _gather_pushes([flat_in(w_in[1])] + [bf16(a[l]) for l in range(2) for a in (w_out, w_ple, w_ple_gate)],
                          ["slot"] + ["rows", "cols", "rows"] * 2)
    late = lambda got: dict(w_out=got[1], w_ple=got[2], w_pg=got[3])
    h, sv0, got = _layer_fwd(x[0], (p, (0, 0)), rope, tabs, _in_operands(full_in(g_in0)), *small[0], rest, late)
    w1 = dict(_in_operands(full_in(got[0])), w_out=got[4], w_ple=got[5], w_pg=got[6])
    dh, sv1, _, loss = _layer_fwd(h, (p, (1, 0)), rope, tabs, w1, *small[1], target=loss_target[0])
    dh, gr1, _ = _layer_bwd(dh, sv1, rope, tabs)

    by_dest = lambda d: d.reshape(D_MODEL, N_DEV, W_IN_SHARD).transpose(1, 0, 2).reshape(N_DEV, rows_in, 128)
    big = ("w_out", "w_ple", "w_ple_gate")
    riding = lambda early: _scatter_pushes([by_dest(gr1["w_in"])] + [gr1[n] for n in big] + [early[n] for n in big],
                                           ["slot"] + ["rows", "cols", "rows"] * 2)
    riding_last = lambda dw_in: _scatter_pushes([by_dest(dw_in)], ["slot"])
    dx, gr0, (r_in1, *r_big, r_in0) = _layer_bwd(dh, sv0, rope, tabs, riding, riding_last)
    grads = (gr0, gr1)
    stack = lambda name: jnp.stack([gl[name] for gl in grads], axis=0)
    small_part = _pack_small(stack("norm_g"), stack("ple_norm_g"), stack("qk_norm_g"), stack("b_f"),
                             _row(loss[0, 0].reshape(1)))
    (r_small,) = _exchange("exchange_small", _scatter_pushes([small_part], ["whole"]))
    r_in = jnp.concatenate([r_in0, r_in1], axis=1)
    r_out, r_ple, r_pg = (jnp.stack([r_big[3 + k], r_big[k]], axis=1) for k in range(3))

    zero_row = jnp.zeros((1, 128), F32)
    small_of = lambda ng, pg, qk, bf: _pack_small(ng, pg, qk, bf, zero_row)[None]
    flat = lambda a: a.reshape(1, W_IN_ROWS, 128)
    outs = dict(
        w_in=[o.reshape(w_in.shape) for o in
              _adamw("adamw_in", r_in[:, None], flat(w_in), flat(m_w_in), flat(v_w_in), W_IN_TILE)],
        w_out=_adamw("adamw_out", r_out, w_out, m_w_out, v_w_out, 128),
        w_ple=_adamw("adamw_ple", r_ple, w_ple, m_w_ple, v_w_ple, 256),
        w_pg=_adamw("adamw_gate", r_pg, w_ple_gate, m_w_ple_gate, v_w_ple_gate, 128),
        small=_adamw("adamw_small", r_small[:, None], small_of(norm_g, ple_norm_g, qk_norm_g, b_f),
                     small_of(m_norm_g, m_ple_norm_g, m_qk_norm_g, m_b_f),
                     small_of(v_norm_g, v_ple_norm_g, v_qk_norm_g, v_b_f), SMALL_ROWS))
    leaves = []
    for kind in range(4):
        ng, pg, qk, bf = _unpack_small(outs["small"][kind][0])
        leaves += [ng, outs["w_in"][kind], bf, qk, outs["w_out"][kind], outs["w_ple"][kind], pg, outs["w_pg"][kind]]
    return (outs["small"][0][0, LOSS_ROW, 0], dx[None], *leaves)
```

```python
import functools

import jax
import jax.numpy as jnp
from jax import lax
from jax.experimental import pallas as pl
from jax.experimental.pallas import tpu as pltpu

F32 = jnp.float32
BF16 = jnp.bfloat16

D_MODEL = 1024
HEAD_DIM = 64
N_HEADS = 8
HEAD_PAD = 128
D_BRANCH = N_HEADS * HEAD_DIM
N_MAIN = 8 * D_BRANCH
N_IN = N_MAIN + N_HEADS
PLE_DIM = 256
ROPE_THETA = 500000.0
ROPE_HALF = 8
EPS = 1e-6
NEG = -1e30
SCALE = HEAD_DIM ** -0.5
LOG2E = 1.4426950408889634
LN2 = 0.6931471805599453
DILATED_PATTERNS = ((128, 1), (512, 4), (2048, 16))
N_DEV = 8
W_IN_SHARD = N_IN // N_DEV
W_IN_ROWS = 2 * D_MODEL * W_IN_SHARD // 128
W_IN_TILE = W_IN_ROWS // 19

ADAM_LR = 0.001
ADAM_B1 = 0.9
ADAM_B2 = 0.999
ADAM_EPS = 1e-08
ADAM_WD = 0.01
ADAM_STEP = 10

ATT_T = 512
ATT_FWD_HEADS = 8
ATT_BWD_HEADS = 2
ATT_CHUNK = 32
TOK_T = 256
VMEM_LIMIT = 60 * 1024 * 1024


def _slab_spec(lead, block, index):
    return pl.BlockSpec((None,) * len(lead) + block, lambda *g: (*lead, *index(*g)))


def _cp(*sem):
    return pltpu.CompilerParams(dimension_semantics=sem, vmem_limit_bytes=VMEM_LIMIT)


def _sigmoid(x):
    return 1.0 / (1.0 + jnp.exp(-x))


def _split3(x):
    hi = x.astype(BF16)
    r1 = x - hi.astype(F32)
    mid = r1.astype(BF16)
    lo = (r1 - mid.astype(F32)).astype(BF16)
    return hi, mid, lo


def _dot(a, b):
    return jnp.dot(a, b, preferred_element_type=F32)


def _dot_nt(a, b):
    return lax.dot_general(a, b, (((1,), (1,)), ((), ())), preferred_element_type=F32)


def _dot_tn(a, b):
    return lax.dot_general(a, b, (((0,), (0,)), ((), ())), preferred_element_type=F32)


def _inproj_fwd(h, g, wm, wf):
    S = h.shape[0]

    def body(h_ref, g_ref, wm_ref, wf_ref, zm_ref, zf_ref, u_ref):
        x = h_ref[...]
        r = lax.rsqrt(jnp.mean(x * x, axis=-1, keepdims=True) + EPS)
        u = (x * r * g_ref[...]).astype(BF16)
        u_ref[...] = u
        zm_ref[...] = _dot(u, wm_ref[...]).astype(BF16)
        zf_ref[...] = _dot(u, wf_ref[...])

    return pl.pallas_call(
        body, name="inproj_fwd", grid=(S // TOK_T,),
        in_specs=[pl.BlockSpec((TOK_T, D_MODEL), lambda i: (i, 0)),
                  pl.BlockSpec((1, D_MODEL), lambda i: (0, 0)),
                  pl.BlockSpec((D_MODEL, N_MAIN), lambda i: (0, 0)),
                  pl.BlockSpec((D_MODEL, 128), lambda i: (0, 0))],
        out_specs=[pl.BlockSpec((TOK_T, N_MAIN), lambda i: (i, 0)),
                   pl.BlockSpec((TOK_T, 128), lambda i: (i, 0)),
                   pl.BlockSpec((TOK_T, D_MODEL), lambda i: (i, 0))],
        out_shape=[jax.ShapeDtypeStruct((S, N_MAIN), BF16), jax.ShapeDtypeStruct((S, 128), F32),
                   jax.ShapeDtypeStruct((S, D_MODEL), BF16)],
        compiler_params=_cp("parallel"),
    )(h, g, wm, wf)


def _log_sigmoid(x):
    return jnp.minimum(x, 0.0) - jnp.log(1.0 + jnp.exp(-jnp.abs(x)))


def _same_head():
    r = lax.broadcasted_iota(jnp.int32, (HEAD_PAD, HEAD_PAD), 0) // HEAD_DIM
    c = lax.broadcasted_iota(jnp.int32, (HEAD_PAD, HEAD_PAD), 1) // HEAD_DIM
    return (r == c).astype(BF16)


def _pair_mean(x, same_head):
    hi = x.astype(BF16)
    lo = (x - hi.astype(F32)).astype(BF16)
    return (_dot(hi, same_head) + _dot(lo, same_head)) * (1.0 / HEAD_DIM)


def _pair_rsqrt(x, same_head):
    return lax.rsqrt(_pair_mean(x * x, same_head) + EPS)


def _prep_fwd(zm, zf, bf, tril, qkg, rope_c, rope_a, rope_b):
    S = zm.shape[0]
    shp = jax.ShapeDtypeStruct((N_HEADS, S, HEAD_PAD), BF16)

    def body(z_ref, zf_ref, b_ref, tri_ref, g_ref, rc_ref, ra_ref, rb_ref,
             qa_ref, ka_ref, va_ref, qb_ref, kb_ref, vb_ref, carry):
        @pl.when(pl.program_id(0) == 0)
        def _():
            carry[...] = jnp.zeros_like(carry)

        tri = tri_ref[...]
        cs = sum(_dot(tri, part) for part in _split3(_log_sigmoid(zf_ref[...] + b_ref[...]))) + carry[...]
        carry[...] = cs[TOK_T - 1:TOK_T, :]
        lane = lax.broadcasted_iota(jnp.int32, (TOK_T, HEAD_PAD), 1)
        lo_half = lane < HEAD_DIM
        aug = (lane >= HEAD_DIM) & (lane < HEAD_DIM + 3)
        q_pad = jnp.where(aug, -1.0, 0.0)
        rc, ra, rb = rc_ref[...], ra_ref[...], rb_ref[...]
        same_head = _same_head()

        def norm(col, gi):
            x = z_ref[:, col:col + HEAD_PAD].astype(F32)
            return x * _pair_rsqrt(x, same_head) * g_ref[gi:gi + 1, :]

        def rope(y):
            return y * rc + pltpu.roll(y, HEAD_PAD - ROPE_HALF, 1) * ra + pltpu.roll(y, ROPE_HALF, 1) * rb

        def put(ref, pi, y, pad_even, pad_odd):
            ref[2 * pi] = jnp.where(lo_half, y, pad_even).astype(BF16)
            ref[2 * pi + 1] = jnp.where(lo_half, pltpu.roll(y, HEAD_DIM, 1), pad_odd).astype(BF16)

        def k_pad(h):
            ch = cs[:, h:h + 1] * LOG2E
            hi = ch.astype(BF16).astype(F32)
            mid = (ch - hi).astype(BF16).astype(F32)
            lo = ch - hi - mid
            ones = jnp.where(lane == HEAD_DIM + 3, 1.0, 0.0)
            return jnp.where(lane == HEAD_DIM, hi, jnp.where(lane == HEAD_DIM + 1, mid,
                                                              jnp.where(lane == HEAD_DIM + 2, lo, ones)))

        for pi in range(N_HEADS // 2):
            col = HEAD_PAD * pi
            put(qa_ref, pi, norm(col, 0) * (SCALE * LOG2E), q_pad, q_pad)
            put(ka_ref, pi, norm(D_BRANCH + col, 1), k_pad(2 * pi), k_pad(2 * pi + 1))
            put(va_ref, pi, z_ref[:, 2 * D_BRANCH + col:2 * D_BRANCH + col + HEAD_PAD].astype(F32), 0.0, 0.0)
            put(qb_ref, pi, rope(norm(4 * D_BRANCH + col, 2)) * (SCALE * LOG2E), 0.0, 0.0)
            put(kb_ref, pi, rope(norm(5 * D_BRANCH + col, 3)), 0.0, 0.0)
            put(vb_ref, pi, z_ref[:, 6 * D_BRANCH + col:6 * D_BRANCH + col + HEAD_PAD].astype(F32), 0.0, 0.0)

    tok = lambda w: pl.BlockSpec((TOK_T, w), lambda i: (i, 0))
    head = pl.BlockSpec((N_HEADS, TOK_T, HEAD_PAD), lambda i: (0, i, 0))
    return pl.pallas_call(
        body, name="prep_fwd", grid=(S // TOK_T,),
        in_specs=[tok(N_MAIN), tok(128), pl.BlockSpec((1, 128), lambda i: (0, 0)),
                  pl.BlockSpec((TOK_T, TOK_T), lambda i: (0, 0)), pl.BlockSpec((4, 128), lambda i: (0, 0)),
                  tok(128), tok(128), tok(128)],
        out_specs=[head] * 6, out_shape=[shp] * 6,
        scratch_shapes=[pltpu.VMEM((1, 128), F32)],
        compiler_params=_cp("arbitrary"),
    )(zm, zf, bf, tril, qkg, rope_c, rope_a, rope_b)


def _pair(ref, pi, lo_half):
    return jnp.where(lo_half, ref[2 * pi].astype(F32), pltpu.roll(ref[2 * pi + 1].astype(F32), HEAD_DIM, 1))


def _mid_fwd(oa, ob, zm, h0, p, w_out, w_pg, w_ple, g2, target=None):
    S = h0.shape[0]
    p, p_lead = p

    def body(oa_ref, ob_ref, ga_ref, gb_ref, h0_ref, p_ref, wo_ref, wg_ref, wp_ref, g2_ref, *rest):
        t_ref, rest = (rest[0], rest[1:]) if target is not None else (None, rest)
        y_ref, h1_ref, h2_ref, u2_ref, e_ref, gate_ref, *loss_ref = rest
        parts = []
        for o_ref, g_ref in ((oa_ref, ga_ref), (ob_ref, gb_ref)):
            for pi in range(N_HEADS // 2):
                g = g_ref[:, HEAD_PAD * pi:HEAD_PAD * (pi + 1)].astype(F32)
                parts.append((o_ref[pi] * (g * _sigmoid(g))).astype(BF16))
        y = jnp.concatenate(parts, axis=1)
        y_ref[...] = y
        h1 = h0_ref[...] + _dot(y, wo_ref[...])
        h1_ref[...] = h1
        r = lax.rsqrt(jnp.mean(h1 * h1, axis=-1, keepdims=True) + EPS)
        u2 = (h1 * r * g2_ref[...]).astype(BF16)
        u2_ref[...] = u2
        gate = _sigmoid(_dot(u2, wg_ref[...]))
        e = _dot(p_ref[...].astype(BF16), wp_ref[...])
        e_ref[...] = e.astype(BF16)
        gate_ref[...] = gate.astype(BF16)
        h2 = h1 + e * gate
        if target is None:
            h2_ref[...] = h2
        else:
            @pl.when(pl.program_id(0) == 0)
            def _():
                loss_ref[0][...] = jnp.zeros_like(loss_ref[0])

            err = h2 - t_ref[...]
            h2_ref[...] = err * (1.0 / D_MODEL)
            part = jnp.sum(jnp.sum(err * err, axis=1, keepdims=True), axis=0, keepdims=True)
            loss_ref[0][...] += part * (0.5 / D_MODEL)

    tok = lambda w: pl.BlockSpec((TOK_T, w), lambda i: (i, 0))
    head = pl.BlockSpec((N_HEADS // 2, TOK_T, HEAD_PAD), lambda i: (0, i, 0))
    full = lambda a, b: pl.BlockSpec((a, b), lambda i: (0, 0))
    act = lambda dt: jax.ShapeDtypeStruct((S, D_MODEL), dt)
    fused = target is not None
    return pl.pallas_call(
        body, name="mid_fwd_loss" if fused else "mid_fwd", grid=(S // TOK_T,),
        in_specs=[head, head,
                  pl.BlockSpec((TOK_T, D_BRANCH), lambda i: (i, 3)), pl.BlockSpec((TOK_T, D_BRANCH), lambda i: (i, 7)),
                  tok(D_MODEL), _slab_spec(p_lead, (TOK_T, PLE_DIM), lambda i: (i, 0)), full(D_MODEL, D_MODEL),
                  full(D_MODEL, D_MODEL), full(PLE_DIM, D_MODEL), full(1, D_MODEL)] + [tok(D_MODEL)] * fused,
        out_specs=[tok(D_MODEL)] * 6 + [full(8, 128)] * fused,
        out_shape=[act(BF16), act(F32), act(F32), act(BF16), act(BF16), act(BF16)]
        + [jax.ShapeDtypeStruct((8, 128), F32)] * fused,
        compiler_params=_cp("arbitrary" if fused else "parallel"),
    )(oa, ob, zm, zm, h0, p, w_out, w_pg, w_ple, g2, *([target] * fused))


def _bias_tables(full_range):
    T = ATT_T
    nb = 1 if full_range else DILATED_PATTERNS[-1][0] // T + 1
    r = lax.broadcasted_iota(jnp.int32, (nb, T, T), 2)
    c = lax.broadcasted_iota(jnp.int32, (nb, T, T), 1)
    b = lax.broadcasted_iota(jnp.int32, (nb, T, T), 0)
    delta = T * b + r - c
    if full_range:
        bias = jnp.where(delta >= 0, 0.0, NEG).astype(F32)
    else:
        mult = jnp.zeros((nb, T, T), F32)
        for window, dil in DILATED_PATTERNS:
            ok = (delta >= 0) & (delta <= window) & (delta % dil == 0)
            mult = mult + ok.astype(F32)
        bias = jnp.where(mult > 0, jnp.log2(jnp.maximum(mult, 1.0)), NEG).astype(F32)
    return bias


def _call_with_rider(body, name, grid, rider, in_specs, out_specs, out_shape, scratch_shapes, operands,
                     semantics=("parallel", "arbitrary")):
    if rider is None:
        return pl.pallas_call(body, name=name, grid=grid, in_specs=in_specs, out_specs=out_specs,
                              out_shape=out_shape, scratch_shapes=scratch_shapes,
                              compiler_params=_cp(*semantics))(*operands)
    n, n_in, n_out = rider.n, len(in_specs), len(out_specs)

    def wrapped(*refs):
        ins, r_ins = refs[:n_in], refs[n_in:n_in + n]
        outs, r_outs = refs[n_in + n:n_in + n + n_out], refs[n_in + n + n_out:n_in + 2 * n + n_out]
        scratch, sems = refs[n_in + 2 * n + n_out:-3], refs[-3:]
        step = [pl.program_id(a) for a in range(len(grid))]

        @pl.when(functools.reduce(jnp.logical_and, [s == 0 for s in step]))
        def _():
            rider.start(r_ins, r_outs, sems)

        body(*ins, *outs, *scratch)

        @pl.when(functools.reduce(jnp.logical_and, [s == g - 1 for s, g in zip(step, grid)]))
        def _():
            rider.wait(r_ins, r_outs, sems)

    return pl.pallas_call(
        wrapped, name=name, grid=grid, in_specs=list(in_specs) + rider.in_specs,
        out_specs=list(out_specs) + rider.out_specs, out_shape=list(out_shape) + rider.out_shapes,
        scratch_shapes=list(scratch_shapes) + rider.scratch_shapes,
        compiler_params=_cp(*["arbitrary"] * len(grid)))(*operands, *rider.arrays)


def _attn_fwd(q, k, v, table_t, full_range, name, rider=None):
    H, S, _ = q.shape
    T = ATT_T
    nb = table_t.shape[0]
    HB = ATT_FWD_HEADS
    KC = ATT_CHUNK
    chunks = [slice(c, c + KC) for c in range(0, T, KC)]
    fold = lambda x, op: functools.reduce(op, [x[r:r + 8] for r in range(0, KC, 8)])

    def body(q_ref, k_ref, v_ref, tab_ref, o_ref, lse_ref, *scratch):
        st_refs, pt_refs, acc_refs = scratch[:HB], scratch[HB:2 * HB], scratch[2 * HB:]
        i = pl.program_id(1)
        rows = lambda j: pl.ds(pl.multiple_of(j * T, T), T)

        def scores(hh, j):
            st_refs[hh][...] = _dot_nt(k_ref[hh, rows(j), :], q_ref[hh])

        def block(j, b, nxt, stats):
            out = []
            for hh, (m, l) in enumerate(stats):
                st_ref, pt_ref, acc_ref = st_refs[hh], pt_refs[hh], acc_refs[hh]
                mx = None
                for ch in chunks:
                    x = st_ref[ch, :]
                    if b is not None:
                        x = x + tab_ref[b, ch, :]
                        st_ref[ch, :] = x
                    x = fold(x, jnp.maximum)
                    mx = x if mx is None else jnp.maximum(mx, x)
                m_new = jnp.maximum(m, jnp.max(mx, axis=0, keepdims=True))
                alpha = jnp.exp2(m - m_new)
                ls = None
                for ch in chunks:
                    pc = jnp.exp2(st_ref[ch, :] - m_new)
                    pt_ref[ch, :] = pc.astype(BF16)
                    pc = fold(pc, jnp.add)
                    ls = pc if ls is None else ls + pc
                if nxt is not None:
                    scores(hh, nxt)
                acc_ref[...] = alpha * acc_ref[...] + _dot_tn(v_ref[hh, rows(j), :], pt_ref[...])
                out.append((m_new, alpha * l + jnp.sum(ls, axis=0, keepdims=True)))
            return tuple(out)

        lo = 0 if full_range else jnp.maximum(i - (nb - 1), 0)
        for hh in range(HB):
            acc_refs[hh][...] = jnp.zeros_like(acc_refs[hh])
            scores(hh, lo)
        stats = lax.fori_loop(lo, i, lambda j, st: block(j, None if full_range else i - j, j + 1, st),
                              ((jnp.full((1, T), NEG, F32), jnp.zeros((1, T), F32)),) * HB)
        stats = block(i, 0, None, stats)
        o_t = [acc_refs[hh][...] * (1.0 / l) for hh, (m, l) in enumerate(stats)]
        for hh, (m, l) in enumerate(stats):
            lse_ref[hh, 0] = m + jnp.log2(l)
        for hp in range(HB // 2):
            o_ref[hp] = jnp.concatenate([o_t[2 * hp][:HEAD_DIM], o_t[2 * hp + 1][:HEAD_DIM]], axis=0).T

    return _call_with_rider(
        body, name, (H // HB, S // T), rider,
        in_specs=[pl.BlockSpec((HB, T, HEAD_PAD), lambda h, i: (h, i, 0)),
                  pl.BlockSpec((HB, S, HEAD_PAD), lambda h, i: (h, 0, 0), pipeline_mode=pl.Buffered(1)),
                  pl.BlockSpec((HB, S, HEAD_PAD), lambda h, i: (h, 0, 0), pipeline_mode=pl.Buffered(1)),
                  pl.BlockSpec((nb, T, T), lambda h, i: (0, 0, 0), pipeline_mode=pl.Buffered(1))],
        out_specs=[pl.BlockSpec((HB // 2, T, HEAD_PAD), lambda h, i: (h, i, 0)),
                   pl.BlockSpec((HB, 1, 1, T), lambda h, i: (h, i, 0, 0))],
        out_shape=[jax.ShapeDtypeStruct((H // 2, S, HEAD_PAD), F32), jax.ShapeDtypeStruct((H, S // T, 1, T), F32)],
        scratch_shapes=([pltpu.VMEM((T, T), F32)] * HB + [pltpu.VMEM((T, T), BF16)] * HB
                        + [pltpu.VMEM((HEAD_PAD, T), F32)] * HB),
        operands=(q, k, v, table_t))


def _attn_bwd(q, k, v, do, lse, dd, table_t, full_range, name, rider=None, narrow=(False, False, False)):
    H, S, _ = q.shape
    T = ATT_T
    nq = S // T
    nb = table_t.shape[0]
    HB = ATT_BWD_HEADS
    KC = ATT_CHUNK
    chunks = [slice(c, c + KC) for c in range(0, T, KC)]

    def body(q_ref, do_ref, lse_ref, dd_ref, k_ref, v_ref, tab_ref, dq_hbm, dk_ref, dv_ref, *scratch):
        st_refs, dpt_refs, pt_refs, dst_refs = (scratch[n * HB:(n + 1) * HB] for n in range(4))
        dq_ref, dk_acc, dv_acc, *dq_cast, dq_sem = scratch[4 * HB:]
        h = pl.program_id(0)
        j = pl.program_id(1)

        @pl.when(j == 0)
        def _():
            dq_ref[...] = jnp.zeros_like(dq_ref)

        dk_acc[...] = jnp.zeros_like(dk_acc)
        dv_acc[...] = jnp.zeros_like(dv_acc)

        def step(i, b):
            rows = pl.ds(pl.multiple_of(i * T, T), T)
            for hh in range(HB):
                st_refs[hh][...] = _dot_nt(k_ref[hh], q_ref[hh, rows, :])
                dpt_refs[hh][...] = _dot_nt(v_ref[hh], do_ref[hh, rows, :])
            for hh in range(HB):
                lse_i = lse_ref[hh, i]
                dd_i = dd_ref[hh, i]
                for ch in chunks:
                    x = st_refs[hh][ch, :]
                    if b is not None:
                        x = x + tab_ref[b, ch, :]
                    pc = jnp.exp2(x - lse_i)
                    pt_refs[hh][ch, :] = pc.astype(BF16)
                    dst_refs[hh][ch, :] = (pc * (dpt_refs[hh][ch, :] - dd_i)).astype(BF16)
                dv_acc[hh] += _dot(pt_refs[hh][...], do_ref[hh, rows, :])
                dk_acc[hh] += _dot(dst_refs[hh][...], q_ref[hh, rows, :])
                dq_ref[hh, rows, :] += _dot_tn(dst_refs[hh][...], k_ref[hh])

        step(j, 0)
        if full_range:
            pl.loop(j + 1, nq)(lambda i: step(i, None))
        else:
            pl.loop(j + 1, jnp.minimum(j + nb, nq))(lambda i: step(i, i - j))
        dk_ref[...] = dk_acc[...].astype(dk_ref.dtype)
        dv_ref[...] = dv_acc[...].astype(dv_ref.dtype)

        @pl.when(j == nq - 1)
        def _():
            src = dq_ref
            if narrow[0]:
                src, = dq_cast
                src[...] = dq_ref[...].astype(BF16)
            out = pltpu.make_async_copy(src, dq_hbm.at[pl.ds(h * HB, HB)], dq_sem)
            out.start()
            out.wait()

    once = dict(pipeline_mode=pl.Buffered(1))
    per_head = pl.BlockSpec((HB, S, HEAD_PAD), lambda h, j: (h, 0, 0), **once)
    rows = pl.BlockSpec((HB, nq, 1, T), lambda h, j: (h, 0, 0, 0))
    blk = pl.BlockSpec((HB, T, HEAD_PAD), lambda h, j: (h, j, 0))
    shp = [jax.ShapeDtypeStruct((H, S, HEAD_PAD), BF16 if nar else F32) for nar in narrow]
    acc = pltpu.VMEM((HB, T, HEAD_PAD), F32)
    return _call_with_rider(
        body, name, (H // HB, nq), rider,
        in_specs=[per_head, per_head, rows, rows, blk, blk,
                  pl.BlockSpec((nb, T, T), lambda h, j: (0, 0, 0), **once)],
        out_specs=[pl.BlockSpec(memory_space=pltpu.HBM), blk, blk], out_shape=shp,
        scratch_shapes=([pltpu.VMEM((T, T), F32)] * (2 * HB) + [pltpu.VMEM((T, T), BF16)] * (2 * HB)
                        + [pltpu.VMEM((HB, S, HEAD_PAD), F32), acc, acc]
                        + [pltpu.VMEM((HB, S, HEAD_PAD), BF16)] * narrow[0] + [pltpu.SemaphoreType.DMA]),
        operands=(q, do, lse, dd, k, v, table_t))


def _mid_bwd(dh2, h1, e, gate, g2, w_pg, w_out, oa, ob, zm):
    S = dh2.shape[0]

    def body(dh2_ref, h1_ref, e_ref, gate_ref, g2_ref, wg_ref, wo_ref, oa_ref, ob_ref, ga_ref, gb_ref,
             dh1_ref, dh1b_ref, de_ref, dpre_ref, doa_ref, dob_ref, dga_ref, dgb_ref, dd_ref, dg2_ref):
        @pl.when(pl.program_id(0) == 0)
        def _():
            dg2_ref[...] = jnp.zeros_like(dg2_ref)

        lane = lax.broadcasted_iota(jnp.int32, (TOK_T, HEAD_PAD), 1)
        lo_half = lane < HEAD_DIM
        dh2 = dh2_ref[...]
        gate = gate_ref[...]
        de_ref[...] = (dh2 * gate).astype(BF16)
        dpre = (dh2 * e_ref[...] * gate * (1.0 - gate)).astype(BF16)
        dpre_ref[...] = dpre
        du2 = _dot_nt(dpre, wg_ref[...])
        h1 = h1_ref[...]
        r = lax.rsqrt(jnp.mean(h1 * h1, axis=-1, keepdims=True) + EPS)
        xh = h1 * r
        a = du2 * g2_ref[...]
        dh1 = dh2 + r * (a - xh * jnp.mean(a * xh, axis=-1, keepdims=True))
        dg2_ref[...] += jnp.sum(du2 * xh, axis=0, keepdims=True)
        dh1_ref[...] = dh1
        dh1b = dh1.astype(BF16)
        dh1b_ref[...] = dh1b
        dy = _dot_nt(dh1b, wo_ref[...])
        dd = jnp.zeros((TOK_T, HEAD_PAD), F32)
        for bi, (o_ref, g_ref, do_ref, dg_ref) in enumerate(
                ((oa_ref, ga_ref, doa_ref, dga_ref), (ob_ref, gb_ref, dob_ref, dgb_ref))):
            for pi in range(N_HEADS // 2):
                col = bi * D_BRANCH + HEAD_PAD * pi
                dyp = dy[:, col:col + HEAD_PAD]
                g = g_ref[:, HEAD_PAD * pi:HEAD_PAD * (pi + 1)].astype(F32)
                sg = _sigmoid(g)
                o_pair = o_ref[pi]
                dg_ref[:, HEAD_PAD * pi:HEAD_PAD * (pi + 1)] = (
                    dyp * o_pair * (sg * (1.0 + g * (1.0 - sg)))).astype(BF16)
                dop = dyp * (g * sg)
                prod = dop * o_pair
                for hh, d_head, mine in ((2 * pi, dop, lo_half),
                                         (2 * pi + 1, pltpu.roll(dop, HEAD_DIM, 1), ~lo_half)):
                    do_ref[hh] = jnp.where(lo_half, d_head, 0.0).astype(BF16)
                    dsum = jnp.sum(jnp.where(mine, prod, 0.0), axis=1, keepdims=True)
                    dd = dd + jnp.where(lane == bi * N_HEADS + hh, dsum, 0.0)
        dd_ref[...] = dd.T[:2 * N_HEADS, :]

    tok = lambda w: pl.BlockSpec((TOK_T, w), lambda i: (i, 0))
    head = pl.BlockSpec((N_HEADS, TOK_T, HEAD_PAD), lambda i: (0, i, 0))
    pairs = pl.BlockSpec((N_HEADS // 2, TOK_T, HEAD_PAD), lambda i: (0, i, 0))
    full = lambda a, b: pl.BlockSpec((a, b), lambda i: (0, 0))
    act = lambda w, dt: jax.ShapeDtypeStruct((S, w), dt)
    hshape = lambda w, dt: jax.ShapeDtypeStruct((N_HEADS, S, w), dt)
    return pl.pallas_call(
        body, name="mid_bwd", grid=(S // TOK_T,),
        in_specs=[tok(D_MODEL)] * 4 + [full(1, D_MODEL), full(D_MODEL, D_MODEL), full(D_MODEL, D_MODEL), pairs, pairs,
                                      pl.BlockSpec((TOK_T, D_BRANCH), lambda i: (i, 3)),
                                      pl.BlockSpec((TOK_T, D_BRANCH), lambda i: (i, 7))],
        out_specs=[tok(D_MODEL)] * 4 + [head, head, tok(D_BRANCH), tok(D_BRANCH),
                                       pl.BlockSpec((2 * N_HEADS, TOK_T), lambda i: (0, i)), full(1, D_MODEL)],
        out_shape=[act(D_MODEL, F32), act(D_MODEL, BF16), act(D_MODEL, BF16), act(D_MODEL, BF16),
                   hshape(HEAD_PAD, BF16), hshape(HEAD_PAD, BF16), act(D_BRANCH, BF16), act(D_BRANCH, BF16),
                   jax.ShapeDtypeStruct((2 * N_HEADS, S), F32), jax.ShapeDtypeStruct((1, D_MODEL), F32)],
        compiler_params=_cp("arbitrary"),
    )(dh2, h1, e, gate, g2, w_pg, w_out, oa, ob, zm, zm)


def _prep_bwd(dqa, dka, dva, dqb, dkb, dvb, zm, qkg, rope_c, rope_a, rope_b, dga, dgb, zf, bf, triu):
    S = zm.shape[0]
    n = S // TOK_T

    def body(dqa_ref, dka_ref, dva_ref, dqb_ref, dkb_ref, dvb_ref, z_ref, g_ref, rc_ref, ra_ref, rb_ref,
             dga_ref, dgb_ref, zf_ref, b_ref, tri_ref, dz_ref, dzf_ref, dqkg_ref, db_ref, carry):
        @pl.when(pl.program_id(0) == 0)
        def _():
            dqkg_ref[...] = jnp.zeros_like(dqkg_ref)
            db_ref[...] = jnp.zeros_like(db_ref)
            carry[...] = jnp.zeros_like(carry)

        lane = lax.broadcasted_iota(jnp.int32, (TOK_T, HEAD_PAD), 1)
        lo_half = lane < HEAD_DIM
        rc, ra, rb = rc_ref[...], ra_ref[...], rb_ref[...]
        same_head = _same_head()

        def unrope(dy):
            return dy * rc + pltpu.roll(dy * ra, ROPE_HALF, 1) + pltpu.roll(dy * rb, HEAD_PAD - ROPE_HALF, 1)

        def norm_bwd(col, gi, dy):
            x = z_ref[:, col:col + HEAD_PAD].astype(F32)
            r = _pair_rsqrt(x, same_head)
            xh = x * r
            dqkg_ref[gi:gi + 1, :] += jnp.sum(dy * xh, axis=0, keepdims=True)
            a = dy * g_ref[gi:gi + 1, :]
            dz_ref[:, col:col + HEAD_PAD] = (r * (a - xh * _pair_mean(a * xh, same_head))).astype(BF16)

        dc = jnp.zeros((TOK_T, HEAD_PAD), F32)
        for pi in range(N_HEADS // 2):
            col = HEAD_PAD * pi
            norm_bwd(col, 0, _pair(dqa_ref, pi, lo_half) * SCALE)
            norm_bwd(D_BRANCH + col, 1, _pair(dka_ref, pi, lo_half) * LN2)
            dz_ref[:, 2 * D_BRANCH + col:2 * D_BRANCH + col + HEAD_PAD] = _pair(dva_ref, pi, lo_half).astype(BF16)
            norm_bwd(4 * D_BRANCH + col, 2, unrope(_pair(dqb_ref, pi, lo_half) * SCALE))
            norm_bwd(5 * D_BRANCH + col, 3, unrope(_pair(dkb_ref, pi, lo_half) * LN2))
            dz_ref[:, 6 * D_BRANCH + col:6 * D_BRANCH + col + HEAD_PAD] = _pair(dvb_ref, pi, lo_half).astype(BF16)
            for hh in (2 * pi, 2 * pi + 1):
                dch = dka_ref[hh][:, HEAD_DIM:HEAD_DIM + 1] + dqa_ref[hh][:, HEAD_DIM + 3:HEAD_DIM + 4]
                dc = dc + jnp.where(lane == hh, dch, 0.0)
        dz_ref[:, 3 * D_BRANCH:4 * D_BRANCH] = dga_ref[...]
        dz_ref[:, 7 * D_BRANCH:8 * D_BRANCH] = dgb_ref[...]
        tri = tri_ref[...]
        dlf = sum(_dot(tri, part) for part in _split3(dc)) + carry[...]
        carry[...] = dlf[0:1, :]
        dfa = dlf * (1.0 - _sigmoid(zf_ref[...] + b_ref[...]))
        dzf_ref[...] = dfa.astype(BF16)
        db_ref[...] += jnp.sum(dfa, axis=0, keepdims=True)

    tok = lambda w: pl.BlockSpec((TOK_T, w), lambda i: (n - 1 - i, 0))
    head = pl.BlockSpec((N_HEADS, TOK_T, HEAD_PAD), lambda i: (0, n - 1 - i, 0))
    fixed = lambda a, b: pl.BlockSpec((a, b), lambda i: (0, 0))
    return pl.pallas_call(
        body, name="prep_bwd", grid=(n,),
        in_specs=[head] * 6 + [tok(N_MAIN), fixed(4, 128), tok(128), tok(128), tok(128), tok(D_BRANCH), tok(D_BRANCH),
                               tok(128), fixed(1, 128), fixed(TOK_T, TOK_T)],
        out_specs=[tok(N_MAIN), tok(128), fixed(4, 128), fixed(1, 128)],
        out_shape=[jax.ShapeDtypeStruct((S, N_MAIN), BF16), jax.ShapeDtypeStruct((S, 128), BF16),
                   jax.ShapeDtypeStruct((4, 128), F32), jax.ShapeDtypeStruct((1, 128), F32)],
        scratch_shapes=[pltpu.VMEM((1, 128), F32)],
        compiler_params=_cp("arbitrary"),
    )(dqa, dka, dva, dqb, dkb, dvb, zm, qkg, rope_c, rope_a, rope_b, dga, dgb, zf, bf, triu)


def _inproj_bwd(dzm, dzf, wm, wf, h0, dh1, g, rider=None):
    S = h0.shape[0]

    def body(dzm_ref, dzf_ref, wm_ref, wf_ref, h_ref, dh1_ref, g_ref, dh0_ref, dg_ref):
        @pl.when(pl.program_id(0) == 0)
        def _():
            dg_ref[...] = jnp.zeros_like(dg_ref)

        du = _dot_nt(dzm_ref[...], wm_ref[...]) + _dot_nt(dzf_ref[...], wf_ref[...])
        x = h_ref[...]
        r = lax.rsqrt(jnp.mean(x * x, axis=-1, keepdims=True) + EPS)
        xh = x * r
        a = du * g_ref[...]
        dh0_ref[...] = dh1_ref[...] + r * (a - xh * jnp.mean(a * xh, axis=-1, keepdims=True))
        dg_ref[...] += jnp.sum(du * xh, axis=0, keepdims=True)

    tok = lambda w: pl.BlockSpec((TOK_T, w), lambda i: (i, 0))
    full = lambda a, b: pl.BlockSpec((a, b), lambda i: (0, 0))
    return _call_with_rider(
        body, "inproj_bwd", (S // TOK_T,), rider,
        in_specs=[tok(N_MAIN), tok(128), full(D_MODEL, N_MAIN), full(D_MODEL, 128), tok(D_MODEL), tok(D_MODEL),
                  full(1, D_MODEL)],
        out_specs=[tok(D_MODEL), full(1, D_MODEL)],
        out_shape=[jax.ShapeDtypeStruct((S, D_MODEL), F32), jax.ShapeDtypeStruct((1, D_MODEL), F32)],
        scratch_shapes=[], operands=(dzm, dzf, wm, wf, h0, dh1, g), semantics=("arbitrary",))


def _wgrad(a, b, name, a_lead=()):
    S, M = a.shape[len(a_lead):]
    N = b.shape[1]
    tn = min(N, 2048)
    ts = 512
    last = S // ts - 1

    def body(a_ref, b_ref, o_ref, acc_ref):
        @pl.when(pl.program_id(1) == 0)
        def _():
            acc_ref[...] = jnp.zeros_like(acc_ref)

        acc_ref[...] += _dot_tn(a_ref[...].astype(BF16), b_ref[...])

        @pl.when(pl.program_id(1) == last)
        def _():
            o_ref[...] = acc_ref[...].astype(BF16)

    return pl.pallas_call(
        body, name=name, grid=(N // tn, S // ts),
        in_specs=[_slab_spec(a_lead, (ts, M), lambda n, s: (s, 0)), pl.BlockSpec((ts, tn), lambda n, s: (s, n))],
        out_specs=pl.BlockSpec((M, tn), lambda n, s: (0, n)),
        out_shape=jax.ShapeDtypeStruct((M, N), BF16),
        scratch_shapes=[pltpu.VMEM((M, tn), F32)],
        compiler_params=_cp("parallel", "arbitrary"),
    )(a, b)


def _rope_tables(positions):
    inv_freq = ROPE_THETA ** (-jnp.arange(ROPE_HALF, dtype=F32) / ROPE_HALF)
    ang = positions.astype(F32)[:, None] * inv_freq
    cos, sin = jnp.cos(ang), jnp.sin(ang)
    S = positions.shape[0]
    one, zero = jnp.ones((S, HEAD_DIM - 2 * ROPE_HALF), F32), jnp.zeros((S, HEAD_DIM - 2 * ROPE_HALF), F32)
    z8 = jnp.zeros((S, ROPE_HALF), F32)
    rc = jnp.concatenate([cos, cos, one], axis=1)
    ra = jnp.concatenate([-sin, z8, zero], axis=1)
    rb = jnp.concatenate([z8, sin, zero], axis=1)
    return tuple(jnp.tile(t, (1, 2)) for t in (rc, ra, rb))


def _in_operands(w_in):
    w_in = w_in.astype(BF16)
    wm = jnp.concatenate([w_in[:, :4 * D_BRANCH], w_in[:, 4 * D_BRANCH + N_HEADS:]], axis=1)
    wf = jnp.pad(w_in[:, 4 * D_BRANCH:4 * D_BRANCH + N_HEADS], ((0, 0), (0, 128 - N_HEADS)))
    return dict(wm=wm, wf=wf)


def _layer_weights(w_in, w_out, w_ple, w_pg):
    return dict(_in_operands(w_in), w_out=w_out.astype(BF16), w_ple=w_ple.astype(BF16), w_pg=w_pg.astype(BF16))


def _row(v, width=128):
    v = v.reshape(1, -1).astype(F32)
    return jnp.pad(v, ((0, 0), (0, width - v.shape[1])))


def _layer_fwd(h0, p, rope, tabs, w, norm_g, b_f, qk_g, ple_g, rider=None, late=None, target=None):
    g1 = norm_g.reshape(1, D_MODEL)
    g2 = ple_g.reshape(1, D_MODEL)
    qkg = jnp.tile(qk_g, (1, 2))
    bf = _row(b_f)
    zm, zf, u = _inproj_fwd(h0, g1, w["wm"], w["wf"])
    qa, ka, va, qb, kb, vb = _prep_fwd(zm, zf, bf, tabs["tril"], qkg, *rope)
    oa, lse_a, *arrivals = _attn_fwd(qa, ka, va, tabs["fox"], True, "fox_fwd", rider)
    if late is not None:
        w = {**w, **late(arrivals)}
    ob, lse_b = _attn_fwd(qb, kb, vb, tabs["dil"], False, "dil_fwd")
    y, h1, h2, u2, e, gate, *loss = _mid_fwd(oa, ob, zm, h0, p, w["w_out"], w["w_pg"], w["w_ple"], g2, target)
    saved = dict(h0=h0, p=p, zm=zm, zf=zf, u=u, qa=qa, ka=ka, va=va, qb=qb, kb=kb, vb=vb, oa=oa, ob=ob,
                 lse_a=lse_a, lse_b=lse_b, y=y, h1=h1, u2=u2, e=e, gate=gate, g1=g1, g2=g2, qkg=qkg, bf=bf, w=w)
    return (h2, saved, arrivals, *loss)


def _layer_bwd(dh2, sv, rope, tabs, make_rider=None, make_last_rider=None):
    S = dh2.shape[0]
    nq = S // ATT_T
    w = sv["w"]
    rows = lambda a: a.reshape(N_HEADS, nq, 1, ATT_T)
    (dh1, dh1b, de, dpre, doa, dob, dga, dgb, dd, dg2) = _mid_bwd(
        dh2, sv["h1"], sv["e"], sv["gate"], sv["g2"], w["w_pg"], w["w_out"], sv["oa"], sv["ob"], sv["zm"])
    dda, ddb = dd[:N_HEADS], dd[N_HEADS:]
    early = dict(w_out=_wgrad(sv["y"], dh1b, "wgrad_out"), w_ple=_wgrad(sv["p"][0], de, "wgrad_ple", sv["p"][1]),
                 w_ple_gate=_wgrad(sv["u2"], dpre, "wgrad_gate"))
    rider = None if make_rider is None else make_rider(early)
    dqa, dka, dva, *arrivals = _attn_bwd(sv["qa"], sv["ka"], sv["va"], doa, sv["lse_a"], rows(dda), tabs["fox"],
                                         True, "fox_bwd", rider, narrow=(False, False, True))
    dqb, dkb, dvb = _attn_bwd(sv["qb"], sv["kb"], sv["vb"], dob, sv["lse_b"], rows(ddb), tabs["dil"], False,
                              "dil_bwd", narrow=(True, True, True))
    dzm, dzf, dqkg, dbf = _prep_bwd(dqa, dka, dva, dqb, dkb, dvb, sv["zm"], sv["qkg"], *rope, dga, dgb,
                                    sv["zf"], sv["bf"], tabs["triu"])
    dwm = _wgrad(sv["u"], dzm, "wgrad_in")
    dwf = _wgrad(sv["u"], dzf, "wgrad_f")
    dw_in = jnp.concatenate([dwm[:, :4 * D_BRANCH], dwf[:, :N_HEADS], dwm[:, 4 * D_BRANCH:]], axis=1)
    last_rider = None if make_last_rider is None else make_last_rider(dw_in)
    dh0, dg1, *last_arrivals = _inproj_bwd(dzm, dzf, w["wm"], w["wf"], sv["h0"], dh1, sv["g1"], last_rider)
    grads = dict(norm_g=dg1.reshape(D_MODEL), w_in=dw_in, b_f=dbf[0, :N_HEADS],
                 qk_norm_g=dqkg[:, :HEAD_DIM] + dqkg[:, HEAD_DIM:], ple_norm_g=dg2.reshape(D_MODEL), **early)
    return dh0, grads, arrivals + last_arrivals


def _tables():
    T = TOK_T
    r = lax.broadcasted_iota(jnp.int32, (T, T), 0)
    c = lax.broadcasted_iota(jnp.int32, (T, T), 1)
    return dict(fox=_bias_tables(True), dil=_bias_tables(False),
                tril=(c <= r).astype(BF16), triu=(c >= r).astype(BF16))


def _local_step(x, p, positions, target, layers, small):
    rope = _rope_tables(positions)
    tabs = _tables()
    ws = [_layer_weights(*lw) for lw in layers]
    h = x
    saved = []
    for li, (w, lp, sm) in enumerate(zip(ws, p, small)):
        h, sv, _, *loss = _layer_fwd(h, (lp, ()), rope, tabs, w, *sm, target=target if li == len(ws) - 1 else None)
        saved.append(sv)
    dh, (loss,) = h, loss
    grads = [None] * len(ws)
    for li in reversed(range(len(ws))):
        dh, grads[li], _ = _layer_bwd(dh, saved[li], rope, tabs)
    return loss[0, 0], dh, grads


def _peers():
    x, y, c = lax.axis_index("x"), lax.axis_index("y"), lax.axis_index("c")
    me = 4 * x + 2 * y + c
    flip = lambda v, bit: 1 - v if bit else v
    return me, [(flip(x, k & 4), flip(y, k & 2), flip(c, k & 1)) for k in range(1, N_DEV)]


def _sel(ref, kind, d):
    if kind == "whole":
        return ref
    if kind == "slot":
        return ref.at[d]
    block = pl.ds(pl.multiple_of(d * 128, 128), 128)
    return ref.at[block, :] if kind == "rows" else ref.at[:, block]


class _Pushes:
    def __init__(self, arrays, src_kinds, dst_kinds, out_shapes):
        self.arrays, self.n = list(arrays), len(arrays)
        self.src_kinds, self.dst_kinds = src_kinds, dst_kinds
        self.out_shapes = [jax.ShapeDtypeStruct(s, a.dtype) for s, a in zip(out_shapes, arrays)]
        hbm = pl.BlockSpec(memory_space=pltpu.HBM)
        self.in_specs, self.out_specs = [hbm] * self.n, [hbm] * self.n
        self.scratch_shapes = [pltpu.SemaphoreType.DMA((N_DEV - 1, self.n)),
                               pltpu.SemaphoreType.DMA((N_DEV - 1, self.n)), pltpu.SemaphoreType.DMA((self.n,))]

    def _copies(self, ins, outs, sems):
        send_sems, recv_sems, local_sems = sems
        me, peers = _peers()
        src = lambda a, d: _sel(ins[a], self.src_kinds[a], d)
        dst = lambda a: _sel(outs[a], self.dst_kinds[a], me)
        local = [pltpu.make_async_copy(src(a, me), dst(a), local_sems.at[a]) for a in range(self.n)]
        remote = [pltpu.make_async_remote_copy(
            src_ref=src(a, 4 * px + 2 * py + pc), dst_ref=dst(a), send_sem=send_sems.at[k, a],
            recv_sem=recv_sems.at[k, a], device_id=(px, py, pc), device_id_type=pl.DeviceIdType.MESH)
            for k, (px, py, pc) in enumerate(peers) for a in range(self.n)]
        return local + remote

    def start(self, ins, outs, sems):
        for cp in self._copies(ins, outs, sems):
            cp.start()

    def wait(self, ins, outs, sems):
        for cp in self._copies(ins, outs, sems):
            cp.wait()


def _exchange(name, pushes):
    n = pushes.n

    def body(*refs):
        pushes.start(refs[:n], refs[n:2 * n], refs[2 * n:])
        pushes.wait(refs[:n], refs[n:2 * n], refs[2 * n:])

    return pl.pallas_call(body, name=name, in_specs=pushes.in_specs, out_specs=pushes.out_specs,
                          out_shape=pushes.out_shapes, scratch_shapes=pushes.scratch_shapes)(*pushes.arrays)


def _gather_two_level(shard, name):
    def body(x_ref, out_ref, send_sems, recv_sems, local_sem):
        x, y, c = lax.axis_index("x"), lax.axis_index("y"), lax.axis_index("c")
        me, sibling = (x, y, c), (x, y, 1 - c)
        chips = [(1 - x, y), (x, 1 - y), (1 - x, 1 - y)]
        slot = lambda px, py, pc: out_ref.at[4 * px + 2 * py + pc]

        def copy(k, block, to, src=None):
            return pltpu.make_async_remote_copy(
                src_ref=slot(*block) if src is None else src, dst_ref=slot(*block), send_sem=send_sems.at[k],
                recv_sem=recv_sems.at[k], device_id=to, device_id_type=pl.DeviceIdType.MESH)

        mine = pltpu.make_async_copy(x_ref, slot(*me), local_sem)
        mine.start()
        first = [copy(0, me, sibling, src=x_ref)] + [copy(1 + j, me, (*chip, c), src=x_ref)
                                                     for j, chip in enumerate(chips)]
        for cp in first:
            cp.start()
        passed = [copy(4 + j, (*chip, c), sibling) for j, chip in enumerate(chips)]
        for j, chip in enumerate(chips):
            copy(1 + j, (*chip, c), me).wait_recv()
            passed[j].start()
        copy(0, sibling, me).wait_recv()
        for j, chip in enumerate(chips):
            copy(4 + j, (*chip, 1 - c), me).wait_recv()
        for cp in first + passed:
            cp.wait_send()
        mine.wait()

    hbm = pl.BlockSpec(memory_space=pltpu.HBM)
    return pl.pallas_call(
        body, name=name, in_specs=[hbm], out_specs=hbm,
        out_shape=jax.ShapeDtypeStruct((N_DEV,) + shard.shape, shard.dtype),
        scratch_shapes=[pltpu.SemaphoreType.DMA((N_DEV - 1,)), pltpu.SemaphoreType.DMA((N_DEV - 1,)),
                        pltpu.SemaphoreType.DMA],
    )(shard)


def _gather_pushes(shards, kinds):
    full = {"slot": lambda s: (N_DEV,) + s, "rows": lambda s: (N_DEV * s[0], s[1]),
            "cols": lambda s: (s[0], N_DEV * s[1])}
    return _Pushes(shards, ["whole"] * len(shards), kinds, [full[k](a.shape) for a, k in zip(shards, kinds)])


def _scatter_pushes(partials, kinds):
    part = {"slot": lambda s: s[1:], "rows": lambda s: (128, s[1]), "cols": lambda s: (s[0], 128),
            "whole": lambda s: s}
    return _Pushes(partials, kinds, ["slot"] * len(partials),
                   [(N_DEV,) + part[k](a.shape) for a, k in zip(partials, kinds)])


def _adamw(name, parts, w, m, v, rows):
    L, R, C = w.shape

    def body(p_ref, w_ref, m_ref, v_ref, g_ref, d_ref, nm_ref, nv_ref):
        g = p_ref[0, 0].astype(F32)
        for s in range(1, N_DEV):
            g = g + p_ref[s, 0].astype(F32)
        g_ref[0] = g
        nm = ADAM_B1 * m_ref[0] + (1.0 - ADAM_B1) * g
        nv = ADAM_B2 * v_ref[0] + (1.0 - ADAM_B2) * (g * g)
        nm_ref[0] = nm
        nv_ref[0] = nv
        m_hat = nm / (1.0 - ADAM_B1 ** ADAM_STEP)
        v_hat = nv / (1.0 - ADAM_B2 ** ADAM_STEP)
        d_ref[0] = -ADAM_LR * (m_hat / (jnp.sqrt(v_hat) + ADAM_EPS) + ADAM_WD * w_ref[0])

    blk = pl.BlockSpec((1, rows, C), lambda l, i: (l, i, 0))
    shp = jax.ShapeDtypeStruct((L, R, C), F32)
    return pl.pallas_call(
        body, name=name, grid=(L, R // rows),
        in_specs=[pl.BlockSpec((N_DEV, 1, rows, C), lambda l, i: (0, l, i, 0)), blk, blk, blk],
        out_specs=[blk] * 4, out_shape=[shp] * 4,
        compiler_params=_cp("parallel", "parallel"),
    )(parts, w, m, v)


SMALL_ROWS = 40
LOSS_ROW = 37


def _pack_small(norm_g, ple_g, qk_g, b_f, last_row):
    rows = lambda a: a.astype(F32).reshape(-1, 128)
    flat = jnp.concatenate([rows(norm_g), rows(ple_g), rows(qk_g), _row(b_f.reshape(-1)), last_row], axis=0)
    return jnp.pad(flat, ((0, SMALL_ROWS - flat.shape[0]), (0, 0)))


def _unpack_small(flat):
    return (flat[0:16].reshape(2, D_MODEL), flat[16:32].reshape(2, D_MODEL), flat[32:36].reshape(2, 4, HEAD_DIM),
            flat[36, :2 * N_HEADS].reshape(2, N_HEADS))


def kernel(x, p, positions, norm_g, w_in, b_f, qk_norm_g, w_out, w_ple, ple_norm_g, w_ple_gate, loss_target, m_norm_g, m_w_in, m_b_f, m_qk_norm_g, m_w_out, m_w_ple, m_ple_norm_g, m_w_ple_gate, v_norm_g, v_w_in, v_b_f, v_qk_norm_g, v_w_out, v_w_ple, v_ple_norm_g, v_w_ple_gate):
    bf16 = lambda a: a.astype(BF16)
    rows_in = W_IN_ROWS // 2
    flat_in = lambda a: bf16(a).reshape(rows_in, 128)
    full_in = lambda g: g.reshape(N_DEV, D_MODEL, W_IN_SHARD).transpose(1, 0, 2).reshape(D_MODEL, N_IN)
    small = [(norm_g[l], b_f[l], qk_norm_g[l], ple_norm_g[l]) for l in range(2)]
    rope = _rope_tables(positions[0])
    tabs = _tables()

    g_in0 = _gather_two_level(flat_in(w_in[0]), "gather_first")
    rest = _gather_pushes([flat_in(w_in[1])] + [bf16(a[l]) for l in range(2) for a in (w_out, w_ple, w_ple_gate)],
                          ["slot"] + ["rows", "cols", "rows"] * 2)
    late = lambda got: dict(w_out=got[1], w_ple=got[2], w_pg=got[3])
    h, sv0, got = _layer_fwd(x[0], (p, (0, 0)), rope, tabs, _in_operands(full_in(g_in0)), *small[0], rest, late)
    w1 = dict(_in_operands(full_in(got[0])), w_out=got[4], w_ple=got[5], w_pg=got[6])
    dh, sv1, _, loss = _layer_fwd(h, (p, (1, 0)), rope, tabs, w1, *small[1], target=loss_target[0])
    dh, gr1, _ = _layer_bwd(dh, sv1, rope, tabs)

    by_dest = lambda d: d.reshape(D_MODEL, N_DEV, W_IN_SHARD).transpose(1, 0, 2).reshape(N_DEV, rows_in, 128)
    big = ("w_out", "w_ple", "w_ple_gate")
    riding = lambda early: _scatter_pushes([by_dest(gr1["w_in"])] + [gr1[n] for n in big] + [early[n] for n in big],
                                           ["slot"] + ["rows", "cols", "rows"] * 2)
    riding_last = lambda dw_in: _scatter_pushes([by_dest(dw_in)], ["slot"])
    dx, gr0, (r_in1, *r_big, r_in0) = _layer_bwd(dh, sv0, rope, tabs, riding, riding_last)
    grads = (gr0, gr1)
    stack = lambda name: jnp.stack([gl[name] for gl in grads], axis=0)
    small_part = _pack_small(stack("norm_g"), stack("ple_norm_g"), stack("qk_norm_g"), stack("b_f"),
                             _row(loss[0, 0].reshape(1)))
    (r_small,) = _exchange("exchange_small", _scatter_pushes([small_part], ["whole"]))
    r_in = jnp.concatenate([r_in0, r_in1], axis=1)
    r_out, r_ple, r_pg = (jnp.stack([r_big[3 + k], r_big[k]], axis=1) for k in range(3))

    zero_row = jnp.zeros((1, 128), F32)
    small_of = lambda ng, pg, qk, bf: _pack_small(ng, pg, qk, bf, zero_row)[None]
    flat = lambda a: a.reshape(1, W_IN_ROWS, 128)
    outs = dict(
        w_in=[o.reshape(w_in.shape) for o in
              _adamw("adamw_in", r_in[:, None], flat(w_in), flat(m_w_in), flat(v_w_in), W_IN_TILE)],
        w_out=_adamw("adamw_out", r_out, w_out, m_w_out, v_w_out, 128),
        w_ple=_adamw("adamw_ple", r_ple, w_ple, m_w_ple, v_w_ple, 256),
        w_pg=_adamw("adamw_gate", r_pg, w_ple_gate, m_w_ple_gate, v_w_ple_gate, 128),
        small=_adamw("adamw_small", r_small[:, None], small_of(norm_g, ple_norm_g, qk_norm_g, b_f),
                     small_of(m_norm_g, m_ple_norm_g, m_qk_norm_g, m_b_f),
                     small_of(v_norm_g, v_ple_norm_g, v_qk_norm_g, v_b_f), SMALL_ROWS))
    leaves = []
    for kind in range(4):
        ng, pg, qk, bf = _unpack_small(outs["small"][kind][0])
        leaves += [ng, outs["w_in"][kind], bf, qk, outs["w_out"][kind], outs["w_ple"][kind], pg, outs["w_pg"][kind]]
    return (outs["small"][0][0, LOSS_ROW, 0], dx[None], *leaves)
```

```python
import functools

import jax
import jax.numpy as jnp
from jax import lax
from jax.experimental import pallas as pl
from jax.experimental.pallas import tpu as pltpu

F32 = jnp.float32
BF16 = jnp.bfloat16

D_MODEL = 1024
HEAD_DIM = 64
N_HEADS = 8
HEAD_PAD = 128
D_BRANCH = N_HEADS * HEAD_DIM
N_MAIN = 8 * D_BRANCH
N_IN = N_MAIN + N_HEADS
PLE_DIM = 256
ROPE_THETA = 500000.0
ROPE_HALF = 8
EPS = 1e-6
NEG = -1e30
SCALE = HEAD_DIM ** -0.5
LOG2E = 1.4426950408889634
LN2 = 0.6931471805599453
DILATED_PATTERNS = ((128, 1), (512, 4), (2048, 16))
N_DEV = 8
W_IN_SHARD = N_IN // N_DEV
W_IN_ROWS = 2 * D_MODEL * W_IN_SHARD // 128
W_IN_TILE = W_IN_ROWS // 19

ADAM_LR = 0.001
ADAM_B1 = 0.9
ADAM_B2 = 0.999
ADAM_EPS = 1e-08
ADAM_WD = 0.01
ADAM_STEP = 10

ATT_T = 512
ATT_FWD_HEADS = 8
ATT_BWD_HEADS = 2
ATT_BWD_HEADS_CAUSAL = 4
ATT_CHUNK = 32
TOK_T = 256
VMEM_LIMIT = 60 * 1024 * 1024


def _slab_spec(lead, block, index):
    return pl.BlockSpec((None,) * len(lead) + block, lambda *g: (*lead, *index(*g)))


def _cp(*sem):
    return pltpu.CompilerParams(dimension_semantics=sem, vmem_limit_bytes=VMEM_LIMIT)


def _sigmoid(x):
    return 1.0 / (1.0 + jnp.exp(-x))


def _split3(x):
    hi = x.astype(BF16)
    r1 = x - hi.astype(F32)
    mid = r1.astype(BF16)
    lo = (r1 - mid.astype(F32)).astype(BF16)
    return hi, mid, lo


def _dot(a, b):
    return jnp.dot(a, b, preferred_element_type=F32)


def _dot_nt(a, b):
    return lax.dot_general(a, b, (((1,), (1,)), ((), ())), preferred_element_type=F32)


def _dot_tn(a, b):
    return lax.dot_general(a, b, (((0,), (0,)), ((), ())), preferred_element_type=F32)


def _inproj_fwd(h, g, wm, wf):
    S = h.shape[0]

    def body(h_ref, g_ref, wm_ref, wf_ref, zm_ref, zf_ref, u_ref):
        x = h_ref[...]
        r = lax.rsqrt(jnp.mean(x * x, axis=-1, keepdims=True) + EPS)
        u = (x * r * g_ref[...]).astype(BF16)
        u_ref[...] = u
        zm_ref[...] = _dot(u, wm_ref[...]).astype(BF16)
        zf_ref[...] = _dot(u, wf_ref[...])

    return pl.pallas_call(
        body, name="inproj_fwd", grid=(S // TOK_T,),
        in_specs=[pl.BlockSpec((TOK_T, D_MODEL), lambda i: (i, 0)),
                  pl.BlockSpec((1, D_MODEL), lambda i: (0, 0)),
                  pl.BlockSpec((D_MODEL, N_MAIN), lambda i: (0, 0)),
                  pl.BlockSpec((D_MODEL, 128), lambda i: (0, 0))],
        out_specs=[pl.BlockSpec((TOK_T, N_MAIN), lambda i: (i, 0)),
                   pl.BlockSpec((TOK_T, 128), lambda i: (i, 0)),
                   pl.BlockSpec((TOK_T, D_MODEL), lambda i: (i, 0))],
        out_shape=[jax.ShapeDtypeStruct((S, N_MAIN), BF16), jax.ShapeDtypeStruct((S, 128), F32),
                   jax.ShapeDtypeStruct((S, D_MODEL), BF16)],
        compiler_params=_cp("parallel"),
    )(h, g, wm, wf)


def _log_sigmoid(x):
    return jnp.minimum(x, 0.0) - jnp.log(1.0 + jnp.exp(-jnp.abs(x)))


def _same_head():
    r = lax.broadcasted_iota(jnp.int32, (HEAD_PAD, HEAD_PAD), 0) // HEAD_DIM
    c = lax.broadcasted_iota(jnp.int32, (HEAD_PAD, HEAD_PAD), 1) // HEAD_DIM
    return (r == c).astype(BF16)


def _pair_mean(x, same_head):
    hi = x.astype(BF16)
    lo = (x - hi.astype(F32)).astype(BF16)
    return (_dot(hi, same_head) + _dot(lo, same_head)) * (1.0 / HEAD_DIM)


def _pair_rsqrt(x, same_head):
    return lax.rsqrt(_pair_mean(x * x, same_head) + EPS)


def _prep_fwd(zm, zf, bf, tril, qkg, rope_c, rope_a, rope_b):
    S = zm.shape[0]
    shp = jax.ShapeDtypeStruct((N_HEADS, S, HEAD_PAD), BF16)

    def body(z_ref, zf_ref, b_ref, tri_ref, g_ref, rc_ref, ra_ref, rb_ref,
             qa_ref, ka_ref, va_ref, qb_ref, kb_ref, vb_ref, carry):
        @pl.when(pl.program_id(0) == 0)
        def _():
            carry[...] = jnp.zeros_like(carry)

        tri = tri_ref[...]
        cs = sum(_dot(tri, part) for part in _split3(_log_sigmoid(zf_ref[...] + b_ref[...]))) + carry[...]
        carry[...] = cs[TOK_T - 1:TOK_T, :]
        lane = lax.broadcasted_iota(jnp.int32, (TOK_T, HEAD_PAD), 1)
        lo_half = lane < HEAD_DIM
        aug = (lane >= HEAD_DIM) & (lane < HEAD_DIM + 3)
        q_pad = jnp.where(aug, -1.0, 0.0)
        rc, ra, rb = rc_ref[...], ra_ref[...], rb_ref[...]
        same_head = _same_head()

        def norm(col, gi):
            x = z_ref[:, col:col + HEAD_PAD].astype(F32)
            return x * _pair_rsqrt(x, same_head) * g_ref[gi:gi + 1, :]

        def rope(y):
            return y * rc + pltpu.roll(y, HEAD_PAD - ROPE_HALF, 1) * ra + pltpu.roll(y, ROPE_HALF, 1) * rb

        def put(ref, pi, y, pad_even, pad_odd):
            ref[2 * pi] = jnp.where(lo_half, y, pad_even).astype(BF16)
            ref[2 * pi + 1] = jnp.where(lo_half, pltpu.roll(y, HEAD_DIM, 1), pad_odd).astype(BF16)

        def k_pad(h):
            ch = cs[:, h:h + 1] * LOG2E
            hi = ch.astype(BF16).astype(F32)
            mid = (ch - hi).astype(BF16).astype(F32)
            lo = ch - hi - mid
            ones = jnp.where(lane == HEAD_DIM + 3, 1.0, 0.0)
            return jnp.where(lane == HEAD_DIM, hi, jnp.where(lane == HEAD_DIM + 1, mid,
                                                              jnp.where(lane == HEAD_DIM + 2, lo, ones)))

        for pi in range(N_HEADS // 2):
            col = HEAD_PAD * pi
            put(qa_ref, pi, norm(col, 0) * (SCALE * LOG2E), q_pad, q_pad)
            put(ka_ref, pi, norm(D_BRANCH + col, 1), k_pad(2 * pi), k_pad(2 * pi + 1))
            put(va_ref, pi, z_ref[:, 2 * D_BRANCH + col:2 * D_BRANCH + col + HEAD_PAD].astype(F32), 0.0, 0.0)
            put(qb_ref, pi, rope(norm(4 * D_BRANCH + col, 2)) * (SCALE * LOG2E), 0.0, 0.0)
            put(kb_ref, pi, rope(norm(5 * D_BRANCH + col, 3)), 0.0, 0.0)
            put(vb_ref, pi, z_ref[:, 6 * D_BRANCH + col:6 * D_BRANCH + col + HEAD_PAD].astype(F32), 0.0, 0.0)

    tok = lambda w: pl.BlockSpec((TOK_T, w), lambda i: (i, 0))
    head = pl.BlockSpec((N_HEADS, TOK_T, HEAD_PAD), lambda i: (0, i, 0))
    return pl.pallas_call(
        body, name="prep_fwd", grid=(S // TOK_T,),
        in_specs=[tok(N_MAIN), tok(128), pl.BlockSpec((1, 128), lambda i: (0, 0)),
                  pl.BlockSpec((TOK_T, TOK_T), lambda i: (0, 0)), pl.BlockSpec((4, 128), lambda i: (0, 0)),
                  tok(128), tok(128), tok(128)],
        out_specs=[head] * 6, out_shape=[shp] * 6,
        scratch_shapes=[pltpu.VMEM((1, 128), F32)],
        compiler_params=_cp("arbitrary"),
    )(zm, zf, bf, tril, qkg, rope_c, rope_a, rope_b)


def _pair(ref, pi, lo_half):
    return jnp.where(lo_half, ref[2 * pi].astype(F32), pltpu.roll(ref[2 * pi + 1].astype(F32), HEAD_DIM, 1))


def _mid_fwd(oa, ob, zm, h0, p, w_out, w_pg, w_ple, g2, target=None):
    S = h0.shape[0]
    p, p_lead = p

    def body(oa_ref, ob_ref, ga_ref, gb_ref, h0_ref, p_ref, wo_ref, wg_ref, wp_ref, g2_ref, *rest):
        t_ref, rest = (rest[0], rest[1:]) if target is not None else (None, rest)
        y_ref, h1_ref, h2_ref, u2_ref, e_ref, gate_ref, *loss_ref = rest
        parts = []
        for o_ref, g_ref in ((oa_ref, ga_ref), (ob_ref, gb_ref)):
            for pi in range(N_HEADS // 2):
                g = g_ref[:, HEAD_PAD * pi:HEAD_PAD * (pi + 1)].astype(F32)
                parts.append((o_ref[pi] * (g * _sigmoid(g))).astype(BF16))
        y = jnp.concatenate(parts, axis=1)
        y_ref[...] = y
        h1 = h0_ref[...] + _dot(y, wo_ref[...])
        h1_ref[...] = h1
        r = lax.rsqrt(jnp.mean(h1 * h1, axis=-1, keepdims=True) + EPS)
        u2 = (h1 * r * g2_ref[...]).astype(BF16)
        u2_ref[...] = u2
        gate = _sigmoid(_dot(u2, wg_ref[...]))
        e = _dot(p_ref[...].astype(BF16), wp_ref[...])
        e_ref[...] = e.astype(BF16)
        gate_ref[...] = gate.astype(BF16)
        h2 = h1 + e * gate
        if target is None:
            h2_ref[...] = h2
        else:
            @pl.when(pl.program_id(0) == 0)
            def _():
                loss_ref[0][...] = jnp.zeros_like(loss_ref[0])

            err = h2 - t_ref[...]
            h2_ref[...] = err * (1.0 / D_MODEL)
            part = jnp.sum(jnp.sum(err * err, axis=1, keepdims=True), axis=0, keepdims=True)
            loss_ref[0][...] += part * (0.5 / D_MODEL)

    tok = lambda w: pl.BlockSpec((TOK_T, w), lambda i: (i, 0))
    head = pl.BlockSpec((N_HEADS // 2, TOK_T, HEAD_PAD), lambda i: (0, i, 0))
    full = lambda a, b: pl.BlockSpec((a, b), lambda i: (0, 0))
    act = lambda dt: jax.ShapeDtypeStruct((S, D_MODEL), dt)
    fused = target is not None
    return pl.pallas_call(
        body, name="mid_fwd_loss" if fused else "mid_fwd", grid=(S // TOK_T,),
        in_specs=[head, head,
                  pl.BlockSpec((TOK_T, D_BRANCH), lambda i: (i, 3)), pl.BlockSpec((TOK_T, D_BRANCH), lambda i: (i, 7)),
                  tok(D_MODEL), _slab_spec(p_lead, (TOK_T, PLE_DIM), lambda i: (i, 0)), full(D_MODEL, D_MODEL),
                  full(D_MODEL, D_MODEL), full(PLE_DIM, D_MODEL), full(1, D_MODEL)] + [tok(D_MODEL)] * fused,
        out_specs=[tok(D_MODEL)] * 6 + [full(8, 128)] * fused,
        out_shape=[act(BF16), act(F32), act(F32), act(BF16), act(BF16), act(BF16)]
        + [jax.ShapeDtypeStruct((8, 128), F32)] * fused,
        compiler_params=_cp("arbitrary" if fused else "parallel"),
    )(oa, ob, zm, zm, h0, p, w_out, w_pg, w_ple, g2, *([target] * fused))


def _bias_tables(full_range):
    T = ATT_T
    nb = 1 if full_range else DILATED_PATTERNS[-1][0] // T + 1
    r = lax.broadcasted_iota(jnp.int32, (nb, T, T), 2)
    c = lax.broadcasted_iota(jnp.int32, (nb, T, T), 1)
    b = lax.broadcasted_iota(jnp.int32, (nb, T, T), 0)
    delta = T * b + r - c
    if full_range:
        bias = jnp.where(delta >= 0, 0.0, NEG).astype(F32)
    else:
        mult = jnp.zeros((nb, T, T), F32)
        for window, dil in DILATED_PATTERNS:
            ok = (delta >= 0) & (delta <= window) & (delta % dil == 0)
            mult = mult + ok.astype(F32)
        bias = jnp.where(mult > 0, jnp.log2(jnp.maximum(mult, 1.0)), NEG).astype(F32)
    return bias


def _call_with_rider(body, name, grid, rider, in_specs, out_specs, out_shape, scratch_shapes, operands,
                     semantics=("parallel", "arbitrary")):
    if rider is None:
        return pl.pallas_call(body, name=name, grid=grid, in_specs=in_specs, out_specs=out_specs,
                              out_shape=out_shape, scratch_shapes=scratch_shapes,
                              compiler_params=_cp(*semantics))(*operands)
    n, n_in, n_out = rider.n, len(in_specs), len(out_specs)

    def wrapped(*refs):
        ins, r_ins = refs[:n_in], refs[n_in:n_in + n]
        outs, r_outs = refs[n_in + n:n_in + n + n_out], refs[n_in + n + n_out:n_in + 2 * n + n_out]
        scratch, sems = refs[n_in + 2 * n + n_out:-3], refs[-3:]
        step = [pl.program_id(a) for a in range(len(grid))]

        @pl.when(functools.reduce(jnp.logical_and, [s == 0 for s in step]))
        def _():
            rider.start(r_ins, r_outs, sems)

        body(*ins, *outs, *scratch)

        @pl.when(functools.reduce(jnp.logical_and, [s == g - 1 for s, g in zip(step, grid)]))
        def _():
            rider.wait(r_ins, r_outs, sems)

    return pl.pallas_call(
        wrapped, name=name, grid=grid, in_specs=list(in_specs) + rider.in_specs,
        out_specs=list(out_specs) + rider.out_specs, out_shape=list(out_shape) + rider.out_shapes,
        scratch_shapes=list(scratch_shapes) + rider.scratch_shapes,
        compiler_params=_cp(*["arbitrary"] * len(grid)))(*operands, *rider.arrays)


def _attn_fwd(q, k, v, table_t, full_range, name, rider=None):
    H, S, _ = q.shape
    T = ATT_T
    nb = table_t.shape[0]
    HB = ATT_FWD_HEADS
    KC = ATT_CHUNK
    chunks = [slice(c, c + KC) for c in range(0, T, KC)]
    fold = lambda x, op: functools.reduce(op, [x[r:r + 8] for r in range(0, KC, 8)])

    def body(q_ref, k_ref, v_ref, tab_ref, o_ref, lse_ref, *scratch):
        st_refs, pt_refs, acc_refs = scratch[:HB], scratch[HB:2 * HB], scratch[2 * HB:]
        i = pl.program_id(1)
        rows = lambda j: pl.ds(pl.multiple_of(j * T, T), T)

        def scores(hh, j):
            st_refs[hh][...] = _dot_nt(k_ref[hh, rows(j), :], q_ref[hh])

        def block(j, b, nxt, stats):
            out = []
            for hh, (m, l) in enumerate(stats):
                st_ref, pt_ref, acc_ref = st_refs[hh], pt_refs[hh], acc_refs[hh]
                mx = None
                for ch in chunks:
                    x = st_ref[ch, :]
                    if b is not None:
                        x = x + tab_ref[b, ch, :]
                        st_ref[ch, :] = x
                    x = fold(x, jnp.maximum)
                    mx = x if mx is None else jnp.maximum(mx, x)
                m_new = jnp.maximum(m, jnp.max(mx, axis=0, keepdims=True))
                alpha = jnp.exp2(m - m_new)
                ls = None
                for ch in chunks:
                    pc = jnp.exp2(st_ref[ch, :] - m_new)
                    pt_ref[ch, :] = pc.astype(BF16)
                    pc = fold(pc, jnp.add)
                    ls = pc if ls is None else ls + pc
                if nxt is not None:
                    scores(hh, nxt)
                acc_ref[...] = alpha * acc_ref[...] + _dot_tn(v_ref[hh, rows(j), :], pt_ref[...])
                out.append((m_new, alpha * l + jnp.sum(ls, axis=0, keepdims=True)))
            return tuple(out)

        lo = 0 if full_range else jnp.maximum(i - (nb - 1), 0)
        for hh in range(HB):
            acc_refs[hh][...] = jnp.zeros_like(acc_refs[hh])
            scores(hh, lo)
        stats = lax.fori_loop(lo, i, lambda j, st: block(j, None if full_range else i - j, j + 1, st),
                              ((jnp.full((1, T), NEG, F32), jnp.zeros((1, T), F32)),) * HB)
        stats = block(i, 0, None, stats)
        o_t = [acc_refs[hh][...] * (1.0 / l) for hh, (m, l) in enumerate(stats)]
        for hh, (m, l) in enumerate(stats):
            lse_ref[hh, 0] = m + jnp.log2(l)
        for hp in range(HB // 2):
            o_ref[hp] = jnp.concatenate([o_t[2 * hp][:HEAD_DIM], o_t[2 * hp + 1][:HEAD_DIM]], axis=0).T

    return _call_with_rider(
        body, name, (H // HB, S // T), rider,
        in_specs=[pl.BlockSpec((HB, T, HEAD_PAD), lambda h, i: (h, i, 0)),
                  pl.BlockSpec((HB, S, HEAD_PAD), lambda h, i: (h, 0, 0), pipeline_mode=pl.Buffered(1)),
                  pl.BlockSpec((HB, S, HEAD_PAD), lambda h, i: (h, 0, 0), pipeline_mode=pl.Buffered(1)),
                  pl.BlockSpec((nb, T, T), lambda h, i: (0, 0, 0), pipeline_mode=pl.Buffered(1))],
        out_specs=[pl.BlockSpec((HB // 2, T, HEAD_PAD), lambda h, i: (h, i, 0)),
                   pl.BlockSpec((HB, 1, 1, T), lambda h, i: (h, i, 0, 0))],
        out_shape=[jax.ShapeDtypeStruct((H // 2, S, HEAD_PAD), F32), jax.ShapeDtypeStruct((H, S // T, 1, T), F32)],
        scratch_shapes=([pltpu.VMEM((T, T), F32)] * HB + [pltpu.VMEM((T, T), BF16)] * HB
                        + [pltpu.VMEM((HEAD_PAD, T), F32)] * HB),
        operands=(q, k, v, table_t))


def _attn_bwd(q, k, v, do, lse, dd, table_t, full_range, name, rider=None, narrow=(False, False, False)):
    H, S, _ = q.shape
    T = ATT_T
    nq = S // T
    nb = table_t.shape[0]
    HB = ATT_BWD_HEADS_CAUSAL if full_range else ATT_BWD_HEADS
    KC = ATT_CHUNK
    chunks = [slice(c, c + KC) for c in range(0, T, KC)]

    def body(q_ref, do_ref, lse_ref, dd_ref, k_ref, v_ref, tab_ref, dq_hbm, dk_ref, dv_ref, *scratch):
        st_refs, dpt_refs, pt_refs, dst_refs = (scratch[n * HB:(n + 1) * HB] for n in range(4))
        dq_ref, dk_acc, dv_acc, *dq_cast, dq_sem = scratch[4 * HB:]
        h = pl.program_id(0)
        j = pl.program_id(1)

        @pl.when(j == 0)
        def _():
            dq_ref[...] = jnp.zeros_like(dq_ref)

        dk_acc[...] = jnp.zeros_like(dk_acc)
        dv_acc[...] = jnp.zeros_like(dv_acc)

        def step(i, b):
            rows = pl.ds(pl.multiple_of(i * T, T), T)
            for hh in range(HB):
                st_refs[hh][...] = _dot_nt(k_ref[hh], q_ref[hh, rows, :])
                dpt_refs[hh][...] = _dot_nt(v_ref[hh], do_ref[hh, rows, :])
            for hh in range(HB):
                lse_i = lse_ref[hh, i]
                dd_i = dd_ref[hh, i]
                for ch in chunks:
                    x = st_refs[hh][ch, :]
                    if b is not None:
                        x = x + tab_ref[b, ch, :]
                    pc = jnp.exp2(x - lse_i)
                    pt_refs[hh][ch, :] = pc.astype(BF16)
                    dst_refs[hh][ch, :] = (pc * (dpt_refs[hh][ch, :] - dd_i)).astype(BF16)
                dv_acc[hh] += _dot(pt_refs[hh][...], do_ref[hh, rows, :])
                dk_acc[hh] += _dot(dst_refs[hh][...], q_ref[hh, rows, :])
                dq_ref[hh, rows, :] += _dot_tn(dst_refs[hh][...], k_ref[hh])

        step(j, 0)
        if full_range:
            pl.loop(j + 1, nq)(lambda i: step(i, None))
        else:
            pl.loop(j + 1, jnp.minimum(j + nb, nq))(lambda i: step(i, i - j))
        dk_ref[...] = dk_acc[...].astype(dk_ref.dtype)
        dv_ref[...] = dv_acc[...].astype(dv_ref.dtype)

        @pl.when(j == nq - 1)
        def _():
            src = dq_ref
            if narrow[0]:
                src, = dq_cast
                src[...] = dq_ref[...].astype(BF16)
            out = pltpu.make_async_copy(src, dq_hbm.at[pl.ds(h * HB, HB)], dq_sem)
            out.start()
            out.wait()

    once = dict(pipeline_mode=pl.Buffered(1))
    per_head = pl.BlockSpec((HB, S, HEAD_PAD), lambda h, j: (h, 0, 0), **once)
    rows = pl.BlockSpec((HB, nq, 1, T), lambda h, j: (h, 0, 0, 0))
    blk = pl.BlockSpec((HB, T, HEAD_PAD), lambda h, j: (h, j, 0))
    shp = [jax.ShapeDtypeStruct((H, S, HEAD_PAD), BF16 if nar else F32) for nar in narrow]
    acc = pltpu.VMEM((HB, T, HEAD_PAD), F32)
    return _call_with_rider(
        body, name, (H // HB, nq), rider,
        in_specs=[per_head, per_head, rows, rows, blk, blk,
                  pl.BlockSpec((nb, T, T), lambda h, j: (0, 0, 0), **once)],
        out_specs=[pl.BlockSpec(memory_space=pltpu.HBM), blk, blk], out_shape=shp,
        scratch_shapes=([pltpu.VMEM((T, T), F32)] * (2 * HB) + [pltpu.VMEM((T, T), BF16)] * (2 * HB)
                        + [pltpu.VMEM((HB, S, HEAD_PAD), F32), acc, acc]
                        + [pltpu.VMEM((HB, S, HEAD_PAD), BF16)] * narrow[0] + [pltpu.SemaphoreType.DMA]),
        operands=(q, do, lse, dd, k, v, table_t))


def _mid_bwd(dh2, h1, e, gate, g2, w_pg, w_out, oa, ob, zm):
    S = dh2.shape[0]

    def body(dh2_ref, h1_ref, e_ref, gate_ref, g2_ref, wg_ref, wo_ref, oa_ref, ob_ref, ga_ref, gb_ref,
             dh1_ref, dh1b_ref, de_ref, dpre_ref, doa_ref, dob_ref, dga_ref, dgb_ref, dd_ref, dg2_ref):
        @pl.when(pl.program_id(0) == 0)
        def _():
            dg2_ref[...] = jnp.zeros_like(dg2_ref)

        lane = lax.broadcasted_iota(jnp.int32, (TOK_T, HEAD_PAD), 1)
        lo_half = lane < HEAD_DIM
        dh2 = dh2_ref[...]
        gate = gate_ref[...]
        de_ref[...] = (dh2 * gate).astype(BF16)
        dpre = (dh2 * e_ref[...] * gate * (1.0 - gate)).astype(BF16)
        dpre_ref[...] = dpre
        du2 = _dot_nt(dpre, wg_ref[...])
        h1 = h1_ref[...]
        r = lax.rsqrt(jnp.mean(h1 * h1, axis=-1, keepdims=True) + EPS)
        xh = h1 * r
        a = du2 * g2_ref[...]
        dh1 = dh2 + r * (a - xh * jnp.mean(a * xh, axis=-1, keepdims=True))
        dg2_ref[...] += jnp.sum(du2 * xh, axis=0, keepdims=True)
        dh1_ref[...] = dh1
        dh1b = dh1.astype(BF16)
        dh1b_ref[...] = dh1b
        dy = _dot_nt(dh1b, wo_ref[...])
        dd = jnp.zeros((TOK_T, HEAD_PAD), F32)
        for bi, (o_ref, g_ref, do_ref, dg_ref) in enumerate(
                ((oa_ref, ga_ref, doa_ref, dga_ref), (ob_ref, gb_ref, dob_ref, dgb_ref))):
            for pi in range(N_HEADS // 2):
                col = bi * D_BRANCH + HEAD_PAD * pi
                dyp = dy[:, col:col + HEAD_PAD]
                g = g_ref[:, HEAD_PAD * pi:HEAD_PAD * (pi + 1)].astype(F32)
                sg = _sigmoid(g)
                o_pair = o_ref[pi]
                dg_ref[:, HEAD_PAD * pi:HEAD_PAD * (pi + 1)] = (
                    dyp * o_pair * (sg * (1.0 + g * (1.0 - sg)))).astype(BF16)
                dop = dyp * (g * sg)
                prod = dop * o_pair
                for hh, d_head, mine in ((2 * pi, dop, lo_half),
                                         (2 * pi + 1, pltpu.roll(dop, HEAD_DIM, 1), ~lo_half)):
                    do_ref[hh] = jnp.where(lo_half, d_head, 0.0).astype(BF16)
                    dsum = jnp.sum(jnp.where(mine, prod, 0.0), axis=1, keepdims=True)
                    dd = dd + jnp.where(lane == bi * N_HEADS + hh, dsum, 0.0)
        dd_ref[...] = dd.T[:2 * N_HEADS, :]

    tok = lambda w: pl.BlockSpec((TOK_T, w), lambda i: (i, 0))
    head = pl.BlockSpec((N_HEADS, TOK_T, HEAD_PAD), lambda i: (0, i, 0))
    pairs = pl.BlockSpec((N_HEADS // 2, TOK_T, HEAD_PAD), lambda i: (0, i, 0))
    full = lambda a, b: pl.BlockSpec((a, b), lambda i: (0, 0))
    act = lambda w, dt: jax.ShapeDtypeStruct((S, w), dt)
    hshape = lambda w, dt: jax.ShapeDtypeStruct((N_HEADS, S, w), dt)
    return pl.pallas_call(
        body, name="mid_bwd", grid=(S // TOK_T,),
        in_specs=[tok(D_MODEL)] * 4 + [full(1, D_MODEL), full(D_MODEL, D_MODEL), full(D_MODEL, D_MODEL), pairs, pairs,
                                      pl.BlockSpec((TOK_T, D_BRANCH), lambda i: (i, 3)),
                                      pl.BlockSpec((TOK_T, D_BRANCH), lambda i: (i, 7))],
        out_specs=[tok(D_MODEL)] * 4 + [head, head, tok(D_BRANCH), tok(D_BRANCH),
                                       pl.BlockSpec((2 * N_HEADS, TOK_T), lambda i: (0, i)), full(1, D_MODEL)],
        out_shape=[act(D_MODEL, F32), act(D_MODEL, BF16), act(D_MODEL, BF16), act(D_MODEL, BF16),
                   hshape(HEAD_PAD, BF16), hshape(HEAD_PAD, BF16), act(D_BRANCH, BF16), act(D_BRANCH, BF16),
                   jax.ShapeDtypeStruct((2 * N_HEADS, S), F32), jax.ShapeDtypeStruct((1, D_MODEL), F32)],
        compiler_params=_cp("arbitrary"),
    )(dh2, h1, e, gate, g2, w_pg, w_out, oa, ob, zm, zm)


def _prep_bwd(dqa, dka, dva, dqb, dkb, dvb, zm, qkg, rope_c, rope_a, rope_b, dga, dgb, zf, bf, triu):
    S = zm.shape[0]
    n = S // TOK_T

    def body(dqa_ref, dka_ref, dva_ref, dqb_ref, dkb_ref, dvb_ref, z_ref, g_ref, rc_ref, ra_ref, rb_ref,
             dga_ref, dgb_ref, zf_ref, b_ref, tri_ref, dz_ref, dzf_ref, dqkg_ref, db_ref, carry):
        @pl.when(pl.program_id(0) == 0)
        def _():
            dqkg_ref[...] = jnp.zeros_like(dqkg_ref)
            db_ref[...] = jnp.zeros_like(db_ref)
            carry[...] = jnp.zeros_like(carry)

        lane = lax.broadcasted_iota(jnp.int32, (TOK_T, HEAD_PAD), 1)
        lo_half = lane < HEAD_DIM
        rc, ra, rb = rc_ref[...], ra_ref[...], rb_ref[...]
        same_head = _same_head()

        def unrope(dy):
            return dy * rc + pltpu.roll(dy * ra, ROPE_HALF, 1) + pltpu.roll(dy * rb, HEAD_PAD - ROPE_HALF, 1)

        def norm_bwd(col, gi, dy):
            x = z_ref[:, col:col + HEAD_PAD].astype(F32)
            r = _pair_rsqrt(x, same_head)
            xh = x * r
            dqkg_ref[gi:gi + 1, :] += jnp.sum(dy * xh, axis=0, keepdims=True)
            a = dy * g_ref[gi:gi + 1, :]
            dz_ref[:, col:col + HEAD_PAD] = (r * (a - xh * _pair_mean(a * xh, same_head))).astype(BF16)

        dc = jnp.zeros((TOK_T, HEAD_PAD), F32)
        for pi in range(N_HEADS // 2):
            col = HEAD_PAD * pi
            norm_bwd(col, 0, _pair(dqa_ref, pi, lo_half) * SCALE)
            norm_bwd(D_BRANCH + col, 1, _pair(dka_ref, pi, lo_half) * LN2)
            dz_ref[:, 2 * D_BRANCH + col:2 * D_BRANCH + col + HEAD_PAD] = _pair(dva_ref, pi, lo_half).astype(BF16)
            norm_bwd(4 * D_BRANCH + col, 2, unrope(_pair(dqb_ref, pi, lo_half) * SCALE))
            norm_bwd(5 * D_BRANCH + col, 3, unrope(_pair(dkb_ref, pi, lo_half) * LN2))
            dz_ref[:, 6 * D_BRANCH + col:6 * D_BRANCH + col + HEAD_PAD] = _pair(dvb_ref, pi, lo_half).astype(BF16)
            for hh in (2 * pi, 2 * pi + 1):
                dch = dka_ref[hh][:, HEAD_DIM:HEAD_DIM + 1] + dqa_ref[hh][:, HEAD_DIM + 3:HEAD_DIM + 4]
                dc = dc + jnp.where(lane == hh, dch, 0.0)
        dz_ref[:, 3 * D_BRANCH:4 * D_BRANCH] = dga_ref[...]
        dz_ref[:, 7 * D_BRANCH:8 * D_BRANCH] = dgb_ref[...]
        tri = tri_ref[...]
        dlf = sum(_dot(tri, part) for part in _split3(dc)) + carry[...]
        carry[...] = dlf[0:1, :]
        dfa = dlf * (1.0 - _sigmoid(zf_ref[...] + b_ref[...]))
        dzf_ref[...] = dfa.astype(BF16)
        db_ref[...] += jnp.sum(dfa, axis=0, keepdims=True)

    tok = lambda w: pl.BlockSpec((TOK_T, w), lambda i: (n - 1 - i, 0))
    head = pl.BlockSpec((N_HEADS, TOK_T, HEAD_PAD), lambda i: (0, n - 1 - i, 0))
    fixed = lambda a, b: pl.BlockSpec((a, b), lambda i: (0, 0))
    return pl.pallas_call(
        body, name="prep_bwd", grid=(n,),
        in_specs=[head] * 6 + [tok(N_MAIN), fixed(4, 128), tok(128), tok(128), tok(128), tok(D_BRANCH), tok(D_BRANCH),
                               tok(128), fixed(1, 128), fixed(TOK_T, TOK_T)],
        out_specs=[tok(N_MAIN), tok(128), fixed(4, 128), fixed(1, 128)],
        out_shape=[jax.ShapeDtypeStruct((S, N_MAIN), BF16), jax.ShapeDtypeStruct((S, 128), BF16),
                   jax.ShapeDtypeStruct((4, 128), F32), jax.ShapeDtypeStruct((1, 128), F32)],
        scratch_shapes=[pltpu.VMEM((1, 128), F32)],
        compiler_params=_cp("arbitrary"),
    )(dqa, dka, dva, dqb, dkb, dvb, zm, qkg, rope_c, rope_a, rope_b, dga, dgb, zf, bf, triu)


def _inproj_bwd(dzm, dzf, wm, wf, h0, dh1, g, rider=None):
    S = h0.shape[0]

    def body(dzm_ref, dzf_ref, wm_ref, wf_ref, h_ref, dh1_ref, g_ref, dh0_ref, dg_ref):
        @pl.when(pl.program_id(0) == 0)
        def _():
            dg_ref[...] = jnp.zeros_like(dg_ref)

        du = _dot_nt(dzm_ref[...], wm_ref[...]) + _dot_nt(dzf_ref[...], wf_ref[...])
        x = h_ref[...]
        r = lax.rsqrt(jnp.mean(x * x, axis=-1, keepdims=True) + EPS)
        xh = x * r
        a = du * g_ref[...]
        dh0_ref[...] = dh1_ref[...] + r * (a - xh * jnp.mean(a * xh, axis=-1, keepdims=True))
        dg_ref[...] += jnp.sum(du * xh, axis=0, keepdims=True)

    tok = lambda w: pl.BlockSpec((TOK_T, w), lambda i: (i, 0))
    full = lambda a, b: pl.BlockSpec((a, b), lambda i: (0, 0))
    return _call_with_rider(
        body, "inproj_bwd", (S // TOK_T,), rider,
        in_specs=[tok(N_MAIN), tok(128), full(D_MODEL, N_MAIN), full(D_MODEL, 128), tok(D_MODEL), tok(D_MODEL),
                  full(1, D_MODEL)],
        out_specs=[tok(D_MODEL), full(1, D_MODEL)],
        out_shape=[jax.ShapeDtypeStruct((S, D_MODEL), F32), jax.ShapeDtypeStruct((1, D_MODEL), F32)],
        scratch_shapes=[], operands=(dzm, dzf, wm, wf, h0, dh1, g), semantics=("arbitrary",))


def _wgrad(a, b, name, a_lead=()):
    S, M = a.shape[len(a_lead):]
    N = b.shape[1]
    tn = min(N, 2048)
    ts = 512
    last = S // ts - 1

    def body(a_ref, b_ref, o_ref, acc_ref):
        @pl.when(pl.program_id(1) == 0)
        def _():
            acc_ref[...] = jnp.zeros_like(acc_ref)

        acc_ref[...] += _dot_tn(a_ref[...].astype(BF16), b_ref[...])

        @pl.when(pl.program_id(1) == last)
        def _():
            o_ref[...] = acc_ref[...].astype(BF16)

    return pl.pallas_call(
        body, name=name, grid=(N // tn, S // ts),
        in_specs=[_slab_spec(a_lead, (ts, M), lambda n, s: (s, 0)), pl.BlockSpec((ts, tn), lambda n, s: (s, n))],
        out_specs=pl.BlockSpec((M, tn), lambda n, s: (0, n)),
        out_shape=jax.ShapeDtypeStruct((M, N), BF16),
        scratch_shapes=[pltpu.VMEM((M, tn), F32)],
        compiler_params=_cp("parallel", "arbitrary"),
    )(a, b)


def _rope_tables(positions):
    inv_freq = ROPE_THETA ** (-jnp.arange(ROPE_HALF, dtype=F32) / ROPE_HALF)
    ang = positions.astype(F32)[:, None] * inv_freq
    cos, sin = jnp.cos(ang), jnp.sin(ang)
    S = positions.shape[0]
    one, zero = jnp.ones((S, HEAD_DIM - 2 * ROPE_HALF), F32), jnp.zeros((S, HEAD_DIM - 2 * ROPE_HALF), F32)
    z8 = jnp.zeros((S, ROPE_HALF), F32)
    rc = jnp.concatenate([cos, cos, one], axis=1)
    ra = jnp.concatenate([-sin, z8, zero], axis=1)
    rb = jnp.concatenate([z8, sin, zero], axis=1)
    return tuple(jnp.tile(t, (1, 2)) for t in (rc, ra, rb))


def _in_operands(w_in):
    w_in = w_in.astype(BF16)
    wm = jnp.concatenate([w_in[:, :4 * D_BRANCH], w_in[:, 4 * D_BRANCH + N_HEADS:]], axis=1)
    wf = jnp.pad(w_in[:, 4 * D_BRANCH:4 * D_BRANCH + N_HEADS], ((0, 0), (0, 128 - N_HEADS)))
    return dict(wm=wm, wf=wf)


def _layer_weights(w_in, w_out, w_ple, w_pg):
    return dict(_in_operands(w_in), w_out=w_out.astype(BF16), w_ple=w_ple.astype(BF16), w_pg=w_pg.astype(BF16))


def _row(v, width=128):
    v = v.reshape(1, -1).astype(F32)
    return jnp.pad(v, ((0, 0), (0, width - v.shape[1])))


def _layer_fwd(h0, p, rope, tabs, w, norm_g, b_f, qk_g, ple_g, rider=None, late=None, target=None):
    g1 = norm_g.reshape(1, D_MODEL)
    g2 = ple_g.reshape(1, D_MODEL)
    qkg = jnp.tile(qk_g, (1, 2))
    bf = _row(b_f)
    zm, zf, u = _inproj_fwd(h0, g1, w["wm"], w["wf"])
    qa, ka, va, qb, kb, vb = _prep_fwd(zm, zf, bf, tabs["tril"], qkg, *rope)
    oa, lse_a, *arrivals = _attn_fwd(qa, ka, va, tabs["fox"], True, "fox_fwd", rider)
    if late is not None:
        w = {**w, **late(arrivals)}
    ob, lse_b = _attn_fwd(qb, kb, vb, tabs["dil"], False, "dil_fwd")
    y, h1, h2, u2, e, gate, *loss = _mid_fwd(oa, ob, zm, h0, p, w["w_out"], w["w_pg"], w["w_ple"], g2, target)
    saved = dict(h0=h0, p=p, zm=zm, zf=zf, u=u, qa=qa, ka=ka, va=va, qb=qb, kb=kb, vb=vb, oa=oa, ob=ob,
                 lse_a=lse_a, lse_b=lse_b, y=y, h1=h1, u2=u2, e=e, gate=gate, g1=g1, g2=g2, qkg=qkg, bf=bf, w=w)
    return (h2, saved, arrivals, *loss)


def _layer_bwd(dh2, sv, rope, tabs, make_rider=None, make_last_rider=None):
    S = dh2.shape[0]
    nq = S // ATT_T
    w = sv["w"]
    rows = lambda a: a.reshape(N_HEADS, nq, 1, ATT_T)
    (dh1, dh1b, de, dpre, doa, dob, dga, dgb, dd, dg2) = _mid_bwd(
        dh2, sv["h1"], sv["e"], sv["gate"], sv["g2"], w["w_pg"], w["w_out"], sv["oa"], sv["ob"], sv["zm"])
    dda, ddb = dd[:N_HEADS], dd[N_HEADS:]
    early = dict(w_out=_wgrad(sv["y"], dh1b, "wgrad_out"), w_ple=_wgrad(sv["p"][0], de, "wgrad_ple", sv["p"][1]),
                 w_ple_gate=_wgrad(sv["u2"], dpre, "wgrad_gate"))
    rider = None if make_rider is None else make_rider(early)
    dqa, dka, dva, *arrivals = _attn_bwd(sv["qa"], sv["ka"], sv["va"], doa, sv["lse_a"], rows(dda), tabs["fox"],
                                         True, "fox_bwd", rider, narrow=(False, False, True))
    dqb, dkb, dvb = _attn_bwd(sv["qb"], sv["kb"], sv["vb"], dob, sv["lse_b"], rows(ddb), tabs["dil"], False,
                              "dil_bwd", narrow=(True, True, True))
    dzm, dzf, dqkg, dbf = _prep_bwd(dqa, dka, dva, dqb, dkb, dvb, sv["zm"], sv["qkg"], *rope, dga, dgb,
                                    sv["zf"], sv["bf"], tabs["triu"])
    dwm = _wgrad(sv["u"], dzm, "wgrad_in")
    dwf = _wgrad(sv["u"], dzf, "wgrad_f")
    dw_in = jnp.concatenate([dwm[:, :4 * D_BRANCH], dwf[:, :N_HEADS], dwm[:, 4 * D_BRANCH:]], axis=1)
    last_rider = None if make_last_rider is None else make_last_rider(dw_in)
    dh0, dg1, *last_arrivals = _inproj_bwd(dzm, dzf, w["wm"], w["wf"], sv["h0"], dh1, sv["g1"], last_rider)
    grads = dict(norm_g=dg1.reshape(D_MODEL), w_in=dw_in, b_f=dbf[0, :N_HEADS],
                 qk_norm_g=dqkg[:, :HEAD_DIM] + dqkg[:, HEAD_DIM:], ple_norm_g=dg2.reshape(D_MODEL), **early)
    return dh0, grads, arrivals + last_arrivals


def _tables():
    T = TOK_T
    r = lax.broadcasted_iota(jnp.int32, (T, T), 0)
    c = lax.broadcasted_iota(jnp.int32, (T, T), 1)
    return dict(fox=_bias_tables(True), dil=_bias_tables(False),
                tril=(c <= r).astype(BF16), triu=(c >= r).astype(BF16))


def _local_step(x, p, positions, target, layers, small):
    rope = _rope_tables(positions)
    tabs = _tables()
    ws = [_layer_weights(*lw) for lw in layers]
    h = x
    saved = []
    for li, (w, lp, sm) in enumerate(zip(ws, p, small)):
        h, sv, _, *loss = _layer_fwd(h, (lp, ()), rope, tabs, w, *sm, target=target if li == len(ws) - 1 else None)
        saved.append(sv)
    dh, (loss,) = h, loss
    grads = [None] * len(ws)
    for li in reversed(range(len(ws))):
        dh, grads[li], _ = _layer_bwd(dh, saved[li], rope, tabs)
    return loss[0, 0], dh, grads


def _peers():
    x, y, c = lax.axis_index("x"), lax.axis_index("y"), lax.axis_index("c")
    me = 4 * x + 2 * y + c
    flip = lambda v, bit: 1 - v if bit else v
    return me, [(flip(x, k & 4), flip(y, k & 2), flip(c, k & 1)) for k in range(1, N_DEV)]


def _sel(ref, kind, d):
    if kind == "whole":
        return ref
    if kind == "slot":
        return ref.at[d]
    block = pl.ds(pl.multiple_of(d * 128, 128), 128)
    return ref.at[block, :] if kind == "rows" else ref.at[:, block]


class _Pushes:
    def __init__(self, arrays, src_kinds, dst_kinds, out_shapes):
        self.arrays, self.n = list(arrays), len(arrays)
        self.src_kinds, self.dst_kinds = src_kinds, dst_kinds
        self.out_shapes = [jax.ShapeDtypeStruct(s, a.dtype) for s, a in zip(out_shapes, arrays)]
        hbm = pl.BlockSpec(memory_space=pltpu.HBM)
        self.in_specs, self.out_specs = [hbm] * self.n, [hbm] * self.n
        self.scratch_shapes = [pltpu.SemaphoreType.DMA((N_DEV - 1, self.n)),
                               pltpu.SemaphoreType.DMA((N_DEV - 1, self.n)), pltpu.SemaphoreType.DMA((self.n,))]

    def _copies(self, ins, outs, sems):
        send_sems, recv_sems, local_sems = sems
        me, peers = _peers()
        src = lambda a, d: _sel(ins[a], self.src_kinds[a], d)
        dst = lambda a: _sel(outs[a], self.dst_kinds[a], me)
        local = [pltpu.make_async_copy(src(a, me), dst(a), local_sems.at[a]) for a in range(self.n)]
        remote = [pltpu.make_async_remote_copy(
            src_ref=src(a, 4 * px + 2 * py + pc), dst_ref=dst(a), send_sem=send_sems.at[k, a],
            recv_sem=recv_sems.at[k, a], device_id=(px, py, pc), device_id_type=pl.DeviceIdType.MESH)
            for k, (px, py, pc) in enumerate(peers) for a in range(self.n)]
        return local + remote

    def start(self, ins, outs, sems):
        for cp in self._copies(ins, outs, sems):
            cp.start()

    def wait(self, ins, outs, sems):
        for cp in self._copies(ins, outs, sems):
            cp.wait()


def _exchange(name, pushes):
    n = pushes.n

    def body(*refs):
        pushes.start(refs[:n], refs[n:2 * n], refs[2 * n:])
        pushes.wait(refs[:n], refs[n:2 * n], refs[2 * n:])

    return pl.pallas_call(body, name=name, in_specs=pushes.in_specs, out_specs=pushes.out_specs,
                          out_shape=pushes.out_shapes, scratch_shapes=pushes.scratch_shapes)(*pushes.arrays)


def _gather_two_level(shard, name):
    def body(x_ref, out_ref, send_sems, recv_sems, local_sem):
        x, y, c = lax.axis_index("x"), lax.axis_index("y"), lax.axis_index("c")
        me, sibling = (x, y, c), (x, y, 1 - c)
        chips = [(1 - x, y), (x, 1 - y), (1 - x, 1 - y)]
        slot = lambda px, py, pc: out_ref.at[4 * px + 2 * py + pc]

        def copy(k, block, to, src=None):
            return pltpu.make_async_remote_copy(
                src_ref=slot(*block) if src is None else src, dst_ref=slot(*block), send_sem=send_sems.at[k],
                recv_sem=recv_sems.at[k], device_id=to, device_id_type=pl.DeviceIdType.MESH)

        mine = pltpu.make_async_copy(x_ref, slot(*me), local_sem)
        mine.start()
        first = [copy(0, me, sibling, src=x_ref)] + [copy(1 + j, me, (*chip, c), src=x_ref)
                                                     for j, chip in enumerate(chips)]
        for cp in first:
            cp.start()
        passed = [copy(4 + j, (*chip, c), sibling) for j, chip in enumerate(chips)]
        for j, chip in enumerate(chips):
            copy(1 + j, (*chip, c), me).wait_recv()
            passed[j].start()
        copy(0, sibling, me).wait_recv()
        for j, chip in enumerate(chips):
            copy(4 + j, (*chip, 1 - c), me).wait_recv()
        for cp in first + passed:
            cp.wait_send()
        mine.wait()

    hbm = pl.BlockSpec(memory_space=pltpu.HBM)
    return pl.pallas_call(
        body, name=name, in_specs=[hbm], out_specs=hbm,
        out_shape=jax.ShapeDtypeStruct((N_DEV,) + shard.shape, shard.dtype),
        scratch_shapes=[pltpu.SemaphoreType.DMA((N_DEV - 1,)), pltpu.SemaphoreType.DMA((N_DEV - 1,)),
                        pltpu.SemaphoreType.DMA],
    )(shard)


def _gather_pushes(shards, kinds):
    full = {"slot": lambda s: (N_DEV,) + s, "rows": lambda s: (N_DEV * s[0], s[1]),
            "cols": lambda s: (s[0], N_DEV * s[1])}
    return _Pushes(shards, ["whole"] * len(shards), kinds, [full[k](a.shape) for a, k in zip(shards, kinds)])


def _scatter_pushes(partials, kinds):
    part = {"slot": lambda s: s[1:], "rows": lambda s: (128, s[1]), "cols": lambda s: (s[0], 128),
            "whole": lambda s: s}
    return _Pushes(partials, kinds, ["slot"] * len(partials),
                   [(N_DEV,) + part[k](a.shape) for a, k in zip(partials, kinds)])


def _adamw(name, parts, w, m, v, rows):
    L, R, C = w.shape

    def body(p_ref, w_ref, m_ref, v_ref, g_ref, d_ref, nm_ref, nv_ref):
        g = p_ref[0, 0].astype(F32)
        for s in range(1, N_DEV):
            g = g + p_ref[s, 0].astype(F32)
        g_ref[0] = g
        nm = ADAM_B1 * m_ref[0] + (1.0 - ADAM_B1) * g
        nv = ADAM_B2 * v_ref[0] + (1.0 - ADAM_B2) * (g * g)
        nm_ref[0] = nm
        nv_ref[0] = nv
        m_hat = nm / (1.0 - ADAM_B1 ** ADAM_STEP)
        v_hat = nv / (1.0 - ADAM_B2 ** ADAM_STEP)
        d_ref[0] = -ADAM_LR * (m_hat / (jnp.sqrt(v_hat) + ADAM_EPS) + ADAM_WD * w_ref[0])

    blk = pl.BlockSpec((1, rows, C), lambda l, i: (l, i, 0))
    shp = jax.ShapeDtypeStruct((L, R, C), F32)
    return pl.pallas_call(
        body, name=name, grid=(L, R // rows),
        in_specs=[pl.BlockSpec((N_DEV, 1, rows, C), lambda l, i: (0, l, i, 0)), blk, blk, blk],
        out_specs=[blk] * 4, out_shape=[shp] * 4,
        compiler_params=_cp("parallel", "parallel"),
    )(parts, w, m, v)


SMALL_ROWS = 40
LOSS_ROW = 37


def _pack_small(norm_g, ple_g, qk_g, b_f, last_row):
    rows = lambda a: a.astype(F32).reshape(-1, 128)
    flat = jnp.concatenate([rows(norm_g), rows(ple_g), rows(qk_g), _row(b_f.reshape(-1)), last_row], axis=0)
    return jnp.pad(flat, ((0, SMALL_ROWS - flat.shape[0]), (0, 0)))


def _unpack_small(flat):
    return (flat[0:16].reshape(2, D_MODEL), flat[16:32].reshape(2, D_MODEL), flat[32:36].reshape(2, 4, HEAD_DIM),
            flat[36, :2 * N_HEADS].reshape(2, N_HEADS))


def kernel(x, p, positions, norm_g, w_in, b_f, qk_norm_g, w_out, w_ple, ple_norm_g, w_ple_gate, loss_target, m_norm_g, m_w_in, m_b_f, m_qk_norm_g, m_w_out, m_w_ple, m_ple_norm_g, m_w_ple_gate, v_norm_g, v_w_in, v_b_f, v_qk_norm_g, v_w_out, v_w_ple, v_ple_norm_g, v_w_ple_gate):
    bf16 = lambda a: a.astype(BF16)
    rows_in = W_IN_ROWS // 2
    flat_in = lambda a: bf16(a).reshape(rows_in, 128)
    full_in = lambda g: g.reshape(N_DEV, D_MODEL, W_IN_SHARD).transpose(1, 0, 2).reshape(D_MODEL, N_IN)
    small = [(norm_g[l], b_f[l], qk_norm_g[l], ple_norm_g[l]) for l in range(2)]
    rope = _rope_tables(positions[0])
    tabs = _tables()

    g_in0 = _gather_two_level(flat_in(w_in[0]), "gather_first")
    rest = _gather_pushes([flat_in(w_in[1])] + [bf16(a[l]) for l in range(2) for a in (w_out, w_ple, w_ple_gate)],
                          ["slot"] + ["rows", "cols", "rows"] * 2)
    late = lambda got: dict(w_out=got[1], w_ple=got[2], w_pg=got[3])
    h, sv0, got = _layer_fwd(x[0], (p, (0, 0)), rope, tabs, _in_operands(full_in(g_in0)), *small[0], rest, late)
    w1 = dict(_in_operands(full_in(got[0])), w_out=got[4], w_ple=got[5], w_pg=got[6])
    dh, sv1, _, loss = _layer_fwd(h, (p, (1, 0)), rope, tabs, w1, *small[1], target=loss_target[0])
    dh, gr1, _ = _layer_bwd(dh, sv1, rope, tabs)

    by_dest = lambda d: d.reshape(D_MODEL, N_DEV, W_IN_SHARD).transpose(1, 0, 2).reshape(N_DEV, rows_in, 128)
    big = ("w_out", "w_ple", "w_ple_gate")
    riding = lambda early: _scatter_pushes([by_dest(gr1["w_in"])] + [gr1[n] for n in big] + [early[n] for n in big],
                                           ["slot"] + ["rows", "cols", "rows"] * 2)
    riding_last = lambda dw_in: _scatter_pushes([by_dest(dw_in)], ["slot"])
    dx, gr0, (r_in1, *r_big, r_in0) = _layer_bwd(dh, sv0, rope, tabs, riding, riding_last)
    grads = (gr0, gr1)
    stack = lambda name: jnp.stack([gl[name] for gl in grads], axis=0)
    small_part = _pack_small(stack("norm_g"), stack("ple_norm_g"), stack("qk_norm_g"), stack("b_f"),
                             _row(loss[0, 0].reshape(1)))
    (r_small,) = _exchange("exchange_small", _scatter_pushes([small_part], ["whole"]))
    r_in = jnp.concatenate([r_in0, r_in1], axis=1)
    r_out, r_ple, r_pg = (jnp.stack([r_big[3 + k], r_big[k]], axis=1) for k in range(3))

    zero_row = jnp.zeros((1, 128), F32)
    small_of = lambda ng, pg, qk, bf: _pack_small(ng, pg, qk, bf, zero_row)[None]
    flat = lambda a: a.reshape(1, W_IN_ROWS, 128)
    outs = dict(
        w_in=[o.reshape(w_in.shape) for o in
              _adamw("adamw_in", r_in[:, None], flat(w_in), flat(m_w_in), flat(v_w_in), W_IN_TILE)],
        w_out=_adamw("adamw_out", r_out, w_out, m_w_out, v_w_out, 128),
        w_ple=_adamw("adamw_ple", r_ple, w_ple, m_w_ple, v_w_ple, 256),
        w_pg=_adamw("adamw_gate", r_pg, w_ple_gate, m_w_ple_gate, v_w_ple_gate, 128),
        small=_adamw("adamw_small", r_small[:, None], small_of(norm_g, ple_norm_g, qk_norm_g, b_f),
                     small_of(m_norm_g, m_ple_norm_g, m_qk_norm_g, m_b_f),
                     small_of(v_norm_g, v_ple_norm_g, v_qk_norm_g, v_b_f), SMALL_ROWS))
    leaves = []
    for kind in range(4):
        ng, pg, qk, bf = _unpack_small(outs["small"][kind][0])
        leaves += [ng, outs["w_in"][kind], bf, qk, outs["w_out"][kind], outs["w_ple"][kind], pg, outs["w_pg"][kind]]
    return (outs["small"][0][0, LOSS_ROW, 0], dx[None], *leaves)
```

```python
import functools

import jax
import jax.numpy as jnp
from jax import lax
from jax.experimental import pallas as pl
from jax.experimental.pallas import tpu as pltpu

F32 = jnp.float32
BF16 = jnp.bfloat16

D_MODEL = 1024
HEAD_DIM = 64
N_HEADS = 8
HEAD_PAD = 128
D_BRANCH = N_HEADS * HEAD_DIM
N_MAIN = 8 * D_BRANCH
N_IN = N_MAIN + N_HEADS
PLE_DIM = 256
ROPE_THETA = 500000.0
ROPE_HALF = 8
EPS = 1e-6
NEG = -1e30
SCALE = HEAD_DIM ** -0.5
LOG2E = 1.4426950408889634
LN2 = 0.6931471805599453
DILATED_PATTERNS = ((128, 1), (512, 4), (2048, 16))
N_DEV = 8
W_IN_SHARD = N_IN // N_DEV
W_IN_ROWS = 2 * D_MODEL * W_IN_SHARD // 128
W_IN_TILE = W_IN_ROWS // 19

ADAM_LR = 0.001
ADAM_B1 = 0.9
ADAM_B2 = 0.999
ADAM_EPS = 1e-08
ADAM_WD = 0.01
ADAM_STEP = 10

ATT_T = 512
ATT_FWD_HEADS = 8
ATT_BWD_HEADS = 4
ATT_CHUNK = 32
TOK_T = 256
VMEM_LIMIT = 60 * 1024 * 1024


def _slab_spec(lead, block, index):
    return pl.BlockSpec((None,) * len(lead) + block, lambda *g: (*lead, *index(*g)))


def _cp(*sem):
    return pltpu.CompilerParams(dimension_semantics=sem, vmem_limit_bytes=VMEM_LIMIT)


def _sigmoid(x):
    return 1.0 / (1.0 + jnp.exp(-x))


def _split3(x):
    hi = x.astype(BF16)
    r1 = x - hi.astype(F32)
    mid = r1.astype(BF16)
    lo = (r1 - mid.astype(F32)).astype(BF16)
    return hi, mid, lo


def _dot(a, b):
    return jnp.dot(a, b, preferred_element_type=F32)


def _dot_nt(a, b):
    return lax.dot_general(a, b, (((1,), (1,)), ((), ())), preferred_element_type=F32)


def _dot_tn(a, b):
    return lax.dot_general(a, b, (((0,), (0,)), ((), ())), preferred_element_type=F32)


def _inproj_fwd(h, g, wm, wf):
    S = h.shape[0]

    def body(h_ref, g_ref, wm_ref, wf_ref, zm_ref, zf_ref, u_ref):
        x = h_ref[...]
        r = lax.rsqrt(jnp.mean(x * x, axis=-1, keepdims=True) + EPS)
        u = (x * r * g_ref[...]).astype(BF16)
        u_ref[...] = u
        zm_ref[...] = _dot(u, wm_ref[...]).astype(BF16)
        zf_ref[...] = _dot(u, wf_ref[...])

    return pl.pallas_call(
        body, name="inproj_fwd", grid=(S // TOK_T,),
        in_specs=[pl.BlockSpec((TOK_T, D_MODEL), lambda i: (i, 0)),
                  pl.BlockSpec((1, D_MODEL), lambda i: (0, 0)),
                  pl.BlockSpec((D_MODEL, N_MAIN), lambda i: (0, 0)),
                  pl.BlockSpec((D_MODEL, 128), lambda i: (0, 0))],
        out_specs=[pl.BlockSpec((TOK_T, N_MAIN), lambda i: (i, 0)),
                   pl.BlockSpec((TOK_T, 128), lambda i: (i, 0)),
                   pl.BlockSpec((TOK_T, D_MODEL), lambda i: (i, 0))],
        out_shape=[jax.ShapeDtypeStruct((S, N_MAIN), BF16), jax.ShapeDtypeStruct((S, 128), F32),
                   jax.ShapeDtypeStruct((S, D_MODEL), BF16)],
        compiler_params=_cp("parallel"),
    )(h, g, wm, wf)


def _log_sigmoid(x):
    return jnp.minimum(x, 0.0) - jnp.log(1.0 + jnp.exp(-jnp.abs(x)))


def _same_head():
    r = lax.broadcasted_iota(jnp.int32, (HEAD_PAD, HEAD_PAD), 0) // HEAD_DIM
    c = lax.broadcasted_iota(jnp.int32, (HEAD_PAD, HEAD_PAD), 1) // HEAD_DIM
    return (r == c).astype(BF16)


def _pair_mean(x, same_head):
    hi = x.astype(BF16)
    lo = (x - hi.astype(F32)).astype(BF16)
    return (_dot(hi, same_head) + _dot(lo, same_head)) * (1.0 / HEAD_DIM)


def _pair_rsqrt(x, same_head):
    return lax.rsqrt(_pair_mean(x * x, same_head) + EPS)


def _prep_fwd(zm, zf, bf, tril, qkg, rope_c, rope_a, rope_b):
    S = zm.shape[0]
    shp = jax.ShapeDtypeStruct((N_HEADS, S, HEAD_PAD), BF16)

    def body(z_ref, zf_ref, b_ref, tri_ref, g_ref, rc_ref, ra_ref, rb_ref,
             qa_ref, ka_ref, va_ref, qb_ref, kb_ref, vb_ref, carry):
        @pl.when(pl.program_id(0) == 0)
        def _():
            carry[...] = jnp.zeros_like(carry)

        tri = tri_ref[...]
        cs = sum(_dot(tri, part) for part in _split3(_log_sigmoid(zf_ref[...] + b_ref[...]))) + carry[...]
        carry[...] = cs[TOK_T - 1:TOK_T, :]
        lane = lax.broadcasted_iota(jnp.int32, (TOK_T, HEAD_PAD), 1)
        lo_half = lane < HEAD_DIM
        aug = (lane >= HEAD_DIM) & (lane < HEAD_DIM + 3)
        q_pad = jnp.where(aug, -1.0, 0.0)
        rc, ra, rb = rc_ref[...], ra_ref[...], rb_ref[...]
        same_head = _same_head()

        def norm(col, gi):
            x = z_ref[:, col:col + HEAD_PAD].astype(F32)
            return x * _pair_rsqrt(x, same_head) * g_ref[gi:gi + 1, :]

        def rope(y):
            return y * rc + pltpu.roll(y, HEAD_PAD - ROPE_HALF, 1) * ra + pltpu.roll(y, ROPE_HALF, 1) * rb

        def put(ref, pi, y, pad_even, pad_odd):
            ref[2 * pi] = jnp.where(lo_half, y, pad_even).astype(BF16)
            ref[2 * pi + 1] = jnp.where(lo_half, pltpu.roll(y, HEAD_DIM, 1), pad_odd).astype(BF16)

        def k_pad(h):
            ch = cs[:, h:h + 1] * LOG2E
            hi = ch.astype(BF16).astype(F32)
            mid = (ch - hi).astype(BF16).astype(F32)
            lo = ch - hi - mid
            ones = jnp.where(lane == HEAD_DIM + 3, 1.0, 0.0)
            return jnp.where(lane == HEAD_DIM, hi, jnp.where(lane == HEAD_DIM + 1, mid,
                                                              jnp.where(lane == HEAD_DIM + 2, lo, ones)))

        for pi in range(N_HEADS // 2):
            col = HEAD_PAD * pi
            put(qa_ref, pi, norm(col, 0) * (SCALE * LOG2E), q_pad, q_pad)
            put(ka_ref, pi, norm(D_BRANCH + col, 1), k_pad(2 * pi), k_pad(2 * pi + 1))
            put(va_ref, pi, z_ref[:, 2 * D_BRANCH + col:2 * D_BRANCH + col + HEAD_PAD].astype(F32), 0.0, 0.0)
            put(qb_ref, pi, rope(norm(4 * D_BRANCH + col, 2)) * (SCALE * LOG2E), 0.0, 0.0)
            put(kb_ref, pi, rope(norm(5 * D_BRANCH + col, 3)), 0.0, 0.0)
            put(vb_ref, pi, z_ref[:, 6 * D_BRANCH + col:6 * D_BRANCH + col + HEAD_PAD].astype(F32), 0.0, 0.0)

    tok = lambda w: pl.BlockSpec((TOK_T, w), lambda i: (i, 0))
    head = pl.BlockSpec((N_HEADS, TOK_T, HEAD_PAD), lambda i: (0, i, 0))
    return pl.pallas_call(
        body, name="prep_fwd", grid=(S // TOK_T,),
        in_specs=[tok(N_MAIN), tok(128), pl.BlockSpec((1, 128), lambda i: (0, 0)),
                  pl.BlockSpec((TOK_T, TOK_T), lambda i: (0, 0)), pl.BlockSpec((4, 128), lambda i: (0, 0)),
                  tok(128), tok(128), tok(128)],
        out_specs=[head] * 6, out_shape=[shp] * 6,
        scratch_shapes=[pltpu.VMEM((1, 128), F32)],
        compiler_params=_cp("arbitrary"),
    )(zm, zf, bf, tril, qkg, rope_c, rope_a, rope_b)


def _pair(ref, pi, lo_half):
    return jnp.where(lo_half, ref[2 * pi].astype(F32), pltpu.roll(ref[2 * pi + 1].astype(F32), HEAD_DIM, 1))


def _mid_fwd(oa, ob, zm, h0, p, w_out, w_pg, w_ple, g2, target=None):
    S = h0.shape[0]
    p, p_lead = p

    def body(oa_ref, ob_ref, ga_ref, gb_ref, h0_ref, p_ref, wo_ref, wg_ref, wp_ref, g2_ref, *rest):
        t_ref, rest = (rest[0], rest[1:]) if target is not None else (None, rest)
        y_ref, h1_ref, h2_ref, u2_ref, e_ref, gate_ref, *loss_ref = rest
        parts = []
        for o_ref, g_ref in ((oa_ref, ga_ref), (ob_ref, gb_ref)):
            for pi in range(N_HEADS // 2):
                g = g_ref[:, HEAD_PAD * pi:HEAD_PAD * (pi + 1)].astype(F32)
                parts.append((o_ref[pi] * (g * _sigmoid(g))).astype(BF16))
        y = jnp.concatenate(parts, axis=1)
        y_ref[...] = y
        h1 = h0_ref[...] + _dot(y, wo_ref[...])
        h1_ref[...] = h1
        r = lax.rsqrt(jnp.mean(h1 * h1, axis=-1, keepdims=True) + EPS)
        u2 = (h1 * r * g2_ref[...]).astype(BF16)
        u2_ref[...] = u2
        gate = _sigmoid(_dot(u2, wg_ref[...]))
        e = _dot(p_ref[...].astype(BF16), wp_ref[...])
        e_ref[...] = e.astype(BF16)
        gate_ref[...] = gate.astype(BF16)
        h2 = h1 + e * gate
        if target is None:
            h2_ref[...] = h2
        else:
            @pl.when(pl.program_id(0) == 0)
            def _():
                loss_ref[0][...] = jnp.zeros_like(loss_ref[0])

            err = h2 - t_ref[...]
            h2_ref[...] = err * (1.0 / D_MODEL)
            part = jnp.sum(jnp.sum(err * err, axis=1, keepdims=True), axis=0, keepdims=True)
            loss_ref[0][...] += part * (0.5 / D_MODEL)

    tok = lambda w: pl.BlockSpec((TOK_T, w), lambda i: (i, 0))
    head = pl.BlockSpec((N_HEADS // 2, TOK_T, HEAD_PAD), lambda i: (0, i, 0))
    full = lambda a, b: pl.BlockSpec((a, b), lambda i: (0, 0))
    act = lambda dt: jax.ShapeDtypeStruct((S, D_MODEL), dt)
    fused = target is not None
    return pl.pallas_call(
        body, name="mid_fwd_loss" if fused else "mid_fwd", grid=(S // TOK_T,),
        in_specs=[head, head,
                  pl.BlockSpec((TOK_T, D_BRANCH), lambda i: (i, 3)), pl.BlockSpec((TOK_T, D_BRANCH), lambda i: (i, 7)),
                  tok(D_MODEL), _slab_spec(p_lead, (TOK_T, PLE_DIM), lambda i: (i, 0)), full(D_MODEL, D_MODEL),
                  full(D_MODEL, D_MODEL), full(PLE_DIM, D_MODEL), full(1, D_MODEL)] + [tok(D_MODEL)] * fused,
        out_specs=[tok(D_MODEL)] * 6 + [full(8, 128)] * fused,
        out_shape=[act(BF16), act(F32), act(F32), act(BF16), act(BF16), act(BF16)]
        + [jax.ShapeDtypeStruct((8, 128), F32)] * fused,
        compiler_params=_cp("arbitrary" if fused else "parallel"),
    )(oa, ob, zm, zm, h0, p, w_out, w_pg, w_ple, g2, *([target] * fused))


def _bias_tables(full_range):
    T = ATT_T
    nb = 1 if full_range else DILATED_PATTERNS[-1][0] // T + 1
    r = lax.broadcasted_iota(jnp.int32, (nb, T, T), 2)
    c = lax.broadcasted_iota(jnp.int32, (nb, T, T), 1)
    b = lax.broadcasted_iota(jnp.int32, (nb, T, T), 0)
    delta = T * b + r - c
    if full_range:
        bias = jnp.where(delta >= 0, 0.0, NEG).astype(F32)
    else:
        mult = jnp.zeros((nb, T, T), F32)
        for window, dil in DILATED_PATTERNS:
            ok = (delta >= 0) & (delta <= window) & (delta % dil == 0)
            mult = mult + ok.astype(F32)
        bias = jnp.where(mult > 0, jnp.log2(jnp.maximum(mult, 1.0)), NEG).astype(F32)
    return bias


def _call_with_rider(body, name, grid, rider, in_specs, out_specs, out_shape, scratch_shapes, operands,
                     semantics=("parallel", "arbitrary")):
    if rider is None:
        return pl.pallas_call(body, name=name, grid=grid, in_specs=in_specs, out_specs=out_specs,
                              out_shape=out_shape, scratch_shapes=scratch_shapes,
                              compiler_params=_cp(*semantics))(*operands)
    n, n_in, n_out = rider.n, len(in_specs), len(out_specs)

    def wrapped(*refs):
        ins, r_ins = refs[:n_in], refs[n_in:n_in + n]
        outs, r_outs = refs[n_in + n:n_in + n + n_out], refs[n_in + n + n_out:n_in + 2 * n + n_out]
        scratch, sems = refs[n_in + 2 * n + n_out:-3], refs[-3:]
        step = [pl.program_id(a) for a in range(len(grid))]

        @pl.when(functools.reduce(jnp.logical_and, [s == 0 for s in step]))
        def _():
            rider.start(r_ins, r_outs, sems)

        body(*ins, *outs, *scratch)

        @pl.when(functools.reduce(jnp.logical_and, [s == g - 1 for s, g in zip(step, grid)]))
        def _():
            rider.wait(r_ins, r_outs, sems)

    return pl.pallas_call(
        wrapped, name=name, grid=grid, in_specs=list(in_specs) + rider.in_specs,
        out_specs=list(out_specs) + rider.out_specs, out_shape=list(out_shape) + rider.out_shapes,
        scratch_shapes=list(scratch_shapes) + rider.scratch_shapes,
        compiler_params=_cp(*["arbitrary"] * len(grid)))(*operands, *rider.arrays)


def _attn_fwd(q, k, v, table_t, full_range, name, rider=None):
    H, S, _ = q.shape
    T = ATT_T
    nb = table_t.shape[0]
    HB = ATT_FWD_HEADS
    KC = ATT_CHUNK
    chunks = [slice(c, c + KC) for c in range(0, T, KC)]
    fold = lambda x, op: functools.reduce(op, [x[r:r + 8] for r in range(0, KC, 8)])

    def body(q_ref, k_ref, v_ref, tab_ref, o_ref, lse_ref, *scratch):
        st_refs, pt_refs, acc_refs = scratch[:HB], scratch[HB:2 * HB], scratch[2 * HB:]
        i = pl.program_id(1)
        rows = lambda j: pl.ds(pl.multiple_of(j * T, T), T)

        def scores(hh, j):
            st_refs[hh][...] = _dot_nt(k_ref[hh, rows(j), :], q_ref[hh])

        def block(j, b, nxt, stats):
            out = []
            for hh, (m, l) in enumerate(stats):
                st_ref, pt_ref, acc_ref = st_refs[hh], pt_refs[hh], acc_refs[hh]
                mx = None
                for ch in chunks:
                    x = st_ref[ch, :]
                    if b is not None:
                        x = x + tab_ref[b, ch, :]
                        st_ref[ch, :] = x
                    x = fold(x, jnp.maximum)
                    mx = x if mx is None else jnp.maximum(mx, x)
                m_new = jnp.maximum(m, jnp.max(mx, axis=0, keepdims=True))
                alpha = jnp.exp2(m - m_new)
                ls = None
                for ch in chunks:
                    pc = jnp.exp2(st_ref[ch, :] - m_new)
                    pt_ref[ch, :] = pc.astype(BF16)
                    pc = fold(pc, jnp.add)
                    ls = pc if ls is None else ls + pc
                if nxt is not None:
                    scores(hh, nxt)
                acc_ref[...] = alpha * acc_ref[...] + _dot_tn(v_ref[hh, rows(j), :], pt_ref[...])
                out.append((m_new, alpha * l + jnp.sum(ls, axis=0, keepdims=True)))
            return tuple(out)

        lo = 0 if full_range else jnp.maximum(i - (nb - 1), 0)
        for hh in range(HB):
            acc_refs[hh][...] = jnp.zeros_like(acc_refs[hh])
            scores(hh, lo)
        stats = lax.fori_loop(lo, i, lambda j, st: block(j, None if full_range else i - j, j + 1, st),
                              ((jnp.full((1, T), NEG, F32), jnp.zeros((1, T), F32)),) * HB)
        stats = block(i, 0, None, stats)
        o_t = [acc_refs[hh][...] * (1.0 / l) for hh, (m, l) in enumerate(stats)]
        for hh, (m, l) in enumerate(stats):
            lse_ref[hh, 0] = m + jnp.log2(l)
        for hp in range(HB // 2):
            o_ref[hp] = jnp.concatenate([o_t[2 * hp][:HEAD_DIM], o_t[2 * hp + 1][:HEAD_DIM]], axis=0).T

    return _call_with_rider(
        body, name, (H // HB, S // T), rider,
        in_specs=[pl.BlockSpec((HB, T, HEAD_PAD), lambda h, i: (h, i, 0)),
                  pl.BlockSpec((HB, S, HEAD_PAD), lambda h, i: (h, 0, 0), pipeline_mode=pl.Buffered(1)),
                  pl.BlockSpec((HB, S, HEAD_PAD), lambda h, i: (h, 0, 0), pipeline_mode=pl.Buffered(1)),
                  pl.BlockSpec((nb, T, T), lambda h, i: (0, 0, 0), pipeline_mode=pl.Buffered(1))],
        out_specs=[pl.BlockSpec((HB // 2, T, HEAD_PAD), lambda h, i: (h, i, 0)),
                   pl.BlockSpec((HB, 1, 1, T), lambda h, i: (h, i, 0, 0))],
        out_shape=[jax.ShapeDtypeStruct((H // 2, S, HEAD_PAD), F32), jax.ShapeDtypeStruct((H, S // T, 1, T), F32)],
        scratch_shapes=([pltpu.VMEM((T, T), F32)] * HB + [pltpu.VMEM((T, T), BF16)] * HB
                        + [pltpu.VMEM((HEAD_PAD, T), F32)] * HB),
        operands=(q, k, v, table_t))


def _attn_bwd(q, k, v, do, lse, dd, table_t, full_range, name, rider=None, narrow=(False, False, False)):
    H, S, _ = q.shape
    T = ATT_T
    nq = S // T
    nb = table_t.shape[0]
    HB = ATT_BWD_HEADS
    KC = ATT_CHUNK
    chunks = [slice(c, c + KC) for c in range(0, T, KC)]

    def body(q_ref, do_ref, lse_ref, dd_ref, k_ref, v_ref, tab_ref, dq_hbm, dk_ref, dv_ref, *scratch):
        st_refs, dpt_refs, pt_refs, dst_refs = (scratch[n * HB:(n + 1) * HB] for n in range(4))
        dq_ref, dk_acc, dv_acc, *dq_cast, dq_sem = scratch[4 * HB:]
        h = pl.program_id(0)
        j = pl.program_id(1)

        @pl.when(j == 0)
        def _():
            dq_ref[...] = jnp.zeros_like(dq_ref)

        dk_acc[...] = jnp.zeros_like(dk_acc)
        dv_acc[...] = jnp.zeros_like(dv_acc)

        def step(i, b):
            rows = pl.ds(pl.multiple_of(i * T, T), T)
            for hh in range(HB):
                st_refs[hh][...] = _dot_nt(k_ref[hh], q_ref[hh, rows, :])
                dpt_refs[hh][...] = _dot_nt(v_ref[hh], do_ref[hh, rows, :])
            for hh in range(HB):
                lse_i = lse_ref[hh, i]
                dd_i = dd_ref[hh, i]
                for ch in chunks:
                    x = st_refs[hh][ch, :]
                    if b is not None:
                        x = x + tab_ref[b, ch, :]
                    pc = jnp.exp2(x - lse_i)
                    pt_refs[hh][ch, :] = pc.astype(BF16)
                    dst_refs[hh][ch, :] = (pc * (dpt_refs[hh][ch, :] - dd_i)).astype(BF16)
                dv_acc[hh] += _dot(pt_refs[hh][...], do_ref[hh, rows, :])
                dk_acc[hh] += _dot(dst_refs[hh][...], q_ref[hh, rows, :])
                dq_ref[hh, rows, :] += _dot_tn(dst_refs[hh][...], k_ref[hh])

        step(j, 0)
        if full_range:
            pl.loop(j + 1, nq)(lambda i: step(i, None))
        else:
            pl.loop(j + 1, jnp.minimum(j + nb, nq))(lambda i: step(i, i - j))
        dk_ref[...] = dk_acc[...].astype(dk_ref.dtype)
        dv_ref[...] = dv_acc[...].astype(dv_ref.dtype)

        @pl.when(j == nq - 1)
        def _():
            src = dq_ref
            if narrow[0]:
                src, = dq_cast
                src[...] = dq_ref[...].astype(BF16)
            out = pltpu.make_async_copy(src, dq_hbm.at[pl.ds(h * HB, HB)], dq_sem)
            out.start()
            out.wait()

    once = dict(pipeline_mode=pl.Buffered(1))
    per_head = pl.BlockSpec((HB, S, HEAD_PAD), lambda h, j: (h, 0, 0), **once)
    rows = pl.BlockSpec((HB, nq, 1, T), lambda h, j: (h, 0, 0, 0))
    blk = pl.BlockSpec((HB, T, HEAD_PAD), lambda h, j: (h, j, 0))
    shp = [jax.ShapeDtypeStruct((H, S, HEAD_PAD), BF16 if nar else F32) for nar in narrow]
    acc = pltpu.VMEM((HB, T, HEAD_PAD), F32)
    return _call_with_rider(
        body, name, (H // HB, nq), rider,
        in_specs=[per_head, per_head, rows, rows, blk, blk,
                  pl.BlockSpec((nb, T, T), lambda h, j: (0, 0, 0), **once)],
        out_specs=[pl.BlockSpec(memory_space=pltpu.HBM), blk, blk], out_shape=shp,
        scratch_shapes=([pltpu.VMEM((T, T), F32)] * (2 * HB) + [pltpu.VMEM((T, T), BF16)] * (2 * HB)
                        + [pltpu.VMEM((HB, S, HEAD_PAD), F32), acc, acc]
                        + [pltpu.VMEM((HB, S, HEAD_PAD), BF16)] * narrow[0] + [pltpu.SemaphoreType.DMA]),
        operands=(q, do, lse, dd, k, v, table_t))


def _mid_bwd(dh2, h1, e, gate, g2, w_pg, w_out, oa, ob, zm):
    S = dh2.shape[0]

    def body(dh2_ref, h1_ref, e_ref, gate_ref, g2_ref, wg_ref, wo_ref, oa_ref, ob_ref, ga_ref, gb_ref,
             dh1_ref, dh1b_ref, de_ref, dpre_ref, doa_ref, dob_ref, dga_ref, dgb_ref, dd_ref, dg2_ref):
        @pl.when(pl.program_id(0) == 0)
        def _():
            dg2_ref[...] = jnp.zeros_like(dg2_ref)

        lane = lax.broadcasted_iota(jnp.int32, (TOK_T, HEAD_PAD), 1)
        lo_half = lane < HEAD_DIM
        dh2 = dh2_ref[...]
        gate = gate_ref[...]
        de_ref[...] = (dh2 * gate).astype(BF16)
        dpre = (dh2 * e_ref[...] * gate * (1.0 - gate)).astype(BF16)
        dpre_ref[...] = dpre
        du2 = _dot_nt(dpre, wg_ref[...])
        h1 = h1_ref[...]
        r = lax.rsqrt(jnp.mean(h1 * h1, axis=-1, keepdims=True) + EPS)
        xh = h1 * r
        a = du2 * g2_ref[...]
        dh1 = dh2 + r * (a - xh * jnp.mean(a * xh, axis=-1, keepdims=True))
        dg2_ref[...] += jnp.sum(du2 * xh, axis=0, keepdims=True)
        dh1_ref[...] = dh1
        dh1b = dh1.astype(BF16)
        dh1b_ref[...] = dh1b
        dy = _dot_nt(dh1b, wo_ref[...])
        dd = jnp.zeros((TOK_T, HEAD_PAD), F32)
        for bi, (o_ref, g_ref, do_ref, dg_ref) in enumerate(
                ((oa_ref, ga_ref, doa_ref, dga_ref), (ob_ref, gb_ref, dob_ref, dgb_ref))):
            for pi in range(N_HEADS // 2):
                col = bi * D_BRANCH + HEAD_PAD * pi
                dyp = dy[:, col:col + HEAD_PAD]
                g = g_ref[:, HEAD_PAD * pi:HEAD_PAD * (pi + 1)].astype(F32)
                sg = _sigmoid(g)
                o_pair = o_ref[pi]
                dg_ref[:, HEAD_PAD * pi:HEAD_PAD * (pi + 1)] = (
                    dyp * o_pair * (sg * (1.0 + g * (1.0 - sg)))).astype(BF16)
                dop = dyp * (g * sg)
                prod = dop * o_pair
                for hh, d_head, mine in ((2 * pi, dop, lo_half),
                                         (2 * pi + 1, pltpu.roll(dop, HEAD_DIM, 1), ~lo_half)):
                    do_ref[hh] = jnp.where(lo_half, d_head, 0.0).astype(BF16)
                    dsum = jnp.sum(jnp.where(mine, prod, 0.0), axis=1, keepdims=True)
                    dd = dd + jnp.where(lane == bi * N_HEADS + hh, dsum, 0.0)
        dd_ref[...] = dd.T[:2 * N_HEADS, :]

    tok = lambda w: pl.BlockSpec((TOK_T, w), lambda i: (i, 0))
    head = pl.BlockSpec((N_HEADS, TOK_T, HEAD_PAD), lambda i: (0, i, 0))
    pairs = pl.BlockSpec((N_HEADS // 2, TOK_T, HEAD_PAD), lambda i: (0, i, 0))
    full = lambda a, b: pl.BlockSpec((a, b), lambda i: (0, 0))
    act = lambda w, dt: jax.ShapeDtypeStruct((S, w), dt)
    hshape = lambda w, dt: jax.ShapeDtypeStruct((N_HEADS, S, w), dt)
    return pl.pallas_call(
        body, name="mid_bwd", grid=(S // TOK_T,),
        in_specs=[tok(D_MODEL)] * 4 + [full(1, D_MODEL), full(D_MODEL, D_MODEL), full(D_MODEL, D_MODEL), pairs, pairs,
                                      pl.BlockSpec((TOK_T, D_BRANCH), lambda i: (i, 3)),
                                      pl.BlockSpec((TOK_T, D_BRANCH), lambda i: (i, 7))],
        out_specs=[tok(D_MODEL)] * 4 + [head, head, tok(D_BRANCH), tok(D_BRANCH),
                                       pl.BlockSpec((2 * N_HEADS, TOK_T), lambda i: (0, i)), full(1, D_MODEL)],
        out_shape=[act(D_MODEL, F32), act(D_MODEL, BF16), act(D_MODEL, BF16), act(D_MODEL, BF16),
                   hshape(HEAD_PAD, BF16), hshape(HEAD_PAD, BF16), act(D_BRANCH, BF16), act(D_BRANCH, BF16),
                   jax.ShapeDtypeStruct((2 * N_HEADS, S), F32), jax.ShapeDtypeStruct((1, D_MODEL), F32)],
        compiler_params=_cp("arbitrary"),
    )(dh2, h1, e, gate, g2, w_pg, w_out, oa, ob, zm, zm)


def _prep_bwd(dqa, dka, dva, dqb, dkb, dvb, zm, qkg, rope_c, rope_a, rope_b, dga, dgb, zf, bf, triu):
    S = zm.shape[0]
    n = S // TOK_T

    def body(dqa_ref, dka_ref, dva_ref, dqb_ref, dkb_ref, dvb_ref, z_ref, g_ref, rc_ref, ra_ref, rb_ref,
             dga_ref, dgb_ref, zf_ref, b_ref, tri_ref, dz_ref, dzf_ref, dqkg_ref, db_ref, carry):
        @pl.when(pl.program_id(0) == 0)
        def _():
            dqkg_ref[...] = jnp.zeros_like(dqkg_ref)
            db_ref[...] = jnp.zeros_like(db_ref)
            carry[...] = jnp.zeros_like(carry)

        lane = lax.broadcasted_iota(jnp.int32, (TOK_T, HEAD_PAD), 1)
        lo_half = lane < HEAD_DIM
        rc, ra, rb = rc_ref[...], ra_ref[...], rb_ref[...]
        same_head = _same_head()

        def unrope(dy):
            return dy * rc + pltpu.roll(dy * ra, ROPE_HALF, 1) + pltpu.roll(dy * rb, HEAD_PAD - ROPE_HALF, 1)

        def norm_bwd(col, gi, dy):
            x = z_ref[:, col:col + HEAD_PAD].astype(F32)
            r = _pair_rsqrt(x, same_head)
            xh = x * r
            dqkg_ref[gi:gi + 1, :] += jnp.sum(dy * xh, axis=0, keepdims=True)
            a = dy * g_ref[gi:gi + 1, :]
            dz_ref[:, col:col + HEAD_PAD] = (r * (a - xh * _pair_mean(a * xh, same_head))).astype(BF16)

        dc = jnp.zeros((TOK_T, HEAD_PAD), F32)
        for pi in range(N_HEADS // 2):
            col = HEAD_PAD * pi
            norm_bwd(col, 0, _pair(dqa_ref, pi, lo_half) * SCALE)
            norm_bwd(D_BRANCH + col, 1, _pair(dka_ref, pi, lo_half) * LN2)
            dz_ref[:, 2 * D_BRANCH + col:2 * D_BRANCH + col + HEAD_PAD] = _pair(dva_ref, pi, lo_half).astype(BF16)
            norm_bwd(4 * D_BRANCH + col, 2, unrope(_pair(dqb_ref, pi, lo_half) * SCALE))
            norm_bwd(5 * D_BRANCH + col, 3, unrope(_pair(dkb_ref, pi, lo_half) * LN2))
            dz_ref[:, 6 * D_BRANCH + col:6 * D_BRANCH + col + HEAD_PAD] = _pair(dvb_ref, pi, lo_half).astype(BF16)
            for hh in (2 * pi, 2 * pi + 1):
                dch = dka_ref[hh][:, HEAD_DIM:HEAD_DIM + 1] + dqa_ref[hh][:, HEAD_DIM + 3:HEAD_DIM + 4]
                dc = dc + jnp.where(lane == hh, dch, 0.0)
        dz_ref[:, 3 * D_BRANCH:4 * D_BRANCH] = dga_ref[...]
        dz_ref[:, 7 * D_BRANCH:8 * D_BRANCH] = dgb_ref[...]
        tri = tri_ref[...]
        dlf = sum(_dot(tri, part) for part in _split3(dc)) + carry[...]
        carry[...] = dlf[0:1, :]
        dfa = dlf * (1.0 - _sigmoid(zf_ref[...] + b_ref[...]))
        dzf_ref[...] = dfa.astype(BF16)
        db_ref[...] += jnp.sum(dfa, axis=0, keepdims=True)

    tok = lambda w: pl.BlockSpec((TOK_T, w), lambda i: (n - 1 - i, 0))
    head = pl.BlockSpec((N_HEADS, TOK_T, HEAD_PAD), lambda i: (0, n - 1 - i, 0))
    fixed = lambda a, b: pl.BlockSpec((a, b), lambda i: (0, 0))
    return pl.pallas_call(
        body, name="prep_bwd", grid=(n,),
        in_specs=[head] * 6 + [tok(N_MAIN), fixed(4, 128), tok(128), tok(128), tok(128), tok(D_BRANCH), tok(D_BRANCH),
                               tok(128), fixed(1, 128), fixed(TOK_T, TOK_T)],
        out_specs=[tok(N_MAIN), tok(128), fixed(4, 128), fixed(1, 128)],
        out_shape=[jax.ShapeDtypeStruct((S, N_MAIN), BF16), jax.ShapeDtypeStruct((S, 128), BF16),
                   jax.ShapeDtypeStruct((4, 128), F32), jax.ShapeDtypeStruct((1, 128), F32)],
        scratch_shapes=[pltpu.VMEM((1, 128), F32)],
        compiler_params=_cp("arbitrary"),
    )(dqa, dka, dva, dqb, dkb, dvb, zm, qkg, rope_c, rope_a, rope_b, dga, dgb, zf, bf, triu)


def _inproj_bwd(dzm, dzf, wm, wf, h0, dh1, g, rider=None):
    S = h0.shape[0]

    def body(dzm_ref, dzf_ref, wm_ref, wf_ref, h_ref, dh1_ref, g_ref, dh0_ref, dg_ref):
        @pl.when(pl.program_id(0) == 0)
        def _():
            dg_ref[...] = jnp.zeros_like(dg_ref)

        du = _dot_nt(dzm_ref[...], wm_ref[...]) + _dot_nt(dzf_ref[...], wf_ref[...])
        x = h_ref[...]
        r = lax.rsqrt(jnp.mean(x * x, axis=-1, keepdims=True) + EPS)
        xh = x * r
        a = du * g_ref[...]
        dh0_ref[...] = dh1_ref[...] + r * (a - xh * jnp.mean(a * xh, axis=-1, keepdims=True))
        dg_ref[...] += jnp.sum(du * xh, axis=0, keepdims=True)

    tok = lambda w: pl.BlockSpec((TOK_T, w), lambda i: (i, 0))
    full = lambda a, b: pl.BlockSpec((a, b), lambda i: (0, 0))
    return _call_with_rider(
        body, "inproj_bwd", (S // TOK_T,), rider,
        in_specs=[tok(N_MAIN), tok(128), full(D_MODEL, N_MAIN), full(D_MODEL, 128), tok(D_MODEL), tok(D_MODEL),
                  full(1, D_MODEL)],
        out_specs=[tok(D_MODEL), full(1, D_MODEL)],
        out_shape=[jax.ShapeDtypeStruct((S, D_MODEL), F32), jax.ShapeDtypeStruct((1, D_MODEL), F32)],
        scratch_shapes=[], operands=(dzm, dzf, wm, wf, h0, dh1, g), semantics=("arbitrary",))


def _wgrad(a, b, name, a_lead=()):
    S, M = a.shape[len(a_lead):]
    N = b.shape[1]
    tn = min(N, 2048)
    ts = 512
    last = S // ts - 1

    def body(a_ref, b_ref, o_ref, acc_ref):
        @pl.when(pl.program_id(1) == 0)
        def _():
            acc_ref[...] = jnp.zeros_like(acc_ref)

        acc_ref[...] += _dot_tn(a_ref[...].astype(BF16), b_ref[...])

        @pl.when(pl.program_id(1) == last)
        def _():
            o_ref[...] = acc_ref[...].astype(BF16)

    return pl.pallas_call(
        body, name=name, grid=(N // tn, S // ts),
        in_specs=[_slab_spec(a_lead, (ts, M), lambda n, s: (s, 0)), pl.BlockSpec((ts, tn), lambda n, s: (s, n))],
        out_specs=pl.BlockSpec((M, tn), lambda n, s: (0, n)),
        out_shape=jax.ShapeDtypeStruct((M, N), BF16),
        scratch_shapes=[pltpu.VMEM((M, tn), F32)],
        compiler_params=_cp("parallel", "arbitrary"),
    )(a, b)


def _rope_tables(positions):
    inv_freq = ROPE_THETA ** (-jnp.arange(ROPE_HALF, dtype=F32) / ROPE_HALF)
    ang = positions.astype(F32)[:, None] * inv_freq
    cos, sin = jnp.cos(ang), jnp.sin(ang)
    S = positions.shape[0]
    one, zero = jnp.ones((S, HEAD_DIM - 2 * ROPE_HALF), F32), jnp.zeros((S, HEAD_DIM - 2 * ROPE_HALF), F32)
    z8 = jnp.zeros((S, ROPE_HALF), F32)
    rc = jnp.concatenate([cos, cos, one], axis=1)
    ra = jnp.concatenate([-sin, z8, zero], axis=1)
    rb = jnp.concatenate([z8, sin, zero], axis=1)
    return tuple(jnp.tile(t, (1, 2)) for t in (rc, ra, rb))


def _in_operands(w_in):
    w_in = w_in.astype(BF16)
    wm = jnp.concatenate([w_in[:, :4 * D_BRANCH], w_in[:, 4 * D_BRANCH + N_HEADS:]], axis=1)
    wf = jnp.pad(w_in[:, 4 * D_BRANCH:4 * D_BRANCH + N_HEADS], ((0, 0), (0, 128 - N_HEADS)))
    return dict(wm=wm, wf=wf)


def _layer_weights(w_in, w_out, w_ple, w_pg):
    return dict(_in_operands(w_in), w_out=w_out.astype(BF16), w_ple=w_ple.astype(BF16), w_pg=w_pg.astype(BF16))


def _row(v, width=128):
    v = v.reshape(1, -1).astype(F32)
    return jnp.pad(v, ((0, 0), (0, width - v.shape[1])))


def _layer_fwd(h0, p, rope, tabs, w, norm_g, b_f, qk_g, ple_g, rider=None, late=None, target=None):
    g1 = norm_g.reshape(1, D_MODEL)
    g2 = ple_g.reshape(1, D_MODEL)
    qkg = jnp.tile(qk_g, (1, 2))
    bf = _row(b_f)
    zm, zf, u = _inproj_fwd(h0, g1, w["wm"], w["wf"])
    qa, ka, va, qb, kb, vb = _prep_fwd(zm, zf, bf, tabs["tril"], qkg, *rope)
    oa, lse_a, *arrivals = _attn_fwd(qa, ka, va, tabs["fox"], True, "fox_fwd", rider)
    if late is not None:
        w = {**w, **late(arrivals)}
    ob, lse_b = _attn_fwd(qb, kb, vb, tabs["dil"], False, "dil_fwd")
    y, h1, h2, u2, e, gate, *loss = _mid_fwd(oa, ob, zm, h0, p, w["w_out"], w["w_pg"], w["w_ple"], g2, target)
    saved = dict(h0=h0, p=p, zm=zm, zf=zf, u=u, qa=qa, ka=ka, va=va, qb=qb, kb=kb, vb=vb, oa=oa, ob=ob,
                 lse_a=lse_a, lse_b=lse_b, y=y, h1=h1, u2=u2, e=e, gate=gate, g1=g1, g2=g2, qkg=qkg, bf=bf, w=w)
    return (h2, saved, arrivals, *loss)


def _layer_bwd(dh2, sv, rope, tabs, make_rider=None, make_last_rider=None):
    S = dh2.shape[0]
    nq = S // ATT_T
    w = sv["w"]
    rows = lambda a: a.reshape(N_HEADS, nq, 1, ATT_T)
    (dh1, dh1b, de, dpre, doa, dob, dga, dgb, dd, dg2) = _mid_bwd(
        dh2, sv["h1"], sv["e"], sv["gate"], sv["g2"], w["w_pg"], w["w_out"], sv["oa"], sv["ob"], sv["zm"])
    dda, ddb = dd[:N_HEADS], dd[N_HEADS:]
    early = dict(w_out=_wgrad(sv["y"], dh1b, "wgrad_out"), w_ple=_wgrad(sv["p"][0], de, "wgrad_ple", sv["p"][1]),
                 w_ple_gate=_wgrad(sv["u2"], dpre, "wgrad_gate"))
    rider = None if make_rider is None else make_rider(early)
    dqa, dka, dva, *arrivals = _attn_bwd(sv["qa"], sv["ka"], sv["va"], doa, sv["lse_a"], rows(dda), tabs["fox"],
                                         True, "fox_bwd", rider, narrow=(False, False, True))
    dqb, dkb, dvb = _attn_bwd(sv["qb"], sv["kb"], sv["vb"], dob, sv["lse_b"], rows(ddb), tabs["dil"], False,
                              "dil_bwd", narrow=(False, True, True))
    dzm, dzf, dqkg, dbf = _prep_bwd(dqa, dka, dva, dqb, dkb, dvb, sv["zm"], sv["qkg"], *rope, dga, dgb,
                                    sv["zf"], sv["bf"], tabs["triu"])
    dwm = _wgrad(sv["u"], dzm, "wgrad_in")
    dwf = _wgrad(sv["u"], dzf, "wgrad_f")
    dw_in = jnp.concatenate([dwm[:, :4 * D_BRANCH], dwf[:, :N_HEADS], dwm[:, 4 * D_BRANCH:]], axis=1)
    last_rider = None if make_last_rider is None else make_last_rider(dw_in)
    dh0, dg1, *last_arrivals = _inproj_bwd(dzm, dzf, w["wm"], w["wf"], sv["h0"], dh1, sv["g1"], last_rider)
    grads = dict(norm_g=dg1.reshape(D_MODEL), w_in=dw_in, b_f=dbf[0, :N_HEADS],
                 qk_norm_g=dqkg[:, :HEAD_DIM] + dqkg[:, HEAD_DIM:], ple_norm_g=dg2.reshape(D_MODEL), **early)
    return dh0, grads, arrivals + last_arrivals


def _tables():
    T = TOK_T
    r = lax.broadcasted_iota(jnp.int32, (T, T), 0)
    c = lax.broadcasted_iota(jnp.int32, (T, T), 1)
    return dict(fox=_bias_tables(True), dil=_bias_tables(False),
                tril=(c <= r).astype(BF16), triu=(c >= r).astype(BF16))


def _local_step(x, p, positions, target, layers, small):
    rope = _rope_tables(positions)
    tabs = _tables()
    ws = [_layer_weights(*lw) for lw in layers]
    h = x
    saved = []
    for li, (w, lp, sm) in enumerate(zip(ws, p, small)):
        h, sv, _, *loss = _layer_fwd(h, (lp, ()), rope, tabs, w, *sm, target=target if li == len(ws) - 1 else None)
        saved.append(sv)
    dh, (loss,) = h, loss
    grads = [None] * len(ws)
    for li in reversed(range(len(ws))):
        dh, grads[li], _ = _layer_bwd(dh, saved[li], rope, tabs)
    return loss[0, 0], dh, grads


def _peers():
    x, y, c = lax.axis_index("x"), lax.axis_index("y"), lax.axis_index("c")
    me = 4 * x + 2 * y + c
    flip = lambda v, bit: 1 - v if bit else v
    return me, [(flip(x, k & 4), flip(y, k & 2), flip(c, k & 1)) for k in range(1, N_DEV)]


def _sel(ref, kind, d):
    if kind == "whole":
        return ref
    if kind == "slot":
        return ref.at[d]
    block = pl.ds(pl.multiple_of(d * 128, 128), 128)
    return ref.at[block, :] if kind == "rows" else ref.at[:, block]


class _Pushes:
    def __init__(self, arrays, src_kinds, dst_kinds, out_shapes):
        self.arrays, self.n = list(arrays), len(arrays)
        self.src_kinds, self.dst_kinds = src_kinds, dst_kinds
        self.out_shapes = [jax.ShapeDtypeStruct(s, a.dtype) for s, a in zip(out_shapes, arrays)]
        hbm = pl.BlockSpec(memory_space=pltpu.HBM)
        self.in_specs, self.out_specs = [hbm] * self.n, [hbm] * self.n
        self.scratch_shapes = [pltpu.SemaphoreType.DMA((N_DEV - 1, self.n)),
                               pltpu.SemaphoreType.DMA((N_DEV - 1, self.n)), pltpu.SemaphoreType.DMA((self.n,))]

    def _copies(self, ins, outs, sems):
        send_sems, recv_sems, local_sems = sems
        me, peers = _peers()
        src = lambda a, d: _sel(ins[a], self.src_kinds[a], d)
        dst = lambda a: _sel(outs[a], self.dst_kinds[a], me)
        local = [pltpu.make_async_copy(src(a, me), dst(a), local_sems.at[a]) for a in range(self.n)]
        remote = [pltpu.make_async_remote_copy(
            src_ref=src(a, 4 * px + 2 * py + pc), dst_ref=dst(a), send_sem=send_sems.at[k, a],
            recv_sem=recv_sems.at[k, a], device_id=(px, py, pc), device_id_type=pl.DeviceIdType.MESH)
            for k, (px, py, pc) in enumerate(peers) for a in range(self.n)]
        return local + remote

    def start(self, ins, outs, sems):
        for cp in self._copies(ins, outs, sems):
            cp.start()

    def wait(self, ins, outs, sems):
        for cp in self._copies(ins, outs, sems):
            cp.wait()


def _exchange(name, pushes):
    n = pushes.n

    def body(*refs):
        pushes.start(refs[:n], refs[n:2 * n], refs[2 * n:])
        pushes.wait(refs[:n], refs[n:2 * n], refs[2 * n:])

    return pl.pallas_call(body, name=name, in_specs=pushes.in_specs, out_specs=pushes.out_specs,
                          out_shape=pushes.out_shapes, scratch_shapes=pushes.scratch_shapes)(*pushes.arrays)


def _gather_two_level(shard, name):
    def body(x_ref, out_ref, send_sems, recv_sems, local_sem):
        x, y, c = lax.axis_index("x"), lax.axis_index("y"), lax.axis_index("c")
        me, sibling = (x, y, c), (x, y, 1 - c)
        chips = [(1 - x, y), (x, 1 - y), (1 - x, 1 - y)]
        slot = lambda px, py, pc: out_ref.at[4 * px + 2 * py + pc]

        def copy(k, block, to, src=None):
            return pltpu.make_async_remote_copy(
                src_ref=slot(*block) if src is None else src, dst_ref=slot(*block), send_sem=send_sems.at[k],
                recv_sem=recv_sems.at[k], device_id=to, device_id_type=pl.DeviceIdType.MESH)

        mine = pltpu.make_async_copy(x_ref, slot(*me), local_sem)
        mine.start()
        first = [copy(0, me, sibling, src=x_ref)] + [copy(1 + j, me, (*chip, c), src=x_ref)
                                                     for j, chip in enumerate(chips)]
        for cp in first:
            cp.start()
        passed = [copy(4 + j, (*chip, c), sibling) for j, chip in enumerate(chips)]
        for j, chip in enumerate(chips):
            copy(1 + j, (*chip, c), me).wait_recv()
            passed[j].start()
        copy(0, sibling, me).wait_recv()
        for j, chip in enumerate(chips):
            copy(4 + j, (*chip, 1 - c), me).wait_recv()
        for cp in first + passed:
            cp.wait_send()
        mine.wait()

    hbm = pl.BlockSpec(memory_space=pltpu.HBM)
    return pl.pallas_call(
        body, name=name, in_specs=[hbm], out_specs=hbm,
        out_shape=jax.ShapeDtypeStruct((N_DEV,) + shard.shape, shard.dtype),
        scratch_shapes=[pltpu.SemaphoreType.DMA((N_DEV - 1,)), pltpu.SemaphoreType.DMA((N_DEV - 1,)),
                        pltpu.SemaphoreType.DMA],
    )(shard)


def _gather_pushes(shards, kinds):
    full = {"slot": lambda s: (N_DEV,) + s, "rows": lambda s: (N_DEV * s[0], s[1]),
            "cols": lambda s: (s[0], N_DEV * s[1])}
    return _Pushes(shards, ["whole"] * len(shards), kinds, [full[k](a.shape) for a, k in zip(shards, kinds)])


def _scatter_pushes(partials, kinds):
    part = {"slot": lambda s: s[1:], "rows": lambda s: (128, s[1]), "cols": lambda s: (s[0], 128),
            "whole": lambda s: s}
    return _Pushes(partials, kinds, ["slot"] * len(partials),
                   [(N_DEV,) + part[k](a.shape) for a, k in zip(partials, kinds)])


def _adamw(name, parts, w, m, v, rows):
    L, R, C = w.shape

    def body(p_ref, w_ref, m_ref, v_ref, g_ref, d_ref, nm_ref, nv_ref):
        g = p_ref[0, 0].astype(F32)
        for s in range(1, N_DEV):
            g = g + p_ref[s, 0].astype(F32)
        g_ref[0] = g
        nm = ADAM_B1 * m_ref[0] + (1.0 - ADAM_B1) * g
        nv = ADAM_B2 * v_ref[0] + (1.0 - ADAM_B2) * (g * g)
        nm_ref[0] = nm
        nv_ref[0] = nv
        m_hat = nm / (1.0 - ADAM_B1 ** ADAM_STEP)
        v_hat = nv / (1.0 - ADAM_B2 ** ADAM_STEP)
        d_ref[0] = -ADAM_LR * (m_hat / (jnp.sqrt(v_hat) + ADAM_EPS) + ADAM_WD * w_ref[0])

    blk = pl.BlockSpec((1, rows, C), lambda l, i: (l, i, 0))
    shp = jax.ShapeDtypeStruct((L, R, C), F32)
    return pl.pallas_call(
        body, name=name, grid=(L, R // rows),
        in_specs=[pl.BlockSpec((N_DEV, 1, rows, C), lambda l, i: (0, l, i, 0)), blk, blk, blk],
        out_specs=[blk] * 4, out_shape=[shp] * 4,
        compiler_params=_cp("parallel", "parallel"),
    )(parts, w, m, v)


SMALL_ROWS = 40
LOSS_ROW = 37


def _pack_small(norm_g, ple_g, qk_g, b_f, last_row):
    rows = lambda a: a.astype(F32).reshape(-1, 128)
    flat = jnp.concatenate([rows(norm_g), rows(ple_g), rows(qk_g), _row(b_f.reshape(-1)), last_row], axis=0)
    return jnp.pad(flat, ((0, SMALL_ROWS - flat.shape[0]), (0, 0)))


def _unpack_small(flat):
    return (flat[0:16].reshape(2, D_MODEL), flat[16:32].reshape(2, D_MODEL), flat[32:36].reshape(2, 4, HEAD_DIM),
            flat[36, :2 * N_HEADS].reshape(2, N_HEADS))


def kernel(x, p, positions, norm_g, w_in, b_f, qk_norm_g, w_out, w_ple, ple_norm_g, w_ple_gate, loss_target, m_norm_g, m_w_in, m_b_f, m_qk_norm_g, m_w_out, m_w_ple, m_ple_norm_g, m_w_ple_gate, v_norm_g, v_w_in, v_b_f, v_qk_norm_g, v_w_out, v_w_ple, v_ple_norm_g, v_w_ple_gate):
    bf16 = lambda a: a.astype(BF16)
    rows_in = W_IN_ROWS // 2
    flat_in = lambda a: bf16(a).reshape(rows_in, 128)
    full_in = lambda g: g.reshape(N_DEV, D_MODEL, W_IN_SHARD).transpose(1, 0, 2).reshape(D_MODEL, N_IN)
    small = [(norm_g[l], b_f[l], qk_norm_g[l], ple_norm_g[l]) for l in range(2)]
    rope = _rope_tables(positions[0])
    tabs = _tables()

    g_in0 = _gather_two_level(flat_in(w_in[0]), "gather_first")
    rest = _gather_pushes([flat_in(w_in[1])] + [bf16(a[l]) for l in range(2) for a in (w_out, w_ple, w_ple_gate)],
                          ["slot"] + ["rows", "cols", "rows"] * 2)
    late = lambda got: dict(w_out=got[1], w_ple=got[2], w_pg=got[3])
    h, sv0, got = _layer_fwd(x[0], (p, (0, 0)), rope, tabs, _in_operands(full_in(g_in0)), *small[0], rest, late)
    w1 = dict(_in_operands(full_in(got[0])), w_out=got[4], w_ple=got[5], w_pg=got[6])
    dh, sv1, _, loss = _layer_fwd(h, (p, (1, 0)), rope, tabs, w1, *small[1], target=loss_target[0])
    dh, gr1, _ = _layer_bwd(dh, sv1, rope, tabs)

    by_dest = lambda d: d.reshape(D_MODEL, N_DEV, W_IN_SHARD).transpose(1, 0, 2).reshape(N_DEV, rows_in, 128)
    big = ("w_out", "w_ple", "w_ple_gate")
    riding = lambda early: _scatter_pushes([by_dest(gr1["w_in"])] + [gr1[n] for n in big] + [early[n] for n in big],
                                           ["slot"] + ["rows", "cols", "rows"] * 2)
    riding_last = lambda dw_in: _scatter_pushes([by_dest(dw_in)], ["slot"])
    dx, gr0, (r_in1, *r_big, r_in0) = _layer_bwd(dh, sv0, rope, tabs, riding, riding_last)
    grads = (gr0, gr1)
    stack = lambda name: jnp.stack([gl[name] for gl in grads], axis=0)
    small_part = _pack_small(stack("norm_g"), stack("ple_norm_g"), stack("qk_norm_g"), stack("b_f"),
                             _row(loss[0, 0].reshape(1)))
    (r_small,) = _exchange("exchange_small", _scatter_pushes([small_part], ["whole"]))
    r_in = jnp.concatenate([r_in0, r_in1], axis=1)
    r_out, r_ple, r_pg = (jnp.stack([r_big[3 + k], r_big[k]], axis=1) for k in range(3))

    zero_row = jnp.zeros((1, 128), F32)
    small_of = lambda ng, pg, qk, bf: _pack_small(ng, pg, qk, bf, zero_row)[None]
    flat = lambda a: a.reshape(1, W_IN_ROWS, 128)
    outs = dict(
        w_in=[o.reshape(w_in.shape) for o in
              _adamw("adamw_in", r_in[:, None], flat(w_in), flat(m_w_in), flat(v_w_in), W_IN_TILE)],
        w_out=_adamw("adamw_out", r_out, w_out, m_w_out, v_w_out, 128),
        w_ple=_adamw("adamw_ple", r_ple, w_ple, m_w_ple, v_w_ple, 256),
        w_pg=_adamw("adamw_gate", r_pg, w_ple_gate, m_w_ple_gate, v_w_ple_gate, 128),
        small=_adamw("adamw_small", r_small[:, None], small_of(norm_g, ple_norm_g, qk_norm_g, b_f),
                     small_of(m_norm_g, m_ple_norm_g, m_qk_norm_g, m_b_f),
                     small_of(v_norm_g, v_ple_norm_g, v_qk_norm_g, v_b_f), SMALL_ROWS))
    leaves = []
    for kind in range(4):
        ng, pg, qk, bf = _unpack_small(outs["small"][kind][0])
        leaves += [ng, outs["w_in"][kind], bf, qk, outs["w_out"][kind], outs["w_ple"][kind], pg, outs["w_pg"][kind]]
    return (outs["small"][0][0, LOSS_ROW, 0], dx[None], *leaves)
```

```python
import functools

import jax
import jax.numpy as jnp
from jax import lax
from jax.experimental import pallas as pl
from jax.experimental.pallas import tpu as pltpu

F32 = jnp.float32
BF16 = jnp.bfloat16

D_MODEL = 1024
HEAD_DIM = 64
N_HEADS = 8
HEAD_PAD = 128
D_BRANCH = N_HEADS * HEAD_DIM
N_MAIN = 8 * D_BRANCH
N_IN = N_MAIN + N_HEADS
PLE_DIM = 256
ROPE_THETA = 500000.0
ROPE_HALF = 8
EPS = 1e-6
NEG = -1e30
SCALE = HEAD_DIM ** -0.5
LOG2E = 1.4426950408889634
LN2 = 0.6931471805599453
DILATED_PATTERNS = ((128, 1), (512, 4), (2048, 16))
N_DEV = 8
W_IN_SHARD = N_IN // N_DEV
W_IN_ROWS = 2 * D_MODEL * W_IN_SHARD // 128
W_IN_TILE = W_IN_ROWS // 19

ADAM_LR = 0.001
ADAM_B1 = 0.9
ADAM_B2 = 0.999
ADAM_EPS = 1e-08
ADAM_WD = 0.01
ADAM_STEP = 10

ATT_T = 512
ATT_FWD_HEADS = 8
ATT_BWD_HEADS = 4
ATT_CHUNK = 32
ATT_CHUNK_CAUSAL_FWD = 16
TOK_T = 256
VMEM_LIMIT = 60 * 1024 * 1024


def _slab_spec(lead, block, index):
    return pl.BlockSpec((None,) * len(lead) + block, lambda *g: (*lead, *index(*g)))


def _cp(*sem):
    return pltpu.CompilerParams(dimension_semantics=sem, vmem_limit_bytes=VMEM_LIMIT)


def _sigmoid(x):
    return 1.0 / (1.0 + jnp.exp(-x))


def _split3(x):
    hi = x.astype(BF16)
    r1 = x - hi.astype(F32)
    mid = r1.astype(BF16)
    lo = (r1 - mid.astype(F32)).astype(BF16)
    return hi, mid, lo


def _dot(a, b):
    return jnp.dot(a, b, preferred_element_type=F32)


def _dot_nt(a, b):
    return lax.dot_general(a, b, (((1,), (1,)), ((), ())), preferred_element_type=F32)


def _dot_tn(a, b):
    return lax.dot_general(a, b, (((0,), (0,)), ((), ())), preferred_element_type=F32)


def _inproj_fwd(h, g, wm, wf):
    S = h.shape[0]

    def body(h_ref, g_ref, wm_ref, wf_ref, zm_ref, zf_ref, u_ref):
        x = h_ref[...]
        r = lax.rsqrt(jnp.mean(x * x, axis=-1, keepdims=True) + EPS)
        u = (x * r * g_ref[...]).astype(BF16)
        u_ref[...] = u
        zm_ref[...] = _dot(u, wm_ref[...]).astype(BF16)
        zf_ref[...] = _dot(u, wf_ref[...])

    return pl.pallas_call(
        body, name="inproj_fwd", grid=(S // TOK_T,),
        in_specs=[pl.BlockSpec((TOK_T, D_MODEL), lambda i: (i, 0)),
                  pl.BlockSpec((1, D_MODEL), lambda i: (0, 0)),
                  pl.BlockSpec((D_MODEL, N_MAIN), lambda i: (0, 0)),
                  pl.BlockSpec((D_MODEL, 128), lambda i: (0, 0))],
        out_specs=[pl.BlockSpec((TOK_T, N_MAIN), lambda i: (i, 0)),
                   pl.BlockSpec((TOK_T, 128), lambda i: (i, 0)),
                   pl.BlockSpec((TOK_T, D_MODEL), lambda i: (i, 0))],
        out_shape=[jax.ShapeDtypeStruct((S, N_MAIN), BF16), jax.ShapeDtypeStruct((S, 128), F32),
                   jax.ShapeDtypeStruct((S, D_MODEL), BF16)],
        compiler_params=_cp("parallel"),
    )(h, g, wm, wf)


def _log_sigmoid(x):
    return jnp.minimum(x, 0.0) - jnp.log(1.0 + jnp.exp(-jnp.abs(x)))


def _same_head():
    r = lax.broadcasted_iota(jnp.int32, (HEAD_PAD, HEAD_PAD), 0) // HEAD_DIM
    c = lax.broadcasted_iota(jnp.int32, (HEAD_PAD, HEAD_PAD), 1) // HEAD_DIM
    return (r == c).astype(BF16)


def _pair_mean(x, same_head):
    hi = x.astype(BF16)
    lo = (x - hi.astype(F32)).astype(BF16)
    return (_dot(hi, same_head) + _dot(lo, same_head)) * (1.0 / HEAD_DIM)


def _pair_rsqrt(x, same_head):
    return lax.rsqrt(_pair_mean(x * x, same_head) + EPS)


def _prep_fwd(zm, zf, bf, tril, qkg, rope_c, rope_a, rope_b):
    S = zm.shape[0]
    shp = jax.ShapeDtypeStruct((N_HEADS, S, HEAD_PAD), BF16)

    def body(z_ref, zf_ref, b_ref, tri_ref, g_ref, rc_ref, ra_ref, rb_ref,
             qa_ref, ka_ref, va_ref, qb_ref, kb_ref, vb_ref, carry):
        @pl.when(pl.program_id(0) == 0)
        def _():
            carry[...] = jnp.zeros_like(carry)

        tri = tri_ref[...]
        cs = sum(_dot(tri, part) for part in _split3(_log_sigmoid(zf_ref[...] + b_ref[...]))) + carry[...]
        carry[...] = cs[TOK_T - 1:TOK_T, :]
        lane = lax.broadcasted_iota(jnp.int32, (TOK_T, HEAD_PAD), 1)
        lo_half = lane < HEAD_DIM
        aug = (lane >= HEAD_DIM) & (lane < HEAD_DIM + 3)
        q_pad = jnp.where(aug, -1.0, 0.0)
        rc, ra, rb = rc_ref[...], ra_ref[...], rb_ref[...]
        same_head = _same_head()

        def norm(col, gi):
            x = z_ref[:, col:col + HEAD_PAD].astype(F32)
            return x * _pair_rsqrt(x, same_head) * g_ref[gi:gi + 1, :]

        def rope(y):
            return y * rc + pltpu.roll(y, HEAD_PAD - ROPE_HALF, 1) * ra + pltpu.roll(y, ROPE_HALF, 1) * rb

        def put(ref, pi, y, pad_even, pad_odd):
            ref[2 * pi] = jnp.where(lo_half, y, pad_even).astype(BF16)
            ref[2 * pi + 1] = jnp.where(lo_half, pltpu.roll(y, HEAD_DIM, 1), pad_odd).astype(BF16)

        def k_pad(h):
            ch = cs[:, h:h + 1] * LOG2E
            hi = ch.astype(BF16).astype(F32)
            mid = (ch - hi).astype(BF16).astype(F32)
            lo = ch - hi - mid
            ones = jnp.where(lane == HEAD_DIM + 3, 1.0, 0.0)
            return jnp.where(lane == HEAD_DIM, hi, jnp.where(lane == HEAD_DIM + 1, mid,
                                                              jnp.where(lane == HEAD_DIM + 2, lo, ones)))

        for pi in range(N_HEADS // 2):
            col = HEAD_PAD * pi
            put(qa_ref, pi, norm(col, 0) * (SCALE * LOG2E), q_pad, q_pad)
            put(ka_ref, pi, norm(D_BRANCH + col, 1), k_pad(2 * pi), k_pad(2 * pi + 1))
            put(va_ref, pi, z_ref[:, 2 * D_BRANCH + col:2 * D_BRANCH + col + HEAD_PAD].astype(F32), 0.0, 0.0)
            put(qb_ref, pi, rope(norm(4 * D_BRANCH + col, 2)) * (SCALE * LOG2E), 0.0, 0.0)
            put(kb_ref, pi, rope(norm(5 * D_BRANCH + col, 3)), 0.0, 0.0)
            put(vb_ref, pi, z_ref[:, 6 * D_BRANCH + col:6 * D_BRANCH + col + HEAD_PAD].astype(F32), 0.0, 0.0)

    tok = lambda w: pl.BlockSpec((TOK_T, w), lambda i: (i, 0))
    head = pl.BlockSpec((N_HEADS, TOK_T, HEAD_PAD), lambda i: (0, i, 0))
    return pl.pallas_call(
        body, name="prep_fwd", grid=(S // TOK_T,),
        in_specs=[tok(N_MAIN), tok(128), pl.BlockSpec((1, 128), lambda i: (0, 0)),
                  pl.BlockSpec((TOK_T, TOK_T), lambda i: (0, 0)), pl.BlockSpec((4, 128), lambda i: (0, 0)),
                  tok(128), tok(128), tok(128)],
        out_specs=[head] * 6, out_shape=[shp] * 6,
        scratch_shapes=[pltpu.VMEM((1, 128), F32)],
        compiler_params=_cp("arbitrary"),
    )(zm, zf, bf, tril, qkg, rope_c, rope_a, rope_b)


def _pair(ref, pi, lo_half):
    return jnp.where(lo_half, ref[2 * pi].astype(F32), pltpu.roll(ref[2 * pi + 1].astype(F32), HEAD_DIM, 1))


def _mid_fwd(oa, ob, zm, h0, p, w_out, w_pg, w_ple, g2, target=None):
    S = h0.shape[0]
    p, p_lead = p

    def body(oa_ref, ob_ref, ga_ref, gb_ref, h0_ref, p_ref, wo_ref, wg_ref, wp_ref, g2_ref, *rest):
        t_ref, rest = (rest[0], rest[1:]) if target is not None else (None, rest)
        y_ref, h1_ref, h2_ref, u2_ref, e_ref, gate_ref, *loss_ref = rest
        parts = []
        for o_ref, g_ref in ((oa_ref, ga_ref), (ob_ref, gb_ref)):
            for pi in range(N_HEADS // 2):
                g = g_ref[:, HEAD_PAD * pi:HEAD_PAD * (pi + 1)].astype(F32)
                parts.append((o_ref[pi] * (g * _sigmoid(g))).astype(BF16))
        y = jnp.concatenate(parts, axis=1)
        y_ref[...] = y
        h1 = h0_ref[...] + _dot(y, wo_ref[...])
        h1_ref[...] = h1
        r = lax.rsqrt(jnp.mean(h1 * h1, axis=-1, keepdims=True) + EPS)
        u2 = (h1 * r * g2_ref[...]).astype(BF16)
        u2_ref[...] = u2
        gate = _sigmoid(_dot(u2, wg_ref[...]))
        e = _dot(p_ref[...].astype(BF16), wp_ref[...])
        e_ref[...] = e.astype(BF16)
        gate_ref[...] = gate.astype(BF16)
        h2 = h1 + e * gate
        if target is None:
            h2_ref[...] = h2
        else:
            @pl.when(pl.program_id(0) == 0)
            def _():
                loss_ref[0][...] = jnp.zeros_like(loss_ref[0])

            err = h2 - t_ref[...]
            h2_ref[...] = err * (1.0 / D_MODEL)
            part = jnp.sum(jnp.sum(err * err, axis=1, keepdims=True), axis=0, keepdims=True)
            loss_ref[0][...] += part * (0.5 / D_MODEL)

    tok = lambda w: pl.BlockSpec((TOK_T, w), lambda i: (i, 0))
    head = pl.BlockSpec((N_HEADS // 2, TOK_T, HEAD_PAD), lambda i: (0, i, 0))
    full = lambda a, b: pl.BlockSpec((a, b), lambda i: (0, 0))
    act = lambda dt: jax.ShapeDtypeStruct((S, D_MODEL), dt)
    fused = target is not None
    return pl.pallas_call(
        body, name="mid_fwd_loss" if fused else "mid_fwd", grid=(S // TOK_T,),
        in_specs=[head, head,
                  pl.BlockSpec((TOK_T, D_BRANCH), lambda i: (i, 3)), pl.BlockSpec((TOK_T, D_BRANCH), lambda i: (i, 7)),
                  tok(D_MODEL), _slab_spec(p_lead, (TOK_T, PLE_DIM), lambda i: (i, 0)), full(D_MODEL, D_MODEL),
                  full(D_MODEL, D_MODEL), full(PLE_DIM, D_MODEL), full(1, D_MODEL)] + [tok(D_MODEL)] * fused,
        out_specs=[tok(D_MODEL)] * 6 + [full(8, 128)] * fused,
        out_shape=[act(BF16), act(F32), act(F32), act(BF16), act(BF16), act(BF16)]
        + [jax.ShapeDtypeStruct((8, 128), F32)] * fused,
        compiler_params=_cp("arbitrary" if fused else "parallel"),
    )(oa, ob, zm, zm, h0, p, w_out, w_pg, w_ple, g2, *([target] * fused))


def _bias_tables(full_range):
    T = ATT_T
    nb = 1 if full_range else DILATED_PATTERNS[-1][0] // T + 1
    r = lax.broadcasted_iota(jnp.int32, (nb, T, T), 2)
    c = lax.broadcasted_iota(jnp.int32, (nb, T, T), 1)
    b = lax.broadcasted_iota(jnp.int32, (nb, T, T), 0)
    delta = T * b + r - c
    if full_range:
        bias = jnp.where(delta >= 0, 0.0, NEG).astype(F32)
    else:
        mult = jnp.zeros((nb, T, T), F32)
        for window, dil in DILATED_PATTERNS:
            ok = (delta >= 0) & (delta <= window) & (delta % dil == 0)
            mult = mult + ok.astype(F32)
        bias = jnp.where(mult > 0, jnp.log2(jnp.maximum(mult, 1.0)), NEG).astype(F32)
    return bias


def _call_with_rider(body, name, grid, rider, in_specs, out_specs, out_shape, scratch_shapes, operands,
                     semantics=("parallel", "arbitrary")):
    if rider is None:
        return pl.pallas_call(body, name=name, grid=grid, in_specs=in_specs, out_specs=out_specs,
                              out_shape=out_shape, scratch_shapes=scratch_shapes,
                              compiler_params=_cp(*semantics))(*operands)
    n, n_in, n_out = rider.n, len(in_specs), len(out_specs)

    def wrapped(*refs):
        ins, r_ins = refs[:n_in], refs[n_in:n_in + n]
        outs, r_outs = refs[n_in + n:n_in + n + n_out], refs[n_in + n + n_out:n_in + 2 * n + n_out]
        scratch, sems = refs[n_in + 2 * n + n_out:-3], refs[-3:]
        step = [pl.program_id(a) for a in range(len(grid))]

        @pl.when(functools.reduce(jnp.logical_and, [s == 0 for s in step]))
        def _():
            rider.start(r_ins, r_outs, sems)

        body(*ins, *outs, *scratch)

        @pl.when(functools.reduce(jnp.logical_and, [s == g - 1 for s, g in zip(step, grid)]))
        def _():
            rider.wait(r_ins, r_outs, sems)

    return pl.pallas_call(
        wrapped, name=name, grid=grid, in_specs=list(in_specs) + rider.in_specs,
        out_specs=list(out_specs) + rider.out_specs, out_shape=list(out_shape) + rider.out_shapes,
        scratch_shapes=list(scratch_shapes) + rider.scratch_shapes,
        compiler_params=_cp(*["arbitrary"] * len(grid)))(*operands, *rider.arrays)


def _attn_fwd(q, k, v, table_t, full_range, name, rider=None):
    H, S, _ = q.shape
    T = ATT_T
    nb = table_t.shape[0]
    HB = ATT_FWD_HEADS
    KC = ATT_CHUNK_CAUSAL_FWD if full_range else ATT_CHUNK
    chunks = [slice(c, c + KC) for c in range(0, T, KC)]
    fold = lambda x, op: functools.reduce(op, [x[r:r + 8] for r in range(0, KC, 8)])

    def body(q_ref, k_ref, v_ref, tab_ref, o_ref, lse_ref, *scratch):
        st_refs, pt_refs, acc_refs = scratch[:HB], scratch[HB:2 * HB], scratch[2 * HB:]
        i = pl.program_id(1)
        rows = lambda j: pl.ds(pl.multiple_of(j * T, T), T)

        def scores(hh, j):
            st_refs[hh][...] = _dot_nt(k_ref[hh, rows(j), :], q_ref[hh])

        def block(j, b, nxt, stats):
            out = []
            for hh, (m, l) in enumerate(stats):
                st_ref, pt_ref, acc_ref = st_refs[hh], pt_refs[hh], acc_refs[hh]
                mx = None
                for ch in chunks:
                    x = st_ref[ch, :]
                    if b is not None:
                        x = x + tab_ref[b, ch, :]
                        st_ref[ch, :] = x
                    x = fold(x, jnp.maximum)
                    mx = x if mx is None else jnp.maximum(mx, x)
                m_new = jnp.maximum(m, jnp.max(mx, axis=0, keepdims=True))
                alpha = jnp.exp2(m - m_new)
                ls = None
                for ch in chunks:
                    pc = jnp.exp2(st_ref[ch, :] - m_new)
                    pt_ref[ch, :] = pc.astype(BF16)
                    pc = fold(pc, jnp.add)
                    ls = pc if ls is None else ls + pc
                if nxt is not None:
                    scores(hh, nxt)
                acc_ref[...] = alpha * acc_ref[...] + _dot_tn(v_ref[hh, rows(j), :], pt_ref[...])
                out.append((m_new, alpha * l + jnp.sum(ls, axis=0, keepdims=True)))
            return tuple(out)

        lo = 0 if full_range else jnp.maximum(i - (nb - 1), 0)
        for hh in range(HB):
            acc_refs[hh][...] = jnp.zeros_like(acc_refs[hh])
            scores(hh, lo)
        stats = lax.fori_loop(lo, i, lambda j, st: block(j, None if full_range else i - j, j + 1, st),
                              ((jnp.full((1, T), NEG, F32), jnp.zeros((1, T), F32)),) * HB)
        stats = block(i, 0, None, stats)
        o_t = [acc_refs[hh][...] * (1.0 / l) for hh, (m, l) in enumerate(stats)]
        for hh, (m, l) in enumerate(stats):
            lse_ref[hh, 0] = m + jnp.log2(l)
        for hp in range(HB // 2):
            o_ref[hp] = jnp.concatenate([o_t[2 * hp][:HEAD_DIM], o_t[2 * hp + 1][:HEAD_DIM]], axis=0).T

    return _call_with_rider(
        body, name, (H // HB, S // T), rider,
        in_specs=[pl.BlockSpec((HB, T, HEAD_PAD), lambda h, i: (h, i, 0)),
                  pl.BlockSpec((HB, S, HEAD_PAD), lambda h, i: (h, 0, 0), pipeline_mode=pl.Buffered(1)),
                  pl.BlockSpec((HB, S, HEAD_PAD), lambda h, i: (h, 0, 0), pipeline_mode=pl.Buffered(1)),
                  pl.BlockSpec((nb, T, T), lambda h, i: (0, 0, 0), pipeline_mode=pl.Buffered(1))],
        out_specs=[pl.BlockSpec((HB // 2, T, HEAD_PAD), lambda h, i: (h, i, 0)),
                   pl.BlockSpec((HB, 1, 1, T), lambda h, i: (h, i, 0, 0))],
        out_shape=[jax.ShapeDtypeStruct((H // 2, S, HEAD_PAD), F32), jax.ShapeDtypeStruct((H, S // T, 1, T), F32)],
        scratch_shapes=([pltpu.VMEM((T, T), F32)] * HB + [pltpu.VMEM((T, T), BF16)] * HB
                        + [pltpu.VMEM((HEAD_PAD, T), F32)] * HB),
        operands=(q, k, v, table_t))


def _attn_bwd(q, k, v, do, lse, dd, table_t, full_range, name, rider=None, narrow=(False, False, False)):
    H, S, _ = q.shape
    T = ATT_T
    nq = S // T
    nb = table_t.shape[0]
    HB = ATT_BWD_HEADS
    KC = ATT_CHUNK
    chunks = [slice(c, c + KC) for c in range(0, T, KC)]

    def body(q_ref, do_ref, lse_ref, dd_ref, k_ref, v_ref, tab_ref, dq_hbm, dk_ref, dv_ref, *scratch):
        st_refs, dpt_refs, pt_refs, dst_refs = (scratch[n * HB:(n + 1) * HB] for n in range(4))
        dq_ref, dk_acc, dv_acc, *dq_cast, dq_sem = scratch[4 * HB:]
        h = pl.program_id(0)
        j = pl.program_id(1)

        @pl.when(j == 0)
        def _():
            dq_ref[...] = jnp.zeros_like(dq_ref)

        dk_acc[...] = jnp.zeros_like(dk_acc)
        dv_acc[...] = jnp.zeros_like(dv_acc)

        def step(i, b):
            rows = pl.ds(pl.multiple_of(i * T, T), T)
            for hh in range(HB):
                st_refs[hh][...] = _dot_nt(k_ref[hh], q_ref[hh, rows, :])
                dpt_refs[hh][...] = _dot_nt(v_ref[hh], do_ref[hh, rows, :])
            for hh in range(HB):
                lse_i = lse_ref[hh, i]
                dd_i = dd_ref[hh, i]
                for ch in chunks:
                    x = st_refs[hh][ch, :]
                    if b is not None:
                        x = x + tab_ref[b, ch, :]
                    pc = jnp.exp2(x - lse_i)
                    pt_refs[hh][ch, :] = pc.astype(BF16)
                    dst_refs[hh][ch, :] = (pc * (dpt_refs[hh][ch, :] - dd_i)).astype(BF16)
                dv_acc[hh] += _dot(pt_refs[hh][...], do_ref[hh, rows, :])
                dk_acc[hh] += _dot(dst_refs[hh][...], q_ref[hh, rows, :])
                dq_ref[hh, rows, :] += _dot_tn(dst_refs[hh][...], k_ref[hh])

        step(j, 0)
        if full_range:
            pl.loop(j + 1, nq)(lambda i: step(i, None))
        else:
            pl.loop(j + 1, jnp.minimum(j + nb, nq))(lambda i: step(i, i - j))
        dk_ref[...] = dk_acc[...].astype(dk_ref.dtype)
        dv_ref[...] = dv_acc[...].astype(dv_ref.dtype)

        @pl.when(j == nq - 1)
        def _():
            src = dq_ref
            if narrow[0]:
                src, = dq_cast
                src[...] = dq_ref[...].astype(BF16)
            out = pltpu.make_async_copy(src, dq_hbm.at[pl.ds(h * HB, HB)], dq_sem)
            out.start()
            out.wait()

    once = dict(pipeline_mode=pl.Buffered(1))
    per_head = pl.BlockSpec((HB, S, HEAD_PAD), lambda h, j: (h, 0, 0), **once)
    rows = pl.BlockSpec((HB, nq, 1, T), lambda h, j: (h, 0, 0, 0))
    blk = pl.BlockSpec((HB, T, HEAD_PAD), lambda h, j: (h, j, 0))
    shp = [jax.ShapeDtypeStruct((H, S, HEAD_PAD), BF16 if nar else F32) for nar in narrow]
    acc = pltpu.VMEM((HB, T, HEAD_PAD), F32)
    return _call_with_rider(
        body, name, (H // HB, nq), rider,
        in_specs=[per_head, per_head, rows, rows, blk, blk,
                  pl.BlockSpec((nb, T, T), lambda h, j: (0, 0, 0), **once)],
        out_specs=[pl.BlockSpec(memory_space=pltpu.HBM), blk, blk], out_shape=shp,
        scratch_shapes=([pltpu.VMEM((T, T), F32)] * (2 * HB) + [pltpu.VMEM((T, T), BF16)] * (2 * HB)
                        + [pltpu.VMEM((HB, S, HEAD_PAD), F32), acc, acc]
                        + [pltpu.VMEM((HB, S, HEAD_PAD), BF16)] * narrow[0] + [pltpu.SemaphoreType.DMA]),
        operands=(q, do, lse, dd, k, v, table_t))


def _mid_bwd(dh2, h1, e, gate, g2, w_pg, w_out, oa, ob, zm):
    S = dh2.shape[0]

    def body(dh2_ref, h1_ref, e_ref, gate_ref, g2_ref, wg_ref, wo_ref, oa_ref, ob_ref, ga_ref, gb_ref,
             dh1_ref, dh1b_ref, de_ref, dpre_ref, doa_ref, dob_ref, dga_ref, dgb_ref, dd_ref, dg2_ref):
        @pl.when(pl.program_id(0) == 0)
        def _():
            dg2_ref[...] = jnp.zeros_like(dg2_ref)

        lane = lax.broadcasted_iota(jnp.int32, (TOK_T, HEAD_PAD), 1)
        lo_half = lane < HEAD_DIM
        dh2 = dh2_ref[...]
        gate = gate_ref[...]
        de_ref[...] = (dh2 * gate).astype(BF16)
        dpre = (dh2 * e_ref[...] * gate * (1.0 - gate)).astype(BF16)
        dpre_ref[...] = dpre
        du2 = _dot_nt(dpre, wg_ref[...])
        h1 = h1_ref[...]
        r = lax.rsqrt(jnp.mean(h1 * h1, axis=-1, keepdims=True) + EPS)
        xh = h1 * r
        a = du2 * g2_ref[...]
        dh1 = dh2 + r * (a - xh * jnp.mean(a * xh, axis=-1, keepdims=True))
        dg2_ref[...] += jnp.sum(du2 * xh, axis=0, keepdims=True)
        dh1_ref[...] = dh1
        dh1b = dh1.astype(BF16)
        dh1b_ref[...] = dh1b
        dy = _dot_nt(dh1b, wo_ref[...])
        dd = jnp.zeros((TOK_T, HEAD_PAD), F32)
        for bi, (o_ref, g_ref, do_ref, dg_ref) in enumerate(
                ((oa_ref, ga_ref, doa_ref, dga_ref), (ob_ref, gb_ref, dob_ref, dgb_ref))):
            for pi in range(N_HEADS // 2):
                col = bi * D_BRANCH + HEAD_PAD * pi
                dyp = dy[:, col:col + HEAD_PAD]
                g = g_ref[:, HEAD_PAD * pi:HEAD_PAD * (pi + 1)].astype(F32)
                sg = _sigmoid(g)
                o_pair = o_ref[pi]
                dg_ref[:, HEAD_PAD * pi:HEAD_PAD * (pi + 1)] = (
                    dyp * o_pair * (sg * (1.0 + g * (1.0 - sg)))).astype(BF16)
                dop = dyp * (g * sg)
                prod = dop * o_pair
                for hh, d_head, mine in ((2 * pi, dop, lo_half),
                                         (2 * pi + 1, pltpu.roll(dop, HEAD_DIM, 1), ~lo_half)):
                    do_ref[hh] = jnp.where(lo_half, d_head, 0.0).astype(BF16)
                    dsum = jnp.sum(jnp.where(mine, prod, 0.0), axis=1, keepdims=True)
                    dd = dd + jnp.where(lane == bi * N_HEADS + hh, dsum, 0.0)
        dd_ref[...] = dd.T[:2 * N_HEADS, :]

    tok = lambda w: pl.BlockSpec((TOK_T, w), lambda i: (i, 0))
    head = pl.BlockSpec((N_HEADS, TOK_T, HEAD_PAD), lambda i: (0, i, 0))
    pairs = pl.BlockSpec((N_HEADS // 2, TOK_T, HEAD_PAD), lambda i: (0, i, 0))
    full = lambda a, b: pl.BlockSpec((a, b), lambda i: (0, 0))
    act = lambda w, dt: jax.ShapeDtypeStruct((S, w), dt)
    hshape = lambda w, dt: jax.ShapeDtypeStruct((N_HEADS, S, w), dt)
    return pl.pallas_call(
        body, name="mid_bwd", grid=(S // TOK_T,),
        in_specs=[tok(D_MODEL)] * 4 + [full(1, D_MODEL), full(D_MODEL, D_MODEL), full(D_MODEL, D_MODEL), pairs, pairs,
                                      pl.BlockSpec((TOK_T, D_BRANCH), lambda i: (i, 3)),
                                      pl.BlockSpec((TOK_T, D_BRANCH), lambda i: (i, 7))],
        out_specs=[tok(D_MODEL)] * 4 + [head, head, tok(D_BRANCH), tok(D_BRANCH),
                                       pl.BlockSpec((2 * N_HEADS, TOK_T), lambda i: (0, i)), full(1, D_MODEL)],
        out_shape=[act(D_MODEL, F32), act(D_MODEL, BF16), act(D_MODEL, BF16), act(D_MODEL, BF16),
                   hshape(HEAD_PAD, BF16), hshape(HEAD_PAD, BF16), act(D_BRANCH, BF16), act(D_BRANCH, BF16),
                   jax.ShapeDtypeStruct((2 * N_HEADS, S), F32), jax.ShapeDtypeStruct((1, D_MODEL), F32)],
        compiler_params=_cp("arbitrary"),
    )(dh2, h1, e, gate, g2, w_pg, w_out, oa, ob, zm, zm)


def _prep_bwd(dqa, dka, dva, dqb, dkb, dvb, zm, qkg, rope_c, rope_a, rope_b, dga, dgb, zf, bf, triu):
    S = zm.shape[0]
    n = S // TOK_T

    def body(dqa_ref, dka_ref, dva_ref, dqb_ref, dkb_ref, dvb_ref, z_ref, g_ref, rc_ref, ra_ref, rb_ref,
             dga_ref, dgb_ref, zf_ref, b_ref, tri_ref, dz_ref, dzf_ref, dqkg_ref, db_ref, carry):
        @pl.when(pl.program_id(0) == 0)
        def _():
            dqkg_ref[...] = jnp.zeros_like(dqkg_ref)
            db_ref[...] = jnp.zeros_like(db_ref)
            carry[...] = jnp.zeros_like(carry)

        lane = lax.broadcasted_iota(jnp.int32, (TOK_T, HEAD_PAD), 1)
        lo_half = lane < HEAD_DIM
        rc, ra, rb = rc_ref[...], ra_ref[...], rb_ref[...]
        same_head = _same_head()

        def unrope(dy):
            return dy * rc + pltpu.roll(dy * ra, ROPE_HALF, 1) + pltpu.roll(dy * rb, HEAD_PAD - ROPE_HALF, 1)

        def norm_bwd(col, gi, dy):
            x = z_ref[:, col:col + HEAD_PAD].astype(F32)
            r = _pair_rsqrt(x, same_head)
            xh = x * r
            dqkg_ref[gi:gi + 1, :] += jnp.sum(dy * xh, axis=0, keepdims=True)
            a = dy * g_ref[gi:gi + 1, :]
            dz_ref[:, col:col + HEAD_PAD] = (r * (a - xh * _pair_mean(a * xh, same_head))).astype(BF16)

        dc = jnp.zeros((TOK_T, HEAD_PAD), F32)
        for pi in range(N_HEADS // 2):
            col = HEAD_PAD * pi
            norm_bwd(col, 0, _pair(dqa_ref, pi, lo_half) * SCALE)
            norm_bwd(D_BRANCH + col, 1, _pair(dka_ref, pi, lo_half) * LN2)
            dz_ref[:, 2 * D_BRANCH + col:2 * D_BRANCH + col + HEAD_PAD] = _pair(dva_ref, pi, lo_half).astype(BF16)
            norm_bwd(4 * D_BRANCH + col, 2, unrope(_pair(dqb_ref, pi, lo_half) * SCALE))
            norm_bwd(5 * D_BRANCH + col, 3, unrope(_pair(dkb_ref, pi, lo_half) * LN2))
            dz_ref[:, 6 * D_BRANCH + col:6 * D_BRANCH + col + HEAD_PAD] = _pair(dvb_ref, pi, lo_half).astype(BF16)
            for hh in (2 * pi, 2 * pi + 1):
                dch = dka_ref[hh][:, HEAD_DIM:HEAD_DIM + 1] + dqa_ref[hh][:, HEAD_DIM + 3:HEAD_DIM + 4]
                dc = dc + jnp.where(lane == hh, dch, 0.0)
        dz_ref[:, 3 * D_BRANCH:4 * D_BRANCH] = dga_ref[...]
        dz_ref[:, 7 * D_BRANCH:8 * D_BRANCH] = dgb_ref[...]
        tri = tri_ref[...]
        dlf = sum(_dot(tri, part) for part in _split3(dc)) + carry[...]
        carry[...] = dlf[0:1, :]
        dfa = dlf * (1.0 - _sigmoid(zf_ref[...] + b_ref[...]))
        dzf_ref[...] = dfa.astype(BF16)
        db_ref[...] += jnp.sum(dfa, axis=0, keepdims=True)

    tok = lambda w: pl.BlockSpec((TOK_T, w), lambda i: (n - 1 - i, 0))
    head = pl.BlockSpec((N_HEADS, TOK_T, HEAD_PAD), lambda i: (0, n - 1 - i, 0))
    fixed = lambda a, b: pl.BlockSpec((a, b), lambda i: (0, 0))
    return pl.pallas_call(
        body, name="prep_bwd", grid=(n,),
        in_specs=[head] * 6 + [tok(N_MAIN), fixed(4, 128), tok(128), tok(128), tok(128), tok(D_BRANCH), tok(D_BRANCH),
                               tok(128), fixed(1, 128), fixed(TOK_T, TOK_T)],
        out_specs=[tok(N_MAIN), tok(128), fixed(4, 128), fixed(1, 128)],
        out_shape=[jax.ShapeDtypeStruct((S, N_MAIN), BF16), jax.ShapeDtypeStruct((S, 128), BF16),
                   jax.ShapeDtypeStruct((4, 128), F32), jax.ShapeDtypeStruct((1, 128), F32)],
        scratch_shapes=[pltpu.VMEM((1, 128), F32)],
        compiler_params=_cp("arbitrary"),
    )(dqa, dka, dva, dqb, dkb, dvb, zm, qkg, rope_c, rope_a, rope_b, dga, dgb, zf, bf, triu)


def _inproj_bwd(dzm, dzf, wm, wf, h0, dh1, g, rider=None):
    S = h0.shape[0]

    def body(dzm_ref, dzf_ref, wm_ref, wf_ref, h_ref, dh1_ref, g_ref, dh0_ref, dg_ref):
        @pl.when(pl.program_id(0) == 0)
        def _():
            dg_ref[...] = jnp.zeros_like(dg_ref)

        du = _dot_nt(dzm_ref[...], wm_ref[...]) + _dot_nt(dzf_ref[...], wf_ref[...])
        x = h_ref[...]
        r = lax.rsqrt(jnp.mean(x * x, axis=-1, keepdims=True) + EPS)
        xh = x * r
        a = du * g_ref[...]
        dh0_ref[...] = dh1_ref[...] + r * (a - xh * jnp.mean(a * xh, axis=-1, keepdims=True))
        dg_ref[...] += jnp.sum(du * xh, axis=0, keepdims=True)

    tok = lambda w: pl.BlockSpec((TOK_T, w), lambda i: (i, 0))
    full = lambda a, b: pl.BlockSpec((a, b), lambda i: (0, 0))
    return _call_with_rider(
        body, "inproj_bwd", (S // TOK_T,), rider,
        in_specs=[tok(N_MAIN), tok(128), full(D_MODEL, N_MAIN), full(D_MODEL, 128), tok(D_MODEL), tok(D_MODEL),
                  full(1, D_MODEL)],
        out_specs=[tok(D_MODEL), full(1, D_MODEL)],
        out_shape=[jax.ShapeDtypeStruct((S, D_MODEL), F32), jax.ShapeDtypeStruct((1, D_MODEL), F32)],
        scratch_shapes=[], operands=(dzm, dzf, wm, wf, h0, dh1, g), semantics=("arbitrary",))


def _wgrad(a, b, name, a_lead=()):
    S, M = a.shape[len(a_lead):]
    N = b.shape[1]
    tn = min(N, 2048)
    ts = 512
    last = S // ts - 1

    def body(a_ref, b_ref, o_ref, acc_ref):
        @pl.when(pl.program_id(1) == 0)
        def _():
            acc_ref[...] = jnp.zeros_like(acc_ref)

        acc_ref[...] += _dot_tn(a_ref[...].astype(BF16), b_ref[...])

        @pl.when(pl.program_id(1) == last)
        def _():
            o_ref[...] = acc_ref[...].astype(BF16)

    return pl.pallas_call(
        body, name=name, grid=(N // tn, S // ts),
        in_specs=[_slab_spec(a_lead, (ts, M), lambda n, s: (s, 0)), pl.BlockSpec((ts, tn), lambda n, s: (s, n))],
        out_specs=pl.BlockSpec((M, tn), lambda n, s: (0, n)),
        out_shape=jax.ShapeDtypeStruct((M, N), BF16),
        scratch_shapes=[pltpu.VMEM((M, tn), F32)],
        compiler_params=_cp("parallel", "arbitrary"),
    )(a, b)


def _rope_tables(positions):
    inv_freq = ROPE_THETA ** (-jnp.arange(ROPE_HALF, dtype=F32) / ROPE_HALF)
    ang = positions.astype(F32)[:, None] * inv_freq
    cos, sin = jnp.cos(ang), jnp.sin(ang)
    S = positions.shape[0]
    one, zero = jnp.ones((S, HEAD_DIM - 2 * ROPE_HALF), F32), jnp.zeros((S, HEAD_DIM - 2 * ROPE_HALF), F32)
    z8 = jnp.zeros((S, ROPE_HALF), F32)
    rc = jnp.concatenate([cos, cos, one], axis=1)
    ra = jnp.concatenate([-sin, z8, zero], axis=1)
    rb = jnp.concatenate([z8, sin, zero], axis=1)
    return tuple(jnp.tile(t, (1, 2)) for t in (rc, ra, rb))


def _in_operands(w_in):
    w_in = w_in.astype(BF16)
    wm = jnp.concatenate([w_in[:, :4 * D_BRANCH], w_in[:, 4 * D_BRANCH + N_HEADS:]], axis=1)
    wf = jnp.pad(w_in[:, 4 * D_BRANCH:4 * D_BRANCH + N_HEADS], ((0, 0), (0, 128 - N_HEADS)))
    return dict(wm=wm, wf=wf)


def _layer_weights(w_in, w_out, w_ple, w_pg):
    return dict(_in_operands(w_in), w_out=w_out.astype(BF16), w_ple=w_ple.astype(BF16), w_pg=w_pg.astype(BF16))


def _row(v, width=128):
    v = v.reshape(1, -1).astype(F32)
    return jnp.pad(v, ((0, 0), (0, width - v.shape[1])))


def _layer_fwd(h0, p, rope, tabs, w, norm_g, b_f, qk_g, ple_g, rider=None, late=None, target=None):
    g1 = norm_g.reshape(1, D_MODEL)
    g2 = ple_g.reshape(1, D_MODEL)
    qkg = jnp.tile(qk_g, (1, 2))
    bf = _row(b_f)
    zm, zf, u = _inproj_fwd(h0, g1, w["wm"], w["wf"])
    qa, ka, va, qb, kb, vb = _prep_fwd(zm, zf, bf, tabs["tril"], qkg, *rope)
    oa, lse_a, *arrivals = _attn_fwd(qa, ka, va, tabs["fox"], True, "fox_fwd", rider)
    if late is not None:
        w = {**w, **late(arrivals)}
    ob, lse_b = _attn_fwd(qb, kb, vb, tabs["dil"], False, "dil_fwd")
    y, h1, h2, u2, e, gate, *loss = _mid_fwd(oa, ob, zm, h0, p, w["w_out"], w["w_pg"], w["w_ple"], g2, target)
    saved = dict(h0=h0, p=p, zm=zm, zf=zf, u=u, qa=qa, ka=ka, va=va, qb=qb, kb=kb, vb=vb, oa=oa, ob=ob,
                 lse_a=lse_a, lse_b=lse_b, y=y, h1=h1, u2=u2, e=e, gate=gate, g1=g1, g2=g2, qkg=qkg, bf=bf, w=w)
    return (h2, saved, arrivals, *loss)


def _layer_bwd(dh2, sv, rope, tabs, make_rider=None, make_last_rider=None):
    S = dh2.shape[0]
    nq = S // ATT_T
    w = sv["w"]
    rows = lambda a: a.reshape(N_HEADS, nq, 1, ATT_T)
    (dh1, dh1b, de, dpre, doa, dob, dga, dgb, dd, dg2) = _mid_bwd(
        dh2, sv["h1"], sv["e"], sv["gate"], sv["g2"], w["w_pg"], w["w_out"], sv["oa"], sv["ob"], sv["zm"])
    dda, ddb = dd[:N_HEADS], dd[N_HEADS:]
    early = dict(w_out=_wgrad(sv["y"], dh1b, "wgrad_out"), w_ple=_wgrad(sv["p"][0], de, "wgrad_ple", sv["p"][1]),
                 w_ple_gate=_wgrad(sv["u2"], dpre, "wgrad_gate"))
    rider = None if make_rider is None else make_rider(early)
    dqa, dka, dva, *arrivals = _attn_bwd(sv["qa"], sv["ka"], sv["va"], doa, sv["lse_a"], rows(dda), tabs["fox"],
                                         True, "fox_bwd", rider, narrow=(False, False, True))
    dqb, dkb, dvb = _attn_bwd(sv["qb"], sv["kb"], sv["vb"], dob, sv["lse_b"], rows(ddb), tabs["dil"], False,
                              "dil_bwd", narrow=(False, True, True))
    dzm, dzf, dqkg, dbf = _prep_bwd(dqa, dka, dva, dqb, dkb, dvb, sv["zm"], sv["qkg"], *rope, dga, dgb,
                                    sv["zf"], sv["bf"], tabs["triu"])
    dwm = _wgrad(sv["u"], dzm, "wgrad_in")
    dwf = _wgrad(sv["u"], dzf, "wgrad_f")
    dw_in = jnp.concatenate([dwm[:, :4 * D_BRANCH], dwf[:, :N_HEADS], dwm[:, 4 * D_BRANCH:]], axis=1)
    last_rider = None if make_last_rider is None else make_last_rider(dw_in)
    dh0, dg1, *last_arrivals = _inproj_bwd(dzm, dzf, w["wm"], w["wf"], sv["h0"], dh1, sv["g1"], last_rider)
    grads = dict(norm_g=dg1.reshape(D_MODEL), w_in=dw_in, b_f=dbf[0, :N_HEADS],
                 qk_norm_g=dqkg[:, :HEAD_DIM] + dqkg[:, HEAD_DIM:], ple_norm_g=dg2.reshape(D_MODEL), **early)
    return dh0, grads, arrivals + last_arrivals


def _tables():
    T = TOK_T
    r = lax.broadcasted_iota(jnp.int32, (T, T), 0)
    c = lax.broadcasted_iota(jnp.int32, (T, T), 1)
    return dict(fox=_bias_tables(True), dil=_bias_tables(False),
                tril=(c <= r).astype(BF16), triu=(c >= r).astype(BF16))


def _local_step(x, p, positions, target, layers, small):
    rope = _rope_tables(positions)
    tabs = _tables()
    ws = [_layer_weights(*lw) for lw in layers]
    h = x
    saved = []
    for li, (w, lp, sm) in enumerate(zip(ws, p, small)):
        h, sv, _, *loss = _layer_fwd(h, (lp, ()), rope, tabs, w, *sm, target=target if li == len(ws) - 1 else None)
        saved.append(sv)
    dh, (loss,) = h, loss
    grads = [None] * len(ws)
    for li in reversed(range(len(ws))):
        dh, grads[li], _ = _layer_bwd(dh, saved[li], rope, tabs)
    return loss[0, 0], dh, grads


def _peers():
    x, y, c = lax.axis_index("x"), lax.axis_index("y"), lax.axis_index("c")
    me = 4 * x + 2 * y + c
    flip = lambda v, bit: 1 - v if bit else v
    return me, [(flip(x, k & 4), flip(y, k & 2), flip(c, k & 1)) for k in range(1, N_DEV)]


def _sel(ref, kind, d):
    if kind == "whole":
        return ref
    if kind == "slot":
        return ref.at[d]
    block = pl.ds(pl.multiple_of(d * 128, 128), 128)
    return ref.at[block, :] if kind == "rows" else ref.at[:, block]


class _Pushes:
    def __init__(self, arrays, src_kinds, dst_kinds, out_shapes):
        self.arrays, self.n = list(arrays), len(arrays)
        self.src_kinds, self.dst_kinds = src_kinds, dst_kinds
        self.out_shapes = [jax.ShapeDtypeStruct(s, a.dtype) for s, a in zip(out_shapes, arrays)]
        hbm = pl.BlockSpec(memory_space=pltpu.HBM)
        self.in_specs, self.out_specs = [hbm] * self.n, [hbm] * self.n
        self.scratch_shapes = [pltpu.SemaphoreType.DMA((N_DEV - 1, self.n)),
                               pltpu.SemaphoreType.DMA((N_DEV - 1, self.n)), pltpu.SemaphoreType.DMA((self.n,))]

    def _copies(self, ins, outs, sems):
        send_sems, recv_sems, local_sems = sems
        me, peers = _peers()
        src = lambda a, d: _sel(ins[a], self.src_kinds[a], d)
        dst = lambda a: _sel(outs[a], self.dst_kinds[a], me)
        local = [pltpu.make_async_copy(src(a, me), dst(a), local_sems.at[a]) for a in range(self.n)]
        remote = [pltpu.make_async_remote_copy(
            src_ref=src(a, 4 * px + 2 * py + pc), dst_ref=dst(a), send_sem=send_sems.at[k, a],
            recv_sem=recv_sems.at[k, a], device_id=(px, py, pc), device_id_type=pl.DeviceIdType.MESH)
            for k, (px, py, pc) in enumerate(peers) for a in range(self.n)]
        return local + remote

    def start(self, ins, outs, sems):
        for cp in self._copies(ins, outs, sems):
            cp.start()

    def wait(self, ins, outs, sems):
        for cp in self._copies(ins, outs, sems):
            cp.wait()


def _exchange(name, pushes):
    n = pushes.n

    def body(*refs):
        pushes.start(refs[:n], refs[n:2 * n], refs[2 * n:])
        pushes.wait(refs[:n], refs[n:2 * n], refs[2 * n:])

    return pl.pallas_call(body, name=name, in_specs=pushes.in_specs, out_specs=pushes.out_specs,
                          out_shape=pushes.out_shapes, scratch_shapes=pushes.scratch_shapes)(*pushes.arrays)


def _gather_two_level(shard, name):
    def body(x_ref, out_ref, send_sems, recv_sems, local_sem):
        x, y, c = lax.axis_index("x"), lax.axis_index("y"), lax.axis_index("c")
        me, sibling = (x, y, c), (x, y, 1 - c)
        chips = [(1 - x, y), (x, 1 - y), (1 - x, 1 - y)]
        slot = lambda px, py, pc: out_ref.at[4 * px + 2 * py + pc]

        def copy(k, block, to, src=None):
            return pltpu.make_async_remote_copy(
                src_ref=slot(*block) if src is None else src, dst_ref=slot(*block), send_sem=send_sems.at[k],
                recv_sem=recv_sems.at[k], device_id=to, device_id_type=pl.DeviceIdType.MESH)

        mine = pltpu.make_async_copy(x_ref, slot(*me), local_sem)
        mine.start()
        first = [copy(0, me, sibling, src=x_ref)] + [copy(1 + j, me, (*chip, c), src=x_ref)
                                                     for j, chip in enumerate(chips)]
        for cp in first:
            cp.start()
        passed = [copy(4 + j, (*chip, c), sibling) for j, chip in enumerate(chips)]
        for j, chip in enumerate(chips):
            copy(1 + j, (*chip, c), me).wait_recv()
            passed[j].start()
        copy(0, sibling, me).wait_recv()
        for j, chip in enumerate(chips):
            copy(4 + j, (*chip, 1 - c), me).wait_recv()
        for cp in first + passed:
            cp.wait_send()
        mine.wait()

    hbm = pl.BlockSpec(memory_space=pltpu.HBM)
    return pl.pallas_call(
        body, name=name, in_specs=[hbm], out_specs=hbm,
        out_shape=jax.ShapeDtypeStruct((N_DEV,) + shard.shape, shard.dtype),
        scratch_shapes=[pltpu.SemaphoreType.DMA((N_DEV - 1,)), pltpu.SemaphoreType.DMA((N_DEV - 1,)),
                        pltpu.SemaphoreType.DMA],
    )(shard)


def _gather_pushes(shards, kinds):
    full = {"slot": lambda s: (N_DEV,) + s, "rows": lambda s: (N_DEV * s[0], s[1]),
            "cols": lambda s: (s[0], N_DEV * s[1])}
    return _Pushes(shards, ["whole"] * len(shards), kinds, [full[k](a.shape) for a, k in zip(shards, kinds)])


def _scatter_pushes(partials, kinds):
    part = {"slot": lambda s: s[1:], "rows": lambda s: (128, s[1]), "cols": lambda s: (s[0], 128),
            "whole": lambda s: s}
    return _Pushes(partials, kinds, ["slot"] * len(partials),
                   [(N_DEV,) + part[k](a.shape) for a, k in zip(partials, kinds)])


def _adamw(name, parts, w, m, v, rows):
    L, R, C = w.shape

    def body(p_ref, w_ref, m_ref, v_ref, g_ref, d_ref, nm_ref, nv_ref):
        g = p_ref[0, 0].astype(F32)
        for s in range(1, N_DEV):
            g = g + p_ref[s, 0].astype(F32)
        g_ref[0] = g
        nm = ADAM_B1 * m_ref[0] + (1.0 - ADAM_B1) * g
        nv = ADAM_B2 * v_ref[0] + (1.0 - ADAM_B2) * (g * g)
        nm_ref[0] = nm
        nv_ref[0] = nv
        m_hat = nm / (1.0 - ADAM_B1 ** ADAM_STEP)
        v_hat = nv / (1.0 - ADAM_B2 ** ADAM_STEP)
        d_ref[0] = -ADAM_LR * (m_hat / (jnp.sqrt(v_hat) + ADAM_EPS) + ADAM_WD * w_ref[0])

    blk = pl.BlockSpec((1, rows, C), lambda l, i: (l, i, 0))
    shp = jax.ShapeDtypeStruct((L, R, C), F32)
    return pl.pallas_call(
        body, name=name, grid=(L, R // rows),
        in_specs=[pl.BlockSpec((N_DEV, 1, rows, C), lambda l, i: (0, l, i, 0)), blk, blk, blk],
        out_specs=[blk] * 4, out_shape=[shp] * 4,
        compiler_params=_cp("parallel", "parallel"),
    )(parts, w, m, v)


SMALL_ROWS = 40
LOSS_ROW = 37


def _pack_small(norm_g, ple_g, qk_g, b_f, last_row):
    rows = lambda a: a.astype(F32).reshape(-1, 128)
    flat = jnp.concatenate([rows(norm_g), rows(ple_g), rows(qk_g), _row(b_f.reshape(-1)), last_row], axis=0)
    return jnp.pad(flat, ((0, SMALL_ROWS - flat.shape[0]), (0, 0)))


def _unpack_small(flat):
    return (flat[0:16].reshape(2, D_MODEL), flat[16:32].reshape(2, D_MODEL), flat[32:36].reshape(2, 4, HEAD_DIM),
            flat[36, :2 * N_HEADS].reshape(2, N_HEADS))


def kernel(x, p, positions, norm_g, w_in, b_f, qk_norm_g, w_out, w_ple, ple_norm_g, w_ple_gate, loss_target, m_norm_g, m_w_in, m_b_f, m_qk_norm_g, m_w_out, m_w_ple, m_ple_norm_g, m_w_ple_gate, v_norm_g, v_w_in, v_b_f, v_qk_norm_g, v_w_out, v_w_ple, v_ple_norm_g, v_w_ple_gate):
    bf16 = lambda a: a.astype(BF16)
    rows_in = W_IN_ROWS // 2
    flat_in = lambda a: bf16(a).reshape(rows_in, 128)
    full_in = lambda g: g.reshape(N_DEV, D_MODEL, W_IN_SHARD).transpose(1, 0, 2).reshape(D_MODEL, N_IN)
    small = [(norm_g[l], b_f[l], qk_norm_g[l], ple_norm_g[l]) for l in range(2)]
    rope = _rope_tables(positions[0])
    tabs = _tables()

    g_in0 = _gather_two_level(flat_in(w_in[0]), "gather_first")
    rest = _gather_pushes([flat_in(w_in[1])] + [bf16(a[l]) for l in range(2) for a in (w_out, w_ple, w_ple_gate)],
                          ["slot"] + ["rows", "cols", "rows"] * 2)
    late = lambda got: dict(w_out=got[1], w_ple=got[2], w_pg=got[3])
    h, sv0, got = _layer_fwd(x[0], (p, (0, 0)), rope, tabs, _in_operands(full_in(g_in0)), *small[0], rest, late)
    w1 = dict(_in_operands(full_in(got[0])), w_out=got[4], w_ple=got[5], w_pg=got[6])
    dh, sv1, _, loss = _layer_fwd(h, (p, (1, 0)), rope, tabs, w1, *small[1], target=loss_target[0])
    dh, gr1, _ = _layer_bwd(dh, sv1, rope, tabs)

    by_dest = lambda d: d.reshape(D_MODEL, N_DEV, W_IN_SHARD).transpose(1, 0, 2).reshape(N_DEV, rows_in, 128)
    big = ("w_out", "w_ple", "w_ple_gate")
    riding = lambda early: _scatter_pushes([by_dest(gr1["w_in"])] + [gr1[n] for n in big] + [early[n] for n in big],
                                           ["slot"] + ["rows", "cols", "rows"] * 2)
    riding_last = lambda dw_in: _scatter_pushes([by_dest(dw_in)], ["slot"])
    dx, gr0, (r_in1, *r_big, r_in0) = _layer_bwd(dh, sv0, rope, tabs, riding, riding_last)
    grads = (gr0, gr1)
    stack = lambda name: jnp.stack([gl[name] for gl in grads], axis=0)
    small_part = _pack_small(stack("norm_g"), stack("ple_norm_g"), stack("qk_norm_g"), stack("b_f"),
                             _row(loss[0, 0].reshape(1)))
    (r_small,) = _exchange("exchange_small", _scatter_pushes([small_part], ["whole"]))
    r_in = jnp.concatenate([r_in0, r_in1], axis=1)
    r_out, r_ple, r_pg = (jnp.stack([r_big[3 + k], r_big[k]], axis=1) for k in range(3))

    zero_row = jnp.zeros((1, 128), F32)
    small_of = lambda ng, pg, qk, bf: _pack_small(ng, pg, qk, bf, zero_row)[None]
    flat = lambda a: a.reshape(1, W_IN_ROWS, 128)
    outs = dict(
        w_in=[o.reshape(w_in.shape) for o in
              _adamw("adamw_in", r_in[:, None], flat(w_in), flat(m_w_in), flat(v_w_in), W_IN_TILE)],
        w_out=_adamw("adamw_out", r_out, w_out, m_w_out, v_w_out, 128),
        w_ple=_adamw("adamw_ple", r_ple, w_ple, m_w_ple, v_w_ple, 256),
        w_pg=_adamw("adamw_gate", r_pg, w_ple_gate, m_w_ple_gate, v_w_ple_gate, 128),
        small=_adamw("adamw_small", r_small[:, None], small_of(norm_g, ple_norm_g, qk_norm_g, b_f),
                     small_of(m_norm_g, m_ple_norm_g, m_qk_norm_g, m_b_f),
                     small_of(v_norm_g, v_ple_norm_g, v_qk_norm_g, v_b_f), SMALL_ROWS))
    leaves = []
    for kind in range(4):
        ng, pg, qk, bf = _unpack_small(outs["small"][kind][0])
        leaves += [ng, outs["w_in"][kind], bf, qk, outs["w_out"][kind], outs["w_ple"][kind], pg, outs["w_pg"][kind]]
    return (outs["small"][0][0, LOSS_ROW, 0], dx[None], *leaves)
```

```python
import functools

import jax
import jax.numpy as jnp
from jax import lax
from jax.experimental import pallas as pl
from jax.experimental.pallas import tpu as pltpu

F32 = jnp.float32
BF16 = jnp.bfloat16

D_MODEL = 1024
HEAD_DIM = 64
N_HEADS = 8
HEAD_PAD = 128
D_BRANCH = N_HEADS * HEAD_DIM
N_MAIN = 8 * D_BRANCH
N_IN = N_MAIN + N_HEADS
PLE_DIM = 256
ROPE_THETA = 500000.0
ROPE_HALF = 8
EPS = 1e-6
NEG = -1e30
SCALE = HEAD_DIM ** -0.5
LOG2E = 1.4426950408889634
LN2 = 0.6931471805599453
DILATED_PATTERNS = ((128, 1), (512, 4), (2048, 16))
N_DEV = 8
W_IN_SHARD = N_IN // N_DEV
W_IN_ROWS = 2 * D_MODEL * W_IN_SHARD // 128
W_IN_TILE = W_IN_ROWS // 19

ADAM_LR = 0.001
ADAM_B1 = 0.9
ADAM_B2 = 0.999
ADAM_EPS = 1e-08
ADAM_WD = 0.01
ADAM_STEP = 10

ATT_T = 512
ATT_FWD_HEADS = 8
ATT_BWD_HEADS = 4
ATT_CHUNK = 32
ATT_CHUNK_CAUSAL_FWD = 16
TOK_T = 256
VMEM_LIMIT = 60 * 1024 * 1024


def _slab_spec(lead, block, index):
    return pl.BlockSpec((None,) * len(lead) + block, lambda *g: (*lead, *index(*g)))


def _cp(*sem):
    return pltpu.CompilerParams(dimension_semantics=sem, vmem_limit_bytes=VMEM_LIMIT)


def _sigmoid(x):
    return 1.0 / (1.0 + jnp.exp(-x))


def _split3(x):
    hi = x.astype(BF16)
    r1 = x - hi.astype(F32)
    mid = r1.astype(BF16)
    lo = (r1 - mid.astype(F32)).astype(BF16)
    return hi, mid, lo


def _dot(a, b):
    return jnp.dot(a, b, preferred_element_type=F32)


def _dot_nt(a, b):
    return lax.dot_general(a, b, (((1,), (1,)), ((), ())), preferred_element_type=F32)


def _dot_tn(a, b):
    return lax.dot_general(a, b, (((0,), (0,)), ((), ())), preferred_element_type=F32)


def _inproj_fwd(h, g, wm, wf):
    S = h.shape[0]

    def body(h_ref, g_ref, wm_ref, wf_ref, zm_ref, zf_ref, u_ref):
        x = h_ref[...]
        r = lax.rsqrt(jnp.mean(x * x, axis=-1, keepdims=True) + EPS)
        u = (x * r * g_ref[...]).astype(BF16)
        u_ref[...] = u
        zm_ref[...] = _dot(u, wm_ref[...]).astype(BF16)
        zf_ref[...] = _dot(u, wf_ref[...])

    return pl.pallas_call(
        body, name="inproj_fwd", grid=(S // TOK_T,),
        in_specs=[pl.BlockSpec((TOK_T, D_MODEL), lambda i: (i, 0)),
                  pl.BlockSpec((1, D_MODEL), lambda i: (0, 0)),
                  pl.BlockSpec((D_MODEL, N_MAIN), lambda i: (0, 0)),
                  pl.BlockSpec((D_MODEL, 128), lambda i: (0, 0))],
        out_specs=[pl.BlockSpec((TOK_T, N_MAIN), lambda i: (i, 0)),
                   pl.BlockSpec((TOK_T, 128), lambda i: (i, 0)),
                   pl.BlockSpec((TOK_T, D_MODEL), lambda i: (i, 0))],
        out_shape=[jax.ShapeDtypeStruct((S, N_MAIN), BF16), jax.ShapeDtypeStruct((S, 128), F32),
                   jax.ShapeDtypeStruct((S, D_MODEL), BF16)],
        compiler_params=_cp("parallel"),
    )(h, g, wm, wf)


def _log_sigmoid(x):
    return jnp.minimum(x, 0.0) - jnp.log(1.0 + jnp.exp(-jnp.abs(x)))


def _same_head():
    r = lax.broadcasted_iota(jnp.int32, (HEAD_PAD, HEAD_PAD), 0) // HEAD_DIM
    c = lax.broadcasted_iota(jnp.int32, (HEAD_PAD, HEAD_PAD), 1) // HEAD_DIM
    return (r == c).astype(BF16)


def _pair_mean(x, same_head):
    hi = x.astype(BF16)
    lo = (x - hi.astype(F32)).astype(BF16)
    return (_dot(hi, same_head) + _dot(lo, same_head)) * (1.0 / HEAD_DIM)


def _pair_rsqrt(x, same_head):
    return lax.rsqrt(_pair_mean(x * x, same_head) + EPS)


def _prep_fwd(zm, zf, bf, tril, qkg, rope_c, rope_a, rope_b):
    S = zm.shape[0]
    shp = jax.ShapeDtypeStruct((N_HEADS, S, HEAD_PAD), BF16)

    def body(z_ref, zf_ref, b_ref, tri_ref, g_ref, rc_ref, ra_ref, rb_ref,
             qa_ref, ka_ref, va_ref, qb_ref, kb_ref, vb_ref, carry):
        @pl.when(pl.program_id(0) == 0)
        def _():
            carry[...] = jnp.zeros_like(carry)

        tri = tri_ref[...]
        cs = sum(_dot(tri, part) for part in _split3(_log_sigmoid(zf_ref[...] + b_ref[...]))) + carry[...]
        carry[...] = cs[TOK_T - 1:TOK_T, :]
        lane = lax.broadcasted_iota(jnp.int32, (TOK_T, HEAD_PAD), 1)
        lo_half = lane < HEAD_DIM
        aug = (lane >= HEAD_DIM) & (lane < HEAD_DIM + 3)
        q_pad = jnp.where(aug, -1.0, 0.0)
        rc, ra, rb = rc_ref[...], ra_ref[...], rb_ref[...]
        same_head = _same_head()

        def norm(col, gi):
            x = z_ref[:, col:col + HEAD_PAD].astype(F32)
            return x * _pair_rsqrt(x, same_head) * g_ref[gi:gi + 1, :]

        def rope(y):
            return y * rc + pltpu.roll(y, HEAD_PAD - ROPE_HALF, 1) * ra + pltpu.roll(y, ROPE_HALF, 1) * rb

        def put(ref, pi, y, pad_even, pad_odd):
            ref[2 * pi] = jnp.where(lo_half, y, pad_even).astype(BF16)
            ref[2 * pi + 1] = jnp.where(lo_half, pltpu.roll(y, HEAD_DIM, 1), pad_odd).astype(BF16)

        def k_pad(h):
            ch = cs[:, h:h + 1] * LOG2E
            hi = ch.astype(BF16).astype(F32)
            mid = (ch - hi).astype(BF16).astype(F32)
            lo = ch - hi - mid
            ones = jnp.where(lane == HEAD_DIM + 3, 1.0, 0.0)
            return jnp.where(lane == HEAD_DIM, hi, jnp.where(lane == HEAD_DIM + 1, mid,
                                                              jnp.where(lane == HEAD_DIM + 2, lo, ones)))

        for pi in range(N_HEADS // 2):
            col = HEAD_PAD * pi
            put(qa_ref, pi, norm(col, 0) * (SCALE * LOG2E), q_pad, q_pad)
            put(ka_ref, pi, norm(D_BRANCH + col, 1), k_pad(2 * pi), k_pad(2 * pi + 1))
            put(va_ref, pi, z_ref[:, 2 * D_BRANCH + col:2 * D_BRANCH + col + HEAD_PAD].astype(F32), 0.0, 0.0)
            put(qb_ref, pi, rope(norm(4 * D_BRANCH + col, 2)) * (SCALE * LOG2E), 0.0, 0.0)
            put(kb_ref, pi, rope(norm(5 * D_BRANCH + col, 3)), 0.0, 0.0)
            put(vb_ref, pi, z_ref[:, 6 * D_BRANCH + col:6 * D_BRANCH + col + HEAD_PAD].astype(F32), 0.0, 0.0)

    tok = lambda w: pl.BlockSpec((TOK_T, w), lambda i: (i, 0))
    head = pl.BlockSpec((N_HEADS, TOK_T, HEAD_PAD), lambda i: (0, i, 0))
    return pl.pallas_call(
        body, name="prep_fwd", grid=(S // TOK_T,),
        in_specs=[tok(N_MAIN), tok(128), pl.BlockSpec((1, 128), lambda i: (0, 0)),
                  pl.BlockSpec((TOK_T, TOK_T), lambda i: (0, 0)), pl.BlockSpec((4, 128), lambda i: (0, 0)),
                  tok(128), tok(128), tok(128)],
        out_specs=[head] * 6, out_shape=[shp] * 6,
        scratch_shapes=[pltpu.VMEM((1, 128), F32)],
        compiler_params=_cp("arbitrary"),
    )(zm, zf, bf, tril, qkg, rope_c, rope_a, rope_b)


def _pair(ref, pi, lo_half):
    return jnp.where(lo_half, ref[2 * pi].astype(F32), pltpu.roll(ref[2 * pi + 1].astype(F32), HEAD_DIM, 1))


def _mid_fwd(oa, ob, zm, h0, p, w_out, w_pg, w_ple, g2, target=None):
    S = h0.shape[0]
    p, p_lead = p

    def body(oa_ref, ob_ref, ga_ref, gb_ref, h0_ref, p_ref, wo_ref, wg_ref, wp_ref, g2_ref, *rest):
        t_ref, rest = (rest[0], rest[1:]) if target is not None else (None, rest)
        y_ref, h1_ref, h2_ref, u2_ref, e_ref, gate_ref, *loss_ref = rest
        parts = []
        for o_ref, g_ref in ((oa_ref, ga_ref), (ob_ref, gb_ref)):
            for pi in range(N_HEADS // 2):
                g = g_ref[:, HEAD_PAD * pi:HEAD_PAD * (pi + 1)].astype(F32)
                parts.append((o_ref[pi] * (g * _sigmoid(g))).astype(BF16))
        y = jnp.concatenate(parts, axis=1)
        y_ref[...] = y
        h1 = h0_ref[...] + _dot(y, wo_ref[...])
        h1_ref[...] = h1
        r = lax.rsqrt(jnp.mean(h1 * h1, axis=-1, keepdims=True) + EPS)
        u2 = (h1 * r * g2_ref[...]).astype(BF16)
        u2_ref[...] = u2
        gate = _sigmoid(_dot(u2, wg_ref[...]))
        e = _dot(p_ref[...].astype(BF16), wp_ref[...])
        e_ref[...] = e.astype(BF16)
        gate_ref[...] = gate.astype(BF16)
        h2 = h1 + e * gate
        if target is None:
            h2_ref[...] = h2
        else:
            @pl.when(pl.program_id(0) == 0)
            def _():
                loss_ref[0][...] = jnp.zeros_like(loss_ref[0])

            err = h2 - t_ref[...]
            h2_ref[...] = err * (1.0 / D_MODEL)
            part = jnp.sum(jnp.sum(err * err, axis=1, keepdims=True), axis=0, keepdims=True)
            loss_ref[0][...] += part * (0.5 / D_MODEL)

    tok = lambda w: pl.BlockSpec((TOK_T, w), lambda i: (i, 0))
    head = pl.BlockSpec((N_HEADS // 2, TOK_T, HEAD_PAD), lambda i: (0, i, 0))
    full = lambda a, b: pl.BlockSpec((a, b), lambda i: (0, 0))
    act = lambda dt: jax.ShapeDtypeStruct((S, D_MODEL), dt)
    fused = target is not None
    return pl.pallas_call(
        body, name="mid_fwd_loss" if fused else "mid_fwd", grid=(S // TOK_T,),
        in_specs=[head, head,
                  pl.BlockSpec((TOK_T, D_BRANCH), lambda i: (i, 3)), pl.BlockSpec((TOK_T, D_BRANCH), lambda i: (i, 7)),
                  tok(D_MODEL), _slab_spec(p_lead, (TOK_T, PLE_DIM), lambda i: (i, 0)), full(D_MODEL, D_MODEL),
                  full(D_MODEL, D_MODEL), full(PLE_DIM, D_MODEL), full(1, D_MODEL)] + [tok(D_MODEL)] * fused,
        out_specs=[tok(D_MODEL)] * 6 + [full(8, 128)] * fused,
        out_shape=[act(BF16), act(F32), act(F32), act(BF16), act(BF16), act(BF16)]
        + [jax.ShapeDtypeStruct((8, 128), F32)] * fused,
        compiler_params=_cp("arbitrary" if fused else "parallel"),
    )(oa, ob, zm, zm, h0, p, w_out, w_pg, w_ple, g2, *([target] * fused))


def _bias_tables(full_range):
    T = ATT_T
    nb = 1 if full_range else DILATED_PATTERNS[-1][0] // T + 1
    r = lax.broadcasted_iota(jnp.int32, (nb, T, T), 2)
    c = lax.broadcasted_iota(jnp.int32, (nb, T, T), 1)
    b = lax.broadcasted_iota(jnp.int32, (nb, T, T), 0)
    delta = T * b + r - c
    if full_range:
        bias = jnp.where(delta >= 0, 0.0, NEG).astype(F32)
    else:
        mult = jnp.zeros((nb, T, T), F32)
        for window, dil in DILATED_PATTERNS:
            ok = (delta >= 0) & (delta <= window) & (delta % dil == 0)
            mult = mult + ok.astype(F32)
        bias = jnp.where(mult > 0, jnp.log2(jnp.maximum(mult, 1.0)), NEG).astype(F32)
    return bias


def _call_with_rider(body, name, grid, rider, in_specs, out_specs, out_shape, scratch_shapes, operands,
                     semantics=("parallel", "arbitrary")):
    if rider is None:
        return pl.pallas_call(body, name=name, grid=grid, in_specs=in_specs, out_specs=out_specs,
                              out_shape=out_shape, scratch_shapes=scratch_shapes,
                              compiler_params=_cp(*semantics))(*operands)
    n, n_in, n_out = rider.n, len(in_specs), len(out_specs)

    def wrapped(*refs):
        ins, r_ins = refs[:n_in], refs[n_in:n_in + n]
        outs, r_outs = refs[n_in + n:n_in + n + n_out], refs[n_in + n + n_out:n_in + 2 * n + n_out]
        scratch, sems = refs[n_in + 2 * n + n_out:-3], refs[-3:]
        step = [pl.program_id(a) for a in range(len(grid))]

        @pl.when(functools.reduce(jnp.logical_and, [s == 0 for s in step]))
        def _():
            rider.start(r_ins, r_outs, sems)

        body(*ins, *outs, *scratch)

        @pl.when(functools.reduce(jnp.logical_and, [s == g - 1 for s, g in zip(step, grid)]))
        def _():
            rider.wait(r_ins, r_outs, sems)

    return pl.pallas_call(
        wrapped, name=name, grid=grid, in_specs=list(in_specs) + rider.in_specs,
        out_specs=list(out_specs) + rider.out_specs, out_shape=list(out_shape) + rider.out_shapes,
        scratch_shapes=list(scratch_shapes) + rider.scratch_shapes,
        compiler_params=_cp(*["arbitrary"] * len(grid)))(*operands, *rider.arrays)


def _attn_fwd(q, k, v, table_t, full_range, name, rider=None):
    H, S, _ = q.shape
    T = ATT_T
    nb = table_t.shape[0]
    HB = ATT_FWD_HEADS
    KC = ATT_CHUNK_CAUSAL_FWD if full_range else ATT_CHUNK
    chunks = [slice(c, c + KC) for c in range(0, T, KC)]
    fold = lambda x, op: functools.reduce(op, [x[r:r + 8] for r in range(0, KC, 8)])

    def body(q_ref, k_ref, v_ref, tab_ref, o_ref, lse_ref, *scratch):
        st_refs, pt_refs, acc_refs = scratch[:HB], scratch[HB:2 * HB], scratch[2 * HB:]
        i = pl.program_id(1)
        rows = lambda j: pl.ds(pl.multiple_of(j * T, T), T)

        def scores(hh, j):
            st_refs[hh][...] = _dot_nt(k_ref[hh, rows(j), :], q_ref[hh])

        def block(j, b, nxt, stats):
            out = []
            for hh, (m, l) in enumerate(stats):
                st_ref, pt_ref, acc_ref = st_refs[hh], pt_refs[hh], acc_refs[hh]
                mx = None
                for ch in chunks:
                    x = st_ref[ch, :]
                    if b is not None:
                        x = x + tab_ref[b, ch, :]
                        st_ref[ch, :] = x
                    x = fold(x, jnp.maximum)
                    mx = x if mx is None else jnp.maximum(mx, x)
                m_new = jnp.maximum(m, jnp.max(mx, axis=0, keepdims=True))
                alpha = jnp.exp2(m - m_new)
                ls = None
                for ch in chunks:
                    pc = jnp.exp2(st_ref[ch, :] - m_new)
                    pt_ref[ch, :] = pc.astype(BF16)
                    pc = fold(pc, jnp.add)
                    ls = pc if ls is None else ls + pc
                if nxt is not None:
                    scores(hh, nxt)
                acc_ref[...] = alpha * acc_ref[...] + _dot_tn(v_ref[hh, rows(j), :], pt_ref[...])
                out.append((m_new, alpha * l + jnp.sum(ls, axis=0, keepdims=True)))
            return tuple(out)

        lo = 0 if full_range else jnp.maximum(i - (nb - 1), 0)
        for hh in range(HB):
            acc_refs[hh][...] = jnp.zeros_like(acc_refs[hh])
            scores(hh, lo)
        stats = lax.fori_loop(lo, i, lambda j, st: block(j, None if full_range else i - j, j + 1, st),
                              ((jnp.full((1, T), NEG, F32), jnp.zeros((1, T), F32)),) * HB)
        stats = block(i, 0, None, stats)
        o_t = [acc_refs[hh][...] * (1.0 / l) for hh, (m, l) in enumerate(stats)]
        for hh, (m, l) in enumerate(stats):
            lse_ref[hh, 0] = m + jnp.log2(l)
        for hp in range(HB // 2):
            o_ref[hp] = jnp.concatenate([o_t[2 * hp][:HEAD_DIM], o_t[2 * hp + 1][:HEAD_DIM]], axis=0).T

    return _call_with_rider(
        body, name, (H // HB, S // T), rider,
        in_specs=[pl.BlockSpec((HB, T, HEAD_PAD), lambda h, i: (h, i, 0)),
                  pl.BlockSpec((HB, S, HEAD_PAD), lambda h, i: (h, 0, 0), pipeline_mode=pl.Buffered(1)),
                  pl.BlockSpec((HB, S, HEAD_PAD), lambda h, i: (h, 0, 0), pipeline_mode=pl.Buffered(1)),
                  pl.BlockSpec((nb, T, T), lambda h, i: (0, 0, 0), pipeline_mode=pl.Buffered(1))],
        out_specs=[pl.BlockSpec((HB // 2, T, HEAD_PAD), lambda h, i: (h, i, 0)),
                   pl.BlockSpec((HB, 1, 1, T), lambda h, i: (h, i, 0, 0))],
        out_shape=[jax.ShapeDtypeStruct((H // 2, S, HEAD_PAD), F32), jax.ShapeDtypeStruct((H, S // T, 1, T), F32)],
        scratch_shapes=([pltpu.VMEM((T, T), F32)] * HB + [pltpu.VMEM((T, T), BF16)] * HB
                        + [pltpu.VMEM((HEAD_PAD, T), F32)] * HB),
        operands=(q, k, v, table_t))


def _attn_bwd(q, k, v, do, lse, dd, table_t, full_range, name, rider=None, narrow=(False, False, False)):
    H, S, _ = q.shape
    T = ATT_T
    nq = S // T
    nb = table_t.shape[0]
    HB = ATT_BWD_HEADS
    KC = ATT_CHUNK
    chunks = [slice(c, c + KC) for c in range(0, T, KC)]

    def body(q_ref, do_ref, lse_ref, dd_ref, k_ref, v_ref, tab_ref, dq_hbm, dk_ref, dv_ref, *scratch):
        st_refs, dpt_refs, pt_refs, dst_refs = (scratch[n * HB:(n + 1) * HB] for n in range(4))
        dq_ref, dk_acc, dv_acc, *dq_cast, dq_sem = scratch[4 * HB:]
        h = pl.program_id(0)
        j = pl.program_id(1)

        @pl.when(j == 0)
        def _():
            dq_ref[...] = jnp.zeros_like(dq_ref)

        dk_acc[...] = jnp.zeros_like(dk_acc)
        dv_acc[...] = jnp.zeros_like(dv_acc)

        def step(i, b):
            rows = pl.ds(pl.multiple_of(i * T, T), T)
            for hh in range(HB):
                st_refs[hh][...] = _dot_nt(k_ref[hh], q_ref[hh, rows, :])
                dpt_refs[hh][...] = _dot_nt(v_ref[hh], do_ref[hh, rows, :])
            for hh in range(HB):
                lse_i = lse_ref[hh, i]
                dd_i = dd_ref[hh, i]
                for ch in chunks:
                    x = st_refs[hh][ch, :]
                    if b is not None:
                        x = x + tab_ref[b, ch, :]
                    pc = jnp.exp2(x - lse_i)
                    pt_refs[hh][ch, :] = pc.astype(BF16)
                    dst_refs[hh][ch, :] = (pc * (dpt_refs[hh][ch, :] - dd_i)).astype(BF16)
                dv_acc[hh] += _dot(pt_refs[hh][...], do_ref[hh, rows, :])
                dk_acc[hh] += _dot(dst_refs[hh][...], q_ref[hh, rows, :])
                dq_ref[hh, rows, :] += _dot_tn(dst_refs[hh][...], k_ref[hh])

        step(j, 0)
        if full_range:
            pl.loop(j + 1, nq)(lambda i: step(i, None))
        else:
            pl.loop(j + 1, jnp.minimum(j + nb, nq))(lambda i: step(i, i - j))
        dk_ref[...] = dk_acc[...].astype(dk_ref.dtype)
        dv_ref[...] = dv_acc[...].astype(dv_ref.dtype)

        @pl.when(j == nq - 1)
        def _():
            src = dq_ref
            if narrow[0]:
                src, = dq_cast
                src[...] = dq_ref[...].astype(BF16)
            out = pltpu.make_async_copy(src, dq_hbm.at[pl.ds(h * HB, HB)], dq_sem)
            out.start()
            out.wait()

    once = dict(pipeline_mode=pl.Buffered(1))
    per_head = pl.BlockSpec((HB, S, HEAD_PAD), lambda h, j: (h, 0, 0), **once)
    rows = pl.BlockSpec((HB, nq, 1, T), lambda h, j: (h, 0, 0, 0))
    blk = pl.BlockSpec((HB, T, HEAD_PAD), lambda h, j: (h, j, 0))
    shp = [jax.ShapeDtypeStruct((H, S, HEAD_PAD), BF16 if nar else F32) for nar in narrow]
    acc = pltpu.VMEM((HB, T, HEAD_PAD), F32)
    return _call_with_rider(
        body, name, (H // HB, nq), rider,
        in_specs=[per_head, per_head, rows, rows, blk, blk,
                  pl.BlockSpec((nb, T, T), lambda h, j: (0, 0, 0), **once)],
        out_specs=[pl.BlockSpec(memory_space=pltpu.HBM), blk, blk], out_shape=shp,
        scratch_shapes=([pltpu.VMEM((T, T), F32)] * (2 * HB) + [pltpu.VMEM((T, T), BF16)] * (2 * HB)
                        + [pltpu.VMEM((HB, S, HEAD_PAD), F32), acc, acc]
                        + [pltpu.VMEM((HB, S, HEAD_PAD), BF16)] * narrow[0] + [pltpu.SemaphoreType.DMA]),
        operands=(q, do, lse, dd, k, v, table_t))


def _mid_bwd(dh2, h1, e, gate, g2, w_pg, w_out, oa, ob, zm):
    S = dh2.shape[0]

    def body(dh2_ref, h1_ref, e_ref, gate_ref, g2_ref, wg_ref, wo_ref, oa_ref, ob_ref, ga_ref, gb_ref,
             dh1_ref, dh1b_ref, de_ref, dpre_ref, doa_ref, dob_ref, dga_ref, dgb_ref, dd_ref, dg2_ref):
        @pl.when(pl.program_id(0) == 0)
        def _():
            dg2_ref[...] = jnp.zeros_like(dg2_ref)

        lane = lax.broadcasted_iota(jnp.int32, (TOK_T, HEAD_PAD), 1)
        lo_half = lane < HEAD_DIM
        dh2 = dh2_ref[...]
        gate = gate_ref[...]
        de_ref[...] = (dh2 * gate).astype(BF16)
        dpre = (dh2 * e_ref[...] * gate * (1.0 - gate)).astype(BF16)
        dpre_ref[...] = dpre
        du2 = _dot_nt(dpre, wg_ref[...])
        h1 = h1_ref[...]
        r = lax.rsqrt(jnp.mean(h1 * h1, axis=-1, keepdims=True) + EPS)
        xh = h1 * r
        a = du2 * g2_ref[...]
        dh1 = dh2 + r * (a - xh * jnp.mean(a * xh, axis=-1, keepdims=True))
        dg2_ref[...] += jnp.sum(du2 * xh, axis=0, keepdims=True)
        dh1_ref[...] = dh1
        dh1b = dh1.astype(BF16)
        dh1b_ref[...] = dh1b
        dy = _dot_nt(dh1b, wo_ref[...])
        dd = jnp.zeros((TOK_T, HEAD_PAD), F32)
        for bi, (o_ref, g_ref, do_ref, dg_ref) in enumerate(
                ((oa_ref, ga_ref, doa_ref, dga_ref), (ob_ref, gb_ref, dob_ref, dgb_ref))):
            for pi in range(N_HEADS // 2):
                col = bi * D_BRANCH + HEAD_PAD * pi
                dyp = dy[:, col:col + HEAD_PAD]
                g = g_ref[:, HEAD_PAD * pi:HEAD_PAD * (pi + 1)].astype(F32)
                sg = _sigmoid(g)
                o_pair = o_ref[pi]
                dg_ref[:, HEAD_PAD * pi:HEAD_PAD * (pi + 1)] = (
                    dyp * o_pair * (sg * (1.0 + g * (1.0 - sg)))).astype(BF16)
                dop = dyp * (g * sg)
                prod = dop * o_pair
                for hh, d_head, mine in ((2 * pi, dop, lo_half),
                                         (2 * pi + 1, pltpu.roll(dop, HEAD_DIM, 1), ~lo_half)):
                    do_ref[hh] = jnp.where(lo_half, d_head, 0.0).astype(BF16)
                    dsum = jnp.sum(jnp.where(mine, prod, 0.0), axis=1, keepdims=True)
                    dd = dd + jnp.where(lane == bi * N_HEADS + hh, dsum, 0.0)
        dd_ref[...] = dd.T[:2 * N_HEADS, :]

    tok = lambda w: pl.BlockSpec((TOK_T, w), lambda i: (i, 0))
    head = pl.BlockSpec((N_HEADS, TOK_T, HEAD_PAD), lambda i: (0, i, 0))
    pairs = pl.BlockSpec((N_HEADS // 2, TOK_T, HEAD_PAD), lambda i: (0, i, 0))
    full = lambda a, b: pl.BlockSpec((a, b), lambda i: (0, 0))
    act = lambda w, dt: jax.ShapeDtypeStruct((S, w), dt)
    hshape = lambda w, dt: jax.ShapeDtypeStruct((N_HEADS, S, w), dt)
    return pl.pallas_call(
        body, name="mid_bwd", grid=(S // TOK_T,),
        in_specs=[tok(D_MODEL)] * 4 + [full(1, D_MODEL), full(D_MODEL, D_MODEL), full(D_MODEL, D_MODEL), pairs, pairs,
                                      pl.BlockSpec((TOK_T, D_BRANCH), lambda i: (i, 3)),
                                      pl.BlockSpec((TOK_T, D_BRANCH), lambda i: (i, 7))],
        out_specs=[tok(D_MODEL)] * 4 + [head, head, tok(D_BRANCH), tok(D_BRANCH),
                                       pl.BlockSpec((2 * N_HEADS, TOK_T), lambda i: (0, i)), full(1, D_MODEL)],
        out_shape=[act(D_MODEL, F32), act(D_MODEL, BF16), act(D_MODEL, BF16), act(D_MODEL, BF16),
                   hshape(HEAD_PAD, BF16), hshape(HEAD_PAD, BF16), act(D_BRANCH, BF16), act(D_BRANCH, BF16),
                   jax.ShapeDtypeStruct((2 * N_HEADS, S), F32), jax.ShapeDtypeStruct((1, D_MODEL), F32)],
        compiler_params=_cp("arbitrary"),
    )(dh2, h1, e, gate, g2, w_pg, w_out, oa, ob, zm, zm)


def _prep_bwd(dqa, dka, dva, dqb, dkb, dvb, zm, qkg, rope_c, rope_a, rope_b, dga, dgb, zf, bf, triu):
    S = zm.shape[0]
    n = S // TOK_T

    def body(dqa_ref, dka_ref, dva_ref, dqb_ref, dkb_ref, dvb_ref, z_ref, g_ref, rc_ref, ra_ref, rb_ref,
             dga_ref, dgb_ref, zf_ref, b_ref, tri_ref, dz_ref, dzf_ref, dqkg_ref, db_ref, carry):
        @pl.when(pl.program_id(0) == 0)
        def _():
            dqkg_ref[...] = jnp.zeros_like(dqkg_ref)
            db_ref[...] = jnp.zeros_like(db_ref)
            carry[...] = jnp.zeros_like(carry)

        lane = lax.broadcasted_iota(jnp.int32, (TOK_T, HEAD_PAD), 1)
        lo_half = lane < HEAD_DIM
        rc, ra, rb = rc_ref[...], ra_ref[...], rb_ref[...]
        same_head = _same_head()

        def unrope(dy):
            return dy * rc + pltpu.roll(dy * ra, ROPE_HALF, 1) + pltpu.roll(dy * rb, HEAD_PAD - ROPE_HALF, 1)

        def norm_bwd(col, gi, dy):
            x = z_ref[:, col:col + HEAD_PAD].astype(F32)
            r = _pair_rsqrt(x, same_head)
            xh = x * r
            dqkg_ref[gi:gi + 1, :] += jnp.sum(dy * xh, axis=0, keepdims=True)
            a = dy * g_ref[gi:gi + 1, :]
            dz_ref[:, col:col + HEAD_PAD] = (r * (a - xh * _pair_mean(a * xh, same_head))).astype(BF16)

        dc = jnp.zeros((TOK_T, HEAD_PAD), F32)
        for pi in range(N_HEADS // 2):
            col = HEAD_PAD * pi
            norm_bwd(col, 0, _pair(dqa_ref, pi, lo_half) * SCALE)
            norm_bwd(D_BRANCH + col, 1, _pair(dka_ref, pi, lo_half) * LN2)
            dz_ref[:, 2 * D_BRANCH + col:2 * D_BRANCH + col + HEAD_PAD] = _pair(dva_ref, pi, lo_half).astype(BF16)
            norm_bwd(4 * D_BRANCH + col, 2, unrope(_pair(dqb_ref, pi, lo_half) * SCALE))
            norm_bwd(5 * D_BRANCH + col, 3, unrope(_pair(dkb_ref, pi, lo_half) * LN2))
            dz_ref[:, 6 * D_BRANCH + col:6 * D_BRANCH + col + HEAD_PAD] = _pair(dvb_ref, pi, lo_half).astype(BF16)
            for hh in (2 * pi, 2 * pi + 1):
                dch = dka_ref[hh][:, HEAD_DIM:HEAD_DIM + 1] + dqa_ref[hh][:, HEAD_DIM + 3:HEAD_DIM + 4]
                dc = dc + jnp.where(lane == hh, dch, 0.0)
        dz_ref[:, 3 * D_BRANCH:4 * D_BRANCH] = dga_ref[...]
        dz_ref[:, 7 * D_BRANCH:8 * D_BRANCH] = dgb_ref[...]
        tri = tri_ref[...]
        dlf = sum(_dot(tri, part) for part in _split3(dc)) + carry[...]
        carry[...] = dlf[0:1, :]
        dfa = dlf * (1.0 - _sigmoid(zf_ref[...] + b_ref[...]))
        dzf_ref[...] = dfa.astype(BF16)
        db_ref[...] += jnp.sum(dfa, axis=0, keepdims=True)

    tok = lambda w: pl.BlockSpec((TOK_T, w), lambda i: (n - 1 - i, 0))
    head = pl.BlockSpec((N_HEADS, TOK_T, HEAD_PAD), lambda i: (0, n - 1 - i, 0))
    fixed = lambda a, b: pl.BlockSpec((a, b), lambda i: (0, 0))
    return pl.pallas_call(
        body, name="prep_bwd", grid=(n,),
        in_specs=[head] * 6 + [tok(N_MAIN), fixed(4, 128), tok(128), tok(128), tok(128), tok(D_BRANCH), tok(D_BRANCH),
                               tok(128), fixed(1, 128), fixed(TOK_T, TOK_T)],
        out_specs=[tok(N_MAIN), tok(128), fixed(4, 128), fixed(1, 128)],
        out_shape=[jax.ShapeDtypeStruct((S, N_MAIN), BF16), jax.ShapeDtypeStruct((S, 128), BF16),
                   jax.ShapeDtypeStruct((4, 128), F32), jax.ShapeDtypeStruct((1, 128), F32)],
        scratch_shapes=[pltpu.VMEM((1, 128), F32)],
        compiler_params=_cp("arbitrary"),
    )(dqa, dka, dva, dqb, dkb, dvb, zm, qkg, rope_c, rope_a, rope_b, dga, dgb, zf, bf, triu)


def _inproj_bwd(dzm, dzf, wm, wf, h0, dh1, g, rider=None):
    S = h0.shape[0]

    def body(dzm_ref, dzf_ref, wm_ref, wf_ref, h_ref, dh1_ref, g_ref, dh0_ref, dg_ref):
        @pl.when(pl.program_id(0) == 0)
        def _():
            dg_ref[...] = jnp.zeros_like(dg_ref)

        du = _dot_nt(dzm_ref[...], wm_ref[...]) + _dot_nt(dzf_ref[...], wf_ref[...])
        x = h_ref[...]
        r = lax.rsqrt(jnp.mean(x * x, axis=-1, keepdims=True) + EPS)
        xh = x * r
        a = du * g_ref[...]
        dh0_ref[...] = dh1_ref[...] + r * (a - xh * jnp.mean(a * xh, axis=-1, keepdims=True))
        dg_ref[...] += jnp.sum(du * xh, axis=0, keepdims=True)

    tok = lambda w: pl.BlockSpec((TOK_T, w), lambda i: (i, 0))
    full = lambda a, b: pl.BlockSpec((a, b), lambda i: (0, 0))
    return _call_with_rider(
        body, "inproj_bwd", (S // TOK_T,), rider,
        in_specs=[tok(N_MAIN), tok(128), full(D_MODEL, N_MAIN), full(D_MODEL, 128), tok(D_MODEL), tok(D_MODEL),
                  full(1, D_MODEL)],
        out_specs=[tok(D_MODEL), full(1, D_MODEL)],
        out_shape=[jax.ShapeDtypeStruct((S, D_MODEL), F32), jax.ShapeDtypeStruct((1, D_MODEL), F32)],
        scratch_shapes=[], operands=(dzm, dzf, wm, wf, h0, dh1, g), semantics=("arbitrary",))


def _wgrad(a, b, name, a_lead=()):
    S, M = a.shape[len(a_lead):]
    N = b.shape[1]
    tn = min(N, 2048)
    ts = 512
    last = S // ts - 1

    def body(a_ref, b_ref, o_ref, acc_ref):
        @pl.when(pl.program_id(1) == 0)
        def _():
            acc_ref[...] = jnp.zeros_like(acc_ref)

        acc_ref[...] += _dot_tn(a_ref[...].astype(BF16), b_ref[...])

        @pl.when(pl.program_id(1) == last)
        def _():
            o_ref[...] = acc_ref[...].astype(BF16)

    return pl.pallas_call(
        body, name=name, grid=(N // tn, S // ts),
        in_specs=[_slab_spec(a_lead, (ts, M), lambda n, s: (s, 0)), pl.BlockSpec((ts, tn), lambda n, s: (s, n))],
        out_specs=pl.BlockSpec((M, tn), lambda n, s: (0, n)),
        out_shape=jax.ShapeDtypeStruct((M, N), BF16),
        scratch_shapes=[pltpu.VMEM((M, tn), F32)],
        compiler_params=_cp("parallel", "arbitrary"),
    )(a, b)


def _wgrad_in(u, dzm, dzf):
    S = u.shape[0]
    ts = 512
    last = S // ts - 1

    def body(u_ref, dzm_ref, dzf_ref, om_ref, of_ref, accm_ref, accf_ref):
        @pl.when(pl.program_id(0) == 0)
        def _():
            accm_ref[...] = jnp.zeros_like(accm_ref)
            accf_ref[...] = jnp.zeros_like(accf_ref)

        ub = u_ref[...]
        accm_ref[...] += _dot_tn(ub, dzm_ref[...])
        accf_ref[...] += _dot_tn(ub, dzf_ref[...])

        @pl.when(pl.program_id(0) == last)
        def _():
            om_ref[...] = accm_ref[...].astype(BF16)
            of_ref[...] = accf_ref[...].astype(BF16)

    tok = lambda w: pl.BlockSpec((ts, w), lambda s: (s, 0))
    full = lambda w: pl.BlockSpec((D_MODEL, w), lambda s: (0, 0))
    return pl.pallas_call(
        body, name="wgrad_in", grid=(S // ts,),
        in_specs=[tok(D_MODEL), tok(N_MAIN), tok(128)], out_specs=[full(N_MAIN), full(128)],
        out_shape=[jax.ShapeDtypeStruct((D_MODEL, N_MAIN), BF16), jax.ShapeDtypeStruct((D_MODEL, 128), BF16)],
        scratch_shapes=[pltpu.VMEM((D_MODEL, N_MAIN), F32), pltpu.VMEM((D_MODEL, 128), F32)],
        compiler_params=_cp("arbitrary"),
    )(u, dzm, dzf)


def _rope_tables(positions):
    inv_freq = ROPE_THETA ** (-jnp.arange(ROPE_HALF, dtype=F32) / ROPE_HALF)
    ang = positions.astype(F32)[:, None] * inv_freq
    cos, sin = jnp.cos(ang), jnp.sin(ang)
    S = positions.shape[0]
    one, zero = jnp.ones((S, HEAD_DIM - 2 * ROPE_HALF), F32), jnp.zeros((S, HEAD_DIM - 2 * ROPE_HALF), F32)
    z8 = jnp.zeros((S, ROPE_HALF), F32)
    rc = jnp.concatenate([cos, cos, one], axis=1)
    ra = jnp.concatenate([-sin, z8, zero], axis=1)
    rb = jnp.concatenate([z8, sin, zero], axis=1)
    return tuple(jnp.tile(t, (1, 2)) for t in (rc, ra, rb))


def _in_operands(w_in):
    w_in = w_in.astype(BF16)
    wm = jnp.concatenate([w_in[:, :4 * D_BRANCH], w_in[:, 4 * D_BRANCH + N_HEADS:]], axis=1)
    wf = jnp.pad(w_in[:, 4 * D_BRANCH:4 * D_BRANCH + N_HEADS], ((0, 0), (0, 128 - N_HEADS)))
    return dict(wm=wm, wf=wf)


def _layer_weights(w_in, w_out, w_ple, w_pg):
    return dict(_in_operands(w_in), w_out=w_out.astype(BF16), w_ple=w_ple.astype(BF16), w_pg=w_pg.astype(BF16))


def _row(v, width=128):
    v = v.reshape(1, -1).astype(F32)
    return jnp.pad(v, ((0, 0), (0, width - v.shape[1])))


def _layer_fwd(h0, p, rope, tabs, w, norm_g, b_f, qk_g, ple_g, rider=None, late=None, target=None):
    g1 = norm_g.reshape(1, D_MODEL)
    g2 = ple_g.reshape(1, D_MODEL)
    qkg = jnp.tile(qk_g, (1, 2))
    bf = _row(b_f)
    zm, zf, u = _inproj_fwd(h0, g1, w["wm"], w["wf"])
    qa, ka, va, qb, kb, vb = _prep_fwd(zm, zf, bf, tabs["tril"], qkg, *rope)
    oa, lse_a, *arrivals = _attn_fwd(qa, ka, va, tabs["fox"], True, "fox_fwd", rider)
    if late is not None:
        w = {**w, **late(arrivals)}
    ob, lse_b = _attn_fwd(qb, kb, vb, tabs["dil"], False, "dil_fwd")
    y, h1, h2, u2, e, gate, *loss = _mid_fwd(oa, ob, zm, h0, p, w["w_out"], w["w_pg"], w["w_ple"], g2, target)
    saved = dict(h0=h0, p=p, zm=zm, zf=zf, u=u, qa=qa, ka=ka, va=va, qb=qb, kb=kb, vb=vb, oa=oa, ob=ob,
                 lse_a=lse_a, lse_b=lse_b, y=y, h1=h1, u2=u2, e=e, gate=gate, g1=g1, g2=g2, qkg=qkg, bf=bf, w=w)
    return (h2, saved, arrivals, *loss)


def _layer_bwd(dh2, sv, rope, tabs, make_rider=None, make_last_rider=None):
    S = dh2.shape[0]
    nq = S // ATT_T
    w = sv["w"]
    rows = lambda a: a.reshape(N_HEADS, nq, 1, ATT_T)
    (dh1, dh1b, de, dpre, doa, dob, dga, dgb, dd, dg2) = _mid_bwd(
        dh2, sv["h1"], sv["e"], sv["gate"], sv["g2"], w["w_pg"], w["w_out"], sv["oa"], sv["ob"], sv["zm"])
    dda, ddb = dd[:N_HEADS], dd[N_HEADS:]
    early = dict(w_out=_wgrad(sv["y"], dh1b, "wgrad_out"), w_ple=_wgrad(sv["p"][0], de, "wgrad_ple", sv["p"][1]),
                 w_ple_gate=_wgrad(sv["u2"], dpre, "wgrad_gate"))
    rider = None if make_rider is None else make_rider(early)
    dqa, dka, dva, *arrivals = _attn_bwd(sv["qa"], sv["ka"], sv["va"], doa, sv["lse_a"], rows(dda), tabs["fox"],
                                         True, "fox_bwd", rider, narrow=(False, False, True))
    dqb, dkb, dvb = _attn_bwd(sv["qb"], sv["kb"], sv["vb"], dob, sv["lse_b"], rows(ddb), tabs["dil"], False,
                              "dil_bwd", narrow=(False, True, True))
    dzm, dzf, dqkg, dbf = _prep_bwd(dqa, dka, dva, dqb, dkb, dvb, sv["zm"], sv["qkg"], *rope, dga, dgb,
                                    sv["zf"], sv["bf"], tabs["triu"])
    dwm, dwf = _wgrad_in(sv["u"], dzm, dzf)
    dw_in = jnp.concatenate([dwm[:, :4 * D_BRANCH], dwf[:, :N_HEADS], dwm[:, 4 * D_BRANCH:]], axis=1)
    last_rider = None if make_last_rider is None else make_last_rider(dw_in)
    dh0, dg1, *last_arrivals = _inproj_bwd(dzm, dzf, w["wm"], w["wf"], sv["h0"], dh1, sv["g1"], last_rider)
    grads = dict(norm_g=dg1.reshape(D_MODEL), w_in=dw_in, b_f=dbf[0, :N_HEADS],
                 qk_norm_g=dqkg[:, :HEAD_DIM] + dqkg[:, HEAD_DIM:], ple_norm_g=dg2.reshape(D_MODEL), **early)
    return dh0, grads, arrivals + last_arrivals


def _tables():
    T = TOK_T
    r = lax.broadcasted_iota(jnp.int32, (T, T), 0)
    c = lax.broadcasted_iota(jnp.int32, (T, T), 1)
    return dict(fox=_bias_tables(True), dil=_bias_tables(False),
                tril=(c <= r).astype(BF16), triu=(c >= r).astype(BF16))


def _local_step(x, p, positions, target, layers, small):
    rope = _rope_tables(positions)
    tabs = _tables()
    ws = [_layer_weights(*lw) for lw in layers]
    h = x
    saved = []
    for li, (w, lp, sm) in enumerate(zip(ws, p, small)):
        h, sv, _, *loss = _layer_fwd(h, (lp, ()), rope, tabs, w, *sm, target=target if li == len(ws) - 1 else None)
        saved.append(sv)
    dh, (loss,) = h, loss
    grads = [None] * len(ws)
    for li in reversed(range(len(ws))):
        dh, grads[li], _ = _layer_bwd(dh, saved[li], rope, tabs)
    return loss[0, 0], dh, grads


def _peers():
    x, y, c = lax.axis_index("x"), lax.axis_index("y"), lax.axis_index("c")
    me = 4 * x + 2 * y + c
    flip = lambda v, bit: 1 - v if bit else v
    return me, [(flip(x, k & 4), flip(y, k & 2), flip(c, k & 1)) for k in range(1, N_DEV)]


def _sel(ref, kind, d):
    if kind == "whole":
        return ref
    if kind == "slot":
        return ref.at[d]
    block = pl.ds(pl.multiple_of(d * 128, 128), 128)
    return ref.at[block, :] if kind == "rows" else ref.at[:, block]


class _Pushes:
    def __init__(self, arrays, src_kinds, dst_kinds, out_shapes):
        self.arrays, self.n = list(arrays), len(arrays)
        self.src_kinds, self.dst_kinds = src_kinds, dst_kinds
        self.out_shapes = [jax.ShapeDtypeStruct(s, a.dtype) for s, a in zip(out_shapes, arrays)]
        hbm = pl.BlockSpec(memory_space=pltpu.HBM)
        self.in_specs, self.out_specs = [hbm] * self.n, [hbm] * self.n
        self.scratch_shapes = [pltpu.SemaphoreType.DMA((N_DEV - 1, self.n)),
                               pltpu.SemaphoreType.DMA((N_DEV - 1, self.n)), pltpu.SemaphoreType.DMA((self.n,))]

    def _copies(self, ins, outs, sems):
        send_sems, recv_sems, local_sems = sems
        me, peers = _peers()
        src = lambda a, d: _sel(ins[a], self.src_kinds[a], d)
        dst = lambda a: _sel(outs[a], self.dst_kinds[a], me)
        local = [pltpu.make_async_copy(src(a, me), dst(a), local_sems.at[a]) for a in range(self.n)]
        remote = [pltpu.make_async_remote_copy(
            src_ref=src(a, 4 * px + 2 * py + pc), dst_ref=dst(a), send_sem=send_sems.at[k, a],
            recv_sem=recv_sems.at[k, a], device_id=(px, py, pc), device_id_type=pl.DeviceIdType.MESH)
            for k, (px, py, pc) in enumerate(peers) for a in range(self.n)]
        return local + remote

    def start(self, ins, outs, sems):
        for cp in self._copies(ins, outs, sems):
            cp.start()

    def wait(self, ins, outs, sems):
        for cp in self._copies(ins, outs, sems):
            cp.wait()


def _exchange(name, pushes):
    n = pushes.n

    def body(*refs):
        pushes.start(refs[:n], refs[n:2 * n], refs[2 * n:])
        pushes.wait(refs[:n], refs[n:2 * n], refs[2 * n:])

    return pl.pallas_call(body, name=name, in_specs=pushes.in_specs, out_specs=pushes.out_specs,
                          out_shape=pushes.out_shapes, scratch_shapes=pushes.scratch_shapes)(*pushes.arrays)


def _gather_two_level(shard, name):
    def body(x_ref, out_ref, send_sems, recv_sems, local_sem):
        x, y, c = lax.axis_index("x"), lax.axis_index("y"), lax.axis_index("c")
        me, sibling = (x, y, c), (x, y, 1 - c)
        chips = [(1 - x, y), (x, 1 - y), (1 - x, 1 - y)]
        slot = lambda px, py, pc: out_ref.at[4 * px + 2 * py + pc]

        def copy(k, block, to, src=None):
            return pltpu.make_async_remote_copy(
                src_ref=slot(*block) if src is None else src, dst_ref=slot(*block), send_sem=send_sems.at[k],
                recv_sem=recv_sems.at[k], device_id=to, device_id_type=pl.DeviceIdType.MESH)

        mine = pltpu.make_async_copy(x_ref, slot(*me), local_sem)
        mine.start()
        first = [copy(0, me, sibling, src=x_ref)] + [copy(1 + j, me, (*chip, c), src=x_ref)
                                                     for j, chip in enumerate(chips)]
        for cp in first:
            cp.start()
        passed = [copy(4 + j, (*chip, c), sibling) for j, chip in enumerate(chips)]
        for j, chip in enumerate(chips):
            copy(1 + j, (*chip, c), me).wait_recv()
            passed[j].start()
        copy(0, sibling, me).wait_recv()
        for j, chip in enumerate(chips):
            copy(4 + j, (*chip, 1 - c), me).wait_recv()
        for cp in first + passed:
            cp.wait_send()
        mine.wait()

    hbm = pl.BlockSpec(memory_space=pltpu.HBM)
    return pl.pallas_call(
        body, name=name, in_specs=[hbm], out_specs=hbm,
        out_shape=jax.ShapeDtypeStruct((N_DEV,) + shard.shape, shard.dtype),
        scratch_shapes=[pltpu.SemaphoreType.DMA((N_DEV - 1,)), pltpu.SemaphoreType.DMA((N_DEV - 1,)),
                        pltpu.SemaphoreType.DMA],
    )(shard)


def _gather_pushes(shards, kinds):
    full = {"slot": lambda s: (N_DEV,) + s, "rows": lambda s: (N_DEV * s[0], s[1]),
            "cols": lambda s: (s[0], N_DEV * s[1])}
    return _Pushes(shards, ["whole"] * len(shards), kinds, [full[k](a.shape) for a, k in zip(shards, kinds)])


def _scatter_pushes(partials, kinds):
    part = {"slot": lambda s: s[1:], "rows": lambda s: (128, s[1]), "cols": lambda s: (s[0], 128),
            "whole": lambda s: s}
    return _Pushes(partials, kinds, ["slot"] * len(partials),
                   [(N_DEV,) + part[k](a.shape) for a, k in zip(partials, kinds)])


def _adamw(name, parts, w, m, v, rows):
    L, R, C = w.shape

    def body(p_ref, w_ref, m_ref, v_ref, g_ref, d_ref, nm_ref, nv_ref):
        g = p_ref[0, 0].astype(F32)
        for s in range(1, N_DEV):
            g = g + p_ref[s, 0].astype(F32)
        g_ref[0] = g
        nm = ADAM_B1 * m_ref[0] + (1.0 - ADAM_B1) * g
        nv = ADAM_B2 * v_ref[0] + (1.0 - ADAM_B2) * (g * g)
        nm_ref[0] = nm
        nv_ref[0] = nv
        m_hat = nm / (1.0 - ADAM_B1 ** ADAM_STEP)
        v_hat = nv / (1.0 - ADAM_B2 ** ADAM_STEP)
        d_ref[0] = -ADAM_LR * (m_hat / (jnp.sqrt(v_hat) + ADAM_EPS) + ADAM_WD * w_ref[0])

    blk = pl.BlockSpec((1, rows, C), lambda l, i: (l, i, 0))
    shp = jax.ShapeDtypeStruct((L, R, C), F32)
    return pl.pallas_call(
        body, name=name, grid=(L, R // rows),
        in_specs=[pl.BlockSpec((N_DEV, 1, rows, C), lambda l, i: (0, l, i, 0)), blk, blk, blk],
        out_specs=[blk] * 4, out_shape=[shp] * 4,
        compiler_params=_cp("parallel", "parallel"),
    )(parts, w, m, v)


SMALL_ROWS = 40
LOSS_ROW = 37


def _pack_small(norm_g, ple_g, qk_g, b_f, last_row):
    rows = lambda a: a.astype(F32).reshape(-1, 128)
    flat = jnp.concatenate([rows(norm_g), rows(ple_g), rows(qk_g), _row(b_f.reshape(-1)), last_row], axis=0)
    return jnp.pad(flat, ((0, SMALL_ROWS - flat.shape[0]), (0, 0)))


def _unpack_small(flat):
    return (flat[0:16].reshape(2, D_MODEL), flat[16:32].reshape(2, D_MODEL), flat[32:36].reshape(2, 4, HEAD_DIM),
            flat[36, :2 * N_HEADS].reshape(2, N_HEADS))


def kernel(x, p, positions, norm_g, w_in, b_f, qk_norm_g, w_out, w_ple, ple_norm_g, w_ple_gate, loss_target, m_norm_g, m_w_in, m_b_f, m_qk_norm_g, m_w_out, m_w_ple, m_ple_norm_g, m_w_ple_gate, v_norm_g, v_w_in, v_b_f, v_qk_norm_g, v_w_out, v_w_ple, v_ple_norm_g, v_w_ple_gate):
    bf16 = lambda a: a.astype(BF16)
    rows_in = W_IN_ROWS // 2
    flat_in = lambda a: bf16(a).reshape(rows_in, 128)
    full_in = lambda g: g.reshape(N_DEV, D_MODEL, W_IN_SHARD).transpose(1, 0, 2).reshape(D_MODEL, N_IN)
    small = [(norm_g[l], b_f[l], qk_norm_g[l], ple_norm_g[l]) for l in range(2)]
    rope = _rope_tables(positions[0])
    tabs = _tables()

    g_in0 = _gather_two_level(flat_in(w_in[0]), "gather_first")
    rest = _gather_pushes([flat_in(w_in[1])] + [bf16(a[l]) for l in range(2) for a in (w_out, w_ple, w_ple_gate)],
                          ["slot"] + ["rows", "cols", "rows"] * 2)
    late = lambda got: dict(w_out=got[1], w_ple=got[2], w_pg=got[3])
    h, sv0, got = _layer_fwd(x[0], (p, (0, 0)), rope, tabs, _in_operands(full_in(g_in0)), *small[0], rest, late)
    w1 = dict(_in_operands(full_in(got[0])), w_out=got[4], w_ple=got[5], w_pg=got[6])
    dh, sv1, _, loss = _layer_fwd(h, (p, (1, 0)), rope, tabs, w1, *small[1], target=loss_target[0])
    dh, gr1, _ = _layer_bwd(dh, sv1, rope, tabs)

    by_dest = lambda d: d.reshape(D_MODEL, N_DEV, W_IN_SHARD).transpose(1, 0, 2).reshape(N_DEV, rows_in, 128)
    big = ("w_out", "w_ple", "w_ple_gate")
    riding = lambda early: _scatter_pushes([by_dest(gr1["w_in"])] + [gr1[n] for n in big] + [early[n] for n in big],
                                           ["slot"] + ["rows", "cols", "rows"] * 2)
    riding_last = lambda dw_in: _scatter_pushes([by_dest(dw_in)], ["slot"])
    dx, gr0, (r_in1, *r_big, r_in0) = _layer_bwd(dh, sv0, rope, tabs, riding, riding_last)
    grads = (gr0, gr1)
    stack = lambda name: jnp.stack([gl[name] for gl in grads], axis=0)
    small_part = _pack_small(stack("norm_g"), stack("ple_norm_g"), stack("qk_norm_g"), stack("b_f"),
                             _row(loss[0, 0].reshape(1)))
    (r_small,) = _exchange("exchange_small", _scatter_pushes([small_part], ["whole"]))
    r_in = jnp.concatenate([r_in0, r_in1], axis=1)
    r_out, r_ple, r_pg = (jnp.stack([r_big[3 + k], r_big[k]], axis=1) for k in range(3))

    zero_row = jnp.zeros((1, 128), F32)
    small_of = lambda ng, pg, qk, bf: _pack_small(ng, pg, qk, bf, zero_row)[None]
    flat = lambda a: a.reshape(1, W_IN_ROWS, 128)
    outs = dict(
        w_in=[o.reshape(w_in.shape) for o in
              _adamw("adamw_in", r_in[:, None], flat(w_in), flat(m_w_in), flat(v_w_in), W_IN_TILE)],
        w_out=_adamw("adamw_out", r_out, w_out, m_w_out, v_w_out, 128),
        w_ple=_adamw("adamw_ple", r_ple, w_ple, m_w_ple, v_w_ple, 256),
        w_pg=_adamw("adamw_gate", r_pg, w_ple_gate, m_w_ple_gate, v_w_ple_gate, 128),
        small=_adamw("adamw_small", r_small[:, None], small_of(norm_g, ple_norm_g, qk_norm_g, b_f),
                     small_of(m_norm_g, m_ple_norm_g, m_qk_norm_g, m_b_f),
                     small_of(v_norm_g, v_ple_norm_g, v_qk_norm_g, v_b_f), SMALL_ROWS))
    leaves = []
    for kind in range(4):
        ng, pg, qk, bf = _unpack_small(outs["small"][kind][0])
        leaves += [ng, outs["w_in"][kind], bf, qk, outs["w_out"][kind], outs["w_ple"][kind], pg, outs["w_pg"][kind]]
    return (outs["small"][0][0, LOSS_ROW, 0], dx[None], *leaves)
```

```python
import functools

import jax
import jax.numpy as jnp
from jax import lax
from jax.experimental import pallas as pl
from jax.experimental.pallas import tpu as pltpu

F32 = jnp.float32
BF16 = jnp.bfloat16

D_MODEL = 1024
HEAD_DIM = 64
N_HEADS = 8
HEAD_PAD = 128
D_BRANCH = N_HEADS * HEAD_DIM
N_MAIN = 8 * D_BRANCH
N_IN = N_MAIN + N_HEADS
PLE_DIM = 256
ROPE_THETA = 500000.0
ROPE_HALF = 8
EPS = 1e-6
NEG = -1e30
SCALE = HEAD_DIM ** -0.5
LOG2E = 1.4426950408889634
LN2 = 0.6931471805599453
DILATED_PATTERNS = ((128, 1), (512, 4), (2048, 16))
N_DEV = 8
W_IN_SHARD = N_IN // N_DEV
W_IN_ROWS = 2 * D_MODEL * W_IN_SHARD // 128
W_IN_TILE = W_IN_ROWS // 19

ADAM_LR = 0.001
ADAM_B1 = 0.9
ADAM_B2 = 0.999
ADAM_EPS = 1e-08
ADAM_WD = 0.01
ADAM_STEP = 10

ATT_T = 512
ATT_FWD_HEADS = 8
ATT_BWD_HEADS = 4
ATT_CHUNK = 32
ATT_CHUNK_CAUSAL_FWD = 16
TOK_T = 256
VMEM_LIMIT = 60 * 1024 * 1024


def _slab_spec(lead, block, index):
    return pl.BlockSpec((None,) * len(lead) + block, lambda *g: (*lead, *index(*g)))


def _cp(*sem):
    return pltpu.CompilerParams(dimension_semantics=sem, vmem_limit_bytes=VMEM_LIMIT)


def _sigmoid(x):
    return 1.0 / (1.0 + jnp.exp(-x))


def _split3(x):
    hi = x.astype(BF16)
    r1 = x - hi.astype(F32)
    mid = r1.astype(BF16)
    lo = (r1 - mid.astype(F32)).astype(BF16)
    return hi, mid, lo


def _dot(a, b):
    return jnp.dot(a, b, preferred_element_type=F32)


def _dot_nt(a, b):
    return lax.dot_general(a, b, (((1,), (1,)), ((), ())), preferred_element_type=F32)


def _dot_tn(a, b):
    return lax.dot_general(a, b, (((0,), (0,)), ((), ())), preferred_element_type=F32)


def _inproj_fwd(h, g, wm, wf):
    S = h.shape[0]

    def body(h_ref, g_ref, wm_ref, wf_ref, zm_ref, zf_ref, u_ref):
        x = h_ref[...]
        r = lax.rsqrt(jnp.mean(x * x, axis=-1, keepdims=True) + EPS)
        u = (x * r * g_ref[...]).astype(BF16)
        u_ref[...] = u
        zm_ref[...] = _dot(u, wm_ref[...]).astype(BF16)
        zf_ref[...] = _dot(u, wf_ref[...])

    return pl.pallas_call(
        body, name="inproj_fwd", grid=(S // TOK_T,),
        in_specs=[pl.BlockSpec((TOK_T, D_MODEL), lambda i: (i, 0)),
                  pl.BlockSpec((1, D_MODEL), lambda i: (0, 0)),
                  pl.BlockSpec((D_MODEL, N_MAIN), lambda i: (0, 0)),
                  pl.BlockSpec((D_MODEL, 128), lambda i: (0, 0))],
        out_specs=[pl.BlockSpec((TOK_T, N_MAIN), lambda i: (i, 0)),
                   pl.BlockSpec((TOK_T, 128), lambda i: (i, 0)),
                   pl.BlockSpec((TOK_T, D_MODEL), lambda i: (i, 0))],
        out_shape=[jax.ShapeDtypeStruct((S, N_MAIN), BF16), jax.ShapeDtypeStruct((S, 128), F32),
                   jax.ShapeDtypeStruct((S, D_MODEL), BF16)],
        compiler_params=_cp("parallel"),
    )(h, g, wm, wf)


def _log_sigmoid(x):
    return jnp.minimum(x, 0.0) - jnp.log(1.0 + jnp.exp(-jnp.abs(x)))


def _same_head():
    r = lax.broadcasted_iota(jnp.int32, (HEAD_PAD, HEAD_PAD), 0) // HEAD_DIM
    c = lax.broadcasted_iota(jnp.int32, (HEAD_PAD, HEAD_PAD), 1) // HEAD_DIM
    return (r == c).astype(BF16)


def _pair_mean(x, same_head):
    hi = x.astype(BF16)
    lo = (x - hi.astype(F32)).astype(BF16)
    return (_dot(hi, same_head) + _dot(lo, same_head)) * (1.0 / HEAD_DIM)


def _pair_rsqrt(x, same_head):
    return lax.rsqrt(_pair_mean(x * x, same_head) + EPS)


def _prep_fwd(zm, zf, bf, tril, qkg, rope_c, rope_a, rope_b):
    S = zm.shape[0]
    shp = jax.ShapeDtypeStruct((N_HEADS, S, HEAD_PAD), BF16)

    def body(z_ref, zf_ref, b_ref, tri_ref, g_ref, rc_ref, ra_ref, rb_ref,
             qa_ref, ka_ref, va_ref, qb_ref, kb_ref, vb_ref, carry):
        @pl.when(pl.program_id(0) == 0)
        def _():
            carry[...] = jnp.zeros_like(carry)

        tri = tri_ref[...]
        cs = sum(_dot(tri, part) for part in _split3(_log_sigmoid(zf_ref[...] + b_ref[...]))) + carry[...]
        carry[...] = cs[TOK_T - 1:TOK_T, :]
        lane = lax.broadcasted_iota(jnp.int32, (TOK_T, HEAD_PAD), 1)
        lo_half = lane < HEAD_DIM
        aug = (lane >= HEAD_DIM) & (lane < HEAD_DIM + 3)
        q_pad = jnp.where(aug, -1.0, 0.0)
        rc, ra, rb = rc_ref[...], ra_ref[...], rb_ref[...]
        same_head = _same_head()

        def norm(col, gi):
            x = z_ref[:, col:col + HEAD_PAD].astype(F32)
            return x * _pair_rsqrt(x, same_head) * g_ref[gi:gi + 1, :]

        def rope(y):
            return y * rc + pltpu.roll(y, HEAD_PAD - ROPE_HALF, 1) * ra + pltpu.roll(y, ROPE_HALF, 1) * rb

        def put(ref, pi, y, pad_even, pad_odd):
            ref[2 * pi] = jnp.where(lo_half, y, pad_even).astype(BF16)
            ref[2 * pi + 1] = jnp.where(lo_half, pltpu.roll(y, HEAD_DIM, 1), pad_odd).astype(BF16)

        def k_pad(h):
            ch = cs[:, h:h + 1] * LOG2E
            hi = ch.astype(BF16).astype(F32)
            mid = (ch - hi).astype(BF16).astype(F32)
            lo = ch - hi - mid
            ones = jnp.where(lane == HEAD_DIM + 3, 1.0, 0.0)
            return jnp.where(lane == HEAD_DIM, hi, jnp.where(lane == HEAD_DIM + 1, mid,
                                                              jnp.where(lane == HEAD_DIM + 2, lo, ones)))

        for pi in range(N_HEADS // 2):
            col = HEAD_PAD * pi
            put(qa_ref, pi, norm(col, 0) * (SCALE * LOG2E), q_pad, q_pad)
            put(ka_ref, pi, norm(D_BRANCH + col, 1), k_pad(2 * pi), k_pad(2 * pi + 1))
            put(va_ref, pi, z_ref[:, 2 * D_BRANCH + col:2 * D_BRANCH + col + HEAD_PAD].astype(F32), 0.0, 0.0)
            put(qb_ref, pi, rope(norm(4 * D_BRANCH + col, 2)) * (SCALE * LOG2E), 0.0, 0.0)
            put(kb_ref, pi, rope(norm(5 * D_BRANCH + col, 3)), 0.0, 0.0)
            put(vb_ref, pi, z_ref[:, 6 * D_BRANCH + col:6 * D_BRANCH + col + HEAD_PAD].astype(F32), 0.0, 0.0)

    tok = lambda w: pl.BlockSpec((TOK_T, w), lambda i: (i, 0))
    head = pl.BlockSpec((N_HEADS, TOK_T, HEAD_PAD), lambda i: (0, i, 0))
    return pl.pallas_call(
        body, name="prep_fwd", grid=(S // TOK_T,),
        in_specs=[tok(N_MAIN), tok(128), pl.BlockSpec((1, 128), lambda i: (0, 0)),
                  pl.BlockSpec((TOK_T, TOK_T), lambda i: (0, 0)), pl.BlockSpec((4, 128), lambda i: (0, 0)),
                  tok(128), tok(128), tok(128)],
        out_specs=[head] * 6, out_shape=[shp] * 6,
        scratch_shapes=[pltpu.VMEM((1, 128), F32)],
        compiler_params=_cp("arbitrary"),
    )(zm, zf, bf, tril, qkg, rope_c, rope_a, rope_b)


def _pair(ref, pi, lo_half):
    return jnp.where(lo_half, ref[2 * pi].astype(F32), pltpu.roll(ref[2 * pi + 1].astype(F32), HEAD_DIM, 1))


def _mid_fwd(oa, ob, zm, h0, p, w_out, w_pg, w_ple, g2, target=None):
    S = h0.shape[0]
    p, p_lead = p

    def body(oa_ref, ob_ref, ga_ref, gb_ref, h0_ref, p_ref, wo_ref, wg_ref, wp_ref, g2_ref, *rest):
        t_ref, rest = (rest[0], rest[1:]) if target is not None else (None, rest)
        y_ref, h1_ref, h2_ref, u2_ref, e_ref, gate_ref, *loss_ref = rest
        parts = []
        for o_ref, g_ref in ((oa_ref, ga_ref), (ob_ref, gb_ref)):
            for pi in range(N_HEADS // 2):
                g = g_ref[:, HEAD_PAD * pi:HEAD_PAD * (pi + 1)].astype(F32)
                parts.append((o_ref[pi] * (g * _sigmoid(g))).astype(BF16))
        y = jnp.concatenate(parts, axis=1)
        y_ref[...] = y
        h1 = h0_ref[...] + _dot(y, wo_ref[...])
        h1_ref[...] = h1
        r = lax.rsqrt(jnp.mean(h1 * h1, axis=-1, keepdims=True) + EPS)
        u2 = (h1 * r * g2_ref[...]).astype(BF16)
        u2_ref[...] = u2
        gate = _sigmoid(_dot(u2, wg_ref[...]))
        e = _dot(p_ref[...].astype(BF16), wp_ref[...])
        e_ref[...] = e.astype(BF16)
        gate_ref[...] = gate.astype(BF16)
        h2 = h1 + e * gate
        if target is None:
            h2_ref[...] = h2
        else:
            @pl.when(pl.program_id(0) == 0)
            def _():
                loss_ref[0][...] = jnp.zeros_like(loss_ref[0])

            err = h2 - t_ref[...]
            h2_ref[...] = err * (1.0 / D_MODEL)
            part = jnp.sum(jnp.sum(err * err, axis=1, keepdims=True), axis=0, keepdims=True)
            loss_ref[0][...] += part * (0.5 / D_MODEL)

    tok = lambda w: pl.BlockSpec((TOK_T, w), lambda i: (i, 0))
    head = pl.BlockSpec((N_HEADS // 2, TOK_T, HEAD_PAD), lambda i: (0, i, 0))
    full = lambda a, b: pl.BlockSpec((a, b), lambda i: (0, 0))
    act = lambda dt: jax.ShapeDtypeStruct((S, D_MODEL), dt)
    fused = target is not None
    return pl.pallas_call(
        body, name="mid_fwd_loss" if fused else "mid_fwd", grid=(S // TOK_T,),
        in_specs=[head, head,
                  pl.BlockSpec((TOK_T, D_BRANCH), lambda i: (i, 3)), pl.BlockSpec((TOK_T, D_BRANCH), lambda i: (i, 7)),
                  tok(D_MODEL), _slab_spec(p_lead, (TOK_T, PLE_DIM), lambda i: (i, 0)), full(D_MODEL, D_MODEL),
                  full(D_MODEL, D_MODEL), full(PLE_DIM, D_MODEL), full(1, D_MODEL)] + [tok(D_MODEL)] * fused,
        out_specs=[tok(D_MODEL)] * 6 + [full(8, 128)] * fused,
        out_shape=[act(BF16), act(F32), act(F32), act(BF16), act(BF16), act(BF16)]
        + [jax.ShapeDtypeStruct((8, 128), F32)] * fused,
        compiler_params=_cp("arbitrary" if fused else "parallel"),
    )(oa, ob, zm, zm, h0, p, w_out, w_pg, w_ple, g2, *([target] * fused))


def _bias_tables(full_range):
    T = ATT_T
    nb = 1 if full_range else DILATED_PATTERNS[-1][0] // T + 1
    r = lax.broadcasted_iota(jnp.int32, (nb, T, T), 2)
    c = lax.broadcasted_iota(jnp.int32, (nb, T, T), 1)
    b = lax.broadcasted_iota(jnp.int32, (nb, T, T), 0)
    delta = T * b + r - c
    if full_range:
        bias = jnp.where(delta >= 0, 0.0, NEG).astype(F32)
    else:
        mult = jnp.zeros((nb, T, T), F32)
        for window, dil in DILATED_PATTERNS:
            ok = (delta >= 0) & (delta <= window) & (delta % dil == 0)
            mult = mult + ok.astype(F32)
        bias = jnp.where(mult > 0, jnp.log2(jnp.maximum(mult, 1.0)), NEG).astype(F32)
    return bias


def _call_with_rider(body, name, grid, rider, in_specs, out_specs, out_shape, scratch_shapes, operands,
                     semantics=("parallel", "arbitrary")):
    if rider is None:
        return pl.pallas_call(body, name=name, grid=grid, in_specs=in_specs, out_specs=out_specs,
                              out_shape=out_shape, scratch_shapes=scratch_shapes,
                              compiler_params=_cp(*semantics))(*operands)
    n, n_in, n_out = rider.n, len(in_specs), len(out_specs)

    def wrapped(*refs):
        ins, r_ins = refs[:n_in], refs[n_in:n_in + n]
        outs, r_outs = refs[n_in + n:n_in + n + n_out], refs[n_in + n + n_out:n_in + 2 * n + n_out]
        scratch, sems = refs[n_in + 2 * n + n_out:-3], refs[-3:]
        step = [pl.program_id(a) for a in range(len(grid))]

        @pl.when(functools.reduce(jnp.logical_and, [s == 0 for s in step]))
        def _():
            rider.start(r_ins, r_outs, sems)

        body(*ins, *outs, *scratch)

        @pl.when(functools.reduce(jnp.logical_and, [s == g - 1 for s, g in zip(step, grid)]))
        def _():
            rider.wait(r_ins, r_outs, sems)

    return pl.pallas_call(
        wrapped, name=name, grid=grid, in_specs=list(in_specs) + rider.in_specs,
        out_specs=list(out_specs) + rider.out_specs, out_shape=list(out_shape) + rider.out_shapes,
        scratch_shapes=list(scratch_shapes) + rider.scratch_shapes,
        compiler_params=_cp(*["arbitrary"] * len(grid)))(*operands, *rider.arrays)


def _attn_fwd(q, k, v, table_t, full_range, name, rider=None):
    H, S, _ = q.shape
    T = ATT_T
    nb = table_t.shape[0]
    HB = ATT_FWD_HEADS
    KC = ATT_CHUNK_CAUSAL_FWD if full_range else ATT_CHUNK
    chunks = [slice(c, c + KC) for c in range(0, T, KC)]
    fold = lambda x, op: functools.reduce(op, [x[r:r + 8] for r in range(0, KC, 8)])

    def body(q_ref, k_ref, v_ref, tab_ref, o_ref, lse_ref, *scratch):
        st_refs, pt_refs, acc_refs = scratch[:HB], scratch[HB:2 * HB], scratch[2 * HB:]
        i = pl.program_id(1)
        rows = lambda j: pl.ds(pl.multiple_of(j * T, T), T)

        def scores(hh, j):
            st_refs[hh][...] = _dot_nt(k_ref[hh, rows(j), :], q_ref[hh])

        def block(j, b, nxt, stats):
            out = []
            for hh, (m, l) in enumerate(stats):
                st_ref, pt_ref, acc_ref = st_refs[hh], pt_refs[hh], acc_refs[hh]
                mx = None
                for ch in chunks:
                    x = st_ref[ch, :]
                    if b is not None:
                        x = x + tab_ref[b, ch, :]
                        st_ref[ch, :] = x
                    x = fold(x, jnp.maximum)
                    mx = x if mx is None else jnp.maximum(mx, x)
                m_new = jnp.maximum(m, jnp.max(mx, axis=0, keepdims=True))
                alpha = jnp.exp2(m - m_new)
                ls = None
                for ch in chunks:
                    pc = jnp.exp2(st_ref[ch, :] - m_new)
                    pt_ref[ch, :] = pc.astype(BF16)
                    pc = fold(pc, jnp.add)
                    ls = pc if ls is None else ls + pc
                if nxt is not None:
                    scores(hh, nxt)
                acc_ref[...] = alpha * acc_ref[...] + _dot_tn(v_ref[hh, rows(j), :], pt_ref[...])
                out.append((m_new, alpha * l + jnp.sum(ls, axis=0, keepdims=True)))
            return tuple(out)

        lo = 0 if full_range else jnp.maximum(i - (nb - 1), 0)
        for hh in range(HB):
            acc_refs[hh][...] = jnp.zeros_like(acc_refs[hh])
            scores(hh, lo)
        stats = lax.fori_loop(lo, i, lambda j, st: block(j, None if full_range else i - j, j + 1, st),
                              ((jnp.full((1, T), NEG, F32), jnp.zeros((1, T), F32)),) * HB)
        stats = block(i, 0, None, stats)
        o_t = [acc_refs[hh][...] * (1.0 / l) for hh, (m, l) in enumerate(stats)]
        for hh, (m, l) in enumerate(stats):
            lse_ref[hh, 0] = m + jnp.log2(l)
        for hp in range(HB // 2):
            o_ref[hp] = jnp.concatenate([o_t[2 * hp][:HEAD_DIM], o_t[2 * hp + 1][:HEAD_DIM]], axis=0).T

    return _call_with_rider(
        body, name, (H // HB, S // T), rider,
        in_specs=[pl.BlockSpec((HB, T, HEAD_PAD), lambda h, i: (h, i, 0)),
                  pl.BlockSpec((HB, S, HEAD_PAD), lambda h, i: (h, 0, 0), pipeline_mode=pl.Buffered(1)),
                  pl.BlockSpec((HB, S, HEAD_PAD), lambda h, i: (h, 0, 0), pipeline_mode=pl.Buffered(1)),
                  pl.BlockSpec((nb, T, T), lambda h, i: (0, 0, 0), pipeline_mode=pl.Buffered(1))],
        out_specs=[pl.BlockSpec((HB // 2, T, HEAD_PAD), lambda h, i: (h, i, 0)),
                   pl.BlockSpec((HB, 1, 1, T), lambda h, i: (h, i, 0, 0))],
        out_shape=[jax.ShapeDtypeStruct((H // 2, S, HEAD_PAD), F32), jax.ShapeDtypeStruct((H, S // T, 1, T), F32)],
        scratch_shapes=([pltpu.VMEM((T, T), F32)] * HB + [pltpu.VMEM((T, T), BF16)] * HB
                        + [pltpu.VMEM((HEAD_PAD, T), F32)] * HB),
        operands=(q, k, v, table_t))


def _attn_bwd(q, k, v, do, lse, dd, table_t, full_range, name, rider=None, narrow=(False, False, False)):
    H, S, _ = q.shape
    T = ATT_T
    nq = S // T
    nb = table_t.shape[0]
    HB = ATT_BWD_HEADS
    KC = ATT_CHUNK
    chunks = [slice(c, c + KC) for c in range(0, T, KC)]

    def body(q_ref, do_ref, lse_ref, dd_ref, k_ref, v_ref, tab_ref, dq_hbm, dk_ref, dv_ref, *scratch):
        st_refs, dpt_refs, pt_refs, dst_refs = (scratch[n * HB:(n + 1) * HB] for n in range(4))
        dq_ref, dk_acc, dv_acc, *dq_cast, dq_sem = scratch[4 * HB:]
        h = pl.program_id(0)
        j = pl.program_id(1)

        @pl.when(j == 0)
        def _():
            dq_ref[...] = jnp.zeros_like(dq_ref)

        dk_acc[...] = jnp.zeros_like(dk_acc)
        dv_acc[...] = jnp.zeros_like(dv_acc)

        def step(i, b):
            rows = pl.ds(pl.multiple_of(i * T, T), T)
            for hh in range(HB):
                st_refs[hh][...] = _dot_nt(k_ref[hh], q_ref[hh, rows, :])
                dpt_refs[hh][...] = _dot_nt(v_ref[hh], do_ref[hh, rows, :])
            for hh in range(HB):
                lse_i = lse_ref[hh, i]
                dd_i = dd_ref[hh, i]
                for ch in chunks:
                    x = st_refs[hh][ch, :]
                    if b is not None:
                        x = x + tab_ref[b, ch, :]
                    pc = jnp.exp2(x - lse_i)
                    pt_refs[hh][ch, :] = pc.astype(BF16)
                    dst_refs[hh][ch, :] = (pc * (dpt_refs[hh][ch, :] - dd_i)).astype(BF16)
                dv_acc[hh] += _dot(pt_refs[hh][...], do_ref[hh, rows, :])
                dk_acc[hh] += _dot(dst_refs[hh][...], q_ref[hh, rows, :])
                dq_ref[hh, rows, :] += _dot_tn(dst_refs[hh][...], k_ref[hh])

        step(j, 0)
        if full_range:
            pl.loop(j + 1, nq)(lambda i: step(i, None))
        else:
            pl.loop(j + 1, jnp.minimum(j + nb, nq))(lambda i: step(i, i - j))
        dk_ref[...] = dk_acc[...].astype(dk_ref.dtype)
        dv_ref[...] = dv_acc[...].astype(dv_ref.dtype)

        @pl.when(j == nq - 1)
        def _():
            src = dq_ref
            if narrow[0]:
                src, = dq_cast
                src[...] = dq_ref[...].astype(BF16)
            out = pltpu.make_async_copy(src, dq_hbm.at[pl.ds(h * HB, HB)], dq_sem)
            out.start()
            out.wait()

    once = dict(pipeline_mode=pl.Buffered(1))
    per_head = pl.BlockSpec((HB, S, HEAD_PAD), lambda h, j: (h, 0, 0), **once)
    rows = pl.BlockSpec((HB, nq, 1, T), lambda h, j: (h, 0, 0, 0))
    blk = pl.BlockSpec((HB, T, HEAD_PAD), lambda h, j: (h, j, 0))
    shp = [jax.ShapeDtypeStruct((H, S, HEAD_PAD), BF16 if nar else F32) for nar in narrow]
    acc = pltpu.VMEM((HB, T, HEAD_PAD), F32)
    return _call_with_rider(
        body, name, (H // HB, nq), rider,
        in_specs=[per_head, per_head, rows, rows, blk, blk,
                  pl.BlockSpec((nb, T, T), lambda h, j: (0, 0, 0), **once)],
        out_specs=[pl.BlockSpec(memory_space=pltpu.HBM), blk, blk], out_shape=shp,
        scratch_shapes=([pltpu.VMEM((T, T), F32)] * (2 * HB) + [pltpu.VMEM((T, T), BF16)] * (2 * HB)
                        + [pltpu.VMEM((HB, S, HEAD_PAD), F32), acc, acc]
                        + [pltpu.VMEM((HB, S, HEAD_PAD), BF16)] * narrow[0] + [pltpu.SemaphoreType.DMA]),
        operands=(q, do, lse, dd, k, v, table_t))


def _mid_bwd(dh2, h1, e, gate, g2, w_pg, w_out, oa, ob, zm):
    S = dh2.shape[0]

    def body(dh2_ref, h1_ref, e_ref, gate_ref, g2_ref, wg_ref, wo_ref, oa_ref, ob_ref, ga_ref, gb_ref,
             dh1_ref, dh1b_ref, de_ref, dpre_ref, doa_ref, dob_ref, dga_ref, dgb_ref, dd_ref, dg2_ref):
        @pl.when(pl.program_id(0) == 0)
        def _():
            dg2_ref[...] = jnp.zeros_like(dg2_ref)

        lane = lax.broadcasted_iota(jnp.int32, (TOK_T, HEAD_PAD), 1)
        lo_half = lane < HEAD_DIM
        dh2 = dh2_ref[...]
        gate = gate_ref[...]
        de_ref[...] = (dh2 * gate).astype(BF16)
        dpre = (dh2 * e_ref[...] * gate * (1.0 - gate)).astype(BF16)
        dpre_ref[...] = dpre
        du2 = _dot_nt(dpre, wg_ref[...])
        h1 = h1_ref[...]
        r = lax.rsqrt(jnp.mean(h1 * h1, axis=-1, keepdims=True) + EPS)
        xh = h1 * r
        a = du2 * g2_ref[...]
        dh1 = dh2 + r * (a - xh * jnp.mean(a * xh, axis=-1, keepdims=True))
        dg2_ref[...] += jnp.sum(du2 * xh, axis=0, keepdims=True)
        dh1_ref[...] = dh1
        dh1b = dh1.astype(BF16)
        dh1b_ref[...] = dh1b
        dy = _dot_nt(dh1b, wo_ref[...])
        dd = jnp.zeros((TOK_T, HEAD_PAD), F32)
        for bi, (o_ref, g_ref, do_ref, dg_ref) in enumerate(
                ((oa_ref, ga_ref, doa_ref, dga_ref), (ob_ref, gb_ref, dob_ref, dgb_ref))):
            for pi in range(N_HEADS // 2):
                col = bi * D_BRANCH + HEAD_PAD * pi
                dyp = dy[:, col:col + HEAD_PAD]
                g = g_ref[:, HEAD_PAD * pi:HEAD_PAD * (pi + 1)].astype(F32)
                sg = _sigmoid(g)
                o_pair = o_ref[pi]
                dg_ref[:, HEAD_PAD * pi:HEAD_PAD * (pi + 1)] = (
                    dyp * o_pair * (sg * (1.0 + g * (1.0 - sg)))).astype(BF16)
                dop = dyp * (g * sg)
                prod = dop * o_pair
                for hh, d_head, mine in ((2 * pi, dop, lo_half),
                                         (2 * pi + 1, pltpu.roll(dop, HEAD_DIM, 1), ~lo_half)):
                    do_ref[hh] = jnp.where(lo_half, d_head, 0.0).astype(BF16)
                    dsum = jnp.sum(jnp.where(mine, prod, 0.0), axis=1, keepdims=True)
                    dd = dd + jnp.where(lane == bi * N_HEADS + hh, dsum, 0.0)
        dd_ref[...] = dd.T[:2 * N_HEADS, :]

    tok = lambda w: pl.BlockSpec((TOK_T, w), lambda i: (i, 0))
    head = pl.BlockSpec((N_HEADS, TOK_T, HEAD_PAD), lambda i: (0, i, 0))
    pairs = pl.BlockSpec((N_HEADS // 2, TOK_T, HEAD_PAD), lambda i: (0, i, 0))
    full = lambda a, b: pl.BlockSpec((a, b), lambda i: (0, 0))
    act = lambda w, dt: jax.ShapeDtypeStruct((S, w), dt)
    hshape = lambda w, dt: jax.ShapeDtypeStruct((N_HEADS, S, w), dt)
    return pl.pallas_call(
        body, name="mid_bwd", grid=(S // TOK_T,),
        in_specs=[tok(D_MODEL)] * 4 + [full(1, D_MODEL), full(D_MODEL, D_MODEL), full(D_MODEL, D_MODEL), pairs, pairs,
                                      pl.BlockSpec((TOK_T, D_BRANCH), lambda i: (i, 3)),
                                      pl.BlockSpec((TOK_T, D_BRANCH), lambda i: (i, 7))],
        out_specs=[tok(D_MODEL)] * 4 + [head, head, tok(D_BRANCH), tok(D_BRANCH),
                                       pl.BlockSpec((2 * N_HEADS, TOK_T), lambda i: (0, i)), full(1, D_MODEL)],
        out_shape=[act(D_MODEL, F32), act(D_MODEL, BF16), act(D_MODEL, BF16), act(D_MODEL, BF16),
                   hshape(HEAD_PAD, BF16), hshape(HEAD_PAD, BF16), act(D_BRANCH, BF16), act(D_BRANCH, BF16),
                   jax.ShapeDtypeStruct((2 * N_HEADS, S), F32), jax.ShapeDtypeStruct((1, D_MODEL), F32)],
        compiler_params=_cp("arbitrary"),
    )(dh2, h1, e, gate, g2, w_pg, w_out, oa, ob, zm, zm)


def _prep_bwd(dqa, dka, dva, dqb, dkb, dvb, zm, qkg, rope_c, rope_a, rope_b, dga, dgb, zf, bf, triu):
    S = zm.shape[0]
    n = S // TOK_T

    def body(dqa_ref, dka_ref, dva_ref, dqb_ref, dkb_ref, dvb_ref, z_ref, g_ref, rc_ref, ra_ref, rb_ref,
             dga_ref, dgb_ref, zf_ref, b_ref, tri_ref, dz_ref, dzf_ref, dqkg_ref, db_ref, carry):
        @pl.when(pl.program_id(0) == 0)
        def _():
            dqkg_ref[...] = jnp.zeros_like(dqkg_ref)
            db_ref[...] = jnp.zeros_like(db_ref)
            carry[...] = jnp.zeros_like(carry)

        lane = lax.broadcasted_iota(jnp.int32, (TOK_T, HEAD_PAD), 1)
        lo_half = lane < HEAD_DIM
        rc, ra, rb = rc_ref[...], ra_ref[...], rb_ref[...]
        same_head = _same_head()

        def unrope(dy):
            return dy * rc + pltpu.roll(dy * ra, ROPE_HALF, 1) + pltpu.roll(dy * rb, HEAD_PAD - ROPE_HALF, 1)

        def norm_bwd(col, gi, dy):
            x = z_ref[:, col:col + HEAD_PAD].astype(F32)
            r = _pair_rsqrt(x, same_head)
            xh = x * r
            dqkg_ref[gi:gi + 1, :] += jnp.sum(dy * xh, axis=0, keepdims=True)
            a = dy * g_ref[gi:gi + 1, :]
            dz_ref[:, col:col + HEAD_PAD] = (r * (a - xh * _pair_mean(a * xh, same_head))).astype(BF16)

        dc = jnp.zeros((TOK_T, HEAD_PAD), F32)
        for pi in range(N_HEADS // 2):
            col = HEAD_PAD * pi
            norm_bwd(col, 0, _pair(dqa_ref, pi, lo_half) * SCALE)
            norm_bwd(D_BRANCH + col, 1, _pair(dka_ref, pi, lo_half) * LN2)
            dz_ref[:, 2 * D_BRANCH + col:2 * D_BRANCH + col + HEAD_PAD] = _pair(dva_ref, pi, lo_half).astype(BF16)
            norm_bwd(4 * D_BRANCH + col, 2, unrope(_pair(dqb_ref, pi, lo_half) * SCALE))
            norm_bwd(5 * D_BRANCH + col, 3, unrope(_pair(dkb_ref, pi, lo_half) * LN2))
            dz_ref[:, 6 * D_BRANCH + col:6 * D_BRANCH + col + HEAD_PAD] = _pair(dvb_ref, pi, lo_half).astype(BF16)
            for hh in (2 * pi, 2 * pi + 1):
                dch = dka_ref[hh][:, HEAD_DIM:HEAD_DIM + 1] + dqa_ref[hh][:, HEAD_DIM + 3:HEAD_DIM + 4]
                dc = dc + jnp.where(lane == hh, dch, 0.0)
        dz_ref[:, 3 * D_BRANCH:4 * D_BRANCH] = dga_ref[...]
        dz_ref[:, 7 * D_BRANCH:8 * D_BRANCH] = dgb_ref[...]
        tri = tri_ref[...]
        dlf = sum(_dot(tri, part) for part in _split3(dc)) + carry[...]
        carry[...] = dlf[0:1, :]
        dfa = dlf * (1.0 - _sigmoid(zf_ref[...] + b_ref[...]))
        dzf_ref[...] = dfa.astype(BF16)
        db_ref[...] += jnp.sum(dfa, axis=0, keepdims=True)

    tok = lambda w: pl.BlockSpec((TOK_T, w), lambda i: (n - 1 - i, 0))
    head = pl.BlockSpec((N_HEADS, TOK_T, HEAD_PAD), lambda i: (0, n - 1 - i, 0))
    fixed = lambda a, b: pl.BlockSpec((a, b), lambda i: (0, 0))
    return pl.pallas_call(
        body, name="prep_bwd", grid=(n,),
        in_specs=[head] * 6 + [tok(N_MAIN), fixed(4, 128), tok(128), tok(128), tok(128), tok(D_BRANCH), tok(D_BRANCH),
                               tok(128), fixed(1, 128), fixed(TOK_T, TOK_T)],
        out_specs=[tok(N_MAIN), tok(128), fixed(4, 128), fixed(1, 128)],
        out_shape=[jax.ShapeDtypeStruct((S, N_MAIN), BF16), jax.ShapeDtypeStruct((S, 128), BF16),
                   jax.ShapeDtypeStruct((4, 128), F32), jax.ShapeDtypeStruct((1, 128), F32)],
        scratch_shapes=[pltpu.VMEM((1, 128), F32)],
        compiler_params=_cp("arbitrary"),
    )(dqa, dka, dva, dqb, dkb, dvb, zm, qkg, rope_c, rope_a, rope_b, dga, dgb, zf, bf, triu)


def _inproj_bwd(dzm, dzf, wm, wf, h0, dh1, g, rider=None):
    S = h0.shape[0]

    def body(dzm_ref, dzf_ref, wm_ref, wf_ref, h_ref, dh1_ref, g_ref, dh0_ref, dg_ref):
        @pl.when(pl.program_id(0) == 0)
        def _():
            dg_ref[...] = jnp.zeros_like(dg_ref)

        du = _dot_nt(dzm_ref[...], wm_ref[...]) + _dot_nt(dzf_ref[...], wf_ref[...])
        x = h_ref[...]
        r = lax.rsqrt(jnp.mean(x * x, axis=-1, keepdims=True) + EPS)
        xh = x * r
        a = du * g_ref[...]
        dh0_ref[...] = dh1_ref[...] + r * (a - xh * jnp.mean(a * xh, axis=-1, keepdims=True))
        dg_ref[...] += jnp.sum(du * xh, axis=0, keepdims=True)

    tok = lambda w: pl.BlockSpec((TOK_T, w), lambda i: (i, 0))
    full = lambda a, b: pl.BlockSpec((a, b), lambda i: (0, 0))
    return _call_with_rider(
        body, "inproj_bwd", (S // TOK_T,), rider,
        in_specs=[tok(N_MAIN), tok(128), full(D_MODEL, N_MAIN), full(D_MODEL, 128), tok(D_MODEL), tok(D_MODEL),
                  full(1, D_MODEL)],
        out_specs=[tok(D_MODEL), full(1, D_MODEL)],
        out_shape=[jax.ShapeDtypeStruct((S, D_MODEL), F32), jax.ShapeDtypeStruct((1, D_MODEL), F32)],
        scratch_shapes=[], operands=(dzm, dzf, wm, wf, h0, dh1, g), semantics=("arbitrary",))


def _wgrad(a, b, name, a_lead=()):
    S, M = a.shape[len(a_lead):]
    N = b.shape[1]
    tn = min(N, 2048)
    ts = 1024
    last = S // ts - 1

    def body(a_ref, b_ref, o_ref, acc_ref):
        @pl.when(pl.program_id(1) == 0)
        def _():
            acc_ref[...] = jnp.zeros_like(acc_ref)

        acc_ref[...] += _dot_tn(a_ref[...].astype(BF16), b_ref[...])

        @pl.when(pl.program_id(1) == last)
        def _():
            o_ref[...] = acc_ref[...].astype(BF16)

    return pl.pallas_call(
        body, name=name, grid=(N // tn, S // ts),
        in_specs=[_slab_spec(a_lead, (ts, M), lambda n, s: (s, 0)), pl.BlockSpec((ts, tn), lambda n, s: (s, n))],
        out_specs=pl.BlockSpec((M, tn), lambda n, s: (0, n)),
        out_shape=jax.ShapeDtypeStruct((M, N), BF16),
        scratch_shapes=[pltpu.VMEM((M, tn), F32)],
        compiler_params=_cp("parallel", "arbitrary"),
    )(a, b)


def _wgrad_in(u, dzm, dzf):
    S = u.shape[0]
    ts = 512
    last = S // ts - 1

    def body(u_ref, dzm_ref, dzf_ref, om_ref, of_ref, accm_ref, accf_ref):
        @pl.when(pl.program_id(0) == 0)
        def _():
            accm_ref[...] = jnp.zeros_like(accm_ref)
            accf_ref[...] = jnp.zeros_like(accf_ref)

        ub = u_ref[...]
        accm_ref[...] += _dot_tn(ub, dzm_ref[...])
        accf_ref[...] += _dot_tn(ub, dzf_ref[...])

        @pl.when(pl.program_id(0) == last)
        def _():
            om_ref[...] = accm_ref[...].astype(BF16)
            of_ref[...] = accf_ref[...].astype(BF16)

    tok = lambda w: pl.BlockSpec((ts, w), lambda s: (s, 0))
    full = lambda w: pl.BlockSpec((D_MODEL, w), lambda s: (0, 0))
    return pl.pallas_call(
        body, name="wgrad_in", grid=(S // ts,),
        in_specs=[tok(D_MODEL), tok(N_MAIN), tok(128)], out_specs=[full(N_MAIN), full(128)],
        out_shape=[jax.ShapeDtypeStruct((D_MODEL, N_MAIN), BF16), jax.ShapeDtypeStruct((D_MODEL, 128), BF16)],
        scratch_shapes=[pltpu.VMEM((D_MODEL, N_MAIN), F32), pltpu.VMEM((D_MODEL, 128), F32)],
        compiler_params=_cp("arbitrary"),
    )(u, dzm, dzf)


def _rope_tables(positions):
    inv_freq = ROPE_THETA ** (-jnp.arange(ROPE_HALF, dtype=F32) / ROPE_HALF)
    ang = positions.astype(F32)[:, None] * inv_freq
    cos, sin = jnp.cos(ang), jnp.sin(ang)
    S = positions.shape[0]
    one, zero = jnp.ones((S, HEAD_DIM - 2 * ROPE_HALF), F32), jnp.zeros((S, HEAD_DIM - 2 * ROPE_HALF), F32)
    z8 = jnp.zeros((S, ROPE_HALF), F32)
    rc = jnp.concatenate([cos, cos, one], axis=1)
    ra = jnp.concatenate([-sin, z8, zero], axis=1)
    rb = jnp.concatenate([z8, sin, zero], axis=1)
    return tuple(jnp.tile(t, (1, 2)) for t in (rc, ra, rb))


def _in_operands(w_in):
    w_in = w_in.astype(BF16)
    wm = jnp.concatenate([w_in[:, :4 * D_BRANCH], w_in[:, 4 * D_BRANCH + N_HEADS:]], axis=1)
    wf = jnp.pad(w_in[:, 4 * D_BRANCH:4 * D_BRANCH + N_HEADS], ((0, 0), (0, 128 - N_HEADS)))
    return dict(wm=wm, wf=wf)


def _layer_weights(w_in, w_out, w_ple, w_pg):
    return dict(_in_operands(w_in), w_out=w_out.astype(BF16), w_ple=w_ple.astype(BF16), w_pg=w_pg.astype(BF16))


def _row(v, width=128):
    v = v.reshape(1, -1).astype(F32)
    return jnp.pad(v, ((0, 0), (0, width - v.shape[1])))


def _layer_fwd(h0, p, rope, tabs, w, norm_g, b_f, qk_g, ple_g, rider=None, late=None, target=None):
    g1 = norm_g.reshape(1, D_MODEL)
    g2 = ple_g.reshape(1, D_MODEL)
    qkg = jnp.tile(qk_g, (1, 2))
    bf = _row(b_f)
    zm, zf, u = _inproj_fwd(h0, g1, w["wm"], w["wf"])
    qa, ka, va, qb, kb, vb = _prep_fwd(zm, zf, bf, tabs["tril"], qkg, *rope)
    oa, lse_a, *arrivals = _attn_fwd(qa, ka, va, tabs["fox"], True, "fox_fwd", rider)
    if late is not None:
        w = {**w, **late(arrivals)}
    ob, lse_b = _attn_fwd(qb, kb, vb, tabs["dil"], False, "dil_fwd")
    y, h1, h2, u2, e, gate, *loss = _mid_fwd(oa, ob, zm, h0, p, w["w_out"], w["w_pg"], w["w_ple"], g2, target)
    saved = dict(h0=h0, p=p, zm=zm, zf=zf, u=u, qa=qa, ka=ka, va=va, qb=qb, kb=kb, vb=vb, oa=oa, ob=ob,
                 lse_a=lse_a, lse_b=lse_b, y=y, h1=h1, u2=u2, e=e, gate=gate, g1=g1, g2=g2, qkg=qkg, bf=bf, w=w)
    return (h2, saved, arrivals, *loss)


def _layer_bwd(dh2, sv, rope, tabs, make_rider=None, make_last_rider=None):
    S = dh2.shape[0]
    nq = S // ATT_T
    w = sv["w"]
    rows = lambda a: a.reshape(N_HEADS, nq, 1, ATT_T)
    (dh1, dh1b, de, dpre, doa, dob, dga, dgb, dd, dg2) = _mid_bwd(
        dh2, sv["h1"], sv["e"], sv["gate"], sv["g2"], w["w_pg"], w["w_out"], sv["oa"], sv["ob"], sv["zm"])
    dda, ddb = dd[:N_HEADS], dd[N_HEADS:]
    early = dict(w_out=_wgrad(sv["y"], dh1b, "wgrad_out"), w_ple=_wgrad(sv["p"][0], de, "wgrad_ple", sv["p"][1]),
                 w_ple_gate=_wgrad(sv["u2"], dpre, "wgrad_gate"))
    rider = None if make_rider is None else make_rider(early)
    dqa, dka, dva, *arrivals = _attn_bwd(sv["qa"], sv["ka"], sv["va"], doa, sv["lse_a"], rows(dda), tabs["fox"],
                                         True, "fox_bwd", rider, narrow=(False, False, True))
    dqb, dkb, dvb = _attn_bwd(sv["qb"], sv["kb"], sv["vb"], dob, sv["lse_b"], rows(ddb), tabs["dil"], False,
                              "dil_bwd", narrow=(False, True, True))
    dzm, dzf, dqkg, dbf = _prep_bwd(dqa, dka, dva, dqb, dkb, dvb, sv["zm"], sv["qkg"], *rope, dga, dgb,
                                    sv["zf"], sv["bf"], tabs["triu"])
    dwm, dwf = _wgrad_in(sv["u"], dzm, dzf)
    dw_in = jnp.concatenate([dwm[:, :4 * D_BRANCH], dwf[:, :N_HEADS], dwm[:, 4 * D_BRANCH:]], axis=1)
    last_rider = None if make_last_rider is None else make_last_rider(dw_in)
    dh0, dg1, *last_arrivals = _inproj_bwd(dzm, dzf, w["wm"], w["wf"], sv["h0"], dh1, sv["g1"], last_rider)
    grads = dict(norm_g=dg1.reshape(D_MODEL), w_in=dw_in, b_f=dbf[0, :N_HEADS],
                 qk_norm_g=dqkg[:, :HEAD_DIM] + dqkg[:, HEAD_DIM:], ple_norm_g=dg2.reshape(D_MODEL), **early)
    return dh0, grads, arrivals + last_arrivals


def _tables():
    T = TOK_T
    r = lax.broadcasted_iota(jnp.int32, (T, T), 0)
    c = lax.broadcasted_iota(jnp.int32, (T, T), 1)
    return dict(fox=_bias_tables(True), dil=_bias_tables(False),
                tril=(c <= r).astype(BF16), triu=(c >= r).astype(BF16))


def _local_step(x, p, positions, target, layers, small):
    rope = _rope_tables(positions)
    tabs = _tables()
    ws = [_layer_weights(*lw) for lw in layers]
    h = x
    saved = []
    for li, (w, lp, sm) in enumerate(zip(ws, p, small)):
        h, sv, _, *loss = _layer_fwd(h, (lp, ()), rope, tabs, w, *sm, target=target if li == len(ws) - 1 else None)
        saved.append(sv)
    dh, (loss,) = h, loss
    grads = [None] * len(ws)
    for li in reversed(range(len(ws))):
        dh, grads[li], _ = _layer_bwd(dh, saved[li], rope, tabs)
    return loss[0, 0], dh, grads


def _peers():
    x, y, c = lax.axis_index("x"), lax.axis_index("y"), lax.axis_index("c")
    me = 4 * x + 2 * y + c
    flip = lambda v, bit: 1 - v if bit else v
    return me, [(flip(x, k & 4), flip(y, k & 2), flip(c, k & 1)) for k in range(1, N_DEV)]


def _sel(ref, kind, d):
    if kind == "whole":
        return ref
    if kind == "slot":
        return ref.at[d]
    block = pl.ds(pl.multiple_of(d * 128, 128), 128)
    return ref.at[block, :] if kind == "rows" else ref.at[:, block]


class _Pushes:
    def __init__(self, arrays, src_kinds, dst_kinds, out_shapes):
        self.arrays, self.n = list(arrays), len(arrays)
        self.src_kinds, self.dst_kinds = src_kinds, dst_kinds
        self.out_shapes = [jax.ShapeDtypeStruct(s, a.dtype) for s, a in zip(out_shapes, arrays)]
        hbm = pl.BlockSpec(memory_space=pltpu.HBM)
        self.in_specs, self.out_specs = [hbm] * self.n, [hbm] * self.n
        self.scratch_shapes = [pltpu.SemaphoreType.DMA((N_DEV - 1, self.n)),
                               pltpu.SemaphoreType.DMA((N_DEV - 1, self.n)), pltpu.SemaphoreType.DMA((self.n,))]

    def _copies(self, ins, outs, sems):
        send_sems, recv_sems, local_sems = sems
        me, peers = _peers()
        src = lambda a, d: _sel(ins[a], self.src_kinds[a], d)
        dst = lambda a: _sel(outs[a], self.dst_kinds[a], me)
        local = [pltpu.make_async_copy(src(a, me), dst(a), local_sems.at[a]) for a in range(self.n)]
        remote = [pltpu.make_async_remote_copy(
            src_ref=src(a, 4 * px + 2 * py + pc), dst_ref=dst(a), send_sem=send_sems.at[k, a],
            recv_sem=recv_sems.at[k, a], device_id=(px, py, pc), device_id_type=pl.DeviceIdType.MESH)
            for k, (px, py, pc) in enumerate(peers) for a in range(self.n)]
        return local + remote

    def start(self, ins, outs, sems):
        for cp in self._copies(ins, outs, sems):
            cp.start()

    def wait(self, ins, outs, sems):
        for cp in self._copies(ins, outs, sems):
            cp.wait()


def _exchange(name, pushes):
    n = pushes.n

    def body(*refs):
        pushes.start(refs[:n], refs[n:2 * n], refs[2 * n:])
        pushes.wait(refs[:n], refs[n:2 * n], refs[2 * n:])

    return pl.pallas_call(body, name=name, in_specs=pushes.in_specs, out_specs=pushes.out_specs,
                          out_shape=pushes.out_shapes, scratch_shapes=pushes.scratch_shapes)(*pushes.arrays)


def _gather_two_level(shard, name):
    def body(x_ref, out_ref, send_sems, recv_sems, local_sem):
        x, y, c = lax.axis_index("x"), lax.axis_index("y"), lax.axis_index("c")
        me, sibling = (x, y, c), (x, y, 1 - c)
        chips = [(1 - x, y), (x, 1 - y), (1 - x, 1 - y)]
        slot = lambda px, py, pc: out_ref.at[4 * px + 2 * py + pc]

        def copy(k, block, to, src=None):
            return pltpu.make_async_remote_copy(
                src_ref=slot(*block) if src is None else src, dst_ref=slot(*block), send_sem=send_sems.at[k],
                recv_sem=recv_sems.at[k], device_id=to, device_id_type=pl.DeviceIdType.MESH)

        mine = pltpu.make_async_copy(x_ref, slot(*me), local_sem)
        mine.start()
        first = [copy(0, me, sibling, src=x_ref)] + [copy(1 + j, me, (*chip, c), src=x_ref)
                                                     for j, chip in enumerate(chips)]
        for cp in first:
            cp.start()
        passed = [copy(4 + j, (*chip, c), sibling) for j, chip in enumerate(chips)]
        for j, chip in enumerate(chips):
            copy(1 + j, (*chip, c), me).wait_recv()
            passed[j].start()
        copy(0, sibling, me).wait_recv()
        for j, chip in enumerate(chips):
            copy(4 + j, (*chip, 1 - c), me).wait_recv()
        for cp in first + passed:
            cp.wait_send()
        mine.wait()

    hbm = pl.BlockSpec(memory_space=pltpu.HBM)
    return pl.pallas_call(
        body, name=name, in_specs=[hbm], out_specs=hbm,
        out_shape=jax.ShapeDtypeStruct((N_DEV,) + shard.shape, shard.dtype),
        scratch_shapes=[pltpu.SemaphoreType.DMA((N_DEV - 1,)), pltpu.SemaphoreType.DMA((N_DEV - 1,)),
                        pltpu.SemaphoreType.DMA],
    )(shard)


def _gather_pushes(shards, kinds):
    full = {"slot": lambda s: (N_DEV,) + s, "rows": lambda s: (N_DEV * s[0], s[1]),
            "cols": lambda s: (s[0], N_DEV * s[1])}
    return _Pushes(shards, ["whole"] * len(shards), kinds, [full[k](a.shape) for a, k in zip(shards, kinds)])


def _scatter_pushes(partials, kinds):
    part = {"slot": lambda s: s[1:], "rows": lambda s: (128, s[1]), "cols": lambda s: (s[0], 128),
            "whole": lambda s: s}
    return _Pushes(partials, kinds, ["slot"] * len(partials),
                   [(N_DEV,) + part[k](a.shape) for a, k in zip(partials, kinds)])


def _adamw(name, parts, w, m, v, rows):
    L, R, C = w.shape

    def body(p_ref, w_ref, m_ref, v_ref, g_ref, d_ref, nm_ref, nv_ref):
        g = p_ref[0, 0].astype(F32)
        for s in range(1, N_DEV):
            g = g + p_ref[s, 0].astype(F32)
        g_ref[0] = g
        nm = ADAM_B1 * m_ref[0] + (1.0 - ADAM_B1) * g
        nv = ADAM_B2 * v_ref[0] + (1.0 - ADAM_B2) * (g * g)
        nm_ref[0] = nm
        nv_ref[0] = nv
        m_hat = nm / (1.0 - ADAM_B1 ** ADAM_STEP)
        v_hat = nv / (1.0 - ADAM_B2 ** ADAM_STEP)
        d_ref[0] = -ADAM_LR * (m_hat / (jnp.sqrt(v_hat) + ADAM_EPS) + ADAM_WD * w_ref[0])

    blk = pl.BlockSpec((1, rows, C), lambda l, i: (l, i, 0))
    shp = jax.ShapeDtypeStruct((L, R, C), F32)
    return pl.pallas_call(
        body, name=name, grid=(L, R // rows),
        in_specs=[pl.BlockSpec((N_DEV, 1, rows, C), lambda l, i: (0, l, i, 0)), blk, blk, blk],
        out_specs=[blk] * 4, out_shape=[shp] * 4,
        compiler_params=_cp("parallel", "parallel"),
    )(parts, w, m, v)


SMALL_ROWS = 40
LOSS_ROW = 37


def _pack_small(norm_g, ple_g, qk_g, b_f, last_row):
    rows = lambda a: a.astype(F32).reshape(-1, 128)
    flat = jnp.concatenate([rows(norm_g), rows(ple_g), rows(qk_g), _row(b_f.reshape(-1)), last_row], axis=0)
    return jnp.pad(flat, ((0, SMALL_ROWS - flat.shape[0]), (0, 0)))


def _unpack_small(flat):
    return (flat[0:16].reshape(2, D_MODEL), flat[16:32].reshape(2, D_MODEL), flat[32:36].reshape(2, 4, HEAD_DIM),
            flat[36, :2 * N_HEADS].reshape(2, N_HEADS))


def kernel(x, p, positions, norm_g, w_in, b_f, qk_norm_g, w_out, w_ple, ple_norm_g, w_ple_gate, loss_target, m_norm_g, m_w_in, m_b_f, m_qk_norm_g, m_w_out, m_w_ple, m_ple_norm_g, m_w_ple_gate, v_norm_g, v_w_in, v_b_f, v_qk_norm_g, v_w_out, v_w_ple, v_ple_norm_g, v_w_ple_gate):
    bf16 = lambda a: a.astype(BF16)
    rows_in = W_IN_ROWS // 2
    flat_in = lambda a: bf16(a).reshape(rows_in, 128)
    full_in = lambda g: g.reshape(N_DEV, D_MODEL, W_IN_SHARD).transpose(1, 0, 2).reshape(D_MODEL, N_IN)
    small = [(norm_g[l], b_f[l], qk_norm_g[l], ple_norm_g[l]) for l in range(2)]
    rope = _rope_tables(positions[0])
    tabs = _tables()

    g_in0 = _gather_two_level(flat_in(w_in[0]), "gather_first")
    rest = _gather_pushes([flat_in(w_in[1])] + [bf16(a[l]) for l in range(2) for a in (w_out, w_ple, w_ple_gate)],
                          ["slot"] + ["rows", "cols", "rows"] * 2)
    late = lambda got: dict(w_out=got[1], w_ple=got[2], w_pg=got[3])
    h, sv0, got = _layer_fwd(x[0], (p, (0, 0)), rope, tabs, _in_operands(full_in(g_in0)), *small[0], rest, late)
    w1 = dict(_in_operands(full_in(got[0])), w_out=got[4], w_ple=got[5], w_pg=got[6])
    dh, sv1, _, loss = _layer_fwd(h, (p, (1, 0)), rope, tabs, w1, *small[1], target=loss_target[0])
    dh, gr1, _ = _layer_bwd(dh, sv1, rope, tabs)

    by_dest = lambda d: d.reshape(D_MODEL, N_DEV, W_IN_SHARD).transpose(1, 0, 2).reshape(N_DEV, rows_in, 128)
    big = ("w_out", "w_ple", "w_ple_gate")
    riding = lambda early: _scatter_pushes([by_dest(gr1["w_in"])] + [gr1[n] for n in big] + [early[n] for n in big],
                                           ["slot"] + ["rows", "cols", "rows"] * 2)
    riding_last = lambda dw_in: _scatter_pushes([by_dest(dw_in)], ["slot"])
    dx, gr0, (r_in1, *r_big, r_in0) = _layer_bwd(dh, sv0, rope, tabs, riding, riding_last)
    grads = (gr0, gr1)
    stack = lambda name: jnp.stack([gl[name] for gl in grads], axis=0)
    small_part = _pack_small(stack("norm_g"), stack("ple_norm_g"), stack("qk_norm_g"), stack("b_f"),
                             _row(loss[0, 0].reshape(1)))
    (r_small,) = _exchange("exchange_small", _scatter_pushes([small_part], ["whole"]))
    r_in = jnp.concatenate([r_in0, r_in1], axis=1)
    r_out, r_ple, r_pg = (jnp.stack([r_big[3 + k], r_big[k]], axis=1) for k in range(3))

    zero_row = jnp.zeros((1, 128), F32)
    small_of = lambda ng, pg, qk, bf: _pack_small(ng, pg, qk, bf, zero_row)[None]
    flat = lambda a: a.reshape(1, W_IN_ROWS, 128)
    outs = dict(
        w_in=[o.reshape(w_in.shape) for o in
              _adamw("adamw_in", r_in[:, None], flat(w_in), flat(m_w_in), flat(v_w_in), W_IN_TILE)],
        w_out=_adamw("adamw_out", r_out, w_out, m_w_out, v_w_out, 128),
        w_ple=_adamw("adamw_ple", r_ple, w_ple, m_w_ple, v_w_ple, 256),
        w_pg=_adamw("adamw_gate", r_pg, w_ple_gate, m_w_ple_gate, v_w_ple_gate, 128),
        small=_adamw("adamw_small", r_small[:, None], small_of(norm_g, ple_norm_g, qk_norm_g, b_f),
                     small_of(m_norm_g, m_ple_norm_g, m_qk_norm_g, m_b_f),
                     small_of(v_norm_g, v_ple_norm_g, v_qk_norm_g, v_b_f), SMALL_ROWS))
    leaves = []
    for kind in range(4):
        ng, pg, qk, bf = _unpack_small(outs["small"][kind][0])
        leaves += [ng, outs["w_in"][kind], bf, qk, outs["w_out"][kind], outs["w_ple"][kind], pg, outs["w_pg"][kind]]
    return (outs["small"][0][0, LOSS_ROW, 0], dx[None], *leaves)
```

```python
import functools

import jax
import jax.numpy as jnp
from jax import lax
from jax.experimental import pallas as pl
from jax.experimental.pallas import tpu as pltpu

F32 = jnp.float32
BF16 = jnp.bfloat16

D_MODEL = 1024
HEAD_DIM = 64
N_HEADS = 8
HEAD_PAD = 128
D_BRANCH = N_HEADS * HEAD_DIM
N_MAIN = 8 * D_BRANCH
N_IN = N_MAIN + N_HEADS
PLE_DIM = 256
ROPE_THETA = 500000.0
ROPE_HALF = 8
EPS = 1e-6
NEG = -1e30
SCALE = HEAD_DIM ** -0.5
LOG2E = 1.4426950408889634
LN2 = 0.6931471805599453
DILATED_PATTERNS = ((128, 1), (512, 4), (2048, 16))
N_DEV = 8
W_IN_SHARD = N_IN // N_DEV
W_IN_ROWS = 2 * D_MODEL * W_IN_SHARD // 128
W_IN_TILE = W_IN_ROWS // 19

ADAM_LR = 0.001
ADAM_B1 = 0.9
ADAM_B2 = 0.999
ADAM_EPS = 1e-08
ADAM_WD = 0.01
ADAM_STEP = 10

ATT_T = 512
ATT_FWD_HEADS = 8
ATT_BWD_HEADS = 4
ATT_CHUNK = 32
ATT_CHUNK_CAUSAL_FWD = 16
TOK_T = 256
VMEM_LIMIT = 60 * 1024 * 1024


def _slab_spec(lead, block, index):
    return pl.BlockSpec((None,) * len(lead) + block, lambda *g: (*lead, *index(*g)))


def _cp(*sem):
    return pltpu.CompilerParams(dimension_semantics=sem, vmem_limit_bytes=VMEM_LIMIT)


def _sigmoid(x):
    return 1.0 / (1.0 + jnp.exp(-x))


def _split3(x):
    hi = x.astype(BF16)
    r1 = x - hi.astype(F32)
    mid = r1.astype(BF16)
    lo = (r1 - mid.astype(F32)).astype(BF16)
    return hi, mid, lo


def _dot(a, b):
    return jnp.dot(a, b, preferred_element_type=F32)


def _dot_nt(a, b):
    return lax.dot_general(a, b, (((1,), (1,)), ((), ())), preferred_element_type=F32)


def _dot_tn(a, b):
    return lax.dot_general(a, b, (((0,), (0,)), ((), ())), preferred_element_type=F32)


def _inproj_fwd(h, g, wm, wf):
    S = h.shape[0]

    def body(h_ref, g_ref, wm_ref, wf_ref, zm_ref, zf_ref, u_ref):
        x = h_ref[...]
        r = lax.rsqrt(jnp.mean(x * x, axis=-1, keepdims=True) + EPS)
        u = (x * r * g_ref[...]).astype(BF16)
        u_ref[...] = u
        zm_ref[...] = _dot(u, wm_ref[...]).astype(BF16)
        zf_ref[...] = _dot(u, wf_ref[...])

    return pl.pallas_call(
        body, name="inproj_fwd", grid=(S // TOK_T,),
        in_specs=[pl.BlockSpec((TOK_T, D_MODEL), lambda i: (i, 0)),
                  pl.BlockSpec((1, D_MODEL), lambda i: (0, 0)),
                  pl.BlockSpec((D_MODEL, N_MAIN), lambda i: (0, 0)),
                  pl.BlockSpec((D_MODEL, 128), lambda i: (0, 0))],
        out_specs=[pl.BlockSpec((TOK_T, N_MAIN), lambda i: (i, 0)),
                   pl.BlockSpec((TOK_T, 128), lambda i: (i, 0)),
                   pl.BlockSpec((TOK_T, D_MODEL), lambda i: (i, 0))],
        out_shape=[jax.ShapeDtypeStruct((S, N_MAIN), BF16), jax.ShapeDtypeStruct((S, 128), F32),
                   jax.ShapeDtypeStruct((S, D_MODEL), BF16)],
        compiler_params=_cp("parallel"),
    )(h, g, wm, wf)


def _log_sigmoid(x):
    return jnp.minimum(x, 0.0) - jnp.log(1.0 + jnp.exp(-jnp.abs(x)))


def _same_head():
    r = lax.broadcasted_iota(jnp.int32, (HEAD_PAD, HEAD_PAD), 0) // HEAD_DIM
    c = lax.broadcasted_iota(jnp.int32, (HEAD_PAD, HEAD_PAD), 1) // HEAD_DIM
    return (r == c).astype(BF16)


def _pair_mean(x, same_head):
    hi = x.astype(BF16)
    lo = (x - hi.astype(F32)).astype(BF16)
    return (_dot(hi, same_head) + _dot(lo, same_head)) * (1.0 / HEAD_DIM)


def _pair_rsqrt(x, same_head):
    return lax.rsqrt(_pair_mean(x * x, same_head) + EPS)


def _prep_fwd(zm, zf, bf, tril, qkg, rope_c, rope_a, rope_b):
    S = zm.shape[0]
    shp = jax.ShapeDtypeStruct((N_HEADS, S, HEAD_PAD), BF16)

    def body(z_ref, zf_ref, b_ref, tri_ref, g_ref, rc_ref, ra_ref, rb_ref,
             qa_ref, ka_ref, va_ref, qb_ref, kb_ref, vb_ref, carry):
        @pl.when(pl.program_id(0) == 0)
        def _():
            carry[...] = jnp.zeros_like(carry)

        tri = tri_ref[...]
        cs = sum(_dot(tri, part) for part in _split3(_log_sigmoid(zf_ref[...] + b_ref[...]))) + carry[...]
        carry[...] = cs[TOK_T - 1:TOK_T, :]
        lane = lax.broadcasted_iota(jnp.int32, (TOK_T, HEAD_PAD), 1)
        lo_half = lane < HEAD_DIM
        aug = (lane >= HEAD_DIM) & (lane < HEAD_DIM + 3)
        q_pad = jnp.where(aug, -1.0, 0.0)
        rc, ra, rb = rc_ref[...], ra_ref[...], rb_ref[...]
        same_head = _same_head()

        def norm(col, gi):
            x = z_ref[:, col:col + HEAD_PAD].astype(F32)
            return x * _pair_rsqrt(x, same_head) * g_ref[gi:gi + 1, :]

        def rope(y):
            return y * rc + pltpu.roll(y, HEAD_PAD - ROPE_HALF, 1) * ra + pltpu.roll(y, ROPE_HALF, 1) * rb

        def put(ref, pi, y, pad_even, pad_odd):
            ref[2 * pi] = jnp.where(lo_half, y, pad_even).astype(BF16)
            ref[2 * pi + 1] = jnp.where(lo_half, pltpu.roll(y, HEAD_DIM, 1), pad_odd).astype(BF16)

        def k_pad(h):
            ch = cs[:, h:h + 1] * LOG2E
            hi = ch.astype(BF16).astype(F32)
            mid = (ch - hi).astype(BF16).astype(F32)
            lo = ch - hi - mid
            ones = jnp.where(lane == HEAD_DIM + 3, 1.0, 0.0)
            return jnp.where(lane == HEAD_DIM, hi, jnp.where(lane == HEAD_DIM + 1, mid,
                                                              jnp.where(lane == HEAD_DIM + 2, lo, ones)))

        for pi in range(N_HEADS // 2):
            col = HEAD_PAD * pi
            put(qa_ref, pi, norm(col, 0) * (SCALE * LOG2E), q_pad, q_pad)
            put(ka_ref, pi, norm(D_BRANCH + col, 1), k_pad(2 * pi), k_pad(2 * pi + 1))
            put(va_ref, pi, z_ref[:, 2 * D_BRANCH + col:2 * D_BRANCH + col + HEAD_PAD].astype(F32), 0.0, 0.0)
            put(qb_ref, pi, rope(norm(4 * D_BRANCH + col, 2)) * (SCALE * LOG2E), 0.0, 0.0)
            put(kb_ref, pi, rope(norm(5 * D_BRANCH + col, 3)), 0.0, 0.0)
            put(vb_ref, pi, z_ref[:, 6 * D_BRANCH + col:6 * D_BRANCH + col + HEAD_PAD].astype(F32), 0.0, 0.0)

    tok = lambda w: pl.BlockSpec((TOK_T, w), lambda i: (i, 0))
    head = pl.BlockSpec((N_HEADS, TOK_T, HEAD_PAD), lambda i: (0, i, 0))
    return pl.pallas_call(
        body, name="prep_fwd", grid=(S // TOK_T,),
        in_specs=[tok(N_MAIN), tok(128), pl.BlockSpec((1, 128), lambda i: (0, 0)),
                  pl.BlockSpec((TOK_T, TOK_T), lambda i: (0, 0)), pl.BlockSpec((4, 128), lambda i: (0, 0)),
                  tok(128), tok(128), tok(128)],
        out_specs=[head] * 6, out_shape=[shp] * 6,
        scratch_shapes=[pltpu.VMEM((1, 128), F32)],
        compiler_params=_cp("arbitrary"),
    )(zm, zf, bf, tril, qkg, rope_c, rope_a, rope_b)


def _pair(ref, pi, lo_half):
    return jnp.where(lo_half, ref[2 * pi].astype(F32), pltpu.roll(ref[2 * pi + 1].astype(F32), HEAD_DIM, 1))


def _mid_fwd(oa, ob, zm, h0, p, w_out, w_pg, w_ple, g2, target=None):
    S = h0.shape[0]
    p, p_lead = p

    def body(oa_ref, ob_ref, ga_ref, gb_ref, h0_ref, p_ref, wo_ref, wg_ref, wp_ref, g2_ref, *rest):
        t_ref, rest = (rest[0], rest[1:]) if target is not None else (None, rest)
        y_ref, h1_ref, h2_ref, u2_ref, e_ref, gate_ref, *loss_ref = rest
        parts = []
        for o_ref, g_ref in ((oa_ref, ga_ref), (ob_ref, gb_ref)):
            for pi in range(N_HEADS // 2):
                g = g_ref[:, HEAD_PAD * pi:HEAD_PAD * (pi + 1)].astype(F32)
                parts.append((o_ref[pi] * (g * _sigmoid(g))).astype(BF16))
        y = jnp.concatenate(parts, axis=1)
        y_ref[...] = y
        h1 = h0_ref[...] + _dot(y, wo_ref[...])
        h1_ref[...] = h1
        r = lax.rsqrt(jnp.mean(h1 * h1, axis=-1, keepdims=True) + EPS)
        u2 = (h1 * r * g2_ref[...]).astype(BF16)
        u2_ref[...] = u2
        gate = _sigmoid(_dot(u2, wg_ref[...]))
        e = _dot(p_ref[...].astype(BF16), wp_ref[...])
        e_ref[...] = e.astype(BF16)
        gate_ref[...] = gate.astype(BF16)
        h2 = h1 + e * gate
        if target is None:
            h2_ref[...] = h2
        else:
            @pl.when(pl.program_id(0) == 0)
            def _():
                loss_ref[0][...] = jnp.zeros_like(loss_ref[0])

            err = h2 - t_ref[...]
            h2_ref[...] = err * (1.0 / D_MODEL)
            part = jnp.sum(jnp.sum(err * err, axis=1, keepdims=True), axis=0, keepdims=True)
            loss_ref[0][...] += part * (0.5 / D_MODEL)

    tok = lambda w: pl.BlockSpec((TOK_T, w), lambda i: (i, 0))
    head = pl.BlockSpec((N_HEADS // 2, TOK_T, HEAD_PAD), lambda i: (0, i, 0))
    full = lambda a, b: pl.BlockSpec((a, b), lambda i: (0, 0))
    act = lambda dt: jax.ShapeDtypeStruct((S, D_MODEL), dt)
    fused = target is not None
    return pl.pallas_call(
        body, name="mid_fwd_loss" if fused else "mid_fwd", grid=(S // TOK_T,),
        in_specs=[head, head,
                  pl.BlockSpec((TOK_T, D_BRANCH), lambda i: (i, 3)), pl.BlockSpec((TOK_T, D_BRANCH), lambda i: (i, 7)),
                  tok(D_MODEL), _slab_spec(p_lead, (TOK_T, PLE_DIM), lambda i: (i, 0)), full(D_MODEL, D_MODEL),
                  full(D_MODEL, D_MODEL), full(PLE_DIM, D_MODEL), full(1, D_MODEL)] + [tok(D_MODEL)] * fused,
        out_specs=[tok(D_MODEL)] * 6 + [full(8, 128)] * fused,
        out_shape=[act(BF16), act(F32), act(F32), act(BF16), act(BF16), act(BF16)]
        + [jax.ShapeDtypeStruct((8, 128), F32)] * fused,
        compiler_params=_cp("arbitrary" if fused else "parallel"),
    )(oa, ob, zm, zm, h0, p, w_out, w_pg, w_ple, g2, *([target] * fused))


def _bias_tables(full_range):
    T = ATT_T
    nb = 1 if full_range else DILATED_PATTERNS[-1][0] // T + 1
    r = lax.broadcasted_iota(jnp.int32, (nb, T, T), 2)
    c = lax.broadcasted_iota(jnp.int32, (nb, T, T), 1)
    b = lax.broadcasted_iota(jnp.int32, (nb, T, T), 0)
    delta = T * b + r - c
    if full_range:
        bias = jnp.where(delta >= 0, 0.0, NEG).astype(F32)
    else:
        mult = jnp.zeros((nb, T, T), F32)
        for window, dil in DILATED_PATTERNS:
            ok = (delta >= 0) & (delta <= window) & (delta % dil == 0)
            mult = mult + ok.astype(F32)
        bias = jnp.where(mult > 0, jnp.log2(jnp.maximum(mult, 1.0)), NEG).astype(F32)
    return bias


def _call_with_rider(body, name, grid, rider, in_specs, out_specs, out_shape, scratch_shapes, operands,
                     semantics=("parallel", "arbitrary")):
    if rider is None:
        return pl.pallas_call(body, name=name, grid=grid, in_specs=in_specs, out_specs=out_specs,
                              out_shape=out_shape, scratch_shapes=scratch_shapes,
                              compiler_params=_cp(*semantics))(*operands)
    n, n_in, n_out = rider.n, len(in_specs), len(out_specs)

    def wrapped(*refs):
        ins, r_ins = refs[:n_in], refs[n_in:n_in + n]
        outs, r_outs = refs[n_in + n:n_in + n + n_out], refs[n_in + n + n_out:n_in + 2 * n + n_out]
        scratch, sems = refs[n_in + 2 * n + n_out:-3], refs[-3:]
        step = [pl.program_id(a) for a in range(len(grid))]

        @pl.when(functools.reduce(jnp.logical_and, [s == 0 for s in step]))
        def _():
            rider.start(r_ins, r_outs, sems)

        body(*ins, *outs, *scratch)

        @pl.when(functools.reduce(jnp.logical_and, [s == g - 1 for s, g in zip(step, grid)]))
        def _():
            rider.wait(r_ins, r_outs, sems)

    return pl.pallas_call(
        wrapped, name=name, grid=grid, in_specs=list(in_specs) + rider.in_specs,
        out_specs=list(out_specs) + rider.out_specs, out_shape=list(out_shape) + rider.out_shapes,
        scratch_shapes=list(scratch_shapes) + rider.scratch_shapes,
        compiler_params=_cp(*["arbitrary"] * len(grid)))(*operands, *rider.arrays)


def _attn_fwd(q, k, v, table_t, full_range, name, rider=None):
    H, S, _ = q.shape
    T = ATT_T
    nb = table_t.shape[0]
    HB = ATT_FWD_HEADS
    KC = ATT_CHUNK_CAUSAL_FWD if full_range else ATT_CHUNK
    chunks = [slice(c, c + KC) for c in range(0, T, KC)]
    fold = lambda x, op: functools.reduce(op, [x[r:r + 8] for r in range(0, KC, 8)])

    def body(q_ref, k_ref, v_ref, tab_ref, o_ref, lse_ref, *scratch):
        st_refs, pt_refs, acc_refs = scratch[:HB], scratch[HB:2 * HB], scratch[2 * HB:]
        i = pl.program_id(1)
        rows = lambda j: pl.ds(pl.multiple_of(j * T, T), T)

        def scores(hh, j):
            st_refs[hh][...] = _dot_nt(k_ref[hh, rows(j), :], q_ref[hh])

        def block(j, b, nxt, stats):
            out = []
            for hh, (m, l) in enumerate(stats):
                st_ref, pt_ref, acc_ref = st_refs[hh], pt_refs[hh], acc_refs[hh]
                mx = None
                for ch in chunks:
                    x = st_ref[ch, :]
                    if b is not None:
                        x = x + tab_ref[b, ch, :]
                        st_ref[ch, :] = x
                    x = fold(x, jnp.maximum)
                    mx = x if mx is None else jnp.maximum(mx, x)
                m_new = jnp.maximum(m, jnp.max(mx, axis=0, keepdims=True))
                alpha = jnp.exp2(m - m_new)
                ls = None
                for ch in chunks:
                    pc = jnp.exp2(st_ref[ch, :] - m_new)
                    pt_ref[ch, :] = pc.astype(BF16)
                    pc = fold(pc, jnp.add)
                    ls = pc if ls is None else ls + pc
                if nxt is not None:
                    scores(hh, nxt)
                acc_ref[...] = alpha * acc_ref[...] + _dot_tn(v_ref[hh, rows(j), :], pt_ref[...])
                out.append((m_new, alpha * l + jnp.sum(ls, axis=0, keepdims=True)))
            return tuple(out)

        lo = 0 if full_range else jnp.maximum(i - (nb - 1), 0)
        for hh in range(HB):
            acc_refs[hh][...] = jnp.zeros_like(acc_refs[hh])
            scores(hh, lo)
        stats = lax.fori_loop(lo, i, lambda j, st: block(j, None if full_range else i - j, j + 1, st),
                              ((jnp.full((1, T), NEG, F32), jnp.zeros((1, T), F32)),) * HB)
        stats = block(i, 0, None, stats)
        o_t = [acc_refs[hh][...] * (1.0 / l) for hh, (m, l) in enumerate(stats)]
        for hh, (m, l) in enumerate(stats):
            lse_ref[hh, 0] = m + jnp.log2(l)
        for hp in range(HB // 2):
            o_ref[hp] = jnp.concatenate([o_t[2 * hp][:HEAD_DIM], o_t[2 * hp + 1][:HEAD_DIM]], axis=0).T

    return _call_with_rider(
        body, name, (H // HB, S // T), rider,
        in_specs=[pl.BlockSpec((HB, T, HEAD_PAD), lambda h, i: (h, i, 0)),
                  pl.BlockSpec((HB, S, HEAD_PAD), lambda h, i: (h, 0, 0), pipeline_mode=pl.Buffered(1)),
                  pl.BlockSpec((HB, S, HEAD_PAD), lambda h, i: (h, 0, 0), pipeline_mode=pl.Buffered(1)),
                  pl.BlockSpec((nb, T, T), lambda h, i: (0, 0, 0), pipeline_mode=pl.Buffered(1))],
        out_specs=[pl.BlockSpec((HB // 2, T, HEAD_PAD), lambda h, i: (h, i, 0)),
                   pl.BlockSpec((HB, 1, 1, T), lambda h, i: (h, i, 0, 0))],
        out_shape=[jax.ShapeDtypeStruct((H // 2, S, HEAD_PAD), F32), jax.ShapeDtypeStruct((H, S // T, 1, T), F32)],
        scratch_shapes=([pltpu.VMEM((T, T), F32)] * HB + [pltpu.VMEM((T, T), BF16)] * HB
                        + [pltpu.VMEM((HEAD_PAD, T), F32)] * HB),
        operands=(q, k, v, table_t))


def _attn_bwd(q, k, v, do, lse, dd, table_t, full_range, name, rider=None, narrow=(False, False, False)):
    H, S, _ = q.shape
    T = ATT_T
    nq = S // T
    nb = table_t.shape[0]
    HB = ATT_BWD_HEADS
    KC = ATT_CHUNK
    chunks = [slice(c, c + KC) for c in range(0, T, KC)]

    def body(q_ref, do_ref, lse_ref, dd_ref, k_ref, v_ref, tab_ref, dq_hbm, dk_ref, dv_ref, *scratch):
        st_refs, dpt_refs, pt_refs, dst_refs = (scratch[n * HB:(n + 1) * HB] for n in range(4))
        dq_ref, dk_acc, dv_acc, *dq_cast, dq_sem = scratch[4 * HB:]
        h = pl.program_id(0)
        j = pl.program_id(1)

        @pl.when(j == 0)
        def _():
            dq_ref[...] = jnp.zeros_like(dq_ref)

        dk_acc[...] = jnp.zeros_like(dk_acc)
        dv_acc[...] = jnp.zeros_like(dv_acc)

        def step(i, b):
            rows = pl.ds(pl.multiple_of(i * T, T), T)
            for hh in range(HB):
                st_refs[hh][...] = _dot_nt(k_ref[hh], q_ref[hh, rows, :])
                dpt_refs[hh][...] = _dot_nt(v_ref[hh], do_ref[hh, rows, :])
            for hh in range(HB):
                lse_i = lse_ref[hh, i]
                dd_i = dd_ref[hh, i]
                for ch in chunks:
                    x = st_refs[hh][ch, :]
                    if b is not None:
                        x = x + tab_ref[b, ch, :]
                    pc = jnp.exp2(x - lse_i)
                    pt_refs[hh][ch, :] = pc.astype(BF16)
                    dst_refs[hh][ch, :] = (pc * (dpt_refs[hh][ch, :] - dd_i)).astype(BF16)
                dv_acc[hh] += _dot(pt_refs[hh][...], do_ref[hh, rows, :])
                dk_acc[hh] += _dot(dst_refs[hh][...], q_ref[hh, rows, :])
                dq_ref[hh, rows, :] += _dot_tn(dst_refs[hh][...], k_ref[hh])

        step(j, 0)
        if full_range:
            pl.loop(j + 1, nq)(lambda i: step(i, None))
        else:
            pl.loop(j + 1, jnp.minimum(j + nb, nq))(lambda i: step(i, i - j))
        dk_ref[...] = dk_acc[...].astype(dk_ref.dtype)
        dv_ref[...] = dv_acc[...].astype(dv_ref.dtype)

        @pl.when(j == nq - 1)
        def _():
            src = dq_ref
            if narrow[0]:
                src, = dq_cast
                src[...] = dq_ref[...].astype(BF16)
            out = pltpu.make_async_copy(src, dq_hbm.at[pl.ds(h * HB, HB)], dq_sem)
            out.start()
            out.wait()

    once = dict(pipeline_mode=pl.Buffered(1))
    per_head = pl.BlockSpec((HB, S, HEAD_PAD), lambda h, j: (h, 0, 0), **once)
    rows = pl.BlockSpec((HB, nq, 1, T), lambda h, j: (h, 0, 0, 0))
    blk = pl.BlockSpec((HB, T, HEAD_PAD), lambda h, j: (h, j, 0))
    shp = [jax.ShapeDtypeStruct((H, S, HEAD_PAD), BF16 if nar else F32) for nar in narrow]
    acc = pltpu.VMEM((HB, T, HEAD_PAD), F32)
    return _call_with_rider(
        body, name, (H // HB, nq), rider,
        in_specs=[per_head, per_head, rows, rows, blk, blk,
                  pl.BlockSpec((nb, T, T), lambda h, j: (0, 0, 0), **once)],
        out_specs=[pl.BlockSpec(memory_space=pltpu.HBM), blk, blk], out_shape=shp,
        scratch_shapes=([pltpu.VMEM((T, T), F32)] * (2 * HB) + [pltpu.VMEM((T, T), BF16)] * (2 * HB)
                        + [pltpu.VMEM((HB, S, HEAD_PAD), F32), acc, acc]
                        + [pltpu.VMEM((HB, S, HEAD_PAD), BF16)] * narrow[0] + [pltpu.SemaphoreType.DMA]),
        operands=(q, do, lse, dd, k, v, table_t))


def _mid_bwd(dh2, h1, e, gate, g2, w_pg, w_out, oa, ob, zm):
    S = dh2.shape[0]

    def body(dh2_ref, h1_ref, e_ref, gate_ref, g2_ref, wg_ref, wo_ref, oa_ref, ob_ref, ga_ref, gb_ref,
             dh1_ref, dh1b_ref, de_ref, dpre_ref, doa_ref, dob_ref, dga_ref, dgb_ref, dd_ref, dg2_ref):
        @pl.when(pl.program_id(0) == 0)
        def _():
            dg2_ref[...] = jnp.zeros_like(dg2_ref)

        lane = lax.broadcasted_iota(jnp.int32, (TOK_T, HEAD_PAD), 1)
        lo_half = lane < HEAD_DIM
        dh2 = dh2_ref[...]
        gate = gate_ref[...]
        de_ref[...] = (dh2 * gate).astype(BF16)
        dpre = (dh2 * e_ref[...] * gate * (1.0 - gate)).astype(BF16)
        dpre_ref[...] = dpre
        du2 = _dot_nt(dpre, wg_ref[...])
        h1 = h1_ref[...]
        r = lax.rsqrt(jnp.mean(h1 * h1, axis=-1, keepdims=True) + EPS)
        xh = h1 * r
        a = du2 * g2_ref[...]
        dh1 = dh2 + r * (a - xh * jnp.mean(a * xh, axis=-1, keepdims=True))
        dg2_ref[...] += jnp.sum(du2 * xh, axis=0, keepdims=True)
        dh1_ref[...] = dh1
        dh1b = dh1.astype(BF16)
        dh1b_ref[...] = dh1b
        dy = _dot_nt(dh1b, wo_ref[...])
        dd = jnp.zeros((TOK_T, HEAD_PAD), F32)
        for bi, (o_ref, g_ref, do_ref, dg_ref) in enumerate(
                ((oa_ref, ga_ref, doa_ref, dga_ref), (ob_ref, gb_ref, dob_ref, dgb_ref))):
            for pi in range(N_HEADS // 2):
                col = bi * D_BRANCH + HEAD_PAD * pi
                dyp = dy[:, col:col + HEAD_PAD]
                g = g_ref[:, HEAD_PAD * pi:HEAD_PAD * (pi + 1)].astype(F32)
                sg = _sigmoid(g)
                o_pair = o_ref[pi]
                dg_ref[:, HEAD_PAD * pi:HEAD_PAD * (pi + 1)] = (
                    dyp * o_pair * (sg * (1.0 + g * (1.0 - sg)))).astype(BF16)
                dop = dyp * (g * sg)
                prod = dop * o_pair
                for hh, d_head, mine in ((2 * pi, dop, lo_half),
                                         (2 * pi + 1, pltpu.roll(dop, HEAD_DIM, 1), ~lo_half)):
                    do_ref[hh] = jnp.where(lo_half, d_head, 0.0).astype(BF16)
                    dsum = jnp.sum(jnp.where(mine, prod, 0.0), axis=1, keepdims=True)
                    dd = dd + jnp.where(lane == bi * N_HEADS + hh, dsum, 0.0)
        dd_ref[...] = dd.T[:2 * N_HEADS, :]

    tok = lambda w: pl.BlockSpec((TOK_T, w), lambda i: (i, 0))
    head = pl.BlockSpec((N_HEADS, TOK_T, HEAD_PAD), lambda i: (0, i, 0))
    pairs = pl.BlockSpec((N_HEADS // 2, TOK_T, HEAD_PAD), lambda i: (0, i, 0))
    full = lambda a, b: pl.BlockSpec((a, b), lambda i: (0, 0))
    act = lambda w, dt: jax.ShapeDtypeStruct((S, w), dt)
    hshape = lambda w, dt: jax.ShapeDtypeStruct((N_HEADS, S, w), dt)
    return pl.pallas_call(
        body, name="mid_bwd", grid=(S // TOK_T,),
        in_specs=[tok(D_MODEL)] * 4 + [full(1, D_MODEL), full(D_MODEL, D_MODEL), full(D_MODEL, D_MODEL), pairs, pairs,
                                      pl.BlockSpec((TOK_T, D_BRANCH), lambda i: (i, 3)),
                                      pl.BlockSpec((TOK_T, D_BRANCH), lambda i: (i, 7))],
        out_specs=[tok(D_MODEL)] * 4 + [head, head, tok(D_BRANCH), tok(D_BRANCH),
                                       pl.BlockSpec((2 * N_HEADS, TOK_T), lambda i: (0, i)), full(1, D_MODEL)],
        out_shape=[act(D_MODEL, F32), act(D_MODEL, BF16), act(D_MODEL, BF16), act(D_MODEL, BF16),
                   hshape(HEAD_PAD, BF16), hshape(HEAD_PAD, BF16), act(D_BRANCH, BF16), act(D_BRANCH, BF16),
                   jax.ShapeDtypeStruct((2 * N_HEADS, S), F32), jax.ShapeDtypeStruct((1, D_MODEL), F32)],
        compiler_params=_cp("arbitrary"),
    )(dh2, h1, e, gate, g2, w_pg, w_out, oa, ob, zm, zm)


def _prep_bwd(dqa, dka, dva, dqb, dkb, dvb, zm, qkg, rope_c, rope_a, rope_b, dga, dgb, zf, bf, triu):
    S = zm.shape[0]
    n = S // TOK_T

    def body(dqa_ref, dka_ref, dva_ref, dqb_ref, dkb_ref, dvb_ref, z_ref, g_ref, rc_ref, ra_ref, rb_ref,
             dga_ref, dgb_ref, zf_ref, b_ref, tri_ref, dz_ref, dzf_ref, dqkg_ref, db_ref, carry):
        @pl.when(pl.program_id(0) == 0)
        def _():
            dqkg_ref[...] = jnp.zeros_like(dqkg_ref)
            db_ref[...] = jnp.zeros_like(db_ref)
            carry[...] = jnp.zeros_like(carry)

        lane = lax.broadcasted_iota(jnp.int32, (TOK_T, HEAD_PAD), 1)
        lo_half = lane < HEAD_DIM
        rc, ra, rb = rc_ref[...], ra_ref[...], rb_ref[...]
        same_head = _same_head()

        def unrope(dy):
            return dy * rc + pltpu.roll(dy * ra, ROPE_HALF, 1) + pltpu.roll(dy * rb, HEAD_PAD - ROPE_HALF, 1)

        def norm_bwd(col, gi, dy):
            x = z_ref[:, col:col + HEAD_PAD].astype(F32)
            r = _pair_rsqrt(x, same_head)
            xh = x * r
            dqkg_ref[gi:gi + 1, :] += jnp.sum(dy * xh, axis=0, keepdims=True)
            a = dy * g_ref[gi:gi + 1, :]
            dz_ref[:, col:col + HEAD_PAD] = (r * (a - xh * _pair_mean(a * xh, same_head))).astype(BF16)

        dc = jnp.zeros((TOK_T, HEAD_PAD), F32)
        for pi in range(N_HEADS // 2):
            col = HEAD_PAD * pi
            norm_bwd(col, 0, _pair(dqa_ref, pi, lo_half) * SCALE)
            norm_bwd(D_BRANCH + col, 1, _pair(dka_ref, pi, lo_half) * LN2)
            dz_ref[:, 2 * D_BRANCH + col:2 * D_BRANCH + col + HEAD_PAD] = _pair(dva_ref, pi, lo_half).astype(BF16)
            norm_bwd(4 * D_BRANCH + col, 2, unrope(_pair(dqb_ref, pi, lo_half) * SCALE))
            norm_bwd(5 * D_BRANCH + col, 3, unrope(_pair(dkb_ref, pi, lo_half) * LN2))
            dz_ref[:, 6 * D_BRANCH + col:6 * D_BRANCH + col + HEAD_PAD] = _pair(dvb_ref, pi, lo_half).astype(BF16)
            for hh in (2 * pi, 2 * pi + 1):
                dch = dka_ref[hh][:, HEAD_DIM:HEAD_DIM + 1] + dqa_ref[hh][:, HEAD_DIM + 3:HEAD_DIM + 4]
                dc = dc + jnp.where(lane == hh, dch, 0.0)
        dz_ref[:, 3 * D_BRANCH:4 * D_BRANCH] = dga_ref[...]
        dz_ref[:, 7 * D_BRANCH:8 * D_BRANCH] = dgb_ref[...]
        tri = tri_ref[...]
        dlf = sum(_dot(tri, part) for part in _split3(dc)) + carry[...]
        carry[...] = dlf[0:1, :]
        dfa = dlf * (1.0 - _sigmoid(zf_ref[...] + b_ref[...]))
        dzf_ref[...] = dfa.astype(BF16)
        db_ref[...] += jnp.sum(dfa, axis=0, keepdims=True)

    tok = lambda w: pl.BlockSpec((TOK_T, w), lambda i: (n - 1 - i, 0))
    head = pl.BlockSpec((N_HEADS, TOK_T, HEAD_PAD), lambda i: (0, n - 1 - i, 0))
    fixed = lambda a, b: pl.BlockSpec((a, b), lambda i: (0, 0))
    return pl.pallas_call(
        body, name="prep_bwd", grid=(n,),
        in_specs=[head] * 6 + [tok(N_MAIN), fixed(4, 128), tok(128), tok(128), tok(128), tok(D_BRANCH), tok(D_BRANCH),
                               tok(128), fixed(1, 128), fixed(TOK_T, TOK_T)],
        out_specs=[tok(N_MAIN), tok(128), fixed(4, 128), fixed(1, 128)],
        out_shape=[jax.ShapeDtypeStruct((S, N_MAIN), BF16), jax.ShapeDtypeStruct((S, 128), BF16),
                   jax.ShapeDtypeStruct((4, 128), F32), jax.ShapeDtypeStruct((1, 128), F32)],
        scratch_shapes=[pltpu.VMEM((1, 128), F32)],
        compiler_params=_cp("arbitrary"),
    )(dqa, dka, dva, dqb, dkb, dvb, zm, qkg, rope_c, rope_a, rope_b, dga, dgb, zf, bf, triu)


def _inproj_bwd(dzm, dzf, wm, wf, h0, dh1, g, rider=None):
    S = h0.shape[0]

    def body(dzm_ref, dzf_ref, wm_ref, wf_ref, h_ref, dh1_ref, g_ref, dh0_ref, dg_ref):
        @pl.when(pl.program_id(0) == 0)
        def _():
            dg_ref[...] = jnp.zeros_like(dg_ref)

        du = _dot_nt(dzm_ref[...], wm_ref[...]) + _dot_nt(dzf_ref[...], wf_ref[...])
        x = h_ref[...]
        r = lax.rsqrt(jnp.mean(x * x, axis=-1, keepdims=True) + EPS)
        xh = x * r
        a = du * g_ref[...]
        dh0_ref[...] = dh1_ref[...] + r * (a - xh * jnp.mean(a * xh, axis=-1, keepdims=True))
        dg_ref[...] += jnp.sum(du * xh, axis=0, keepdims=True)

    tok = lambda w: pl.BlockSpec((TOK_T, w), lambda i: (i, 0))
    full = lambda a, b: pl.BlockSpec((a, b), lambda i: (0, 0))
    return _call_with_rider(
        body, "inproj_bwd", (S // TOK_T,), rider,
        in_specs=[tok(N_MAIN), tok(128), full(D_MODEL, N_MAIN), full(D_MODEL, 128), tok(D_MODEL), tok(D_MODEL),
                  full(1, D_MODEL)],
        out_specs=[tok(D_MODEL), full(1, D_MODEL)],
        out_shape=[jax.ShapeDtypeStruct((S, D_MODEL), F32), jax.ShapeDtypeStruct((1, D_MODEL), F32)],
        scratch_shapes=[], operands=(dzm, dzf, wm, wf, h0, dh1, g), semantics=("arbitrary",))


def _wgrad(a, b, name, a_lead=()):
    S, M = a.shape[len(a_lead):]
    N = b.shape[1]
    tn = min(N, 2048)
    ts = 1024
    last = S // ts - 1

    def body(a_ref, b_ref, o_ref, acc_ref):
        @pl.when(pl.program_id(1) == 0)
        def _():
            acc_ref[...] = jnp.zeros_like(acc_ref)

        acc_ref[...] += _dot_tn(a_ref[...].astype(BF16), b_ref[...])

        @pl.when(pl.program_id(1) == last)
        def _():
            o_ref[...] = acc_ref[...].astype(BF16)

    return pl.pallas_call(
        body, name=name, grid=(N // tn, S // ts),
        in_specs=[_slab_spec(a_lead, (ts, M), lambda n, s: (s, 0)), pl.BlockSpec((ts, tn), lambda n, s: (s, n))],
        out_specs=pl.BlockSpec((M, tn), lambda n, s: (0, n)),
        out_shape=jax.ShapeDtypeStruct((M, N), BF16),
        scratch_shapes=[pltpu.VMEM((M, tn), F32)],
        compiler_params=_cp("parallel", "arbitrary"),
    )(a, b)


def _wgrad_in(u, dzm, dzf):
    S = u.shape[0]
    ts = 1024
    last = S // ts - 1

    def body(u_ref, dzm_ref, dzf_ref, om_ref, of_ref, accm_ref, accf_ref):
        @pl.when(pl.program_id(0) == 0)
        def _():
            accm_ref[...] = jnp.zeros_like(accm_ref)
            accf_ref[...] = jnp.zeros_like(accf_ref)

        ub = u_ref[...]
        accm_ref[...] += _dot_tn(ub, dzm_ref[...])
        accf_ref[...] += _dot_tn(ub, dzf_ref[...])

        @pl.when(pl.program_id(0) == last)
        def _():
            om_ref[...] = accm_ref[...].astype(BF16)
            of_ref[...] = accf_ref[...].astype(BF16)

    tok = lambda w: pl.BlockSpec((ts, w), lambda s: (s, 0))
    full = lambda w: pl.BlockSpec((D_MODEL, w), lambda s: (0, 0))
    return pl.pallas_call(
        body, name="wgrad_in", grid=(S // ts,),
        in_specs=[tok(D_MODEL), tok(N_MAIN), tok(128)], out_specs=[full(N_MAIN), full(128)],
        out_shape=[jax.ShapeDtypeStruct((D_MODEL, N_MAIN), BF16), jax.ShapeDtypeStruct((D_MODEL, 128), BF16)],
        scratch_shapes=[pltpu.VMEM((D_MODEL, N_MAIN), F32), pltpu.VMEM((D_MODEL, 128), F32)],
        compiler_params=_cp("arbitrary"),
    )(u, dzm, dzf)


def _rope_tables(positions):
    inv_freq = ROPE_THETA ** (-jnp.arange(ROPE_HALF, dtype=F32) / ROPE_HALF)
    ang = positions.astype(F32)[:, None] * inv_freq
    cos, sin = jnp.cos(ang), jnp.sin(ang)
    S = positions.shape[0]
    one, zero = jnp.ones((S, HEAD_DIM - 2 * ROPE_HALF), F32), jnp.zeros((S, HEAD_DIM - 2 * ROPE_HALF), F32)
    z8 = jnp.zeros((S, ROPE_HALF), F32)
    rc = jnp.concatenate([cos, cos, one], axis=1)
    ra = jnp.concatenate([-sin, z8, zero], axis=1)
    rb = jnp.concatenate([z8, sin, zero], axis=1)
    return tuple(jnp.tile(t, (1, 2)) for t in (rc, ra, rb))


def _in_operands(w_in):
    w_in = w_in.astype(BF16)
    wm = jnp.concatenate([w_in[:, :4 * D_BRANCH], w_in[:, 4 * D_BRANCH + N_HEADS:]], axis=1)
    wf = jnp.pad(w_in[:, 4 * D_BRANCH:4 * D_BRANCH + N_HEADS], ((0, 0), (0, 128 - N_HEADS)))
    return dict(wm=wm, wf=wf)


def _layer_weights(w_in, w_out, w_ple, w_pg):
    return dict(_in_operands(w_in), w_out=w_out.astype(BF16), w_ple=w_ple.astype(BF16), w_pg=w_pg.astype(BF16))


def _row(v, width=128):
    v = v.reshape(1, -1).astype(F32)
    return jnp.pad(v, ((0, 0), (0, width - v.shape[1])))


def _layer_fwd(h0, p, rope, tabs, w, norm_g, b_f, qk_g, ple_g, rider=None, late=None, target=None):
    g1 = norm_g.reshape(1, D_MODEL)
    g2 = ple_g.reshape(1, D_MODEL)
    qkg = jnp.tile(qk_g, (1, 2))
    bf = _row(b_f)
    zm, zf, u = _inproj_fwd(h0, g1, w["wm"], w["wf"])
    qa, ka, va, qb, kb, vb = _prep_fwd(zm, zf, bf, tabs["tril"], qkg, *rope)
    oa, lse_a, *arrivals = _attn_fwd(qa, ka, va, tabs["fox"], True, "fox_fwd", rider)
    if late is not None:
        w = {**w, **late(arrivals)}
    ob, lse_b = _attn_fwd(qb, kb, vb, tabs["dil"], False, "dil_fwd")
    y, h1, h2, u2, e, gate, *loss = _mid_fwd(oa, ob, zm, h0, p, w["w_out"], w["w_pg"], w["w_ple"], g2, target)
    saved = dict(h0=h0, p=p, zm=zm, zf=zf, u=u, qa=qa, ka=ka, va=va, qb=qb, kb=kb, vb=vb, oa=oa, ob=ob,
                 lse_a=lse_a, lse_b=lse_b, y=y, h1=h1, u2=u2, e=e, gate=gate, g1=g1, g2=g2, qkg=qkg, bf=bf, w=w)
    return (h2, saved, arrivals, *loss)


def _layer_bwd(dh2, sv, rope, tabs, make_rider=None, make_last_rider=None):
    S = dh2.shape[0]
    nq = S // ATT_T
    w = sv["w"]
    rows = lambda a: a.reshape(N_HEADS, nq, 1, ATT_T)
    (dh1, dh1b, de, dpre, doa, dob, dga, dgb, dd, dg2) = _mid_bwd(
        dh2, sv["h1"], sv["e"], sv["gate"], sv["g2"], w["w_pg"], w["w_out"], sv["oa"], sv["ob"], sv["zm"])
    dda, ddb = dd[:N_HEADS], dd[N_HEADS:]
    early = dict(w_out=_wgrad(sv["y"], dh1b, "wgrad_out"), w_ple=_wgrad(sv["p"][0], de, "wgrad_ple", sv["p"][1]),
                 w_ple_gate=_wgrad(sv["u2"], dpre, "wgrad_gate"))
    rider = None if make_rider is None else make_rider(early)
    dqa, dka, dva, *arrivals = _attn_bwd(sv["qa"], sv["ka"], sv["va"], doa, sv["lse_a"], rows(dda), tabs["fox"],
                                         True, "fox_bwd", rider, narrow=(False, False, True))
    dqb, dkb, dvb = _attn_bwd(sv["qb"], sv["kb"], sv["vb"], dob, sv["lse_b"], rows(ddb), tabs["dil"], False,
                              "dil_bwd", narrow=(False, True, True))
    dzm, dzf, dqkg, dbf = _prep_bwd(dqa, dka, dva, dqb, dkb, dvb, sv["zm"], sv["qkg"], *rope, dga, dgb,
                                    sv["zf"], sv["bf"], tabs["triu"])
    dwm, dwf = _wgrad_in(sv["u"], dzm, dzf)
    dw_in = jnp.concatenate([dwm[:, :4 * D_BRANCH], dwf[:, :N_HEADS], dwm[:, 4 * D_BRANCH:]], axis=1)
    last_rider = None if make_last_rider is None else make_last_rider(dw_in)
    dh0, dg1, *last_arrivals = _inproj_bwd(dzm, dzf, w["wm"], w["wf"], sv["h0"], dh1, sv["g1"], last_rider)
    grads = dict(norm_g=dg1.reshape(D_MODEL), w_in=dw_in, b_f=dbf[0, :N_HEADS],
                 qk_norm_g=dqkg[:, :HEAD_DIM] + dqkg[:, HEAD_DIM:], ple_norm_g=dg2.reshape(D_MODEL), **early)
    return dh0, grads, arrivals + last_arrivals


def _tables():
    T = TOK_T
    r = lax.broadcasted_iota(jnp.int32, (T, T), 0)
    c = lax.broadcasted_iota(jnp.int32, (T, T), 1)
    return dict(fox=_bias_tables(True), dil=_bias_tables(False),
                tril=(c <= r).astype(BF16), triu=(c >= r).astype(BF16))


def _local_step(x, p, positions, target, layers, small):
    rope = _rope_tables(positions)
    tabs = _tables()
    ws = [_layer_weights(*lw) for lw in layers]
    h = x
    saved = []
    for li, (w, lp, sm) in enumerate(zip(ws, p, small)):
        h, sv, _, *loss = _layer_fwd(h, (lp, ()), rope, tabs, w, *sm, target=target if li == len(ws) - 1 else None)
        saved.append(sv)
    dh, (loss,) = h, loss
    grads = [None] * len(ws)
    for li in reversed(range(len(ws))):
        dh, grads[li], _ = _layer_bwd(dh, saved[li], rope, tabs)
    return loss[0, 0], dh, grads


def _peers():
    x, y, c = lax.axis_index("x"), lax.axis_index("y"), lax.axis_index("c")
    me = 4 * x + 2 * y + c
    flip = lambda v, bit: 1 - v if bit else v
    return me, [(flip(x, k & 4), flip(y, k & 2), flip(c, k & 1)) for k in range(1, N_DEV)]


def _sel(ref, kind, d):
    if kind == "whole":
        return ref
    if kind == "slot":
        return ref.at[d]
    block = pl.ds(pl.multiple_of(d * 128, 128), 128)
    return ref.at[block, :] if kind == "rows" else ref.at[:, block]


class _Pushes:
    def __init__(self, arrays, src_kinds, dst_kinds, out_shapes):
        self.arrays, self.n = list(arrays), len(arrays)
        self.src_kinds, self.dst_kinds = src_kinds, dst_kinds
        self.out_shapes = [jax.ShapeDtypeStruct(s, a.dtype) for s, a in zip(out_shapes, arrays)]
        hbm = pl.BlockSpec(memory_space=pltpu.HBM)
        self.in_specs, self.out_specs = [hbm] * self.n, [hbm] * self.n
        self.scratch_shapes = [pltpu.SemaphoreType.DMA((N_DEV - 1, self.n)),
                               pltpu.SemaphoreType.DMA((N_DEV - 1, self.n)), pltpu.SemaphoreType.DMA((self.n,))]

    def _copies(self, ins, outs, sems):
        send_sems, recv_sems, local_sems = sems
        me, peers = _peers()
        src = lambda a, d: _sel(ins[a], self.src_kinds[a], d)
        dst = lambda a: _sel(outs[a], self.dst_kinds[a], me)
        local = [pltpu.make_async_copy(src(a, me), dst(a), local_sems.at[a]) for a in range(self.n)]
        remote = [pltpu.make_async_remote_copy(
            src_ref=src(a, 4 * px + 2 * py + pc), dst_ref=dst(a), send_sem=send_sems.at[k, a],
            recv_sem=recv_sems.at[k, a], device_id=(px, py, pc), device_id_type=pl.DeviceIdType.MESH)
            for k, (px, py, pc) in enumerate(peers) for a in range(self.n)]
        return local + remote

    def start(self, ins, outs, sems):
        for cp in self._copies(ins, outs, sems):
            cp.start()

    def wait(self, ins, outs, sems):
        for cp in self._copies(ins, outs, sems):
            cp.wait()


def _exchange(name, pushes):
    n = pushes.n

    def body(*refs):
        pushes.start(refs[:n], refs[n:2 * n], refs[2 * n:])
        pushes.wait(refs[:n], refs[n:2 * n], refs[2 * n:])

    return pl.pallas_call(body, name=name, in_specs=pushes.in_specs, out_specs=pushes.out_specs,
                          out_shape=pushes.out_shapes, scratch_shapes=pushes.scratch_shapes)(*pushes.arrays)


def _gather_two_level(shard, name):
    def body(x_ref, out_ref, send_sems, recv_sems, local_sem):
        x, y, c = lax.axis_index("x"), lax.axis_index("y"), lax.axis_index("c")
        me, sibling = (x, y, c), (x, y, 1 - c)
        chips = [(1 - x, y), (x, 1 - y), (1 - x, 1 - y)]
        slot = lambda px, py, pc: out_ref.at[4 * px + 2 * py + pc]

        def copy(k, block, to, src=None):
            return pltpu.make_async_remote_copy(
                src_ref=slot(*block) if src is None else src, dst_ref=slot(*block), send_sem=send_sems.at[k],
                recv_sem=recv_sems.at[k], device_id=to, device_id_type=pl.DeviceIdType.MESH)

        mine = pltpu.make_async_copy(x_ref, slot(*me), local_sem)
        mine.start()
        first = [copy(0, me, sibling, src=x_ref)] + [copy(1 + j, me, (*chip, c), src=x_ref)
                                                     for j, chip in enumerate(chips)]
        for cp in first:
            cp.start()
        passed = [copy(4 + j, (*chip, c), sibling) for j, chip in enumerate(chips)]
        for j, chip in enumerate(chips):
            copy(1 + j, (*chip, c), me).wait_recv()
            passed[j].start()
        copy(0, sibling, me).wait_recv()
        for j, chip in enumerate(chips):
            copy(4 + j, (*chip, 1 - c), me).wait_recv()
        for cp in first + passed:
            cp.wait_send()
        mine.wait()

    hbm = pl.BlockSpec(memory_space=pltpu.HBM)
    return pl.pallas_call(
        body, name=name, in_specs=[hbm], out_specs=hbm,
        out_shape=jax.ShapeDtypeStruct((N_DEV,) + shard.shape, shard.dtype),
        scratch_shapes=[pltpu.SemaphoreType.DMA((N_DEV - 1,)), pltpu.SemaphoreType.DMA((N_DEV - 1,)),
                        pltpu.SemaphoreType.DMA],
    )(shard)


def _gather_pushes(shards, kinds):
    full = {"slot": lambda s: (N_DEV,) + s, "rows": lambda s: (N_DEV * s[0], s[1]),
            "cols": lambda s: (s[0], N_DEV * s[1])}
    return _Pushes(shards, ["whole"] * len(shards), kinds, [full[k](a.shape) for a, k in zip(shards, kinds)])


def _scatter_pushes(partials, kinds):
    part = {"slot": lambda s: s[1:], "rows": lambda s: (128, s[1]), "cols": lambda s: (s[0], 128),
            "whole": lambda s: s}
    return _Pushes(partials, kinds, ["slot"] * len(partials),
                   [(N_DEV,) + part[k](a.shape) for a, k in zip(partials, kinds)])


def _adamw(name, parts, w, m, v, rows):
    L, R, C = w.shape

    def body(p_ref, w_ref, m_ref, v_ref, g_ref, d_ref, nm_ref, nv_ref):
        g = p_ref[0, 0].astype(F32)
        for s in range(1, N_DEV):
            g = g + p_ref[s, 0].astype(F32)
        g_ref[0] = g
        nm = ADAM_B1 * m_ref[0] + (1.0 - ADAM_B1) * g
        nv = ADAM_B2 * v_ref[0] + (1.0 - ADAM_B2) * (g * g)
        nm_ref[0] = nm
        nv_ref[0] = nv
        m_hat = nm / (1.0 - ADAM_B1 ** ADAM_STEP)
        v_hat = nv / (1.0 - ADAM_B2 ** ADAM_STEP)
        d_ref[0] = -ADAM_LR * (m_hat / (jnp.sqrt(v_hat) + ADAM_EPS) + ADAM_WD * w_ref[0])

    blk = pl.BlockSpec((1, rows, C), lambda l, i: (l, i, 0))
    shp = jax.ShapeDtypeStruct((L, R, C), F32)
    return pl.pallas_call(
        body, name=name, grid=(L, R // rows),
        in_specs=[pl.BlockSpec((N_DEV, 1, rows, C), lambda l, i: (0, l, i, 0)), blk, blk, blk],
        out_specs=[blk] * 4, out_shape=[shp] * 4,
        compiler_params=_cp("parallel", "parallel"),
    )(parts, w, m, v)


SMALL_ROWS = 40
LOSS_ROW = 37


def _pack_small(norm_g, ple_g, qk_g, b_f, last_row):
    rows = lambda a: a.astype(F32).reshape(-1, 128)
    flat = jnp.concatenate([rows(norm_g), rows(ple_g), rows(qk_g), _row(b_f.reshape(-1)), last_row], axis=0)
    return jnp.pad(flat, ((0, SMALL_ROWS - flat.shape[0]), (0, 0)))


def _unpack_small(flat):
    return (flat[0:16].reshape(2, D_MODEL), flat[16:32].reshape(2, D_MODEL), flat[32:36].reshape(2, 4, HEAD_DIM),
            flat[36, :2 * N_HEADS].reshape(2, N_HEADS))


def kernel(x, p, positions, norm_g, w_in, b_f, qk_norm_g, w_out, w_ple, ple_norm_g, w_ple_gate, loss_target, m_norm_g, m_w_in, m_b_f, m_qk_norm_g, m_w_out, m_w_ple, m_ple_norm_g, m_w_ple_gate, v_norm_g, v_w_in, v_b_f, v_qk_norm_g, v_w_out, v_w_ple, v_ple_norm_g, v_w_ple_gate):
    bf16 = lambda a: a.astype(BF16)
    rows_in = W_IN_ROWS // 2
    flat_in = lambda a: bf16(a).reshape(rows_in, 128)
    full_in = lambda g: g.reshape(N_DEV, D_MODEL, W_IN_SHARD).transpose(1, 0, 2).reshape(D_MODEL, N_IN)
    small = [(norm_g[l], b_f[l], qk_norm_g[l], ple_norm_g[l]) for l in range(2)]
    rope = _rope_tables(positions[0])
    tabs = _tables()

    g_in0 = _gather_two_level(flat_in(w_in[0]), "gather_first")
    rest = _gather_pushes([flat_in(w_in[1])] + [bf16(a[l]) for l in range(2) for a in (w_out, w_ple, w_ple_gate)],
                          ["slot"] + ["rows", "cols", "rows"] * 2)
    late = lambda got: dict(w_out=got[1], w_ple=got[2], w_pg=got[3])
    h, sv0, got = _layer_fwd(x[0], (p, (0, 0)), rope, tabs, _in_operands(full_in(g_in0)), *small[0], rest, late)
    w1 = dict(_in_operands(full_in(got[0])), w_out=got[4], w_ple=got[5], w_pg=got[6])
    dh, sv1, _, loss = _layer_fwd(h, (p, (1, 0)), rope, tabs, w1, *small[1], target=loss_target[0])
    dh, gr1, _ = _layer_bwd(dh, sv1, rope, tabs)

    by_dest = lambda d: d.reshape(D_MODEL, N_DEV, W_IN_SHARD).transpose(1, 0, 2).reshape(N_DEV, rows_in, 128)
    big = ("w_out", "w_ple", "w_ple_gate")
    riding = lambda early: _scatter_pushes([by_dest(gr1["w_in"])] + [gr1[n] for n in big] + [early[n] for n in big],
                                           ["slot"] + ["rows", "cols", "rows"] * 2)
    riding_last = lambda dw_in: _scatter_pushes([by_dest(dw_in)], ["slot"])
    dx, gr0, (r_in1, *r_big, r_in0) = _layer_bwd(dh, sv0, rope, tabs, riding, riding_last)
    grads = (gr0, gr1)
    stack = lambda name: jnp.stack([gl[name] for gl in grads], axis=0)
    small_part = _pack_small(stack("norm_g"), stack("ple_norm_g"), stack("qk_norm_g"), stack("b_f"),
                             _row(loss[0, 0].reshape(1)))
    (r_small,) = _exchange("exchange_small", _scatter_pushes([small_part], ["whole"]))
    r_in = jnp.concatenate([r_in0, r_in1], axis=1)
    r_out, r_ple, r_pg = (jnp.stack([r_big[3 + k], r_big[k]], axis=1) for k in range(3))

    zero_row = jnp.zeros((1, 128), F32)
    small_of = lambda ng, pg, qk, bf: _pack_small(ng, pg, qk, bf, zero_row)[None]
    flat = lambda a: a.reshape(1, W_IN_ROWS, 128)
    outs = dict(
        w_in=[o.reshape(w_in.shape) for o in
              _adamw("adamw_in", r_in[:, None], flat(w_in), flat(m_w_in), flat(v_w_in), W_IN_TILE)],
        w_out=_adamw("adamw_out", r_out, w_out, m_w_out, v_w_out, 128),
        w_ple=_adamw("adamw_ple", r_ple, w_ple, m_w_ple, v_w_ple, 256),
        w_pg=_adamw("adamw_gate", r_pg, w_ple_gate, m_w_ple_gate, v_w_ple_gate, 128),
        small=_adamw("adamw_small", r_small[:, None], small_of(norm_g, ple_norm_g, qk_norm_g, b_f),
                     small_of(m_norm_g, m_ple_norm_g, m_qk_norm_g, m_b_f),
                     small_of(v_norm_g, v_ple_norm_g, v_qk_norm_g, v_b_f), SMALL_ROWS))
    leaves = []
    for kind in range(4):
        ng, pg, qk, bf = _unpack_small(outs["small"][kind][0])
        leaves += [ng, outs["w_in"][kind], bf, qk, outs["w_out"][kind], outs["w_ple"][kind], pg, outs["w_pg"][kind]]
    return (outs["small"][0][0, LOSS_ROW, 0], dx[None], *leaves)
```
